```python
import math
import jax
import jax.numpy as jnp
from jax import lax
import numpy as np

D_MODEL = 1024
BATCH = 16
SEQ = 256
DEPTH = 2
DEC_BATCH = 2
DEC_SEQ = 2048
PAST_LEN = 512

GRID_W = 64
N_MIXERS = 4
GROUP_W = D_MODEL // N_MIXERS
N_DIR = 2
S5_P = 16
S5_G = GROUP_W // S5_P
S5_N = 64
HG_HEADS = 4
HG_DK = GROUP_W // HG_HEADS
HG_DV = GROUP_W // HG_HEADS
HG_CHUNK = 32
FN_HEADS = 4
FN_DH = GROUP_W // FN_HEADS
GM_HEADS = 4
GM_DH = GROUP_W // GM_HEADS
GM_CHUNK = 128
D_FF = ((8 * D_MODEL // 3 + 127) // 128) * 128
CONV_W = 3
N_MOD = 6
N_IN_SLICES = 9
D_IN = N_IN_SLICES * GROUP_W
EPS = 1e-6
LAM_RE_MAX = -1e-4

kernel_name = 'hybrid_flow_prefix_s5_hgrn2_fnet_gmlp'


def rmsnorm(x, g):
    xf = x.astype(jnp.float32)
    y = xf * lax.rsqrt(jnp.mean(xf * xf, axis=-1, keepdims=True) + EPS)
    return (y * g.astype(jnp.float32)).astype(x.dtype)


def group_rmsnorm(x, g, n_groups):
    shp = x.shape
    xg = x.reshape(shp[:-1] + (n_groups, shp[-1] // n_groups))
    return rmsnorm(xg, g.reshape(n_groups, -1)).reshape(shp)


def s5_direction(u, lam_re, lam_im, log_dt, b_re, b_im, c_re, c_im, h0_re, h0_im, reverse):
    lr = jnp.minimum(lam_re.astype(jnp.float32), LAM_RE_MAX)
    li = lam_im.astype(jnp.float32)
    dt = jnp.exp(log_dt.astype(jnp.float32))[:, None]
    mag = jnp.exp(lr * dt)
    ang = li * dt
    ab_re = mag * jnp.cos(ang)
    ab_im = mag * jnp.sin(ang)
    den = lr * lr + li * li
    xr = ab_re - 1.0
    z_re = (xr * lr + ab_im * li) / den
    z_im = (ab_im * lr - xr * li) / den
    b_re = b_re.astype(jnp.float32)
    b_im = b_im.astype(jnp.float32)
    bb_re = z_re[..., None] * b_re - z_im[..., None] * b_im
    bb_im = z_re[..., None] * b_im + z_im[..., None] * b_re
    bu_re = jnp.einsum('btgp,gnp->btgn', u, bb_re)
    bu_im = jnp.einsum('btgp,gnp->btgn', u, bb_im)
    a_re = jnp.broadcast_to(ab_re, bu_re.shape)
    a_im = jnp.broadcast_to(ab_im, bu_im.shape)

    def combine(e1, e2):
        a1r, a1i, b1r, b1i = e1
        a2r, a2i, b2r, b2i = e2
        return (a2r * a1r - a2i * a1i, a2r * a1i + a2i * a1r,
                a2r * b1r - a2i * b1i + b2r, a2r * b1i + a2i * b1r + b2i)

    ac_re, ac_im, hb_re, hb_im = lax.associative_scan(combine, (a_re, a_im, bu_re, bu_im), reverse=reverse, axis=1)
    h_re = ac_re * h0_re[:, None] - ac_im * h0_im[:, None] + hb_re
    h_im = ac_re * h0_im[:, None] + ac_im * h0_re[:, None] + hb_im
    y = (jnp.einsum('gpn,btgn->btgp', c_re.astype(jnp.float32), h_re)
         - jnp.einsum('gpn,btgn->btgp', c_im.astype(jnp.float32), h_im))
    end = 0 if reverse else -1
    return y, h_re[:, end], h_im[:, end]


def hgrn_direction(q, logf, k, v, s0):
    bsz, t, nh = q.shape[0], q.shape[1], q.shape[2]
    nc = t // HG_CHUNK

    def chunked(a):
        return a.reshape((bsz, nc, HG_CHUNK) + a.shape[2:]).transpose(1, 0, 3, 2, 4)

    qc, fc, kc, vc = chunked(q), chunked(logf), chunked(k), chunked(v)
    bcum = jnp.cumsum(fc, axis=3)
    lower = jnp.tril(jnp.ones((HG_CHUNK, HG_CHUNK), dtype=bool))
    diff = bcum[..., :, None, :] - bcum[..., None, :, :]
    decay = jnp.exp(jnp.where(lower[:, :, None], diff, -jnp.inf))
    scores = jnp.einsum('cbhjk,cbhjik,cbhik->cbhji', qc, decay, kc)
    o_intra = jnp.einsum('cbhji,cbhiv->cbhjv', scores, vc)
    b_last = bcum[..., -1:, :]
    ds = jnp.einsum('cbhik,cbhiv->cbhkv', kc * jnp.exp(b_last - bcum), vc)
    g_last = jnp.exp(b_last[..., 0, :])

    def step(s, inp):
        g, d = inp
        return g[..., None] * s + d, s

    s_fin, s_start = lax.scan(step, s0, (g_last, ds))
    o_inter = jnp.einsum('cbhjk,cbhkv->cbhjv', qc * jnp.exp(bcum), s_start)
    o = (o_intra + o_inter).transpose(1, 0, 3, 2, 4).reshape(bsz, t, nh, -1)
    return o, s_fin


def mixer(h, p, lb, s5_h0_re, s5_h0_im, hg_s0):
    dt = h.dtype
    f32 = jnp.float32
    bsz, t = h.shape[0], h.shape[1]
    proj = h @ p['w_in']
    xa, hq, hf_fwd, hf_bwd, hi, hgate, xc, gu, gv = jnp.split(proj, N_IN_SLICES, axis=-1)
    gn = p['grp_norm_g']
    xa32 = xa.astype(f32)
    u = xa32.reshape(bsz, t, S5_G, S5_P)
    ys, fre, fim = [], [], []
    for d in range(N_DIR):
        y_d, r_d, i_d = s5_direction(u, p['s5_lam_re'][d], p['s5_lam_im'][d], p['s5_log_dt'][d],
                                     p['s5_b_re'][d], p['s5_b_im'][d], p['s5_c_re'][d], p['s5_c_im'][d],
                                     s5_h0_re[:, d].astype(f32), s5_h0_im[:, d].astype(f32), d == 1)
        ys.append(y_d)
        fre.append(r_d)
        fim.append(i_d)
    y5 = (ys[0] + ys[1]).reshape(bsz, t, GROUP_W) + p['s5_d'].astype(f32) * xa32
    y5 = jax.nn.gelu(y5).astype(dt)
    out_a = rmsnorm(y5 * jax.nn.sigmoid(y5 @ p['s5_w_glu']), gn[:GROUP_W])
    q = hq.astype(f32).reshape(bsz, t, HG_HEADS, HG_DK)
    v = hi.astype(f32).reshape(bsz, t, HG_HEADS, HG_DV)
    outs, sfin = [], []
    for d, zf in enumerate((hf_fwd, hf_bwd)):
        z = zf.astype(f32).reshape(bsz, t, HG_HEADS, HG_DK)
        lbd = lb[d].astype(f32).reshape(HG_HEADS, HG_DK)
        logf = jnp.logaddexp(jnp.log(lbd), jnp.log1p(-lbd) + jax.nn.log_sigmoid(z))
        k = (1.0 - lbd) * jax.nn.sigmoid(-z)
        s0 = hg_s0[:, d].astype(f32)
        if d == 0:
            o_d, s_d = hgrn_direction(q, logf, k, v, s0)
        else:
            o_d, s_d = hgrn_direction(q[:, ::-1], logf[:, ::-1], k[:, ::-1], v[:, ::-1], s0)
            o_d = o_d[:, ::-1]
        outs.append(o_d)
        sfin.append(s_d)
    o = (outs[0] + outs[1]).reshape(bsz, t, GROUP_W).astype(dt)
    out_b = group_rmsnorm(o, gn[GROUP_W:2 * GROUP_W], HG_HEADS) * jax.nn.silu(hgate)
    xcf = xc.astype(f32).reshape(bsz, t, FN_HEADS, FN_DH)
    xfr = jnp.real(jnp.fft.fft2(xcf, axes=(1, 3), norm='ortho')).reshape(bsz, t, GROUP_W).astype(dt)
    out_c = rmsnorm(xfr @ p['fn_w'], gn[2 * GROUP_W:3 * GROUP_W])
    gu = jax.nn.gelu(gu)
    gv = rmsnorm(jax.nn.gelu(gv), p['gm_norm_g'])
    nc = t // GM_CHUNK
    gvc = gv.reshape(bsz, nc, GM_CHUNK, GM_HEADS, GM_DH)
    sp = jnp.einsum('hij,bcjhd->bcihd', p['gm_ws'], gvc) + p['gm_bs'].T[None, None, :, :, None]
    out_d = rmsnorm(gu * sp.reshape(bsz, t, GROUP_W), gn[3 * GROUP_W:])
    out = jnp.concatenate([out_a, out_b, out_c, out_d], axis=-1) @ p['w_out']
    return out, (jnp.stack(fre, axis=1), jnp.stack(fim, axis=1), jnp.stack(sfin, axis=1))


def conv_ffn(h, p, grid):
    z = h @ p['ffn_w_up']
    bsz, t, ch = z.shape
    if grid:
        zg = z.reshape(bsz, t // GRID_W, GRID_W, ch)
        axis = 2
    else:
        zg = z
        axis = 1
    pad = [(0, 0)] * zg.ndim
    pad[axis] = (CONV_W // 2, CONV_W // 2)
    zp = jnp.pad(zg, pad)
    n = zg.shape[axis]
    w = p['ffn_conv_w']
    zc = p['ffn_conv_b'] + lax.slice_in_dim(zp, 0, n, axis=axis) * w[0]
    for j in range(1, CONV_W):
        zc = zc + lax.slice_in_dim(zp, j, j + n, axis=axis) * w[j]
    a, b = jnp.split(zc.reshape(bsz, t, ch), 2, axis=-1)
    return (jax.nn.gelu(a) * b) @ p['ffn_w_down']


def trunk_layer(x, mod, p, lb, s5_h0_re, s5_h0_im, hg_s0, grid):
    sh1, sc1, g1, sh2, sc2, g2 = jnp.split(mod, N_MOD, axis=-1)
    h = rmsnorm(x, p['norm1_g']) * (1.0 + sc1) + sh1
    m, fin = mixer(h, p, lb, s5_h0_re, s5_h0_im, hg_s0)
    x = x + g1 * m
    h = rmsnorm(x, p['norm2_g']) * (1.0 + sc2) + sh2
    x = x + g2 * conv_ffn(h, p, grid)
    return x, fin


def setup_inputs(seed: int = 0) -> dict:
    key = jax.random.key(seed)
    ks = list(jax.random.split(key, 40))
    nrm = lambda i, shape, s: jax.random.normal(ks[i], shape, jnp.float32) * s
    n_idx = jnp.arange(S5_N, dtype=jnp.float32)
    return {
        'x_prompt': nrm(0, (BATCH, SEQ, D_MODEL), 1.0),
        'x_sample': nrm(1, (DEC_BATCH, DEC_SEQ, D_MODEL), 1.0),
        'state_s5_re': nrm(2, (DEC_BATCH, DEPTH, N_DIR, S5_G, S5_N), 0.5),
        'state_s5_im': nrm(3, (DEC_BATCH, DEPTH, N_DIR, S5_G, S5_N), 0.5),
        'state_hgrn': nrm(4, (DEC_BATCH, DEPTH, N_DIR, HG_HEADS, HG_DK, HG_DV), 0.5),
        'c': nrm(5, (DEC_BATCH, D_MODEL), 1.0),
        'c_ctx': nrm(6, (D_MODEL,), 1.0),
        'w_ada': nrm(7, (DEPTH, D_MODEL, N_MOD * D_MODEL), 0.5 / math.sqrt(D_MODEL)),
        'b_ada': nrm(8, (DEPTH, N_MOD * D_MODEL), 0.02),
        'norm1_g': 1.0 + nrm(9, (DEPTH, D_MODEL), 0.02),
        'norm2_g': 1.0 + nrm(10, (DEPTH, D_MODEL), 0.02),
        'w_in': nrm(11, (DEPTH, D_MODEL, D_IN), 1.0 / math.sqrt(D_MODEL)),
        's5_lam_re': -0.5 + nrm(12, (DEPTH, N_DIR, S5_G, S5_N), 0.02),
        's5_lam_im': math.pi * n_idx + nrm(13, (DEPTH, N_DIR, S5_G, S5_N), 0.02),
        's5_log_dt': jax.random.uniform(ks[14], (DEPTH, N_DIR, S5_G), jnp.float32, math.log(1e-3), math.log(1e-1)),
        's5_b_re': nrm(15, (DEPTH, N_DIR, S5_G, S5_N, S5_P), 1.0 / math.sqrt(2 * S5_P)),
        's5_b_im': nrm(16, (DEPTH, N_DIR, S5_G, S5_N, S5_P), 1.0 / math.sqrt(2 * S5_P)),
        's5_c_re': nrm(17, (DEPTH, N_DIR, S5_G, S5_P, S5_N), 1.0 / math.sqrt(2 * S5_N)),
        's5_c_im': nrm(18, (DEPTH, N_DIR, S5_G, S5_P, S5_N), 1.0 / math.sqrt(2 * S5_N)),
        's5_d': nrm(19, (DEPTH, GROUP_W), 1.0),
        's5_w_glu': nrm(20, (DEPTH, GROUP_W, GROUP_W), 1.0 / math.sqrt(GROUP_W)),
        'hg_lb_logits': nrm(21, (DEPTH, N_DIR, GROUP_W), 0.5),
        'fn_w': nrm(22, (DEPTH, GROUP_W, GROUP_W), 1.0 / math.sqrt(GROUP_W)),
        'gm_norm_g': 1.0 + nrm(23, (DEPTH, GROUP_W), 0.02),
        'gm_ws': nrm(24, (DEPTH, GM_HEADS, GM_CHUNK, GM_CHUNK), 1.0 / math.sqrt(GM_CHUNK)),
        'gm_bs': 1.0 + nrm(25, (DEPTH, GM_HEADS, GM_CHUNK), 0.02),
        'grp_norm_g': 1.0 + nrm(26, (DEPTH, D_MODEL), 0.02),
        'w_out': nrm(27, (DEPTH, D_MODEL, D_MODEL), 1.0 / math.sqrt(D_MODEL)),
        'ffn_w_up': nrm(28, (DEPTH, D_MODEL, 2 * D_FF), 1.0 / math.sqrt(D_MODEL)),
        'ffn_conv_w': nrm(29, (DEPTH, CONV_W, 2 * D_FF), 1.0 / math.sqrt(CONV_W)),
        'ffn_conv_b': nrm(30, (DEPTH, 2 * D_FF), 0.02),
        'ffn_w_down': nrm(31, (DEPTH, D_FF, D_MODEL), 1.0 / math.sqrt(D_FF)),
        'final_norm_g': 1.0 + nrm(32, (D_MODEL,), 0.02),
    }


def reference(x_prompt, x_sample, state_s5_re, state_s5_im, state_hgrn, c, c_ctx,
              w_ada, b_ada, norm1_g, norm2_g, w_in, s5_lam_re, s5_lam_im, s5_log_dt,
              s5_b_re, s5_b_im, s5_c_re, s5_c_im, s5_d, s5_w_glu, hg_lb_logits, fn_w,
              gm_norm_g, gm_ws, gm_bs, grp_norm_g, w_out, ffn_w_up, ffn_conv_w, ffn_conv_b,
              ffn_w_down, final_norm_g):
    lb_p = jax.nn.softmax(hg_lb_logits.astype(jnp.float32), axis=0)
    lbs = jnp.maximum(jnp.cumsum(lb_p, axis=0) - lb_p[0], 0.0)
    bp = x_prompt.shape[0]
    z_s5 = jnp.zeros((bp, N_DIR, S5_G, S5_N), jnp.float32)
    z_hg = jnp.zeros((bp, N_DIR, HG_HEADS, HG_DK, HG_DV), jnp.float32)
    xp, xs = x_prompt, x_sample
    new_re, new_im, new_hg = [], [], []
    for l in range(DEPTH):
        p = {'w_in': w_in[l], 'norm1_g': norm1_g[l], 'norm2_g': norm2_g[l],
             's5_lam_re': s5_lam_re[l], 's5_lam_im': s5_lam_im[l], 's5_log_dt': s5_log_dt[l],
             's5_b_re': s5_b_re[l], 's5_b_im': s5_b_im[l], 's5_c_re': s5_c_re[l], 's5_c_im': s5_c_im[l],
             's5_d': s5_d[l], 's5_w_glu': s5_w_glu[l], 'fn_w': fn_w[l], 'gm_norm_g': gm_norm_g[l],
             'gm_ws': gm_ws[l], 'gm_bs': gm_bs[l], 'grp_norm_g': grp_norm_g[l], 'w_out': w_out[l],
             'ffn_w_up': ffn_w_up[l], 'ffn_conv_w': ffn_conv_w[l], 'ffn_conv_b': ffn_conv_b[l],
             'ffn_w_down': ffn_w_down[l]}
        mod_ctx = (jax.nn.silu(c_ctx) @ w_ada[l] + b_ada[l])[None, None, :]
        xp, (fr, fi, fh) = trunk_layer(xp, mod_ctx, p, lbs[l], z_s5, z_s5, z_hg, False)
        new_re.append(fr)
        new_im.append(fi)
        new_hg.append(fh)
        mod_lat = (jax.nn.silu(c) @ w_ada[l] + b_ada[l])[:, None, :]
        xs, _ = trunk_layer(xs, mod_lat, p, lbs[l], state_s5_re[:, l], state_s5_im[:, l], state_hgrn[:, l], True)
    y_prompt = rmsnorm(xp, final_norm_g)
    y_sample = rmsnorm(xs, final_norm_g)
    new_state_s5_re = jnp.stack(new_re, axis=1)
    new_state_s5_im = jnp.stack(new_im, axis=1)
    new_state_hgrn = jnp.stack(new_hg, axis=1)
    return (y_prompt, y_sample, new_state_s5_re, new_state_s5_im, new_state_hgrn)
```

```python
import functools
import math

import numpy as np
import jax
import jax.numpy as jnp
from jax import lax
from jax.experimental import pallas as pl
from jax.experimental.pallas import tpu as pltpu

D_MODEL = 1024
BATCH = 16
SEQ = 256
DEPTH = 2
DEC_BATCH = 2
DEC_SEQ = 2048
GRID_W = 64
GROUP_W = 256
S5_P = 16
S5_G = 16
S5_N = 64
HG_HEADS = 4
HG_DK = 64
GM_HEADS = 4
GM_CHUNK = 128
D_FF = 2816
N_MOD = 6
D_IN = 9 * GROUP_W
EPS = 1e-6
LAM_RE_MAX = -1e-4

N_CTX = BATCH * SEQ
N_LAT = DEC_BATCH * DEC_SEQ
N_TOK = N_CTX + N_LAT
TB = 256
NB_CTX = N_CTX // TB
NB_LAT_SEQ = DEC_SEQ // TB
S5_L = 16
HG_L = 64
FF_TILE = 256
VMEM_LIMIT = 56 * 1024 * 1024

F32 = jnp.float32
BF16 = jnp.bfloat16

COL_XA, COL_HQ, COL_HF_FWD, COL_HF_BWD, COL_HI, COL_HGATE, COL_XC, COL_GU, COL_GV = range(9)


def _mod_row(i):
    return jnp.where(i < NB_CTX, 0, 1 + (i - NB_CTX) // NB_LAT_SEQ)


def _gelu(x):
    return 0.5 * x * (1.0 + jnp.tanh(0.7978845608028654 * (x + 0.044715 * (x * x * x))))


def _rms(x, g):
    return x * lax.rsqrt(jnp.mean(x * x, axis=-1, keepdims=True) + EPS) * g


def _dot(a, b):
    return jnp.dot(a, b, preferred_element_type=F32)


def _dot_nt(a, b):
    return lax.dot_general(a, b, (((1,), (1,)), ((), ())), preferred_element_type=F32)


def _dot_tn(a, b):
    return lax.dot_general(a, b, (((0,), (0,)), ((), ())), preferred_element_type=F32)


ADA_TN = 1536


def _ada_kernel(c_ref, w_ref, b_ref, o_ref):
    cv = c_ref[...]
    s = cv * jax.nn.sigmoid(cv)
    o_ref[0] = _dot(s.astype(BF16), w_ref[0].astype(BF16)) + b_ref[0]


def _ada(cvecs, w_ada, b_ada):
    n = N_MOD * D_MODEL
    return pl.pallas_call(
        _ada_kernel,
        grid=(DEPTH, n // ADA_TN),
        in_specs=[
            pl.BlockSpec((8, D_MODEL), lambda l, j: (0, 0)),
            pl.BlockSpec((1, D_MODEL, ADA_TN), lambda l, j: (l, 0, j)),
            pl.BlockSpec((1, 1, ADA_TN), lambda l, j: (l, 0, j)),
        ],
        out_specs=pl.BlockSpec((1, 8, ADA_TN), lambda l, j: (l, 0, j)),
        out_shape=jax.ShapeDtypeStruct((DEPTH, 8, n), F32),
        compiler_params=pltpu.CompilerParams(vmem_limit_bytes=VMEM_LIMIT),
        name="ada_mod",
    )(cvecs, w_ada, b_ada.reshape(DEPTH, 1, n))


def _inproj_kernel(x_ref, mod_ref, g_ref, w_ref, o_ref):
    mod = mod_ref[0]
    sh = mod[:, 0:D_MODEL]
    sc = mod[:, D_MODEL:2 * D_MODEL]
    h = _rms(x_ref[...], g_ref[...]) * (1.0 + sc) + sh
    o_ref[...] = _dot(h.astype(BF16), w_ref[...])


def _inproj(x, mod, norm_g, w_in_bf):
    return pl.pallas_call(
        _inproj_kernel,
        grid=(N_TOK // TB,),
        in_specs=[
            pl.BlockSpec((TB, D_MODEL), lambda i: (i, 0)),
            pl.BlockSpec((1, 1, N_MOD * D_MODEL), lambda i: (_mod_row(i), 0, 0)),
            pl.BlockSpec((1, D_MODEL), lambda i: (0, 0)),
            pl.BlockSpec((D_MODEL, D_IN), lambda i: (0, 0)),
        ],
        out_specs=pl.BlockSpec((TB, D_IN), lambda i: (i, 0)),
        out_shape=jax.ShapeDtypeStruct((N_TOK, D_IN), F32),
        compiler_params=pltpu.CompilerParams(vmem_limit_bytes=VMEM_LIMIT),
        name="in_proj",
    )(x, mod, norm_g.reshape(1, D_MODEL), w_in_bf)


S5_ROWS = N_TOK // S5_L
S5_SEQS = tuple((s * (SEQ // S5_L), SEQ // S5_L) for s in range(BATCH)) + tuple(
    (N_CTX // S5_L + s * (DEC_SEQ // S5_L), DEC_SEQ // S5_L) for s in range(DEC_BATCH))
S5_SW = S5_G * S5_N


def _s5_weights(lam_re, lam_im, log_dt, b_re, b_im, c_re, c_im):
    hp = lax.Precision.HIGHEST
    lr = jnp.minimum(lam_re.astype(F32), LAM_RE_MAX)
    li = lam_im.astype(F32)
    dt = jnp.exp(log_dt.astype(F32))[..., None]
    mag = jnp.exp(lr * dt)
    ang = li * dt
    ab_re = mag * jnp.cos(ang)
    ab_im = mag * jnp.sin(ang)
    den = lr * lr + li * li
    xr = ab_re - 1.0
    z_re = (xr * lr + ab_im * li) / den
    z_im = (ab_im * lr - xr * li) / den
    bb_re = z_re[..., None] * b_re - z_im[..., None] * b_im
    bb_im = z_re[..., None] * b_im + z_im[..., None] * b_re
    tau = jnp.arange(S5_L + 1, dtype=F32)[:, None, None, None]
    pm = jnp.exp(lr * dt * tau)
    pa = li * dt * tau
    pw_re = pm * jnp.cos(pa)
    pw_im = pm * jnp.sin(pa)
    ca_re = c_re[None] * pw_re[:, :, :, None, :] - c_im[None] * pw_im[:, :, :, None, :]
    ca_im = c_re[None] * pw_im[:, :, :, None, :] + c_im[None] * pw_re[:, :, :, None, :]
    kern = (jnp.einsum('tdgpn,dgnq->tdgqp', ca_re[:S5_L], bb_re, precision=hp)
            - jnp.einsum('tdgpn,dgnq->tdgqp', ca_im[:S5_L], bb_im, precision=hp))
    sig = np.arange(S5_L)[:, None]
    ta = np.arange(S5_L)[None, :]
    lag_f = ta - sig
    lag_b = sig - ta
    mf = jnp.where((lag_f >= 0)[:, :, None, None, None], kern[:, 0][np.clip(lag_f, 0, S5_L - 1)], 0.0)
    mb = jnp.where((lag_b >= 0)[:, :, None, None, None], kern[:, 1][np.clip(lag_b, 0, S5_L - 1)], 0.0)
    m = (mf + mb).transpose(2, 0, 3, 1, 4).reshape(S5_G, S5_L * S5_P, S5_L * S5_P)

    def cmul(ar, ai, br, bi):
        return ar * br - ai * bi, ar * bi + ai * br

    pf_re, pf_im = pw_re[:S5_L, 0][::-1], pw_im[:S5_L, 0][::-1]
    pb_re, pb_im = pw_re[:S5_L, 1], pw_im[:S5_L, 1]
    ff_re, ff_im = cmul(pf_re[..., None], pf_im[..., None], bb_re[0][None], bb_im[0][None])
    fb_re, fb_im = cmul(pb_re[..., None], pb_im[..., None], bb_re[1][None], bb_im[1][None])
    f = jnp.stack([jnp.stack([ff_re, ff_im]), jnp.stack([fb_re, fb_im])])
    f = f.transpose(0, 3, 2, 5, 1, 4).reshape(2, S5_G // 2, 2, S5_L * S5_P, 2, S5_N)
    eye2 = jnp.eye(2, dtype=F32)
    f2 = jnp.einsum('djasrn,ab->djasrbn', f, eye2).reshape(2, S5_G // 2, 2 * S5_L * S5_P, 4 * S5_N)
    ef_re, ef_im = ca_re[1:, 0], -ca_im[1:, 0]
    eb_re, eb_im = ca_re[1:, 1][::-1], -ca_im[1:, 1][::-1]
    e = jnp.stack([jnp.stack([ef_re, ef_im]), jnp.stack([eb_re, eb_im])])
    e = e.transpose(0, 1, 3, 5, 2, 4).reshape(2, 2, S5_G // 2, 2, S5_N, S5_L * S5_P)
    e2 = jnp.einsum('drjant,ab->drjanbt', e, eye2).reshape(2, 2, S5_G // 2, 2 * S5_N, 2 * S5_L * S5_P)
    al = jnp.stack([pw_re[S5_L], pw_im[S5_L]], axis=1).reshape(2, 2, 1, S5_SW)
    return m.astype(BF16), f2.astype(BF16), e2.astype(BF16), al


def _s5_kernel(u_ref, m_ref, f_ref, e_ref, al_ref, h0_ref, y_ref, hfin_ref, z_ref, h_ref):
    npair = S5_G // 2
    pw = 2 * S5_L * S5_P
    sw = 2 * S5_N
    for j in range(npair):
        ub = u_ref[:, pw * j:pw * (j + 1)].astype(BF16)
        for d in range(2):
            o = _dot(ub, f_ref[d, j])
            z_ref[d, 0, :, sw * j:sw * (j + 1)] = o[:, :sw]
            z_ref[d, 1, :, sw * j:sw * (j + 1)] = o[:, sw:]

    def scan_group(seq_ids):
        nc = S5_SEQS[seq_ids[0]][1]
        chains = [(d, s) for d in range(2) for s in seq_ids]
        init = []
        for d, s in chains:
            init.append(h0_ref[d, 0, s:s + 1, :])
            init.append(h0_ref[d, 1, s:s + 1, :])

        def body(c, carry):
            out = []
            for k, (d, s) in enumerate(chains):
                hr, hi = carry[2 * k], carry[2 * k + 1]
                r = S5_SEQS[s][0] + (c if d == 0 else nc - 1 - c)
                h_ref[d, 0, pl.ds(r, 1), :] = hr
                h_ref[d, 1, pl.ds(r, 1), :] = hi
                ar = al_ref[d, 0]
                ai = al_ref[d, 1]
                out.append(ar * hr - ai * hi + z_ref[d, 0, pl.ds(r, 1), :])
                out.append(ar * hi + ai * hr + z_ref[d, 1, pl.ds(r, 1), :])
            return tuple(out)

        fin = lax.fori_loop(0, nc, body, tuple(init))
        for k, (d, s) in enumerate(chains):
            if s < BATCH:
                hfin_ref[d, 0, s:s + 1, :] = fin[2 * k]
                hfin_ref[d, 1, s:s + 1, :] = fin[2 * k + 1]

    for s0 in range(0, BATCH, 4):
        scan_group(tuple(range(s0, s0 + 4)))
    scan_group(tuple(range(BATCH, BATCH + DEC_BATCH)))

    gw = S5_L * S5_P
    for j in range(npair):
        acc = None
        for d in range(2):
            for ri in range(2):
                hb = h_ref[d, ri, :, sw * j:sw * (j + 1)].astype(BF16)
                t = _dot(hb, e_ref[d, ri, j])
                acc = t if acc is None else acc + t
        for gl in range(2):
            g = 2 * j + gl
            ug = u_ref[:, gw * g:gw * (g + 1)].astype(BF16)
            y_ref[:, gw * g:gw * (g + 1)] = acc[:, gw * gl:gw * (gl + 1)] + _dot(ug, m_ref[g])


def _s5(u, m, f2, e2, al, h0):
    return pl.pallas_call(
        _s5_kernel,
        out_shape=(jax.ShapeDtypeStruct((S5_ROWS, S5_G * S5_L * S5_P), F32),
                   jax.ShapeDtypeStruct((2, 2, BATCH, S5_SW), F32)),
        scratch_shapes=[pltpu.VMEM((2, 2, S5_ROWS, S5_SW), F32),
                        pltpu.VMEM((2, 2, S5_ROWS, S5_SW), F32)],
        compiler_params=pltpu.CompilerParams(vmem_limit_bytes=VMEM_LIMIT),
        name="s5_mixer",
    )(u, m, f2, e2, al, h0)


HG_NLEV = int(math.log2(HG_L))


def _hgrn_consts():
    L = HG_L
    w = np.zeros((HG_NLEV + 2, L, L), np.float32)
    mask = np.zeros((HG_NLEV + 1, L, L), np.float32)
    for lev in range(HG_NLEV):
        blk = L >> lev
        half = blk // 2
        for t in range(L):
            p, o = divmod(t, blk)
            bd = p * blk + half - 1
            if o >= half:
                w[lev, t, bd + 1:t + 1] = 1.0
            else:
                w[lev, t, t + 1:bd + 1] = 1.0
        jj, ii = np.meshgrid(np.arange(L), np.arange(L), indexing='ij')
        mask[lev] = ((jj // blk == ii // blk) & (jj % blk >= half) & (ii % blk < half)).astype(np.float32)
    mask[HG_NLEV] = np.eye(L, dtype=np.float32)
    for t in range(L):
        w[HG_NLEV, t, :t + 1] = 1.0
        w[HG_NLEV + 1, t, t + 1:] = 1.0
    out = []
    for wd, md in ((w, mask), (w[:, ::-1, ::-1], mask[:, ::-1, ::-1])):
        wflat = wd.reshape((HG_NLEV + 2) * L, L)
        out.append((np.concatenate([wflat] * 3, axis=1), np.tile(md, (1, 1, HG_HEADS))))
    wcat = np.stack([out[0][0], out[1][0]])
    mask4 = np.stack([out[0][1], out[1][1]])
    hm = np.kron(np.eye(HG_HEADS, dtype=np.float32), np.ones((HG_DK, HG_DK), np.float32))
    return wcat, mask4, hm


def _hgrn_dir(d, q, z, v, lbp, wcat_ref, mask_ref, hm, st):
    L = HG_L
    az = jnp.abs(z)
    sp = jnp.log1p(jnp.exp(-az))
    ls = jnp.minimum(z, 0.0) - sp
    kk = lbp[2:3] * jnp.exp(jnp.minimum(-z, 0.0) - sp)
    a = lbp[0:1]
    b = lbp[1:2] + ls
    logf = jnp.maximum(a, b) + jnp.log1p(jnp.exp(-jnp.abs(a - b)))
    p0 = logf.astype(BF16)
    r0 = logf - p0.astype(F32)
    p1 = r0.astype(BF16)
    p2 = (r0 - p1.astype(F32)).astype(BF16)
    ex = jnp.exp(_dot(wcat_ref[d], jnp.concatenate([p0, p1, p2], axis=0)))
    hmb = hm.astype(BF16)

    def bd4(x):
        return jnp.concatenate([x] * HG_HEADS, axis=0) * hmb

    sc = None
    for lev in range(HG_NLEV + 1):
        if lev < HG_NLEV:
            al = ex[L * lev:L * (lev + 1)]
            lhs = (q * al).astype(BF16)
            rhs = (kk * al).astype(BF16)
        else:
            lhs = q.astype(BF16)
            rhs = kk.astype(BF16)
        t = _dot_nt(lhs, bd4(rhs)) * mask_ref[d, lev]
        sc = t if sc is None else sc + t
    vb = v.astype(BF16)
    o = _dot(sc.astype(BF16), bd4(vb))
    eq = ex[L * HG_NLEV:L * (HG_NLEV + 1)]
    o = o + _dot_nt((q * eq).astype(BF16), st.astype(BF16))
    ke = (kk * ex[L * (HG_NLEV + 1):L * (HG_NLEV + 2)]).astype(BF16)
    g = eq[L - 1:L] if d == 0 else eq[0:1]
    st_new = st * g + _dot_tn(vb, ke) * hm
    return o, st_new


def _hgrn_kernel(*refs, has_init):
    if has_init:
        (qf, zf, vf, qb, zb, vb, lb_ref, w_ref, mask_ref, hm_ref, s0_ref,
         of_ref, ob_ref, sfin_ref, st_ref) = refs
    else:
        (qf, zf, vf, qb, zb, vb, lb_ref, w_ref, mask_ref, hm_ref,
         of_ref, ob_ref, sfin_ref, st_ref) = refs
    c = pl.program_id(1)

    @pl.when(c == 0)
    def _():
        if has_init:
            st_ref[...] = s0_ref[0]
        else:
            st_ref[...] = jnp.zeros_like(st_ref)

    hm = hm_ref[...]
    o, s = _hgrn_dir(0, qf[...], zf[...], vf[...], lb_ref[0], w_ref, mask_ref, hm, st_ref[0])
    of_ref[...] = o
    st_ref[0] = s
    o, s = _hgrn_dir(1, qb[...], zb[...], vb[...], lb_ref[1], w_ref, mask_ref, hm, st_ref[1])
    ob_ref[...] = o
    st_ref[1] = s

    @pl.when(c == pl.num_programs(1) - 1)
    def _():
        sfin_ref[0] = st_ref[...]


def _hgrn(proj, lbp, consts, s0, *, n_seq, seq_len, tok0):
    wcat, mask4, hm = consts
    nc = seq_len // HG_L
    b0 = tok0 // HG_L
    hw = HG_HEADS * HG_DK

    def fwd(col):
        return pl.BlockSpec((HG_L, hw), lambda s, c: (b0 + s * nc + c, col))

    def bwd(col):
        return pl.BlockSpec((HG_L, hw), lambda s, c: (b0 + s * nc + nc - 1 - c, col))

    def const(shape):
        return pl.BlockSpec(shape, lambda s, c: (0,) * len(shape))

    in_specs = [fwd(COL_HQ), fwd(COL_HF_FWD), fwd(COL_HI), bwd(COL_HQ), bwd(COL_HF_BWD), bwd(COL_HI),
                const(lbp.shape), const(wcat.shape), const(mask4.shape), const(hm.shape)]
    args = [proj] * 6 + [lbp, wcat, mask4, hm]
    if s0 is not None:
        in_specs.append(pl.BlockSpec((1, 2, hw, hw), lambda s, c: (s, 0, 0, 0)))
        args.append(s0)
    n_rows = n_seq * seq_len
    return pl.pallas_call(
        functools.partial(_hgrn_kernel, has_init=s0 is not None),
        grid=(n_seq, nc),
        in_specs=in_specs,
        out_specs=(pl.BlockSpec((HG_L, hw), lambda s, c: (s * nc + c, 0)),
                   pl.BlockSpec((HG_L, hw), lambda s, c: (s * nc + nc - 1 - c, 0)),
                   pl.BlockSpec((1, 2, hw, hw), lambda s, c: (s, 0, 0, 0))),
        out_shape=(jax.ShapeDtypeStruct((n_rows, hw), F32),
                   jax.ShapeDtypeStruct((n_rows, hw), F32),
                   jax.ShapeDtypeStruct((n_seq, 2, hw, hw), F32)),
        scratch_shapes=[pltpu.VMEM((2, hw, hw), F32)],
        compiler_params=pltpu.CompilerParams(vmem_limit_bytes=VMEM_LIMIT),
        name="hgrn_mixer",
    )(*args)


FN_TR = 256


def _dft_consts(t_len):
    n = GROUP_W // 4
    k = np.arange(n)
    ang = 2.0 * np.pi * ((k[:, None] * k[None, :]) % n) / n
    eye = np.eye(4)
    cs = np.concatenate([np.kron(eye, np.cos(ang)), np.kron(eye, np.sin(ang))], axis=1) / math.sqrt(n)
    t = np.arange(t_len)
    angt = 2.0 * np.pi * ((t[:, None] * t[None, :]) % t_len) / t_len
    dft = np.concatenate([np.cos(angt), -np.sin(angt)], axis=1) / math.sqrt(t_len)
    return cs.astype(np.float32), dft.astype(np.float32)


def _fnet_kernel(x_ref, cs_ref, dft_ref, w_ref, o_ref, r_ref, *, t_len):
    @pl.when(pl.program_id(1) == 0)
    def _():
        t = _dot(x_ref[...].astype(BF16), cs_ref[...])
        r_ref[0:t_len, :] = t[:, :GROUP_W].astype(BF16)
        r_ref[t_len:2 * t_len, :] = t[:, GROUP_W:].astype(BF16)

    y = _dot(dft_ref[...], r_ref[...])
    o_ref[...] = _dot(y.astype(BF16), w_ref[...])


def _fnet(proj, cs, dft, fn_w_bf, *, n_seq, seq_len, tok0):
    nj = seq_len // FN_TR
    sb0 = tok0 // seq_len
    return pl.pallas_call(
        functools.partial(_fnet_kernel, t_len=seq_len),
        grid=(n_seq, nj),
        in_specs=[
            pl.BlockSpec((seq_len, GROUP_W), lambda s, j: (sb0 + s, COL_XC)),
            pl.BlockSpec((GROUP_W, 2 * GROUP_W), lambda s, j: (0, 0)),
            pl.BlockSpec((FN_TR, 2 * seq_len), lambda s, j: (j, 0)),
            pl.BlockSpec((GROUP_W, GROUP_W), lambda s, j: (0, 0)),
        ],
        out_specs=pl.BlockSpec((FN_TR, GROUP_W), lambda s, j: (s * nj + j, 0)),
        out_shape=jax.ShapeDtypeStruct((n_seq * seq_len, GROUP_W), F32),
        scratch_shapes=[pltpu.VMEM((2 * seq_len, GROUP_W), BF16)],
        compiler_params=pltpu.CompilerParams(vmem_limit_bytes=VMEM_LIMIT),
        name="fnet_mixer",
    )(proj, cs, dft, fn_w_bf)


GM_TB = 512


def _gmlp_kernel(gu_ref, gv_ref, g_ref, w_ref, bias_ref, hm_ref, gn_ref, o_ref):
    hmb = hm_ref[...]
    for c in range(GM_TB // GM_CHUNK):
        rows = slice(GM_CHUNK * c, GM_CHUNK * (c + 1))
        gu = _gelu(gu_ref[rows, :])
        gv = _rms(_gelu(gv_ref[rows, :]), g_ref[...]).astype(BF16)
        g4 = jnp.concatenate([gv] * GM_HEADS, axis=0) * hmb
        sp = _dot(w_ref[...], g4) + bias_ref[...]
        o_ref[rows, :] = _rms(gu * sp, gn_ref[...])


def _gmlp(proj, gm_norm_g, wcat, bias, hm, gn_d):
    return pl.pallas_call(
        _gmlp_kernel,
        grid=(N_TOK // GM_TB,),
        in_specs=[
            pl.BlockSpec((GM_TB, GROUP_W), lambda i: (i, COL_GU)),
            pl.BlockSpec((GM_TB, GROUP_W), lambda i: (i, COL_GV)),
            pl.BlockSpec((1, GROUP_W), lambda i: (0, 0)),
            pl.BlockSpec((GM_CHUNK, GM_HEADS * GM_CHUNK), lambda i: (0, 0)),
            pl.BlockSpec((GM_CHUNK, GROUP_W), lambda i: (0, 0)),
            pl.BlockSpec((GM_HEADS * GM_CHUNK, GROUP_W), lambda i: (0, 0)),
            pl.BlockSpec((1, GROUP_W), lambda i: (0, 0)),
        ],
        out_specs=pl.BlockSpec((GM_TB, GROUP_W), lambda i: (i, 0)),
        out_shape=jax.ShapeDtypeStruct((N_TOK, GROUP_W), F32),
        compiler_params=pltpu.CompilerParams(vmem_limit_bytes=VMEM_LIMIT),
        name="gmlp_mixer",
    )(proj, proj, gm_norm_g.reshape(1, GROUP_W), wcat, bias, hm, gn_d.reshape(1, GROUP_W))


def _outproj_kernel(x_ref, xa_ref, hgate_ref, ys_ref, of_ref, ob_ref, yc_ref, od_ref, mod_ref,
                    d_ref, wglu_ref, gn_ref, bdm_ref, wout_ref, n2_ref, x1_ref, h2_ref):
    gn = gn_ref[...]
    y5 = _gelu(ys_ref[...] + d_ref[...] * xa_ref[...])
    glu = jax.nn.sigmoid(_dot(y5.astype(BF16), wglu_ref[...]))
    out_a = _rms(y5 * glu, gn[:, 0:GROUP_W])
    o = of_ref[...] + ob_ref[...]
    o2 = o * o
    o2h = o2.astype(BF16)
    o2l = (o2 - o2h.astype(F32)).astype(BF16)
    ms = _dot(o2h, bdm_ref[...]) + _dot(o2l, bdm_ref[...])
    hg = hgate_ref[...]
    out_b = o * lax.rsqrt(ms + EPS) * gn[:, GROUP_W:2 * GROUP_W] * (hg * jax.nn.sigmoid(hg))
    out_c = _rms(yc_ref[...], gn[:, 2 * GROUP_W:3 * GROUP_W])
    out_d = od_ref[...]
    m = None
    for k, part in enumerate((out_a, out_b, out_c, out_d)):
        t = _dot(part.astype(BF16), wout_ref[GROUP_W * k:GROUP_W * (k + 1), :])
        m = t if m is None else m + t
    mod = mod_ref[0]
    g1 = mod[:, 2 * D_MODEL:3 * D_MODEL]
    sh2 = mod[:, 3 * D_MODEL:4 * D_MODEL]
    sc2 = mod[:, 4 * D_MODEL:5 * D_MODEL]
    x1 = x_ref[...] + g1 * m
    x1_ref[...] = x1
    h2_ref[...] = (_rms(x1, n2_ref[...]) * (1.0 + sc2) + sh2).astype(BF16)


def _outproj(x, proj, ys, of, ob, yc, od, mod, s5_d, wglu_bf, gn, bdm, wout_bf, norm2_g):
    def tok(width, col=0):
        return pl.BlockSpec((TB, width), lambda i: (i, col))

    def const(shape):
        return pl.BlockSpec(shape, lambda i: (0,) * len(shape))

    return pl.pallas_call(
        _outproj_kernel,
        grid=(N_TOK // TB,),
        in_specs=[
            tok(D_MODEL), tok(GROUP_W, COL_XA), tok(GROUP_W, COL_HGATE),
            tok(GROUP_W), tok(GROUP_W), tok(GROUP_W), tok(GROUP_W), tok(GROUP_W),
            pl.BlockSpec((1, 1, N_MOD * D_MODEL), lambda i: (_mod_row(i), 0, 0)),
            const((1, GROUP_W)), const((GROUP_W, GROUP_W)), const((1, D_MODEL)),
            const((GROUP_W, GROUP_W)), const((D_MODEL, D_MODEL)), const((1, D_MODEL)),
        ],
        out_specs=(tok(D_MODEL), tok(D_MODEL)),
        out_shape=(jax.ShapeDtypeStruct((N_TOK, D_MODEL), F32),
                   jax.ShapeDtypeStruct((N_TOK, D_MODEL), BF16)),
        compiler_params=pltpu.CompilerParams(vmem_limit_bytes=VMEM_LIMIT),
        name="out_proj",
    )(x, proj, proj, ys, of, ob, yc, od, mod, s5_d.reshape(1, GROUP_W), wglu_bf,
      gn.reshape(1, D_MODEL), bdm, wout_bf, norm2_g.reshape(1, D_MODEL))


def _ffn_kernel(x1_ref, h2_ref, mod_ref, wup_ref, cw_ref, cb_ref, wdn_ref, fg_ref, o_ref, *, final):
    i = pl.program_id(0)
    seg = jnp.where(i < NB_CTX, SEQ, GRID_W)
    row = lax.broadcasted_iota(jnp.int32, (TB, 1), 0)
    pos = jnp.bitwise_and(row, seg - 1)
    keep_prev = (pos != 0).astype(F32)
    keep_next = (pos != seg - 1).astype(F32)
    h2 = h2_ref[...]

    def conv(z, lo):
        w = cw_ref[:, lo:lo + FF_TILE]
        zp = pltpu.roll(z, 1, 0) * keep_prev
        zn = pltpu.roll(z, TB - 1, 0) * keep_next
        return cb_ref[:, lo:lo + FF_TILE] + zp * w[0:1] + z * w[1:2] + zn * w[2:3]

    acc = None
    for j in range(D_FF // FF_TILE):
        lo = FF_TILE * j
        za = _dot(h2, wup_ref[:, lo:lo + FF_TILE])
        zb = _dot(h2, wup_ref[:, D_FF + lo:D_FF + lo + FF_TILE])
        hid = _gelu(conv(za, lo)) * conv(zb, D_FF + lo)
        t = _dot(hid.astype(BF16), wdn_ref[lo:lo + FF_TILE, :])
        acc = t if acc is None else acc + t
    g2 = mod_ref[0][:, 5 * D_MODEL:6 * D_MODEL]
    x2 = x1_ref[...] + g2 * acc
    if final:
        x2 = _rms(x2, fg_ref[...])
    o_ref[...] = x2


def _ffn(x1, h2, mod, wup_bf, conv_w, conv_b, wdn_bf, final_g, *, final):
    def const(shape):
        return pl.BlockSpec(shape, lambda i: (0,) * len(shape), pipeline_mode=pl.Buffered(1))

    return pl.pallas_call(
        functools.partial(_ffn_kernel, final=final),
        grid=(N_TOK // TB,),
        in_specs=[
            pl.BlockSpec((TB, D_MODEL), lambda i: (i, 0)),
            pl.BlockSpec((TB, D_MODEL), lambda i: (i, 0)),
            pl.BlockSpec((1, 1, N_MOD * D_MODEL), lambda i: (_mod_row(i), 0, 0)),
            const((D_MODEL, 2 * D_FF)), const((3, 2 * D_FF)), const((1, 2 * D_FF)),
            const((D_FF, D_MODEL)), const((1, D_MODEL)),
        ],
        out_specs=pl.BlockSpec((TB, D_MODEL), lambda i: (i, 0)),
        out_shape=jax.ShapeDtypeStruct((N_TOK, D_MODEL), F32),
        compiler_params=pltpu.CompilerParams(vmem_limit_bytes=VMEM_LIMIT),
        name="conv_ffn",
    )(x1, h2, mod, wup_bf, conv_w, conv_b.reshape(1, 2 * D_FF), wdn_bf, final_g.reshape(1, D_MODEL))


def kernel(x_prompt, x_sample, state_s5_re, state_s5_im, state_hgrn, c, c_ctx, w_ada, b_ada, norm1_g,
           norm2_g, w_in, s5_lam_re, s5_lam_im, s5_log_dt, s5_b_re, s5_b_im, s5_c_re, s5_c_im, s5_d,
           s5_w_glu, hg_lb_logits, fn_w, gm_norm_g, gm_ws, gm_bs, grp_norm_g, w_out, ffn_w_up,
           ffn_conv_w, ffn_conv_b, ffn_w_down, final_norm_g):
    x = jnp.concatenate([x_prompt.reshape(N_CTX, D_MODEL), x_sample.reshape(N_LAT, D_MODEL)], axis=0)
    cvecs = jnp.concatenate([c_ctx[None], c, jnp.zeros((8 - 1 - DEC_BATCH, D_MODEL), F32)], axis=0)
    mods = _ada(cvecs, w_ada, b_ada).reshape(DEPTH, 8, 1, N_MOD * D_MODEL)

    lb_p = jax.nn.softmax(hg_lb_logits.astype(F32), axis=0)
    lbs = jnp.maximum(jnp.cumsum(lb_p, axis=0) - lb_p[0], 0.0)
    lbps = jnp.stack([jnp.log(lbs), jnp.log1p(-lbs), 1.0 - lbs], axis=2)

    hg_consts_np = _hgrn_consts()
    hg_consts = (jnp.asarray(hg_consts_np[0], BF16), jnp.asarray(hg_consts_np[1], F32),
                 jnp.asarray(hg_consts_np[2], F32))
    cs_np, dft_ctx_np = _dft_consts(SEQ)
    _, dft_lat_np = _dft_consts(DEC_SEQ)
    cs = jnp.asarray(cs_np, F32).astype(BF16)
    dft_ctx = jnp.asarray(dft_ctx_np, F32).astype(BF16)
    dft_lat = jnp.asarray(dft_lat_np, F32).astype(BF16)
    gm_hm = jnp.asarray(np.kron(np.eye(GM_HEADS), np.ones((GM_CHUNK, GROUP_W // GM_HEADS))), BF16)
    bdm = jnp.asarray(np.kron(np.eye(HG_HEADS), np.ones((HG_DK, HG_DK))) / HG_DK, BF16)

    new_re, new_im, new_hg = [], [], []
    for l in range(DEPTH):
        mod = mods[l]
        proj = _inproj(x, mod, norm1_g[l], w_in[l].astype(BF16))

        m, f2, e2, al = _s5_weights(s5_lam_re[l], s5_lam_im[l], s5_log_dt[l], s5_b_re[l], s5_b_im[l],
                                    s5_c_re[l], s5_c_im[l])
        u = proj[:, :GROUP_W].reshape(S5_ROWS, S5_L, S5_G, S5_P).transpose(0, 2, 1, 3)
        u = u.reshape(S5_ROWS, S5_G * S5_L * S5_P)
        h0_lat = jnp.stack([state_s5_re[:, l], state_s5_im[:, l]])
        h0_lat = h0_lat.transpose(2, 0, 1, 3, 4).reshape(2, 2, DEC_BATCH, S5_SW)
        h0 = jnp.concatenate([jnp.zeros((2, 2, BATCH, S5_SW), F32), h0_lat], axis=2)
        ys, hfin = _s5(u, m, f2, e2, al, h0)
        ys = ys.reshape(S5_ROWS, S5_G, S5_L, S5_P).transpose(0, 2, 1, 3).reshape(N_TOK, GROUP_W)
        hfin = hfin.reshape(2, 2, BATCH, S5_G, S5_N).transpose(1, 2, 0, 3, 4)
        new_re.append(hfin[0])
        new_im.append(hfin[1])

        s0 = state_hgrn[:, l].astype(F32)
        s0 = jnp.einsum('bdhkv,hg->bdhvgk', s0, jnp.eye(HG_HEADS, dtype=F32))
        s0 = s0.reshape(DEC_BATCH, 2, GROUP_W, GROUP_W)
        of_c, ob_c, sfin = _hgrn(proj, lbps[l], hg_consts, None, n_seq=BATCH, seq_len=SEQ, tok0=0)
        of_l, ob_l, _ = _hgrn(proj, lbps[l], hg_consts, s0, n_seq=DEC_BATCH, seq_len=DEC_SEQ, tok0=N_CTX)
        of = jnp.concatenate([of_c, of_l], axis=0)
        ob = jnp.concatenate([ob_c, ob_l], axis=0)
        sfin = sfin.reshape(BATCH, 2, HG_HEADS, HG_DK, HG_HEADS, HG_DK)
        sfin = jnp.stack([sfin[:, :, h, :, h, :] for h in range(HG_HEADS)], axis=2)
        new_hg.append(sfin.transpose(0, 1, 2, 4, 3))

        fnw = fn_w[l].astype(BF16)
        yc = jnp.concatenate([
            _fnet(proj, cs, dft_ctx, fnw, n_seq=BATCH, seq_len=SEQ, tok0=0),
            _fnet(proj, cs, dft_lat, fnw, n_seq=DEC_BATCH, seq_len=DEC_SEQ, tok0=N_CTX)], axis=0)

        gm_w = gm_ws[l].transpose(1, 0, 2).reshape(GM_CHUNK, GM_HEADS * GM_CHUNK).astype(BF16)
        gm_b = jnp.repeat(gm_bs[l].T, GROUP_W // GM_HEADS, axis=1)
        od = _gmlp(proj, gm_norm_g[l], gm_w, gm_b, gm_hm, grp_norm_g[l, 3 * GROUP_W:])

        x1, h2 = _outproj(x, proj, ys, of, ob, yc, od, mod, s5_d[l], s5_w_glu[l].astype(BF16),
                          grp_norm_g[l], bdm, w_out[l].astype(BF16), norm2_g[l])
        x = _ffn(x1, h2, mod, ffn_w_up[l].astype(BF16), ffn_conv_w[l], ffn_conv_b[l],
                 ffn_w_down[l].astype(BF16), final_norm_g, final=(l == DEPTH - 1))

    y_prompt = x[:N_CTX].reshape(BATCH, SEQ, D_MODEL)
    y_sample = x[N_CTX:].reshape(DEC_BATCH, DEC_SEQ, D_MODEL)
    return (y_prompt, y_sample, jnp.stack(new_re, axis=1), jnp.stack(new_im, axis=1),
            jnp.stack(new_hg, axis=1))
```

```python
import functools
import math

import numpy as np
import jax
import jax.numpy as jnp
from jax import lax
from jax.experimental import pallas as pl
from jax.experimental.pallas import tpu as pltpu

D_MODEL = 1024
BATCH = 16
SEQ = 256
DEPTH = 2
DEC_BATCH = 2
DEC_SEQ = 2048
GRID_W = 64
GROUP_W = 256
S5_P = 16
S5_G = 16
S5_N = 64
HG_HEADS = 4
HG_DK = 64
GM_HEADS = 4
GM_CHUNK = 128
D_FF = 2816
N_MOD = 6
D_IN = 9 * GROUP_W
EPS = 1e-6
LAM_RE_MAX = -1e-4

N_CTX = BATCH * SEQ
N_LAT = DEC_BATCH * DEC_SEQ
N_TOK = N_CTX + N_LAT
TB = 256
NB_CTX = N_CTX // TB
NB_LAT_SEQ = DEC_SEQ // TB
S5_L = 16
HG_L = 64
FF_TILE = 256
LANES = 128
VMEM_LIMIT = 56 * 1024 * 1024

F32 = jnp.float32
BF16 = jnp.bfloat16

COL_XA, COL_HQ, COL_HF_FWD, COL_HF_BWD, COL_HI, COL_HGATE, COL_XC, COL_GU, COL_GV = range(9)


def _mod_row(i):
    return jnp.where(i < NB_CTX, 0, 1 + (i - NB_CTX) // NB_LAT_SEQ)


def _gelu(x):
    return 0.5 * x * (1.0 + jnp.tanh(0.7978845608028654 * (x + 0.044715 * (x * x * x))))


def _rms(x, g):
    return x * lax.rsqrt(jnp.mean(x * x, axis=-1, keepdims=True) + EPS) * g


def _dot(a, b):
    return jnp.dot(a, b, preferred_element_type=F32)


def _dot_nt(a, b, precision=None):
    return lax.dot_general(a, b, (((1,), (1,)), ((), ())), precision=precision, preferred_element_type=F32)


def _dot_tn(a, b):
    return lax.dot_general(a, b, (((0,), (0,)), ((), ())), preferred_element_type=F32)


def _const_spec(shape, single=False):
    kw = {"pipeline_mode": pl.Buffered(1)} if single else {}
    return pl.BlockSpec(shape, lambda *_: (0,) * len(shape), **kw)


def _layer_spec(shape, l, single=False):
    kw = {"pipeline_mode": pl.Buffered(1)} if single else {}
    return pl.BlockSpec((None,) + tuple(shape), lambda *_: (l,) + (0,) * len(shape), **kw)


def _mod_spec(l):
    return pl.BlockSpec((None, None, 1, N_MOD * D_MODEL), lambda i: (l, _mod_row(i), 0, 0))


def _x_specs(split):
    if split:
        return [pl.BlockSpec((TB, D_MODEL), lambda i: (jnp.minimum(i, NB_CTX - 1), 0)),
                pl.BlockSpec((TB, D_MODEL), lambda i: (jnp.maximum(i - NB_CTX, 0), 0))]
    return [pl.BlockSpec((TB, D_MODEL), lambda i: (i, 0))]


def _x_value(x_refs):
    if len(x_refs) == 1:
        return x_refs[0][...]
    return jnp.where(pl.program_id(0) < NB_CTX, x_refs[0][...], x_refs[1][...])


ADA_TN = 1536


def _ada_kernel(c_ref, w_ref, b_ref, o_ref):
    cv = c_ref[...]
    s = cv * jax.nn.sigmoid(cv)
    o_ref[0] = _dot(s.astype(BF16), w_ref[0].astype(BF16)) + b_ref[0]


def _ada(cvecs, w_ada, b_ada):
    n = N_MOD * D_MODEL
    return pl.pallas_call(
        _ada_kernel,
        grid=(DEPTH, n // ADA_TN),
        in_specs=[
            pl.BlockSpec((8, D_MODEL), lambda l, j: (0, 0)),
            pl.BlockSpec((1, D_MODEL, ADA_TN), lambda l, j: (l, 0, j)),
            pl.BlockSpec((1, 1, ADA_TN), lambda l, j: (l, 0, j)),
        ],
        out_specs=pl.BlockSpec((1, 8, ADA_TN), lambda l, j: (l, 0, j)),
        out_shape=jax.ShapeDtypeStruct((DEPTH, 8, n), F32),
        compiler_params=pltpu.CompilerParams(vmem_limit_bytes=VMEM_LIMIT),
        name="ada_mod",
    )(cvecs, w_ada, b_ada.reshape(DEPTH, 1, n))


def _inproj_kernel(*refs):
    *x_refs, mod_ref, g_ref, w_ref, o_ref, wb_ref = refs

    @pl.when(pl.program_id(0) == 0)
    def _():
        wb_ref[...] = w_ref[...].astype(BF16)

    mod = mod_ref[...]
    sh = mod[:, 0:D_MODEL]
    sc = mod[:, D_MODEL:2 * D_MODEL]
    h = _rms(_x_value(x_refs), g_ref[...]) * (1.0 + sc) + sh
    o_ref[...] = _dot(h.astype(BF16), wb_ref[...])


def _inproj(xs, mods, norm_g, w_in, l):
    return pl.pallas_call(
        _inproj_kernel,
        grid=(N_TOK // TB,),
        in_specs=_x_specs(len(xs) == 2) + [
            _mod_spec(l),
            _layer_spec((1, D_MODEL), l),
            _layer_spec((D_MODEL, D_IN), l, single=True),
        ],
        out_specs=pl.BlockSpec((TB, D_IN), lambda i: (i, 0)),
        out_shape=jax.ShapeDtypeStruct((N_TOK, D_IN), F32),
        scratch_shapes=[pltpu.VMEM((D_MODEL, D_IN), BF16)],
        compiler_params=pltpu.CompilerParams(vmem_limit_bytes=VMEM_LIMIT),
        name="in_proj",
    )(*xs, mods, norm_g, w_in)


S5_NPAIR = S5_G // 2
S5_CW = S5_L * S5_P
S5_SW = S5_G * S5_N
S5_STEP_ROWS = N_CTX // S5_L
S5_CTX_NC = SEQ // S5_L
S5_LAT_NC = DEC_SEQ // S5_L


def _s5_tables(lam_re, lam_im, log_dt, b_re, b_im, c_re, c_im):
    lr = jnp.minimum(lam_re.astype(F32), LAM_RE_MAX)
    li = lam_im.astype(F32)
    dt = jnp.exp(log_dt.astype(F32))[..., None]
    mag = jnp.exp(lr * dt)
    ang = li * dt
    ab_re = mag * jnp.cos(ang)
    ab_im = mag * jnp.sin(ang)
    den = lr * lr + li * li
    xr = ab_re - 1.0
    z_re = (xr * lr + ab_im * li) / den
    z_im = (ab_im * lr - xr * li) / den
    bb_re = z_re[..., None] * b_re - z_im[..., None] * b_im
    bb_im = z_re[..., None] * b_im + z_im[..., None] * b_re
    tau = jnp.arange(S5_L + 1, dtype=F32)[:, None, None, None, None]
    pm = jnp.exp(lr * dt * tau)
    pa = li * dt * tau
    pw_re = pm * jnp.cos(pa)
    pw_im = pm * jnp.sin(pa)
    eye2 = jnp.eye(2, dtype=F32)

    def pw_pairs(a):
        a = a.transpose(1, 2, 3, 0, 4).reshape(DEPTH, 2, S5_NPAIR, 2, S5_L + 1, S5_N)
        return a.transpose(0, 1, 2, 4, 3, 5).reshape(DEPTH, 2, S5_NPAIR, S5_L + 1, 2 * S5_N)

    def mat_pairs(a):
        a = a.reshape(DEPTH, 2, S5_NPAIR, 2, S5_P, S5_N)
        return jnp.einsum('ldjaqn,ab->ldjaqbn', a, eye2).reshape(DEPTH, 2, S5_NPAIR, 2, S5_P, 2 * S5_N)

    al = jnp.stack([pw_re[S5_L], pw_im[S5_L]], axis=2).reshape(DEPTH, 2, 2, 1, S5_SW)
    return (pw_pairs(pw_re), pw_pairs(pw_im),
            mat_pairs(bb_re.transpose(0, 1, 2, 4, 3)), mat_pairs(bb_im.transpose(0, 1, 2, 4, 3)),
            mat_pairs(c_re.astype(F32)), mat_pairs(c_im.astype(F32)), al)


def _s5_shift_consts():
    n = S5_CW
    r = np.arange(n)[:, None]
    c = np.arange(n)[None, :]
    fwd = [(c == r + S5_P * s) for s in range(S5_L)]
    bwd = [(c == r - S5_P * (S5_L - 1 - s)) for s in range(S5_L)]
    return np.stack([np.concatenate(fwd, axis=1), np.concatenate(bwd, axis=1)]).astype(np.float32)


def _s5_prep_kernel(pwr_ref, pwi_ref, br_ref, bi_ref, cr_ref, ci_ref, scat_ref, m_ref, f_ref, e_ref, k_ref):
    L = S5_L
    hp = lax.Precision.HIGHEST
    sw = 2 * S5_N

    def pair_body(j, carry):
        for d in range(2):
            pwr = pwr_ref[0, d, j]
            pwi = pwi_ref[0, d, j]
            for gl in range(2):
                br, bi = br_ref[0, d, j, gl], bi_ref[0, d, j, gl]
                cr, ci = cr_ref[0, d, j, gl], ci_ref[0, d, j, gl]
                ca_r, ca_i, f_r, f_i, e_r, e_i = [], [], [], [], [], []
                for t in range(L):
                    kk = t if d == 0 else L - 1 - t
                    pr, pi = pwr[kk:kk + 1], pwi[kk:kk + 1]
                    ca_r.append(cr * pr - ci * pi)
                    ca_i.append(cr * pi + ci * pr)
                    kf = L - 1 - t if d == 0 else t
                    pr, pi = pwr[kf:kf + 1], pwi[kf:kf + 1]
                    f_r.append(br * pr - bi * pi)
                    f_i.append(br * pi + bi * pr)
                    ke = t + 1 if d == 0 else L - t
                    pr, pi = pwr[ke:ke + 1], pwi[ke:ke + 1]
                    e_r.append(cr * pr - ci * pi)
                    e_i.append(-(cr * pi + ci * pr))
                cat = lambda xs: jnp.concatenate(xs, axis=0)
                k = _dot_nt(br, cat(ca_r), hp) - _dot_nt(bi, cat(ca_i), hp)
                g16 = pl.multiple_of((2 * j + gl) * S5_P, S5_P)
                k_ref[d, pl.ds(g16, S5_P), :] = k
                rows = slice(S5_CW * gl, S5_CW * (gl + 1))
                f_ref[0, d, j, rows, 0:sw] = cat(f_r).astype(BF16)
                f_ref[0, d, j, rows, sw:2 * sw] = cat(f_i).astype(BF16)
                e_ref[0, d, 0, j, rows, :] = cat(e_r).astype(BF16)
                e_ref[0, d, 1, j, rows, :] = cat(e_i).astype(BF16)
        return carry

    lax.fori_loop(0, S5_NPAIR, pair_body, 0)
    kf = k_ref[0].astype(BF16)
    kb = k_ref[1].astype(BF16)
    for s in range(L):
        cols = slice(S5_CW * s, S5_CW * (s + 1))
        res = _dot(kf, scat_ref[0, :, cols]) + _dot(kb, scat_ref[1, :, cols])
        for g in range(S5_G):
            m_ref[0, g, S5_P * s:S5_P * (s + 1), :] = res[S5_P * g:S5_P * (g + 1), :].astype(BF16)


def _s5_prep(tables, scat):
    pwr, pwi, br, bi, cr, ci, _ = tables

    def lspec(shape):
        return pl.BlockSpec((1,) + shape, lambda l: (l,) + (0,) * len(shape))

    pw_shape = (2, S5_NPAIR, S5_L + 1, 2 * S5_N)
    mat_shape = (2, S5_NPAIR, 2, S5_P, 2 * S5_N)
    return pl.pallas_call(
        _s5_prep_kernel,
        grid=(DEPTH,),
        in_specs=[lspec(pw_shape), lspec(pw_shape), lspec(mat_shape), lspec(mat_shape), lspec(mat_shape),
                  lspec(mat_shape), _const_spec(scat.shape)],
        out_specs=(lspec((S5_G, S5_CW, S5_CW)), lspec((2, S5_NPAIR, 2 * S5_CW, 4 * S5_N)),
                   lspec((2, 2, S5_NPAIR, 2 * S5_CW, 2 * S5_N))),
        out_shape=(jax.ShapeDtypeStruct((DEPTH, S5_G, S5_CW, S5_CW), BF16),
                   jax.ShapeDtypeStruct((DEPTH, 2, S5_NPAIR, 2 * S5_CW, 4 * S5_N), BF16),
                   jax.ShapeDtypeStruct((DEPTH, 2, 2, S5_NPAIR, 2 * S5_CW, 2 * S5_N), BF16)),
        scratch_shapes=[pltpu.VMEM((2, S5_G * S5_P, S5_CW), F32)],
        compiler_params=pltpu.CompilerParams(vmem_limit_bytes=VMEM_LIMIT),
        name="s5_prep",
    )(pwr, pwi, br, bi, cr, ci, scat)


def _s5_kernel(xlo_ref, xhi_ref, m_ref, f_ref, e_ref, al_ref, h0_ref, ylo_ref, yhi_ref, hfin_ref,
               uy_ref, zh_ref):
    R = S5_STEP_ROWS
    pw = 2 * S5_CW
    sw = 2 * S5_N
    gph = S5_G // 2
    step = pl.program_id(0)

    for half, x_ref in enumerate((xlo_ref, xhi_ref)):
        for t in range(S5_L):
            xt = x_ref[pl.ds(t, R, stride=S5_L), :]
            for gl in range(gph):
                lo = S5_CW * (gph * half + gl) + S5_P * t
                uy_ref[:, lo:lo + S5_P] = xt[:, S5_P * gl:S5_P * (gl + 1)]

    for j in range(S5_NPAIR):
        ub = uy_ref[:, pw * j:pw * (j + 1)].astype(BF16)
        for d in range(2):
            o = _dot(ub, f_ref[d, j])
            zh_ref[d, 0, :, sw * j:sw * (j + 1)] = o[:, :sw]
            zh_ref[d, 1, :, sw * j:sw * (j + 1)] = o[:, sw:]

    def scan_group(seq_rows, nc, init, fin_rows):
        chains = [(d, r0) for d in range(2) for r0 in seq_rows]

        def body(c, carry):
            out = []
            for k, (d, r0) in enumerate(chains):
                hr, hi = carry[2 * k], carry[2 * k + 1]
                r = r0 + (c if d == 0 else nc - 1 - c)
                zr = zh_ref[d, 0, pl.ds(r, 1), :]
                zi = zh_ref[d, 1, pl.ds(r, 1), :]
                zh_ref[d, 0, pl.ds(r, 1), :] = hr
                zh_ref[d, 1, pl.ds(r, 1), :] = hi
                ar = al_ref[d, 0]
                ai = al_ref[d, 1]
                out.append(ar * hr - ai * hi + zr)
                out.append(ar * hi + ai * hr + zi)
            return tuple(out)

        fin = lax.fori_loop(0, nc, body, tuple(init))
        if fin_rows is not None:
            for k, (d, _) in enumerate(chains):
                s = fin_rows[k % len(seq_rows)]
                hfin_ref[d, 0, s:s + 1, :] = fin[2 * k]
                hfin_ref[d, 1, s:s + 1, :] = fin[2 * k + 1]

    @pl.when(step == 0)
    def _():
        zero = jnp.zeros((1, S5_SW), F32)
        for s0 in range(0, BATCH, 4):
            seqs = list(range(s0, s0 + 4))
            scan_group([s * S5_CTX_NC for s in seqs], S5_CTX_NC, [zero] * 16, seqs)

    @pl.when(step == 1)
    def _():
        init = []
        for d in range(2):
            for s in range(DEC_BATCH):
                init.append(h0_ref[d, 0, s:s + 1, :])
                init.append(h0_ref[d, 1, s:s + 1, :])
        scan_group([s * S5_LAT_NC for s in range(DEC_BATCH)], S5_LAT_NC, init, None)

    for j in range(S5_NPAIR):
        acc = None
        for d in range(2):
            for ri in range(2):
                hb = zh_ref[d, ri, :, sw * j:sw * (j + 1)].astype(BF16)
                t = _dot_nt(hb, e_ref[d, ri, j])
                acc = t if acc is None else acc + t
        for gl in range(2):
            g = 2 * j + gl
            ug = uy_ref[:, S5_CW * g:S5_CW * (g + 1)].astype(BF16)
            uy_ref[:, S5_CW * g:S5_CW * (g + 1)] = acc[:, S5_CW * gl:S5_CW * (gl + 1)] + _dot(ug, m_ref[g])

    for half, y_ref in enumerate((ylo_ref, yhi_ref)):
        for t in range(S5_L):
            pieces = []
            for gl in range(gph):
                lo = S5_CW * (gph * half + gl) + S5_P * t
                pieces.append(uy_ref[:, lo:lo + S5_P])
            y_ref[pl.ds(t, R, stride=S5_L), :] = jnp.concatenate(pieces, axis=1)


def _s5(proj, m, f2, e2, al, h0, l):
    half = pl.BlockSpec((N_CTX, LANES), lambda s: (s, 0))
    return pl.pallas_call(
        _s5_kernel,
        grid=(2,),
        in_specs=[pl.BlockSpec((N_CTX, LANES), lambda s: (s, 0)),
                  pl.BlockSpec((N_CTX, LANES), lambda s: (s, 1)),
                  _layer_spec(m.shape[1:], l, single=True), _layer_spec(f2.shape[1:], l, single=True),
                  _layer_spec(e2.shape[1:], l, single=True), _layer_spec(al.shape[1:], l),
                  _layer_spec(h0.shape[1:], l)],
        out_specs=(half, half, _const_spec((2, 2, BATCH, S5_SW))),
        out_shape=(jax.ShapeDtypeStruct((N_TOK, LANES), F32),
                   jax.ShapeDtypeStruct((N_TOK, LANES), F32),
                   jax.ShapeDtypeStruct((2, 2, BATCH, S5_SW), F32)),
        scratch_shapes=[pltpu.VMEM((S5_STEP_ROWS, S5_G * S5_CW), F32),
                        pltpu.VMEM((2, 2, S5_STEP_ROWS, S5_SW), F32)],
        compiler_params=pltpu.CompilerParams(vmem_limit_bytes=VMEM_LIMIT),
        name="s5_mixer",
    )(proj, proj, m, f2, e2, al, h0)


HG_NLEV = int(math.log2(HG_L))
HG_W = HG_HEADS * HG_DK
HG_NB_CTX = N_CTX // HG_L
HG_NC_CTX = SEQ // HG_L
HG_NC_LAT = DEC_SEQ // HG_L
assert HG_L == HG_DK


def _hgrn_consts():
    L = HG_L
    w = np.zeros((HG_NLEV + 2, L, L), np.float32)
    mask = np.zeros((HG_NLEV + 1, L, L), np.float32)
    for lev in range(HG_NLEV):
        blk = L >> lev
        half = blk // 2
        for t in range(L):
            p, o = divmod(t, blk)
            bd = p * blk + half - 1
            if o >= half:
                w[lev, t, bd + 1:t + 1] = 1.0
            else:
                w[lev, t, t + 1:bd + 1] = 1.0
        jj, ii = np.meshgrid(np.arange(L), np.arange(L), indexing='ij')
        mask[lev] = ((jj // blk == ii // blk) & (jj % blk >= half) & (ii % blk < half)).astype(np.float32)
    mask[HG_NLEV] = np.eye(L, dtype=np.float32)
    for t in range(L):
        w[HG_NLEV, t, :t + 1] = 1.0
        w[HG_NLEV + 1, t, t + 1:] = 1.0
    out = []
    for wd, md in ((w, mask), (w[:, ::-1, ::-1], mask[:, ::-1, ::-1])):
        wflat = wd.reshape((HG_NLEV + 2) * L, L)
        out.append((np.concatenate([wflat] * 3, axis=1), np.tile(md, (1, 1, HG_HEADS))))
    wcat = np.stack([out[0][0], out[1][0]])
    mask4 = np.stack([out[0][1], out[1][1]])
    hm = np.kron(np.eye(HG_HEADS, dtype=np.float32), np.ones((HG_DK, HG_DK), np.float32))
    return wcat, mask4, hm


def _hg_pos(i):
    is_ctx = i < HG_NB_CTX
    c = jnp.where(is_ctx, i % HG_NC_CTX, (i - HG_NB_CTX) % HG_NC_LAT)
    nc = jnp.where(is_ctx, HG_NC_CTX, HG_NC_LAT)
    return is_ctx, c, nc


def _hg_bwd_blk(i):
    _, c, nc = _hg_pos(i)
    return i + nc - 1 - 2 * c


def _hgrn_dir(d, q, z, v, lbp, wcat_ref, mask_ref, hm, st):
    L = HG_L
    az = jnp.abs(z)
    sp = jnp.log1p(jnp.exp(-az))
    ls = jnp.minimum(z, 0.0) - sp
    kk = lbp[2:3] * jnp.exp(jnp.minimum(-z, 0.0) - sp)
    a = lbp[0:1]
    b = lbp[1:2] + ls
    logf = jnp.maximum(a, b) + jnp.log1p(jnp.exp(-jnp.abs(a - b)))
    p0 = logf.astype(BF16)
    r0 = logf - p0.astype(F32)
    p1 = r0.astype(BF16)
    p2 = (r0 - p1.astype(F32)).astype(BF16)
    ex = jnp.exp(_dot(wcat_ref[d], jnp.concatenate([p0, p1, p2], axis=0)))
    hmb = hm.astype(BF16)

    def bd4(x):
        return jnp.concatenate([x] * HG_HEADS, axis=0) * hmb

    sc = None
    for lev in range(HG_NLEV + 1):
        if lev < HG_NLEV:
            al = ex[L * lev:L * (lev + 1)]
            lhs = (q * al).astype(BF16)
            rhs = (kk * al).astype(BF16)
        else:
            lhs = q.astype(BF16)
            rhs = kk.astype(BF16)
        t = _dot_nt(lhs, bd4(rhs)) * mask_ref[d, lev]
        sc = t if sc is None else sc + t
    vb = v.astype(BF16)
    o = _dot(sc.astype(BF16), bd4(vb))
    eq = ex[L * HG_NLEV:L * (HG_NLEV + 1)]
    o = o + _dot_nt((q * eq).astype(BF16), st.astype(BF16))
    ke = (kk * ex[L * (HG_NLEV + 1):L * (HG_NLEV + 2)]).astype(BF16)
    g = eq[L - 1:L] if d == 0 else eq[0:1]
    st_new = st * g + _dot_tn(vb, ke) * hm
    return o, st_new


def _hgrn_kernel(qf, zf, vf, qb, zb, vb, lb_ref, w_ref, mask_ref, hm_ref, s0_ref,
                 of_ref, ob_ref, sfin_ref, st_ref):
    is_ctx, c, nc = _hg_pos(pl.program_id(0))

    @pl.when(c == 0)
    def _():
        st_ref[...] = jnp.where(is_ctx, 0.0, s0_ref[0])

    hm = hm_ref[...]
    o, s = _hgrn_dir(0, qf[...], zf[...], vf[...], lb_ref[0], w_ref, mask_ref, hm, st_ref[0])
    of_ref[...] = o
    st_ref[0] = s
    o, s = _hgrn_dir(1, qb[...], zb[...], vb[...], lb_ref[1], w_ref, mask_ref, hm, st_ref[1])
    ob_ref[...] = o
    st_ref[1] = s

    @pl.when(jnp.logical_and(is_ctx, c == nc - 1))
    def _():
        for d in range(2):
            s_kv = st_ref[d].T
            for h in range(HG_HEADS):
                sfin_ref[0, d, h] = s_kv[HG_DK * h:HG_DK * (h + 1), HG_DK * h:HG_DK * (h + 1)]


def _hgrn(proj, lbp, consts, s0, l):
    wcat, mask4, hm = consts

    def fwd(col):
        return pl.BlockSpec((HG_L, HG_W), lambda i: (i, col))

    def bwd(col):
        return pl.BlockSpec((HG_L, HG_W), lambda i: (_hg_bwd_blk(i), col))

    return pl.pallas_call(
        _hgrn_kernel,
        grid=(N_TOK // HG_L,),
        in_specs=[fwd(COL_HQ), fwd(COL_HF_FWD), fwd(COL_HI), bwd(COL_HQ), bwd(COL_HF_BWD), bwd(COL_HI),
                  _layer_spec(lbp.shape[1:], l), _const_spec(wcat.shape), _const_spec(mask4.shape),
                  _const_spec(hm.shape),
                  pl.BlockSpec((None, 1, 2, HG_W, HG_W),
                               lambda i: (l, jnp.maximum(i - HG_NB_CTX, 0) // HG_NC_LAT, 0, 0, 0))],
        out_specs=(pl.BlockSpec((HG_L, HG_W), lambda i: (i, 0)),
                   pl.BlockSpec((HG_L, HG_W), lambda i: (_hg_bwd_blk(i), 0)),
                   pl.BlockSpec((1, 2, HG_HEADS, HG_DK, HG_DK),
                                lambda i: (jnp.minimum(i // HG_NC_CTX, BATCH - 1), 0, 0, 0, 0))),
        out_shape=(jax.ShapeDtypeStruct((N_TOK, HG_W), F32),
                   jax.ShapeDtypeStruct((N_TOK, HG_W), F32),
                   jax.ShapeDtypeStruct((BATCH, 2, HG_HEADS, HG_DK, HG_DK), F32)),
        scratch_shapes=[pltpu.VMEM((2, HG_W, HG_W), F32)],
        compiler_params=pltpu.CompilerParams(vmem_limit_bytes=VMEM_LIMIT),
        name="hgrn_mixer",
    )(proj, proj, proj, proj, proj, proj, lbp, wcat, mask4, hm, s0)


FN_TR = 256


def _dft_consts(t_len):
    n = GROUP_W // 4
    k = np.arange(n)
    ang = 2.0 * np.pi * ((k[:, None] * k[None, :]) % n) / n
    eye = np.eye(4)
    cs = np.concatenate([np.kron(eye, np.cos(ang)), np.kron(eye, np.sin(ang))], axis=1) / math.sqrt(n)
    t = np.arange(t_len)
    angt = 2.0 * np.pi * ((t[:, None] * t[None, :]) % t_len) / t_len
    dft = np.concatenate([np.cos(angt), -np.sin(angt)], axis=1) / math.sqrt(t_len)
    return cs.astype(np.float32), dft.astype(np.float32)


def _fnet_kernel(x_ref, cs_ref, dft_ref, w_ref, *rest, t_len):
    o_ref, r_ref = rest[-2:]

    @pl.when(pl.program_id(1) == 0)
    def _():
        t = _dot(x_ref[...].astype(BF16), cs_ref[...])
        r_ref[0:t_len, :] = t[:, :GROUP_W].astype(BF16)
        r_ref[t_len:2 * t_len, :] = t[:, GROUP_W:].astype(BF16)

    y = _dot(dft_ref[...], r_ref[...])
    o_ref[...] = _dot(y.astype(BF16), w_ref[...].astype(BF16))


def _fnet(proj, cs, dft, fn_w, prev, l, *, n_seq, seq_len, tok0):
    nj = seq_len // FN_TR
    sb0 = tok0 // seq_len
    rb0 = tok0 // FN_TR
    in_specs = [
        pl.BlockSpec((seq_len, GROUP_W), lambda s, j: (sb0 + s, COL_XC)),
        _const_spec((GROUP_W, 2 * GROUP_W)),
        pl.BlockSpec((FN_TR, 2 * seq_len), lambda s, j: (j, 0)),
        _layer_spec((GROUP_W, GROUP_W), l),
    ]
    args = [proj, cs, dft, fn_w]
    aliases = {}
    if prev is not None:
        in_specs.append(pl.BlockSpec(memory_space=pl.ANY))
        args.append(prev)
        aliases = {4: 0}
    return pl.pallas_call(
        functools.partial(_fnet_kernel, t_len=seq_len),
        grid=(n_seq, nj),
        in_specs=in_specs,
        out_specs=pl.BlockSpec((FN_TR, GROUP_W), lambda s, j: (rb0 + s * nj + j, 0)),
        out_shape=jax.ShapeDtypeStruct((N_TOK, GROUP_W), F32),
        scratch_shapes=[pltpu.VMEM((2 * seq_len, GROUP_W), BF16)],
        input_output_aliases=aliases,
        compiler_params=pltpu.CompilerParams(vmem_limit_bytes=VMEM_LIMIT),
        name="fnet_mixer",
    )(*args)


GM_TB = 512


def _gmlp_kernel(gu_ref, gv_ref, g_ref, w_ref, bias_ref, hm_ref, gn_ref, o_ref):
    hmb = hm_ref[...]
    for c in range(GM_TB // GM_CHUNK):
        rows = slice(GM_CHUNK * c, GM_CHUNK * (c + 1))
        gu = _gelu(gu_ref[rows, :])
        gv = _rms(_gelu(gv_ref[rows, :]), g_ref[...]).astype(BF16)
        g4 = jnp.concatenate([gv] * GM_HEADS, axis=0) * hmb
        sp = _dot(w_ref[...], g4) + bias_ref[...]
        o_ref[rows, :] = _rms(gu * sp, gn_ref[...])


def _gmlp(proj, gm_norm_g, wcat, bias, hm, gn, l):
    return pl.pallas_call(
        _gmlp_kernel,
        grid=(N_TOK // GM_TB,),
        in_specs=[
            pl.BlockSpec((GM_TB, GROUP_W), lambda i: (i, COL_GU)),
            pl.BlockSpec((GM_TB, GROUP_W), lambda i: (i, COL_GV)),
            _layer_spec((1, GROUP_W), l),
            _layer_spec((GM_CHUNK, GM_HEADS * GM_CHUNK), l),
            _layer_spec((GM_CHUNK, GROUP_W), l),
            _const_spec((GM_HEADS * GM_CHUNK, GROUP_W)),
            pl.BlockSpec((None, 1, GROUP_W), lambda i: (l, 0, 3)),
        ],
        out_specs=pl.BlockSpec((GM_TB, GROUP_W), lambda i: (i, 0)),
        out_shape=jax.ShapeDtypeStruct((N_TOK, GROUP_W), F32),
        compiler_params=pltpu.CompilerParams(vmem_limit_bytes=VMEM_LIMIT),
        name="gmlp_mixer",
    )(proj, proj, gm_norm_g, wcat, bias, hm, gn)


def _outproj_kernel(*refs):
    (*x_refs, xa_ref, hgate_ref, ylo_ref, yhi_ref, of_ref, ob_ref, yc_ref, od_ref, mod_ref, d_ref,
     wglu_ref, gn_ref, bdm_ref, wout_ref, n2_ref, x1_ref, h2_ref, wob_ref) = refs

    @pl.when(pl.program_id(0) == 0)
    def _():
        wob_ref[...] = wout_ref[...].astype(BF16)

    gn = gn_ref[...]
    ys = jnp.concatenate([ylo_ref[...], yhi_ref[...]], axis=1)
    y5 = _gelu(ys + d_ref[...] * xa_ref[...])
    glu = jax.nn.sigmoid(_dot(y5.astype(BF16), wglu_ref[...].astype(BF16)))
    out_a = _rms(y5 * glu, gn[:, 0:GROUP_W])
    o = of_ref[...] + ob_ref[...]
    o2 = o * o
    o2h = o2.astype(BF16)
    o2l = (o2 - o2h.astype(F32)).astype(BF16)
    ms = _dot(o2h, bdm_ref[...]) + _dot(o2l, bdm_ref[...])
    hg = hgate_ref[...]
    out_b = o * lax.rsqrt(ms + EPS) * gn[:, GROUP_W:2 * GROUP_W] * (hg * jax.nn.sigmoid(hg))
    out_c = _rms(yc_ref[...], gn[:, 2 * GROUP_W:3 * GROUP_W])
    out_d = od_ref[...]
    m = None
    for k, part in enumerate((out_a, out_b, out_c, out_d)):
        t = _dot(part.astype(BF16), wob_ref[GROUP_W * k:GROUP_W * (k + 1), :])
        m = t if m is None else m + t
    mod = mod_ref[...]
    g1 = mod[:, 2 * D_MODEL:3 * D_MODEL]
    sh2 = mod[:, 3 * D_MODEL:4 * D_MODEL]
    sc2 = mod[:, 4 * D_MODEL:5 * D_MODEL]
    x1 = _x_value(x_refs) + g1 * m
    x1_ref[...] = x1
    h2_ref[...] = (_rms(x1, n2_ref[...]) * (1.0 + sc2) + sh2).astype(BF16)


def _outproj(xs, proj, ylo, yhi, of, ob, yc, od, mods, s5_d, wglu, gn, bdm, wout, norm2_g, l):
    def tok(width, col=0):
        return pl.BlockSpec((TB, width), lambda i: (i, col))

    return pl.pallas_call(
        _outproj_kernel,
        grid=(N_TOK // TB,),
        in_specs=_x_specs(len(xs) == 2) + [
            tok(GROUP_W, COL_XA), tok(GROUP_W, COL_HGATE), tok(LANES), tok(LANES),
            tok(GROUP_W), tok(GROUP_W), tok(GROUP_W), tok(GROUP_W),
            _mod_spec(l),
            _layer_spec((1, GROUP_W), l), _layer_spec((GROUP_W, GROUP_W), l), _layer_spec((1, D_MODEL), l),
            _const_spec((GROUP_W, GROUP_W)), _layer_spec((D_MODEL, D_MODEL), l, single=True),
            _layer_spec((1, D_MODEL), l),
        ],
        out_specs=(tok(D_MODEL), tok(D_MODEL)),
        out_shape=(jax.ShapeDtypeStruct((N_TOK, D_MODEL), F32),
                   jax.ShapeDtypeStruct((N_TOK, D_MODEL), BF16)),
        scratch_shapes=[pltpu.VMEM((D_MODEL, D_MODEL), BF16)],
        compiler_params=pltpu.CompilerParams(vmem_limit_bytes=VMEM_LIMIT),
        name="out_proj",
    )(*xs, proj, proj, ylo, yhi, of, ob, yc, od, mods, s5_d, wglu, gn, bdm, wout, norm2_g)


FF_SEG = GRID_W
FF_PAD = 8
FF_STRIDE = FF_SEG + FF_PAD
FF_NSEG = TB // FF_SEG
FF_ZROWS = FF_PAD + FF_NSEG * FF_STRIDE
GELU_C0 = 0.7978845608028654
GELU_C1 = GELU_C0 * 0.044715


def _ffn_kernel(x1_ref, h2_ref, mod_ref, wup_ref, cw_ref, cb_ref, wdn_ref, fg_ref, *rest, final):
    *o_refs, z_ref, hid_ref = rest
    i = pl.program_id(0)
    joined = (i < NB_CTX).astype(F32)
    n_tiles = D_FF // FF_TILE

    def up(j):
        slot = j % 2
        lo = FF_TILE * j
        h2 = h2_ref[...]
        for c, col in enumerate((lo, D_FF + lo)):
            z = _dot(h2, wup_ref[:, col:col + FF_TILE])
            lanes = slice(FF_TILE * c, FF_TILE * (c + 1))
            for k in range(FF_NSEG):
                r0 = FF_PAD + FF_STRIDE * k
                z_ref[slot, r0:r0 + FF_SEG, lanes] = z[FF_SEG * k:FF_SEG * (k + 1)]
                before = z[FF_SEG * k - 1:FF_SEG * k] * joined if k > 0 else jnp.zeros((1, FF_TILE), F32)
                after = (z[FF_SEG * (k + 1):FF_SEG * (k + 1) + 1] * joined if k < FF_NSEG - 1
                         else jnp.zeros((1, FF_TILE), F32))
                z_ref[slot, r0 - 1:r0, lanes] = before
                z_ref[slot, r0 + FF_SEG:r0 + FF_SEG + 1, lanes] = after

    def gate(j):
        slot = j % 2
        lo = FF_TILE * j
        w = jnp.concatenate([cw_ref[:, lo:lo + FF_TILE], cw_ref[:, D_FF + lo:D_FF + lo + FF_TILE]], axis=1)
        b = jnp.concatenate([cb_ref[:, lo:lo + FF_TILE], cb_ref[:, D_FF + lo:D_FF + lo + FF_TILE]], axis=1)
        for k in range(FF_NSEG):
            r0 = FF_PAD + FF_STRIDE * k
            zc = (b + z_ref[slot, r0 - 1:r0 - 1 + FF_SEG, :] * w[0:1] + z_ref[slot, r0:r0 + FF_SEG, :] * w[1:2]
                  + z_ref[slot, r0 + 1:r0 + 1 + FF_SEG, :] * w[2:3])
            a = zc[:, :FF_TILE]
            t = jnp.tanh(a * (GELU_C0 + GELU_C1 * (a * a)))
            hid = a * (0.5 + 0.5 * t) * zc[:, FF_TILE:]
            hid_ref[FF_SEG * k:FF_SEG * (k + 1), lo:lo + FF_TILE] = hid.astype(BF16)

    up(0)
    for j in range(n_tiles):
        if j + 1 < n_tiles:
            up(j + 1)
        gate(j)
    acc = _dot(hid_ref[...], wdn_ref[...])
    g2 = mod_ref[...][:, 5 * D_MODEL:6 * D_MODEL]
    x2 = x1_ref[...] + g2 * acc
    if final:
        y = _rms(x2, fg_ref[...])
        yp_ref, ys_ref = o_refs

        @pl.when(i < NB_CTX)
        def _():
            yp_ref[...] = y

        @pl.when(i >= NB_CTX)
        def _():
            ys_ref[...] = y
    else:
        o_refs[0][...] = x2


def _ffn(x1, h2, mods, wup_bf, conv_w, conv_b, wdn_bf, final_g, l, *, final):
    if final:
        out_specs = tuple(_x_specs(True))
        out_shape = (jax.ShapeDtypeStruct((N_CTX, D_MODEL), F32), jax.ShapeDtypeStruct((N_LAT, D_MODEL), F32))
    else:
        out_specs = pl.BlockSpec((TB, D_MODEL), lambda i: (i, 0))
        out_shape = jax.ShapeDtypeStruct((N_TOK, D_MODEL), F32)
    return pl.pallas_call(
        functools.partial(_ffn_kernel, final=final),
        grid=(N_TOK // TB,),
        in_specs=[
            pl.BlockSpec((TB, D_MODEL), lambda i: (i, 0)),
            pl.BlockSpec((TB, D_MODEL), lambda i: (i, 0)),
            _mod_spec(l),
            _layer_spec((D_MODEL, 2 * D_FF), l, single=True), _layer_spec((3, 2 * D_FF), l),
            _layer_spec((1, 2 * D_FF), l), _layer_spec((D_FF, D_MODEL), l, single=True),
            _const_spec((1, D_MODEL)),
        ],
        out_specs=out_specs,
        out_shape=out_shape,
        scratch_shapes=[pltpu.VMEM((2, FF_ZROWS, 2 * FF_TILE), F32), pltpu.VMEM((TB, D_FF), BF16)],
        compiler_params=pltpu.CompilerParams(vmem_limit_bytes=VMEM_LIMIT),
        name="conv_ffn",
    )(x1, h2, mods, wup_bf, conv_w, conv_b, wdn_bf, final_g.reshape(1, D_MODEL))


def kernel(x_prompt, x_sample, state_s5_re, state_s5_im, state_hgrn, c, c_ctx, w_ada, b_ada, norm1_g,
           norm2_g, w_in, s5_lam_re, s5_lam_im, s5_log_dt, s5_b_re, s5_b_im, s5_c_re, s5_c_im, s5_d,
           s5_w_glu, hg_lb_logits, fn_w, gm_norm_g, gm_ws, gm_bs, grp_norm_g, w_out, ffn_w_up,
           ffn_conv_w, ffn_conv_b, ffn_w_down, final_norm_g):
    xs = (x_prompt.reshape(N_CTX, D_MODEL), x_sample.reshape(N_LAT, D_MODEL))
    cvecs = jnp.concatenate([c_ctx[None], c, jnp.zeros((8 - 1 - DEC_BATCH, D_MODEL), F32)], axis=0)
    mods = _ada(cvecs, w_ada, b_ada).reshape(DEPTH, 8, 1, N_MOD * D_MODEL)

    lb_p = jax.nn.softmax(hg_lb_logits.astype(F32), axis=0)
    lbs = jnp.maximum(jnp.cumsum(lb_p, axis=0) - lb_p[0], 0.0)
    lbps = jnp.stack([jnp.log(lbs), jnp.log1p(-lbs), 1.0 - lbs], axis=2)

    hg_consts_np = _hgrn_consts()
    hg_consts = (jnp.asarray(hg_consts_np[0], BF16), jnp.asarray(hg_consts_np[1], F32),
                 jnp.asarray(hg_consts_np[2], F32))
    cs_np, dft_ctx_np = _dft_consts(SEQ)
    _, dft_lat_np = _dft_consts(DEC_SEQ)
    cs = jnp.asarray(cs_np, F32).astype(BF16)
    dft_ctx = jnp.asarray(dft_ctx_np, F32).astype(BF16)
    dft_lat = jnp.asarray(dft_lat_np, F32).astype(BF16)
    gm_hm = jnp.asarray(np.kron(np.eye(GM_HEADS), np.ones((GM_CHUNK, GROUP_W // GM_HEADS))), BF16)
    bdm = jnp.asarray(np.kron(np.eye(HG_HEADS), np.ones((HG_DK, HG_DK))) / HG_DK, BF16)

    tables = _s5_tables(s5_lam_re, s5_lam_im, s5_log_dt, s5_b_re, s5_b_im, s5_c_re, s5_c_im)
    s5_m, s5_f, s5_e = _s5_prep(tables, jnp.asarray(_s5_shift_consts(), BF16))
    s5_al = tables[-1]
    s5_h0 = jnp.stack([state_s5_re, state_s5_im]).astype(F32).transpose(2, 3, 0, 1, 4, 5)
    s5_h0 = s5_h0.reshape(DEPTH, 2, 2, DEC_BATCH, S5_SW)
    hg_s0 = jnp.einsum('bldhkv,hg->lbdhvgk', state_hgrn.astype(F32), jnp.eye(HG_HEADS, dtype=F32))
    hg_s0 = hg_s0.reshape(DEPTH, DEC_BATCH, 2, HG_W, HG_W)

    def rows(a):
        return a.reshape(DEPTH, 1, a.shape[-1])

    norm1_r, norm2_r, gn_r, s5_d_r, gmn_r, cb_r = (rows(a) for a in (
        norm1_g, norm2_g, grp_norm_g, s5_d, gm_norm_g, ffn_conv_b))
    gm_w = gm_ws.transpose(0, 2, 1, 3).reshape(DEPTH, GM_CHUNK, GM_HEADS * GM_CHUNK).astype(BF16)
    gm_b = jnp.repeat(gm_bs.transpose(0, 2, 1), GROUP_W // GM_HEADS, axis=2)
    wup_bf = ffn_w_up.astype(BF16)
    wdn_bf = ffn_w_down.astype(BF16)

    new_re, new_im, new_hg = [], [], []
    for l in range(DEPTH):
        proj = _inproj(xs, mods, norm1_r, w_in, l)

        ylo, yhi, hfin = _s5(proj, s5_m, s5_f, s5_e, s5_al, s5_h0, l)
        hfin = hfin.reshape(2, 2, BATCH, S5_G, S5_N).transpose(1, 2, 0, 3, 4)
        new_re.append(hfin[0])
        new_im.append(hfin[1])

        of, ob, sfin = _hgrn(proj, lbps, hg_consts, hg_s0, l)
        new_hg.append(sfin)

        yc = _fnet(proj, cs, dft_ctx, fn_w, None, l, n_seq=BATCH, seq_len=SEQ, tok0=0)
        yc = _fnet(proj, cs, dft_lat, fn_w, yc, l, n_seq=DEC_BATCH, seq_len=DEC_SEQ, tok0=N_CTX)

        od = _gmlp(proj, gmn_r, gm_w, gm_b, gm_hm, gn_r, l)

        x1, h2 = _outproj(xs, proj, ylo, yhi, of, ob, yc, od, mods, s5_d_r, s5_w_glu, gn_r, bdm, w_out,
                          norm2_r, l)
        res = _ffn(x1, h2, mods, wup_bf, ffn_conv_w, cb_r, wdn_bf, final_norm_g, l, final=(l == DEPTH - 1))
        xs = (res,)

    y_prompt = res[0].reshape(BATCH, SEQ, D_MODEL)
    y_sample = res[1].reshape(DEC_BATCH, DEC_SEQ, D_MODEL)
    return (y_prompt, y_sample, jnp.stack(new_re, axis=1), jnp.stack(new_im, axis=1),
            jnp.stack(new_hg, axis=1))
```

```python
import functools
import math

import numpy as np
import jax
import jax.numpy as jnp
from jax import lax
from jax.experimental import pallas as pl
from jax.experimental.pallas import tpu as pltpu

D_MODEL = 1024
BATCH = 16
SEQ = 256
DEPTH = 2
DEC_BATCH = 2
DEC_SEQ = 2048
GRID_W = 64
GROUP_W = 256
S5_P = 16
S5_G = 16
S5_N = 64
HG_HEADS = 4
HG_DK = 64
GM_HEADS = 4
GM_CHUNK = 128
D_FF = 2816
N_MOD = 6
D_IN = 9 * GROUP_W
EPS = 1e-6
LAM_RE_MAX = -1e-4

N_CTX = BATCH * SEQ
N_LAT = DEC_BATCH * DEC_SEQ
N_TOK = N_CTX + N_LAT
TB = 256
NB_CTX = N_CTX // TB
NB_LAT_SEQ = DEC_SEQ // TB
S5_L = 16
HG_L = 64
FF_TILE = 256
LANES = 128
VMEM_LIMIT = 56 * 1024 * 1024

F32 = jnp.float32
BF16 = jnp.bfloat16

COL_XA, COL_HQ, COL_HF_FWD, COL_HF_BWD, COL_HI, COL_HGATE, COL_XC, COL_GU, COL_GV = range(9)


def _mod_row(i):
    return jnp.where(i < NB_CTX, 0, 1 + (i - NB_CTX) // NB_LAT_SEQ)


def _gelu(x):
    return 0.5 * x * (1.0 + jnp.tanh(0.7978845608028654 * (x + 0.044715 * (x * x * x))))


def _rms(x, g):
    return x * lax.rsqrt(jnp.mean(x * x, axis=-1, keepdims=True) + EPS) * g


def _dot(a, b):
    return jnp.dot(a, b, preferred_element_type=F32)


def _dot_nt(a, b, precision=None):
    return lax.dot_general(a, b, (((1,), (1,)), ((), ())), precision=precision, preferred_element_type=F32)


def _dot_tn(a, b):
    return lax.dot_general(a, b, (((0,), (0,)), ((), ())), preferred_element_type=F32)


def _const_spec(shape, single=False):
    kw = {"pipeline_mode": pl.Buffered(1)} if single else {}
    return pl.BlockSpec(shape, lambda *_: (0,) * len(shape), **kw)


def _layer_spec(shape, l, single=False):
    kw = {"pipeline_mode": pl.Buffered(1)} if single else {}
    return pl.BlockSpec((None,) + tuple(shape), lambda *_: (l,) + (0,) * len(shape), **kw)


def _mod_spec(l):
    return pl.BlockSpec((None, None, 1, N_MOD * D_MODEL), lambda i: (l, _mod_row(i), 0, 0))


def _x_specs(split):
    if split:
        return [pl.BlockSpec((TB, D_MODEL), lambda i: (jnp.minimum(i, NB_CTX - 1), 0)),
                pl.BlockSpec((TB, D_MODEL), lambda i: (jnp.maximum(i - NB_CTX, 0), 0))]
    return [pl.BlockSpec((TB, D_MODEL), lambda i: (i, 0))]


def _x_value(x_refs):
    if len(x_refs) == 1:
        return x_refs[0][...]
    return jnp.where(pl.program_id(0) < NB_CTX, x_refs[0][...], x_refs[1][...])


ADA_TN = 1536


def _ada_kernel(c_ref, w_ref, b_ref, o_ref):
    cv = c_ref[...]
    s = cv * jax.nn.sigmoid(cv)
    o_ref[0] = _dot(s.astype(BF16), w_ref[0].astype(BF16)) + b_ref[0]


def _ada(cvecs, w_ada, b_ada):
    n = N_MOD * D_MODEL
    return pl.pallas_call(
        _ada_kernel,
        grid=(DEPTH, n // ADA_TN),
        in_specs=[
            pl.BlockSpec((8, D_MODEL), lambda l, j: (0, 0)),
            pl.BlockSpec((1, D_MODEL, ADA_TN), lambda l, j: (l, 0, j)),
            pl.BlockSpec((1, 1, ADA_TN), lambda l, j: (l, 0, j)),
        ],
        out_specs=pl.BlockSpec((1, 8, ADA_TN), lambda l, j: (l, 0, j)),
        out_shape=jax.ShapeDtypeStruct((DEPTH, 8, n), F32),
        compiler_params=pltpu.CompilerParams(vmem_limit_bytes=VMEM_LIMIT),
        name="ada_mod",
    )(cvecs, w_ada, b_ada.reshape(DEPTH, 1, n))


def _inproj_kernel(*refs):
    *x_refs, mod_ref, g_ref, w_ref, o_ref, wb_ref = refs

    @pl.when(pl.program_id(0) == 0)
    def _():
        wb_ref[...] = w_ref[...].astype(BF16)

    mod = mod_ref[...]
    sh = mod[:, 0:D_MODEL]
    sc = mod[:, D_MODEL:2 * D_MODEL]
    h = _rms(_x_value(x_refs), g_ref[...]) * (1.0 + sc) + sh
    o_ref[...] = _dot(h.astype(BF16), wb_ref[...])


def _inproj(xs, mods, norm_g, w_in, l):
    return pl.pallas_call(
        _inproj_kernel,
        grid=(N_TOK // TB,),
        in_specs=_x_specs(len(xs) == 2) + [
            _mod_spec(l),
            _layer_spec((1, D_MODEL), l),
            _layer_spec((D_MODEL, D_IN), l, single=True),
        ],
        out_specs=pl.BlockSpec((TB, D_IN), lambda i: (i, 0)),
        out_shape=jax.ShapeDtypeStruct((N_TOK, D_IN), F32),
        scratch_shapes=[pltpu.VMEM((D_MODEL, D_IN), BF16)],
        compiler_params=pltpu.CompilerParams(vmem_limit_bytes=VMEM_LIMIT),
        name="in_proj",
    )(*xs, mods, norm_g, w_in)


S5_NPAIR = S5_G // 2
S5_CW = S5_L * S5_P
S5_SW = S5_G * S5_N
S5_STEP_ROWS = N_CTX // S5_L
S5_CTX_NC = SEQ // S5_L
S5_LAT_NC = DEC_SEQ // S5_L


def _s5_tables(lam_re, lam_im, log_dt, b_re, b_im, c_re, c_im):
    lr = jnp.minimum(lam_re.astype(F32), LAM_RE_MAX)
    li = lam_im.astype(F32)
    dt = jnp.exp(log_dt.astype(F32))[..., None]
    mag = jnp.exp(lr * dt)
    ang = li * dt
    ab_re = mag * jnp.cos(ang)
    ab_im = mag * jnp.sin(ang)
    den = lr * lr + li * li
    xr = ab_re - 1.0
    z_re = (xr * lr + ab_im * li) / den
    z_im = (ab_im * lr - xr * li) / den
    bb_re = z_re[..., None] * b_re - z_im[..., None] * b_im
    bb_im = z_re[..., None] * b_im + z_im[..., None] * b_re
    tau = jnp.arange(S5_L + 1, dtype=F32)[:, None, None, None, None]
    pm = jnp.exp(lr * dt * tau)
    pa = li * dt * tau
    pw_re = pm * jnp.cos(pa)
    pw_im = pm * jnp.sin(pa)
    eye2 = jnp.eye(2, dtype=F32)

    def pw_pairs(a):
        a = a.transpose(1, 2, 3, 0, 4).reshape(DEPTH, 2, S5_NPAIR, 2, S5_L + 1, S5_N)
        return a.transpose(0, 1, 2, 4, 3, 5).reshape(DEPTH, 2, S5_NPAIR, S5_L + 1, 2 * S5_N)

    def mat_pairs(a):
        a = a.reshape(DEPTH, 2, S5_NPAIR, 2, S5_P, S5_N)
        return jnp.einsum('ldjaqn,ab->ldjaqbn', a, eye2).reshape(DEPTH, 2, S5_NPAIR, 2, S5_P, 2 * S5_N)

    al = jnp.stack([pw_re[S5_L], pw_im[S5_L]], axis=2).reshape(DEPTH, 2, 2, 1, S5_SW)
    return (pw_pairs(pw_re), pw_pairs(pw_im),
            mat_pairs(bb_re.transpose(0, 1, 2, 4, 3)), mat_pairs(bb_im.transpose(0, 1, 2, 4, 3)),
            mat_pairs(c_re.astype(F32)), mat_pairs(c_im.astype(F32)), al)


def _s5_shift_consts():
    n = S5_CW
    r = np.arange(n)[:, None]
    c = np.arange(n)[None, :]
    fwd = [(c == r + S5_P * s) for s in range(S5_L)]
    bwd = [(c == r - S5_P * (S5_L - 1 - s)) for s in range(S5_L)]
    return np.stack([np.concatenate(fwd, axis=1), np.concatenate(bwd, axis=1)]).astype(np.float32)


def _s5_prep_kernel(pwr_ref, pwi_ref, br_ref, bi_ref, cr_ref, ci_ref, scat_ref, m_ref, f_ref, e_ref, k_ref):
    L = S5_L
    hp = lax.Precision.HIGHEST
    sw = 2 * S5_N

    def pair_body(j, carry):
        for d in range(2):
            pwr = pwr_ref[0, d, j]
            pwi = pwi_ref[0, d, j]
            for gl in range(2):
                br, bi = br_ref[0, d, j, gl], bi_ref[0, d, j, gl]
                cr, ci = cr_ref[0, d, j, gl], ci_ref[0, d, j, gl]
                ca_r, ca_i, f_r, f_i, e_r, e_i = [], [], [], [], [], []
                for t in range(L):
                    kk = t if d == 0 else L - 1 - t
                    pr, pi = pwr[kk:kk + 1], pwi[kk:kk + 1]
                    ca_r.append(cr * pr - ci * pi)
                    ca_i.append(cr * pi + ci * pr)
                    kf = L - 1 - t if d == 0 else t
                    pr, pi = pwr[kf:kf + 1], pwi[kf:kf + 1]
                    f_r.append(br * pr - bi * pi)
                    f_i.append(br * pi + bi * pr)
                    ke = t + 1 if d == 0 else L - t
                    pr, pi = pwr[ke:ke + 1], pwi[ke:ke + 1]
                    e_r.append(cr * pr - ci * pi)
                    e_i.append(-(cr * pi + ci * pr))
                cat = lambda xs: jnp.concatenate(xs, axis=0)
                k = _dot_nt(br, cat(ca_r), hp) - _dot_nt(bi, cat(ca_i), hp)
                g16 = pl.multiple_of((2 * j + gl) * S5_P, S5_P)
                k_ref[d, pl.ds(g16, S5_P), :] = k
                rows = slice(S5_CW * gl, S5_CW * (gl + 1))
                f_ref[0, d, j, rows, 0:sw] = cat(f_r).astype(BF16)
                f_ref[0, d, j, rows, sw:2 * sw] = cat(f_i).astype(BF16)
                e_ref[0, d, 0, j, rows, :] = cat(e_r).astype(BF16)
                e_ref[0, d, 1, j, rows, :] = cat(e_i).astype(BF16)
        return carry

    lax.fori_loop(0, S5_NPAIR, pair_body, 0)
    kf = k_ref[0].astype(BF16)
    kb = k_ref[1].astype(BF16)
    for s in range(L):
        cols = slice(S5_CW * s, S5_CW * (s + 1))
        res = _dot(kf, scat_ref[0, :, cols]) + _dot(kb, scat_ref[1, :, cols])
        for g in range(S5_G):
            m_ref[0, g, S5_P * s:S5_P * (s + 1), :] = res[S5_P * g:S5_P * (g + 1), :].astype(BF16)


def _s5_prep(tables, scat):
    pwr, pwi, br, bi, cr, ci, _ = tables

    def lspec(shape):
        return pl.BlockSpec((1,) + shape, lambda l: (l,) + (0,) * len(shape))

    pw_shape = (2, S5_NPAIR, S5_L + 1, 2 * S5_N)
    mat_shape = (2, S5_NPAIR, 2, S5_P, 2 * S5_N)
    return pl.pallas_call(
        _s5_prep_kernel,
        grid=(DEPTH,),
        in_specs=[lspec(pw_shape), lspec(pw_shape), lspec(mat_shape), lspec(mat_shape), lspec(mat_shape),
                  lspec(mat_shape), _const_spec(scat.shape)],
        out_specs=(lspec((S5_G, S5_CW, S5_CW)), lspec((2, S5_NPAIR, 2 * S5_CW, 4 * S5_N)),
                   lspec((2, 2, S5_NPAIR, 2 * S5_CW, 2 * S5_N))),
        out_shape=(jax.ShapeDtypeStruct((DEPTH, S5_G, S5_CW, S5_CW), BF16),
                   jax.ShapeDtypeStruct((DEPTH, 2, S5_NPAIR, 2 * S5_CW, 4 * S5_N), BF16),
                   jax.ShapeDtypeStruct((DEPTH, 2, 2, S5_NPAIR, 2 * S5_CW, 2 * S5_N), BF16)),
        scratch_shapes=[pltpu.VMEM((2, S5_G * S5_P, S5_CW), F32)],
        compiler_params=pltpu.CompilerParams(vmem_limit_bytes=VMEM_LIMIT),
        name="s5_prep",
    )(pwr, pwi, br, bi, cr, ci, scat)


def _s5_kernel(xlo_ref, xhi_ref, m_ref, f_ref, e_ref, al_ref, h0_ref, ylo_ref, yhi_ref, hfin_ref,
               uy_ref, zh_ref):
    R = S5_STEP_ROWS
    pw = 2 * S5_CW
    sw = 2 * S5_N
    gph = S5_G // 2
    step = pl.program_id(0)

    for half, x_ref in enumerate((xlo_ref, xhi_ref)):
        for t in range(S5_L):
            xt = x_ref[pl.ds(t, R, stride=S5_L), :]
            for gl in range(gph):
                lo = S5_CW * (gph * half + gl) + S5_P * t
                uy_ref[:, lo:lo + S5_P] = xt[:, S5_P * gl:S5_P * (gl + 1)]

    for j in range(S5_NPAIR):
        ub = uy_ref[:, pw * j:pw * (j + 1)].astype(BF16)
        for d in range(2):
            o = _dot(ub, f_ref[d, j])
            zh_ref[d, 0, :, sw * j:sw * (j + 1)] = o[:, :sw]
            zh_ref[d, 1, :, sw * j:sw * (j + 1)] = o[:, sw:]

    def scan_group(seq_rows, nc, init, fin_rows):
        chains = [(d, r0) for d in range(2) for r0 in seq_rows]

        def body(c, carry):
            out = []
            for k, (d, r0) in enumerate(chains):
                hr, hi = carry[2 * k], carry[2 * k + 1]
                r = r0 + (c if d == 0 else nc - 1 - c)
                zr = zh_ref[d, 0, pl.ds(r, 1), :]
                zi = zh_ref[d, 1, pl.ds(r, 1), :]
                zh_ref[d, 0, pl.ds(r, 1), :] = hr
                zh_ref[d, 1, pl.ds(r, 1), :] = hi
                ar = al_ref[d, 0]
                ai = al_ref[d, 1]
                out.append(ar * hr - ai * hi + zr)
                out.append(ar * hi + ai * hr + zi)
            return tuple(out)

        fin = lax.fori_loop(0, nc, body, tuple(init))
        if fin_rows is not None:
            for k, (d, _) in enumerate(chains):
                s = fin_rows[k % len(seq_rows)]
                hfin_ref[d, 0, s:s + 1, :] = fin[2 * k]
                hfin_ref[d, 1, s:s + 1, :] = fin[2 * k + 1]

    @pl.when(step == 0)
    def _():
        zero = jnp.zeros((1, S5_SW), F32)
        for s0 in range(0, BATCH, 4):
            seqs = list(range(s0, s0 + 4))
            scan_group([s * S5_CTX_NC for s in seqs], S5_CTX_NC, [zero] * 16, seqs)

    @pl.when(step == 1)
    def _():
        init = []
        for d in range(2):
            for s in range(DEC_BATCH):
                init.append(h0_ref[d, 0, s:s + 1, :])
                init.append(h0_ref[d, 1, s:s + 1, :])
        scan_group([s * S5_LAT_NC for s in range(DEC_BATCH)], S5_LAT_NC, init, None)

    for j in range(S5_NPAIR):
        acc = None
        for d in range(2):
            for ri in range(2):
                hb = zh_ref[d, ri, :, sw * j:sw * (j + 1)].astype(BF16)
                t = _dot_nt(hb, e_ref[d, ri, j])
                acc = t if acc is None else acc + t
        for gl in range(2):
            g = 2 * j + gl
            ug = uy_ref[:, S5_CW * g:S5_CW * (g + 1)].astype(BF16)
            uy_ref[:, S5_CW * g:S5_CW * (g + 1)] = acc[:, S5_CW * gl:S5_CW * (gl + 1)] + _dot(ug, m_ref[g])

    for half, y_ref in enumerate((ylo_ref, yhi_ref)):
        for t in range(S5_L):
            pieces = []
            for gl in range(gph):
                lo = S5_CW * (gph * half + gl) + S5_P * t
                pieces.append(uy_ref[:, lo:lo + S5_P])
            y_ref[pl.ds(t, R, stride=S5_L), :] = jnp.concatenate(pieces, axis=1)


def _s5(proj, m, f2, e2, al, h0, l):
    half = pl.BlockSpec((N_CTX, LANES), lambda s: (s, 0))
    return pl.pallas_call(
        _s5_kernel,
        grid=(2,),
        in_specs=[pl.BlockSpec((N_CTX, LANES), lambda s: (s, 0)),
                  pl.BlockSpec((N_CTX, LANES), lambda s: (s, 1)),
                  _layer_spec(m.shape[1:], l, single=True), _layer_spec(f2.shape[1:], l, single=True),
                  _layer_spec(e2.shape[1:], l, single=True), _layer_spec(al.shape[1:], l),
                  _layer_spec(h0.shape[1:], l)],
        out_specs=(half, half, _const_spec((2, 2, BATCH, S5_SW))),
        out_shape=(jax.ShapeDtypeStruct((N_TOK, LANES), F32),
                   jax.ShapeDtypeStruct((N_TOK, LANES), F32),
                   jax.ShapeDtypeStruct((2, 2, BATCH, S5_SW), F32)),
        scratch_shapes=[pltpu.VMEM((S5_STEP_ROWS, S5_G * S5_CW), F32),
                        pltpu.VMEM((2, 2, S5_STEP_ROWS, S5_SW), F32)],
        compiler_params=pltpu.CompilerParams(vmem_limit_bytes=VMEM_LIMIT),
        name="s5_mixer",
    )(proj, proj, m, f2, e2, al, h0)


HG_NLEV = int(math.log2(HG_L))
HG_W = HG_HEADS * HG_DK
HG_CPS = 4
HG_BLK = HG_CPS * HG_L
HG_NB_CTX = N_CTX // HG_BLK
HG_NC_CTX = SEQ // HG_BLK
HG_NC_LAT = DEC_SEQ // HG_BLK
assert HG_L == HG_DK


def _hgrn_consts():
    L = HG_L
    w = np.zeros((HG_NLEV + 2, L, L), np.float32)
    mask = np.zeros((HG_NLEV + 1, L, L), np.float32)
    for lev in range(HG_NLEV):
        blk = L >> lev
        half = blk // 2
        for t in range(L):
            p, o = divmod(t, blk)
            bd = p * blk + half - 1
            if o >= half:
                w[lev, t, bd + 1:t + 1] = 1.0
            else:
                w[lev, t, t + 1:bd + 1] = 1.0
        jj, ii = np.meshgrid(np.arange(L), np.arange(L), indexing='ij')
        mask[lev] = ((jj // blk == ii // blk) & (jj % blk >= half) & (ii % blk < half)).astype(np.float32)
    mask[HG_NLEV] = np.eye(L, dtype=np.float32)
    for t in range(L):
        w[HG_NLEV, t, :t + 1] = 1.0
        w[HG_NLEV + 1, t, t + 1:] = 1.0
    out = []
    for wd, md in ((w, mask), (w[:, ::-1, ::-1], mask[:, ::-1, ::-1])):
        wflat = wd.reshape((HG_NLEV + 2) * L, L)
        out.append((np.concatenate([wflat] * 3, axis=1), np.tile(md, (1, 1, HG_HEADS))))
    wcat = np.stack([out[0][0], out[1][0]])
    mask4 = np.stack([out[0][1], out[1][1]])
    hm = np.kron(np.eye(HG_HEADS, dtype=np.float32), np.ones((HG_DK, HG_DK), np.float32))
    return wcat, mask4, hm


def _hg_pos(i):
    is_ctx = i < HG_NB_CTX
    c = jnp.where(is_ctx, i % HG_NC_CTX, (i - HG_NB_CTX) % HG_NC_LAT)
    nc = jnp.where(is_ctx, HG_NC_CTX, HG_NC_LAT)
    return is_ctx, c, nc


def _hg_bwd_blk(i):
    _, c, nc = _hg_pos(i)
    return i + nc - 1 - 2 * c


def _hg_local(d, q, z, v, lbp, wcat_ref, mask_ref, hm):
    L = HG_L
    az = jnp.abs(z)
    sp = jnp.log1p(jnp.exp(-az))
    ls = jnp.minimum(z, 0.0) - sp
    kk = lbp[2:3] * jnp.exp(jnp.minimum(-z, 0.0) - sp)
    a = lbp[0:1]
    b = lbp[1:2] + ls
    logf = jnp.maximum(a, b) + jnp.log1p(jnp.exp(-jnp.abs(a - b)))
    p0 = logf.astype(BF16)
    r0 = logf - p0.astype(F32)
    p1 = r0.astype(BF16)
    p2 = (r0 - p1.astype(F32)).astype(BF16)
    ex = jnp.exp(_dot(wcat_ref[d], jnp.concatenate([p0, p1, p2], axis=0)))
    hmb = hm.astype(BF16)

    def bd4(x):
        return jnp.concatenate([x] * HG_HEADS, axis=0) * hmb

    sc = None
    for lev in range(HG_NLEV + 1):
        if lev < HG_NLEV:
            al = ex[L * lev:L * (lev + 1)]
            lhs = (q * al).astype(BF16)
            rhs = (kk * al).astype(BF16)
        else:
            lhs = q.astype(BF16)
            rhs = kk.astype(BF16)
        t = _dot_nt(lhs, bd4(rhs)) * mask_ref[d, lev]
        sc = t if sc is None else sc + t
    vb = v.astype(BF16)
    o = _dot(sc.astype(BF16), bd4(vb))
    eq = ex[L * HG_NLEV:L * (HG_NLEV + 1)]
    qe = (q * eq).astype(BF16)
    ke = (kk * ex[L * (HG_NLEV + 1):L * (HG_NLEV + 2)]).astype(BF16)
    g = eq[L - 1:L] if d == 0 else eq[0:1]
    return o, qe, g, _dot_tn(vb, ke) * hm


def _hgrn_kernel(qf, zf, vf, qb, zb, vb, lb_ref, w_ref, mask_ref, hm_ref, s0_ref,
                 of_ref, ob_ref, sfin_ref, st_ref):
    is_ctx, c, nc = _hg_pos(pl.program_id(0))

    @pl.when(c == 0)
    def _():
        st_ref[...] = jnp.where(is_ctx, 0.0, s0_ref[0])

    hm = hm_ref[...]
    for d, (q_ref, z_ref, v_ref, o_ref) in enumerate(((qf, zf, vf, of_ref), (qb, zb, vb, ob_ref))):
        rows = [slice(HG_L * k, HG_L * (k + 1)) for k in range(HG_CPS)]
        loc = [_hg_local(d, q_ref[r, :], z_ref[r, :], v_ref[r, :], lb_ref[d], w_ref, mask_ref, hm)
               for r in rows]
        st = st_ref[d]
        for k in (range(HG_CPS) if d == 0 else reversed(range(HG_CPS))):
            o, qe, g, upd = loc[k]
            o_ref[rows[k], :] = o + _dot_nt(qe, st.astype(BF16))
            st = st * g + upd
        st_ref[d] = st

    @pl.when(jnp.logical_and(is_ctx, c == nc - 1))
    def _():
        for d in range(2):
            s_kv = st_ref[d].T
            for h in range(HG_HEADS):
                sfin_ref[0, d, h] = s_kv[HG_DK * h:HG_DK * (h + 1), HG_DK * h:HG_DK * (h + 1)]


def _hgrn(proj, lbp, consts, s0, l):
    wcat, mask4, hm = consts

    def fwd(col):
        return pl.BlockSpec((HG_BLK, HG_W), lambda i: (i, col))

    def bwd(col):
        return pl.BlockSpec((HG_BLK, HG_W), lambda i: (_hg_bwd_blk(i), col))

    return pl.pallas_call(
        _hgrn_kernel,
        grid=(N_TOK // HG_BLK,),
        in_specs=[fwd(COL_HQ), fwd(COL_HF_FWD), fwd(COL_HI), bwd(COL_HQ), bwd(COL_HF_BWD), bwd(COL_HI),
                  _layer_spec(lbp.shape[1:], l), _const_spec(wcat.shape), _const_spec(mask4.shape),
                  _const_spec(hm.shape),
                  pl.BlockSpec((None, 1, 2, HG_W, HG_W),
                               lambda i: (l, jnp.maximum(i - HG_NB_CTX, 0) // HG_NC_LAT, 0, 0, 0))],
        out_specs=(pl.BlockSpec((HG_BLK, HG_W), lambda i: (i, 0)),
                   pl.BlockSpec((HG_BLK, HG_W), lambda i: (_hg_bwd_blk(i), 0)),
                   pl.BlockSpec((1, 2, HG_HEADS, HG_DK, HG_DK),
                                lambda i: (jnp.minimum(i // HG_NC_CTX, BATCH - 1), 0, 0, 0, 0))),
        out_shape=(jax.ShapeDtypeStruct((N_TOK, HG_W), F32),
                   jax.ShapeDtypeStruct((N_TOK, HG_W), F32),
                   jax.ShapeDtypeStruct((BATCH, 2, HG_HEADS, HG_DK, HG_DK), F32)),
        scratch_shapes=[pltpu.VMEM((2, HG_W, HG_W), F32)],
        compiler_params=pltpu.CompilerParams(vmem_limit_bytes=VMEM_LIMIT),
        name="hgrn_mixer",
    )(proj, proj, proj, proj, proj, proj, lbp, wcat, mask4, hm, s0)


FN_TR = 256


def _dft_consts(t_len):
    n = GROUP_W // 4
    k = np.arange(n)
    ang = 2.0 * np.pi * ((k[:, None] * k[None, :]) % n) / n
    eye = np.eye(4)
    cs = np.concatenate([np.kron(eye, np.cos(ang)), np.kron(eye, np.sin(ang))], axis=1) / math.sqrt(n)
    t = np.arange(t_len)
    angt = 2.0 * np.pi * ((t[:, None] * t[None, :]) % t_len) / t_len
    dft = np.concatenate([np.cos(angt), -np.sin(angt)], axis=1) / math.sqrt(t_len)
    return cs.astype(np.float32), dft.astype(np.float32)


def _fnet_kernel(x_ref, cs_ref, dft_ref, w_ref, *rest, t_len):
    o_ref, r_ref = rest[-2:]

    @pl.when(pl.program_id(1) == 0)
    def _():
        t = _dot(x_ref[...].astype(BF16), cs_ref[...])
        r_ref[0:t_len, :] = t[:, :GROUP_W].astype(BF16)
        r_ref[t_len:2 * t_len, :] = t[:, GROUP_W:].astype(BF16)

    y = _dot(dft_ref[...], r_ref[...])
    o_ref[...] = _dot(y.astype(BF16), w_ref[...].astype(BF16))


def _fnet(proj, cs, dft, fn_w, prev, l, *, n_seq, seq_len, tok0):
    nj = seq_len // FN_TR
    sb0 = tok0 // seq_len
    rb0 = tok0 // FN_TR
    in_specs = [
        pl.BlockSpec((seq_len, GROUP_W), lambda s, j: (sb0 + s, COL_XC)),
        _const_spec((GROUP_W, 2 * GROUP_W)),
        pl.BlockSpec((FN_TR, 2 * seq_len), lambda s, j: (j, 0)),
        _layer_spec((GROUP_W, GROUP_W), l),
    ]
    args = [proj, cs, dft, fn_w]
    aliases = {}
    if prev is not None:
        in_specs.append(pl.BlockSpec(memory_space=pl.ANY))
        args.append(prev)
        aliases = {4: 0}
    return pl.pallas_call(
        functools.partial(_fnet_kernel, t_len=seq_len),
        grid=(n_seq, nj),
        in_specs=in_specs,
        out_specs=pl.BlockSpec((FN_TR, GROUP_W), lambda s, j: (rb0 + s * nj + j, 0)),
        out_shape=jax.ShapeDtypeStruct((N_TOK, GROUP_W), F32),
        scratch_shapes=[pltpu.VMEM((2 * seq_len, GROUP_W), BF16)],
        input_output_aliases=aliases,
        compiler_params=pltpu.CompilerParams(vmem_limit_bytes=VMEM_LIMIT),
        name="fnet_mixer",
    )(*args)


GM_TB = 512


def _gmlp_kernel(gu_ref, gv_ref, g_ref, w_ref, bias_ref, hm_ref, gn_ref, o_ref):
    hmb = hm_ref[...]
    for c in range(GM_TB // GM_CHUNK):
        rows = slice(GM_CHUNK * c, GM_CHUNK * (c + 1))
        gu = _gelu(gu_ref[rows, :])
        gv = _rms(_gelu(gv_ref[rows, :]), g_ref[...]).astype(BF16)
        g4 = jnp.concatenate([gv] * GM_HEADS, axis=0) * hmb
        sp = _dot(w_ref[...], g4) + bias_ref[...]
        o_ref[rows, :] = _rms(gu * sp, gn_ref[...])


def _gmlp(proj, gm_norm_g, wcat, bias, hm, gn, l):
    return pl.pallas_call(
        _gmlp_kernel,
        grid=(N_TOK // GM_TB,),
        in_specs=[
            pl.BlockSpec((GM_TB, GROUP_W), lambda i: (i, COL_GU)),
            pl.BlockSpec((GM_TB, GROUP_W), lambda i: (i, COL_GV)),
            _layer_spec((1, GROUP_W), l),
            _layer_spec((GM_CHUNK, GM_HEADS * GM_CHUNK), l),
            _layer_spec((GM_CHUNK, GROUP_W), l),
            _const_spec((GM_HEADS * GM_CHUNK, GROUP_W)),
            pl.BlockSpec((None, 1, GROUP_W), lambda i: (l, 0, 3)),
        ],
        out_specs=pl.BlockSpec((GM_TB, GROUP_W), lambda i: (i, 0)),
        out_shape=jax.ShapeDtypeStruct((N_TOK, GROUP_W), F32),
        compiler_params=pltpu.CompilerParams(vmem_limit_bytes=VMEM_LIMIT),
        name="gmlp_mixer",
    )(proj, proj, gm_norm_g, wcat, bias, hm, gn)


def _outproj_kernel(*refs):
    (*x_refs, xa_ref, hgate_ref, ylo_ref, yhi_ref, of_ref, ob_ref, yc_ref, od_ref, mod_ref, d_ref,
     wglu_ref, gn_ref, bdm_ref, wout_ref, n2_ref, x1_ref, h2_ref, wob_ref) = refs

    @pl.when(pl.program_id(0) == 0)
    def _():
        wob_ref[...] = wout_ref[...].astype(BF16)

    gn = gn_ref[...]
    ys = jnp.concatenate([ylo_ref[...], yhi_ref[...]], axis=1)
    y5 = _gelu(ys + d_ref[...] * xa_ref[...])
    glu = jax.nn.sigmoid(_dot(y5.astype(BF16), wglu_ref[...].astype(BF16)))
    out_a = _rms(y5 * glu, gn[:, 0:GROUP_W])
    o = of_ref[...] + ob_ref[...]
    o2 = o * o
    o2h = o2.astype(BF16)
    o2l = (o2 - o2h.astype(F32)).astype(BF16)
    ms = _dot(o2h, bdm_ref[...]) + _dot(o2l, bdm_ref[...])
    hg = hgate_ref[...]
    out_b = o * lax.rsqrt(ms + EPS) * gn[:, GROUP_W:2 * GROUP_W] * (hg * jax.nn.sigmoid(hg))
    out_c = _rms(yc_ref[...], gn[:, 2 * GROUP_W:3 * GROUP_W])
    out_d = od_ref[...]
    m = None
    for k, part in enumerate((out_a, out_b, out_c, out_d)):
        t = _dot(part.astype(BF16), wob_ref[GROUP_W * k:GROUP_W * (k + 1), :])
        m = t if m is None else m + t
    mod = mod_ref[...]
    g1 = mod[:, 2 * D_MODEL:3 * D_MODEL]
    sh2 = mod[:, 3 * D_MODEL:4 * D_MODEL]
    sc2 = mod[:, 4 * D_MODEL:5 * D_MODEL]
    x1 = _x_value(x_refs) + g1 * m
    x1_ref[...] = x1
    h2_ref[...] = (_rms(x1, n2_ref[...]) * (1.0 + sc2) + sh2).astype(BF16)


def _outproj(xs, proj, ylo, yhi, of, ob, yc, od, mods, s5_d, wglu, gn, bdm, wout, norm2_g, l):
    def tok(width, col=0):
        return pl.BlockSpec((TB, width), lambda i: (i, col))

    return pl.pallas_call(
        _outproj_kernel,
        grid=(N_TOK // TB,),
        in_specs=_x_specs(len(xs) == 2) + [
            tok(GROUP_W, COL_XA), tok(GROUP_W, COL_HGATE), tok(LANES), tok(LANES),
            tok(GROUP_W), tok(GROUP_W), tok(GROUP_W), tok(GROUP_W),
            _mod_spec(l),
            _layer_spec((1, GROUP_W), l), _layer_spec((GROUP_W, GROUP_W), l), _layer_spec((1, D_MODEL), l),
            _const_spec((GROUP_W, GROUP_W)), _layer_spec((D_MODEL, D_MODEL), l, single=True),
            _layer_spec((1, D_MODEL), l),
        ],
        out_specs=(tok(D_MODEL), tok(D_MODEL)),
        out_shape=(jax.ShapeDtypeStruct((N_TOK, D_MODEL), F32),
                   jax.ShapeDtypeStruct((N_TOK, D_MODEL), BF16)),
        scratch_shapes=[pltpu.VMEM((D_MODEL, D_MODEL), BF16)],
        compiler_params=pltpu.CompilerParams(vmem_limit_bytes=VMEM_LIMIT),
        name="out_proj",
    )(*xs, proj, proj, ylo, yhi, of, ob, yc, od, mods, s5_d, wglu, gn, bdm, wout, norm2_g)


FF_SEG = GRID_W
FF_PAD = 8
FF_STRIDE = FF_SEG + FF_PAD
FF_NSEG = TB // FF_SEG
FF_ZROWS = FF_PAD + FF_NSEG * FF_STRIDE
GELU_C0 = 0.7978845608028654
GELU_C1 = GELU_C0 * 0.044715


def _ffn_kernel(x1_ref, h2_ref, mod_ref, wup_ref, cw_ref, cb_ref, wdn_ref, fg_ref, *rest, final):
    *o_refs, z_ref, hid_ref = rest
    i = pl.program_id(0)
    joined = (i < NB_CTX).astype(F32)
    n_tiles = D_FF // FF_TILE

    def up(j):
        slot = j % 2
        lo = FF_TILE * j
        h2 = h2_ref[...]
        for c, col in enumerate((lo, D_FF + lo)):
            z = _dot(h2, wup_ref[:, col:col + FF_TILE])
            lanes = slice(FF_TILE * c, FF_TILE * (c + 1))
            for k in range(FF_NSEG):
                r0 = FF_PAD + FF_STRIDE * k
                z_ref[slot, r0:r0 + FF_SEG, lanes] = z[FF_SEG * k:FF_SEG * (k + 1)]
                before = z[FF_SEG * k - 1:FF_SEG * k] * joined if k > 0 else jnp.zeros((1, FF_TILE), F32)
                after = (z[FF_SEG * (k + 1):FF_SEG * (k + 1) + 1] * joined if k < FF_NSEG - 1
                         else jnp.zeros((1, FF_TILE), F32))
                z_ref[slot, r0 - 1:r0, lanes] = before
                z_ref[slot, r0 + FF_SEG:r0 + FF_SEG + 1, lanes] = after

    def gate(j):
        slot = j % 2
        lo = FF_TILE * j
        w = jnp.concatenate([cw_ref[:, lo:lo + FF_TILE], cw_ref[:, D_FF + lo:D_FF + lo + FF_TILE]], axis=1)
        b = jnp.concatenate([cb_ref[:, lo:lo + FF_TILE], cb_ref[:, D_FF + lo:D_FF + lo + FF_TILE]], axis=1)
        for k in range(FF_NSEG):
            r0 = FF_PAD + FF_STRIDE * k
            zc = (b + z_ref[slot, r0 - 1:r0 - 1 + FF_SEG, :] * w[0:1] + z_ref[slot, r0:r0 + FF_SEG, :] * w[1:2]
                  + z_ref[slot, r0 + 1:r0 + 1 + FF_SEG, :] * w[2:3])
            a = zc[:, :FF_TILE]
            t = jnp.tanh(a * (GELU_C0 + GELU_C1 * (a * a)))
            hid = a * (0.5 + 0.5 * t) * zc[:, FF_TILE:]
            hid_ref[FF_SEG * k:FF_SEG * (k + 1), lo:lo + FF_TILE] = hid.astype(BF16)

    up(0)
    for j in range(n_tiles):
        if j + 1 < n_tiles:
            up(j + 1)
        gate(j)
    acc = _dot(hid_ref[...], wdn_ref[...])
    g2 = mod_ref[...][:, 5 * D_MODEL:6 * D_MODEL]
    x2 = x1_ref[...] + g2 * acc
    if final:
        y = _rms(x2, fg_ref[...])
        yp_ref, ys_ref = o_refs

        @pl.when(i < NB_CTX)
        def _():
            yp_ref[...] = y

        @pl.when(i >= NB_CTX)
        def _():
            ys_ref[...] = y
    else:
        o_refs[0][...] = x2


def _ffn(x1, h2, mods, wup_bf, conv_w, conv_b, wdn_bf, final_g, l, *, final):
    if final:
        out_specs = tuple(_x_specs(True))
        out_shape = (jax.ShapeDtypeStruct((N_CTX, D_MODEL), F32), jax.ShapeDtypeStruct((N_LAT, D_MODEL), F32))
    else:
        out_specs = pl.BlockSpec((TB, D_MODEL), lambda i: (i, 0))
        out_shape = jax.ShapeDtypeStruct((N_TOK, D_MODEL), F32)
    return pl.pallas_call(
        functools.partial(_ffn_kernel, final=final),
        grid=(N_TOK // TB,),
        in_specs=[
            pl.BlockSpec((TB, D_MODEL), lambda i: (i, 0)),
            pl.BlockSpec((TB, D_MODEL), lambda i: (i, 0)),
            _mod_spec(l),
            _layer_spec((D_MODEL, 2 * D_FF), l, single=True), _layer_spec((3, 2 * D_FF), l),
            _layer_spec((1, 2 * D_FF), l), _layer_spec((D_FF, D_MODEL), l, single=True),
            _const_spec((1, D_MODEL)),
        ],
        out_specs=out_specs,
        out_shape=out_shape,
        scratch_shapes=[pltpu.VMEM((2, FF_ZROWS, 2 * FF_TILE), F32), pltpu.VMEM((TB, D_FF), BF16)],
        compiler_params=pltpu.CompilerParams(vmem_limit_bytes=VMEM_LIMIT),
        name="conv_ffn",
    )(x1, h2, mods, wup_bf, conv_w, conv_b, wdn_bf, final_g.reshape(1, D_MODEL))


def kernel(x_prompt, x_sample, state_s5_re, state_s5_im, state_hgrn, c, c_ctx, w_ada, b_ada, norm1_g,
           norm2_g, w_in, s5_lam_re, s5_lam_im, s5_log_dt, s5_b_re, s5_b_im, s5_c_re, s5_c_im, s5_d,
           s5_w_glu, hg_lb_logits, fn_w, gm_norm_g, gm_ws, gm_bs, grp_norm_g, w_out, ffn_w_up,
           ffn_conv_w, ffn_conv_b, ffn_w_down, final_norm_g):
    xs = (x_prompt.reshape(N_CTX, D_MODEL), x_sample.reshape(N_LAT, D_MODEL))
    cvecs = jnp.concatenate([c_ctx[None], c, jnp.zeros((8 - 1 - DEC_BATCH, D_MODEL), F32)], axis=0)
    mods = _ada(cvecs, w_ada, b_ada).reshape(DEPTH, 8, 1, N_MOD * D_MODEL)

    lb_p = jax.nn.softmax(hg_lb_logits.astype(F32), axis=0)
    lbs = jnp.maximum(jnp.cumsum(lb_p, axis=0) - lb_p[0], 0.0)
    lbps = jnp.stack([jnp.log(lbs), jnp.log1p(-lbs), 1.0 - lbs], axis=2)

    hg_consts_np = _hgrn_consts()
    hg_consts = (jnp.asarray(hg_consts_np[0], BF16), jnp.asarray(hg_consts_np[1], F32),
                 jnp.asarray(hg_consts_np[2], F32))
    cs_np, dft_ctx_np = _dft_consts(SEQ)
    _, dft_lat_np = _dft_consts(DEC_SEQ)
    cs = jnp.asarray(cs_np, F32).astype(BF16)
    dft_ctx = jnp.asarray(dft_ctx_np, F32).astype(BF16)
    dft_lat = jnp.asarray(dft_lat_np, F32).astype(BF16)
    gm_hm = jnp.asarray(np.kron(np.eye(GM_HEADS), np.ones((GM_CHUNK, GROUP_W // GM_HEADS))), BF16)
    bdm = jnp.asarray(np.kron(np.eye(HG_HEADS), np.ones((HG_DK, HG_DK))) / HG_DK, BF16)

    tables = _s5_tables(s5_lam_re, s5_lam_im, s5_log_dt, s5_b_re, s5_b_im, s5_c_re, s5_c_im)
    s5_m, s5_f, s5_e = _s5_prep(tables, jnp.asarray(_s5_shift_consts(), BF16))
    s5_al = tables[-1]
    s5_h0 = jnp.stack([state_s5_re, state_s5_im]).astype(F32).transpose(2, 3, 0, 1, 4, 5)
    s5_h0 = s5_h0.reshape(DEPTH, 2, 2, DEC_BATCH, S5_SW)
    hg_s0 = jnp.einsum('bldhkv,hg->lbdhvgk', state_hgrn.astype(F32), jnp.eye(HG_HEADS, dtype=F32))
    hg_s0 = hg_s0.reshape(DEPTH, DEC_BATCH, 2, HG_W, HG_W)

    def rows(a):
        return a.reshape(DEPTH, 1, a.shape[-1])

    norm1_r, norm2_r, gn_r, s5_d_r, gmn_r, cb_r = (rows(a) for a in (
        norm1_g, norm2_g, grp_norm_g, s5_d, gm_norm_g, ffn_conv_b))
    gm_w = gm_ws.transpose(0, 2, 1, 3).reshape(DEPTH, GM_CHUNK, GM_HEADS * GM_CHUNK).astype(BF16)
    gm_b = jnp.repeat(gm_bs.transpose(0, 2, 1), GROUP_W // GM_HEADS, axis=2)
    wup_bf = ffn_w_up.astype(BF16)
    wdn_bf = ffn_w_down.astype(BF16)

    new_re, new_im, new_hg = [], [], []
    for l in range(DEPTH):
        proj = _inproj(xs, mods, norm1_r, w_in, l)

        ylo, yhi, hfin = _s5(proj, s5_m, s5_f, s5_e, s5_al, s5_h0, l)
        hfin = hfin.reshape(2, 2, BATCH, S5_G, S5_N).transpose(1, 2, 0, 3, 4)
        new_re.append(hfin[0])
        new_im.append(hfin[1])

        of, ob, sfin = _hgrn(proj, lbps, hg_consts, hg_s0, l)
        new_hg.append(sfin)

        yc = _fnet(proj, cs, dft_ctx, fn_w, None, l, n_seq=BATCH, seq_len=SEQ, tok0=0)
        yc = _fnet(proj, cs, dft_lat, fn_w, yc, l, n_seq=DEC_BATCH, seq_len=DEC_SEQ, tok0=N_CTX)

        od = _gmlp(proj, gmn_r, gm_w, gm_b, gm_hm, gn_r, l)

        x1, h2 = _outproj(xs, proj, ylo, yhi, of, ob, yc, od, mods, s5_d_r, s5_w_glu, gn_r, bdm, w_out,
                          norm2_r, l)
        res = _ffn(x1, h2, mods, wup_bf, ffn_conv_w, cb_r, wdn_bf, final_norm_g, l, final=(l == DEPTH - 1))
        xs = (res,)

    y_prompt = res[0].reshape(BATCH, SEQ, D_MODEL)
    y_sample = res[1].reshape(DEC_BATCH, DEC_SEQ, D_MODEL)
    return (y_prompt, y_sample, jnp.stack(new_re, axis=1), jnp.stack(new_im, axis=1),
            jnp.stack(new_hg, axis=1))
```

```python
import functools
import math

import numpy as np
import jax
import jax.numpy as jnp
from jax import lax
from jax.experimental import pallas as pl
from jax.experimental.pallas import tpu as pltpu

D_MODEL = 1024
BATCH = 16
SEQ = 256
DEPTH = 2
DEC_BATCH = 2
DEC_SEQ = 2048
GRID_W = 64
GROUP_W = 256
S5_P = 16
S5_G = 16
S5_N = 64
HG_HEADS = 4
HG_DK = 64
GM_HEADS = 4
GM_CHUNK = 128
D_FF = 2816
N_MOD = 6
D_IN = 9 * GROUP_W
EPS = 1e-6
LAM_RE_MAX = -1e-4

N_CTX = BATCH * SEQ
N_LAT = DEC_BATCH * DEC_SEQ
N_TOK = N_CTX + N_LAT
TB = 256
NB_CTX = N_CTX // TB
NB_LAT_SEQ = DEC_SEQ // TB
S5_L = 16
HG_L = 64
FF_TILE = 256
LANES = 128
VMEM_LIMIT = 56 * 1024 * 1024

F32 = jnp.float32
BF16 = jnp.bfloat16

COL_XA, COL_HQ, COL_HF_FWD, COL_HF_BWD, COL_HI, COL_HGATE, COL_XC, COL_GU, COL_GV = range(9)


def _mod_row(i):
    return jnp.where(i < NB_CTX, 0, 1 + (i - NB_CTX) // NB_LAT_SEQ)


def _gelu(x):
    return 0.5 * x * (1.0 + jnp.tanh(0.7978845608028654 * (x + 0.044715 * (x * x * x))))


def _rms(x, g):
    return x * lax.rsqrt(jnp.mean(x * x, axis=-1, keepdims=True) + EPS) * g


def _dot(a, b):
    return jnp.dot(a, b, preferred_element_type=F32)


def _dot_nt(a, b, precision=None):
    return lax.dot_general(a, b, (((1,), (1,)), ((), ())), precision=precision, preferred_element_type=F32)


def _dot_tn(a, b):
    return lax.dot_general(a, b, (((0,), (0,)), ((), ())), preferred_element_type=F32)


def _const_spec(shape, single=False):
    kw = {"pipeline_mode": pl.Buffered(1)} if single else {}
    return pl.BlockSpec(shape, lambda *_: (0,) * len(shape), **kw)


def _layer_spec(shape, l, single=False):
    kw = {"pipeline_mode": pl.Buffered(1)} if single else {}
    return pl.BlockSpec((None,) + tuple(shape), lambda *_: (l,) + (0,) * len(shape), **kw)


def _mod_spec(l):
    return pl.BlockSpec((None, None, 1, N_MOD * D_MODEL), lambda i: (l, _mod_row(i), 0, 0))


def _x_specs(split):
    if split:
        return [pl.BlockSpec((TB, D_MODEL), lambda i: (jnp.minimum(i, NB_CTX - 1), 0)),
                pl.BlockSpec((TB, D_MODEL), lambda i: (jnp.maximum(i - NB_CTX, 0), 0))]
    return [pl.BlockSpec((TB, D_MODEL), lambda i: (i, 0))]


def _x_value(x_refs):
    if len(x_refs) == 1:
        return x_refs[0][...]
    return jnp.where(pl.program_id(0) < NB_CTX, x_refs[0][...], x_refs[1][...])


ADA_TN = 1536


def _ada_kernel(c_ref, w_ref, b_ref, o_ref):
    cv = c_ref[...]
    s = cv * jax.nn.sigmoid(cv)
    o_ref[0] = _dot(s.astype(BF16), w_ref[0].astype(BF16)) + b_ref[0]


def _ada(cvecs, w_ada, b_ada):
    n = N_MOD * D_MODEL
    return pl.pallas_call(
        _ada_kernel,
        grid=(DEPTH, n // ADA_TN),
        in_specs=[
            pl.BlockSpec((8, D_MODEL), lambda l, j: (0, 0)),
            pl.BlockSpec((1, D_MODEL, ADA_TN), lambda l, j: (l, 0, j)),
            pl.BlockSpec((1, 1, ADA_TN), lambda l, j: (l, 0, j)),
        ],
        out_specs=pl.BlockSpec((1, 8, ADA_TN), lambda l, j: (l, 0, j)),
        out_shape=jax.ShapeDtypeStruct((DEPTH, 8, n), F32),
        compiler_params=pltpu.CompilerParams(vmem_limit_bytes=VMEM_LIMIT),
        name="ada_mod",
    )(cvecs, w_ada, b_ada.reshape(DEPTH, 1, n))


def _inproj_kernel(*refs):
    *x_refs, mod_ref, g_ref, w_ref, o_ref, wb_ref = refs

    @pl.when(pl.program_id(0) == 0)
    def _():
        wb_ref[...] = w_ref[...].astype(BF16)

    mod = mod_ref[...]
    sh = mod[:, 0:D_MODEL]
    sc = mod[:, D_MODEL:2 * D_MODEL]
    h = _rms(_x_value(x_refs), g_ref[...]) * (1.0 + sc) + sh
    o_ref[...] = _dot(h.astype(BF16), wb_ref[...])


def _inproj(xs, mods, norm_g, w_in, l):
    return pl.pallas_call(
        _inproj_kernel,
        grid=(N_TOK // TB,),
        in_specs=_x_specs(len(xs) == 2) + [
            _mod_spec(l),
            _layer_spec((1, D_MODEL), l),
            _layer_spec((D_MODEL, D_IN), l, single=True),
        ],
        out_specs=pl.BlockSpec((TB, D_IN), lambda i: (i, 0)),
        out_shape=jax.ShapeDtypeStruct((N_TOK, D_IN), F32),
        scratch_shapes=[pltpu.VMEM((D_MODEL, D_IN), BF16)],
        compiler_params=pltpu.CompilerParams(vmem_limit_bytes=VMEM_LIMIT),
        name="in_proj",
    )(*xs, mods, norm_g, w_in)


S5_NPAIR = S5_G // 2
S5_CW = S5_L * S5_P
S5_SW = S5_G * S5_N
S5_STEP_ROWS = N_CTX // S5_L
S5_CTX_NC = SEQ // S5_L
S5_LAT_NC = DEC_SEQ // S5_L


def _s5_tables(lam_re, lam_im, log_dt, b_re, b_im, c_re, c_im):
    lr = jnp.minimum(lam_re.astype(F32), LAM_RE_MAX)
    li = lam_im.astype(F32)
    dt = jnp.exp(log_dt.astype(F32))[..., None]
    mag = jnp.exp(lr * dt)
    ang = li * dt
    ab_re = mag * jnp.cos(ang)
    ab_im = mag * jnp.sin(ang)
    den = lr * lr + li * li
    xr = ab_re - 1.0
    z_re = (xr * lr + ab_im * li) / den
    z_im = (ab_im * lr - xr * li) / den
    bb_re = z_re[..., None] * b_re - z_im[..., None] * b_im
    bb_im = z_re[..., None] * b_im + z_im[..., None] * b_re
    tau = jnp.arange(S5_L + 1, dtype=F32)[:, None, None, None, None]
    pm = jnp.exp(lr * dt * tau)
    pa = li * dt * tau
    pw_re = pm * jnp.cos(pa)
    pw_im = pm * jnp.sin(pa)
    eye2 = jnp.eye(2, dtype=F32)

    def pw_pairs(a):
        a = a.transpose(1, 2, 3, 0, 4).reshape(DEPTH, 2, S5_NPAIR, 2, S5_L + 1, S5_N)
        return a.transpose(0, 1, 2, 4, 3, 5).reshape(DEPTH, 2, S5_NPAIR, S5_L + 1, 2 * S5_N)

    def mat_pairs(a):
        a = a.reshape(DEPTH, 2, S5_NPAIR, 2, S5_P, S5_N)
        return jnp.einsum('ldjaqn,ab->ldjaqbn', a, eye2).reshape(DEPTH, 2, S5_NPAIR, 2, S5_P, 2 * S5_N)

    al = jnp.stack([pw_re[S5_L], pw_im[S5_L]], axis=2).reshape(DEPTH, 2, 2, 1, S5_SW)
    return (pw_pairs(pw_re), pw_pairs(pw_im),
            mat_pairs(bb_re.transpose(0, 1, 2, 4, 3)), mat_pairs(bb_im.transpose(0, 1, 2, 4, 3)),
            mat_pairs(c_re.astype(F32)), mat_pairs(c_im.astype(F32)), al)


def _s5_shift_consts():
    n = S5_CW
    r = np.arange(n)[:, None]
    c = np.arange(n)[None, :]
    fwd = [(c == r + S5_P * s) for s in range(S5_L)]
    bwd = [(c == r - S5_P * (S5_L - 1 - s)) for s in range(S5_L)]
    return np.stack([np.concatenate(fwd, axis=1), np.concatenate(bwd, axis=1)]).astype(np.float32)


def _s5_prep_kernel(pwr_ref, pwi_ref, br_ref, bi_ref, cr_ref, ci_ref, scat_ref, m_ref, f_ref, e_ref, k_ref):
    L = S5_L
    hp = lax.Precision.HIGHEST
    sw = 2 * S5_N

    def pair_body(j, carry):
        for d in range(2):
            pwr = pwr_ref[0, d, j]
            pwi = pwi_ref[0, d, j]
            for gl in range(2):
                br, bi = br_ref[0, d, j, gl], bi_ref[0, d, j, gl]
                cr, ci = cr_ref[0, d, j, gl], ci_ref[0, d, j, gl]
                ca_r, ca_i, f_r, f_i, e_r, e_i = [], [], [], [], [], []
                for t in range(L):
                    kk = t if d == 0 else L - 1 - t
                    pr, pi = pwr[kk:kk + 1], pwi[kk:kk + 1]
                    ca_r.append(cr * pr - ci * pi)
                    ca_i.append(cr * pi + ci * pr)
                    kf = L - 1 - t if d == 0 else t
                    pr, pi = pwr[kf:kf + 1], pwi[kf:kf + 1]
                    f_r.append(br * pr - bi * pi)
                    f_i.append(br * pi + bi * pr)
                    ke = t + 1 if d == 0 else L - t
                    pr, pi = pwr[ke:ke + 1], pwi[ke:ke + 1]
                    e_r.append(cr * pr - ci * pi)
                    e_i.append(-(cr * pi + ci * pr))
                cat = lambda xs: jnp.concatenate(xs, axis=0)
                k = _dot_nt(br, cat(ca_r), hp) - _dot_nt(bi, cat(ca_i), hp)
                g16 = pl.multiple_of((2 * j + gl) * S5_P, S5_P)
                k_ref[d, pl.ds(g16, S5_P), :] = k
                rows = slice(S5_CW * gl, S5_CW * (gl + 1))
                f_ref[0, d, j, rows, 0:sw] = cat(f_r).astype(BF16)
                f_ref[0, d, j, rows, sw:2 * sw] = cat(f_i).astype(BF16)
                e_ref[0, d, 0, j, rows, :] = cat(e_r).astype(BF16)
                e_ref[0, d, 1, j, rows, :] = cat(e_i).astype(BF16)
        return carry

    lax.fori_loop(0, S5_NPAIR, pair_body, 0)
    kf = k_ref[0].astype(BF16)
    kb = k_ref[1].astype(BF16)
    for s in range(L):
        cols = slice(S5_CW * s, S5_CW * (s + 1))
        res = _dot(kf, scat_ref[0, :, cols]) + _dot(kb, scat_ref[1, :, cols])
        for g in range(S5_G):
            m_ref[0, g, S5_P * s:S5_P * (s + 1), :] = res[S5_P * g:S5_P * (g + 1), :].astype(BF16)


def _s5_prep(tables, scat):
    pwr, pwi, br, bi, cr, ci, _ = tables

    def lspec(shape):
        return pl.BlockSpec((1,) + shape, lambda l: (l,) + (0,) * len(shape))

    pw_shape = (2, S5_NPAIR, S5_L + 1, 2 * S5_N)
    mat_shape = (2, S5_NPAIR, 2, S5_P, 2 * S5_N)
    return pl.pallas_call(
        _s5_prep_kernel,
        grid=(DEPTH,),
        in_specs=[lspec(pw_shape), lspec(pw_shape), lspec(mat_shape), lspec(mat_shape), lspec(mat_shape),
                  lspec(mat_shape), _const_spec(scat.shape)],
        out_specs=(lspec((S5_G, S5_CW, S5_CW)), lspec((2, S5_NPAIR, 2 * S5_CW, 4 * S5_N)),
                   lspec((2, 2, S5_NPAIR, 2 * S5_CW, 2 * S5_N))),
        out_shape=(jax.ShapeDtypeStruct((DEPTH, S5_G, S5_CW, S5_CW), BF16),
                   jax.ShapeDtypeStruct((DEPTH, 2, S5_NPAIR, 2 * S5_CW, 4 * S5_N), BF16),
                   jax.ShapeDtypeStruct((DEPTH, 2, 2, S5_NPAIR, 2 * S5_CW, 2 * S5_N), BF16)),
        scratch_shapes=[pltpu.VMEM((2, S5_G * S5_P, S5_CW), F32)],
        compiler_params=pltpu.CompilerParams(vmem_limit_bytes=VMEM_LIMIT),
        name="s5_prep",
    )(pwr, pwi, br, bi, cr, ci, scat)


def _s5_kernel(xlo_ref, xhi_ref, m_ref, f_ref, e_ref, al_ref, h0_ref, ylo_ref, yhi_ref, hfin_ref,
               uy_ref, zh_ref):
    R = S5_STEP_ROWS
    pw = 2 * S5_CW
    sw = 2 * S5_N
    gph = S5_G // 2
    step = pl.program_id(0)

    for half, x_ref in enumerate((xlo_ref, xhi_ref)):
        for t in range(S5_L):
            xt = x_ref[pl.ds(t, R, stride=S5_L), :]
            for gl in range(gph):
                lo = S5_CW * (gph * half + gl) + S5_P * t
                uy_ref[:, lo:lo + S5_P] = xt[:, S5_P * gl:S5_P * (gl + 1)]

    for j in range(S5_NPAIR):
        ub = uy_ref[:, pw * j:pw * (j + 1)].astype(BF16)
        for d in range(2):
            o = _dot(ub, f_ref[d, j])
            zh_ref[d, 0, :, sw * j:sw * (j + 1)] = o[:, :sw]
            zh_ref[d, 1, :, sw * j:sw * (j + 1)] = o[:, sw:]

    def scan_group(seq_rows, nc, init, fin_rows):
        chains = [(d, r0) for d in range(2) for r0 in seq_rows]

        def body(c, carry):
            out = []
            for k, (d, r0) in enumerate(chains):
                hr, hi = carry[2 * k], carry[2 * k + 1]
                r = r0 + (c if d == 0 else nc - 1 - c)
                zr = zh_ref[d, 0, pl.ds(r, 1), :]
                zi = zh_ref[d, 1, pl.ds(r, 1), :]
                zh_ref[d, 0, pl.ds(r, 1), :] = hr
                zh_ref[d, 1, pl.ds(r, 1), :] = hi
                ar = al_ref[d, 0]
                ai = al_ref[d, 1]
                out.append(ar * hr - ai * hi + zr)
                out.append(ar * hi + ai * hr + zi)
            return tuple(out)

        fin = lax.fori_loop(0, nc, body, tuple(init))
        if fin_rows is not None:
            for k, (d, _) in enumerate(chains):
                s = fin_rows[k % len(seq_rows)]
                hfin_ref[d, 0, s:s + 1, :] = fin[2 * k]
                hfin_ref[d, 1, s:s + 1, :] = fin[2 * k + 1]

    @pl.when(step == 0)
    def _():
        zero = jnp.zeros((1, S5_SW), F32)
        for s0 in range(0, BATCH, 4):
            seqs = list(range(s0, s0 + 4))
            scan_group([s * S5_CTX_NC for s in seqs], S5_CTX_NC, [zero] * 16, seqs)

    @pl.when(step == 1)
    def _():
        init = []
        for d in range(2):
            for s in range(DEC_BATCH):
                init.append(h0_ref[d, 0, s:s + 1, :])
                init.append(h0_ref[d, 1, s:s + 1, :])
        scan_group([s * S5_LAT_NC for s in range(DEC_BATCH)], S5_LAT_NC, init, None)

    for j in range(S5_NPAIR):
        acc = None
        for d in range(2):
            for ri in range(2):
                hb = zh_ref[d, ri, :, sw * j:sw * (j + 1)].astype(BF16)
                t = _dot_nt(hb, e_ref[d, ri, j])
                acc = t if acc is None else acc + t
        for gl in range(2):
            g = 2 * j + gl
            ug = uy_ref[:, S5_CW * g:S5_CW * (g + 1)].astype(BF16)
            uy_ref[:, S5_CW * g:S5_CW * (g + 1)] = acc[:, S5_CW * gl:S5_CW * (gl + 1)] + _dot(ug, m_ref[g])

    for half, y_ref in enumerate((ylo_ref, yhi_ref)):
        for t in range(S5_L):
            pieces = []
            for gl in range(gph):
                lo = S5_CW * (gph * half + gl) + S5_P * t
                pieces.append(uy_ref[:, lo:lo + S5_P])
            y_ref[pl.ds(t, R, stride=S5_L), :] = jnp.concatenate(pieces, axis=1)


def _s5(proj, m, f2, e2, al, h0, l):
    half = pl.BlockSpec((N_CTX, LANES), lambda s: (s, 0))
    return pl.pallas_call(
        _s5_kernel,
        grid=(2,),
        in_specs=[pl.BlockSpec((N_CTX, LANES), lambda s: (s, 0)),
                  pl.BlockSpec((N_CTX, LANES), lambda s: (s, 1)),
                  _layer_spec(m.shape[1:], l, single=True), _layer_spec(f2.shape[1:], l, single=True),
                  _layer_spec(e2.shape[1:], l, single=True), _layer_spec(al.shape[1:], l),
                  _layer_spec(h0.shape[1:], l)],
        out_specs=(half, half, _const_spec((2, 2, BATCH, S5_SW))),
        out_shape=(jax.ShapeDtypeStruct((N_TOK, LANES), F32),
                   jax.ShapeDtypeStruct((N_TOK, LANES), F32),
                   jax.ShapeDtypeStruct((2, 2, BATCH, S5_SW), F32)),
        scratch_shapes=[pltpu.VMEM((S5_STEP_ROWS, S5_G * S5_CW), F32),
                        pltpu.VMEM((2, 2, S5_STEP_ROWS, S5_SW), F32)],
        compiler_params=pltpu.CompilerParams(vmem_limit_bytes=VMEM_LIMIT),
        name="s5_mixer",
    )(proj, proj, m, f2, e2, al, h0)


HG_NLEV = int(math.log2(HG_L))
HG_W = HG_HEADS * HG_DK
HG_CPS = 4
HG_BLK = HG_CPS * HG_L
HG_NB_CTX = N_CTX // HG_BLK
HG_NC_CTX = SEQ // HG_BLK
HG_NC_LAT = DEC_SEQ // HG_BLK
assert HG_L == HG_DK


def _hgrn_consts():
    L = HG_L
    w = np.zeros((HG_NLEV + 2, L, L), np.float32)
    mask = np.zeros((HG_NLEV + 1, L, L), np.float32)
    for lev in range(HG_NLEV):
        blk = L >> lev
        half = blk // 2
        for t in range(L):
            p, o = divmod(t, blk)
            bd = p * blk + half - 1
            if o >= half:
                w[lev, t, bd + 1:t + 1] = 1.0
            else:
                w[lev, t, t + 1:bd + 1] = 1.0
        jj, ii = np.meshgrid(np.arange(L), np.arange(L), indexing='ij')
        mask[lev] = ((jj // blk == ii // blk) & (jj % blk >= half) & (ii % blk < half)).astype(np.float32)
    mask[HG_NLEV] = np.eye(L, dtype=np.float32)
    for t in range(L):
        w[HG_NLEV, t, :t + 1] = 1.0
        w[HG_NLEV + 1, t, t + 1:] = 1.0
    out = []
    for wd, md in ((w, mask), (w[:, ::-1, ::-1], mask[:, ::-1, ::-1])):
        wflat = wd.reshape((HG_NLEV + 2) * L, L)
        out.append((np.concatenate([wflat] * 3, axis=1), np.tile(md, (1, 1, HG_HEADS))))
    wcat = np.stack([out[0][0], out[1][0]])
    mask4 = np.stack([out[0][1], out[1][1]])
    hm = np.kron(np.eye(HG_HEADS, dtype=np.float32), np.ones((HG_DK, HG_DK), np.float32))
    return wcat, mask4, hm


def _hg_pos(i):
    is_ctx = i < HG_NB_CTX
    c = jnp.where(is_ctx, i % HG_NC_CTX, (i - HG_NB_CTX) % HG_NC_LAT)
    nc = jnp.where(is_ctx, HG_NC_CTX, HG_NC_LAT)
    return is_ctx, c, nc


def _hg_bwd_blk(i):
    _, c, nc = _hg_pos(i)
    return i + nc - 1 - 2 * c


def _hg_local(d, q, z, v, lbp, wcat_ref, mask_ref, hm):
    L = HG_L
    az = jnp.abs(z)
    sp = jnp.log1p(jnp.exp(-az))
    ls = jnp.minimum(z, 0.0) - sp
    kk = lbp[2:3] * jnp.exp(jnp.minimum(-z, 0.0) - sp)
    a = lbp[0:1]
    b = lbp[1:2] + ls
    logf = jnp.maximum(a, b) + jnp.log1p(jnp.exp(-jnp.abs(a - b)))
    p0 = logf.astype(BF16)
    r0 = logf - p0.astype(F32)
    p1 = r0.astype(BF16)
    p2 = (r0 - p1.astype(F32)).astype(BF16)
    ex = jnp.exp(_dot(wcat_ref[d], jnp.concatenate([p0, p1, p2], axis=0)))
    hmb = hm.astype(BF16)

    def bd4(x):
        return jnp.concatenate([x] * HG_HEADS, axis=0) * hmb

    sc = None
    for lev in range(HG_NLEV + 1):
        if lev < HG_NLEV:
            al = ex[L * lev:L * (lev + 1)]
            lhs = (q * al).astype(BF16)
            rhs = (kk * al).astype(BF16)
        else:
            lhs = q.astype(BF16)
            rhs = kk.astype(BF16)
        t = _dot_nt(lhs, bd4(rhs)) * mask_ref[d, lev]
        sc = t if sc is None else sc + t
    vb = v.astype(BF16)
    o = _dot(sc.astype(BF16), bd4(vb))
    eq = ex[L * HG_NLEV:L * (HG_NLEV + 1)]
    qe = (q * eq).astype(BF16)
    ke = (kk * ex[L * (HG_NLEV + 1):L * (HG_NLEV + 2)]).astype(BF16)
    g = eq[L - 1:L] if d == 0 else eq[0:1]
    return o, qe, g, _dot_tn(vb, ke) * hm


def _hgrn_kernel(qf, zf, vf, qb, zb, vb, lb_ref, w_ref, mask_ref, hm_ref, s0_ref,
                 of_ref, ob_ref, sfin_ref, st_ref):
    is_ctx, c, nc = _hg_pos(pl.program_id(0))

    @pl.when(c == 0)
    def _():
        st_ref[...] = jnp.where(is_ctx, 0.0, s0_ref[0])

    hm = hm_ref[...]
    for d, (q_ref, z_ref, v_ref, o_ref) in enumerate(((qf, zf, vf, of_ref), (qb, zb, vb, ob_ref))):
        rows = [slice(HG_L * k, HG_L * (k + 1)) for k in range(HG_CPS)]
        loc = [_hg_local(d, q_ref[r, :], z_ref[r, :], v_ref[r, :], lb_ref[d], w_ref, mask_ref, hm)
               for r in rows]
        st = st_ref[d]
        for k in (range(HG_CPS) if d == 0 else reversed(range(HG_CPS))):
            o, qe, g, upd = loc[k]
            o_ref[rows[k], :] = o + _dot_nt(qe, st.astype(BF16))
            st = st * g + upd
        st_ref[d] = st

    @pl.when(jnp.logical_and(is_ctx, c == nc - 1))
    def _():
        for d in range(2):
            s_kv = st_ref[d].T
            for h in range(HG_HEADS):
                sfin_ref[0, d, h] = s_kv[HG_DK * h:HG_DK * (h + 1), HG_DK * h:HG_DK * (h + 1)]


def _hgrn(proj, lbp, consts, s0, l):
    wcat, mask4, hm = consts

    def fwd(col):
        return pl.BlockSpec((HG_BLK, HG_W), lambda i: (i, col))

    def bwd(col):
        return pl.BlockSpec((HG_BLK, HG_W), lambda i: (_hg_bwd_blk(i), col))

    return pl.pallas_call(
        _hgrn_kernel,
        grid=(N_TOK // HG_BLK,),
        in_specs=[fwd(COL_HQ), fwd(COL_HF_FWD), fwd(COL_HI), bwd(COL_HQ), bwd(COL_HF_BWD), bwd(COL_HI),
                  _layer_spec(lbp.shape[1:], l), _const_spec(wcat.shape), _const_spec(mask4.shape),
                  _const_spec(hm.shape),
                  pl.BlockSpec((None, 1, 2, HG_W, HG_W),
                               lambda i: (l, jnp.maximum(i - HG_NB_CTX, 0) // HG_NC_LAT, 0, 0, 0))],
        out_specs=(pl.BlockSpec((HG_BLK, HG_W), lambda i: (i, 0)),
                   pl.BlockSpec((HG_BLK, HG_W), lambda i: (_hg_bwd_blk(i), 0)),
                   pl.BlockSpec((1, 2, HG_HEADS, HG_DK, HG_DK),
                                lambda i: (jnp.minimum(i // HG_NC_CTX, BATCH - 1), 0, 0, 0, 0))),
        out_shape=(jax.ShapeDtypeStruct((N_TOK, HG_W), F32),
                   jax.ShapeDtypeStruct((N_TOK, HG_W), F32),
                   jax.ShapeDtypeStruct((BATCH, 2, HG_HEADS, HG_DK, HG_DK), F32)),
        scratch_shapes=[pltpu.VMEM((2, HG_W, HG_W), F32)],
        compiler_params=pltpu.CompilerParams(vmem_limit_bytes=VMEM_LIMIT),
        name="hgrn_mixer",
    )(proj, proj, proj, proj, proj, proj, lbp, wcat, mask4, hm, s0)


FN_TR = 256


def _dft_consts(t_len):
    n = GROUP_W // 4
    k = np.arange(n)
    ang = 2.0 * np.pi * ((k[:, None] * k[None, :]) % n) / n
    eye = np.eye(4)
    cs = np.concatenate([np.kron(eye, np.cos(ang)), np.kron(eye, np.sin(ang))], axis=1) / math.sqrt(n)
    t = np.arange(t_len)
    angt = 2.0 * np.pi * ((t[:, None] * t[None, :]) % t_len) / t_len
    dft = np.concatenate([np.cos(angt), -np.sin(angt)], axis=1) / math.sqrt(t_len)
    return cs.astype(np.float32), dft.astype(np.float32)


def _fnet_kernel(x_ref, cs_ref, dft_ref, w_ref, *rest, t_len):
    o_ref, r_ref = rest[-2:]

    @pl.when(pl.program_id(1) == 0)
    def _():
        t = _dot(x_ref[...].astype(BF16), cs_ref[...])
        r_ref[0:t_len, :] = t[:, :GROUP_W].astype(BF16)
        r_ref[t_len:2 * t_len, :] = t[:, GROUP_W:].astype(BF16)

    y = _dot(dft_ref[...], r_ref[...])
    o_ref[...] = _dot(y.astype(BF16), w_ref[...].astype(BF16))


def _fnet(proj, cs, dft, fn_w, prev, l, *, n_seq, seq_len, tok0):
    nj = seq_len // FN_TR
    sb0 = tok0 // seq_len
    rb0 = tok0 // FN_TR
    in_specs = [
        pl.BlockSpec((seq_len, GROUP_W), lambda s, j: (sb0 + s, COL_XC)),
        _const_spec((GROUP_W, 2 * GROUP_W)),
        pl.BlockSpec((FN_TR, 2 * seq_len), lambda s, j: (j, 0)),
        _layer_spec((GROUP_W, GROUP_W), l),
    ]
    args = [proj, cs, dft, fn_w]
    aliases = {}
    if prev is not None:
        in_specs.append(pl.BlockSpec(memory_space=pl.ANY))
        args.append(prev)
        aliases = {4: 0}
    return pl.pallas_call(
        functools.partial(_fnet_kernel, t_len=seq_len),
        grid=(n_seq, nj),
        in_specs=in_specs,
        out_specs=pl.BlockSpec((FN_TR, GROUP_W), lambda s, j: (rb0 + s * nj + j, 0)),
        out_shape=jax.ShapeDtypeStruct((N_TOK, GROUP_W), F32),
        scratch_shapes=[pltpu.VMEM((2 * seq_len, GROUP_W), BF16)],
        input_output_aliases=aliases,
        compiler_params=pltpu.CompilerParams(vmem_limit_bytes=VMEM_LIMIT),
        name="fnet_mixer",
    )(*args)


GM_TB = 512


def _gmlp_kernel(gu_ref, gv_ref, g_ref, w_ref, bias_ref, hm_ref, gn_ref, o_ref):
    hmb = hm_ref[...]
    for c in range(GM_TB // GM_CHUNK):
        rows = slice(GM_CHUNK * c, GM_CHUNK * (c + 1))
        gu = _gelu(gu_ref[rows, :])
        gv = _rms(_gelu(gv_ref[rows, :]), g_ref[...]).astype(BF16)
        g4 = jnp.concatenate([gv] * GM_HEADS, axis=0) * hmb
        sp = _dot(w_ref[...], g4) + bias_ref[...]
        o_ref[rows, :] = _rms(gu * sp, gn_ref[...])


def _gmlp(proj, gm_norm_g, wcat, bias, hm, gn, l):
    return pl.pallas_call(
        _gmlp_kernel,
        grid=(N_TOK // GM_TB,),
        in_specs=[
            pl.BlockSpec((GM_TB, GROUP_W), lambda i: (i, COL_GU)),
            pl.BlockSpec((GM_TB, GROUP_W), lambda i: (i, COL_GV)),
            _layer_spec((1, GROUP_W), l),
            _layer_spec((GM_CHUNK, GM_HEADS * GM_CHUNK), l),
            _layer_spec((GM_CHUNK, GROUP_W), l),
            _const_spec((GM_HEADS * GM_CHUNK, GROUP_W)),
            pl.BlockSpec((None, 1, GROUP_W), lambda i: (l, 0, 3)),
        ],
        out_specs=pl.BlockSpec((GM_TB, GROUP_W), lambda i: (i, 0)),
        out_shape=jax.ShapeDtypeStruct((N_TOK, GROUP_W), F32),
        compiler_params=pltpu.CompilerParams(vmem_limit_bytes=VMEM_LIMIT),
        name="gmlp_mixer",
    )(proj, proj, gm_norm_g, wcat, bias, hm, gn)


def _outproj_kernel(*refs):
    (*x_refs, xa_ref, hgate_ref, ylo_ref, yhi_ref, of_ref, ob_ref, yc_ref, od_ref, mod_ref, d_ref,
     wglu_ref, gn_ref, bdm_ref, wout_ref, n2_ref, x1_ref, h2_ref, wob_ref) = refs

    @pl.when(pl.program_id(0) == 0)
    def _():
        wob_ref[...] = wout_ref[...].astype(BF16)

    gn = gn_ref[...]
    ys = jnp.concatenate([ylo_ref[...], yhi_ref[...]], axis=1)
    y5 = _gelu(ys + d_ref[...] * xa_ref[...])
    glu = jax.nn.sigmoid(_dot(y5.astype(BF16), wglu_ref[...].astype(BF16)))
    out_a = _rms(y5 * glu, gn[:, 0:GROUP_W])
    o = of_ref[...] + ob_ref[...]
    o2 = o * o
    o2h = o2.astype(BF16)
    o2l = (o2 - o2h.astype(F32)).astype(BF16)
    ms = _dot(o2h, bdm_ref[...]) + _dot(o2l, bdm_ref[...])
    hg = hgate_ref[...]
    out_b = o * lax.rsqrt(ms + EPS) * gn[:, GROUP_W:2 * GROUP_W] * (hg * jax.nn.sigmoid(hg))
    out_c = _rms(yc_ref[...], gn[:, 2 * GROUP_W:3 * GROUP_W])
    out_d = od_ref[...]
    m = None
    for k, part in enumerate((out_a, out_b, out_c, out_d)):
        t = _dot(part.astype(BF16), wob_ref[GROUP_W * k:GROUP_W * (k + 1), :])
        m = t if m is None else m + t
    mod = mod_ref[...]
    g1 = mod[:, 2 * D_MODEL:3 * D_MODEL]
    sh2 = mod[:, 3 * D_MODEL:4 * D_MODEL]
    sc2 = mod[:, 4 * D_MODEL:5 * D_MODEL]
    x1 = _x_value(x_refs) + g1 * m
    x1_ref[...] = x1
    h2_ref[...] = (_rms(x1, n2_ref[...]) * (1.0 + sc2) + sh2).astype(BF16)


def _outproj(xs, proj, ylo, yhi, of, ob, yc, od, mods, s5_d, wglu, gn, bdm, wout, norm2_g, l):
    def tok(width, col=0):
        return pl.BlockSpec((TB, width), lambda i: (i, col))

    return pl.pallas_call(
        _outproj_kernel,
        grid=(N_TOK // TB,),
        in_specs=_x_specs(len(xs) == 2) + [
            tok(GROUP_W, COL_XA), tok(GROUP_W, COL_HGATE), tok(LANES), tok(LANES),
            tok(GROUP_W), tok(GROUP_W), tok(GROUP_W), tok(GROUP_W),
            _mod_spec(l),
            _layer_spec((1, GROUP_W), l), _layer_spec((GROUP_W, GROUP_W), l), _layer_spec((1, D_MODEL), l),
            _const_spec((GROUP_W, GROUP_W)), _layer_spec((D_MODEL, D_MODEL), l, single=True),
            _layer_spec((1, D_MODEL), l),
        ],
        out_specs=(tok(D_MODEL), tok(D_MODEL)),
        out_shape=(jax.ShapeDtypeStruct((N_TOK, D_MODEL), F32),
                   jax.ShapeDtypeStruct((N_TOK, D_MODEL), BF16)),
        scratch_shapes=[pltpu.VMEM((D_MODEL, D_MODEL), BF16)],
        compiler_params=pltpu.CompilerParams(vmem_limit_bytes=VMEM_LIMIT),
        name="out_proj",
    )(*xs, proj, proj, ylo, yhi, of, ob, yc, od, mods, s5_d, wglu, gn, bdm, wout, norm2_g)


FF_SEG = GRID_W
FF_NSEG = TB // FF_SEG
SUBLANES = 8
FF_NGRP = FF_SEG // SUBLANES
GELU_C0 = 0.7978845608028654
GELU_C1 = GELU_C0 * 0.044715


def _ffn_perm_consts():
    p = np.zeros((TB, TB), np.float32)
    for k in range(FF_NSEG):
        for t in range(FF_SEG):
            p[FF_SEG * k + SUBLANES * (t % FF_NGRP) + t // FF_NGRP, FF_SEG * k + t] = 1.0
    return p


def _ffn_kernel(x1_ref, h2_ref, mod_ref, perm_ref, wup_ref, cw_ref, cb_ref, wdn_ref, fg_ref, *rest, final):
    *o_refs, h2p_ref, z_ref, hid_ref = rest
    i = pl.program_id(0)
    joined = (i < NB_CTX).astype(F32)
    n_tiles = D_FF // FF_TILE
    sub = lax.broadcasted_iota(jnp.int32, (SUBLANES, 2 * FF_TILE), 0)
    is_first = sub == 0
    is_last = sub == SUBLANES - 1
    zero_grp = jnp.zeros((SUBLANES, 2 * FF_TILE), F32)
    h2p_ref[...] = _dot(perm_ref[...], h2_ref[...]).astype(BF16)

    def up(j):
        lo = FF_TILE * j
        for c, col in enumerate((lo, D_FF + lo)):
            z_ref[j % 2, :, FF_TILE * c:FF_TILE * (c + 1)] = _dot(h2p_ref[...], wup_ref[:, col:col + FF_TILE])

    def gate(j):
        slot = j % 2
        lo = FF_TILE * j
        w = jnp.concatenate([cw_ref[:, lo:lo + FF_TILE], cw_ref[:, D_FF + lo:D_FF + lo + FF_TILE]], axis=1)
        b = jnp.concatenate([cb_ref[:, lo:lo + FF_TILE], cb_ref[:, D_FF + lo:D_FF + lo + FF_TILE]], axis=1)

        def grp(k, v):
            r = FF_SEG * k + SUBLANES * v
            return z_ref[slot, r:r + SUBLANES, :]

        down = [pltpu.roll(grp(k, FF_NGRP - 1), 1, 0) for k in range(FF_NSEG)]
        up_ = [pltpu.roll(grp(k, 0), SUBLANES - 1, 0) for k in range(FF_NSEG)]
        strips = []
        for k in range(FF_NSEG):
            g = [grp(k, v) for v in range(FF_NGRP)]
            before = down[k - 1] * joined if k > 0 else zero_grp
            after = up_[k + 1] * joined if k < FF_NSEG - 1 else zero_grp
            zm1 = jnp.concatenate([jnp.where(is_first, before, down[k])] + g[:-1], axis=0)
            zp1 = jnp.concatenate(g[1:] + [jnp.where(is_last, after, up_[k])], axis=0)
            zc = b + zm1 * w[0:1] + jnp.concatenate(g, axis=0) * w[1:2] + zp1 * w[2:3]
            a = zc[:, :FF_TILE]
            t = jnp.tanh(a * (GELU_C0 + GELU_C1 * (a * a)))
            strips.append((a * (0.5 + 0.5 * t) * zc[:, FF_TILE:]).astype(BF16))
        hid_ref[:, lo:lo + FF_TILE] = jnp.concatenate(strips, axis=0)

    up(0)
    for j in range(n_tiles):
        if j + 1 < n_tiles:
            up(j + 1)
        gate(j)
    acc = _dot(hid_ref[...], wdn_ref[...])
    acc = jnp.swapaxes(acc.reshape(FF_NSEG, FF_NGRP, SUBLANES, D_MODEL), 1, 2).reshape(TB, D_MODEL)
    g2 = mod_ref[...][:, 5 * D_MODEL:6 * D_MODEL]
    x2 = x1_ref[...] + g2 * acc
    if final:
        y = _rms(x2, fg_ref[...])
        yp_ref, ys_ref = o_refs

        @pl.when(i < NB_CTX)
        def _():
            yp_ref[...] = y

        @pl.when(i >= NB_CTX)
        def _():
            ys_ref[...] = y
    else:
        o_refs[0][...] = x2


def _ffn(x1, h2, mods, perm, wup_bf, conv_w, conv_b, wdn_bf, final_g, l, *, final):
    if final:
        out_specs = tuple(_x_specs(True))
        out_shape = (jax.ShapeDtypeStruct((N_CTX, D_MODEL), F32), jax.ShapeDtypeStruct((N_LAT, D_MODEL), F32))
    else:
        out_specs = pl.BlockSpec((TB, D_MODEL), lambda i: (i, 0))
        out_shape = jax.ShapeDtypeStruct((N_TOK, D_MODEL), F32)
    return pl.pallas_call(
        functools.partial(_ffn_kernel, final=final),
        grid=(N_TOK // TB,),
        in_specs=[
            pl.BlockSpec((TB, D_MODEL), lambda i: (i, 0)),
            pl.BlockSpec((TB, D_MODEL), lambda i: (i, 0)),
            _mod_spec(l), _const_spec((TB, TB)),
            _layer_spec((D_MODEL, 2 * D_FF), l, single=True), _layer_spec((3, 2 * D_FF), l),
            _layer_spec((1, 2 * D_FF), l), _layer_spec((D_FF, D_MODEL), l, single=True),
            _const_spec((1, D_MODEL)),
        ],
        out_specs=out_specs,
        out_shape=out_shape,
        scratch_shapes=[pltpu.VMEM((TB, D_MODEL), BF16), pltpu.VMEM((2, TB, 2 * FF_TILE), F32),
                        pltpu.VMEM((TB, D_FF), BF16)],
        compiler_params=pltpu.CompilerParams(vmem_limit_bytes=VMEM_LIMIT),
        name="conv_ffn",
    )(x1, h2, mods, perm, wup_bf, conv_w, conv_b, wdn_bf, final_g.reshape(1, D_MODEL))


def kernel(x_prompt, x_sample, state_s5_re, state_s5_im, state_hgrn, c, c_ctx, w_ada, b_ada, norm1_g,
           norm2_g, w_in, s5_lam_re, s5_lam_im, s5_log_dt, s5_b_re, s5_b_im, s5_c_re, s5_c_im, s5_d,
           s5_w_glu, hg_lb_logits, fn_w, gm_norm_g, gm_ws, gm_bs, grp_norm_g, w_out, ffn_w_up,
           ffn_conv_w, ffn_conv_b, ffn_w_down, final_norm_g):
    xs = (x_prompt.reshape(N_CTX, D_MODEL), x_sample.reshape(N_LAT, D_MODEL))
    cvecs = jnp.concatenate([c_ctx[None], c, jnp.zeros((8 - 1 - DEC_BATCH, D_MODEL), F32)], axis=0)
    mods = _ada(cvecs, w_ada, b_ada).reshape(DEPTH, 8, 1, N_MOD * D_MODEL)

    lb_p = jax.nn.softmax(hg_lb_logits.astype(F32), axis=0)
    lbs = jnp.maximum(jnp.cumsum(lb_p, axis=0) - lb_p[0], 0.0)
    lbps = jnp.stack([jnp.log(lbs), jnp.log1p(-lbs), 1.0 - lbs], axis=2)

    hg_consts_np = _hgrn_consts()
    hg_consts = (jnp.asarray(hg_consts_np[0], BF16), jnp.asarray(hg_consts_np[1], F32),
                 jnp.asarray(hg_consts_np[2], F32))
    cs_np, dft_ctx_np = _dft_consts(SEQ)
    _, dft_lat_np = _dft_consts(DEC_SEQ)
    cs = jnp.asarray(cs_np, F32).astype(BF16)
    dft_ctx = jnp.asarray(dft_ctx_np, F32).astype(BF16)
    dft_lat = jnp.asarray(dft_lat_np, F32).astype(BF16)
    gm_hm = jnp.asarray(np.kron(np.eye(GM_HEADS), np.ones((GM_CHUNK, GROUP_W // GM_HEADS))), BF16)
    bdm = jnp.asarray(np.kron(np.eye(HG_HEADS), np.ones((HG_DK, HG_DK))) / HG_DK, BF16)

    tables = _s5_tables(s5_lam_re, s5_lam_im, s5_log_dt, s5_b_re, s5_b_im, s5_c_re, s5_c_im)
    s5_m, s5_f, s5_e = _s5_prep(tables, jnp.asarray(_s5_shift_consts(), BF16))
    s5_al = tables[-1]
    s5_h0 = jnp.stack([state_s5_re, state_s5_im]).astype(F32).transpose(2, 3, 0, 1, 4, 5)
    s5_h0 = s5_h0.reshape(DEPTH, 2, 2, DEC_BATCH, S5_SW)
    hg_s0 = jnp.einsum('bldhkv,hg->lbdhvgk', state_hgrn.astype(F32), jnp.eye(HG_HEADS, dtype=F32))
    hg_s0 = hg_s0.reshape(DEPTH, DEC_BATCH, 2, HG_W, HG_W)

    def rows(a):
        return a.reshape(DEPTH, 1, a.shape[-1])

    norm1_r, norm2_r, gn_r, s5_d_r, gmn_r, cb_r = (rows(a) for a in (
        norm1_g, norm2_g, grp_norm_g, s5_d, gm_norm_g, ffn_conv_b))
    gm_w = gm_ws.transpose(0, 2, 1, 3).reshape(DEPTH, GM_CHUNK, GM_HEADS * GM_CHUNK).astype(BF16)
    gm_b = jnp.repeat(gm_bs.transpose(0, 2, 1), GROUP_W // GM_HEADS, axis=2)
    ff_perm = jnp.asarray(_ffn_perm_consts(), BF16)
    wup_bf = ffn_w_up.astype(BF16)
    wdn_bf = ffn_w_down.astype(BF16)

    new_re, new_im, new_hg = [], [], []
    for l in range(DEPTH):
        proj = _inproj(xs, mods, norm1_r, w_in, l)

        ylo, yhi, hfin = _s5(proj, s5_m, s5_f, s5_e, s5_al, s5_h0, l)
        hfin = hfin.reshape(2, 2, BATCH, S5_G, S5_N).transpose(1, 2, 0, 3, 4)
        new_re.append(hfin[0])
        new_im.append(hfin[1])

        of, ob, sfin = _hgrn(proj, lbps, hg_consts, hg_s0, l)
        new_hg.append(sfin)

        yc = _fnet(proj, cs, dft_ctx, fn_w, None, l, n_seq=BATCH, seq_len=SEQ, tok0=0)
        yc = _fnet(proj, cs, dft_lat, fn_w, yc, l, n_seq=DEC_BATCH, seq_len=DEC_SEQ, tok0=N_CTX)

        od = _gmlp(proj, gmn_r, gm_w, gm_b, gm_hm, gn_r, l)

        x1, h2 = _outproj(xs, proj, ylo, yhi, of, ob, yc, od, mods, s5_d_r, s5_w_glu, gn_r, bdm, w_out,
                          norm2_r, l)
        res = _ffn(x1, h2, mods, ff_perm, wup_bf, ffn_conv_w, cb_r, wdn_bf, final_norm_g, l,
                   final=(l == DEPTH - 1))
        xs = (res,)

    y_prompt = res[0].reshape(BATCH, SEQ, D_MODEL)
    y_sample = res[1].reshape(DEC_BATCH, DEC_SEQ, D_MODEL)
    return (y_prompt, y_sample, jnp.stack(new_re, axis=1), jnp.stack(new_im, axis=1),
            jnp.stack(new_hg, axis=1))
```

```python
import functools
import math

import numpy as np
import jax
import jax.numpy as jnp
from jax import lax
from jax.experimental import pallas as pl
from jax.experimental.pallas import tpu as pltpu

D_MODEL = 1024
BATCH = 16
SEQ = 256
DEPTH = 2
DEC_BATCH = 2
DEC_SEQ = 2048
GRID_W = 64
GROUP_W = 256
S5_P = 16
S5_G = 16
S5_N = 64
HG_HEADS = 4
HG_DK = 64
GM_HEADS = 4
GM_CHUNK = 128
D_FF = 2816
N_MOD = 6
D_IN = 9 * GROUP_W
EPS = 1e-6
LAM_RE_MAX = -1e-4

N_CTX = BATCH * SEQ
N_LAT = DEC_BATCH * DEC_SEQ
N_TOK = N_CTX + N_LAT
TB = 256
NB_CTX = N_CTX // TB
NB_LAT_SEQ = DEC_SEQ // TB
S5_L = 16
HG_L = 64
FF_TILE = 256
LANES = 128
VMEM_LIMIT = 56 * 1024 * 1024

F32 = jnp.float32
BF16 = jnp.bfloat16

COL_XA, COL_HQ, COL_HF_FWD, COL_HF_BWD, COL_HI, COL_HGATE, COL_XC, COL_GU, COL_GV = range(9)


def _mod_row(i):
    return jnp.where(i < NB_CTX, 0, 1 + (i - NB_CTX) // NB_LAT_SEQ)


def _gelu(x):
    return 0.5 * x * (1.0 + jnp.tanh(0.7978845608028654 * (x + 0.044715 * (x * x * x))))


def _rms(x, g):
    return x * lax.rsqrt(jnp.mean(x * x, axis=-1, keepdims=True) + EPS) * g


def _dot(a, b):
    return jnp.dot(a, b, preferred_element_type=F32)


def _dot_nt(a, b, precision=None):
    return lax.dot_general(a, b, (((1,), (1,)), ((), ())), precision=precision, preferred_element_type=F32)


def _dot_tn(a, b):
    return lax.dot_general(a, b, (((0,), (0,)), ((), ())), preferred_element_type=F32)


def _const_spec(shape, single=False):
    kw = {"pipeline_mode": pl.Buffered(1)} if single else {}
    return pl.BlockSpec(shape, lambda *_: (0,) * len(shape), **kw)


def _layer_spec(shape, l, single=False):
    kw = {"pipeline_mode": pl.Buffered(1)} if single else {}
    return pl.BlockSpec((None,) + tuple(shape), lambda *_: (l,) + (0,) * len(shape), **kw)


def _mod_spec(l):
    return pl.BlockSpec((None, None, 1, N_MOD * D_MODEL), lambda i: (l, _mod_row(i), 0, 0))


def _x_specs(split, width=D_MODEL):
    if split:
        return [pl.BlockSpec((TB, width), lambda i: (jnp.minimum(i, NB_CTX - 1), 0)),
                pl.BlockSpec((TB, width), lambda i: (jnp.maximum(i - NB_CTX, 0), 0))]
    return [pl.BlockSpec((TB, width), lambda i: (i, 0))]


def _x_value(x_refs):
    if len(x_refs) == 1:
        return x_refs[0][...]
    return jnp.where(pl.program_id(0) < NB_CTX, x_refs[0][...], x_refs[1][...])


ADA_TN = 1536


def _ada_kernel(c_ref, w_ref, b_ref, o_ref):
    cv = c_ref[...]
    s = cv * jax.nn.sigmoid(cv)
    o_ref[0] = _dot(s.astype(BF16), w_ref[0].astype(BF16)) + b_ref[0]


def _ada(cvecs, w_ada, b_ada):
    n = N_MOD * D_MODEL
    return pl.pallas_call(
        _ada_kernel,
        grid=(DEPTH, n // ADA_TN),
        in_specs=[
            pl.BlockSpec((8, D_MODEL), lambda l, j: (0, 0)),
            pl.BlockSpec((1, D_MODEL, ADA_TN), lambda l, j: (l, 0, j)),
            pl.BlockSpec((1, 1, ADA_TN), lambda l, j: (l, 0, j)),
        ],
        out_specs=pl.BlockSpec((1, 8, ADA_TN), lambda l, j: (l, 0, j)),
        out_shape=jax.ShapeDtypeStruct((DEPTH, 8, n), F32),
        compiler_params=pltpu.CompilerParams(vmem_limit_bytes=VMEM_LIMIT),
        name="ada_mod",
    )(cvecs, w_ada, b_ada.reshape(DEPTH, 1, n))


def _inproj_kernel(*refs):
    *x_refs, mod_ref, g_ref, w_ref, o_ref, wb_ref = refs

    @pl.when(pl.program_id(0) == 0)
    def _():
        wb_ref[...] = w_ref[...].astype(BF16)

    mod = mod_ref[...]
    sh = mod[:, 0:D_MODEL]
    sc = mod[:, D_MODEL:2 * D_MODEL]
    h = _rms(_x_value(x_refs), g_ref[...]) * (1.0 + sc) + sh
    o_ref[...] = _dot(h.astype(BF16), wb_ref[...])


def _inproj(xs, mods, norm_g, w_in, l):
    return pl.pallas_call(
        _inproj_kernel,
        grid=(N_TOK // TB,),
        in_specs=_x_specs(len(xs) == 2) + [
            _mod_spec(l),
            _layer_spec((1, D_MODEL), l),
            _layer_spec((D_MODEL, D_IN), l, single=True),
        ],
        out_specs=pl.BlockSpec((TB, D_IN), lambda i: (i, 0)),
        out_shape=jax.ShapeDtypeStruct((N_TOK, D_IN), F32),
        scratch_shapes=[pltpu.VMEM((D_MODEL, D_IN), BF16)],
        compiler_params=pltpu.CompilerParams(vmem_limit_bytes=VMEM_LIMIT),
        name="in_proj",
    )(*xs, mods, norm_g, w_in)


S5_NPAIR = S5_G // 2
S5_CW = S5_L * S5_P
S5_SW = S5_G * S5_N
S5_STEP_ROWS = N_CTX // S5_L
S5_CTX_NC = SEQ // S5_L
S5_LAT_NC = DEC_SEQ // S5_L


def _s5_tables(lam_re, lam_im, log_dt, b_re, b_im, c_re, c_im):
    lr = jnp.minimum(lam_re.astype(F32), LAM_RE_MAX)
    li = lam_im.astype(F32)
    dt = jnp.exp(log_dt.astype(F32))[..., None]
    mag = jnp.exp(lr * dt)
    ang = li * dt
    ab_re = mag * jnp.cos(ang)
    ab_im = mag * jnp.sin(ang)
    den = lr * lr + li * li
    xr = ab_re - 1.0
    z_re = (xr * lr + ab_im * li) / den
    z_im = (ab_im * lr - xr * li) / den
    bb_re = z_re[..., None] * b_re - z_im[..., None] * b_im
    bb_im = z_re[..., None] * b_im + z_im[..., None] * b_re
    tau = jnp.arange(S5_L + 1, dtype=F32)[:, None, None, None, None]
    pm = jnp.exp(lr * dt * tau)
    pa = li * dt * tau
    pw_re = pm * jnp.cos(pa)
    pw_im = pm * jnp.sin(pa)
    eye2 = jnp.eye(2, dtype=F32)

    def pw_pairs(a):
        a = a.transpose(1, 2, 3, 0, 4).reshape(DEPTH, 2, S5_NPAIR, 2, S5_L + 1, S5_N)
        return a.transpose(0, 1, 2, 4, 3, 5).reshape(DEPTH, 2, S5_NPAIR, S5_L + 1, 2 * S5_N)

    def mat_pairs(a):
        a = a.reshape(DEPTH, 2, S5_NPAIR, 2, S5_P, S5_N)
        return jnp.einsum('ldjaqn,ab->ldjaqbn', a, eye2).reshape(DEPTH, 2, S5_NPAIR, 2, S5_P, 2 * S5_N)

    al = jnp.stack([pw_re[S5_L], pw_im[S5_L]], axis=2).reshape(DEPTH, 2, 2, 1, S5_SW)
    return (pw_pairs(pw_re), pw_pairs(pw_im),
            mat_pairs(bb_re.transpose(0, 1, 2, 4, 3)), mat_pairs(bb_im.transpose(0, 1, 2, 4, 3)),
            mat_pairs(c_re.astype(F32)), mat_pairs(c_im.astype(F32)), al)


def _s5_shift_consts():
    n = S5_CW
    r = np.arange(n)[:, None]
    c = np.arange(n)[None, :]
    fwd = [(c == r + S5_P * s) for s in range(S5_L)]
    bwd = [(c == r - S5_P * (S5_L - 1 - s)) for s in range(S5_L)]
    return np.stack([np.concatenate(fwd, axis=1), np.concatenate(bwd, axis=1)]).astype(np.float32)


def _s5_prep_kernel(pwr_ref, pwi_ref, br_ref, bi_ref, cr_ref, ci_ref, scat_ref, m_ref, f_ref, e_ref, k_ref):
    L = S5_L
    hp = lax.Precision.HIGHEST
    sw = 2 * S5_N

    def pair_body(j, carry):
        for d in range(2):
            pwr = pwr_ref[0, d, j]
            pwi = pwi_ref[0, d, j]
            for gl in range(2):
                br, bi = br_ref[0, d, j, gl], bi_ref[0, d, j, gl]
                cr, ci = cr_ref[0, d, j, gl], ci_ref[0, d, j, gl]
                ca_r, ca_i, f_r, f_i, e_r, e_i = [], [], [], [], [], []
                for t in range(L):
                    kk = t if d == 0 else L - 1 - t
                    pr, pi = pwr[kk:kk + 1], pwi[kk:kk + 1]
                    ca_r.append(cr * pr - ci * pi)
                    ca_i.append(cr * pi + ci * pr)
                    kf = L - 1 - t if d == 0 else t
                    pr, pi = pwr[kf:kf + 1], pwi[kf:kf + 1]
                    f_r.append(br * pr - bi * pi)
                    f_i.append(br * pi + bi * pr)
                    ke = t + 1 if d == 0 else L - t
                    pr, pi = pwr[ke:ke + 1], pwi[ke:ke + 1]
                    e_r.append(cr * pr - ci * pi)
                    e_i.append(-(cr * pi + ci * pr))
                cat = lambda xs: jnp.concatenate(xs, axis=0)
                k = _dot_nt(br, cat(ca_r), hp) - _dot_nt(bi, cat(ca_i), hp)
                g16 = pl.multiple_of((2 * j + gl) * S5_P, S5_P)
                k_ref[d, pl.ds(g16, S5_P), :] = k
                rows = slice(S5_CW * gl, S5_CW * (gl + 1))
                f_ref[0, d, j, rows, 0:sw] = cat(f_r).astype(BF16)
                f_ref[0, d, j, rows, sw:2 * sw] = cat(f_i).astype(BF16)
                e_ref[0, d, 0, j, rows, :] = cat(e_r).astype(BF16)
                e_ref[0, d, 1, j, rows, :] = cat(e_i).astype(BF16)
        return carry

    lax.fori_loop(0, S5_NPAIR, pair_body, 0)
    kf = k_ref[0].astype(BF16)
    kb = k_ref[1].astype(BF16)
    for s in range(L):
        cols = slice(S5_CW * s, S5_CW * (s + 1))
        res = _dot(kf, scat_ref[0, :, cols]) + _dot(kb, scat_ref[1, :, cols])
        for g in range(S5_G):
            m_ref[0, g, S5_P * s:S5_P * (s + 1), :] = res[S5_P * g:S5_P * (g + 1), :].astype(BF16)


def _s5_prep(tables, scat):
    pwr, pwi, br, bi, cr, ci, _ = tables

    def lspec(shape):
        return pl.BlockSpec((1,) + shape, lambda l: (l,) + (0,) * len(shape))

    pw_shape = (2, S5_NPAIR, S5_L + 1, 2 * S5_N)
    mat_shape = (2, S5_NPAIR, 2, S5_P, 2 * S5_N)
    return pl.pallas_call(
        _s5_prep_kernel,
        grid=(DEPTH,),
        in_specs=[lspec(pw_shape), lspec(pw_shape), lspec(mat_shape), lspec(mat_shape), lspec(mat_shape),
                  lspec(mat_shape), _const_spec(scat.shape)],
        out_specs=(lspec((S5_G, S5_CW, S5_CW)), lspec((2, S5_NPAIR, 2 * S5_CW, 4 * S5_N)),
                   lspec((2, 2, S5_NPAIR, 2 * S5_CW, 2 * S5_N))),
        out_shape=(jax.ShapeDtypeStruct((DEPTH, S5_G, S5_CW, S5_CW), BF16),
                   jax.ShapeDtypeStruct((DEPTH, 2, S5_NPAIR, 2 * S5_CW, 4 * S5_N), BF16),
                   jax.ShapeDtypeStruct((DEPTH, 2, 2, S5_NPAIR, 2 * S5_CW, 2 * S5_N), BF16)),
        scratch_shapes=[pltpu.VMEM((2, S5_G * S5_P, S5_CW), F32)],
        compiler_params=pltpu.CompilerParams(vmem_limit_bytes=VMEM_LIMIT),
        name="s5_prep",
    )(pwr, pwi, br, bi, cr, ci, scat)


def _s5_kernel(xlo_ref, xhi_ref, m_ref, f_ref, e_ref, al_ref, h0_ref, ylo_ref, yhi_ref, hfin_ref,
               uy_ref, zh_ref):
    R = S5_STEP_ROWS
    pw = 2 * S5_CW
    sw = 2 * S5_N
    gph = S5_G // 2
    step = pl.program_id(0)

    for half, x_ref in enumerate((xlo_ref, xhi_ref)):
        for t in range(S5_L):
            xt = x_ref[pl.ds(t, R, stride=S5_L), :]
            for gl in range(gph):
                lo = S5_CW * (gph * half + gl) + S5_P * t
                uy_ref[:, lo:lo + S5_P] = xt[:, S5_P * gl:S5_P * (gl + 1)]

    for j in range(S5_NPAIR):
        ub = uy_ref[:, pw * j:pw * (j + 1)].astype(BF16)
        for d in range(2):
            o = _dot(ub, f_ref[d, j])
            zh_ref[d, 0, :, sw * j:sw * (j + 1)] = o[:, :sw]
            zh_ref[d, 1, :, sw * j:sw * (j + 1)] = o[:, sw:]

    def scan_group(seq_rows, nc, init, fin_rows):
        chains = [(d, r0) for d in range(2) for r0 in seq_rows]

        def body(c, carry):
            out = []
            for k, (d, r0) in enumerate(chains):
                hr, hi = carry[2 * k], carry[2 * k + 1]
                r = r0 + (c if d == 0 else nc - 1 - c)
                zr = zh_ref[d, 0, pl.ds(r, 1), :]
                zi = zh_ref[d, 1, pl.ds(r, 1), :]
                zh_ref[d, 0, pl.ds(r, 1), :] = hr
                zh_ref[d, 1, pl.ds(r, 1), :] = hi
                ar = al_ref[d, 0]
                ai = al_ref[d, 1]
                out.append(ar * hr - ai * hi + zr)
                out.append(ar * hi + ai * hr + zi)
            return tuple(out)

        fin = lax.fori_loop(0, nc, body, tuple(init))
        if fin_rows is not None:
            for k, (d, _) in enumerate(chains):
                s = fin_rows[k % len(seq_rows)]
                hfin_ref[d, 0, s:s + 1, :] = fin[2 * k]
                hfin_ref[d, 1, s:s + 1, :] = fin[2 * k + 1]

    @pl.when(step == 0)
    def _():
        zero = jnp.zeros((1, S5_SW), F32)
        for s0 in range(0, BATCH, 4):
            seqs = list(range(s0, s0 + 4))
            scan_group([s * S5_CTX_NC for s in seqs], S5_CTX_NC, [zero] * 16, seqs)

    @pl.when(step == 1)
    def _():
        init = []
        for d in range(2):
            for s in range(DEC_BATCH):
                init.append(h0_ref[d, 0, s:s + 1, :])
                init.append(h0_ref[d, 1, s:s + 1, :])
        scan_group([s * S5_LAT_NC for s in range(DEC_BATCH)], S5_LAT_NC, init, None)

    for j in range(S5_NPAIR):
        acc = None
        for d in range(2):
            for ri in range(2):
                hb = zh_ref[d, ri, :, sw * j:sw * (j + 1)].astype(BF16)
                t = _dot_nt(hb, e_ref[d, ri, j])
                acc = t if acc is None else acc + t
        for gl in range(2):
            g = 2 * j + gl
            ug = uy_ref[:, S5_CW * g:S5_CW * (g + 1)].astype(BF16)
            uy_ref[:, S5_CW * g:S5_CW * (g + 1)] = acc[:, S5_CW * gl:S5_CW * (gl + 1)] + _dot(ug, m_ref[g])

    for half, y_ref in enumerate((ylo_ref, yhi_ref)):
        for t in range(S5_L):
            pieces = []
            for gl in range(gph):
                lo = S5_CW * (gph * half + gl) + S5_P * t
                pieces.append(uy_ref[:, lo:lo + S5_P])
            y_ref[pl.ds(t, R, stride=S5_L), :] = jnp.concatenate(pieces, axis=1)


def _s5(proj, m, f2, e2, al, h0, l):
    half = pl.BlockSpec((N_CTX, LANES), lambda s: (s, 0))
    return pl.pallas_call(
        _s5_kernel,
        grid=(2,),
        in_specs=[pl.BlockSpec((N_CTX, LANES), lambda s: (s, 0)),
                  pl.BlockSpec((N_CTX, LANES), lambda s: (s, 1)),
                  _layer_spec(m.shape[1:], l, single=True), _layer_spec(f2.shape[1:], l, single=True),
                  _layer_spec(e2.shape[1:], l, single=True), _layer_spec(al.shape[1:], l),
                  _layer_spec(h0.shape[1:], l)],
        out_specs=(half, half, _const_spec((2, 2, BATCH, S5_SW))),
        out_shape=(jax.ShapeDtypeStruct((N_TOK, LANES), F32),
                   jax.ShapeDtypeStruct((N_TOK, LANES), F32),
                   jax.ShapeDtypeStruct((2, 2, BATCH, S5_SW), F32)),
        scratch_shapes=[pltpu.VMEM((S5_STEP_ROWS, S5_G * S5_CW), F32),
                        pltpu.VMEM((2, 2, S5_STEP_ROWS, S5_SW), F32)],
        compiler_params=pltpu.CompilerParams(vmem_limit_bytes=VMEM_LIMIT),
        name="s5_mixer",
    )(proj, proj, m, f2, e2, al, h0)


HG_NLEV = int(math.log2(HG_L))
HG_W = HG_HEADS * HG_DK
HG_CPS = 4
HG_BLK = HG_CPS * HG_L
HG_NB_CTX = N_CTX // HG_BLK
HG_NC_CTX = SEQ // HG_BLK
HG_NC_LAT = DEC_SEQ // HG_BLK
assert HG_L == HG_DK


def _hgrn_consts():
    L = HG_L
    w = np.zeros((HG_NLEV + 2, L, L), np.float32)
    mask = np.zeros((HG_NLEV + 1, L, L), np.float32)
    for lev in range(HG_NLEV):
        blk = L >> lev
        half = blk // 2
        for t in range(L):
            p, o = divmod(t, blk)
            bd = p * blk + half - 1
            if o >= half:
                w[lev, t, bd + 1:t + 1] = 1.0
            else:
                w[lev, t, t + 1:bd + 1] = 1.0
        jj, ii = np.meshgrid(np.arange(L), np.arange(L), indexing='ij')
        mask[lev] = ((jj // blk == ii // blk) & (jj % blk >= half) & (ii % blk < half)).astype(np.float32)
    mask[HG_NLEV] = np.eye(L, dtype=np.float32)
    for t in range(L):
        w[HG_NLEV, t, :t + 1] = 1.0
        w[HG_NLEV + 1, t, t + 1:] = 1.0
    out = []
    for wd, md in ((w, mask), (w[:, ::-1, ::-1], mask[:, ::-1, ::-1])):
        wflat = wd.reshape((HG_NLEV + 2) * L, L)
        out.append((np.concatenate([wflat] * 3, axis=1), np.tile(md, (1, 1, HG_HEADS))))
    wcat = np.stack([out[0][0], out[1][0]])
    mask4 = np.stack([out[0][1], out[1][1]])
    hm = np.kron(np.eye(HG_HEADS, dtype=np.float32), np.ones((HG_DK, HG_DK), np.float32))
    return wcat, mask4, hm


def _hg_pos(i):
    is_ctx = i < HG_NB_CTX
    c = jnp.where(is_ctx, i % HG_NC_CTX, (i - HG_NB_CTX) % HG_NC_LAT)
    nc = jnp.where(is_ctx, HG_NC_CTX, HG_NC_LAT)
    return is_ctx, c, nc


def _hg_bwd_blk(i):
    _, c, nc = _hg_pos(i)
    return i + nc - 1 - 2 * c


def _hg_local(d, q, z, v, lbp, wcat_ref, mask_ref, hm):
    L = HG_L
    az = jnp.abs(z)
    sp = jnp.log1p(jnp.exp(-az))
    ls = jnp.minimum(z, 0.0) - sp
    kk = lbp[2:3] * jnp.exp(jnp.minimum(-z, 0.0) - sp)
    a = lbp[0:1]
    b = lbp[1:2] + ls
    logf = jnp.maximum(a, b) + jnp.log1p(jnp.exp(-jnp.abs(a - b)))
    p0 = logf.astype(BF16)
    r0 = logf - p0.astype(F32)
    p1 = r0.astype(BF16)
    p2 = (r0 - p1.astype(F32)).astype(BF16)
    ex = jnp.exp(_dot(wcat_ref[d], jnp.concatenate([p0, p1, p2], axis=0)))
    hmb = hm.astype(BF16)

    def bd4(x):
        return jnp.concatenate([x] * HG_HEADS, axis=0) * hmb

    sc = None
    for lev in range(HG_NLEV + 1):
        if lev < HG_NLEV:
            al = ex[L * lev:L * (lev + 1)]
            lhs = (q * al).astype(BF16)
            rhs = (kk * al).astype(BF16)
        else:
            lhs = q.astype(BF16)
            rhs = kk.astype(BF16)
        t = _dot_nt(lhs, bd4(rhs)) * mask_ref[d, lev]
        sc = t if sc is None else sc + t
    vb = v.astype(BF16)
    o = _dot(sc.astype(BF16), bd4(vb))
    eq = ex[L * HG_NLEV:L * (HG_NLEV + 1)]
    qe = (q * eq).astype(BF16)
    ke = (kk * ex[L * (HG_NLEV + 1):L * (HG_NLEV + 2)]).astype(BF16)
    g = eq[L - 1:L] if d == 0 else eq[0:1]
    return o, qe, g, _dot_tn(vb, ke) * hm


def _hgrn_kernel(qf, zf, vf, qb, zb, vb, lb_ref, w_ref, mask_ref, hm_ref, s0_ref,
                 of_ref, ob_ref, sfin_ref, st_ref):
    is_ctx, c, nc = _hg_pos(pl.program_id(0))

    @pl.when(c == 0)
    def _():
        st_ref[...] = jnp.where(is_ctx, 0.0, s0_ref[0])

    hm = hm_ref[...]
    for d, (q_ref, z_ref, v_ref, o_ref) in enumerate(((qf, zf, vf, of_ref), (qb, zb, vb, ob_ref))):
        rows = [slice(HG_L * k, HG_L * (k + 1)) for k in range(HG_CPS)]
        loc = [_hg_local(d, q_ref[r, :], z_ref[r, :], v_ref[r, :], lb_ref[d], w_ref, mask_ref, hm)
               for r in rows]
        st = st_ref[d]
        for k in (range(HG_CPS) if d == 0 else reversed(range(HG_CPS))):
            o, qe, g, upd = loc[k]
            o_ref[rows[k], :] = o + _dot_nt(qe, st.astype(BF16))
            st = st * g + upd
        st_ref[d] = st

    @pl.when(jnp.logical_and(is_ctx, c == nc - 1))
    def _():
        for d in range(2):
            s_kv = st_ref[d].T
            for h in range(HG_HEADS):
                sfin_ref[0, d, h] = s_kv[HG_DK * h:HG_DK * (h + 1), HG_DK * h:HG_DK * (h + 1)]


def _hgrn(proj, lbp, consts, s0, l):
    wcat, mask4, hm = consts

    def fwd(col):
        return pl.BlockSpec((HG_BLK, HG_W), lambda i: (i, col))

    def bwd(col):
        return pl.BlockSpec((HG_BLK, HG_W), lambda i: (_hg_bwd_blk(i), col))

    return pl.pallas_call(
        _hgrn_kernel,
        grid=(N_TOK // HG_BLK,),
        in_specs=[fwd(COL_HQ), fwd(COL_HF_FWD), fwd(COL_HI), bwd(COL_HQ), bwd(COL_HF_BWD), bwd(COL_HI),
                  _layer_spec(lbp.shape[1:], l), _const_spec(wcat.shape), _const_spec(mask4.shape),
                  _const_spec(hm.shape),
                  pl.BlockSpec((None, 1, 2, HG_W, HG_W),
                               lambda i: (l, jnp.maximum(i - HG_NB_CTX, 0) // HG_NC_LAT, 0, 0, 0))],
        out_specs=(pl.BlockSpec((HG_BLK, HG_W), lambda i: (i, 0)),
                   pl.BlockSpec((HG_BLK, HG_W), lambda i: (_hg_bwd_blk(i), 0)),
                   pl.BlockSpec((1, 2, HG_HEADS, HG_DK, HG_DK),
                                lambda i: (jnp.minimum(i // HG_NC_CTX, BATCH - 1), 0, 0, 0, 0))),
        out_shape=(jax.ShapeDtypeStruct((N_TOK, HG_W), F32),
                   jax.ShapeDtypeStruct((N_TOK, HG_W), F32),
                   jax.ShapeDtypeStruct((BATCH, 2, HG_HEADS, HG_DK, HG_DK), F32)),
        scratch_shapes=[pltpu.VMEM((2, HG_W, HG_W), F32)],
        compiler_params=pltpu.CompilerParams(vmem_limit_bytes=VMEM_LIMIT),
        name="hgrn_mixer",
    )(proj, proj, proj, proj, proj, proj, lbp, wcat, mask4, hm, s0)


FN_CTX_SPG = 4
FN_LAT_TR = 512


def _dft_consts(t_len):
    n = GROUP_W // 4
    k = np.arange(n)
    ang = 2.0 * np.pi * ((k[:, None] * k[None, :]) % n) / n
    eye = np.eye(4)
    cs = np.concatenate([np.kron(eye, np.cos(ang)), np.kron(eye, np.sin(ang))], axis=1) / math.sqrt(n)
    t = np.arange(t_len)
    angt = 2.0 * np.pi * ((t[:, None] * t[None, :]) % t_len) / t_len
    dft = np.concatenate([np.cos(angt), -np.sin(angt)], axis=1) / math.sqrt(t_len)
    return cs.astype(np.float32), dft.astype(np.float32)


def _fnet_kernel(x_ref, cs_ref, dft_ref, w_ref, o_ref, r_ref, *, t_len, spg, tr):
    @pl.when(pl.program_id(1) == 0)
    def _():
        t = _dot(x_ref[...].astype(BF16), cs_ref[...])
        for s in range(spg):
            rows = slice(t_len * s, t_len * (s + 1))
            r_ref[s, 0:t_len, :] = t[rows, :GROUP_W].astype(BF16)
            r_ref[s, t_len:2 * t_len, :] = t[rows, GROUP_W:].astype(BF16)

    w = w_ref[...].astype(BF16)
    for s in range(spg):
        y = _dot(dft_ref[...], r_ref[s])
        o_ref[tr * s:tr * (s + 1), :] = _dot(y.astype(BF16), w)


def _fnet(proj, cs, dft, fn_w, l, *, n_seq, seq_len, tok0, spg, tr):
    nj = seq_len // tr
    assert spg == 1 or nj == 1
    gb0 = tok0 // (spg * seq_len)
    return pl.pallas_call(
        functools.partial(_fnet_kernel, t_len=seq_len, spg=spg, tr=tr),
        grid=(n_seq // spg, nj),
        in_specs=[
            pl.BlockSpec((spg * seq_len, GROUP_W), lambda g, j: (gb0 + g, COL_XC)),
            _const_spec((GROUP_W, 2 * GROUP_W)),
            pl.BlockSpec((tr, 2 * seq_len), lambda g, j: (j, 0)),
            _layer_spec((GROUP_W, GROUP_W), l),
        ],
        out_specs=pl.BlockSpec((spg * tr, GROUP_W), lambda g, j: (g * nj + j, 0)),
        out_shape=jax.ShapeDtypeStruct((n_seq * seq_len, GROUP_W), F32),
        scratch_shapes=[pltpu.VMEM((spg, 2 * seq_len, GROUP_W), BF16)],
        compiler_params=pltpu.CompilerParams(vmem_limit_bytes=VMEM_LIMIT),
        name="fnet_mixer",
    )(proj, cs, dft, fn_w)


GM_TB = 512


def _gmlp_kernel(gu_ref, gv_ref, g_ref, w_ref, bias_ref, hm_ref, gn_ref, o_ref):
    hmb = hm_ref[...]
    for c in range(GM_TB // GM_CHUNK):
        rows = slice(GM_CHUNK * c, GM_CHUNK * (c + 1))
        gu = _gelu(gu_ref[rows, :])
        gv = _rms(_gelu(gv_ref[rows, :]), g_ref[...]).astype(BF16)
        g4 = jnp.concatenate([gv] * GM_HEADS, axis=0) * hmb
        sp = _dot(w_ref[...], g4) + bias_ref[...]
        o_ref[rows, :] = _rms(gu * sp, gn_ref[...])


def _gmlp(proj, gm_norm_g, wcat, bias, hm, gn, l):
    return pl.pallas_call(
        _gmlp_kernel,
        grid=(N_TOK // GM_TB,),
        in_specs=[
            pl.BlockSpec((GM_TB, GROUP_W), lambda i: (i, COL_GU)),
            pl.BlockSpec((GM_TB, GROUP_W), lambda i: (i, COL_GV)),
            _layer_spec((1, GROUP_W), l),
            _layer_spec((GM_CHUNK, GM_HEADS * GM_CHUNK), l),
            _layer_spec((GM_CHUNK, GROUP_W), l),
            _const_spec((GM_HEADS * GM_CHUNK, GROUP_W)),
            pl.BlockSpec((None, 1, GROUP_W), lambda i: (l, 0, 3)),
        ],
        out_specs=pl.BlockSpec((GM_TB, GROUP_W), lambda i: (i, 0)),
        out_shape=jax.ShapeDtypeStruct((N_TOK, GROUP_W), F32),
        compiler_params=pltpu.CompilerParams(vmem_limit_bytes=VMEM_LIMIT),
        name="gmlp_mixer",
    )(proj, proj, gm_norm_g, wcat, bias, hm, gn)


def _outproj_kernel(*refs):
    (xa_ref, hgate_ref, ylo_ref, yhi_ref, of_ref, ob_ref, ycc_ref, ycl_ref, od_ref, mod_ref, d_ref,
     wglu_ref, gn_ref, bdm_ref, wout_ref, n2_ref, x1_ref, h2_ref, wob_ref) = refs[-19:]
    x_refs = refs[:-19]

    @pl.when(pl.program_id(0) == 0)
    def _():
        wob_ref[...] = wout_ref[...].astype(BF16)

    gn = gn_ref[...]
    ys = jnp.concatenate([ylo_ref[...], yhi_ref[...]], axis=1)
    y5 = _gelu(ys + d_ref[...] * xa_ref[...])
    glu = jax.nn.sigmoid(_dot(y5.astype(BF16), wglu_ref[...].astype(BF16)))
    out_a = _rms(y5 * glu, gn[:, 0:GROUP_W])
    o = of_ref[...] + ob_ref[...]
    o2 = o * o
    o2h = o2.astype(BF16)
    o2l = (o2 - o2h.astype(F32)).astype(BF16)
    ms = _dot(o2h, bdm_ref[...]) + _dot(o2l, bdm_ref[...])
    hg = hgate_ref[...]
    out_b = o * lax.rsqrt(ms + EPS) * gn[:, GROUP_W:2 * GROUP_W] * (hg * jax.nn.sigmoid(hg))
    out_c = _rms(_x_value((ycc_ref, ycl_ref)), gn[:, 2 * GROUP_W:3 * GROUP_W])
    out_d = od_ref[...]
    m = None
    for k, part in enumerate((out_a, out_b, out_c, out_d)):
        t = _dot(part.astype(BF16), wob_ref[GROUP_W * k:GROUP_W * (k + 1), :])
        m = t if m is None else m + t
    mod = mod_ref[...]
    g1 = mod[:, 2 * D_MODEL:3 * D_MODEL]
    sh2 = mod[:, 3 * D_MODEL:4 * D_MODEL]
    sc2 = mod[:, 4 * D_MODEL:5 * D_MODEL]
    x1 = _x_value(x_refs) + g1 * m
    x1_ref[...] = x1
    h2_ref[...] = (_rms(x1, n2_ref[...]) * (1.0 + sc2) + sh2).astype(BF16)


def _outproj(xs, proj, ylo, yhi, of, ob, yc, od, mods, s5_d, wglu, gn, bdm, wout, norm2_g, l):
    def tok(width, col=0):
        return pl.BlockSpec((TB, width), lambda i: (i, col))

    return pl.pallas_call(
        _outproj_kernel,
        grid=(N_TOK // TB,),
        in_specs=_x_specs(len(xs) == 2) + [
            tok(GROUP_W, COL_XA), tok(GROUP_W, COL_HGATE), tok(LANES), tok(LANES),
            tok(GROUP_W), tok(GROUP_W), *_x_specs(True, GROUP_W), tok(GROUP_W),
            _mod_spec(l),
            _layer_spec((1, GROUP_W), l), _layer_spec((GROUP_W, GROUP_W), l), _layer_spec((1, D_MODEL), l),
            _const_spec((GROUP_W, GROUP_W)), _layer_spec((D_MODEL, D_MODEL), l, single=True),
            _layer_spec((1, D_MODEL), l),
        ],
        out_specs=(tok(D_MODEL), tok(D_MODEL)),
        out_shape=(jax.ShapeDtypeStruct((N_TOK, D_MODEL), F32),
                   jax.ShapeDtypeStruct((N_TOK, D_MODEL), BF16)),
        scratch_shapes=[pltpu.VMEM((D_MODEL, D_MODEL), BF16)],
        compiler_params=pltpu.CompilerParams(vmem_limit_bytes=VMEM_LIMIT),
        name="out_proj",
    )(*xs, proj, proj, ylo, yhi, of, ob, *yc, od, mods, s5_d, wglu, gn, bdm, wout, norm2_g)


FF_SEG = GRID_W
FF_NSEG = TB // FF_SEG
SUBLANES = 8
FF_NGRP = FF_SEG // SUBLANES
GELU_C0 = 0.7978845608028654
GELU_C1 = GELU_C0 * 0.044715


def _ffn_perm_consts():
    p = np.zeros((TB, TB), np.float32)
    for k in range(FF_NSEG):
        for t in range(FF_SEG):
            p[FF_SEG * k + SUBLANES * (t % FF_NGRP) + t // FF_NGRP, FF_SEG * k + t] = 1.0
    return p


def _ffn_kernel(x1_ref, h2_ref, mod_ref, perm_ref, wup_ref, cw_ref, cb_ref, wdn_ref, fg_ref, *rest, final):
    *o_refs, h2p_ref, z_ref, hid_ref = rest
    i = pl.program_id(0)
    joined = (i < NB_CTX).astype(F32)
    n_tiles = D_FF // FF_TILE
    sub = lax.broadcasted_iota(jnp.int32, (SUBLANES, 2 * FF_TILE), 0)
    is_first = sub == 0
    is_last = sub == SUBLANES - 1
    zero_grp = jnp.zeros((SUBLANES, 2 * FF_TILE), F32)
    h2p_ref[...] = _dot(perm_ref[...], h2_ref[...]).astype(BF16)

    def up(j):
        lo = FF_TILE * j
        for c, col in enumerate((lo, D_FF + lo)):
            z_ref[j % 2, :, FF_TILE * c:FF_TILE * (c + 1)] = _dot(h2p_ref[...], wup_ref[:, col:col + FF_TILE])

    def gate(j):
        slot = j % 2
        lo = FF_TILE * j
        w = jnp.concatenate([cw_ref[:, lo:lo + FF_TILE], cw_ref[:, D_FF + lo:D_FF + lo + FF_TILE]], axis=1)
        b = jnp.concatenate([cb_ref[:, lo:lo + FF_TILE], cb_ref[:, D_FF + lo:D_FF + lo + FF_TILE]], axis=1)

        def grp(k, v):
            r = FF_SEG * k + SUBLANES * v
            return z_ref[slot, r:r + SUBLANES, :]

        down = [pltpu.roll(grp(k, FF_NGRP - 1), 1, 0) for k in range(FF_NSEG)]
        up_ = [pltpu.roll(grp(k, 0), SUBLANES - 1, 0) for k in range(FF_NSEG)]
        strips = []
        for k in range(FF_NSEG):
            g = [grp(k, v) for v in range(FF_NGRP)]
            before = down[k - 1] * joined if k > 0 else zero_grp
            after = up_[k + 1] * joined if k < FF_NSEG - 1 else zero_grp
            zm1 = jnp.concatenate([jnp.where(is_first, before, down[k])] + g[:-1], axis=0)
            zp1 = jnp.concatenate(g[1:] + [jnp.where(is_last, after, up_[k])], axis=0)
            zc = b + zm1 * w[0:1] + jnp.concatenate(g, axis=0) * w[1:2] + zp1 * w[2:3]
            a = zc[:, :FF_TILE]
            t = jnp.tanh(a * (GELU_C0 + GELU_C1 * (a * a)))
            strips.append((a * (0.5 + 0.5 * t) * zc[:, FF_TILE:]).astype(BF16))
        hid_ref[:, lo:lo + FF_TILE] = jnp.concatenate(strips, axis=0)

    up(0)
    for j in range(n_tiles):
        if j + 1 < n_tiles:
            up(j + 1)
        gate(j)
    acc = _dot(hid_ref[...], wdn_ref[...])
    acc = jnp.swapaxes(acc.reshape(FF_NSEG, FF_NGRP, SUBLANES, D_MODEL), 1, 2).reshape(TB, D_MODEL)
    g2 = mod_ref[...][:, 5 * D_MODEL:6 * D_MODEL]
    x2 = x1_ref[...] + g2 * acc
    if final:
        y = _rms(x2, fg_ref[...])
        yp_ref, ys_ref = o_refs

        @pl.when(i < NB_CTX)
        def _():
            yp_ref[...] = y

        @pl.when(i >= NB_CTX)
        def _():
            ys_ref[...] = y
    else:
        o_refs[0][...] = x2


def _ffn(x1, h2, mods, perm, wup_bf, conv_w, conv_b, wdn_bf, final_g, l, *, final):
    if final:
        out_specs = tuple(_x_specs(True))
        out_shape = (jax.ShapeDtypeStruct((N_CTX, D_MODEL), F32), jax.ShapeDtypeStruct((N_LAT, D_MODEL), F32))
    else:
        out_specs = pl.BlockSpec((TB, D_MODEL), lambda i: (i, 0))
        out_shape = jax.ShapeDtypeStruct((N_TOK, D_MODEL), F32)
    return pl.pallas_call(
        functools.partial(_ffn_kernel, final=final),
        grid=(N_TOK // TB,),
        in_specs=[
            pl.BlockSpec((TB, D_MODEL), lambda i: (i, 0)),
            pl.BlockSpec((TB, D_MODEL), lambda i: (i, 0)),
            _mod_spec(l), _const_spec((TB, TB)),
            _layer_spec((D_MODEL, 2 * D_FF), l, single=True), _layer_spec((3, 2 * D_FF), l),
            _layer_spec((1, 2 * D_FF), l), _layer_spec((D_FF, D_MODEL), l, single=True),
            _const_spec((1, D_MODEL)),
        ],
        out_specs=out_specs,
        out_shape=out_shape,
        scratch_shapes=[pltpu.VMEM((TB, D_MODEL), BF16), pltpu.VMEM((2, TB, 2 * FF_TILE), F32),
                        pltpu.VMEM((TB, D_FF), BF16)],
        compiler_params=pltpu.CompilerParams(vmem_limit_bytes=VMEM_LIMIT),
        name="conv_ffn",
    )(x1, h2, mods, perm, wup_bf, conv_w, conv_b, wdn_bf, final_g.reshape(1, D_MODEL))


def kernel(x_prompt, x_sample, state_s5_re, state_s5_im, state_hgrn, c, c_ctx, w_ada, b_ada, norm1_g,
           norm2_g, w_in, s5_lam_re, s5_lam_im, s5_log_dt, s5_b_re, s5_b_im, s5_c_re, s5_c_im, s5_d,
           s5_w_glu, hg_lb_logits, fn_w, gm_norm_g, gm_ws, gm_bs, grp_norm_g, w_out, ffn_w_up,
           ffn_conv_w, ffn_conv_b, ffn_w_down, final_norm_g):
    xs = (x_prompt.reshape(N_CTX, D_MODEL), x_sample.reshape(N_LAT, D_MODEL))
    cvecs = jnp.concatenate([c_ctx[None], c, jnp.zeros((8 - 1 - DEC_BATCH, D_MODEL), F32)], axis=0)
    mods = _ada(cvecs, w_ada, b_ada).reshape(DEPTH, 8, 1, N_MOD * D_MODEL)

    lb_p = jax.nn.softmax(hg_lb_logits.astype(F32), axis=0)
    lbs = jnp.maximum(jnp.cumsum(lb_p, axis=0) - lb_p[0], 0.0)
    lbps = jnp.stack([jnp.log(lbs), jnp.log1p(-lbs), 1.0 - lbs], axis=2)

    hg_consts_np = _hgrn_consts()
    hg_consts = (jnp.asarray(hg_consts_np[0], BF16), jnp.asarray(hg_consts_np[1], F32),
                 jnp.asarray(hg_consts_np[2], F32))
    cs_np, dft_ctx_np = _dft_consts(SEQ)
    _, dft_lat_np = _dft_consts(DEC_SEQ)
    cs = jnp.asarray(cs_np, F32).astype(BF16)
    dft_ctx = jnp.asarray(dft_ctx_np, F32).astype(BF16)
    dft_lat = jnp.asarray(dft_lat_np, F32).astype(BF16)
    gm_hm = jnp.asarray(np.kron(np.eye(GM_HEADS), np.ones((GM_CHUNK, GROUP_W // GM_HEADS))), BF16)
    bdm = jnp.asarray(np.kron(np.eye(HG_HEADS), np.ones((HG_DK, HG_DK))) / HG_DK, BF16)

    tables = _s5_tables(s5_lam_re, s5_lam_im, s5_log_dt, s5_b_re, s5_b_im, s5_c_re, s5_c_im)
    s5_m, s5_f, s5_e = _s5_prep(tables, jnp.asarray(_s5_shift_consts(), BF16))
    s5_al = tables[-1]
    s5_h0 = jnp.stack([state_s5_re, state_s5_im]).astype(F32).transpose(2, 3, 0, 1, 4, 5)
    s5_h0 = s5_h0.reshape(DEPTH, 2, 2, DEC_BATCH, S5_SW)
    hg_s0 = jnp.einsum('bldhkv,hg->lbdhvgk', state_hgrn.astype(F32), jnp.eye(HG_HEADS, dtype=F32))
    hg_s0 = hg_s0.reshape(DEPTH, DEC_BATCH, 2, HG_W, HG_W)

    def rows(a):
        return a.reshape(DEPTH, 1, a.shape[-1])

    norm1_r, norm2_r, gn_r, s5_d_r, gmn_r, cb_r = (rows(a) for a in (
        norm1_g, norm2_g, grp_norm_g, s5_d, gm_norm_g, ffn_conv_b))
    gm_w = gm_ws.transpose(0, 2, 1, 3).reshape(DEPTH, GM_CHUNK, GM_HEADS * GM_CHUNK).astype(BF16)
    gm_b = jnp.repeat(gm_bs.transpose(0, 2, 1), GROUP_W // GM_HEADS, axis=2)
    ff_perm = jnp.asarray(_ffn_perm_consts(), BF16)
    wup_bf = ffn_w_up.astype(BF16)
    wdn_bf = ffn_w_down.astype(BF16)

    new_re, new_im, new_hg = [], [], []
    for l in range(DEPTH):
        proj = _inproj(xs, mods, norm1_r, w_in, l)

        ylo, yhi, hfin = _s5(proj, s5_m, s5_f, s5_e, s5_al, s5_h0, l)
        hfin = hfin.reshape(2, 2, BATCH, S5_G, S5_N).transpose(1, 2, 0, 3, 4)
        new_re.append(hfin[0])
        new_im.append(hfin[1])

        of, ob, sfin = _hgrn(proj, lbps, hg_consts, hg_s0, l)
        new_hg.append(sfin)

        yc = (_fnet(proj, cs, dft_ctx, fn_w, l, n_seq=BATCH, seq_len=SEQ, tok0=0, spg=FN_CTX_SPG, tr=SEQ),
              _fnet(proj, cs, dft_lat, fn_w, l, n_seq=DEC_BATCH, seq_len=DEC_SEQ, tok0=N_CTX, spg=1,
                    tr=FN_LAT_TR))

        od = _gmlp(proj, gmn_r, gm_w, gm_b, gm_hm, gn_r, l)

        x1, h2 = _outproj(xs, proj, ylo, yhi, of, ob, yc, od, mods, s5_d_r, s5_w_glu, gn_r, bdm, w_out,
                          norm2_r, l)
        res = _ffn(x1, h2, mods, ff_perm, wup_bf, ffn_conv_w, cb_r, wdn_bf, final_norm_g, l,
                   final=(l == DEPTH - 1))
        xs = (res,)

    y_prompt = res[0].reshape(BATCH, SEQ, D_MODEL)
    y_sample = res[1].reshape(DEC_BATCH, DEC_SEQ, D_MODEL)
    return (y_prompt, y_sample, jnp.stack(new_re, axis=1), jnp.stack(new_im, axis=1),
            jnp.stack(new_hg, axis=1))
```

```python
import functools
import math

import numpy as np
import jax
import jax.numpy as jnp
from jax import lax
from jax.experimental import pallas as pl
from jax.experimental.pallas import tpu as pltpu

D_MODEL = 1024
BATCH = 16
SEQ = 256
DEPTH = 2
DEC_BATCH = 2
DEC_SEQ = 2048
GRID_W = 64
GROUP_W = 256
S5_P = 16
S5_G = 16
S5_N = 64
HG_HEADS = 4
HG_DK = 64
GM_HEADS = 4
GM_CHUNK = 128
D_FF = 2816
N_MOD = 6
D_IN = 9 * GROUP_W
EPS = 1e-6
LAM_RE_MAX = -1e-4

N_CTX = BATCH * SEQ
N_LAT = DEC_BATCH * DEC_SEQ
N_TOK = N_CTX + N_LAT
TB = 256
NB_CTX = N_CTX // TB
NB_LAT_SEQ = DEC_SEQ // TB
S5_L = 16
HG_L = 64
FF_TILE = 256
LANES = 128
VMEM_LIMIT = 56 * 1024 * 1024

F32 = jnp.float32
BF16 = jnp.bfloat16

COL_XA, COL_HQ, COL_HF_FWD, COL_HF_BWD, COL_HI, COL_HGATE, COL_XC, COL_GU, COL_GV = range(9)


def _mod_row(i):
    return jnp.where(i < NB_CTX, 0, 1 + (i - NB_CTX) // NB_LAT_SEQ)


def _gelu(x):
    return 0.5 * x * (1.0 + jnp.tanh(0.7978845608028654 * (x + 0.044715 * (x * x * x))))


def _rms(x, g):
    return x * lax.rsqrt(jnp.mean(x * x, axis=-1, keepdims=True) + EPS) * g


def _dot(a, b):
    return jnp.dot(a, b, preferred_element_type=F32)


def _dot_nt(a, b, precision=None):
    return lax.dot_general(a, b, (((1,), (1,)), ((), ())), precision=precision, preferred_element_type=F32)


def _dot_tn(a, b):
    return lax.dot_general(a, b, (((0,), (0,)), ((), ())), preferred_element_type=F32)


def _const_spec(shape, single=False):
    kw = {"pipeline_mode": pl.Buffered(1)} if single else {}
    return pl.BlockSpec(shape, lambda *_: (0,) * len(shape), **kw)


def _layer_spec(shape, l, single=False):
    kw = {"pipeline_mode": pl.Buffered(1)} if single else {}
    return pl.BlockSpec((None,) + tuple(shape), lambda *_: (l,) + (0,) * len(shape), **kw)


def _mod_spec(l):
    return pl.BlockSpec((None, None, 1, N_MOD * D_MODEL), lambda i: (l, _mod_row(i), 0, 0))


def _x_specs(split, width=D_MODEL):
    if split:
        return [pl.BlockSpec((TB, width), lambda i: (jnp.minimum(i, NB_CTX - 1), 0)),
                pl.BlockSpec((TB, width), lambda i: (jnp.maximum(i - NB_CTX, 0), 0))]
    return [pl.BlockSpec((TB, width), lambda i: (i, 0))]


def _x_value(x_refs):
    if len(x_refs) == 1:
        return x_refs[0][...]
    return jnp.where(pl.program_id(0) < NB_CTX, x_refs[0][...], x_refs[1][...])


ADA_TN = 1536


def _ada_kernel(c_ref, w_ref, b_ref, o_ref):
    cv = c_ref[...]
    s = cv * jax.nn.sigmoid(cv)
    o_ref[0] = _dot(s.astype(BF16), w_ref[0].astype(BF16)) + b_ref[0]


def _ada(cvecs, w_ada, b_ada):
    n = N_MOD * D_MODEL
    return pl.pallas_call(
        _ada_kernel,
        grid=(DEPTH, n // ADA_TN),
        in_specs=[
            pl.BlockSpec((8, D_MODEL), lambda l, j: (0, 0)),
            pl.BlockSpec((1, D_MODEL, ADA_TN), lambda l, j: (l, 0, j)),
            pl.BlockSpec((1, 1, ADA_TN), lambda l, j: (l, 0, j)),
        ],
        out_specs=pl.BlockSpec((1, 8, ADA_TN), lambda l, j: (l, 0, j)),
        out_shape=jax.ShapeDtypeStruct((DEPTH, 8, n), F32),
        compiler_params=pltpu.CompilerParams(vmem_limit_bytes=VMEM_LIMIT),
        name="ada_mod",
    )(cvecs, w_ada, b_ada.reshape(DEPTH, 1, n))


def _inproj_kernel(*refs):
    *x_refs, mod_ref, g_ref, w_ref, o_ref, wb_ref = refs

    @pl.when(pl.program_id(0) == 0)
    def _():
        wb_ref[...] = w_ref[...].astype(BF16)

    mod = mod_ref[...]
    sh = mod[:, 0:D_MODEL]
    sc = mod[:, D_MODEL:2 * D_MODEL]
    h = _rms(_x_value(x_refs), g_ref[...]) * (1.0 + sc) + sh
    o_ref[...] = _dot(h.astype(BF16), wb_ref[...])


def _inproj(xs, mods, norm_g, w_in, l):
    return pl.pallas_call(
        _inproj_kernel,
        grid=(N_TOK // TB,),
        in_specs=_x_specs(len(xs) == 2) + [
            _mod_spec(l),
            _layer_spec((1, D_MODEL), l),
            _layer_spec((D_MODEL, D_IN), l, single=True),
        ],
        out_specs=pl.BlockSpec((TB, D_IN), lambda i: (i, 0)),
        out_shape=jax.ShapeDtypeStruct((N_TOK, D_IN), F32),
        scratch_shapes=[pltpu.VMEM((D_MODEL, D_IN), BF16)],
        compiler_params=pltpu.CompilerParams(vmem_limit_bytes=VMEM_LIMIT),
        name="in_proj",
    )(*xs, mods, norm_g, w_in)


S5_NPAIR = S5_G // 2
S5_CW = S5_L * S5_P
S5_SW = S5_G * S5_N
S5_STEP_ROWS = N_CTX // S5_L
S5_CTX_NC = SEQ // S5_L
S5_LAT_NC = DEC_SEQ // S5_L


def _s5_slot(g, t):
    return (t + g) % S5_L


def _s5_tables(lam_re, lam_im, log_dt, b_re, b_im, c_re, c_im):
    lr = jnp.minimum(lam_re.astype(F32), LAM_RE_MAX)
    li = lam_im.astype(F32)
    dt = jnp.exp(log_dt.astype(F32))[..., None]
    mag = jnp.exp(lr * dt)
    ang = li * dt
    ab_re = mag * jnp.cos(ang)
    ab_im = mag * jnp.sin(ang)
    den = lr * lr + li * li
    xr = ab_re - 1.0
    z_re = (xr * lr + ab_im * li) / den
    z_im = (ab_im * lr - xr * li) / den
    bb_re = z_re[..., None] * b_re - z_im[..., None] * b_im
    bb_im = z_re[..., None] * b_im + z_im[..., None] * b_re
    tau = jnp.arange(S5_L + 1, dtype=F32)[:, None, None, None, None]
    pm = jnp.exp(lr * dt * tau)
    pa = li * dt * tau
    pw_re = pm * jnp.cos(pa)
    pw_im = pm * jnp.sin(pa)
    eye2 = jnp.eye(2, dtype=F32)

    def pw_pairs(a):
        a = a.transpose(1, 2, 3, 0, 4).reshape(DEPTH, 2, S5_NPAIR, 2, S5_L + 1, S5_N)
        return a.transpose(0, 1, 2, 4, 3, 5).reshape(DEPTH, 2, S5_NPAIR, S5_L + 1, 2 * S5_N)

    def mat_pairs(a):
        a = a.reshape(DEPTH, 2, S5_NPAIR, 2, S5_P, S5_N)
        return jnp.einsum('ldjaqn,ab->ldjaqbn', a, eye2).reshape(DEPTH, 2, S5_NPAIR, 2, S5_P, 2 * S5_N)

    al = jnp.stack([pw_re[S5_L], pw_im[S5_L]], axis=2).reshape(DEPTH, 2, 2, 1, S5_SW)
    return (pw_pairs(pw_re), pw_pairs(pw_im),
            mat_pairs(bb_re.transpose(0, 1, 2, 4, 3)), mat_pairs(bb_im.transpose(0, 1, 2, 4, 3)),
            mat_pairs(c_re.astype(F32)), mat_pairs(c_im.astype(F32)), al)


def _s5_shift_consts():
    n = S5_CW
    r = np.arange(n)[:, None]
    c = np.arange(n)[None, :]
    fwd = [(c == r + S5_P * s) for s in range(S5_L)]
    bwd = [(c == r - S5_P * (S5_L - 1 - s)) for s in range(S5_L)]
    return np.stack([np.concatenate(fwd, axis=1), np.concatenate(bwd, axis=1)]).astype(np.float32)


def _s5_prep_kernel(pwr_ref, pwi_ref, br_ref, bi_ref, cr_ref, ci_ref, scat_ref, m_ref, f_ref, e_ref, k_ref):
    L = S5_L
    hp = lax.Precision.HIGHEST
    sw = 2 * S5_N

    for j in range(S5_NPAIR):
        for d in range(2):
            pwr = pwr_ref[0, d, j]
            pwi = pwi_ref[0, d, j]
            for gl in range(2):
                g = 2 * j + gl
                br, bi = br_ref[0, d, j, gl], bi_ref[0, d, j, gl]
                cr, ci = cr_ref[0, d, j, gl], ci_ref[0, d, j, gl]
                ca_r, ca_i, f_r, f_i, e_r, e_i = [], [], [], [], [], []
                for t in range(L):
                    kk = t if d == 0 else L - 1 - t
                    pr, pi = pwr[kk:kk + 1], pwi[kk:kk + 1]
                    ca_r.append(cr * pr - ci * pi)
                    ca_i.append(cr * pi + ci * pr)
                    kf = L - 1 - t if d == 0 else t
                    pr, pi = pwr[kf:kf + 1], pwi[kf:kf + 1]
                    f_r.append(br * pr - bi * pi)
                    f_i.append(br * pi + bi * pr)
                    ke = t + 1 if d == 0 else L - t
                    pr, pi = pwr[ke:ke + 1], pwi[ke:ke + 1]
                    e_r.append(cr * pr - ci * pi)
                    e_i.append(-(cr * pi + ci * pr))
                cat = lambda xs: jnp.concatenate(xs, axis=0)
                k = _dot_nt(br, cat(ca_r), hp) - _dot_nt(bi, cat(ca_i), hp)
                k_ref[d, S5_P * g:S5_P * (g + 1), :] = k
                slots = lambda xs: cat([xs[(p - g) % L] for p in range(L)])
                rows = slice(S5_CW * gl, S5_CW * (gl + 1))
                f_ref[0, d, j, rows, 0:sw] = slots(f_r).astype(BF16)
                f_ref[0, d, j, rows, sw:2 * sw] = slots(f_i).astype(BF16)
                e_ref[0, d, 0, j, rows, :] = slots(e_r).astype(BF16)
                e_ref[0, d, 1, j, rows, :] = slots(e_i).astype(BF16)

    kf = k_ref[0].astype(BF16)
    kb = k_ref[1].astype(BF16)
    for s in range(L):
        cols = slice(S5_CW * s, S5_CW * (s + 1))
        res = _dot(kf, scat_ref[0, :, cols]) + _dot(kb, scat_ref[1, :, cols])
        for g in range(S5_G):
            blk = res[S5_P * g:S5_P * (g + 1), :]
            if g:
                blk = pltpu.roll(blk, S5_P * g, 1)
            r0 = S5_P * _s5_slot(g, s)
            m_ref[0, g, r0:r0 + S5_P, :] = blk.astype(BF16)


def _s5_prep(tables, scat):
    pwr, pwi, br, bi, cr, ci, _ = tables

    def lspec(shape):
        return pl.BlockSpec((1,) + shape, lambda l: (l,) + (0,) * len(shape))

    pw_shape = (2, S5_NPAIR, S5_L + 1, 2 * S5_N)
    mat_shape = (2, S5_NPAIR, 2, S5_P, 2 * S5_N)
    return pl.pallas_call(
        _s5_prep_kernel,
        grid=(DEPTH,),
        in_specs=[lspec(pw_shape), lspec(pw_shape), lspec(mat_shape), lspec(mat_shape), lspec(mat_shape),
                  lspec(mat_shape), _const_spec(scat.shape)],
        out_specs=(lspec((S5_G, S5_CW, S5_CW)), lspec((2, S5_NPAIR, 2 * S5_CW, 4 * S5_N)),
                   lspec((2, 2, S5_NPAIR, 2 * S5_CW, 2 * S5_N))),
        out_shape=(jax.ShapeDtypeStruct((DEPTH, S5_G, S5_CW, S5_CW), BF16),
                   jax.ShapeDtypeStruct((DEPTH, 2, S5_NPAIR, 2 * S5_CW, 4 * S5_N), BF16),
                   jax.ShapeDtypeStruct((DEPTH, 2, 2, S5_NPAIR, 2 * S5_CW, 2 * S5_N), BF16)),
        scratch_shapes=[pltpu.VMEM((2, S5_G * S5_P, S5_CW), F32)],
        compiler_params=pltpu.CompilerParams(vmem_limit_bytes=VMEM_LIMIT),
        name="s5_prep",
    )(pwr, pwi, br, bi, cr, ci, scat)


def _s5_kernel(xlo_ref, xhi_ref, m_ref, f_ref, e_ref, al_ref, h0_ref, ylo_ref, yhi_ref, hfin_ref,
               uy_ref, zh_ref):
    R = S5_STEP_ROWS
    pw = 2 * S5_CW
    sw = 2 * S5_N
    gph = S5_G // 2
    step = pl.program_id(0)

    for half, x_ref in enumerate((xlo_ref, xhi_ref)):
        for t in range(S5_L):
            xt = x_ref[pl.ds(t, R, stride=S5_L), :]
            shift = (S5_P * t) % LANES
            xr = pltpu.roll(xt, shift, 1) if shift else xt
            for gl in range(gph):
                g = gph * half + gl
                src = (S5_P * gl + shift) % LANES
                lo = S5_CW * g + S5_P * _s5_slot(g, t)
                assert lo % LANES == src
                uy_ref[:, lo:lo + S5_P] = xr[:, src:src + S5_P]

    for j in range(S5_NPAIR):
        ub = uy_ref[:, pw * j:pw * (j + 1)].astype(BF16)
        for d in range(2):
            o = _dot(ub, f_ref[d, j])
            zh_ref[d, 0, :, sw * j:sw * (j + 1)] = o[:, :sw]
            zh_ref[d, 1, :, sw * j:sw * (j + 1)] = o[:, sw:]

    def scan_group(seq_rows, nc, init, fin_rows):
        chains = [(d, r0) for d in range(2) for r0 in seq_rows]

        def body(c, carry):
            out = []
            for k, (d, r0) in enumerate(chains):
                hr, hi = carry[2 * k], carry[2 * k + 1]
                r = r0 + (c if d == 0 else nc - 1 - c)
                zr = zh_ref[d, 0, pl.ds(r, 1), :]
                zi = zh_ref[d, 1, pl.ds(r, 1), :]
                zh_ref[d, 0, pl.ds(r, 1), :] = hr
                zh_ref[d, 1, pl.ds(r, 1), :] = hi
                ar = al_ref[d, 0]
                ai = al_ref[d, 1]
                out.append(ar * hr - ai * hi + zr)
                out.append(ar * hi + ai * hr + zi)
            return tuple(out)

        fin = lax.fori_loop(0, nc, body, tuple(init))
        if fin_rows is not None:
            for k, (d, _) in enumerate(chains):
                s = fin_rows[k % len(seq_rows)]
                hfin_ref[d, 0, s:s + 1, :] = fin[2 * k]
                hfin_ref[d, 1, s:s + 1, :] = fin[2 * k + 1]

    @pl.when(step == 0)
    def _():
        zero = jnp.zeros((1, S5_SW), F32)
        for s0 in range(0, BATCH, 4):
            seqs = list(range(s0, s0 + 4))
            scan_group([s * S5_CTX_NC for s in seqs], S5_CTX_NC, [zero] * 16, seqs)

    @pl.when(step == 1)
    def _():
        init = []
        for d in range(2):
            for s in range(DEC_BATCH):
                init.append(h0_ref[d, 0, s:s + 1, :])
                init.append(h0_ref[d, 1, s:s + 1, :])
        scan_group([s * S5_LAT_NC for s in range(DEC_BATCH)], S5_LAT_NC, init, None)

    for j in range(S5_NPAIR):
        acc = None
        for d in range(2):
            for ri in range(2):
                hb = zh_ref[d, ri, :, sw * j:sw * (j + 1)].astype(BF16)
                t = _dot_nt(hb, e_ref[d, ri, j])
                acc = t if acc is None else acc + t
        for gl in range(2):
            g = 2 * j + gl
            ug = uy_ref[:, S5_CW * g:S5_CW * (g + 1)].astype(BF16)
            uy_ref[:, S5_CW * g:S5_CW * (g + 1)] = acc[:, S5_CW * gl:S5_CW * (gl + 1)] + _dot(ug, m_ref[g])

    lane = lax.broadcasted_iota(jnp.int32, (R, LANES), 1)
    in_piece = [jnp.logical_and(lane >= S5_P * k, lane < S5_P * (k + 1)) for k in range(LANES // S5_P)]
    for half, y_ref in enumerate((ylo_ref, yhi_ref)):
        for t in range(S5_L):
            merged = None
            for gl in range(gph):
                g = gph * half + gl
                slot = _s5_slot(g, t)
                col = S5_CW * g + LANES * (S5_P * slot // LANES)
                v = uy_ref[:, col:col + LANES]
                merged = v if merged is None else jnp.where(in_piece[slot % (LANES // S5_P)], v, merged)
            shift = (LANES - S5_P * t % LANES) % LANES
            y_ref[pl.ds(t, R, stride=S5_L), :] = pltpu.roll(merged, shift, 1) if shift else merged


def _s5(proj, m, f2, e2, al, h0, l):
    half = pl.BlockSpec((N_CTX, LANES), lambda s: (s, 0))
    return pl.pallas_call(
        _s5_kernel,
        grid=(2,),
        in_specs=[pl.BlockSpec((N_CTX, LANES), lambda s: (s, 0)),
                  pl.BlockSpec((N_CTX, LANES), lambda s: (s, 1)),
                  _layer_spec(m.shape[1:], l, single=True), _layer_spec(f2.shape[1:], l, single=True),
                  _layer_spec(e2.shape[1:], l, single=True), _layer_spec(al.shape[1:], l),
                  _layer_spec(h0.shape[1:], l)],
        out_specs=(half, half, _const_spec((2, 2, BATCH, S5_SW))),
        out_shape=(jax.ShapeDtypeStruct((N_TOK, LANES), F32),
                   jax.ShapeDtypeStruct((N_TOK, LANES), F32),
                   jax.ShapeDtypeStruct((2, 2, BATCH, S5_SW), F32)),
        scratch_shapes=[pltpu.VMEM((S5_STEP_ROWS, S5_G * S5_CW), F32),
                        pltpu.VMEM((2, 2, S5_STEP_ROWS, S5_SW), F32)],
        compiler_params=pltpu.CompilerParams(vmem_limit_bytes=VMEM_LIMIT),
        name="s5_mixer",
    )(proj, proj, m, f2, e2, al, h0)


HG_NLEV = int(math.log2(HG_L))
HG_W = HG_HEADS * HG_DK
HG_CPS = 4
HG_BLK = HG_CPS * HG_L
HG_NB_CTX = N_CTX // HG_BLK
HG_NC_CTX = SEQ // HG_BLK
HG_NC_LAT = DEC_SEQ // HG_BLK
assert HG_L == HG_DK


def _hgrn_consts():
    L = HG_L
    w = np.zeros((HG_NLEV + 2, L, L), np.float32)
    mask = np.zeros((HG_NLEV + 1, L, L), np.float32)
    for lev in range(HG_NLEV):
        blk = L >> lev
        half = blk // 2
        for t in range(L):
            p, o = divmod(t, blk)
            bd = p * blk + half - 1
            if o >= half:
                w[lev, t, bd + 1:t + 1] = 1.0
            else:
                w[lev, t, t + 1:bd + 1] = 1.0
        jj, ii = np.meshgrid(np.arange(L), np.arange(L), indexing='ij')
        mask[lev] = ((jj // blk == ii // blk) & (jj % blk >= half) & (ii % blk < half)).astype(np.float32)
    mask[HG_NLEV] = np.eye(L, dtype=np.float32)
    for t in range(L):
        w[HG_NLEV, t, :t + 1] = 1.0
        w[HG_NLEV + 1, t, t + 1:] = 1.0
    out = []
    for wd, md in ((w, mask), (w[:, ::-1, ::-1], mask[:, ::-1, ::-1])):
        wflat = wd.reshape((HG_NLEV + 2) * L, L)
        out.append((np.concatenate([wflat] * 3, axis=1), np.tile(md, (1, 1, HG_HEADS))))
    wcat = np.stack([out[0][0], out[1][0]])
    mask4 = np.stack([out[0][1], out[1][1]])
    hm = np.kron(np.eye(HG_HEADS, dtype=np.float32), np.ones((HG_DK, HG_DK), np.float32))
    return wcat, mask4, hm


def _hg_pos(i):
    is_ctx = i < HG_NB_CTX
    c = jnp.where(is_ctx, i % HG_NC_CTX, (i - HG_NB_CTX) % HG_NC_LAT)
    nc = jnp.where(is_ctx, HG_NC_CTX, HG_NC_LAT)
    return is_ctx, c, nc


def _hg_bwd_blk(i):
    _, c, nc = _hg_pos(i)
    return i + nc - 1 - 2 * c


def _hg_local(d, q, z, v, lbp, wcat_ref, mask_ref, hm):
    L = HG_L
    az = jnp.abs(z)
    sp = jnp.log1p(jnp.exp(-az))
    ls = jnp.minimum(z, 0.0) - sp
    kk = lbp[2:3] * jnp.exp(jnp.minimum(-z, 0.0) - sp)
    a = lbp[0:1]
    b = lbp[1:2] + ls
    logf = jnp.maximum(a, b) + jnp.log1p(jnp.exp(-jnp.abs(a - b)))
    p0 = logf.astype(BF16)
    r0 = logf - p0.astype(F32)
    p1 = r0.astype(BF16)
    p2 = (r0 - p1.astype(F32)).astype(BF16)
    ex = jnp.exp(_dot(wcat_ref[d], jnp.concatenate([p0, p1, p2], axis=0)))
    hmb = hm.astype(BF16)

    def bd4(x):
        return jnp.concatenate([x] * HG_HEADS, axis=0) * hmb

    sc = None
    for lev in range(HG_NLEV + 1):
        if lev < HG_NLEV:
            al = ex[L * lev:L * (lev + 1)]
            lhs = (q * al).astype(BF16)
            rhs = (kk * al).astype(BF16)
        else:
            lhs = q.astype(BF16)
            rhs = kk.astype(BF16)
        t = _dot_nt(lhs, bd4(rhs)) * mask_ref[d, lev]
        sc = t if sc is None else sc + t
    vb = v.astype(BF16)
    o = _dot(sc.astype(BF16), bd4(vb))
    eq = ex[L * HG_NLEV:L * (HG_NLEV + 1)]
    qe = (q * eq).astype(BF16)
    ke = (kk * ex[L * (HG_NLEV + 1):L * (HG_NLEV + 2)]).astype(BF16)
    g = eq[L - 1:L] if d == 0 else eq[0:1]
    return o, qe, g, _dot_tn(vb, ke) * hm


def _hgrn_kernel(qf, zf, vf, qb, zb, vb, lb_ref, w_ref, mask_ref, hm_ref, s0_ref,
                 of_ref, ob_ref, sfin_ref, st_ref):
    is_ctx, c, nc = _hg_pos(pl.program_id(0))

    @pl.when(c == 0)
    def _():
        st_ref[...] = jnp.where(is_ctx, 0.0, s0_ref[0])

    hm = hm_ref[...]
    for d, (q_ref, z_ref, v_ref, o_ref) in enumerate(((qf, zf, vf, of_ref), (qb, zb, vb, ob_ref))):
        rows = [slice(HG_L * k, HG_L * (k + 1)) for k in range(HG_CPS)]
        loc = [_hg_local(d, q_ref[r, :], z_ref[r, :], v_ref[r, :], lb_ref[d], w_ref, mask_ref, hm)
               for r in rows]
        st = st_ref[d]
        for k in (range(HG_CPS) if d == 0 else reversed(range(HG_CPS))):
            o, qe, g, upd = loc[k]
            o_ref[rows[k], :] = o + _dot_nt(qe, st.astype(BF16))
            st = st * g + upd
        st_ref[d] = st

    @pl.when(jnp.logical_and(is_ctx, c == nc - 1))
    def _():
        for d in range(2):
            s_kv = st_ref[d].T
            for h in range(HG_HEADS):
                sfin_ref[0, d, h] = s_kv[HG_DK * h:HG_DK * (h + 1), HG_DK * h:HG_DK * (h + 1)]


def _hgrn(proj, lbp, consts, s0, l):
    wcat, mask4, hm = consts

    def fwd(col):
        return pl.BlockSpec((HG_BLK, HG_W), lambda i: (i, col))

    def bwd(col):
        return pl.BlockSpec((HG_BLK, HG_W), lambda i: (_hg_bwd_blk(i), col))

    return pl.pallas_call(
        _hgrn_kernel,
        grid=(N_TOK // HG_BLK,),
        in_specs=[fwd(COL_HQ), fwd(COL_HF_FWD), fwd(COL_HI), bwd(COL_HQ), bwd(COL_HF_BWD), bwd(COL_HI),
                  _layer_spec(lbp.shape[1:], l), _const_spec(wcat.shape), _const_spec(mask4.shape),
                  _const_spec(hm.shape),
                  pl.BlockSpec((None, 1, 2, HG_W, HG_W),
                               lambda i: (l, jnp.maximum(i - HG_NB_CTX, 0) // HG_NC_LAT, 0, 0, 0))],
        out_specs=(pl.BlockSpec((HG_BLK, HG_W), lambda i: (i, 0)),
                   pl.BlockSpec((HG_BLK, HG_W), lambda i: (_hg_bwd_blk(i), 0)),
                   pl.BlockSpec((1, 2, HG_HEADS, HG_DK, HG_DK),
                                lambda i: (jnp.minimum(i // HG_NC_CTX, BATCH - 1), 0, 0, 0, 0))),
        out_shape=(jax.ShapeDtypeStruct((N_TOK, HG_W), F32),
                   jax.ShapeDtypeStruct((N_TOK, HG_W), F32),
                   jax.ShapeDtypeStruct((BATCH, 2, HG_HEADS, HG_DK, HG_DK), F32)),
        scratch_shapes=[pltpu.VMEM((2, HG_W, HG_W), F32)],
        compiler_params=pltpu.CompilerParams(vmem_limit_bytes=VMEM_LIMIT),
        name="hgrn_mixer",
    )(proj, proj, proj, proj, proj, proj, lbp, wcat, mask4, hm, s0)


FN_CTX_SPG = 4
FN_LAT_TR = 512


def _dft_consts(t_len):
    n = GROUP_W // 4
    k = np.arange(n)
    ang = 2.0 * np.pi * ((k[:, None] * k[None, :]) % n) / n
    eye = np.eye(4)
    cs = np.concatenate([np.kron(eye, np.cos(ang)), np.kron(eye, np.sin(ang))], axis=1) / math.sqrt(n)
    t = np.arange(t_len)
    angt = 2.0 * np.pi * ((t[:, None] * t[None, :]) % t_len) / t_len
    dft = np.concatenate([np.cos(angt), -np.sin(angt)], axis=1) / math.sqrt(t_len)
    return cs.astype(np.float32), dft.astype(np.float32)


def _fnet_kernel(x_ref, cs_ref, dft_ref, w_ref, o_ref, r_ref, *, t_len, spg, tr):
    @pl.when(pl.program_id(1) == 0)
    def _():
        t = _dot(x_ref[...].astype(BF16), cs_ref[...])
        for s in range(spg):
            rows = slice(t_len * s, t_len * (s + 1))
            r_ref[s, 0:t_len, :] = t[rows, :GROUP_W].astype(BF16)
            r_ref[s, t_len:2 * t_len, :] = t[rows, GROUP_W:].astype(BF16)

    w = w_ref[...].astype(BF16)
    for s in range(spg):
        y = _dot(dft_ref[...], r_ref[s])
        o_ref[tr * s:tr * (s + 1), :] = _dot(y.astype(BF16), w)


def _fnet(proj, cs, dft, fn_w, l, *, n_seq, seq_len, tok0, spg, tr):
    nj = seq_len // tr
    assert spg == 1 or nj == 1
    gb0 = tok0 // (spg * seq_len)
    return pl.pallas_call(
        functools.partial(_fnet_kernel, t_len=seq_len, spg=spg, tr=tr),
        grid=(n_seq // spg, nj),
        in_specs=[
            pl.BlockSpec((spg * seq_len, GROUP_W), lambda g, j: (gb0 + g, COL_XC)),
            _const_spec((GROUP_W, 2 * GROUP_W)),
            pl.BlockSpec((tr, 2 * seq_len), lambda g, j: (j, 0)),
            _layer_spec((GROUP_W, GROUP_W), l),
        ],
        out_specs=pl.BlockSpec((spg * tr, GROUP_W), lambda g, j: (g * nj + j, 0)),
        out_shape=jax.ShapeDtypeStruct((n_seq * seq_len, GROUP_W), F32),
        scratch_shapes=[pltpu.VMEM((spg, 2 * seq_len, GROUP_W), BF16)],
        compiler_params=pltpu.CompilerParams(vmem_limit_bytes=VMEM_LIMIT),
        name="fnet_mixer",
    )(proj, cs, dft, fn_w)


GM_TB = 512


def _gmlp_kernel(gu_ref, gv_ref, g_ref, w_ref, bias_ref, hm_ref, gn_ref, o_ref):
    hmb = hm_ref[...]
    for c in range(GM_TB // GM_CHUNK):
        rows = slice(GM_CHUNK * c, GM_CHUNK * (c + 1))
        gu = _gelu(gu_ref[rows, :])
        gv = _rms(_gelu(gv_ref[rows, :]), g_ref[...]).astype(BF16)
        g4 = jnp.concatenate([gv] * GM_HEADS, axis=0) * hmb
        sp = _dot(w_ref[...], g4) + bias_ref[...]
        o_ref[rows, :] = _rms(gu * sp, gn_ref[...])


def _gmlp(proj, gm_norm_g, wcat, bias, hm, gn, l):
    return pl.pallas_call(
        _gmlp_kernel,
        grid=(N_TOK // GM_TB,),
        in_specs=[
            pl.BlockSpec((GM_TB, GROUP_W), lambda i: (i, COL_GU)),
            pl.BlockSpec((GM_TB, GROUP_W), lambda i: (i, COL_GV)),
            _layer_spec((1, GROUP_W), l),
            _layer_spec((GM_CHUNK, GM_HEADS * GM_CHUNK), l),
            _layer_spec((GM_CHUNK, GROUP_W), l),
            _const_spec((GM_HEADS * GM_CHUNK, GROUP_W)),
            pl.BlockSpec((None, 1, GROUP_W), lambda i: (l, 0, 3)),
        ],
        out_specs=pl.BlockSpec((GM_TB, GROUP_W), lambda i: (i, 0)),
        out_shape=jax.ShapeDtypeStruct((N_TOK, GROUP_W), F32),
        compiler_params=pltpu.CompilerParams(vmem_limit_bytes=VMEM_LIMIT),
        name="gmlp_mixer",
    )(proj, proj, gm_norm_g, wcat, bias, hm, gn)


def _outproj_kernel(*refs):
    (xa_ref, hgate_ref, ylo_ref, yhi_ref, of_ref, ob_ref, ycc_ref, ycl_ref, od_ref, mod_ref, d_ref,
     wglu_ref, gn_ref, bdm_ref, wout_ref, n2_ref, x1_ref, h2_ref, wob_ref) = refs[-19:]
    x_refs = refs[:-19]

    @pl.when(pl.program_id(0) == 0)
    def _():
        wob_ref[...] = wout_ref[...].astype(BF16)

    gn = gn_ref[...]
    ys = jnp.concatenate([ylo_ref[...], yhi_ref[...]], axis=1)
    y5 = _gelu(ys + d_ref[...] * xa_ref[...])
    glu = jax.nn.sigmoid(_dot(y5.astype(BF16), wglu_ref[...].astype(BF16)))
    out_a = _rms(y5 * glu, gn[:, 0:GROUP_W])
    o = of_ref[...] + ob_ref[...]
    o2 = o * o
    o2h = o2.astype(BF16)
    o2l = (o2 - o2h.astype(F32)).astype(BF16)
    ms = _dot(o2h, bdm_ref[...]) + _dot(o2l, bdm_ref[...])
    hg = hgate_ref[...]
    out_b = o * lax.rsqrt(ms + EPS) * gn[:, GROUP_W:2 * GROUP_W] * (hg * jax.nn.sigmoid(hg))
    out_c = _rms(_x_value((ycc_ref, ycl_ref)), gn[:, 2 * GROUP_W:3 * GROUP_W])
    out_d = od_ref[...]
    m = None
    for k, part in enumerate((out_a, out_b, out_c, out_d)):
        t = _dot(part.astype(BF16), wob_ref[GROUP_W * k:GROUP_W * (k + 1), :])
        m = t if m is None else m + t
    mod = mod_ref[...]
    g1 = mod[:, 2 * D_MODEL:3 * D_MODEL]
    sh2 = mod[:, 3 * D_MODEL:4 * D_MODEL]
    sc2 = mod[:, 4 * D_MODEL:5 * D_MODEL]
    x1 = _x_value(x_refs) + g1 * m
    x1_ref[...] = x1
    h2_ref[...] = (_rms(x1, n2_ref[...]) * (1.0 + sc2) + sh2).astype(BF16)


def _outproj(xs, proj, ylo, yhi, of, ob, yc, od, mods, s5_d, wglu, gn, bdm, wout, norm2_g, l):
    def tok(width, col=0):
        return pl.BlockSpec((TB, width), lambda i: (i, col))

    return pl.pallas_call(
        _outproj_kernel,
        grid=(N_TOK // TB,),
        in_specs=_x_specs(len(xs) == 2) + [
            tok(GROUP_W, COL_XA), tok(GROUP_W, COL_HGATE), tok(LANES), tok(LANES),
            tok(GROUP_W), tok(GROUP_W), *_x_specs(True, GROUP_W), tok(GROUP_W),
            _mod_spec(l),
            _layer_spec((1, GROUP_W), l), _layer_spec((GROUP_W, GROUP_W), l), _layer_spec((1, D_MODEL), l),
            _const_spec((GROUP_W, GROUP_W)), _layer_spec((D_MODEL, D_MODEL), l, single=True),
            _layer_spec((1, D_MODEL), l),
        ],
        out_specs=(tok(D_MODEL), tok(D_MODEL)),
        out_shape=(jax.ShapeDtypeStruct((N_TOK, D_MODEL), F32),
                   jax.ShapeDtypeStruct((N_TOK, D_MODEL), BF16)),
        scratch_shapes=[pltpu.VMEM((D_MODEL, D_MODEL), BF16)],
        compiler_params=pltpu.CompilerParams(vmem_limit_bytes=VMEM_LIMIT),
        name="out_proj",
    )(*xs, proj, proj, ylo, yhi, of, ob, *yc, od, mods, s5_d, wglu, gn, bdm, wout, norm2_g)


FF_SEG = GRID_W
FF_NSEG = TB // FF_SEG
SUBLANES = 8
FF_NGRP = FF_SEG // SUBLANES
GELU_C0 = 0.7978845608028654
GELU_C1 = GELU_C0 * 0.044715


def _ffn_perm_consts():
    p = np.zeros((TB, TB), np.float32)
    for k in range(FF_NSEG):
        for t in range(FF_SEG):
            p[FF_SEG * k + SUBLANES * (t % FF_NGRP) + t // FF_NGRP, FF_SEG * k + t] = 1.0
    return p


def _ffn_kernel(x1_ref, h2_ref, mod_ref, perm_ref, wup_ref, cw_ref, cb_ref, wdn_ref, fg_ref, *rest, final):
    *o_refs, h2p_ref, z_ref, hid_ref = rest
    i = pl.program_id(0)
    joined = (i < NB_CTX).astype(F32)
    n_tiles = D_FF // FF_TILE
    sub = lax.broadcasted_iota(jnp.int32, (SUBLANES, 2 * FF_TILE), 0)
    is_first = sub == 0
    is_last = sub == SUBLANES - 1
    zero_grp = jnp.zeros((SUBLANES, 2 * FF_TILE), F32)
    h2p_ref[...] = _dot(perm_ref[...], h2_ref[...]).astype(BF16)

    def up(j):
        lo = FF_TILE * j
        for c, col in enumerate((lo, D_FF + lo)):
            z_ref[j % 2, :, FF_TILE * c:FF_TILE * (c + 1)] = _dot(h2p_ref[...], wup_ref[:, col:col + FF_TILE])

    def gate(j):
        slot = j % 2
        lo = FF_TILE * j
        w = jnp.concatenate([cw_ref[:, lo:lo + FF_TILE], cw_ref[:, D_FF + lo:D_FF + lo + FF_TILE]], axis=1)
        b = jnp.concatenate([cb_ref[:, lo:lo + FF_TILE], cb_ref[:, D_FF + lo:D_FF + lo + FF_TILE]], axis=1)

        def grp(k, v):
            r = FF_SEG * k + SUBLANES * v
            return z_ref[slot, r:r + SUBLANES, :]

        down = [pltpu.roll(grp(k, FF_NGRP - 1), 1, 0) for k in range(FF_NSEG)]
        up_ = [pltpu.roll(grp(k, 0), SUBLANES - 1, 0) for k in range(FF_NSEG)]
        strips = []
        for k in range(FF_NSEG):
            g = [grp(k, v) for v in range(FF_NGRP)]
            before = down[k - 1] * joined if k > 0 else zero_grp
            after = up_[k + 1] * joined if k < FF_NSEG - 1 else zero_grp
            zm1 = jnp.concatenate([jnp.where(is_first, before, down[k])] + g[:-1], axis=0)
            zp1 = jnp.concatenate(g[1:] + [jnp.where(is_last, after, up_[k])], axis=0)
            zc = b + zm1 * w[0:1] + jnp.concatenate(g, axis=0) * w[1:2] + zp1 * w[2:3]
            a = zc[:, :FF_TILE]
            t = jnp.tanh(a * (GELU_C0 + GELU_C1 * (a * a)))
            strips.append((a * (0.5 + 0.5 * t) * zc[:, FF_TILE:]).astype(BF16))
        hid_ref[:, lo:lo + FF_TILE] = jnp.concatenate(strips, axis=0)

    up(0)
    for j in range(n_tiles):
        if j + 1 < n_tiles:
            up(j + 1)
        gate(j)
    acc = _dot(hid_ref[...], wdn_ref[...])
    acc = jnp.swapaxes(acc.reshape(FF_NSEG, FF_NGRP, SUBLANES, D_MODEL), 1, 2).reshape(TB, D_MODEL)
    g2 = mod_ref[...][:, 5 * D_MODEL:6 * D_MODEL]
    x2 = x1_ref[...] + g2 * acc
    if final:
        y = _rms(x2, fg_ref[...])
        yp_ref, ys_ref = o_refs

        @pl.when(i < NB_CTX)
        def _():
            yp_ref[...] = y

        @pl.when(i >= NB_CTX)
        def _():
            ys_ref[...] = y
    else:
        o_refs[0][...] = x2


def _ffn(x1, h2, mods, perm, wup_bf, conv_w, conv_b, wdn_bf, final_g, l, *, final):
    if final:
        out_specs = tuple(_x_specs(True))
        out_shape = (jax.ShapeDtypeStruct((N_CTX, D_MODEL), F32), jax.ShapeDtypeStruct((N_LAT, D_MODEL), F32))
    else:
        out_specs = pl.BlockSpec((TB, D_MODEL), lambda i: (i, 0))
        out_shape = jax.ShapeDtypeStruct((N_TOK, D_MODEL), F32)
    return pl.pallas_call(
        functools.partial(_ffn_kernel, final=final),
        grid=(N_TOK // TB,),
        in_specs=[
            pl.BlockSpec((TB, D_MODEL), lambda i: (i, 0)),
            pl.BlockSpec((TB, D_MODEL), lambda i: (i, 0)),
            _mod_spec(l), _const_spec((TB, TB)),
            _layer_spec((D_MODEL, 2 * D_FF), l, single=True), _layer_spec((3, 2 * D_FF), l),
            _layer_spec((1, 2 * D_FF), l), _layer_spec((D_FF, D_MODEL), l, single=True),
            _const_spec((1, D_MODEL)),
        ],
        out_specs=out_specs,
        out_shape=out_shape,
        scratch_shapes=[pltpu.VMEM((TB, D_MODEL), BF16), pltpu.VMEM((2, TB, 2 * FF_TILE), F32),
                        pltpu.VMEM((TB, D_FF), BF16)],
        compiler_params=pltpu.CompilerParams(vmem_limit_bytes=VMEM_LIMIT),
        name="conv_ffn",
    )(x1, h2, mods, perm, wup_bf, conv_w, conv_b, wdn_bf, final_g.reshape(1, D_MODEL))


def kernel(x_prompt, x_sample, state_s5_re, state_s5_im, state_hgrn, c, c_ctx, w_ada, b_ada, norm1_g,
           norm2_g, w_in, s5_lam_re, s5_lam_im, s5_log_dt, s5_b_re, s5_b_im, s5_c_re, s5_c_im, s5_d,
           s5_w_glu, hg_lb_logits, fn_w, gm_norm_g, gm_ws, gm_bs, grp_norm_g, w_out, ffn_w_up,
           ffn_conv_w, ffn_conv_b, ffn_w_down, final_norm_g):
    xs = (x_prompt.reshape(N_CTX, D_MODEL), x_sample.reshape(N_LAT, D_MODEL))
    cvecs = jnp.concatenate([c_ctx[None], c, jnp.zeros((8 - 1 - DEC_BATCH, D_MODEL), F32)], axis=0)
    mods = _ada(cvecs, w_ada, b_ada).reshape(DEPTH, 8, 1, N_MOD * D_MODEL)

    lb_p = jax.nn.softmax(hg_lb_logits.astype(F32), axis=0)
    lbs = jnp.maximum(jnp.cumsum(lb_p, axis=0) - lb_p[0], 0.0)
    lbps = jnp.stack([jnp.log(lbs), jnp.log1p(-lbs), 1.0 - lbs], axis=2)

    hg_consts_np = _hgrn_consts()
    hg_consts = (jnp.asarray(hg_consts_np[0], BF16), jnp.asarray(hg_consts_np[1], F32),
                 jnp.asarray(hg_consts_np[2], F32))
    cs_np, dft_ctx_np = _dft_consts(SEQ)
    _, dft_lat_np = _dft_consts(DEC_SEQ)
    cs = jnp.asarray(cs_np, F32).astype(BF16)
    dft_ctx = jnp.asarray(dft_ctx_np, F32).astype(BF16)
    dft_lat = jnp.asarray(dft_lat_np, F32).astype(BF16)
    gm_hm = jnp.asarray(np.kron(np.eye(GM_HEADS), np.ones((GM_CHUNK, GROUP_W // GM_HEADS))), BF16)
    bdm = jnp.asarray(np.kron(np.eye(HG_HEADS), np.ones((HG_DK, HG_DK))) / HG_DK, BF16)

    tables = _s5_tables(s5_lam_re, s5_lam_im, s5_log_dt, s5_b_re, s5_b_im, s5_c_re, s5_c_im)
    s5_m, s5_f, s5_e = _s5_prep(tables, jnp.asarray(_s5_shift_consts(), BF16))
    s5_al = tables[-1]
    s5_h0 = jnp.stack([state_s5_re, state_s5_im]).astype(F32).transpose(2, 3, 0, 1, 4, 5)
    s5_h0 = s5_h0.reshape(DEPTH, 2, 2, DEC_BATCH, S5_SW)
    hg_s0 = jnp.einsum('bldhkv,hg->lbdhvgk', state_hgrn.astype(F32), jnp.eye(HG_HEADS, dtype=F32))
    hg_s0 = hg_s0.reshape(DEPTH, DEC_BATCH, 2, HG_W, HG_W)

    def rows(a):
        return a.reshape(DEPTH, 1, a.shape[-1])

    norm1_r, norm2_r, gn_r, s5_d_r, gmn_r, cb_r = (rows(a) for a in (
        norm1_g, norm2_g, grp_norm_g, s5_d, gm_norm_g, ffn_conv_b))
    gm_w = gm_ws.transpose(0, 2, 1, 3).reshape(DEPTH, GM_CHUNK, GM_HEADS * GM_CHUNK).astype(BF16)
    gm_b = jnp.repeat(gm_bs.transpose(0, 2, 1), GROUP_W // GM_HEADS, axis=2)
    ff_perm = jnp.asarray(_ffn_perm_consts(), BF16)
    wup_bf = ffn_w_up.astype(BF16)
    wdn_bf = ffn_w_down.astype(BF16)

    new_re, new_im, new_hg = [], [], []
    for l in range(DEPTH):
        proj = _inproj(xs, mods, norm1_r, w_in, l)

        ylo, yhi, hfin = _s5(proj, s5_m, s5_f, s5_e, s5_al, s5_h0, l)
        hfin = hfin.reshape(2, 2, BATCH, S5_G, S5_N).transpose(1, 2, 0, 3, 4)
        new_re.append(hfin[0])
        new_im.append(hfin[1])

        of, ob, sfin = _hgrn(proj, lbps, hg_consts, hg_s0, l)
        new_hg.append(sfin)

        yc = (_fnet(proj, cs, dft_ctx, fn_w, l, n_seq=BATCH, seq_len=SEQ, tok0=0, spg=FN_CTX_SPG, tr=SEQ),
              _fnet(proj, cs, dft_lat, fn_w, l, n_seq=DEC_BATCH, seq_len=DEC_SEQ, tok0=N_CTX, spg=1,
                    tr=FN_LAT_TR))

        od = _gmlp(proj, gmn_r, gm_w, gm_b, gm_hm, gn_r, l)

        x1, h2 = _outproj(xs, proj, ylo, yhi, of, ob, yc, od, mods, s5_d_r, s5_w_glu, gn_r, bdm, w_out,
                          norm2_r, l)
        res = _ffn(x1, h2, mods, ff_perm, wup_bf, ffn_conv_w, cb_r, wdn_bf, final_norm_g, l,
                   final=(l == DEPTH - 1))
        xs = (res,)

    y_prompt = res[0].reshape(BATCH, SEQ, D_MODEL)
    y_sample = res[1].reshape(DEC_BATCH, DEC_SEQ, D_MODEL)
    return (y_prompt, y_sample, jnp.stack(new_re, axis=1), jnp.stack(new_im, axis=1),
            jnp.stack(new_hg, axis=1))
```

```python
import functools
import math

import numpy as np
import jax
import jax.numpy as jnp
from jax import lax
from jax.experimental import pallas as pl
from jax.experimental.pallas import tpu as pltpu

D_MODEL = 1024
BATCH = 16
SEQ = 256
DEPTH = 2
DEC_BATCH = 2
DEC_SEQ = 2048
GRID_W = 64
GROUP_W = 256
S5_P = 16
S5_G = 16
S5_N = 64
HG_HEADS = 4
HG_DK = 64
GM_HEADS = 4
GM_CHUNK = 128
D_FF = 2816
N_MOD = 6
D_IN = 9 * GROUP_W
EPS = 1e-6
LAM_RE_MAX = -1e-4

N_CTX = BATCH * SEQ
N_LAT = DEC_BATCH * DEC_SEQ
N_TOK = N_CTX + N_LAT
TB = 256
TBP = 512
NB_CTX = N_CTX // TB
S5_L = 16
HG_L = 64
FF_TILE = 256
LANES = 128
VMEM_LIMIT = 56 * 1024 * 1024

F32 = jnp.float32
BF16 = jnp.bfloat16

COL_XA, COL_HQ, COL_HF_FWD, COL_HF_BWD, COL_HI, COL_HGATE, COL_XC, COL_GU, COL_GV = range(9)


def _mod_row(i, tb):
    return jnp.where(i < N_CTX // tb, 0, 1 + (i - N_CTX // tb) // (DEC_SEQ // tb))


def _gelu(x):
    return 0.5 * x * (1.0 + jnp.tanh(0.7978845608028654 * (x + 0.044715 * (x * x * x))))


def _rms(x, g):
    return x * lax.rsqrt(jnp.mean(x * x, axis=-1, keepdims=True) + EPS) * g


def _dot(a, b):
    return jnp.dot(a, b, preferred_element_type=F32)


def _dot_nt(a, b, precision=None):
    return lax.dot_general(a, b, (((1,), (1,)), ((), ())), precision=precision, preferred_element_type=F32)


def _dot_tn(a, b):
    return lax.dot_general(a, b, (((0,), (0,)), ((), ())), preferred_element_type=F32)


def _const_spec(shape, single=False):
    kw = {"pipeline_mode": pl.Buffered(1)} if single else {}
    return pl.BlockSpec(shape, lambda *_: (0,) * len(shape), **kw)


def _layer_spec(shape, l, single=False):
    kw = {"pipeline_mode": pl.Buffered(1)} if single else {}
    return pl.BlockSpec((None,) + tuple(shape), lambda *_: (l,) + (0,) * len(shape), **kw)


def _mod_spec(l, tb=TB):
    return pl.BlockSpec((None, None, 1, N_MOD * D_MODEL), lambda i: (l, _mod_row(i, tb), 0, 0))


def _x_specs(split, width=D_MODEL, tb=TB):
    nb_ctx = N_CTX // tb
    if split:
        return [pl.BlockSpec((tb, width), lambda i: (jnp.minimum(i, nb_ctx - 1), 0)),
                pl.BlockSpec((tb, width), lambda i: (jnp.maximum(i - nb_ctx, 0), 0))]
    return [pl.BlockSpec((tb, width), lambda i: (i, 0))]


def _x_value(x_refs, tb=TB):
    if len(x_refs) == 1:
        return x_refs[0][...]
    return jnp.where(pl.program_id(0) < N_CTX // tb, x_refs[0][...], x_refs[1][...])


ADA_TN = 1536


def _ada_kernel(c_ref, w_ref, b_ref, o_ref):
    cv = c_ref[...]
    s = cv * jax.nn.sigmoid(cv)
    o_ref[0] = _dot(s.astype(BF16), w_ref[0].astype(BF16)) + b_ref[0]


def _ada(cvecs, w_ada, b_ada):
    n = N_MOD * D_MODEL
    return pl.pallas_call(
        _ada_kernel,
        grid=(DEPTH, n // ADA_TN),
        in_specs=[
            pl.BlockSpec((8, D_MODEL), lambda l, j: (0, 0)),
            pl.BlockSpec((1, D_MODEL, ADA_TN), lambda l, j: (l, 0, j)),
            pl.BlockSpec((1, 1, ADA_TN), lambda l, j: (l, 0, j)),
        ],
        out_specs=pl.BlockSpec((1, 8, ADA_TN), lambda l, j: (l, 0, j)),
        out_shape=jax.ShapeDtypeStruct((DEPTH, 8, n), F32),
        compiler_params=pltpu.CompilerParams(vmem_limit_bytes=VMEM_LIMIT),
        name="ada_mod",
    )(cvecs, w_ada, b_ada.reshape(DEPTH, 1, n))


def _inproj_kernel(*refs):
    *x_refs, mod_ref, g_ref, w_ref, o_ref, wb_ref = refs

    @pl.when(pl.program_id(0) == 0)
    def _():
        wb_ref[...] = w_ref[...].astype(BF16)

    mod = mod_ref[...]
    sh = mod[:, 0:D_MODEL]
    sc = mod[:, D_MODEL:2 * D_MODEL]
    h = _rms(_x_value(x_refs, TBP), g_ref[...]) * (1.0 + sc) + sh
    o_ref[...] = _dot(h.astype(BF16), wb_ref[...])


def _inproj(xs, mods, norm_g, w_in, l):
    return pl.pallas_call(
        _inproj_kernel,
        grid=(N_TOK // TBP,),
        in_specs=_x_specs(len(xs) == 2, tb=TBP) + [
            _mod_spec(l, TBP),
            _layer_spec((1, D_MODEL), l),
            _layer_spec((D_MODEL, D_IN), l, single=True),
        ],
        out_specs=pl.BlockSpec((TBP, D_IN), lambda i: (i, 0)),
        out_shape=jax.ShapeDtypeStruct((N_TOK, D_IN), F32),
        scratch_shapes=[pltpu.VMEM((D_MODEL, D_IN), BF16)],
        compiler_params=pltpu.CompilerParams(vmem_limit_bytes=VMEM_LIMIT),
        name="in_proj",
    )(*xs, mods, norm_g, w_in)


S5_NPAIR = S5_G // 2
S5_CW = S5_L * S5_P
S5_SW = S5_G * S5_N
S5_STEP_ROWS = N_CTX // S5_L
S5_CTX_NC = SEQ // S5_L
S5_LAT_NC = DEC_SEQ // S5_L


def _s5_slot(g, t):
    return (t + g) % S5_L


def _s5_tables(lam_re, lam_im, log_dt, b_re, b_im, c_re, c_im):
    lr = jnp.minimum(lam_re.astype(F32), LAM_RE_MAX)
    li = lam_im.astype(F32)
    dt = jnp.exp(log_dt.astype(F32))[..., None]
    mag = jnp.exp(lr * dt)
    ang = li * dt
    ab_re = mag * jnp.cos(ang)
    ab_im = mag * jnp.sin(ang)
    den = lr * lr + li * li
    xr = ab_re - 1.0
    z_re = (xr * lr + ab_im * li) / den
    z_im = (ab_im * lr - xr * li) / den
    bb_re = z_re[..., None] * b_re - z_im[..., None] * b_im
    bb_im = z_re[..., None] * b_im + z_im[..., None] * b_re
    tau = jnp.arange(S5_L + 1, dtype=F32)[:, None, None, None, None]
    pm = jnp.exp(lr * dt * tau)
    pa = li * dt * tau
    pw_re = pm * jnp.cos(pa)
    pw_im = pm * jnp.sin(pa)
    eye2 = jnp.eye(2, dtype=F32)

    def pw_pairs(a):
        a = a.transpose(1, 2, 3, 0, 4).reshape(DEPTH, 2, S5_NPAIR, 2, S5_L + 1, S5_N)
        return a.transpose(0, 1, 2, 4, 3, 5).reshape(DEPTH, 2, S5_NPAIR, S5_L + 1, 2 * S5_N)

    def mat_pairs(a):
        a = a.reshape(DEPTH, 2, S5_NPAIR, 2, S5_P, S5_N)
        return jnp.einsum('ldjaqn,ab->ldjaqbn', a, eye2).reshape(DEPTH, 2, S5_NPAIR, 2, S5_P, 2 * S5_N)

    al = jnp.stack([pw_re[S5_L], pw_im[S5_L]], axis=2).reshape(DEPTH, 2, 2, 1, S5_SW)
    return (pw_pairs(pw_re), pw_pairs(pw_im),
            mat_pairs(bb_re.transpose(0, 1, 2, 4, 3)), mat_pairs(bb_im.transpose(0, 1, 2, 4, 3)),
            mat_pairs(c_re.astype(F32)), mat_pairs(c_im.astype(F32)), al)


def _s5_shift_consts():
    n = S5_CW
    r = np.arange(n)[:, None]
    c = np.arange(n)[None, :]
    fwd = [(c == r + S5_P * s) for s in range(S5_L)]
    bwd = [(c == r - S5_P * (S5_L - 1 - s)) for s in range(S5_L)]
    return np.stack([np.concatenate(fwd, axis=1), np.concatenate(bwd, axis=1)]).astype(np.float32)


def _s5_prep_kernel(pwr_ref, pwi_ref, br_ref, bi_ref, cr_ref, ci_ref, scat_ref, m_ref, f_ref, e_ref, k_ref):
    L = S5_L
    hp = lax.Precision.HIGHEST
    sw = 2 * S5_N

    for j in range(S5_NPAIR):
        for d in range(2):
            pwr = pwr_ref[0, d, j]
            pwi = pwi_ref[0, d, j]
            for gl in range(2):
                g = 2 * j + gl
                br, bi = br_ref[0, d, j, gl], bi_ref[0, d, j, gl]
                cr, ci = cr_ref[0, d, j, gl], ci_ref[0, d, j, gl]
                ca_r, ca_i, f_r, f_i, e_r, e_i = [], [], [], [], [], []
                for t in range(L):
                    kk = t if d == 0 else L - 1 - t
                    pr, pi = pwr[kk:kk + 1], pwi[kk:kk + 1]
                    ca_r.append(cr * pr - ci * pi)
                    ca_i.append(cr * pi + ci * pr)
                    kf = L - 1 - t if d == 0 else t
                    pr, pi = pwr[kf:kf + 1], pwi[kf:kf + 1]
                    f_r.append(br * pr - bi * pi)
                    f_i.append(br * pi + bi * pr)
                    ke = t + 1 if d == 0 else L - t
                    pr, pi = pwr[ke:ke + 1], pwi[ke:ke + 1]
                    e_r.append(cr * pr - ci * pi)
                    e_i.append(-(cr * pi + ci * pr))
                cat = lambda xs: jnp.concatenate(xs, axis=0)
                k = _dot_nt(br, cat(ca_r), hp) - _dot_nt(bi, cat(ca_i), hp)
                k_ref[d, S5_P * g:S5_P * (g + 1), :] = k
                slots = lambda xs: cat([xs[(p - g) % L] for p in range(L)])
                rows = slice(S5_CW * gl, S5_CW * (gl + 1))
                f_ref[0, d, j, rows, 0:sw] = slots(f_r).astype(BF16)
                f_ref[0, d, j, rows, sw:2 * sw] = slots(f_i).astype(BF16)
                e_ref[0, d, 0, j, rows, :] = slots(e_r).astype(BF16)
                e_ref[0, d, 1, j, rows, :] = slots(e_i).astype(BF16)

    kf = k_ref[0].astype(BF16)
    kb = k_ref[1].astype(BF16)
    for s in range(L):
        cols = slice(S5_CW * s, S5_CW * (s + 1))
        res = _dot(kf, scat_ref[0, :, cols]) + _dot(kb, scat_ref[1, :, cols])
        for g in range(S5_G):
            blk = res[S5_P * g:S5_P * (g + 1), :]
            if g:
                blk = pltpu.roll(blk, S5_P * g, 1)
            r0 = S5_P * _s5_slot(g, s)
            m_ref[0, g, r0:r0 + S5_P, :] = blk.astype(BF16)


def _s5_prep(tables, scat):
    pwr, pwi, br, bi, cr, ci, _ = tables

    def lspec(shape):
        return pl.BlockSpec((1,) + shape, lambda l: (l,) + (0,) * len(shape))

    pw_shape = (2, S5_NPAIR, S5_L + 1, 2 * S5_N)
    mat_shape = (2, S5_NPAIR, 2, S5_P, 2 * S5_N)
    return pl.pallas_call(
        _s5_prep_kernel,
        grid=(DEPTH,),
        in_specs=[lspec(pw_shape), lspec(pw_shape), lspec(mat_shape), lspec(mat_shape), lspec(mat_shape),
                  lspec(mat_shape), _const_spec(scat.shape)],
        out_specs=(lspec((S5_G, S5_CW, S5_CW)), lspec((2, S5_NPAIR, 2 * S5_CW, 4 * S5_N)),
                   lspec((2, 2, S5_NPAIR, 2 * S5_CW, 2 * S5_N))),
        out_shape=(jax.ShapeDtypeStruct((DEPTH, S5_G, S5_CW, S5_CW), BF16),
                   jax.ShapeDtypeStruct((DEPTH, 2, S5_NPAIR, 2 * S5_CW, 4 * S5_N), BF16),
                   jax.ShapeDtypeStruct((DEPTH, 2, 2, S5_NPAIR, 2 * S5_CW, 2 * S5_N), BF16)),
        scratch_shapes=[pltpu.VMEM((2, S5_G * S5_P, S5_CW), F32)],
        compiler_params=pltpu.CompilerParams(vmem_limit_bytes=VMEM_LIMIT),
        name="s5_prep",
    )(pwr, pwi, br, bi, cr, ci, scat)


def _s5_kernel(xlo_ref, xhi_ref, m_ref, f_ref, e_ref, al_ref, h0_ref, ylo_ref, yhi_ref, hfin_ref,
               uy_ref, zh_ref):
    R = S5_STEP_ROWS
    pw = 2 * S5_CW
    sw = 2 * S5_N
    gph = S5_G // 2
    step = pl.program_id(0)

    for half, x_ref in enumerate((xlo_ref, xhi_ref)):
        for t in range(S5_L):
            xt = x_ref[pl.ds(t, R, stride=S5_L), :]
            shift = (S5_P * t) % LANES
            xr = pltpu.roll(xt, shift, 1) if shift else xt
            for gl in range(gph):
                g = gph * half + gl
                src = (S5_P * gl + shift) % LANES
                lo = S5_CW * g + S5_P * _s5_slot(g, t)
                assert lo % LANES == src
                uy_ref[:, lo:lo + S5_P] = xr[:, src:src + S5_P]

    for j in range(S5_NPAIR):
        ub = uy_ref[:, pw * j:pw * (j + 1)].astype(BF16)
        for d in range(2):
            o = _dot(ub, f_ref[d, j])
            zh_ref[d, 0, :, sw * j:sw * (j + 1)] = o[:, :sw]
            zh_ref[d, 1, :, sw * j:sw * (j + 1)] = o[:, sw:]

    def scan_group(seq_rows, nc, init, fin_rows):
        chains = [(d, r0) for d in range(2) for r0 in seq_rows]

        def body(c, carry):
            out = []
            for k, (d, r0) in enumerate(chains):
                hr, hi = carry[2 * k], carry[2 * k + 1]
                r = r0 + (c if d == 0 else nc - 1 - c)
                zr = zh_ref[d, 0, pl.ds(r, 1), :]
                zi = zh_ref[d, 1, pl.ds(r, 1), :]
                zh_ref[d, 0, pl.ds(r, 1), :] = hr
                zh_ref[d, 1, pl.ds(r, 1), :] = hi
                ar = al_ref[d, 0]
                ai = al_ref[d, 1]
                out.append(ar * hr - ai * hi + zr)
                out.append(ar * hi + ai * hr + zi)
            return tuple(out)

        fin = lax.fori_loop(0, nc, body, tuple(init))
        if fin_rows is not None:
            for k, (d, _) in enumerate(chains):
                s = fin_rows[k % len(seq_rows)]
                hfin_ref[d, 0, s:s + 1, :] = fin[2 * k]
                hfin_ref[d, 1, s:s + 1, :] = fin[2 * k + 1]

    @pl.when(step == 0)
    def _():
        zero = jnp.zeros((1, S5_SW), F32)
        for s0 in range(0, BATCH, 4):
            seqs = list(range(s0, s0 + 4))
            scan_group([s * S5_CTX_NC for s in seqs], S5_CTX_NC, [zero] * 16, seqs)

    @pl.when(step == 1)
    def _():
        init = []
        for d in range(2):
            for s in range(DEC_BATCH):
                init.append(h0_ref[d, 0, s:s + 1, :])
                init.append(h0_ref[d, 1, s:s + 1, :])
        scan_group([s * S5_LAT_NC for s in range(DEC_BATCH)], S5_LAT_NC, init, None)

    for j in range(S5_NPAIR):
        acc = None
        for d in range(2):
            for ri in range(2):
                hb = zh_ref[d, ri, :, sw * j:sw * (j + 1)].astype(BF16)
                t = _dot_nt(hb, e_ref[d, ri, j])
                acc = t if acc is None else acc + t
        for gl in range(2):
            g = 2 * j + gl
            ug = uy_ref[:, S5_CW * g:S5_CW * (g + 1)].astype(BF16)
            uy_ref[:, S5_CW * g:S5_CW * (g + 1)] = acc[:, S5_CW * gl:S5_CW * (gl + 1)] + _dot(ug, m_ref[g])

    lane = lax.broadcasted_iota(jnp.int32, (R, LANES), 1)
    in_piece = [jnp.logical_and(lane >= S5_P * k, lane < S5_P * (k + 1)) for k in range(LANES // S5_P)]
    for half, y_ref in enumerate((ylo_ref, yhi_ref)):
        for t in range(S5_L):
            merged = None
            for gl in range(gph):
                g = gph * half + gl
                slot = _s5_slot(g, t)
                col = S5_CW * g + LANES * (S5_P * slot // LANES)
                v = uy_ref[:, col:col + LANES]
                merged = v if merged is None else jnp.where(in_piece[slot % (LANES // S5_P)], v, merged)
            shift = (LANES - S5_P * t % LANES) % LANES
            y_ref[pl.ds(t, R, stride=S5_L), :] = pltpu.roll(merged, shift, 1) if shift else merged


def _s5(proj, m, f2, e2, al, h0, l):
    half = pl.BlockSpec((N_CTX, LANES), lambda s: (s, 0))
    return pl.pallas_call(
        _s5_kernel,
        grid=(2,),
        in_specs=[pl.BlockSpec((N_CTX, LANES), lambda s: (s, 0)),
                  pl.BlockSpec((N_CTX, LANES), lambda s: (s, 1)),
                  _layer_spec(m.shape[1:], l, single=True), _layer_spec(f2.shape[1:], l, single=True),
                  _layer_spec(e2.shape[1:], l, single=True), _layer_spec(al.shape[1:], l),
                  _layer_spec(h0.shape[1:], l)],
        out_specs=(half, half, _const_spec((2, 2, BATCH, S5_SW))),
        out_shape=(jax.ShapeDtypeStruct((N_TOK, LANES), F32),
                   jax.ShapeDtypeStruct((N_TOK, LANES), F32),
                   jax.ShapeDtypeStruct((2, 2, BATCH, S5_SW), F32)),
        scratch_shapes=[pltpu.VMEM((S5_STEP_ROWS, S5_G * S5_CW), F32),
                        pltpu.VMEM((2, 2, S5_STEP_ROWS, S5_SW), F32)],
        compiler_params=pltpu.CompilerParams(vmem_limit_bytes=VMEM_LIMIT),
        name="s5_mixer",
    )(proj, proj, m, f2, e2, al, h0)


HG_NLEV = int(math.log2(HG_L))
HG_W = HG_HEADS * HG_DK
HG_CPS = 4
HG_BLK = HG_CPS * HG_L
HG_NB_CTX = N_CTX // HG_BLK
HG_NC_CTX = SEQ // HG_BLK
HG_NC_LAT = DEC_SEQ // HG_BLK
assert HG_L == HG_DK


def _hgrn_consts():
    L = HG_L
    w = np.zeros((HG_NLEV + 2, L, L), np.float32)
    mask = np.zeros((HG_NLEV + 1, L, L), np.float32)
    for lev in range(HG_NLEV):
        blk = L >> lev
        half = blk // 2
        for t in range(L):
            p, o = divmod(t, blk)
            bd = p * blk + half - 1
            if o >= half:
                w[lev, t, bd + 1:t + 1] = 1.0
            else:
                w[lev, t, t + 1:bd + 1] = 1.0
        jj, ii = np.meshgrid(np.arange(L), np.arange(L), indexing='ij')
        mask[lev] = ((jj // blk == ii // blk) & (jj % blk >= half) & (ii % blk < half)).astype(np.float32)
    mask[HG_NLEV] = np.eye(L, dtype=np.float32)
    for t in range(L):
        w[HG_NLEV, t, :t + 1] = 1.0
        w[HG_NLEV + 1, t, t + 1:] = 1.0
    out = []
    for wd, md in ((w, mask), (w[:, ::-1, ::-1], mask[:, ::-1, ::-1])):
        wflat = wd.reshape((HG_NLEV + 2) * L, L)
        out.append((np.concatenate([wflat] * 3, axis=1), np.tile(md, (1, 1, HG_HEADS))))
    wcat = np.stack([out[0][0], out[1][0]])
    mask4 = np.stack([out[0][1], out[1][1]])
    hm = np.kron(np.eye(HG_HEADS, dtype=np.float32), np.ones((HG_DK, HG_DK), np.float32))
    return wcat, mask4, hm


def _hg_pos(i):
    is_ctx = i < HG_NB_CTX
    c = jnp.where(is_ctx, i % HG_NC_CTX, (i - HG_NB_CTX) % HG_NC_LAT)
    nc = jnp.where(is_ctx, HG_NC_CTX, HG_NC_LAT)
    return is_ctx, c, nc


def _hg_bwd_blk(i):
    _, c, nc = _hg_pos(i)
    return i + nc - 1 - 2 * c


def _hg_local(d, q, z, v, lbp, wcat_ref, mask_ref, hm):
    L = HG_L
    az = jnp.abs(z)
    sp = jnp.log1p(jnp.exp(-az))
    ls = jnp.minimum(z, 0.0) - sp
    kk = lbp[2:3] * jnp.exp(jnp.minimum(-z, 0.0) - sp)
    a = lbp[0:1]
    b = lbp[1:2] + ls
    logf = jnp.maximum(a, b) + jnp.log1p(jnp.exp(-jnp.abs(a - b)))
    p0 = logf.astype(BF16)
    r0 = logf - p0.astype(F32)
    p1 = r0.astype(BF16)
    p2 = (r0 - p1.astype(F32)).astype(BF16)
    ex = jnp.exp(_dot(wcat_ref[d], jnp.concatenate([p0, p1, p2], axis=0)))
    hmb = hm.astype(BF16)

    def bd4(x):
        return jnp.concatenate([x] * HG_HEADS, axis=0) * hmb

    sc = None
    for lev in range(HG_NLEV + 1):
        if lev < HG_NLEV:
            al = ex[L * lev:L * (lev + 1)]
            lhs = (q * al).astype(BF16)
            rhs = (kk * al).astype(BF16)
        else:
            lhs = q.astype(BF16)
            rhs = kk.astype(BF16)
        t = _dot_nt(lhs, bd4(rhs)) * mask_ref[d, lev]
        sc = t if sc is None else sc + t
    vb = v.astype(BF16)
    o = _dot(sc.astype(BF16), bd4(vb))
    eq = ex[L * HG_NLEV:L * (HG_NLEV + 1)]
    qe = (q * eq).astype(BF16)
    ke = (kk * ex[L * (HG_NLEV + 1):L * (HG_NLEV + 2)]).astype(BF16)
    g = eq[L - 1:L] if d == 0 else eq[0:1]
    return o, qe, g, _dot_tn(vb, ke) * hm


def _hgrn_kernel(qf, zf, vf, qb, zb, vb, lb_ref, w_ref, mask_ref, hm_ref, s0_ref,
                 of_ref, ob_ref, sfin_ref, st_ref):
    is_ctx, c, nc = _hg_pos(pl.program_id(0))

    @pl.when(c == 0)
    def _():
        st_ref[...] = jnp.where(is_ctx, 0.0, s0_ref[0])

    hm = hm_ref[...]
    for d, (q_ref, z_ref, v_ref, o_ref) in enumerate(((qf, zf, vf, of_ref), (qb, zb, vb, ob_ref))):
        rows = [slice(HG_L * k, HG_L * (k + 1)) for k in range(HG_CPS)]
        loc = [_hg_local(d, q_ref[r, :], z_ref[r, :], v_ref[r, :], lb_ref[d], w_ref, mask_ref, hm)
               for r in rows]
        st = st_ref[d]
        for k in (range(HG_CPS) if d == 0 else reversed(range(HG_CPS))):
            o, qe, g, upd = loc[k]
            o_ref[rows[k], :] = o + _dot_nt(qe, st.astype(BF16))
            st = st * g + upd
        st_ref[d] = st

    @pl.when(jnp.logical_and(is_ctx, c == nc - 1))
    def _():
        for d in range(2):
            s_kv = st_ref[d].T
            for h in range(HG_HEADS):
                sfin_ref[0, d, h] = s_kv[HG_DK * h:HG_DK * (h + 1), HG_DK * h:HG_DK * (h + 1)]


def _hgrn(proj, lbp, consts, s0, l):
    wcat, mask4, hm = consts

    def fwd(col):
        return pl.BlockSpec((HG_BLK, HG_W), lambda i: (i, col))

    def bwd(col):
        return pl.BlockSpec((HG_BLK, HG_W), lambda i: (_hg_bwd_blk(i), col))

    return pl.pallas_call(
        _hgrn_kernel,
        grid=(N_TOK // HG_BLK,),
        in_specs=[fwd(COL_HQ), fwd(COL_HF_FWD), fwd(COL_HI), bwd(COL_HQ), bwd(COL_HF_BWD), bwd(COL_HI),
                  _layer_spec(lbp.shape[1:], l), _const_spec(wcat.shape), _const_spec(mask4.shape),
                  _const_spec(hm.shape),
                  pl.BlockSpec((None, 1, 2, HG_W, HG_W),
                               lambda i: (l, jnp.maximum(i - HG_NB_CTX, 0) // HG_NC_LAT, 0, 0, 0))],
        out_specs=(pl.BlockSpec((HG_BLK, HG_W), lambda i: (i, 0)),
                   pl.BlockSpec((HG_BLK, HG_W), lambda i: (_hg_bwd_blk(i), 0)),
                   pl.BlockSpec((1, 2, HG_HEADS, HG_DK, HG_DK),
                                lambda i: (jnp.minimum(i // HG_NC_CTX, BATCH - 1), 0, 0, 0, 0))),
        out_shape=(jax.ShapeDtypeStruct((N_TOK, HG_W), F32),
                   jax.ShapeDtypeStruct((N_TOK, HG_W), F32),
                   jax.ShapeDtypeStruct((BATCH, 2, HG_HEADS, HG_DK, HG_DK), F32)),
        scratch_shapes=[pltpu.VMEM((2, HG_W, HG_W), F32)],
        compiler_params=pltpu.CompilerParams(vmem_limit_bytes=VMEM_LIMIT),
        name="hgrn_mixer",
    )(proj, proj, proj, proj, proj, proj, lbp, wcat, mask4, hm, s0)


FN_CTX_SPG = 4
FN_LAT_TR = 512


def _dft_consts(t_len):
    n = GROUP_W // 4
    k = np.arange(n)
    ang = 2.0 * np.pi * ((k[:, None] * k[None, :]) % n) / n
    eye = np.eye(4)
    cs = np.concatenate([np.kron(eye, np.cos(ang)), np.kron(eye, np.sin(ang))], axis=1) / math.sqrt(n)
    t = np.arange(t_len)
    angt = 2.0 * np.pi * ((t[:, None] * t[None, :]) % t_len) / t_len
    dft = np.concatenate([np.cos(angt), -np.sin(angt)], axis=1) / math.sqrt(t_len)
    return cs.astype(np.float32), dft.astype(np.float32)


def _fnet_kernel(x_ref, cs_ref, dft_ref, w_ref, o_ref, r_ref, *, t_len, spg, tr):
    @pl.when(pl.program_id(1) == 0)
    def _():
        t = _dot(x_ref[...].astype(BF16), cs_ref[...])
        for s in range(spg):
            rows = slice(t_len * s, t_len * (s + 1))
            r_ref[s, 0:t_len, :] = t[rows, :GROUP_W].astype(BF16)
            r_ref[s, t_len:2 * t_len, :] = t[rows, GROUP_W:].astype(BF16)

    w = w_ref[...].astype(BF16)
    for s in range(spg):
        y = _dot(dft_ref[...], r_ref[s])
        o_ref[tr * s:tr * (s + 1), :] = _dot(y.astype(BF16), w)


def _fnet(proj, cs, dft, fn_w, l, *, n_seq, seq_len, tok0, spg, tr):
    nj = seq_len // tr
    assert spg == 1 or nj == 1
    gb0 = tok0 // (spg * seq_len)
    return pl.pallas_call(
        functools.partial(_fnet_kernel, t_len=seq_len, spg=spg, tr=tr),
        grid=(n_seq // spg, nj),
        in_specs=[
            pl.BlockSpec((spg * seq_len, GROUP_W), lambda g, j: (gb0 + g, COL_XC)),
            _const_spec((GROUP_W, 2 * GROUP_W)),
            pl.BlockSpec((tr, 2 * seq_len), lambda g, j: (j, 0)),
            _layer_spec((GROUP_W, GROUP_W), l),
        ],
        out_specs=pl.BlockSpec((spg * tr, GROUP_W), lambda g, j: (g * nj + j, 0)),
        out_shape=jax.ShapeDtypeStruct((n_seq * seq_len, GROUP_W), F32),
        scratch_shapes=[pltpu.VMEM((spg, 2 * seq_len, GROUP_W), BF16)],
        compiler_params=pltpu.CompilerParams(vmem_limit_bytes=VMEM_LIMIT),
        name="fnet_mixer",
    )(proj, cs, dft, fn_w)


def _outproj_kernel(*refs):
    (xa_ref, hgate_ref, gu_ref, gv_ref, ylo_ref, yhi_ref, of_ref, ob_ref, ycc_ref, ycl_ref, mod_ref, d_ref,
     wglu_ref, gn_ref, bdm_ref, gmn_ref, gmw_ref, gmb_ref, gmh_ref, wout_ref, n2_ref,
     x1_ref, h2_ref, wob_ref) = refs[-24:]
    x_refs = refs[:-24]

    @pl.when(pl.program_id(0) == 0)
    def _():
        wob_ref[...] = wout_ref[...].astype(BF16)

    gn = gn_ref[...]
    ys = jnp.concatenate([ylo_ref[...], yhi_ref[...]], axis=1)
    y5 = _gelu(ys + d_ref[...] * xa_ref[...])
    glu = jax.nn.sigmoid(_dot(y5.astype(BF16), wglu_ref[...].astype(BF16)))
    out_a = _rms(y5 * glu, gn[:, 0:GROUP_W])
    o = of_ref[...] + ob_ref[...]
    o2 = o * o
    o2h = o2.astype(BF16)
    o2l = (o2 - o2h.astype(F32)).astype(BF16)
    ms = _dot(o2h, bdm_ref[...]) + _dot(o2l, bdm_ref[...])
    hg = hgate_ref[...]
    out_b = o * lax.rsqrt(ms + EPS) * gn[:, GROUP_W:2 * GROUP_W] * (hg * jax.nn.sigmoid(hg))
    out_c = _rms(_x_value((ycc_ref, ycl_ref), TBP), gn[:, 2 * GROUP_W:3 * GROUP_W])
    gm_hm = gmh_ref[...]
    out_d = []
    for c in range(TBP // GM_CHUNK):
        rows = slice(GM_CHUNK * c, GM_CHUNK * (c + 1))
        gv = _rms(_gelu(gv_ref[rows, :]), gmn_ref[...]).astype(BF16)
        g4 = jnp.concatenate([gv] * GM_HEADS, axis=0) * gm_hm
        sp = _dot(gmw_ref[...], g4) + gmb_ref[...]
        out_d.append(_rms(_gelu(gu_ref[rows, :]) * sp, gn[:, 3 * GROUP_W:]))
    out_d = jnp.concatenate(out_d, axis=0)
    m = None
    for k, part in enumerate((out_a, out_b, out_c, out_d)):
        t = _dot(part.astype(BF16), wob_ref[GROUP_W * k:GROUP_W * (k + 1), :])
        m = t if m is None else m + t
    mod = mod_ref[...]
    g1 = mod[:, 2 * D_MODEL:3 * D_MODEL]
    sh2 = mod[:, 3 * D_MODEL:4 * D_MODEL]
    sc2 = mod[:, 4 * D_MODEL:5 * D_MODEL]
    x1 = _x_value(x_refs, TBP) + g1 * m
    x1_ref[...] = x1
    h2_ref[...] = (_rms(x1, n2_ref[...]) * (1.0 + sc2) + sh2).astype(BF16)


def _outproj(xs, proj, ylo, yhi, of, ob, yc, mods, s5_d, wglu, gn, bdm, gm_norm_g, gm_w, gm_b, gm_hm, wout,
             norm2_g, l):
    def tok(width, col=0):
        return pl.BlockSpec((TBP, width), lambda i: (i, col))

    return pl.pallas_call(
        _outproj_kernel,
        grid=(N_TOK // TBP,),
        in_specs=_x_specs(len(xs) == 2, tb=TBP) + [
            tok(GROUP_W, COL_XA), tok(GROUP_W, COL_HGATE), tok(GROUP_W, COL_GU), tok(GROUP_W, COL_GV),
            tok(LANES), tok(LANES), tok(GROUP_W), tok(GROUP_W), *_x_specs(True, GROUP_W, TBP),
            _mod_spec(l, TBP),
            _layer_spec((1, GROUP_W), l), _layer_spec((GROUP_W, GROUP_W), l), _layer_spec((1, D_MODEL), l),
            _const_spec((GROUP_W, GROUP_W)),
            _layer_spec((1, GROUP_W), l), _layer_spec((GM_CHUNK, GM_HEADS * GM_CHUNK), l),
            _layer_spec((GM_CHUNK, GROUP_W), l), _const_spec((GM_HEADS * GM_CHUNK, GROUP_W)),
            _layer_spec((D_MODEL, D_MODEL), l, single=True), _layer_spec((1, D_MODEL), l),
        ],
        out_specs=(tok(D_MODEL), tok(D_MODEL)),
        out_shape=(jax.ShapeDtypeStruct((N_TOK, D_MODEL), F32),
                   jax.ShapeDtypeStruct((N_TOK, D_MODEL), BF16)),
        scratch_shapes=[pltpu.VMEM((D_MODEL, D_MODEL), BF16)],
        compiler_params=pltpu.CompilerParams(vmem_limit_bytes=VMEM_LIMIT),
        name="out_proj",
    )(*xs, proj, proj, proj, proj, ylo, yhi, of, ob, *yc, mods, s5_d, wglu, gn, bdm, gm_norm_g, gm_w, gm_b,
      gm_hm, wout, norm2_g)


FF_SEG = GRID_W
FF_NSEG = TB // FF_SEG
SUBLANES = 8
FF_NGRP = FF_SEG // SUBLANES
GELU_C0 = 0.7978845608028654
GELU_C1 = GELU_C0 * 0.044715


def _ffn_perm_consts():
    p = np.zeros((TB, TB), np.float32)
    for k in range(FF_NSEG):
        for t in range(FF_SEG):
            p[FF_SEG * k + SUBLANES * (t % FF_NGRP) + t // FF_NGRP, FF_SEG * k + t] = 1.0
    return p


def _ffn_kernel(x1_ref, h2_ref, mod_ref, perm_ref, wup_ref, cw_ref, cb_ref, wdn_ref, fg_ref, *rest, final):
    *o_refs, h2p_ref, z_ref, hid_ref = rest
    i = pl.program_id(0)
    joined = (i < NB_CTX).astype(F32)
    n_tiles = D_FF // FF_TILE
    sub = lax.broadcasted_iota(jnp.int32, (SUBLANES, 2 * FF_TILE), 0)
    is_first = sub == 0
    is_last = sub == SUBLANES - 1
    zero_grp = jnp.zeros((SUBLANES, 2 * FF_TILE), F32)
    h2p_ref[...] = _dot(perm_ref[...], h2_ref[...]).astype(BF16)

    def up(j):
        lo = FF_TILE * j
        for c, col in enumerate((lo, D_FF + lo)):
            z_ref[j % 2, :, FF_TILE * c:FF_TILE * (c + 1)] = _dot(h2p_ref[...], wup_ref[:, col:col + FF_TILE])

    def gate(j):
        slot = j % 2
        lo = FF_TILE * j
        w = jnp.concatenate([cw_ref[:, lo:lo + FF_TILE], cw_ref[:, D_FF + lo:D_FF + lo + FF_TILE]], axis=1)
        b = jnp.concatenate([cb_ref[:, lo:lo + FF_TILE], cb_ref[:, D_FF + lo:D_FF + lo + FF_TILE]], axis=1)

        def grp(k, v):
            r = FF_SEG * k + SUBLANES * v
            return z_ref[slot, r:r + SUBLANES, :]

        down = [pltpu.roll(grp(k, FF_NGRP - 1), 1, 0) for k in range(FF_NSEG)]
        up_ = [pltpu.roll(grp(k, 0), SUBLANES - 1, 0) for k in range(FF_NSEG)]
        strips = []
        for k in range(FF_NSEG):
            g = [grp(k, v) for v in range(FF_NGRP)]
            before = down[k - 1] * joined if k > 0 else zero_grp
            after = up_[k + 1] * joined if k < FF_NSEG - 1 else zero_grp
            zm1 = jnp.concatenate([jnp.where(is_first, before, down[k])] + g[:-1], axis=0)
            zp1 = jnp.concatenate(g[1:] + [jnp.where(is_last, after, up_[k])], axis=0)
            zc = b + zm1 * w[0:1] + jnp.concatenate(g, axis=0) * w[1:2] + zp1 * w[2:3]
            a = zc[:, :FF_TILE]
            t = jnp.tanh(a * (GELU_C0 + GELU_C1 * (a * a)))
            strips.append((a * (0.5 + 0.5 * t) * zc[:, FF_TILE:]).astype(BF16))
        hid_ref[:, lo:lo + FF_TILE] = jnp.concatenate(strips, axis=0)

    up(0)
    for j in range(n_tiles):
        if j + 1 < n_tiles:
            up(j + 1)
        gate(j)
    acc = _dot(hid_ref[...], wdn_ref[...])
    acc = jnp.swapaxes(acc.reshape(FF_NSEG, FF_NGRP, SUBLANES, D_MODEL), 1, 2).reshape(TB, D_MODEL)
    g2 = mod_ref[...][:, 5 * D_MODEL:6 * D_MODEL]
    x2 = x1_ref[...] + g2 * acc
    if final:
        y = _rms(x2, fg_ref[...])
        yp_ref, ys_ref = o_refs

        @pl.when(i < NB_CTX)
        def _():
            yp_ref[...] = y

        @pl.when(i >= NB_CTX)
        def _():
            ys_ref[...] = y
    else:
        o_refs[0][...] = x2


def _ffn(x1, h2, mods, perm, wup_bf, conv_w, conv_b, wdn_bf, final_g, l, *, final):
    if final:
        out_specs = tuple(_x_specs(True))
        out_shape = (jax.ShapeDtypeStruct((N_CTX, D_MODEL), F32), jax.ShapeDtypeStruct((N_LAT, D_MODEL), F32))
    else:
        out_specs = pl.BlockSpec((TB, D_MODEL), lambda i: (i, 0))
        out_shape = jax.ShapeDtypeStruct((N_TOK, D_MODEL), F32)
    return pl.pallas_call(
        functools.partial(_ffn_kernel, final=final),
        grid=(N_TOK // TB,),
        in_specs=[
            pl.BlockSpec((TB, D_MODEL), lambda i: (i, 0)),
            pl.BlockSpec((TB, D_MODEL), lambda i: (i, 0)),
            _mod_spec(l), _const_spec((TB, TB)),
            _layer_spec((D_MODEL, 2 * D_FF), l, single=True), _layer_spec((3, 2 * D_FF), l),
            _layer_spec((1, 2 * D_FF), l), _layer_spec((D_FF, D_MODEL), l, single=True),
            _const_spec((1, D_MODEL)),
        ],
        out_specs=out_specs,
        out_shape=out_shape,
        scratch_shapes=[pltpu.VMEM((TB, D_MODEL), BF16), pltpu.VMEM((2, TB, 2 * FF_TILE), F32),
                        pltpu.VMEM((TB, D_FF), BF16)],
        compiler_params=pltpu.CompilerParams(vmem_limit_bytes=VMEM_LIMIT),
        name="conv_ffn",
    )(x1, h2, mods, perm, wup_bf, conv_w, conv_b, wdn_bf, final_g.reshape(1, D_MODEL))


def kernel(x_prompt, x_sample, state_s5_re, state_s5_im, state_hgrn, c, c_ctx, w_ada, b_ada, norm1_g,
           norm2_g, w_in, s5_lam_re, s5_lam_im, s5_log_dt, s5_b_re, s5_b_im, s5_c_re, s5_c_im, s5_d,
           s5_w_glu, hg_lb_logits, fn_w, gm_norm_g, gm_ws, gm_bs, grp_norm_g, w_out, ffn_w_up,
           ffn_conv_w, ffn_conv_b, ffn_w_down, final_norm_g):
    xs = (x_prompt.reshape(N_CTX, D_MODEL), x_sample.reshape(N_LAT, D_MODEL))
    cvecs = jnp.concatenate([c_ctx[None], c, jnp.zeros((8 - 1 - DEC_BATCH, D_MODEL), F32)], axis=0)
    mods = _ada(cvecs, w_ada, b_ada).reshape(DEPTH, 8, 1, N_MOD * D_MODEL)

    lb_p = jax.nn.softmax(hg_lb_logits.astype(F32), axis=0)
    lbs = jnp.maximum(jnp.cumsum(lb_p, axis=0) - lb_p[0], 0.0)
    lbps = jnp.stack([jnp.log(lbs), jnp.log1p(-lbs), 1.0 - lbs], axis=2)

    hg_consts_np = _hgrn_consts()
    hg_consts = (jnp.asarray(hg_consts_np[0], BF16), jnp.asarray(hg_consts_np[1], F32),
                 jnp.asarray(hg_consts_np[2], F32))
    cs_np, dft_ctx_np = _dft_consts(SEQ)
    _, dft_lat_np = _dft_consts(DEC_SEQ)
    cs = jnp.asarray(cs_np, F32).astype(BF16)
    dft_ctx = jnp.asarray(dft_ctx_np, F32).astype(BF16)
    dft_lat = jnp.asarray(dft_lat_np, F32).astype(BF16)
    gm_hm = jnp.asarray(np.kron(np.eye(GM_HEADS), np.ones((GM_CHUNK, GROUP_W // GM_HEADS))), BF16)
    bdm = jnp.asarray(np.kron(np.eye(HG_HEADS), np.ones((HG_DK, HG_DK))) / HG_DK, BF16)

    tables = _s5_tables(s5_lam_re, s5_lam_im, s5_log_dt, s5_b_re, s5_b_im, s5_c_re, s5_c_im)
    s5_m, s5_f, s5_e = _s5_prep(tables, jnp.asarray(_s5_shift_consts(), BF16))
    s5_al = tables[-1]
    s5_h0 = jnp.stack([state_s5_re, state_s5_im]).astype(F32).transpose(2, 3, 0, 1, 4, 5)
    s5_h0 = s5_h0.reshape(DEPTH, 2, 2, DEC_BATCH, S5_SW)
    hg_s0 = jnp.einsum('bldhkv,hg->lbdhvgk', state_hgrn.astype(F32), jnp.eye(HG_HEADS, dtype=F32))
    hg_s0 = hg_s0.reshape(DEPTH, DEC_BATCH, 2, HG_W, HG_W)

    def rows(a):
        return a.reshape(DEPTH, 1, a.shape[-1])

    norm1_r, norm2_r, gn_r, s5_d_r, gmn_r, cb_r = (rows(a) for a in (
        norm1_g, norm2_g, grp_norm_g, s5_d, gm_norm_g, ffn_conv_b))
    gm_w = gm_ws.transpose(0, 2, 1, 3).reshape(DEPTH, GM_CHUNK, GM_HEADS * GM_CHUNK).astype(BF16)
    gm_b = jnp.repeat(gm_bs.transpose(0, 2, 1), GROUP_W // GM_HEADS, axis=2)
    ff_perm = jnp.asarray(_ffn_perm_consts(), BF16)
    wup_bf = ffn_w_up.astype(BF16)
    wdn_bf = ffn_w_down.astype(BF16)

    new_re, new_im, new_hg = [], [], []
    for l in range(DEPTH):
        proj = _inproj(xs, mods, norm1_r, w_in, l)

        ylo, yhi, hfin = _s5(proj, s5_m, s5_f, s5_e, s5_al, s5_h0, l)
        hfin = hfin.reshape(2, 2, BATCH, S5_G, S5_N).transpose(1, 2, 0, 3, 4)
        new_re.append(hfin[0])
        new_im.append(hfin[1])

        of, ob, sfin = _hgrn(proj, lbps, hg_consts, hg_s0, l)
        new_hg.append(sfin)

        yc = (_fnet(proj, cs, dft_ctx, fn_w, l, n_seq=BATCH, seq_len=SEQ, tok0=0, spg=FN_CTX_SPG, tr=SEQ),
              _fnet(proj, cs, dft_lat, fn_w, l, n_seq=DEC_BATCH, seq_len=DEC_SEQ, tok0=N_CTX, spg=1,
                    tr=FN_LAT_TR))

        x1, h2 = _outproj(xs, proj, ylo, yhi, of, ob, yc, mods, s5_d_r, s5_w_glu, gn_r, bdm, gmn_r, gm_w, gm_b,
                          gm_hm, w_out, norm2_r, l)
        res = _ffn(x1, h2, mods, ff_perm, wup_bf, ffn_conv_w, cb_r, wdn_bf, final_norm_g, l,
                   final=(l == DEPTH - 1))
        xs = (res,)

    y_prompt = res[0].reshape(BATCH, SEQ, D_MODEL)
    y_sample = res[1].reshape(DEC_BATCH, DEC_SEQ, D_MODEL)
    return (y_prompt, y_sample, jnp.stack(new_re, axis=1), jnp.stack(new_im, axis=1),
            jnp.stack(new_hg, axis=1))
```

```python
import functools
import math

import numpy as np
import jax
import jax.numpy as jnp
from jax import lax
from jax.experimental import pallas as pl
from jax.experimental.pallas import tpu as pltpu

D_MODEL = 1024
BATCH = 16
SEQ = 256
DEPTH = 2
DEC_BATCH = 2
DEC_SEQ = 2048
GRID_W = 64
GROUP_W = 256
S5_P = 16
S5_G = 16
S5_N = 64
HG_HEADS = 4
HG_DK = 64
GM_HEADS = 4
GM_CHUNK = 128
D_FF = 2816
N_MOD = 6
D_IN = 9 * GROUP_W
EPS = 1e-6
LAM_RE_MAX = -1e-4

N_CTX = BATCH * SEQ
N_LAT = DEC_BATCH * DEC_SEQ
N_TOK = N_CTX + N_LAT
TB = 512
TBP = 512
NB_CTX = N_CTX // TB
S5_L = 16
HG_L = 64
FF_TILE = 256
LANES = 128
VMEM_LIMIT = 56 * 1024 * 1024

F32 = jnp.float32
BF16 = jnp.bfloat16

COL_XA, COL_HQ, COL_HF_FWD, COL_HF_BWD, COL_HI, COL_HGATE, COL_XC, COL_GU, COL_GV = range(9)


def _mod_row(i, tb):
    return jnp.where(i < N_CTX // tb, 0, 1 + (i - N_CTX // tb) // (DEC_SEQ // tb))


def _gelu(x):
    return 0.5 * x * (1.0 + jnp.tanh(0.7978845608028654 * (x + 0.044715 * (x * x * x))))


def _rms(x, g):
    return x * lax.rsqrt(jnp.mean(x * x, axis=-1, keepdims=True) + EPS) * g


def _dot(a, b):
    return jnp.dot(a, b, preferred_element_type=F32)


def _dot_nt(a, b, precision=None):
    return lax.dot_general(a, b, (((1,), (1,)), ((), ())), precision=precision, preferred_element_type=F32)


def _dot_tn(a, b):
    return lax.dot_general(a, b, (((0,), (0,)), ((), ())), preferred_element_type=F32)


def _const_spec(shape, single=False):
    kw = {"pipeline_mode": pl.Buffered(1)} if single else {}
    return pl.BlockSpec(shape, lambda *_: (0,) * len(shape), **kw)


def _layer_spec(shape, l, single=False):
    kw = {"pipeline_mode": pl.Buffered(1)} if single else {}
    return pl.BlockSpec((None,) + tuple(shape), lambda *_: (l,) + (0,) * len(shape), **kw)


def _mod_spec(l, tb=TB):
    return pl.BlockSpec((None, None, 1, N_MOD * D_MODEL), lambda i: (l, _mod_row(i, tb), 0, 0))


def _x_specs(split, width=D_MODEL, tb=TB):
    nb_ctx = N_CTX // tb
    if split:
        return [pl.BlockSpec((tb, width), lambda i: (jnp.minimum(i, nb_ctx - 1), 0)),
                pl.BlockSpec((tb, width), lambda i: (jnp.maximum(i - nb_ctx, 0), 0))]
    return [pl.BlockSpec((tb, width), lambda i: (i, 0))]


def _x_value(x_refs, tb=TB):
    if len(x_refs) == 1:
        return x_refs[0][...]
    return jnp.where(pl.program_id(0) < N_CTX // tb, x_refs[0][...], x_refs[1][...])


ADA_TN = 1536


def _ada_kernel(c_ref, w_ref, b_ref, o_ref):
    cv = c_ref[...]
    s = cv * jax.nn.sigmoid(cv)
    o_ref[0] = _dot(s.astype(BF16), w_ref[0].astype(BF16)) + b_ref[0]


def _ada(cvecs, w_ada, b_ada):
    n = N_MOD * D_MODEL
    return pl.pallas_call(
        _ada_kernel,
        grid=(DEPTH, n // ADA_TN),
        in_specs=[
            pl.BlockSpec((8, D_MODEL), lambda l, j: (0, 0)),
            pl.BlockSpec((1, D_MODEL, ADA_TN), lambda l, j: (l, 0, j)),
            pl.BlockSpec((1, 1, ADA_TN), lambda l, j: (l, 0, j)),
        ],
        out_specs=pl.BlockSpec((1, 8, ADA_TN), lambda l, j: (l, 0, j)),
        out_shape=jax.ShapeDtypeStruct((DEPTH, 8, n), F32),
        compiler_params=pltpu.CompilerParams(vmem_limit_bytes=VMEM_LIMIT),
        name="ada_mod",
    )(cvecs, w_ada, b_ada.reshape(DEPTH, 1, n))


def _inproj_kernel(*refs):
    *x_refs, mod_ref, g_ref, w_ref, o_ref, wb_ref = refs

    @pl.when(pl.program_id(0) == 0)
    def _():
        wb_ref[...] = w_ref[...].astype(BF16)

    mod = mod_ref[...]
    sh = mod[:, 0:D_MODEL]
    sc = mod[:, D_MODEL:2 * D_MODEL]
    h = _rms(_x_value(x_refs, TBP), g_ref[...]) * (1.0 + sc) + sh
    o_ref[...] = _dot(h.astype(BF16), wb_ref[...])


def _inproj(xs, mods, norm_g, w_in, l):
    return pl.pallas_call(
        _inproj_kernel,
        grid=(N_TOK // TBP,),
        in_specs=_x_specs(len(xs) == 2, tb=TBP) + [
            _mod_spec(l, TBP),
            _layer_spec((1, D_MODEL), l),
            _layer_spec((D_MODEL, D_IN), l, single=True),
        ],
        out_specs=pl.BlockSpec((TBP, D_IN), lambda i: (i, 0)),
        out_shape=jax.ShapeDtypeStruct((N_TOK, D_IN), F32),
        scratch_shapes=[pltpu.VMEM((D_MODEL, D_IN), BF16)],
        compiler_params=pltpu.CompilerParams(vmem_limit_bytes=VMEM_LIMIT),
        name="in_proj",
    )(*xs, mods, norm_g, w_in)


S5_NPAIR = S5_G // 2
S5_CW = S5_L * S5_P
S5_SW = S5_G * S5_N
S5_STEP_ROWS = N_CTX // S5_L
S5_CTX_NC = SEQ // S5_L
S5_LAT_NC = DEC_SEQ // S5_L


def _s5_slot(g, t):
    return (t + g) % S5_L


def _s5_tables(lam_re, lam_im, log_dt, b_re, b_im, c_re, c_im):
    lr = jnp.minimum(lam_re.astype(F32), LAM_RE_MAX)
    li = lam_im.astype(F32)
    dt = jnp.exp(log_dt.astype(F32))[..., None]
    mag = jnp.exp(lr * dt)
    ang = li * dt
    ab_re = mag * jnp.cos(ang)
    ab_im = mag * jnp.sin(ang)
    den = lr * lr + li * li
    xr = ab_re - 1.0
    z_re = (xr * lr + ab_im * li) / den
    z_im = (ab_im * lr - xr * li) / den
    bb_re = z_re[..., None] * b_re - z_im[..., None] * b_im
    bb_im = z_re[..., None] * b_im + z_im[..., None] * b_re
    tau = jnp.arange(S5_L + 1, dtype=F32)[:, None, None, None, None]
    pm = jnp.exp(lr * dt * tau)
    pa = li * dt * tau
    pw_re = pm * jnp.cos(pa)
    pw_im = pm * jnp.sin(pa)
    eye2 = jnp.eye(2, dtype=F32)

    def pw_pairs(a):
        a = a.transpose(1, 2, 3, 0, 4).reshape(DEPTH, 2, S5_NPAIR, 2, S5_L + 1, S5_N)
        return a.transpose(0, 1, 2, 4, 3, 5).reshape(DEPTH, 2, S5_NPAIR, S5_L + 1, 2 * S5_N)

    def mat_pairs(a):
        a = a.reshape(DEPTH, 2, S5_NPAIR, 2, S5_P, S5_N)
        return jnp.einsum('ldjaqn,ab->ldjaqbn', a, eye2).reshape(DEPTH, 2, S5_NPAIR, 2, S5_P, 2 * S5_N)

    al = jnp.stack([pw_re[S5_L], pw_im[S5_L]], axis=2).reshape(DEPTH, 2, 2, 1, S5_SW)
    return (pw_pairs(pw_re), pw_pairs(pw_im),
            mat_pairs(bb_re.transpose(0, 1, 2, 4, 3)), mat_pairs(bb_im.transpose(0, 1, 2, 4, 3)),
            mat_pairs(c_re.astype(F32)), mat_pairs(c_im.astype(F32)), al)


def _s5_shift_consts():
    n = S5_CW
    r = np.arange(n)[:, None]
    c = np.arange(n)[None, :]
    fwd = [(c == r + S5_P * s) for s in range(S5_L)]
    bwd = [(c == r - S5_P * (S5_L - 1 - s)) for s in range(S5_L)]
    return np.stack([np.concatenate(fwd, axis=1), np.concatenate(bwd, axis=1)]).astype(np.float32)


def _s5_prep_kernel(pwr_ref, pwi_ref, br_ref, bi_ref, cr_ref, ci_ref, scat_ref, m_ref, f_ref, e_ref, k_ref):
    L = S5_L
    hp = lax.Precision.HIGHEST
    sw = 2 * S5_N

    for j in range(S5_NPAIR):
        for d in range(2):
            pwr = pwr_ref[0, d, j]
            pwi = pwi_ref[0, d, j]
            for gl in range(2):
                g = 2 * j + gl
                br, bi = br_ref[0, d, j, gl], bi_ref[0, d, j, gl]
                cr, ci = cr_ref[0, d, j, gl], ci_ref[0, d, j, gl]
                ca_r, ca_i, f_r, f_i, e_r, e_i = [], [], [], [], [], []
                for t in range(L):
                    kk = t if d == 0 else L - 1 - t
                    pr, pi = pwr[kk:kk + 1], pwi[kk:kk + 1]
                    ca_r.append(cr * pr - ci * pi)
                    ca_i.append(cr * pi + ci * pr)
                    kf = L - 1 - t if d == 0 else t
                    pr, pi = pwr[kf:kf + 1], pwi[kf:kf + 1]
                    f_r.append(br * pr - bi * pi)
                    f_i.append(br * pi + bi * pr)
                    ke = t + 1 if d == 0 else L - t
                    pr, pi = pwr[ke:ke + 1], pwi[ke:ke + 1]
                    e_r.append(cr * pr - ci * pi)
                    e_i.append(-(cr * pi + ci * pr))
                cat = lambda xs: jnp.concatenate(xs, axis=0)
                k = _dot_nt(br, cat(ca_r), hp) - _dot_nt(bi, cat(ca_i), hp)
                k_ref[d, S5_P * g:S5_P * (g + 1), :] = k
                slots = lambda xs: cat([xs[(p - g) % L] for p in range(L)])
                rows = slice(S5_CW * gl, S5_CW * (gl + 1))
                f_ref[0, d, j, rows, 0:sw] = slots(f_r).astype(BF16)
                f_ref[0, d, j, rows, sw:2 * sw] = slots(f_i).astype(BF16)
                e_ref[0, d, 0, j, rows, :] = slots(e_r).astype(BF16)
                e_ref[0, d, 1, j, rows, :] = slots(e_i).astype(BF16)

    kf = k_ref[0].astype(BF16)
    kb = k_ref[1].astype(BF16)
    for s in range(L):
        cols = slice(S5_CW * s, S5_CW * (s + 1))
        res = _dot(kf, scat_ref[0, :, cols]) + _dot(kb, scat_ref[1, :, cols])
        for g in range(S5_G):
            blk = res[S5_P * g:S5_P * (g + 1), :]
            if g:
                blk = pltpu.roll(blk, S5_P * g, 1)
            r0 = S5_P * _s5_slot(g, s)
            m_ref[0, g, r0:r0 + S5_P, :] = blk.astype(BF16)


def _s5_prep(tables, scat):
    pwr, pwi, br, bi, cr, ci, _ = tables

    def lspec(shape):
        return pl.BlockSpec((1,) + shape, lambda l: (l,) + (0,) * len(shape))

    pw_shape = (2, S5_NPAIR, S5_L + 1, 2 * S5_N)
    mat_shape = (2, S5_NPAIR, 2, S5_P, 2 * S5_N)
    return pl.pallas_call(
        _s5_prep_kernel,
        grid=(DEPTH,),
        in_specs=[lspec(pw_shape), lspec(pw_shape), lspec(mat_shape), lspec(mat_shape), lspec(mat_shape),
                  lspec(mat_shape), _const_spec(scat.shape)],
        out_specs=(lspec((S5_G, S5_CW, S5_CW)), lspec((2, S5_NPAIR, 2 * S5_CW, 4 * S5_N)),
                   lspec((2, 2, S5_NPAIR, 2 * S5_CW, 2 * S5_N))),
        out_shape=(jax.ShapeDtypeStruct((DEPTH, S5_G, S5_CW, S5_CW), BF16),
                   jax.ShapeDtypeStruct((DEPTH, 2, S5_NPAIR, 2 * S5_CW, 4 * S5_N), BF16),
                   jax.ShapeDtypeStruct((DEPTH, 2, 2, S5_NPAIR, 2 * S5_CW, 2 * S5_N), BF16)),
        scratch_shapes=[pltpu.VMEM((2, S5_G * S5_P, S5_CW), F32)],
        compiler_params=pltpu.CompilerParams(vmem_limit_bytes=VMEM_LIMIT),
        name="s5_prep",
    )(pwr, pwi, br, bi, cr, ci, scat)


def _s5_kernel(xlo_ref, xhi_ref, m_ref, f_ref, e_ref, al_ref, h0_ref, ylo_ref, yhi_ref, hfin_ref,
               uy_ref, zh_ref):
    R = S5_STEP_ROWS
    pw = 2 * S5_CW
    sw = 2 * S5_N
    gph = S5_G // 2
    step = pl.program_id(0)

    for half, x_ref in enumerate((xlo_ref, xhi_ref)):
        for t in range(S5_L):
            xt = x_ref[pl.ds(t, R, stride=S5_L), :]
            shift = (S5_P * t) % LANES
            xr = pltpu.roll(xt, shift, 1) if shift else xt
            for gl in range(gph):
                g = gph * half + gl
                src = (S5_P * gl + shift) % LANES
                lo = S5_CW * g + S5_P * _s5_slot(g, t)
                assert lo % LANES == src
                uy_ref[:, lo:lo + S5_P] = xr[:, src:src + S5_P]

    for j in range(S5_NPAIR):
        ub = uy_ref[:, pw * j:pw * (j + 1)].astype(BF16)
        for d in range(2):
            o = _dot(ub, f_ref[d, j])
            zh_ref[d, 0, :, sw * j:sw * (j + 1)] = o[:, :sw]
            zh_ref[d, 1, :, sw * j:sw * (j + 1)] = o[:, sw:]

    def scan_group(seq_rows, nc, init, fin_rows):
        chains = [(d, r0) for d in range(2) for r0 in seq_rows]

        def body(c, carry):
            out = []
            for k, (d, r0) in enumerate(chains):
                hr, hi = carry[2 * k], carry[2 * k + 1]
                r = r0 + (c if d == 0 else nc - 1 - c)
                zr = zh_ref[d, 0, pl.ds(r, 1), :]
                zi = zh_ref[d, 1, pl.ds(r, 1), :]
                zh_ref[d, 0, pl.ds(r, 1), :] = hr
                zh_ref[d, 1, pl.ds(r, 1), :] = hi
                ar = al_ref[d, 0]
                ai = al_ref[d, 1]
                out.append(ar * hr - ai * hi + zr)
                out.append(ar * hi + ai * hr + zi)
            return tuple(out)

        fin = lax.fori_loop(0, nc, body, tuple(init))
        if fin_rows is not None:
            for k, (d, _) in enumerate(chains):
                s = fin_rows[k % len(seq_rows)]
                hfin_ref[d, 0, s:s + 1, :] = fin[2 * k]
                hfin_ref[d, 1, s:s + 1, :] = fin[2 * k + 1]

    @pl.when(step == 0)
    def _():
        zero = jnp.zeros((1, S5_SW), F32)
        for s0 in range(0, BATCH, 4):
            seqs = list(range(s0, s0 + 4))
            scan_group([s * S5_CTX_NC for s in seqs], S5_CTX_NC, [zero] * 16, seqs)

    @pl.when(step == 1)
    def _():
        init = []
        for d in range(2):
            for s in range(DEC_BATCH):
                init.append(h0_ref[d, 0, s:s + 1, :])
                init.append(h0_ref[d, 1, s:s + 1, :])
        scan_group([s * S5_LAT_NC for s in range(DEC_BATCH)], S5_LAT_NC, init, None)

    for j in range(S5_NPAIR):
        acc = None
        for d in range(2):
            for ri in range(2):
                hb = zh_ref[d, ri, :, sw * j:sw * (j + 1)].astype(BF16)
                t = _dot_nt(hb, e_ref[d, ri, j])
                acc = t if acc is None else acc + t
        for gl in range(2):
            g = 2 * j + gl
            ug = uy_ref[:, S5_CW * g:S5_CW * (g + 1)].astype(BF16)
            uy_ref[:, S5_CW * g:S5_CW * (g + 1)] = acc[:, S5_CW * gl:S5_CW * (gl + 1)] + _dot(ug, m_ref[g])

    lane = lax.broadcasted_iota(jnp.int32, (R, LANES), 1)
    in_piece = [jnp.logical_and(lane >= S5_P * k, lane < S5_P * (k + 1)) for k in range(LANES // S5_P)]
    for half, y_ref in enumerate((ylo_ref, yhi_ref)):
        for t in range(S5_L):
            merged = None
            for gl in range(gph):
                g = gph * half + gl
                slot = _s5_slot(g, t)
                col = S5_CW * g + LANES * (S5_P * slot // LANES)
                v = uy_ref[:, col:col + LANES]
                merged = v if merged is None else jnp.where(in_piece[slot % (LANES // S5_P)], v, merged)
            shift = (LANES - S5_P * t % LANES) % LANES
            y_ref[pl.ds(t, R, stride=S5_L), :] = pltpu.roll(merged, shift, 1) if shift else merged


def _s5(proj, m, f2, e2, al, h0, l):
    half = pl.BlockSpec((N_CTX, LANES), lambda s: (s, 0))
    return pl.pallas_call(
        _s5_kernel,
        grid=(2,),
        in_specs=[pl.BlockSpec((N_CTX, LANES), lambda s: (s, 0)),
                  pl.BlockSpec((N_CTX, LANES), lambda s: (s, 1)),
                  _layer_spec(m.shape[1:], l, single=True), _layer_spec(f2.shape[1:], l, single=True),
                  _layer_spec(e2.shape[1:], l, single=True), _layer_spec(al.shape[1:], l),
                  _layer_spec(h0.shape[1:], l)],
        out_specs=(half, half, _const_spec((2, 2, BATCH, S5_SW))),
        out_shape=(jax.ShapeDtypeStruct((N_TOK, LANES), F32),
                   jax.ShapeDtypeStruct((N_TOK, LANES), F32),
                   jax.ShapeDtypeStruct((2, 2, BATCH, S5_SW), F32)),
        scratch_shapes=[pltpu.VMEM((S5_STEP_ROWS, S5_G * S5_CW), F32),
                        pltpu.VMEM((2, 2, S5_STEP_ROWS, S5_SW), F32)],
        compiler_params=pltpu.CompilerParams(vmem_limit_bytes=VMEM_LIMIT),
        name="s5_mixer",
    )(proj, proj, m, f2, e2, al, h0)


HG_NLEV = int(math.log2(HG_L))
HG_W = HG_HEADS * HG_DK
HG_CPS = 4
HG_BLK = HG_CPS * HG_L
HG_NB_CTX = N_CTX // HG_BLK
HG_NC_CTX = SEQ // HG_BLK
HG_NC_LAT = DEC_SEQ // HG_BLK
assert HG_L == HG_DK


def _hgrn_consts():
    L = HG_L
    w = np.zeros((HG_NLEV + 2, L, L), np.float32)
    mask = np.zeros((HG_NLEV + 1, L, L), np.float32)
    for lev in range(HG_NLEV):
        blk = L >> lev
        half = blk // 2
        for t in range(L):
            p, o = divmod(t, blk)
            bd = p * blk + half - 1
            if o >= half:
                w[lev, t, bd + 1:t + 1] = 1.0
            else:
                w[lev, t, t + 1:bd + 1] = 1.0
        jj, ii = np.meshgrid(np.arange(L), np.arange(L), indexing='ij')
        mask[lev] = ((jj // blk == ii // blk) & (jj % blk >= half) & (ii % blk < half)).astype(np.float32)
    mask[HG_NLEV] = np.eye(L, dtype=np.float32)
    for t in range(L):
        w[HG_NLEV, t, :t + 1] = 1.0
        w[HG_NLEV + 1, t, t + 1:] = 1.0
    out = []
    for wd, md in ((w, mask), (w[:, ::-1, ::-1], mask[:, ::-1, ::-1])):
        wflat = wd.reshape((HG_NLEV + 2) * L, L)
        out.append((np.concatenate([wflat] * 3, axis=1), np.tile(md, (1, 1, HG_HEADS))))
    wcat = np.stack([out[0][0], out[1][0]])
    mask4 = np.stack([out[0][1], out[1][1]])
    hm = np.kron(np.eye(HG_HEADS, dtype=np.float32), np.ones((HG_DK, HG_DK), np.float32))
    return wcat, mask4, hm


def _hg_pos(i):
    is_ctx = i < HG_NB_CTX
    c = jnp.where(is_ctx, i % HG_NC_CTX, (i - HG_NB_CTX) % HG_NC_LAT)
    nc = jnp.where(is_ctx, HG_NC_CTX, HG_NC_LAT)
    return is_ctx, c, nc


def _hg_bwd_blk(i):
    _, c, nc = _hg_pos(i)
    return i + nc - 1 - 2 * c


def _hg_local(items, lb_ref, wcat_ref, mask_ref, hm):
    L = HG_L
    hmb = hm.astype(BF16)

    def bd4(x):
        return jnp.concatenate([x] * HG_HEADS, axis=0) * hmb

    kks, exs = [], []
    for d, q, z, v in items:
        lbp = lb_ref[d]
        sp = jnp.log1p(jnp.exp(-jnp.abs(z)))
        ls = jnp.minimum(z, 0.0) - sp
        kks.append(lbp[2:3] * jnp.exp(jnp.minimum(-z, 0.0) - sp))
        a = lbp[0:1]
        b = lbp[1:2] + ls
        logf = jnp.maximum(a, b) + jnp.log1p(jnp.exp(-jnp.abs(a - b)))
        p0 = logf.astype(BF16)
        r0 = logf - p0.astype(F32)
        p1 = r0.astype(BF16)
        p2 = (r0 - p1.astype(F32)).astype(BF16)
        exs.append(jnp.exp(_dot(wcat_ref[d], jnp.concatenate([p0, p1, p2], axis=0))))
    scs = [None] * len(items)
    for lev in range(HG_NLEV + 1):
        for n, (d, q, z, v) in enumerate(items):
            if lev < HG_NLEV:
                al = exs[n][L * lev:L * (lev + 1)]
                lhs = (q * al).astype(BF16)
                rhs = (kks[n] * al).astype(BF16)
            else:
                lhs = q.astype(BF16)
                rhs = kks[n].astype(BF16)
            t = _dot_nt(lhs, bd4(rhs)) * mask_ref[d, lev]
            scs[n] = t if scs[n] is None else scs[n] + t
    out = []
    for n, (d, q, z, v) in enumerate(items):
        vb = v.astype(BF16)
        o = _dot(scs[n].astype(BF16), bd4(vb))
        eq = exs[n][L * HG_NLEV:L * (HG_NLEV + 1)]
        qe = (q * eq).astype(BF16)
        ke = (kks[n] * exs[n][L * (HG_NLEV + 1):L * (HG_NLEV + 2)]).astype(BF16)
        g = eq[L - 1:L] if d == 0 else eq[0:1]
        out.append((o, qe, g, _dot_tn(vb, ke) * hm))
    return out


def _hgrn_kernel(qf, zf, vf, qb, zb, vb, lb_ref, w_ref, mask_ref, hm_ref, s0_ref,
                 of_ref, ob_ref, sfin_ref, st_ref):
    is_ctx, c, nc = _hg_pos(pl.program_id(0))

    @pl.when(c == 0)
    def _():
        st_ref[...] = jnp.where(is_ctx, 0.0, s0_ref[0])

    hm = hm_ref[...]
    rows = [slice(HG_L * k, HG_L * (k + 1)) for k in range(HG_CPS)]
    items = [(d, q_ref[r, :], z_ref[r, :], v_ref[r, :])
             for d, (q_ref, z_ref, v_ref) in enumerate(((qf, zf, vf), (qb, zb, vb))) for r in rows]
    loc = _hg_local(items, lb_ref, w_ref, mask_ref, hm)
    for d, o_ref in enumerate((of_ref, ob_ref)):
        st = st_ref[d]
        for k in (range(HG_CPS) if d == 0 else reversed(range(HG_CPS))):
            o, qe, g, upd = loc[HG_CPS * d + k]
            o_ref[rows[k], :] = o + _dot_nt(qe, st.astype(BF16))
            st = st * g + upd
        st_ref[d] = st

    @pl.when(jnp.logical_and(is_ctx, c == nc - 1))
    def _():
        for d in range(2):
            s_kv = st_ref[d].T
            for h in range(HG_HEADS):
                sfin_ref[0, d, h] = s_kv[HG_DK * h:HG_DK * (h + 1), HG_DK * h:HG_DK * (h + 1)]


def _hgrn(proj, lbp, consts, s0, l):
    wcat, mask4, hm = consts

    def fwd(col):
        return pl.BlockSpec((HG_BLK, HG_W), lambda i: (i, col))

    def bwd(col):
        return pl.BlockSpec((HG_BLK, HG_W), lambda i: (_hg_bwd_blk(i), col))

    return pl.pallas_call(
        _hgrn_kernel,
        grid=(N_TOK // HG_BLK,),
        in_specs=[fwd(COL_HQ), fwd(COL_HF_FWD), fwd(COL_HI), bwd(COL_HQ), bwd(COL_HF_BWD), bwd(COL_HI),
                  _layer_spec(lbp.shape[1:], l), _const_spec(wcat.shape), _const_spec(mask4.shape),
                  _const_spec(hm.shape),
                  pl.BlockSpec((None, 1, 2, HG_W, HG_W),
                               lambda i: (l, jnp.maximum(i - HG_NB_CTX, 0) // HG_NC_LAT, 0, 0, 0))],
        out_specs=(pl.BlockSpec((HG_BLK, HG_W), lambda i: (i, 0)),
                   pl.BlockSpec((HG_BLK, HG_W), lambda i: (_hg_bwd_blk(i), 0)),
                   pl.BlockSpec((1, 2, HG_HEADS, HG_DK, HG_DK),
                                lambda i: (jnp.minimum(i // HG_NC_CTX, BATCH - 1), 0, 0, 0, 0))),
        out_shape=(jax.ShapeDtypeStruct((N_TOK, HG_W), F32),
                   jax.ShapeDtypeStruct((N_TOK, HG_W), F32),
                   jax.ShapeDtypeStruct((BATCH, 2, HG_HEADS, HG_DK, HG_DK), F32)),
        scratch_shapes=[pltpu.VMEM((2, HG_W, HG_W), F32)],
        compiler_params=pltpu.CompilerParams(vmem_limit_bytes=VMEM_LIMIT),
        name="hgrn_mixer",
    )(proj, proj, proj, proj, proj, proj, lbp, wcat, mask4, hm, s0)


FN_CTX_SPG = 4
FN_LAT_TR = 512


def _dft_consts(t_len):
    n = GROUP_W // 4
    k = np.arange(n)
    ang = 2.0 * np.pi * ((k[:, None] * k[None, :]) % n) / n
    eye = np.eye(4)
    cs = np.concatenate([np.kron(eye, np.cos(ang)), np.kron(eye, np.sin(ang))], axis=1) / math.sqrt(n)
    t = np.arange(t_len)
    angt = 2.0 * np.pi * ((t[:, None] * t[None, :]) % t_len) / t_len
    dft = np.concatenate([np.cos(angt), -np.sin(angt)], axis=1) / math.sqrt(t_len)
    return cs.astype(np.float32), dft.astype(np.float32)


def _fnet_kernel(x_ref, cs_ref, dft_ref, w_ref, o_ref, r_ref, *, t_len, spg, tr):
    @pl.when(pl.program_id(1) == 0)
    def _():
        t = _dot(x_ref[...].astype(BF16), cs_ref[...])
        for s in range(spg):
            rows = slice(t_len * s, t_len * (s + 1))
            r_ref[s, 0:t_len, :] = t[rows, :GROUP_W].astype(BF16)
            r_ref[s, t_len:2 * t_len, :] = t[rows, GROUP_W:].astype(BF16)

    w = w_ref[...].astype(BF16)
    for s in range(spg):
        y = _dot(dft_ref[...], r_ref[s])
        o_ref[tr * s:tr * (s + 1), :] = _dot(y.astype(BF16), w)


def _fnet(proj, cs, dft, fn_w, l, *, n_seq, seq_len, tok0, spg, tr):
    nj = seq_len // tr
    assert spg == 1 or nj == 1
    gb0 = tok0 // (spg * seq_len)
    return pl.pallas_call(
        functools.partial(_fnet_kernel, t_len=seq_len, spg=spg, tr=tr),
        grid=(n_seq // spg, nj),
        in_specs=[
            pl.BlockSpec((spg * seq_len, GROUP_W), lambda g, j: (gb0 + g, COL_XC)),
            _const_spec((GROUP_W, 2 * GROUP_W)),
            pl.BlockSpec((tr, 2 * seq_len), lambda g, j: (j, 0)),
            _layer_spec((GROUP_W, GROUP_W), l),
        ],
        out_specs=pl.BlockSpec((spg * tr, GROUP_W), lambda g, j: (g * nj + j, 0)),
        out_shape=jax.ShapeDtypeStruct((n_seq * seq_len, GROUP_W), F32),
        scratch_shapes=[pltpu.VMEM((spg, 2 * seq_len, GROUP_W), BF16)],
        compiler_params=pltpu.CompilerParams(vmem_limit_bytes=VMEM_LIMIT),
        name="fnet_mixer",
    )(proj, cs, dft, fn_w)


def _outproj_kernel(*refs):
    (xa_ref, hgate_ref, gu_ref, gv_ref, ylo_ref, yhi_ref, of_ref, ob_ref, ycc_ref, ycl_ref, mod_ref, d_ref,
     wglu_ref, gn_ref, bdm_ref, gmn_ref, gmw_ref, gmb_ref, gmh_ref, wout_ref, n2_ref,
     x1_ref, h2_ref, wob_ref) = refs[-24:]
    x_refs = refs[:-24]

    @pl.when(pl.program_id(0) == 0)
    def _():
        wob_ref[...] = wout_ref[...].astype(BF16)

    gn = gn_ref[...]
    ys = jnp.concatenate([ylo_ref[...], yhi_ref[...]], axis=1)
    y5 = _gelu(ys + d_ref[...] * xa_ref[...])
    glu = jax.nn.sigmoid(_dot(y5.astype(BF16), wglu_ref[...].astype(BF16)))
    out_a = _rms(y5 * glu, gn[:, 0:GROUP_W])
    o = of_ref[...] + ob_ref[...]
    o2 = o * o
    o2h = o2.astype(BF16)
    o2l = (o2 - o2h.astype(F32)).astype(BF16)
    ms = _dot(o2h, bdm_ref[...]) + _dot(o2l, bdm_ref[...])
    hg = hgate_ref[...]
    out_b = o * lax.rsqrt(ms + EPS) * gn[:, GROUP_W:2 * GROUP_W] * (hg * jax.nn.sigmoid(hg))
    out_c = _rms(_x_value((ycc_ref, ycl_ref), TBP), gn[:, 2 * GROUP_W:3 * GROUP_W])
    gm_hm = gmh_ref[...]
    out_d = []
    for c in range(TBP // GM_CHUNK):
        rows = slice(GM_CHUNK * c, GM_CHUNK * (c + 1))
        gv = _rms(_gelu(gv_ref[rows, :]), gmn_ref[...]).astype(BF16)
        g4 = jnp.concatenate([gv] * GM_HEADS, axis=0) * gm_hm
        sp = _dot(gmw_ref[...], g4) + gmb_ref[...]
        out_d.append(_rms(_gelu(gu_ref[rows, :]) * sp, gn[:, 3 * GROUP_W:]))
    out_d = jnp.concatenate(out_d, axis=0)
    m = None
    for k, part in enumerate((out_a, out_b, out_c, out_d)):
        t = _dot(part.astype(BF16), wob_ref[GROUP_W * k:GROUP_W * (k + 1), :])
        m = t if m is None else m + t
    mod = mod_ref[...]
    g1 = mod[:, 2 * D_MODEL:3 * D_MODEL]
    sh2 = mod[:, 3 * D_MODEL:4 * D_MODEL]
    sc2 = mod[:, 4 * D_MODEL:5 * D_MODEL]
    x1 = _x_value(x_refs, TBP) + g1 * m
    x1_ref[...] = x1
    h2_ref[...] = (_rms(x1, n2_ref[...]) * (1.0 + sc2) + sh2).astype(BF16)


def _outproj(xs, proj, ylo, yhi, of, ob, yc, mods, s5_d, wglu, gn, bdm, gm_norm_g, gm_w, gm_b, gm_hm, wout,
             norm2_g, l):
    def tok(width, col=0):
        return pl.BlockSpec((TBP, width), lambda i: (i, col))

    return pl.pallas_call(
        _outproj_kernel,
        grid=(N_TOK // TBP,),
        in_specs=_x_specs(len(xs) == 2, tb=TBP) + [
            tok(GROUP_W, COL_XA), tok(GROUP_W, COL_HGATE), tok(GROUP_W, COL_GU), tok(GROUP_W, COL_GV),
            tok(LANES), tok(LANES), tok(GROUP_W), tok(GROUP_W), *_x_specs(True, GROUP_W, TBP),
            _mod_spec(l, TBP),
            _layer_spec((1, GROUP_W), l), _layer_spec((GROUP_W, GROUP_W), l), _layer_spec((1, D_MODEL), l),
            _const_spec((GROUP_W, GROUP_W)),
            _layer_spec((1, GROUP_W), l), _layer_spec((GM_CHUNK, GM_HEADS * GM_CHUNK), l),
            _layer_spec((GM_CHUNK, GROUP_W), l), _const_spec((GM_HEADS * GM_CHUNK, GROUP_W)),
            _layer_spec((D_MODEL, D_MODEL), l, single=True), _layer_spec((1, D_MODEL), l),
        ],
        out_specs=(tok(D_MODEL), tok(D_MODEL)),
        out_shape=(jax.ShapeDtypeStruct((N_TOK, D_MODEL), F32),
                   jax.ShapeDtypeStruct((N_TOK, D_MODEL), BF16)),
        scratch_shapes=[pltpu.VMEM((D_MODEL, D_MODEL), BF16)],
        compiler_params=pltpu.CompilerParams(vmem_limit_bytes=VMEM_LIMIT),
        name="out_proj",
    )(*xs, proj, proj, proj, proj, ylo, yhi, of, ob, *yc, mods, s5_d, wglu, gn, bdm, gm_norm_g, gm_w, gm_b,
      gm_hm, wout, norm2_g)


FF_SEG = GRID_W
FF_NSEG = TB // FF_SEG
FF_SEQ_STRIPS = SEQ // FF_SEG
FF_PERM_ROWS = FF_SEQ_STRIPS * FF_SEG
SUBLANES = 8
FF_NGRP = FF_SEG // SUBLANES
GELU_C0 = 0.7978845608028654
GELU_C1 = GELU_C0 * 0.044715


def _ffn_perm_consts():
    p = np.zeros((FF_PERM_ROWS, FF_PERM_ROWS), np.float32)
    for k in range(FF_SEQ_STRIPS):
        for t in range(FF_SEG):
            p[FF_SEG * k + SUBLANES * (t % FF_NGRP) + t // FF_NGRP, FF_SEG * k + t] = 1.0
    return p


def _ffn_kernel(x1_ref, h2_ref, mod_ref, perm_ref, wup_ref, cw_ref, cb_ref, wdn_ref, fg_ref, *rest, final):
    *o_refs, h2p_ref, z_ref, hid_ref = rest
    i = pl.program_id(0)
    joined = (i < NB_CTX).astype(F32)
    n_tiles = D_FF // FF_TILE
    sub = lax.broadcasted_iota(jnp.int32, (SUBLANES, 2 * FF_TILE), 0)
    is_first = sub == 0
    is_last = sub == SUBLANES - 1
    zero_grp = jnp.zeros((SUBLANES, 2 * FF_TILE), F32)
    for r in range(0, TB, FF_PERM_ROWS):
        h2p_ref[r:r + FF_PERM_ROWS, :] = _dot(perm_ref[...], h2_ref[r:r + FF_PERM_ROWS, :]).astype(BF16)

    def up(j):
        lo = FF_TILE * j
        for c, col in enumerate((lo, D_FF + lo)):
            z_ref[j % 2, :, FF_TILE * c:FF_TILE * (c + 1)] = _dot(h2p_ref[...], wup_ref[:, col:col + FF_TILE])

    def gate(j):
        slot = j % 2
        lo = FF_TILE * j
        w = jnp.concatenate([cw_ref[:, lo:lo + FF_TILE], cw_ref[:, D_FF + lo:D_FF + lo + FF_TILE]], axis=1)
        b = jnp.concatenate([cb_ref[:, lo:lo + FF_TILE], cb_ref[:, D_FF + lo:D_FF + lo + FF_TILE]], axis=1)

        def grp(k, v):
            r = FF_SEG * k + SUBLANES * v
            return z_ref[slot, r:r + SUBLANES, :]

        down = [pltpu.roll(grp(k, FF_NGRP - 1), 1, 0) for k in range(FF_NSEG)]
        up_ = [pltpu.roll(grp(k, 0), SUBLANES - 1, 0) for k in range(FF_NSEG)]
        strips = []
        for k in range(FF_NSEG):
            g = [grp(k, v) for v in range(FF_NGRP)]
            before = down[k - 1] * joined if k % FF_SEQ_STRIPS > 0 else zero_grp
            after = up_[k + 1] * joined if k % FF_SEQ_STRIPS < FF_SEQ_STRIPS - 1 else zero_grp
            zm1 = jnp.concatenate([jnp.where(is_first, before, down[k])] + g[:-1], axis=0)
            zp1 = jnp.concatenate(g[1:] + [jnp.where(is_last, after, up_[k])], axis=0)
            zc = b + zm1 * w[0:1] + jnp.concatenate(g, axis=0) * w[1:2] + zp1 * w[2:3]
            a = zc[:, :FF_TILE]
            t = jnp.tanh(a * (GELU_C0 + GELU_C1 * (a * a)))
            strips.append((a * (0.5 + 0.5 * t) * zc[:, FF_TILE:]).astype(BF16))
        hid_ref[:, lo:lo + FF_TILE] = jnp.concatenate(strips, axis=0)

    up(0)
    for j in range(n_tiles):
        if j + 1 < n_tiles:
            up(j + 1)
        gate(j)
    acc = _dot(hid_ref[...], wdn_ref[...])
    acc = jnp.swapaxes(acc.reshape(FF_NSEG, FF_NGRP, SUBLANES, D_MODEL), 1, 2).reshape(TB, D_MODEL)
    g2 = mod_ref[...][:, 5 * D_MODEL:6 * D_MODEL]
    x2 = x1_ref[...] + g2 * acc
    if final:
        y = _rms(x2, fg_ref[...])
        yp_ref, ys_ref = o_refs

        @pl.when(i < NB_CTX)
        def _():
            yp_ref[...] = y

        @pl.when(i >= NB_CTX)
        def _():
            ys_ref[...] = y
    else:
        o_refs[0][...] = x2


def _ffn(x1, h2, mods, perm, wup_bf, conv_w, conv_b, wdn_bf, final_g, l, *, final):
    if final:
        out_specs = tuple(_x_specs(True))
        out_shape = (jax.ShapeDtypeStruct((N_CTX, D_MODEL), F32), jax.ShapeDtypeStruct((N_LAT, D_MODEL), F32))
    else:
        out_specs = pl.BlockSpec((TB, D_MODEL), lambda i: (i, 0))
        out_shape = jax.ShapeDtypeStruct((N_TOK, D_MODEL), F32)
    return pl.pallas_call(
        functools.partial(_ffn_kernel, final=final),
        grid=(N_TOK // TB,),
        in_specs=[
            pl.BlockSpec((TB, D_MODEL), lambda i: (i, 0)),
            pl.BlockSpec((TB, D_MODEL), lambda i: (i, 0)),
            _mod_spec(l), _const_spec((FF_PERM_ROWS, FF_PERM_ROWS)),
            _layer_spec((D_MODEL, 2 * D_FF), l, single=True), _layer_spec((3, 2 * D_FF), l),
            _layer_spec((1, 2 * D_FF), l), _layer_spec((D_FF, D_MODEL), l, single=True),
            _const_spec((1, D_MODEL)),
        ],
        out_specs=out_specs,
        out_shape=out_shape,
        scratch_shapes=[pltpu.VMEM((TB, D_MODEL), BF16), pltpu.VMEM((2, TB, 2 * FF_TILE), F32),
                        pltpu.VMEM((TB, D_FF), BF16)],
        compiler_params=pltpu.CompilerParams(vmem_limit_bytes=VMEM_LIMIT),
        name="conv_ffn",
    )(x1, h2, mods, perm, wup_bf, conv_w, conv_b, wdn_bf, final_g.reshape(1, D_MODEL))


def kernel(x_prompt, x_sample, state_s5_re, state_s5_im, state_hgrn, c, c_ctx, w_ada, b_ada, norm1_g,
           norm2_g, w_in, s5_lam_re, s5_lam_im, s5_log_dt, s5_b_re, s5_b_im, s5_c_re, s5_c_im, s5_d,
           s5_w_glu, hg_lb_logits, fn_w, gm_norm_g, gm_ws, gm_bs, grp_norm_g, w_out, ffn_w_up,
           ffn_conv_w, ffn_conv_b, ffn_w_down, final_norm_g):
    xs = (x_prompt.reshape(N_CTX, D_MODEL), x_sample.reshape(N_LAT, D_MODEL))
    cvecs = jnp.concatenate([c_ctx[None], c, jnp.zeros((8 - 1 - DEC_BATCH, D_MODEL), F32)], axis=0)
    mods = _ada(cvecs, w_ada, b_ada).reshape(DEPTH, 8, 1, N_MOD * D_MODEL)

    lb_p = jax.nn.softmax(hg_lb_logits.astype(F32), axis=0)
    lbs = jnp.maximum(jnp.cumsum(lb_p, axis=0) - lb_p[0], 0.0)
    lbps = jnp.stack([jnp.log(lbs), jnp.log1p(-lbs), 1.0 - lbs], axis=2)

    hg_consts_np = _hgrn_consts()
    hg_consts = (jnp.asarray(hg_consts_np[0], BF16), jnp.asarray(hg_consts_np[1], F32),
                 jnp.asarray(hg_consts_np[2], F32))
    cs_np, dft_ctx_np = _dft_consts(SEQ)
    _, dft_lat_np = _dft_consts(DEC_SEQ)
    cs = jnp.asarray(cs_np, F32).astype(BF16)
    dft_ctx = jnp.asarray(dft_ctx_np, F32).astype(BF16)
    dft_lat = jnp.asarray(dft_lat_np, F32).astype(BF16)
    gm_hm = jnp.asarray(np.kron(np.eye(GM_HEADS), np.ones((GM_CHUNK, GROUP_W // GM_HEADS))), BF16)
    bdm = jnp.asarray(np.kron(np.eye(HG_HEADS), np.ones((HG_DK, HG_DK))) / HG_DK, BF16)

    tables = _s5_tables(s5_lam_re, s5_lam_im, s5_log_dt, s5_b_re, s5_b_im, s5_c_re, s5_c_im)
    s5_m, s5_f, s5_e = _s5_prep(tables, jnp.asarray(_s5_shift_consts(), BF16))
    s5_al = tables[-1]
    s5_h0 = jnp.stack([state_s5_re, state_s5_im]).astype(F32).transpose(2, 3, 0, 1, 4, 5)
    s5_h0 = s5_h0.reshape(DEPTH, 2, 2, DEC_BATCH, S5_SW)
    hg_s0 = jnp.einsum('bldhkv,hg->lbdhvgk', state_hgrn.astype(F32), jnp.eye(HG_HEADS, dtype=F32))
    hg_s0 = hg_s0.reshape(DEPTH, DEC_BATCH, 2, HG_W, HG_W)

    def rows(a):
        return a.reshape(DEPTH, 1, a.shape[-1])

    norm1_r, norm2_r, gn_r, s5_d_r, gmn_r, cb_r = (rows(a) for a in (
        norm1_g, norm2_g, grp_norm_g, s5_d, gm_norm_g, ffn_conv_b))
    gm_w = gm_ws.transpose(0, 2, 1, 3).reshape(DEPTH, GM_CHUNK, GM_HEADS * GM_CHUNK).astype(BF16)
    gm_b = jnp.repeat(gm_bs.transpose(0, 2, 1), GROUP_W // GM_HEADS, axis=2)
    ff_perm = jnp.asarray(_ffn_perm_consts(), BF16)
    wup_bf = ffn_w_up.astype(BF16)
    wdn_bf = ffn_w_down.astype(BF16)

    new_re, new_im, new_hg = [], [], []
    for l in range(DEPTH):
        proj = _inproj(xs, mods, norm1_r, w_in, l)

        ylo, yhi, hfin = _s5(proj, s5_m, s5_f, s5_e, s5_al, s5_h0, l)
        hfin = hfin.reshape(2, 2, BATCH, S5_G, S5_N).transpose(1, 2, 0, 3, 4)
        new_re.append(hfin[0])
        new_im.append(hfin[1])

        of, ob, sfin = _hgrn(proj, lbps, hg_consts, hg_s0, l)
        new_hg.append(sfin)

        yc = (_fnet(proj, cs, dft_ctx, fn_w, l, n_seq=BATCH, seq_len=SEQ, tok0=0, spg=FN_CTX_SPG, tr=SEQ),
              _fnet(proj, cs, dft_lat, fn_w, l, n_seq=DEC_BATCH, seq_len=DEC_SEQ, tok0=N_CTX, spg=1,
                    tr=FN_LAT_TR))

        x1, h2 = _outproj(xs, proj, ylo, yhi, of, ob, yc, mods, s5_d_r, s5_w_glu, gn_r, bdm, gmn_r, gm_w, gm_b,
                          gm_hm, w_out, norm2_r, l)
        res = _ffn(x1, h2, mods, ff_perm, wup_bf, ffn_conv_w, cb_r, wdn_bf, final_norm_g, l,
                   final=(l == DEPTH - 1))
        xs = (res,)

    y_prompt = res[0].reshape(BATCH, SEQ, D_MODEL)
    y_sample = res[1].reshape(DEC_BATCH, DEC_SEQ, D_MODEL)
    return (y_prompt, y_sample, jnp.stack(new_re, axis=1), jnp.stack(new_im, axis=1),
            jnp.stack(new_hg, axis=1))
```

```python
import functools
import math

import numpy as np
import jax
import jax.numpy as jnp
from jax import lax
from jax.experimental import pallas as pl
from jax.experimental.pallas import tpu as pltpu

D_MODEL = 1024
BATCH = 16
SEQ = 256
DEPTH = 2
DEC_BATCH = 2
DEC_SEQ = 2048
GRID_W = 64
GROUP_W = 256
S5_P = 16
S5_G = 16
S5_N = 64
HG_HEADS = 4
HG_DK = 64
GM_HEADS = 4
GM_CHUNK = 128
D_FF = 2816
N_MOD = 6
D_IN = 9 * GROUP_W
EPS = 1e-6
LAM_RE_MAX = -1e-4

N_CTX = BATCH * SEQ
N_LAT = DEC_BATCH * DEC_SEQ
N_TOK = N_CTX + N_LAT
TB = 512
TBP = 512
NB_CTX = N_CTX // TB
S5_L = 16
HG_L = 64
FF_TILE = 256
LANES = 128
VMEM_LIMIT = 56 * 1024 * 1024

F32 = jnp.float32
BF16 = jnp.bfloat16

COL_XA, COL_HQ, COL_HF_FWD, COL_HF_BWD, COL_HI, COL_HGATE, COL_XC, COL_GU, COL_GV = range(9)


def _mod_row(i, tb):
    return jnp.where(i < N_CTX // tb, 0, 1 + (i - N_CTX // tb) // (DEC_SEQ // tb))


def _gelu(x):
    return 0.5 * x * (1.0 + jnp.tanh(0.7978845608028654 * (x + 0.044715 * (x * x * x))))


def _rms(x, g):
    return x * lax.rsqrt(jnp.mean(x * x, axis=-1, keepdims=True) + EPS) * g


def _dot(a, b):
    return jnp.dot(a, b, preferred_element_type=F32)


def _dot_nt(a, b, precision=None):
    return lax.dot_general(a, b, (((1,), (1,)), ((), ())), precision=precision, preferred_element_type=F32)


def _dot_tn(a, b):
    return lax.dot_general(a, b, (((0,), (0,)), ((), ())), preferred_element_type=F32)


def _const_spec(shape, single=False):
    kw = {"pipeline_mode": pl.Buffered(1)} if single else {}
    return pl.BlockSpec(shape, lambda *_: (0,) * len(shape), **kw)


def _layer_spec(shape, l, single=False):
    kw = {"pipeline_mode": pl.Buffered(1)} if single else {}
    return pl.BlockSpec((None,) + tuple(shape), lambda *_: (l,) + (0,) * len(shape), **kw)


def _mod_spec(l, tb=TB):
    return pl.BlockSpec((None, None, 1, N_MOD * D_MODEL), lambda i: (l, _mod_row(i, tb), 0, 0))


def _x_specs(split, width=D_MODEL, tb=TB):
    nb_ctx = N_CTX // tb
    if split:
        return [pl.BlockSpec((tb, width), lambda i: (jnp.minimum(i, nb_ctx - 1), 0)),
                pl.BlockSpec((tb, width), lambda i: (jnp.maximum(i - nb_ctx, 0), 0))]
    return [pl.BlockSpec((tb, width), lambda i: (i, 0))]


def _x_value(x_refs, tb=TB):
    if len(x_refs) == 1:
        return x_refs[0][...]
    return jnp.where(pl.program_id(0) < N_CTX // tb, x_refs[0][...], x_refs[1][...])


ADA_TN = 1536


def _ada_kernel(c_ref, w_ref, b_ref, o_ref):
    cv = c_ref[...]
    s = cv * jax.nn.sigmoid(cv)
    o_ref[0] = _dot(s.astype(BF16), w_ref[0].astype(BF16)) + b_ref[0]


def _ada(cvecs, w_ada, b_ada):
    n = N_MOD * D_MODEL
    return pl.pallas_call(
        _ada_kernel,
        grid=(DEPTH, n // ADA_TN),
        in_specs=[
            pl.BlockSpec((8, D_MODEL), lambda l, j: (0, 0)),
            pl.BlockSpec((1, D_MODEL, ADA_TN), lambda l, j: (l, 0, j)),
            pl.BlockSpec((1, 1, ADA_TN), lambda l, j: (l, 0, j)),
        ],
        out_specs=pl.BlockSpec((1, 8, ADA_TN), lambda l, j: (l, 0, j)),
        out_shape=jax.ShapeDtypeStruct((DEPTH, 8, n), F32),
        compiler_params=pltpu.CompilerParams(vmem_limit_bytes=VMEM_LIMIT),
        name="ada_mod",
    )(cvecs, w_ada, b_ada.reshape(DEPTH, 1, n))


def _inproj_kernel(*refs):
    *x_refs, mod_ref, g_ref, w_ref, o_ref, wb_ref = refs

    @pl.when(pl.program_id(0) == 0)
    def _():
        wb_ref[...] = w_ref[...].astype(BF16)

    mod = mod_ref[...]
    sh = mod[:, 0:D_MODEL]
    sc = mod[:, D_MODEL:2 * D_MODEL]
    h = _rms(_x_value(x_refs, TBP), g_ref[...]) * (1.0 + sc) + sh
    o_ref[...] = _dot(h.astype(BF16), wb_ref[...])


def _inproj(xs, mods, norm_g, w_in, l):
    return pl.pallas_call(
        _inproj_kernel,
        grid=(N_TOK // TBP,),
        in_specs=_x_specs(len(xs) == 2, tb=TBP) + [
            _mod_spec(l, TBP),
            _layer_spec((1, D_MODEL), l),
            _layer_spec((D_MODEL, D_IN), l, single=True),
        ],
        out_specs=pl.BlockSpec((TBP, D_IN), lambda i: (i, 0)),
        out_shape=jax.ShapeDtypeStruct((N_TOK, D_IN), F32),
        scratch_shapes=[pltpu.VMEM((D_MODEL, D_IN), BF16)],
        compiler_params=pltpu.CompilerParams(vmem_limit_bytes=VMEM_LIMIT),
        name="in_proj",
    )(*xs, mods, norm_g, w_in)


S5_NPAIR = S5_G // 2
S5_CW = S5_L * S5_P
S5_SW = S5_G * S5_N
S5_STEP_ROWS = N_CTX // S5_L
S5_CTX_NC = SEQ // S5_L
S5_LAT_NC = DEC_SEQ // S5_L


def _s5_slot(g, t):
    return (t + g) % S5_L


def _s5_tables(lam_re, lam_im, log_dt, b_re, b_im, c_re, c_im):
    lr = jnp.minimum(lam_re.astype(F32), LAM_RE_MAX)
    li = lam_im.astype(F32)
    dt = jnp.exp(log_dt.astype(F32))[..., None]
    mag = jnp.exp(lr * dt)
    ang = li * dt
    ab_re = mag * jnp.cos(ang)
    ab_im = mag * jnp.sin(ang)
    den = lr * lr + li * li
    xr = ab_re - 1.0
    z_re = (xr * lr + ab_im * li) / den
    z_im = (ab_im * lr - xr * li) / den
    bb_re = z_re[..., None] * b_re - z_im[..., None] * b_im
    bb_im = z_re[..., None] * b_im + z_im[..., None] * b_re
    tau = jnp.arange(S5_L + 1, dtype=F32)[:, None, None, None, None]
    pm = jnp.exp(lr * dt * tau)
    pa = li * dt * tau
    pw_re = pm * jnp.cos(pa)
    pw_im = pm * jnp.sin(pa)
    eye2 = jnp.eye(2, dtype=F32)

    def pw_pairs(a):
        a = a.transpose(1, 2, 3, 0, 4).reshape(DEPTH, 2, S5_NPAIR, 2, S5_L + 1, S5_N)
        return a.transpose(0, 1, 2, 4, 3, 5).reshape(DEPTH, 2, S5_NPAIR, S5_L + 1, 2 * S5_N)

    def mat_pairs(a):
        a = a.reshape(DEPTH, 2, S5_NPAIR, 2, S5_P, S5_N)
        return jnp.einsum('ldjaqn,ab->ldjaqbn', a, eye2).reshape(DEPTH, 2, S5_NPAIR, 2, S5_P, 2 * S5_N)

    al = jnp.stack([pw_re[S5_L], pw_im[S5_L]], axis=2).reshape(DEPTH, 2, 2, 1, S5_SW)
    return (pw_pairs(pw_re), pw_pairs(pw_im),
            mat_pairs(bb_re.transpose(0, 1, 2, 4, 3)), mat_pairs(bb_im.transpose(0, 1, 2, 4, 3)),
            mat_pairs(c_re.astype(F32)), mat_pairs(c_im.astype(F32)), al)


def _s5_shift_consts():
    n = S5_CW
    r = np.arange(n)[:, None]
    c = np.arange(n)[None, :]
    fwd = [(c == r + S5_P * s) for s in range(S5_L)]
    bwd = [(c == r - S5_P * (S5_L - 1 - s)) for s in range(S5_L)]
    return np.stack([np.concatenate(fwd, axis=1), np.concatenate(bwd, axis=1)]).astype(np.float32)


def _s5_prep_kernel(pwr_ref, pwi_ref, br_ref, bi_ref, cr_ref, ci_ref, scat_ref, m_ref, f_ref, e_ref, k_ref):
    L = S5_L
    hp = lax.Precision.HIGHEST
    sw = 2 * S5_N

    for j in range(S5_NPAIR):
        for d in range(2):
            pwr = pwr_ref[0, d, j]
            pwi = pwi_ref[0, d, j]
            for gl in range(2):
                g = 2 * j + gl
                br, bi = br_ref[0, d, j, gl], bi_ref[0, d, j, gl]
                cr, ci = cr_ref[0, d, j, gl], ci_ref[0, d, j, gl]
                ca_r, ca_i, f_r, f_i, e_r, e_i = [], [], [], [], [], []
                for t in range(L):
                    kk = t if d == 0 else L - 1 - t
                    pr, pi = pwr[kk:kk + 1], pwi[kk:kk + 1]
                    ca_r.append(cr * pr - ci * pi)
                    ca_i.append(cr * pi + ci * pr)
                    kf = L - 1 - t if d == 0 else t
                    pr, pi = pwr[kf:kf + 1], pwi[kf:kf + 1]
                    f_r.append(br * pr - bi * pi)
                    f_i.append(br * pi + bi * pr)
                    ke = t + 1 if d == 0 else L - t
                    pr, pi = pwr[ke:ke + 1], pwi[ke:ke + 1]
                    e_r.append(cr * pr - ci * pi)
                    e_i.append(-(cr * pi + ci * pr))
                cat = lambda xs: jnp.concatenate(xs, axis=0)
                k = _dot_nt(br, cat(ca_r), hp) - _dot_nt(bi, cat(ca_i), hp)
                k_ref[d, S5_P * g:S5_P * (g + 1), :] = k
                slots = lambda xs: cat([xs[(p - g) % L] for p in range(L)])
                rows = slice(S5_CW * gl, S5_CW * (gl + 1))
                f_ref[0, d, j, rows, 0:sw] = slots(f_r).astype(BF16)
                f_ref[0, d, j, rows, sw:2 * sw] = slots(f_i).astype(BF16)
                e_ref[0, d, 0, j, rows, :] = slots(e_r).astype(BF16)
                e_ref[0, d, 1, j, rows, :] = slots(e_i).astype(BF16)

    kf = k_ref[0].astype(BF16)
    kb = k_ref[1].astype(BF16)
    for s in range(L):
        cols = slice(S5_CW * s, S5_CW * (s + 1))
        res = _dot(kf, scat_ref[0, :, cols]) + _dot(kb, scat_ref[1, :, cols])
        for g in range(S5_G):
            blk = res[S5_P * g:S5_P * (g + 1), :]
            if g:
                blk = pltpu.roll(blk, S5_P * g, 1)
            r0 = S5_P * _s5_slot(g, s)
            m_ref[0, g, r0:r0 + S5_P, :] = blk.astype(BF16)


def _s5_prep(tables, scat):
    pwr, pwi, br, bi, cr, ci, _ = tables

    def lspec(shape):
        return pl.BlockSpec((1,) + shape, lambda l: (l,) + (0,) * len(shape))

    pw_shape = (2, S5_NPAIR, S5_L + 1, 2 * S5_N)
    mat_shape = (2, S5_NPAIR, 2, S5_P, 2 * S5_N)
    return pl.pallas_call(
        _s5_prep_kernel,
        grid=(DEPTH,),
        in_specs=[lspec(pw_shape), lspec(pw_shape), lspec(mat_shape), lspec(mat_shape), lspec(mat_shape),
                  lspec(mat_shape), _const_spec(scat.shape)],
        out_specs=(lspec((S5_G, S5_CW, S5_CW)), lspec((2, S5_NPAIR, 2 * S5_CW, 4 * S5_N)),
                   lspec((2, 2, S5_NPAIR, 2 * S5_CW, 2 * S5_N))),
        out_shape=(jax.ShapeDtypeStruct((DEPTH, S5_G, S5_CW, S5_CW), BF16),
                   jax.ShapeDtypeStruct((DEPTH, 2, S5_NPAIR, 2 * S5_CW, 4 * S5_N), BF16),
                   jax.ShapeDtypeStruct((DEPTH, 2, 2, S5_NPAIR, 2 * S5_CW, 2 * S5_N), BF16)),
        scratch_shapes=[pltpu.VMEM((2, S5_G * S5_P, S5_CW), F32)],
        compiler_params=pltpu.CompilerParams(vmem_limit_bytes=VMEM_LIMIT),
        name="s5_prep",
    )(pwr, pwi, br, bi, cr, ci, scat)


def _s5_kernel(xlo_ref, xhi_ref, m_ref, f_ref, e_ref, al_ref, h0_ref, ylo_ref, yhi_ref, hfin_ref,
               uy_ref, zh_ref):
    R = S5_STEP_ROWS
    pw = 2 * S5_CW
    sw = 2 * S5_N
    gph = S5_G // 2
    step = pl.program_id(0)

    for half, x_ref in enumerate((xlo_ref, xhi_ref)):
        for t in range(S5_L):
            xt = x_ref[pl.ds(t, R, stride=S5_L), :]
            shift = (S5_P * t) % LANES
            xr = pltpu.roll(xt, shift, 1) if shift else xt
            for gl in range(gph):
                g = gph * half + gl
                src = (S5_P * gl + shift) % LANES
                lo = S5_CW * g + S5_P * _s5_slot(g, t)
                assert lo % LANES == src
                uy_ref[:, lo:lo + S5_P] = xr[:, src:src + S5_P]

    for j in range(S5_NPAIR):
        ub = uy_ref[:, pw * j:pw * (j + 1)].astype(BF16)
        for d in range(2):
            o = _dot(ub, f_ref[d, j])
            zh_ref[d, 0, :, sw * j:sw * (j + 1)] = o[:, :sw]
            zh_ref[d, 1, :, sw * j:sw * (j + 1)] = o[:, sw:]

    def scan_group(seq_rows, nc, init, fin_rows):
        chains = [(d, r0) for d in range(2) for r0 in seq_rows]

        def body(c, carry):
            out = []
            for k, (d, r0) in enumerate(chains):
                hr, hi = carry[2 * k], carry[2 * k + 1]
                r = r0 + (c if d == 0 else nc - 1 - c)
                zr = zh_ref[d, 0, pl.ds(r, 1), :]
                zi = zh_ref[d, 1, pl.ds(r, 1), :]
                zh_ref[d, 0, pl.ds(r, 1), :] = hr
                zh_ref[d, 1, pl.ds(r, 1), :] = hi
                ar = al_ref[d, 0]
                ai = al_ref[d, 1]
                out.append(ar * hr - ai * hi + zr)
                out.append(ar * hi + ai * hr + zi)
            return tuple(out)

        fin = lax.fori_loop(0, nc, body, tuple(init))
        if fin_rows is not None:
            for k, (d, _) in enumerate(chains):
                s = fin_rows[k % len(seq_rows)]
                hfin_ref[d, 0, s:s + 1, :] = fin[2 * k]
                hfin_ref[d, 1, s:s + 1, :] = fin[2 * k + 1]

    @pl.when(step == 0)
    def _():
        zero = jnp.zeros((1, S5_SW), F32)
        for s0 in range(0, BATCH, 4):
            seqs = list(range(s0, s0 + 4))
            scan_group([s * S5_CTX_NC for s in seqs], S5_CTX_NC, [zero] * 16, seqs)

    @pl.when(step == 1)
    def _():
        init = []
        for d in range(2):
            for s in range(DEC_BATCH):
                init.append(h0_ref[d, 0, s:s + 1, :])
                init.append(h0_ref[d, 1, s:s + 1, :])
        scan_group([s * S5_LAT_NC for s in range(DEC_BATCH)], S5_LAT_NC, init, None)

    for j in range(S5_NPAIR):
        acc = None
        for d in range(2):
            for ri in range(2):
                hb = zh_ref[d, ri, :, sw * j:sw * (j + 1)].astype(BF16)
                t = _dot_nt(hb, e_ref[d, ri, j])
                acc = t if acc is None else acc + t
        for gl in range(2):
            g = 2 * j + gl
            ug = uy_ref[:, S5_CW * g:S5_CW * (g + 1)].astype(BF16)
            uy_ref[:, S5_CW * g:S5_CW * (g + 1)] = acc[:, S5_CW * gl:S5_CW * (gl + 1)] + _dot(ug, m_ref[g])

    lane = lax.broadcasted_iota(jnp.int32, (R, LANES), 1)
    in_piece = [jnp.logical_and(lane >= S5_P * k, lane < S5_P * (k + 1)) for k in range(LANES // S5_P)]
    for half, y_ref in enumerate((ylo_ref, yhi_ref)):
        for t in range(S5_L):
            merged = None
            for gl in range(gph):
                g = gph * half + gl
                slot = _s5_slot(g, t)
                col = S5_CW * g + LANES * (S5_P * slot // LANES)
                v = uy_ref[:, col:col + LANES]
                merged = v if merged is None else jnp.where(in_piece[slot % (LANES // S5_P)], v, merged)
            shift = (LANES - S5_P * t % LANES) % LANES
            y_ref[pl.ds(t, R, stride=S5_L), :] = pltpu.roll(merged, shift, 1) if shift else merged


def _s5(proj, m, f2, e2, al, h0, l):
    half = pl.BlockSpec((N_CTX, LANES), lambda s: (s, 0))
    return pl.pallas_call(
        _s5_kernel,
        grid=(2,),
        in_specs=[pl.BlockSpec((N_CTX, LANES), lambda s: (s, 0)),
                  pl.BlockSpec((N_CTX, LANES), lambda s: (s, 1)),
                  _layer_spec(m.shape[1:], l, single=True), _layer_spec(f2.shape[1:], l, single=True),
                  _layer_spec(e2.shape[1:], l, single=True), _layer_spec(al.shape[1:], l),
                  _layer_spec(h0.shape[1:], l)],
        out_specs=(half, half, _const_spec((2, 2, BATCH, S5_SW))),
        out_shape=(jax.ShapeDtypeStruct((N_TOK, LANES), F32),
                   jax.ShapeDtypeStruct((N_TOK, LANES), F32),
                   jax.ShapeDtypeStruct((2, 2, BATCH, S5_SW), F32)),
        scratch_shapes=[pltpu.VMEM((S5_STEP_ROWS, S5_G * S5_CW), F32),
                        pltpu.VMEM((2, 2, S5_STEP_ROWS, S5_SW), F32)],
        compiler_params=pltpu.CompilerParams(vmem_limit_bytes=VMEM_LIMIT),
        name="s5_mixer",
    )(proj, proj, m, f2, e2, al, h0)


HG_NLEV = int(math.log2(HG_L))
HG_W = HG_HEADS * HG_DK
HG_CPS = 4
HG_BLK = HG_CPS * HG_L
HG_NB_CTX = N_CTX // HG_BLK
HG_NC_CTX = SEQ // HG_BLK
HG_NC_LAT = DEC_SEQ // HG_BLK
assert HG_L == HG_DK


def _hgrn_consts():
    L = HG_L
    w = np.zeros((HG_NLEV + 2, L, L), np.float32)
    mask = np.zeros((HG_NLEV + 1, L, L), np.float32)
    for lev in range(HG_NLEV):
        blk = L >> lev
        half = blk // 2
        for t in range(L):
            p, o = divmod(t, blk)
            bd = p * blk + half - 1
            if o >= half:
                w[lev, t, bd + 1:t + 1] = 1.0
            else:
                w[lev, t, t + 1:bd + 1] = 1.0
        jj, ii = np.meshgrid(np.arange(L), np.arange(L), indexing='ij')
        mask[lev] = ((jj // blk == ii // blk) & (jj % blk >= half) & (ii % blk < half)).astype(np.float32)
    mask[HG_NLEV] = np.eye(L, dtype=np.float32)
    for t in range(L):
        w[HG_NLEV, t, :t + 1] = 1.0
        w[HG_NLEV + 1, t, t + 1:] = 1.0
    out = []
    for wd, md in ((w, mask), (w[:, ::-1, ::-1], mask[:, ::-1, ::-1])):
        wflat = wd.reshape((HG_NLEV + 2) * L, L)
        out.append((np.concatenate([wflat] * 3, axis=1), np.tile(md, (1, 1, HG_HEADS))))
    wcat = np.stack([out[0][0], out[1][0]])
    mask4 = np.stack([out[0][1], out[1][1]])
    hm = np.kron(np.eye(HG_HEADS, dtype=np.float32), np.ones((HG_DK, HG_DK), np.float32))
    return wcat, mask4, hm


def _hg_pos(i):
    is_ctx = i < HG_NB_CTX
    c = jnp.where(is_ctx, i % HG_NC_CTX, (i - HG_NB_CTX) % HG_NC_LAT)
    nc = jnp.where(is_ctx, HG_NC_CTX, HG_NC_LAT)
    return is_ctx, c, nc


def _hg_bwd_blk(i):
    _, c, nc = _hg_pos(i)
    return i + nc - 1 - 2 * c


def _hg_local(items, lb_ref, wcat_ref, mask_ref, hm):
    L = HG_L
    hmb = hm.astype(BF16)

    def bd4(x):
        return jnp.concatenate([x] * HG_HEADS, axis=0) * hmb

    kks, exs = [], []
    for d, q, z, v in items:
        lbp = lb_ref[d]
        sp = jnp.log1p(jnp.exp(-jnp.abs(z)))
        ls = jnp.minimum(z, 0.0) - sp
        kks.append(lbp[2:3] * jnp.exp(jnp.minimum(-z, 0.0) - sp))
        a = lbp[0:1]
        b = lbp[1:2] + ls
        logf = jnp.maximum(a, b) + jnp.log1p(jnp.exp(-jnp.abs(a - b)))
        p0 = logf.astype(BF16)
        r0 = logf - p0.astype(F32)
        p1 = r0.astype(BF16)
        p2 = (r0 - p1.astype(F32)).astype(BF16)
        exs.append(jnp.exp(_dot(wcat_ref[d], jnp.concatenate([p0, p1, p2], axis=0))))
    scs = [None] * len(items)
    for lev in range(HG_NLEV + 1):
        for n, (d, q, z, v) in enumerate(items):
            if lev < HG_NLEV:
                al = exs[n][L * lev:L * (lev + 1)]
                lhs = (q * al).astype(BF16)
                rhs = (kks[n] * al).astype(BF16)
            else:
                lhs = q.astype(BF16)
                rhs = kks[n].astype(BF16)
            t = _dot_nt(lhs, bd4(rhs)) * mask_ref[d, lev]
            scs[n] = t if scs[n] is None else scs[n] + t
    out = []
    for n, (d, q, z, v) in enumerate(items):
        vb = v.astype(BF16)
        o = _dot(scs[n].astype(BF16), bd4(vb))
        eq = exs[n][L * HG_NLEV:L * (HG_NLEV + 1)]
        qe = (q * eq).astype(BF16)
        ke = (kks[n] * exs[n][L * (HG_NLEV + 1):L * (HG_NLEV + 2)]).astype(BF16)
        g = eq[L - 1:L] if d == 0 else eq[0:1]
        out.append((o, qe, g, _dot_tn(vb, ke) * hm))
    return out


def _hgrn_kernel(qf, zf, vf, qb, zb, vb, lb_ref, w_ref, mask_ref, hm_ref, s0_ref, wup_ref, wdn_ref,
                 of_ref, ob_ref, sfin_ref, wupb_ref, wdnb_ref, st_ref):
    is_ctx, c, nc = _hg_pos(pl.program_id(0))
    wupb_ref[...] = wup_ref[...].astype(BF16)
    wdnb_ref[...] = wdn_ref[...].astype(BF16)

    @pl.when(c == 0)
    def _():
        st_ref[...] = jnp.where(is_ctx, 0.0, s0_ref[0])

    hm = hm_ref[...]
    rows = [slice(HG_L * k, HG_L * (k + 1)) for k in range(HG_CPS)]
    items = [(d, q_ref[r, :], z_ref[r, :], v_ref[r, :])
             for d, (q_ref, z_ref, v_ref) in enumerate(((qf, zf, vf), (qb, zb, vb))) for r in rows]
    loc = _hg_local(items, lb_ref, w_ref, mask_ref, hm)
    for d, o_ref in enumerate((of_ref, ob_ref)):
        st = st_ref[d]
        for k in (range(HG_CPS) if d == 0 else reversed(range(HG_CPS))):
            o, qe, g, upd = loc[HG_CPS * d + k]
            o_ref[rows[k], :] = o + _dot_nt(qe, st.astype(BF16))
            st = st * g + upd
        st_ref[d] = st

    @pl.when(jnp.logical_and(is_ctx, c == nc - 1))
    def _():
        for d in range(2):
            s_kv = st_ref[d].T
            for h in range(HG_HEADS):
                sfin_ref[0, d, h] = s_kv[HG_DK * h:HG_DK * (h + 1), HG_DK * h:HG_DK * (h + 1)]


def _hgrn(proj, lbp, consts, s0, w_up, w_down, l):
    wcat, mask4, hm = consts
    steps = N_TOK // HG_BLK
    up_rows = D_MODEL // steps
    dn_rows = 2 * D_FF // steps

    def fwd(col):
        return pl.BlockSpec((HG_BLK, HG_W), lambda i: (i, col))

    def bwd(col):
        return pl.BlockSpec((HG_BLK, HG_W), lambda i: (_hg_bwd_blk(i), col))

    return pl.pallas_call(
        _hgrn_kernel,
        grid=(N_TOK // HG_BLK,),
        in_specs=[fwd(COL_HQ), fwd(COL_HF_FWD), fwd(COL_HI), bwd(COL_HQ), bwd(COL_HF_BWD), bwd(COL_HI),
                  _layer_spec(lbp.shape[1:], l), _const_spec(wcat.shape), _const_spec(mask4.shape),
                  _const_spec(hm.shape),
                  pl.BlockSpec((None, 1, 2, HG_W, HG_W),
                               lambda i: (l, jnp.maximum(i - HG_NB_CTX, 0) // HG_NC_LAT, 0, 0, 0)),
                  pl.BlockSpec((None, up_rows, 2 * D_FF), lambda i: (l, i, 0)),
                  pl.BlockSpec((None, dn_rows, D_MODEL), lambda i: (l, i // 2, 0))],
        out_specs=(pl.BlockSpec((HG_BLK, HG_W), lambda i: (i, 0)),
                   pl.BlockSpec((HG_BLK, HG_W), lambda i: (_hg_bwd_blk(i), 0)),
                   pl.BlockSpec((1, 2, HG_HEADS, HG_DK, HG_DK),
                                lambda i: (jnp.minimum(i // HG_NC_CTX, BATCH - 1), 0, 0, 0, 0)),
                   pl.BlockSpec((up_rows, 2 * D_FF), lambda i: (i, 0)),
                   pl.BlockSpec((dn_rows, D_MODEL), lambda i: (i // 2, 0))),
        out_shape=(jax.ShapeDtypeStruct((N_TOK, HG_W), F32),
                   jax.ShapeDtypeStruct((N_TOK, HG_W), F32),
                   jax.ShapeDtypeStruct((BATCH, 2, HG_HEADS, HG_DK, HG_DK), F32),
                   jax.ShapeDtypeStruct((D_MODEL, 2 * D_FF), BF16),
                   jax.ShapeDtypeStruct((D_FF, D_MODEL), BF16)),
        scratch_shapes=[pltpu.VMEM((2, HG_W, HG_W), F32)],
        compiler_params=pltpu.CompilerParams(vmem_limit_bytes=VMEM_LIMIT),
        name="hgrn_mixer",
    )(proj, proj, proj, proj, proj, proj, lbp, wcat, mask4, hm, s0, w_up, w_down)


FN_CTX_SPG = 4
FN_LAT_TR = 512


def _dft_consts(t_len):
    n = GROUP_W // 4
    k = np.arange(n)
    ang = 2.0 * np.pi * ((k[:, None] * k[None, :]) % n) / n
    eye = np.eye(4)
    cs = np.concatenate([np.kron(eye, np.cos(ang)), np.kron(eye, np.sin(ang))], axis=1) / math.sqrt(n)
    t = np.arange(t_len)
    angt = 2.0 * np.pi * ((t[:, None] * t[None, :]) % t_len) / t_len
    dft = np.concatenate([np.cos(angt), -np.sin(angt)], axis=1) / math.sqrt(t_len)
    return cs.astype(np.float32), dft.astype(np.float32)


def _fnet_kernel(x_ref, cs_ref, dft_ref, w_ref, o_ref, r_ref, *, t_len, spg, tr):
    @pl.when(pl.program_id(1) == 0)
    def _():
        t = _dot(x_ref[...].astype(BF16), cs_ref[...])
        for s in range(spg):
            rows = slice(t_len * s, t_len * (s + 1))
            r_ref[s, 0:t_len, :] = t[rows, :GROUP_W].astype(BF16)
            r_ref[s, t_len:2 * t_len, :] = t[rows, GROUP_W:].astype(BF16)

    w = w_ref[...].astype(BF16)
    for s in range(spg):
        y = _dot(dft_ref[...], r_ref[s])
        o_ref[tr * s:tr * (s + 1), :] = _dot(y.astype(BF16), w)


def _fnet(proj, cs, dft, fn_w, l, *, n_seq, seq_len, tok0, spg, tr):
    nj = seq_len // tr
    assert spg == 1 or nj == 1
    gb0 = tok0 // (spg * seq_len)
    return pl.pallas_call(
        functools.partial(_fnet_kernel, t_len=seq_len, spg=spg, tr=tr),
        grid=(n_seq // spg, nj),
        in_specs=[
            pl.BlockSpec((spg * seq_len, GROUP_W), lambda g, j: (gb0 + g, COL_XC)),
            _const_spec((GROUP_W, 2 * GROUP_W)),
            pl.BlockSpec((tr, 2 * seq_len), lambda g, j: (j, 0)),
            _layer_spec((GROUP_W, GROUP_W), l),
        ],
        out_specs=pl.BlockSpec((spg * tr, GROUP_W), lambda g, j: (g * nj + j, 0)),
        out_shape=jax.ShapeDtypeStruct((n_seq * seq_len, GROUP_W), F32),
        scratch_shapes=[pltpu.VMEM((spg, 2 * seq_len, GROUP_W), BF16)],
        compiler_params=pltpu.CompilerParams(vmem_limit_bytes=VMEM_LIMIT),
        name="fnet_mixer",
    )(proj, cs, dft, fn_w)


def _outproj_kernel(*refs):
    (xa_ref, hgate_ref, gu_ref, gv_ref, ylo_ref, yhi_ref, of_ref, ob_ref, ycc_ref, ycl_ref, mod_ref, d_ref,
     wglu_ref, gn_ref, bdm_ref, gmn_ref, gmw_ref, gmb_ref, gmh_ref, wout_ref, n2_ref,
     x1_ref, h2_ref, wob_ref) = refs[-24:]
    x_refs = refs[:-24]

    @pl.when(pl.program_id(0) == 0)
    def _():
        wob_ref[...] = wout_ref[...].astype(BF16)

    gn = gn_ref[...]
    ys = jnp.concatenate([ylo_ref[...], yhi_ref[...]], axis=1)
    y5 = _gelu(ys + d_ref[...] * xa_ref[...])
    glu = jax.nn.sigmoid(_dot(y5.astype(BF16), wglu_ref[...].astype(BF16)))
    out_a = _rms(y5 * glu, gn[:, 0:GROUP_W])
    o = of_ref[...] + ob_ref[...]
    o2 = o * o
    o2h = o2.astype(BF16)
    o2l = (o2 - o2h.astype(F32)).astype(BF16)
    ms = _dot(o2h, bdm_ref[...]) + _dot(o2l, bdm_ref[...])
    hg = hgate_ref[...]
    out_b = o * lax.rsqrt(ms + EPS) * gn[:, GROUP_W:2 * GROUP_W] * (hg * jax.nn.sigmoid(hg))
    out_c = _rms(_x_value((ycc_ref, ycl_ref), TBP), gn[:, 2 * GROUP_W:3 * GROUP_W])
    gm_hm = gmh_ref[...]
    out_d = []
    for c in range(TBP // GM_CHUNK):
        rows = slice(GM_CHUNK * c, GM_CHUNK * (c + 1))
        gv = _rms(_gelu(gv_ref[rows, :]), gmn_ref[...]).astype(BF16)
        g4 = jnp.concatenate([gv] * GM_HEADS, axis=0) * gm_hm
        sp = _dot(gmw_ref[...], g4) + gmb_ref[...]
        out_d.append(_rms(_gelu(gu_ref[rows, :]) * sp, gn[:, 3 * GROUP_W:]))
    out_d = jnp.concatenate(out_d, axis=0)
    m = None
    for k, part in enumerate((out_a, out_b, out_c, out_d)):
        t = _dot(part.astype(BF16), wob_ref[GROUP_W * k:GROUP_W * (k + 1), :])
        m = t if m is None else m + t
    mod = mod_ref[...]
    g1 = mod[:, 2 * D_MODEL:3 * D_MODEL]
    sh2 = mod[:, 3 * D_MODEL:4 * D_MODEL]
    sc2 = mod[:, 4 * D_MODEL:5 * D_MODEL]
    x1 = _x_value(x_refs, TBP) + g1 * m
    x1_ref[...] = x1
    h2_ref[...] = (_rms(x1, n2_ref[...]) * (1.0 + sc2) + sh2).astype(BF16)


def _outproj(xs, proj, ylo, yhi, of, ob, yc, mods, s5_d, wglu, gn, bdm, gm_norm_g, gm_w, gm_b, gm_hm, wout,
             norm2_g, l):
    def tok(width, col=0):
        return pl.BlockSpec((TBP, width), lambda i: (i, col))

    return pl.pallas_call(
        _outproj_kernel,
        grid=(N_TOK // TBP,),
        in_specs=_x_specs(len(xs) == 2, tb=TBP) + [
            tok(GROUP_W, COL_XA), tok(GROUP_W, COL_HGATE), tok(GROUP_W, COL_GU), tok(GROUP_W, COL_GV),
            tok(LANES), tok(LANES), tok(GROUP_W), tok(GROUP_W), *_x_specs(True, GROUP_W, TBP),
            _mod_spec(l, TBP),
            _layer_spec((1, GROUP_W), l), _layer_spec((GROUP_W, GROUP_W), l), _layer_spec((1, D_MODEL), l),
            _const_spec((GROUP_W, GROUP_W)),
            _layer_spec((1, GROUP_W), l), _layer_spec((GM_CHUNK, GM_HEADS * GM_CHUNK), l),
            _layer_spec((GM_CHUNK, GROUP_W), l), _const_spec((GM_HEADS * GM_CHUNK, GROUP_W)),
            _layer_spec((D_MODEL, D_MODEL), l, single=True), _layer_spec((1, D_MODEL), l),
        ],
        out_specs=(tok(D_MODEL), tok(D_MODEL)),
        out_shape=(jax.ShapeDtypeStruct((N_TOK, D_MODEL), F32),
                   jax.ShapeDtypeStruct((N_TOK, D_MODEL), BF16)),
        scratch_shapes=[pltpu.VMEM((D_MODEL, D_MODEL), BF16)],
        compiler_params=pltpu.CompilerParams(vmem_limit_bytes=VMEM_LIMIT),
        name="out_proj",
    )(*xs, proj, proj, proj, proj, ylo, yhi, of, ob, *yc, mods, s5_d, wglu, gn, bdm, gm_norm_g, gm_w, gm_b,
      gm_hm, wout, norm2_g)


FF_SEG = GRID_W
FF_NSEG = TB // FF_SEG
FF_SEQ_STRIPS = SEQ // FF_SEG
FF_PERM_ROWS = FF_SEQ_STRIPS * FF_SEG
SUBLANES = 8
FF_NGRP = FF_SEG // SUBLANES
GELU_C0 = 0.7978845608028654
GELU_C1 = GELU_C0 * 0.044715


def _ffn_perm_consts():
    p = np.zeros((FF_PERM_ROWS, FF_PERM_ROWS), np.float32)
    for k in range(FF_SEQ_STRIPS):
        for t in range(FF_SEG):
            p[FF_SEG * k + SUBLANES * (t % FF_NGRP) + t // FF_NGRP, FF_SEG * k + t] = 1.0
    return p


def _ffn_kernel(x1_ref, h2_ref, mod_ref, perm_ref, wup_ref, cw_ref, cb_ref, wdn_ref, fg_ref, *rest, final):
    *o_refs, h2p_ref, z_ref, hid_ref = rest
    i = pl.program_id(0)
    joined = (i < NB_CTX).astype(F32)
    n_tiles = D_FF // FF_TILE
    sub = lax.broadcasted_iota(jnp.int32, (SUBLANES, 2 * FF_TILE), 0)
    is_first = sub == 0
    is_last = sub == SUBLANES - 1
    zero_grp = jnp.zeros((SUBLANES, 2 * FF_TILE), F32)
    for r in range(0, TB, FF_PERM_ROWS):
        h2p_ref[r:r + FF_PERM_ROWS, :] = _dot(perm_ref[...], h2_ref[r:r + FF_PERM_ROWS, :]).astype(BF16)

    def up(j):
        lo = FF_TILE * j
        for c, col in enumerate((lo, D_FF + lo)):
            z_ref[j % 2, :, FF_TILE * c:FF_TILE * (c + 1)] = _dot(h2p_ref[...], wup_ref[:, col:col + FF_TILE])

    def gate(j):
        slot = j % 2
        lo = FF_TILE * j
        w = jnp.concatenate([cw_ref[:, lo:lo + FF_TILE], cw_ref[:, D_FF + lo:D_FF + lo + FF_TILE]], axis=1)
        b = jnp.concatenate([cb_ref[:, lo:lo + FF_TILE], cb_ref[:, D_FF + lo:D_FF + lo + FF_TILE]], axis=1)

        def grp(k, v):
            r = FF_SEG * k + SUBLANES * v
            return z_ref[slot, r:r + SUBLANES, :]

        down = [pltpu.roll(grp(k, FF_NGRP - 1), 1, 0) for k in range(FF_NSEG)]
        up_ = [pltpu.roll(grp(k, 0), SUBLANES - 1, 0) for k in range(FF_NSEG)]
        strips = []
        for k in range(FF_NSEG):
            g = [grp(k, v) for v in range(FF_NGRP)]
            before = down[k - 1] * joined if k % FF_SEQ_STRIPS > 0 else zero_grp
            after = up_[k + 1] * joined if k % FF_SEQ_STRIPS < FF_SEQ_STRIPS - 1 else zero_grp
            zm1 = jnp.concatenate([jnp.where(is_first, before, down[k])] + g[:-1], axis=0)
            zp1 = jnp.concatenate(g[1:] + [jnp.where(is_last, after, up_[k])], axis=0)
            zc = b + zm1 * w[0:1] + jnp.concatenate(g, axis=0) * w[1:2] + zp1 * w[2:3]
            a = zc[:, :FF_TILE]
            t = jnp.tanh(a * (GELU_C0 + GELU_C1 * (a * a)))
            strips.append((a * (0.5 + 0.5 * t) * zc[:, FF_TILE:]).astype(BF16))
        hid_ref[:, lo:lo + FF_TILE] = jnp.concatenate(strips, axis=0)

    up(0)
    for j in range(n_tiles):
        if j + 1 < n_tiles:
            up(j + 1)
        gate(j)
    acc = _dot(hid_ref[...], wdn_ref[...])
    acc = jnp.swapaxes(acc.reshape(FF_NSEG, FF_NGRP, SUBLANES, D_MODEL), 1, 2).reshape(TB, D_MODEL)
    g2 = mod_ref[...][:, 5 * D_MODEL:6 * D_MODEL]
    x2 = x1_ref[...] + g2 * acc
    if final:
        y = _rms(x2, fg_ref[...])
        yp_ref, ys_ref = o_refs

        @pl.when(i < NB_CTX)
        def _():
            yp_ref[...] = y

        @pl.when(i >= NB_CTX)
        def _():
            ys_ref[...] = y
    else:
        o_refs[0][...] = x2


def _ffn(x1, h2, mods, perm, wup_bf, conv_w, conv_b, wdn_bf, final_g, l, *, final):
    if final:
        out_specs = tuple(_x_specs(True))
        out_shape = (jax.ShapeDtypeStruct((N_CTX, D_MODEL), F32), jax.ShapeDtypeStruct((N_LAT, D_MODEL), F32))
    else:
        out_specs = pl.BlockSpec((TB, D_MODEL), lambda i: (i, 0))
        out_shape = jax.ShapeDtypeStruct((N_TOK, D_MODEL), F32)
    return pl.pallas_call(
        functools.partial(_ffn_kernel, final=final),
        grid=(N_TOK // TB,),
        in_specs=[
            pl.BlockSpec((TB, D_MODEL), lambda i: (i, 0)),
            pl.BlockSpec((TB, D_MODEL), lambda i: (i, 0)),
            _mod_spec(l), _const_spec((FF_PERM_ROWS, FF_PERM_ROWS)),
            _const_spec((D_MODEL, 2 * D_FF), single=True), _layer_spec((3, 2 * D_FF), l),
            _layer_spec((1, 2 * D_FF), l), _const_spec((D_FF, D_MODEL), single=True),
            _const_spec((1, D_MODEL)),
        ],
        out_specs=out_specs,
        out_shape=out_shape,
        scratch_shapes=[pltpu.VMEM((TB, D_MODEL), BF16), pltpu.VMEM((2, TB, 2 * FF_TILE), F32),
                        pltpu.VMEM((TB, D_FF), BF16)],
        compiler_params=pltpu.CompilerParams(vmem_limit_bytes=VMEM_LIMIT),
        name="conv_ffn",
    )(x1, h2, mods, perm, wup_bf, conv_w, conv_b, wdn_bf, final_g.reshape(1, D_MODEL))


def kernel(x_prompt, x_sample, state_s5_re, state_s5_im, state_hgrn, c, c_ctx, w_ada, b_ada, norm1_g,
           norm2_g, w_in, s5_lam_re, s5_lam_im, s5_log_dt, s5_b_re, s5_b_im, s5_c_re, s5_c_im, s5_d,
           s5_w_glu, hg_lb_logits, fn_w, gm_norm_g, gm_ws, gm_bs, grp_norm_g, w_out, ffn_w_up,
           ffn_conv_w, ffn_conv_b, ffn_w_down, final_norm_g):
    xs = (x_prompt.reshape(N_CTX, D_MODEL), x_sample.reshape(N_LAT, D_MODEL))
    cvecs = jnp.concatenate([c_ctx[None], c, jnp.zeros((8 - 1 - DEC_BATCH, D_MODEL), F32)], axis=0)
    mods = _ada(cvecs, w_ada, b_ada).reshape(DEPTH, 8, 1, N_MOD * D_MODEL)

    lb_p = jax.nn.softmax(hg_lb_logits.astype(F32), axis=0)
    lbs = jnp.maximum(jnp.cumsum(lb_p, axis=0) - lb_p[0], 0.0)
    lbps = jnp.stack([jnp.log(lbs), jnp.log1p(-lbs), 1.0 - lbs], axis=2)

    hg_consts_np = _hgrn_consts()
    hg_consts = (jnp.asarray(hg_consts_np[0], BF16), jnp.asarray(hg_consts_np[1], F32),
                 jnp.asarray(hg_consts_np[2], F32))
    cs_np, dft_ctx_np = _dft_consts(SEQ)
    _, dft_lat_np = _dft_consts(DEC_SEQ)
    cs = jnp.asarray(cs_np, F32).astype(BF16)
    dft_ctx = jnp.asarray(dft_ctx_np, F32).astype(BF16)
    dft_lat = jnp.asarray(dft_lat_np, F32).astype(BF16)
    gm_hm = jnp.asarray(np.kron(np.eye(GM_HEADS), np.ones((GM_CHUNK, GROUP_W // GM_HEADS))), BF16)
    bdm = jnp.asarray(np.kron(np.eye(HG_HEADS), np.ones((HG_DK, HG_DK))) / HG_DK, BF16)

    tables = _s5_tables(s5_lam_re, s5_lam_im, s5_log_dt, s5_b_re, s5_b_im, s5_c_re, s5_c_im)
    s5_m, s5_f, s5_e = _s5_prep(tables, jnp.asarray(_s5_shift_consts(), BF16))
    s5_al = tables[-1]
    s5_h0 = jnp.stack([state_s5_re, state_s5_im]).astype(F32).transpose(2, 3, 0, 1, 4, 5)
    s5_h0 = s5_h0.reshape(DEPTH, 2, 2, DEC_BATCH, S5_SW)
    hg_s0 = jnp.einsum('bldhkv,hg->lbdhvgk', state_hgrn.astype(F32), jnp.eye(HG_HEADS, dtype=F32))
    hg_s0 = hg_s0.reshape(DEPTH, DEC_BATCH, 2, HG_W, HG_W)

    def rows(a):
        return a.reshape(DEPTH, 1, a.shape[-1])

    norm1_r, norm2_r, gn_r, s5_d_r, gmn_r, cb_r = (rows(a) for a in (
        norm1_g, norm2_g, grp_norm_g, s5_d, gm_norm_g, ffn_conv_b))
    gm_w = gm_ws.transpose(0, 2, 1, 3).reshape(DEPTH, GM_CHUNK, GM_HEADS * GM_CHUNK).astype(BF16)
    gm_b = jnp.repeat(gm_bs.transpose(0, 2, 1), GROUP_W // GM_HEADS, axis=2)
    ff_perm = jnp.asarray(_ffn_perm_consts(), BF16)

    new_re, new_im, new_hg = [], [], []
    for l in range(DEPTH):
        proj = _inproj(xs, mods, norm1_r, w_in, l)

        ylo, yhi, hfin = _s5(proj, s5_m, s5_f, s5_e, s5_al, s5_h0, l)
        hfin = hfin.reshape(2, 2, BATCH, S5_G, S5_N).transpose(1, 2, 0, 3, 4)
        new_re.append(hfin[0])
        new_im.append(hfin[1])

        of, ob, sfin, wup_bf, wdn_bf = _hgrn(proj, lbps, hg_consts, hg_s0, ffn_w_up, ffn_w_down, l)
        new_hg.append(sfin)

        yc = (_fnet(proj, cs, dft_ctx, fn_w, l, n_seq=BATCH, seq_len=SEQ, tok0=0, spg=FN_CTX_SPG, tr=SEQ),
              _fnet(proj, cs, dft_lat, fn_w, l, n_seq=DEC_BATCH, seq_len=DEC_SEQ, tok0=N_CTX, spg=1,
                    tr=FN_LAT_TR))

        x1, h2 = _outproj(xs, proj, ylo, yhi, of, ob, yc, mods, s5_d_r, s5_w_glu, gn_r, bdm, gmn_r, gm_w, gm_b,
                          gm_hm, w_out, norm2_r, l)
        res = _ffn(x1, h2, mods, ff_perm, wup_bf, ffn_conv_w, cb_r, wdn_bf, final_norm_g, l,
                   final=(l == DEPTH - 1))
        xs = (res,)

    y_prompt = res[0].reshape(BATCH, SEQ, D_MODEL)
    y_sample = res[1].reshape(DEC_BATCH, DEC_SEQ, D_MODEL)
    return (y_prompt, y_sample, jnp.stack(new_re, axis=1), jnp.stack(new_im, axis=1),
            jnp.stack(new_hg, axis=1))
```

```python
import functools
import math

import numpy as np
import jax
import jax.numpy as jnp
from jax import lax
from jax.experimental import pallas as pl
from jax.experimental.pallas import tpu as pltpu

D_MODEL = 1024
BATCH = 16
SEQ = 256
DEPTH = 2
DEC_BATCH = 2
DEC_SEQ = 2048
GRID_W = 64
GROUP_W = 256
S5_P = 16
S5_G = 16
S5_N = 64
HG_HEADS = 4
HG_DK = 64
GM_HEADS = 4
GM_CHUNK = 128
D_FF = 2816
N_MOD = 6
D_IN = 9 * GROUP_W
EPS = 1e-6
LAM_RE_MAX = -1e-4

N_CTX = BATCH * SEQ
N_LAT = DEC_BATCH * DEC_SEQ
N_TOK = N_CTX + N_LAT
TB = 512
TBP = 512
NB_CTX = N_CTX // TB
S5_L = 16
HG_L = 64
FF_TILE = 256
LANES = 128
VMEM_LIMIT = 56 * 1024 * 1024

F32 = jnp.float32
BF16 = jnp.bfloat16

COL_XA, COL_HQ, COL_HF_FWD, COL_HF_BWD, COL_HI, COL_HGATE, COL_XC, COL_GU, COL_GV = range(9)


def _mod_row(i, tb):
    return jnp.where(i < N_CTX // tb, 0, 1 + (i - N_CTX // tb) // (DEC_SEQ // tb))


def _gelu(x):
    return 0.5 * x * (1.0 + jnp.tanh(0.7978845608028654 * (x + 0.044715 * (x * x * x))))


def _rms(x, g):
    return x * lax.rsqrt(jnp.mean(x * x, axis=-1, keepdims=True) + EPS) * g


def _dot(a, b):
    return jnp.dot(a, b, preferred_element_type=F32)


def _dot_nt(a, b, precision=None):
    return lax.dot_general(a, b, (((1,), (1,)), ((), ())), precision=precision, preferred_element_type=F32)


def _dot_tn(a, b):
    return lax.dot_general(a, b, (((0,), (0,)), ((), ())), preferred_element_type=F32)


def _const_spec(shape, single=False):
    kw = {"pipeline_mode": pl.Buffered(1)} if single else {}
    return pl.BlockSpec(shape, lambda *_: (0,) * len(shape), **kw)


def _layer_spec(shape, l, single=False):
    kw = {"pipeline_mode": pl.Buffered(1)} if single else {}
    return pl.BlockSpec((None,) + tuple(shape), lambda *_: (l,) + (0,) * len(shape), **kw)


def _mod_spec(l, tb=TB):
    return pl.BlockSpec((None, None, 1, N_MOD * D_MODEL), lambda i: (l, _mod_row(i, tb), 0, 0))


def _x_specs(split, width=D_MODEL, tb=TB):
    nb_ctx = N_CTX // tb
    if split:
        return [pl.BlockSpec((tb, width), lambda i: (jnp.minimum(i, nb_ctx - 1), 0)),
                pl.BlockSpec((tb, width), lambda i: (jnp.maximum(i - nb_ctx, 0), 0))]
    return [pl.BlockSpec((tb, width), lambda i: (i, 0))]


def _x_value(x_refs, tb=TB):
    if len(x_refs) == 1:
        return x_refs[0][...]
    return jnp.where(pl.program_id(0) < N_CTX // tb, x_refs[0][...], x_refs[1][...])


def _inproj_kernel(*refs):
    *x_refs, mod_ref, g_ref, w_ref, o_ref, wb_ref = refs

    @pl.when(pl.program_id(0) == 0)
    def _():
        wb_ref[...] = w_ref[...].astype(BF16)

    mod = mod_ref[...]
    sh = mod[:, 0:D_MODEL]
    sc = mod[:, D_MODEL:2 * D_MODEL]
    h = _rms(_x_value(x_refs, TBP), g_ref[...]) * (1.0 + sc) + sh
    o_ref[...] = _dot(h.astype(BF16), wb_ref[...])


def _inproj(xs, mods, norm_g, w_in, l):
    return pl.pallas_call(
        _inproj_kernel,
        grid=(N_TOK // TBP,),
        in_specs=_x_specs(len(xs) == 2, tb=TBP) + [
            _mod_spec(l, TBP),
            _layer_spec((1, D_MODEL), l),
            _layer_spec((D_MODEL, D_IN), l, single=True),
        ],
        out_specs=pl.BlockSpec((TBP, D_IN), lambda i: (i, 0)),
        out_shape=jax.ShapeDtypeStruct((N_TOK, D_IN), F32),
        scratch_shapes=[pltpu.VMEM((D_MODEL, D_IN), BF16)],
        compiler_params=pltpu.CompilerParams(vmem_limit_bytes=VMEM_LIMIT),
        name="in_proj",
    )(*xs, mods, norm_g, w_in)


S5_NPAIR = S5_G // 2
S5_CW = S5_L * S5_P
S5_SW = S5_G * S5_N
S5_STEP_ROWS = N_CTX // S5_L
S5_CTX_NC = SEQ // S5_L
S5_LAT_NC = DEC_SEQ // S5_L


def _s5_slot(g, t):
    return (t + g) % S5_L


def _s5_tables(lam_re, lam_im, log_dt, b_re, b_im, c_re, c_im):
    lr = jnp.minimum(lam_re.astype(F32), LAM_RE_MAX)
    li = lam_im.astype(F32)
    dt = jnp.exp(log_dt.astype(F32))[..., None]
    mag = jnp.exp(lr * dt)
    ang = li * dt
    ab_re = mag * jnp.cos(ang)
    ab_im = mag * jnp.sin(ang)
    den = lr * lr + li * li
    xr = ab_re - 1.0
    z_re = (xr * lr + ab_im * li) / den
    z_im = (ab_im * lr - xr * li) / den
    bb_re = z_re[..., None] * b_re - z_im[..., None] * b_im
    bb_im = z_re[..., None] * b_im + z_im[..., None] * b_re
    tau = jnp.arange(S5_L + 1, dtype=F32)[:, None, None, None, None]
    pm = jnp.exp(lr * dt * tau)
    pa = li * dt * tau
    pw_re = pm * jnp.cos(pa)
    pw_im = pm * jnp.sin(pa)
    eye2 = jnp.eye(2, dtype=F32)

    def pw_pairs(a):
        a = a.transpose(1, 2, 3, 0, 4).reshape(DEPTH, 2, S5_NPAIR, 2, S5_L + 1, S5_N)
        return a.transpose(0, 1, 2, 4, 3, 5).reshape(DEPTH, 2, S5_NPAIR, S5_L + 1, 2 * S5_N)

    def mat_pairs(a):
        a = a.reshape(DEPTH, 2, S5_NPAIR, 2, S5_P, S5_N)
        return jnp.einsum('ldjaqn,ab->ldjaqbn', a, eye2).reshape(DEPTH, 2, S5_NPAIR, 2, S5_P, 2 * S5_N)

    al = jnp.stack([pw_re[S5_L], pw_im[S5_L]], axis=2).reshape(DEPTH, 2, 2, 1, S5_SW)
    return (pw_pairs(pw_re), pw_pairs(pw_im),
            mat_pairs(bb_re.transpose(0, 1, 2, 4, 3)), mat_pairs(bb_im.transpose(0, 1, 2, 4, 3)),
            mat_pairs(c_re.astype(F32)), mat_pairs(c_im.astype(F32)), al)


def _s5_shift_consts():
    n = S5_CW
    r = np.arange(n)[:, None]
    c = np.arange(n)[None, :]
    fwd = [(c == r + S5_P * s) for s in range(S5_L)]
    bwd = [(c == r - S5_P * (S5_L - 1 - s)) for s in range(S5_L)]
    return np.stack([np.concatenate(fwd, axis=1), np.concatenate(bwd, axis=1)]).astype(np.float32)


PREP_STEPS = 4
ADA_TN = N_MOD * D_MODEL // PREP_STEPS


def _prep_kernel(c_ref, wada_ref, bada_ref, pwr_ref, pwi_ref, br_ref, bi_ref, cr_ref, ci_ref, scat_ref,
                 mod_ref, m_ref, f_ref, e_ref, k_ref):
    cv = c_ref[...]
    mod_ref[0] = _dot((cv * jax.nn.sigmoid(cv)).astype(BF16), wada_ref[0].astype(BF16)) + bada_ref[0]
    step = pl.program_id(1)
    for qq in range(PREP_STEPS):
        pl.when(step == qq)(functools.partial(_s5_pair_ops, qq, pwr_ref, pwi_ref, br_ref, bi_ref, cr_ref,
                                              ci_ref, f_ref, e_ref, k_ref))
    pl.when(step == PREP_STEPS - 1)(functools.partial(_s5_toeplitz, scat_ref, m_ref, k_ref))


def _s5_pair_ops(qq, pwr_ref, pwi_ref, br_ref, bi_ref, cr_ref, ci_ref, f_ref, e_ref, k_ref):
    L = S5_L
    hp = lax.Precision.HIGHEST
    sw = 2 * S5_N
    per_step = S5_NPAIR // PREP_STEPS
    for j in range(per_step * qq, per_step * (qq + 1)):
        for d in range(2):
            pwr = pwr_ref[0, d, j]
            pwi = pwi_ref[0, d, j]
            for gl in range(2):
                g = 2 * j + gl
                br, bi = br_ref[0, d, j, gl], bi_ref[0, d, j, gl]
                cr, ci = cr_ref[0, d, j, gl], ci_ref[0, d, j, gl]
                ca_r, ca_i, f_r, f_i, e_r, e_i = [], [], [], [], [], []
                for t in range(L):
                    kk = t if d == 0 else L - 1 - t
                    pr, pi = pwr[kk:kk + 1], pwi[kk:kk + 1]
                    ca_r.append(cr * pr - ci * pi)
                    ca_i.append(cr * pi + ci * pr)
                    kf = L - 1 - t if d == 0 else t
                    pr, pi = pwr[kf:kf + 1], pwi[kf:kf + 1]
                    f_r.append(br * pr - bi * pi)
                    f_i.append(br * pi + bi * pr)
                    ke = t + 1 if d == 0 else L - t
                    pr, pi = pwr[ke:ke + 1], pwi[ke:ke + 1]
                    e_r.append(cr * pr - ci * pi)
                    e_i.append(-(cr * pi + ci * pr))
                cat = lambda xs: jnp.concatenate(xs, axis=0)
                k = _dot_nt(br, cat(ca_r), hp) - _dot_nt(bi, cat(ca_i), hp)
                k_ref[d, S5_P * g:S5_P * (g + 1), :] = k
                slots = lambda xs: cat([xs[(p - g) % L] for p in range(L)])
                rows = slice(S5_CW * gl, S5_CW * (gl + 1))
                f_ref[0, d, j, rows, 0:sw] = slots(f_r).astype(BF16)
                f_ref[0, d, j, rows, sw:2 * sw] = slots(f_i).astype(BF16)
                e_ref[0, d, 0, j, rows, :] = slots(e_r).astype(BF16)
                e_ref[0, d, 1, j, rows, :] = slots(e_i).astype(BF16)


def _s5_toeplitz(scat_ref, m_ref, k_ref):
    L = S5_L
    kf = k_ref[0].astype(BF16)
    kb = k_ref[1].astype(BF16)
    for s in range(L):
        cols = slice(S5_CW * s, S5_CW * (s + 1))
        res = _dot(kf, scat_ref[0, :, cols]) + _dot(kb, scat_ref[1, :, cols])
        for g in range(S5_G):
            blk = res[S5_P * g:S5_P * (g + 1), :]
            if g:
                blk = pltpu.roll(blk, S5_P * g, 1)
            r0 = S5_P * _s5_slot(g, s)
            m_ref[0, g, r0:r0 + S5_P, :] = blk.astype(BF16)


def _prep(cvecs, w_ada, b_ada, tables, scat):
    pwr, pwi, br, bi, cr, ci, _ = tables
    n = N_MOD * D_MODEL

    def lspec(shape):
        return pl.BlockSpec((1,) + shape, lambda l, q: (l,) + (0,) * len(shape))

    pw_shape = (2, S5_NPAIR, S5_L + 1, 2 * S5_N)
    mat_shape = (2, S5_NPAIR, 2, S5_P, 2 * S5_N)
    return pl.pallas_call(
        _prep_kernel,
        grid=(DEPTH, PREP_STEPS),
        in_specs=[pl.BlockSpec((8, D_MODEL), lambda l, q: (0, 0)),
                  pl.BlockSpec((1, D_MODEL, ADA_TN), lambda l, q: (l, 0, q)),
                  pl.BlockSpec((1, 1, ADA_TN), lambda l, q: (l, 0, q)),
                  lspec(pw_shape), lspec(pw_shape), lspec(mat_shape), lspec(mat_shape), lspec(mat_shape),
                  lspec(mat_shape), _const_spec(scat.shape)],
        out_specs=(pl.BlockSpec((1, 8, ADA_TN), lambda l, q: (l, 0, q)),
                   lspec((S5_G, S5_CW, S5_CW)), lspec((2, S5_NPAIR, 2 * S5_CW, 4 * S5_N)),
                   lspec((2, 2, S5_NPAIR, 2 * S5_CW, 2 * S5_N))),
        out_shape=(jax.ShapeDtypeStruct((DEPTH, 8, n), F32),
                   jax.ShapeDtypeStruct((DEPTH, S5_G, S5_CW, S5_CW), BF16),
                   jax.ShapeDtypeStruct((DEPTH, 2, S5_NPAIR, 2 * S5_CW, 4 * S5_N), BF16),
                   jax.ShapeDtypeStruct((DEPTH, 2, 2, S5_NPAIR, 2 * S5_CW, 2 * S5_N), BF16)),
        scratch_shapes=[pltpu.VMEM((2, S5_G * S5_P, S5_CW), F32)],
        compiler_params=pltpu.CompilerParams(vmem_limit_bytes=VMEM_LIMIT),
        name="param_prep",
    )(cvecs, w_ada, b_ada.reshape(DEPTH, 1, n), pwr, pwi, br, bi, cr, ci, scat)


def _s5_kernel(xlo_ref, xhi_ref, m_ref, f_ref, e_ref, al_ref, h0_ref, ylo_ref, yhi_ref, hfin_ref,
               uy_ref, zh_ref):
    R = S5_STEP_ROWS
    pw = 2 * S5_CW
    sw = 2 * S5_N
    gph = S5_G // 2
    step = pl.program_id(0)

    for half, x_ref in enumerate((xlo_ref, xhi_ref)):
        for t in range(S5_L):
            xt = x_ref[pl.ds(t, R, stride=S5_L), :]
            shift = (S5_P * t) % LANES
            xr = pltpu.roll(xt, shift, 1) if shift else xt
            for gl in range(gph):
                g = gph * half + gl
                src = (S5_P * gl + shift) % LANES
                lo = S5_CW * g + S5_P * _s5_slot(g, t)
                assert lo % LANES == src
                uy_ref[:, lo:lo + S5_P] = xr[:, src:src + S5_P]

    for j in range(S5_NPAIR):
        ub = uy_ref[:, pw * j:pw * (j + 1)].astype(BF16)
        for d in range(2):
            o = _dot(ub, f_ref[d, j])
            zh_ref[d, 0, :, sw * j:sw * (j + 1)] = o[:, :sw]
            zh_ref[d, 1, :, sw * j:sw * (j + 1)] = o[:, sw:]

    def scan_group(seq_rows, nc, init, fin_rows):
        chains = [(d, r0) for d in range(2) for r0 in seq_rows]

        def body(c, carry):
            out = []
            for k, (d, r0) in enumerate(chains):
                hr, hi = carry[2 * k], carry[2 * k + 1]
                r = r0 + (c if d == 0 else nc - 1 - c)
                zr = zh_ref[d, 0, pl.ds(r, 1), :]
                zi = zh_ref[d, 1, pl.ds(r, 1), :]
                zh_ref[d, 0, pl.ds(r, 1), :] = hr
                zh_ref[d, 1, pl.ds(r, 1), :] = hi
                ar = al_ref[d, 0]
                ai = al_ref[d, 1]
                out.append(ar * hr - ai * hi + zr)
                out.append(ar * hi + ai * hr + zi)
            return tuple(out)

        fin = lax.fori_loop(0, nc, body, tuple(init))
        if fin_rows is not None:
            for k, (d, _) in enumerate(chains):
                s = fin_rows[k % len(seq_rows)]
                hfin_ref[d, 0, s:s + 1, :] = fin[2 * k]
                hfin_ref[d, 1, s:s + 1, :] = fin[2 * k + 1]

    @pl.when(step == 0)
    def _():
        zero = jnp.zeros((1, S5_SW), F32)
        for s0 in range(0, BATCH, 4):
            seqs = list(range(s0, s0 + 4))
            scan_group([s * S5_CTX_NC for s in seqs], S5_CTX_NC, [zero] * 16, seqs)

    @pl.when(step == 1)
    def _():
        init = []
        for d in range(2):
            for s in range(DEC_BATCH):
                init.append(h0_ref[d, 0, s:s + 1, :])
                init.append(h0_ref[d, 1, s:s + 1, :])
        scan_group([s * S5_LAT_NC for s in range(DEC_BATCH)], S5_LAT_NC, init, None)

    for j in range(S5_NPAIR):
        acc = None
        for d in range(2):
            for ri in range(2):
                hb = zh_ref[d, ri, :, sw * j:sw * (j + 1)].astype(BF16)
                t = _dot_nt(hb, e_ref[d, ri, j])
                acc = t if acc is None else acc + t
        for gl in range(2):
            g = 2 * j + gl
            ug = uy_ref[:, S5_CW * g:S5_CW * (g + 1)].astype(BF16)
            uy_ref[:, S5_CW * g:S5_CW * (g + 1)] = acc[:, S5_CW * gl:S5_CW * (gl + 1)] + _dot(ug, m_ref[g])

    lane = lax.broadcasted_iota(jnp.int32, (R, LANES), 1)
    in_piece = [jnp.logical_and(lane >= S5_P * k, lane < S5_P * (k + 1)) for k in range(LANES // S5_P)]
    for half, y_ref in enumerate((ylo_ref, yhi_ref)):
        for t in range(S5_L):
            merged = None
            for gl in range(gph):
                g = gph * half + gl
                slot = _s5_slot(g, t)
                col = S5_CW * g + LANES * (S5_P * slot // LANES)
                v = uy_ref[:, col:col + LANES]
                merged = v if merged is None else jnp.where(in_piece[slot % (LANES // S5_P)], v, merged)
            shift = (LANES - S5_P * t % LANES) % LANES
            y_ref[pl.ds(t, R, stride=S5_L), :] = pltpu.roll(merged, shift, 1) if shift else merged


def _s5(proj, m, f2, e2, al, h0, l):
    half = pl.BlockSpec((N_CTX, LANES), lambda s: (s, 0))
    return pl.pallas_call(
        _s5_kernel,
        grid=(2,),
        in_specs=[pl.BlockSpec((N_CTX, LANES), lambda s: (s, 0)),
                  pl.BlockSpec((N_CTX, LANES), lambda s: (s, 1)),
                  _layer_spec(m.shape[1:], l, single=True), _layer_spec(f2.shape[1:], l, single=True),
                  _layer_spec(e2.shape[1:], l, single=True), _layer_spec(al.shape[1:], l),
                  _layer_spec(h0.shape[1:], l)],
        out_specs=(half, half, _const_spec((2, 2, BATCH, S5_SW))),
        out_shape=(jax.ShapeDtypeStruct((N_TOK, LANES), F32),
                   jax.ShapeDtypeStruct((N_TOK, LANES), F32),
                   jax.ShapeDtypeStruct((2, 2, BATCH, S5_SW), F32)),
        scratch_shapes=[pltpu.VMEM((S5_STEP_ROWS, S5_G * S5_CW), F32),
                        pltpu.VMEM((2, 2, S5_STEP_ROWS, S5_SW), F32)],
        compiler_params=pltpu.CompilerParams(vmem_limit_bytes=VMEM_LIMIT),
        name="s5_mixer",
    )(proj, proj, m, f2, e2, al, h0)


HG_NLEV = int(math.log2(HG_L))
HG_W = HG_HEADS * HG_DK
HG_CPS = 4
HG_BLK = HG_CPS * HG_L
HG_NB_CTX = N_CTX // HG_BLK
HG_NC_CTX = SEQ // HG_BLK
HG_NC_LAT = DEC_SEQ // HG_BLK
assert HG_L == HG_DK


def _hgrn_consts():
    L = HG_L
    w = np.zeros((HG_NLEV + 2, L, L), np.float32)
    mask = np.zeros((HG_NLEV + 1, L, L), np.float32)
    for lev in range(HG_NLEV):
        blk = L >> lev
        half = blk // 2
        for t in range(L):
            p, o = divmod(t, blk)
            bd = p * blk + half - 1
            if o >= half:
                w[lev, t, bd + 1:t + 1] = 1.0
            else:
                w[lev, t, t + 1:bd + 1] = 1.0
        jj, ii = np.meshgrid(np.arange(L), np.arange(L), indexing='ij')
        mask[lev] = ((jj // blk == ii // blk) & (jj % blk >= half) & (ii % blk < half)).astype(np.float32)
    mask[HG_NLEV] = np.eye(L, dtype=np.float32)
    for t in range(L):
        w[HG_NLEV, t, :t + 1] = 1.0
        w[HG_NLEV + 1, t, t + 1:] = 1.0
    out = []
    for wd, md in ((w, mask), (w[:, ::-1, ::-1], mask[:, ::-1, ::-1])):
        wflat = wd.reshape((HG_NLEV + 2) * L, L)
        out.append((np.concatenate([wflat] * 3, axis=1), np.tile(md, (1, 1, HG_HEADS))))
    wcat = np.stack([out[0][0], out[1][0]])
    mask4 = np.stack([out[0][1], out[1][1]])
    hm = np.kron(np.eye(HG_HEADS, dtype=np.float32), np.ones((HG_DK, HG_DK), np.float32))
    return wcat, mask4, hm


def _hg_pos(i):
    is_ctx = i < HG_NB_CTX
    c = jnp.where(is_ctx, i % HG_NC_CTX, (i - HG_NB_CTX) % HG_NC_LAT)
    nc = jnp.where(is_ctx, HG_NC_CTX, HG_NC_LAT)
    return is_ctx, c, nc


def _hg_bwd_blk(i):
    _, c, nc = _hg_pos(i)
    return i + nc - 1 - 2 * c


def _hg_local(items, lb_ref, wcat_ref, mask_ref, hm):
    L = HG_L
    hmb = hm.astype(BF16)

    def bd4(x):
        return jnp.concatenate([x] * HG_HEADS, axis=0) * hmb

    kks, exs = [], []
    for d, q, z, v in items:
        lbp = lb_ref[d]
        sp = jnp.log1p(jnp.exp(-jnp.abs(z)))
        ls = jnp.minimum(z, 0.0) - sp
        kks.append(lbp[2:3] * jnp.exp(jnp.minimum(-z, 0.0) - sp))
        a = lbp[0:1]
        b = lbp[1:2] + ls
        logf = jnp.maximum(a, b) + jnp.log1p(jnp.exp(-jnp.abs(a - b)))
        p0 = logf.astype(BF16)
        r0 = logf - p0.astype(F32)
        p1 = r0.astype(BF16)
        p2 = (r0 - p1.astype(F32)).astype(BF16)
        exs.append(jnp.exp(_dot(wcat_ref[d], jnp.concatenate([p0, p1, p2], axis=0))))
    scs = [None] * len(items)
    for lev in range(HG_NLEV + 1):
        for n, (d, q, z, v) in enumerate(items):
            if lev < HG_NLEV:
                al = exs[n][L * lev:L * (lev + 1)]
                lhs = (q * al).astype(BF16)
                rhs = (kks[n] * al).astype(BF16)
            else:
                lhs = q.astype(BF16)
                rhs = kks[n].astype(BF16)
            t = _dot_nt(lhs, bd4(rhs)) * mask_ref[d, lev]
            scs[n] = t if scs[n] is None else scs[n] + t
    out = []
    for n, (d, q, z, v) in enumerate(items):
        vb = v.astype(BF16)
        o = _dot(scs[n].astype(BF16), bd4(vb))
        eq = exs[n][L * HG_NLEV:L * (HG_NLEV + 1)]
        qe = (q * eq).astype(BF16)
        ke = (kks[n] * exs[n][L * (HG_NLEV + 1):L * (HG_NLEV + 2)]).astype(BF16)
        g = eq[L - 1:L] if d == 0 else eq[0:1]
        out.append((o, qe, g, _dot_tn(vb, ke) * hm))
    return out


def _hgrn_kernel(qf, zf, vf, qb, zb, vb, lb_ref, w_ref, mask_ref, hm_ref, s0_ref, wup_ref, wdn_ref,
                 of_ref, ob_ref, sfin_ref, wupb_ref, wdnb_ref, st_ref):
    is_ctx, c, nc = _hg_pos(pl.program_id(0))
    wupb_ref[...] = wup_ref[...].astype(BF16)
    wdnb_ref[...] = wdn_ref[...].astype(BF16)

    @pl.when(c == 0)
    def _():
        st_ref[...] = jnp.where(is_ctx, 0.0, s0_ref[0])

    hm = hm_ref[...]
    rows = [slice(HG_L * k, HG_L * (k + 1)) for k in range(HG_CPS)]
    items = [(d, q_ref[r, :], z_ref[r, :], v_ref[r, :])
             for d, (q_ref, z_ref, v_ref) in enumerate(((qf, zf, vf), (qb, zb, vb))) for r in rows]
    loc = _hg_local(items, lb_ref, w_ref, mask_ref, hm)
    for d, o_ref in enumerate((of_ref, ob_ref)):
        st = st_ref[d]
        for k in (range(HG_CPS) if d == 0 else reversed(range(HG_CPS))):
            o, qe, g, upd = loc[HG_CPS * d + k]
            o_ref[rows[k], :] = o + _dot_nt(qe, st.astype(BF16))
            st = st * g + upd
        st_ref[d] = st

    @pl.when(jnp.logical_and(is_ctx, c == nc - 1))
    def _():
        for d in range(2):
            s_kv = st_ref[d].T
            for h in range(HG_HEADS):
                sfin_ref[0, d, h] = s_kv[HG_DK * h:HG_DK * (h + 1), HG_DK * h:HG_DK * (h + 1)]


def _hgrn(proj, lbp, consts, s0, w_up, w_down, l):
    wcat, mask4, hm = consts
    steps = N_TOK // HG_BLK
    up_rows = D_MODEL // steps
    dn_rows = 2 * D_FF // steps

    def fwd(col):
        return pl.BlockSpec((HG_BLK, HG_W), lambda i: (i, col))

    def bwd(col):
        return pl.BlockSpec((HG_BLK, HG_W), lambda i: (_hg_bwd_blk(i), col))

    return pl.pallas_call(
        _hgrn_kernel,
        grid=(N_TOK // HG_BLK,),
        in_specs=[fwd(COL_HQ), fwd(COL_HF_FWD), fwd(COL_HI), bwd(COL_HQ), bwd(COL_HF_BWD), bwd(COL_HI),
                  _layer_spec(lbp.shape[1:], l), _const_spec(wcat.shape), _const_spec(mask4.shape),
                  _const_spec(hm.shape),
                  pl.BlockSpec((None, 1, 2, HG_W, HG_W),
                               lambda i: (l, jnp.maximum(i - HG_NB_CTX, 0) // HG_NC_LAT, 0, 0, 0)),
                  pl.BlockSpec((None, up_rows, 2 * D_FF), lambda i: (l, i, 0)),
                  pl.BlockSpec((None, dn_rows, D_MODEL), lambda i: (l, i // 2, 0))],
        out_specs=(pl.BlockSpec((HG_BLK, HG_W), lambda i: (i, 0)),
                   pl.BlockSpec((HG_BLK, HG_W), lambda i: (_hg_bwd_blk(i), 0)),
                   pl.BlockSpec((1, 2, HG_HEADS, HG_DK, HG_DK),
                                lambda i: (jnp.minimum(i // HG_NC_CTX, BATCH - 1), 0, 0, 0, 0)),
                   pl.BlockSpec((up_rows, 2 * D_FF), lambda i: (i, 0)),
                   pl.BlockSpec((dn_rows, D_MODEL), lambda i: (i // 2, 0))),
        out_shape=(jax.ShapeDtypeStruct((N_TOK, HG_W), F32),
                   jax.ShapeDtypeStruct((N_TOK, HG_W), F32),
                   jax.ShapeDtypeStruct((BATCH, 2, HG_HEADS, HG_DK, HG_DK), F32),
                   jax.ShapeDtypeStruct((D_MODEL, 2 * D_FF), BF16),
                   jax.ShapeDtypeStruct((D_FF, D_MODEL), BF16)),
        scratch_shapes=[pltpu.VMEM((2, HG_W, HG_W), F32)],
        compiler_params=pltpu.CompilerParams(vmem_limit_bytes=VMEM_LIMIT),
        name="hgrn_mixer",
    )(proj, proj, proj, proj, proj, proj, lbp, wcat, mask4, hm, s0, w_up, w_down)


FN_CTX_SPG = 4
FN_LAT_TR = 512


def _dft_consts(t_len):
    n = GROUP_W // 4
    k = np.arange(n)
    ang = 2.0 * np.pi * ((k[:, None] * k[None, :]) % n) / n
    eye = np.eye(4)
    cs = np.concatenate([np.kron(eye, np.cos(ang)), np.kron(eye, np.sin(ang))], axis=1) / math.sqrt(n)
    t = np.arange(t_len)
    angt = 2.0 * np.pi * ((t[:, None] * t[None, :]) % t_len) / t_len
    dft = np.concatenate([np.cos(angt), -np.sin(angt)], axis=1) / math.sqrt(t_len)
    return cs.astype(np.float32), dft.astype(np.float32)


def _fnet_kernel(x_ref, cs_ref, dft_ref, w_ref, o_ref, r_ref, *, t_len, spg, tr):
    @pl.when(pl.program_id(1) == 0)
    def _():
        t = _dot(x_ref[...].astype(BF16), cs_ref[...])
        for s in range(spg):
            rows = slice(t_len * s, t_len * (s + 1))
            r_ref[s, 0:t_len, :] = t[rows, :GROUP_W].astype(BF16)
            r_ref[s, t_len:2 * t_len, :] = t[rows, GROUP_W:].astype(BF16)

    w = w_ref[...].astype(BF16)
    for s in range(spg):
        y = _dot(dft_ref[...], r_ref[s])
        o_ref[tr * s:tr * (s + 1), :] = _dot(y.astype(BF16), w)


def _fnet(proj, cs, dft, fn_w, l, *, n_seq, seq_len, tok0, spg, tr):
    nj = seq_len // tr
    assert spg == 1 or nj == 1
    gb0 = tok0 // (spg * seq_len)
    return pl.pallas_call(
        functools.partial(_fnet_kernel, t_len=seq_len, spg=spg, tr=tr),
        grid=(n_seq // spg, nj),
        in_specs=[
            pl.BlockSpec((spg * seq_len, GROUP_W), lambda g, j: (gb0 + g, COL_XC)),
            _const_spec((GROUP_W, 2 * GROUP_W)),
            pl.BlockSpec((tr, 2 * seq_len), lambda g, j: (j, 0)),
            _layer_spec((GROUP_W, GROUP_W), l),
        ],
        out_specs=pl.BlockSpec((spg * tr, GROUP_W), lambda g, j: (g * nj + j, 0)),
        out_shape=jax.ShapeDtypeStruct((n_seq * seq_len, GROUP_W), F32),
        scratch_shapes=[pltpu.VMEM((spg, 2 * seq_len, GROUP_W), BF16)],
        compiler_params=pltpu.CompilerParams(vmem_limit_bytes=VMEM_LIMIT),
        name="fnet_mixer",
    )(proj, cs, dft, fn_w)


def _outproj_kernel(*refs):
    (xa_ref, hgate_ref, gu_ref, gv_ref, ylo_ref, yhi_ref, of_ref, ob_ref, ycc_ref, ycl_ref, mod_ref, d_ref,
     wglu_ref, gn_ref, bdm_ref, gmn_ref, gmw_ref, gmb_ref, gmh_ref, wout_ref, n2_ref,
     x1_ref, h2_ref, wob_ref) = refs[-24:]
    x_refs = refs[:-24]

    @pl.when(pl.program_id(0) == 0)
    def _():
        wob_ref[...] = wout_ref[...].astype(BF16)

    gn = gn_ref[...]
    ys = jnp.concatenate([ylo_ref[...], yhi_ref[...]], axis=1)
    y5 = _gelu(ys + d_ref[...] * xa_ref[...])
    glu = jax.nn.sigmoid(_dot(y5.astype(BF16), wglu_ref[...].astype(BF16)))
    out_a = _rms(y5 * glu, gn[:, 0:GROUP_W])
    o = of_ref[...] + ob_ref[...]
    o2 = o * o
    o2h = o2.astype(BF16)
    o2l = (o2 - o2h.astype(F32)).astype(BF16)
    ms = _dot(o2h, bdm_ref[...]) + _dot(o2l, bdm_ref[...])
    hg = hgate_ref[...]
    out_b = o * lax.rsqrt(ms + EPS) * gn[:, GROUP_W:2 * GROUP_W] * (hg * jax.nn.sigmoid(hg))
    out_c = _rms(_x_value((ycc_ref, ycl_ref), TBP), gn[:, 2 * GROUP_W:3 * GROUP_W])
    gm_hm = gmh_ref[...]
    out_d = []
    for c in range(TBP // GM_CHUNK):
        rows = slice(GM_CHUNK * c, GM_CHUNK * (c + 1))
        gv = _rms(_gelu(gv_ref[rows, :]), gmn_ref[...]).astype(BF16)
        g4 = jnp.concatenate([gv] * GM_HEADS, axis=0) * gm_hm
        sp = _dot(gmw_ref[...], g4) + gmb_ref[...]
        out_d.append(_rms(_gelu(gu_ref[rows, :]) * sp, gn[:, 3 * GROUP_W:]))
    out_d = jnp.concatenate(out_d, axis=0)
    m = None
    for k, part in enumerate((out_a, out_b, out_c, out_d)):
        t = _dot(part.astype(BF16), wob_ref[GROUP_W * k:GROUP_W * (k + 1), :])
        m = t if m is None else m + t
    mod = mod_ref[...]
    g1 = mod[:, 2 * D_MODEL:3 * D_MODEL]
    sh2 = mod[:, 3 * D_MODEL:4 * D_MODEL]
    sc2 = mod[:, 4 * D_MODEL:5 * D_MODEL]
    x1 = _x_value(x_refs, TBP) + g1 * m
    x1_ref[...] = x1
    h2_ref[...] = (_rms(x1, n2_ref[...]) * (1.0 + sc2) + sh2).astype(BF16)


def _outproj(xs, proj, ylo, yhi, of, ob, yc, mods, s5_d, wglu, gn, bdm, gm_norm_g, gm_w, gm_b, gm_hm, wout,
             norm2_g, l):
    def tok(width, col=0):
        return pl.BlockSpec((TBP, width), lambda i: (i, col))

    return pl.pallas_call(
        _outproj_kernel,
        grid=(N_TOK // TBP,),
        in_specs=_x_specs(len(xs) == 2, tb=TBP) + [
            tok(GROUP_W, COL_XA), tok(GROUP_W, COL_HGATE), tok(GROUP_W, COL_GU), tok(GROUP_W, COL_GV),
            tok(LANES), tok(LANES), tok(GROUP_W), tok(GROUP_W), *_x_specs(True, GROUP_W, TBP),
            _mod_spec(l, TBP),
            _layer_spec((1, GROUP_W), l), _layer_spec((GROUP_W, GROUP_W), l), _layer_spec((1, D_MODEL), l),
            _const_spec((GROUP_W, GROUP_W)),
            _layer_spec((1, GROUP_W), l), _layer_spec((GM_CHUNK, GM_HEADS * GM_CHUNK), l),
            _layer_spec((GM_CHUNK, GROUP_W), l), _const_spec((GM_HEADS * GM_CHUNK, GROUP_W)),
            _layer_spec((D_MODEL, D_MODEL), l, single=True), _layer_spec((1, D_MODEL), l),
        ],
        out_specs=(tok(D_MODEL), tok(D_MODEL)),
        out_shape=(jax.ShapeDtypeStruct((N_TOK, D_MODEL), F32),
                   jax.ShapeDtypeStruct((N_TOK, D_MODEL), BF16)),
        scratch_shapes=[pltpu.VMEM((D_MODEL, D_MODEL), BF16)],
        compiler_params=pltpu.CompilerParams(vmem_limit_bytes=VMEM_LIMIT),
        name="out_proj",
    )(*xs, proj, proj, proj, proj, ylo, yhi, of, ob, *yc, mods, s5_d, wglu, gn, bdm, gm_norm_g, gm_w, gm_b,
      gm_hm, wout, norm2_g)


FF_SEG = GRID_W
FF_NSEG = TB // FF_SEG
FF_SEQ_STRIPS = SEQ // FF_SEG
FF_PERM_ROWS = FF_SEQ_STRIPS * FF_SEG
SUBLANES = 8
FF_NGRP = FF_SEG // SUBLANES
GELU_C0 = 0.7978845608028654
GELU_C1 = GELU_C0 * 0.044715


def _ffn_perm_consts():
    p = np.zeros((FF_PERM_ROWS, FF_PERM_ROWS), np.float32)
    for k in range(FF_SEQ_STRIPS):
        for t in range(FF_SEG):
            p[FF_SEG * k + SUBLANES * (t % FF_NGRP) + t // FF_NGRP, FF_SEG * k + t] = 1.0
    return p


def _ffn_kernel(x1_ref, h2_ref, mod_ref, perm_ref, wup_ref, cw_ref, cb_ref, wdn_ref, fg_ref, *rest, final):
    *o_refs, h2p_ref, z_ref, hid_ref = rest
    i = pl.program_id(0)
    joined = (i < NB_CTX).astype(F32)
    n_tiles = D_FF // FF_TILE
    sub = lax.broadcasted_iota(jnp.int32, (SUBLANES, 2 * FF_TILE), 0)
    is_first = sub == 0
    is_last = sub == SUBLANES - 1
    zero_grp = jnp.zeros((SUBLANES, 2 * FF_TILE), F32)
    for r in range(0, TB, FF_PERM_ROWS):
        h2p_ref[r:r + FF_PERM_ROWS, :] = _dot(perm_ref[...], h2_ref[r:r + FF_PERM_ROWS, :]).astype(BF16)

    def up(j):
        lo = FF_TILE * j
        for c, col in enumerate((lo, D_FF + lo)):
            z_ref[j % 2, :, FF_TILE * c:FF_TILE * (c + 1)] = _dot(h2p_ref[...], wup_ref[:, col:col + FF_TILE])

    def gate(j):
        slot = j % 2
        lo = FF_TILE * j
        w = jnp.concatenate([cw_ref[:, lo:lo + FF_TILE], cw_ref[:, D_FF + lo:D_FF + lo + FF_TILE]], axis=1)
        b = jnp.concatenate([cb_ref[:, lo:lo + FF_TILE], cb_ref[:, D_FF + lo:D_FF + lo + FF_TILE]], axis=1)

        def grp(k, v):
            r = FF_SEG * k + SUBLANES * v
            return z_ref[slot, r:r + SUBLANES, :]

        down = [pltpu.roll(grp(k, FF_NGRP - 1), 1, 0) for k in range(FF_NSEG)]
        up_ = [pltpu.roll(grp(k, 0), SUBLANES - 1, 0) for k in range(FF_NSEG)]
        strips = []
        for k in range(FF_NSEG):
            g = [grp(k, v) for v in range(FF_NGRP)]
            before = down[k - 1] * joined if k % FF_SEQ_STRIPS > 0 else zero_grp
            after = up_[k + 1] * joined if k % FF_SEQ_STRIPS < FF_SEQ_STRIPS - 1 else zero_grp
            zm1 = jnp.concatenate([jnp.where(is_first, before, down[k])] + g[:-1], axis=0)
            zp1 = jnp.concatenate(g[1:] + [jnp.where(is_last, after, up_[k])], axis=0)
            zc = b + zm1 * w[0:1] + jnp.concatenate(g, axis=0) * w[1:2] + zp1 * w[2:3]
            a = zc[:, :FF_TILE]
            t = jnp.tanh(a * (GELU_C0 + GELU_C1 * (a * a)))
            strips.append((a * (0.5 + 0.5 * t) * zc[:, FF_TILE:]).astype(BF16))
        hid_ref[:, lo:lo + FF_TILE] = jnp.concatenate(strips, axis=0)

    up(0)
    for j in range(n_tiles):
        if j + 1 < n_tiles:
            up(j + 1)
        gate(j)
    acc = _dot(hid_ref[...], wdn_ref[...])
    acc = jnp.swapaxes(acc.reshape(FF_NSEG, FF_NGRP, SUBLANES, D_MODEL), 1, 2).reshape(TB, D_MODEL)
    g2 = mod_ref[...][:, 5 * D_MODEL:6 * D_MODEL]
    x2 = x1_ref[...] + g2 * acc
    if final:
        y = _rms(x2, fg_ref[...])
        yp_ref, ys_ref = o_refs

        @pl.when(i < NB_CTX)
        def _():
            yp_ref[...] = y

        @pl.when(i >= NB_CTX)
        def _():
            ys_ref[...] = y
    else:
        o_refs[0][...] = x2


def _ffn(x1, h2, mods, perm, wup_bf, conv_w, conv_b, wdn_bf, final_g, l, *, final):
    if final:
        out_specs = tuple(_x_specs(True))
        out_shape = (jax.ShapeDtypeStruct((N_CTX, D_MODEL), F32), jax.ShapeDtypeStruct((N_LAT, D_MODEL), F32))
    else:
        out_specs = pl.BlockSpec((TB, D_MODEL), lambda i: (i, 0))
        out_shape = jax.ShapeDtypeStruct((N_TOK, D_MODEL), F32)
    return pl.pallas_call(
        functools.partial(_ffn_kernel, final=final),
        grid=(N_TOK // TB,),
        in_specs=[
            pl.BlockSpec((TB, D_MODEL), lambda i: (i, 0)),
            pl.BlockSpec((TB, D_MODEL), lambda i: (i, 0)),
            _mod_spec(l), _const_spec((FF_PERM_ROWS, FF_PERM_ROWS)),
            _const_spec((D_MODEL, 2 * D_FF), single=True), _layer_spec((3, 2 * D_FF), l),
            _layer_spec((1, 2 * D_FF), l), _const_spec((D_FF, D_MODEL), single=True),
            _const_spec((1, D_MODEL)),
        ],
        out_specs=out_specs,
        out_shape=out_shape,
        scratch_shapes=[pltpu.VMEM((TB, D_MODEL), BF16), pltpu.VMEM((2, TB, 2 * FF_TILE), F32),
                        pltpu.VMEM((TB, D_FF), BF16)],
        compiler_params=pltpu.CompilerParams(vmem_limit_bytes=VMEM_LIMIT),
        name="conv_ffn",
    )(x1, h2, mods, perm, wup_bf, conv_w, conv_b, wdn_bf, final_g.reshape(1, D_MODEL))


def kernel(x_prompt, x_sample, state_s5_re, state_s5_im, state_hgrn, c, c_ctx, w_ada, b_ada, norm1_g,
           norm2_g, w_in, s5_lam_re, s5_lam_im, s5_log_dt, s5_b_re, s5_b_im, s5_c_re, s5_c_im, s5_d,
           s5_w_glu, hg_lb_logits, fn_w, gm_norm_g, gm_ws, gm_bs, grp_norm_g, w_out, ffn_w_up,
           ffn_conv_w, ffn_conv_b, ffn_w_down, final_norm_g):
    xs = (x_prompt.reshape(N_CTX, D_MODEL), x_sample.reshape(N_LAT, D_MODEL))
    cvecs = jnp.concatenate([c_ctx[None], c, jnp.zeros((8 - 1 - DEC_BATCH, D_MODEL), F32)], axis=0)

    lb_p = jax.nn.softmax(hg_lb_logits.astype(F32), axis=0)
    lbs = jnp.maximum(jnp.cumsum(lb_p, axis=0) - lb_p[0], 0.0)
    lbps = jnp.stack([jnp.log(lbs), jnp.log1p(-lbs), 1.0 - lbs], axis=2)

    hg_consts_np = _hgrn_consts()
    hg_consts = (jnp.asarray(hg_consts_np[0], BF16), jnp.asarray(hg_consts_np[1], F32),
                 jnp.asarray(hg_consts_np[2], F32))
    cs_np, dft_ctx_np = _dft_consts(SEQ)
    _, dft_lat_np = _dft_consts(DEC_SEQ)
    cs = jnp.asarray(cs_np, F32).astype(BF16)
    dft_ctx = jnp.asarray(dft_ctx_np, F32).astype(BF16)
    dft_lat = jnp.asarray(dft_lat_np, F32).astype(BF16)
    gm_hm = jnp.asarray(np.kron(np.eye(GM_HEADS), np.ones((GM_CHUNK, GROUP_W // GM_HEADS))), BF16)
    bdm = jnp.asarray(np.kron(np.eye(HG_HEADS), np.ones((HG_DK, HG_DK))) / HG_DK, BF16)

    tables = _s5_tables(s5_lam_re, s5_lam_im, s5_log_dt, s5_b_re, s5_b_im, s5_c_re, s5_c_im)
    mods, s5_m, s5_f, s5_e = _prep(cvecs, w_ada, b_ada, tables, jnp.asarray(_s5_shift_consts(), BF16))
    mods = mods.reshape(DEPTH, 8, 1, N_MOD * D_MODEL)
    s5_al = tables[-1]
    s5_h0 = jnp.stack([state_s5_re, state_s5_im]).astype(F32).transpose(2, 3, 0, 1, 4, 5)
    s5_h0 = s5_h0.reshape(DEPTH, 2, 2, DEC_BATCH, S5_SW)
    hg_s0 = jnp.einsum('bldhkv,hg->lbdhvgk', state_hgrn.astype(F32), jnp.eye(HG_HEADS, dtype=F32))
    hg_s0 = hg_s0.reshape(DEPTH, DEC_BATCH, 2, HG_W, HG_W)

    def rows(a):
        return a.reshape(DEPTH, 1, a.shape[-1])

    norm1_r, norm2_r, gn_r, s5_d_r, gmn_r, cb_r = (rows(a) for a in (
        norm1_g, norm2_g, grp_norm_g, s5_d, gm_norm_g, ffn_conv_b))
    gm_w = gm_ws.transpose(0, 2, 1, 3).reshape(DEPTH, GM_CHUNK, GM_HEADS * GM_CHUNK).astype(BF16)
    gm_b = jnp.repeat(gm_bs.transpose(0, 2, 1), GROUP_W // GM_HEADS, axis=2)
    ff_perm = jnp.asarray(_ffn_perm_consts(), BF16)

    new_re, new_im, new_hg = [], [], []
    for l in range(DEPTH):
        proj = _inproj(xs, mods, norm1_r, w_in, l)

        ylo, yhi, hfin = _s5(proj, s5_m, s5_f, s5_e, s5_al, s5_h0, l)
        hfin = hfin.reshape(2, 2, BATCH, S5_G, S5_N).transpose(1, 2, 0, 3, 4)
        new_re.append(hfin[0])
        new_im.append(hfin[1])

        of, ob, sfin, wup_bf, wdn_bf = _hgrn(proj, lbps, hg_consts, hg_s0, ffn_w_up, ffn_w_down, l)
        new_hg.append(sfin)

        yc = (_fnet(proj, cs, dft_ctx, fn_w, l, n_seq=BATCH, seq_len=SEQ, tok0=0, spg=FN_CTX_SPG, tr=SEQ),
              _fnet(proj, cs, dft_lat, fn_w, l, n_seq=DEC_BATCH, seq_len=DEC_SEQ, tok0=N_CTX, spg=1,
                    tr=FN_LAT_TR))

        x1, h2 = _outproj(xs, proj, ylo, yhi, of, ob, yc, mods, s5_d_r, s5_w_glu, gn_r, bdm, gmn_r, gm_w, gm_b,
                          gm_hm, w_out, norm2_r, l)
        res = _ffn(x1, h2, mods, ff_perm, wup_bf, ffn_conv_w, cb_r, wdn_bf, final_norm_g, l,
                   final=(l == DEPTH - 1))
        xs = (res,)

    y_prompt = res[0].reshape(BATCH, SEQ, D_MODEL)
    y_sample = res[1].reshape(DEC_BATCH, DEC_SEQ, D_MODEL)
    return (y_prompt, y_sample, jnp.stack(new_re, axis=1), jnp.stack(new_im, axis=1),
            jnp.stack(new_hg, axis=1))
```

```python
import functools
import math

import numpy as np
import jax
import jax.numpy as jnp
from jax import lax
from jax.experimental import pallas as pl
from jax.experimental.pallas import tpu as pltpu

D_MODEL = 1024
BATCH = 16
SEQ = 256
DEPTH = 2
DEC_BATCH = 2
DEC_SEQ = 2048
GRID_W = 64
GROUP_W = 256
S5_P = 16
S5_G = 16
S5_N = 64
HG_HEADS = 4
HG_DK = 64
GM_HEADS = 4
GM_CHUNK = 128
D_FF = 2816
N_MOD = 6
D_IN = 9 * GROUP_W
EPS = 1e-6
LAM_RE_MAX = -1e-4

N_CTX = BATCH * SEQ
N_LAT = DEC_BATCH * DEC_SEQ
N_TOK = N_CTX + N_LAT
TB = 512
TBP = 512
NB_CTX = N_CTX // TB
S5_L = 16
HG_L = 64
FF_TILE = 256
LANES = 128
VMEM_LIMIT = 56 * 1024 * 1024

F32 = jnp.float32
BF16 = jnp.bfloat16

COL_XA, COL_HQ, COL_HF_FWD, COL_HF_BWD, COL_HI, COL_HGATE, COL_XC, COL_GU, COL_GV = range(9)


def _mod_row(i, tb):
    return jnp.where(i < N_CTX // tb, 0, 1 + (i - N_CTX // tb) // (DEC_SEQ // tb))


def _gelu(x):
    return 0.5 * x * (1.0 + jnp.tanh(0.7978845608028654 * (x + 0.044715 * (x * x * x))))


def _rms(x, g):
    return x * lax.rsqrt(jnp.mean(x * x, axis=-1, keepdims=True) + EPS) * g


def _dot(a, b):
    return jnp.dot(a, b, preferred_element_type=F32)


def _dot_nt(a, b, precision=None):
    return lax.dot_general(a, b, (((1,), (1,)), ((), ())), precision=precision, preferred_element_type=F32)


def _dot_tn(a, b):
    return lax.dot_general(a, b, (((0,), (0,)), ((), ())), preferred_element_type=F32)


def _const_spec(shape, single=False):
    kw = {"pipeline_mode": pl.Buffered(1)} if single else {}
    return pl.BlockSpec(shape, lambda *_: (0,) * len(shape), **kw)


def _layer_spec(shape, l, single=False):
    kw = {"pipeline_mode": pl.Buffered(1)} if single else {}
    return pl.BlockSpec((None,) + tuple(shape), lambda *_: (l,) + (0,) * len(shape), **kw)


def _mod_spec(l, tb=TB):
    return pl.BlockSpec((None, None, 1, N_MOD * D_MODEL), lambda i: (l, _mod_row(i, tb), 0, 0))


def _x_specs(split, width=D_MODEL, tb=TB):
    nb_ctx = N_CTX // tb
    if split:
        return [pl.BlockSpec((tb, width), lambda i: (jnp.minimum(i, nb_ctx - 1), 0)),
                pl.BlockSpec((tb, width), lambda i: (jnp.maximum(i - nb_ctx, 0), 0))]
    return [pl.BlockSpec((tb, width), lambda i: (i, 0))]


def _x_value(x_refs, tb=TB):
    if len(x_refs) == 1:
        return x_refs[0][...]
    return jnp.where(pl.program_id(0) < N_CTX // tb, x_refs[0][...], x_refs[1][...])


def _inproj_kernel(*refs):
    *x_refs, mod_ref, g_ref, w_ref, o_ref, wb_ref = refs

    @pl.when(pl.program_id(0) == 0)
    def _():
        wb_ref[...] = w_ref[...].astype(BF16)

    mod = mod_ref[...]
    sh = mod[:, 0:D_MODEL]
    sc = mod[:, D_MODEL:2 * D_MODEL]
    h = _rms(_x_value(x_refs, TBP), g_ref[...]) * (1.0 + sc) + sh
    o_ref[...] = _dot(h.astype(BF16), wb_ref[...])


def _inproj(xs, mods, norm_g, w_in, l):
    return pl.pallas_call(
        _inproj_kernel,
        grid=(N_TOK // TBP,),
        in_specs=_x_specs(len(xs) == 2, tb=TBP) + [
            _mod_spec(l, TBP),
            _layer_spec((1, D_MODEL), l),
            _layer_spec((D_MODEL, D_IN), l, single=True),
        ],
        out_specs=pl.BlockSpec((TBP, D_IN), lambda i: (i, 0)),
        out_shape=jax.ShapeDtypeStruct((N_TOK, D_IN), F32),
        scratch_shapes=[pltpu.VMEM((D_MODEL, D_IN), BF16)],
        compiler_params=pltpu.CompilerParams(vmem_limit_bytes=VMEM_LIMIT),
        name="in_proj",
    )(*xs, mods, norm_g, w_in)


S5_NPAIR = S5_G // 2
S5_CW = S5_L * S5_P
S5_SW = S5_G * S5_N
S5_STEP_ROWS = N_CTX // S5_L
S5_CTX_NC = SEQ // S5_L
S5_LAT_NC = DEC_SEQ // S5_L


def _s5_slot(g, t):
    return (t + g) % S5_L


def _s5_tables(lam_re, lam_im, log_dt, b_re, b_im, c_re, c_im):
    lr = jnp.minimum(lam_re.astype(F32), LAM_RE_MAX)
    li = lam_im.astype(F32)
    dt = jnp.exp(log_dt.astype(F32))[..., None]
    mag = jnp.exp(lr * dt)
    ang = li * dt
    ab_re = mag * jnp.cos(ang)
    ab_im = mag * jnp.sin(ang)
    den = lr * lr + li * li
    xr = ab_re - 1.0
    z_re = (xr * lr + ab_im * li) / den
    z_im = (ab_im * lr - xr * li) / den
    bb_re = z_re[..., None] * b_re - z_im[..., None] * b_im
    bb_im = z_re[..., None] * b_im + z_im[..., None] * b_re
    tau = jnp.arange(S5_L + 1, dtype=F32)[:, None, None, None, None]
    pm = jnp.exp(lr * dt * tau)
    pa = li * dt * tau
    pw_re = pm * jnp.cos(pa)
    pw_im = pm * jnp.sin(pa)
    eye2 = jnp.eye(2, dtype=F32)

    def pw_pairs(a):
        a = a.transpose(1, 2, 3, 0, 4).reshape(DEPTH, 2, S5_NPAIR, 2, S5_L + 1, S5_N)
        return a.transpose(0, 1, 2, 4, 3, 5).reshape(DEPTH, 2, S5_NPAIR, S5_L + 1, 2 * S5_N)

    def mat_pairs(a):
        a = a.reshape(DEPTH, 2, S5_NPAIR, 2, S5_P, S5_N)
        return jnp.einsum('ldjaqn,ab->ldjaqbn', a, eye2).reshape(DEPTH, 2, S5_NPAIR, 2, S5_P, 2 * S5_N)

    al = jnp.stack([pw_re[S5_L], pw_im[S5_L]], axis=2).reshape(DEPTH, 2, 2, 1, S5_SW)
    return (pw_pairs(pw_re), pw_pairs(pw_im),
            mat_pairs(bb_re.transpose(0, 1, 2, 4, 3)), mat_pairs(bb_im.transpose(0, 1, 2, 4, 3)),
            mat_pairs(c_re.astype(F32)), mat_pairs(c_im.astype(F32)), al)


def _s5_shift_consts():
    n = S5_CW
    r = np.arange(n)[:, None]
    c = np.arange(n)[None, :]
    fwd = [(c == r + S5_P * s) for s in range(S5_L)]
    bwd = [(c == r - S5_P * (S5_L - 1 - s)) for s in range(S5_L)]
    return np.stack([np.concatenate(fwd, axis=1), np.concatenate(bwd, axis=1)]).astype(np.float32)


PREP_STEPS = 4
ADA_TN = N_MOD * D_MODEL // PREP_STEPS


def _prep_kernel(c_ref, wada_ref, bada_ref, pwr_ref, pwi_ref, br_ref, bi_ref, cr_ref, ci_ref, scat_ref,
                 mod_ref, m_ref, f_ref, e_ref, k_ref):
    cv = c_ref[...]
    mod_ref[0] = _dot((cv * jax.nn.sigmoid(cv)).astype(BF16), wada_ref[0].astype(BF16)) + bada_ref[0]
    step = pl.program_id(1)
    for qq in range(PREP_STEPS):
        pl.when(step == qq)(functools.partial(_s5_pair_ops, qq, pwr_ref, pwi_ref, br_ref, bi_ref, cr_ref,
                                              ci_ref, f_ref, e_ref, k_ref))
    pl.when(step == PREP_STEPS - 1)(functools.partial(_s5_toeplitz, scat_ref, m_ref, k_ref))


def _s5_pair_ops(qq, pwr_ref, pwi_ref, br_ref, bi_ref, cr_ref, ci_ref, f_ref, e_ref, k_ref):
    L = S5_L
    hp = lax.Precision.HIGHEST
    sw = 2 * S5_N
    per_step = S5_NPAIR // PREP_STEPS
    for j in range(per_step * qq, per_step * (qq + 1)):
        for d in range(2):
            pwr = pwr_ref[0, d, j]
            pwi = pwi_ref[0, d, j]
            for gl in range(2):
                g = 2 * j + gl
                br, bi = br_ref[0, d, j, gl], bi_ref[0, d, j, gl]
                cr, ci = cr_ref[0, d, j, gl], ci_ref[0, d, j, gl]
                ca_r, ca_i, f_r, f_i, e_r, e_i = [], [], [], [], [], []
                for t in range(L):
                    kk = t if d == 0 else L - 1 - t
                    pr, pi = pwr[kk:kk + 1], pwi[kk:kk + 1]
                    ca_r.append(cr * pr - ci * pi)
                    ca_i.append(cr * pi + ci * pr)
                    kf = L - 1 - t if d == 0 else t
                    pr, pi = pwr[kf:kf + 1], pwi[kf:kf + 1]
                    f_r.append(br * pr - bi * pi)
                    f_i.append(br * pi + bi * pr)
                    ke = t + 1 if d == 0 else L - t
                    pr, pi = pwr[ke:ke + 1], pwi[ke:ke + 1]
                    e_r.append(cr * pr - ci * pi)
                    e_i.append(-(cr * pi + ci * pr))
                cat = lambda xs: jnp.concatenate(xs, axis=0)
                k = _dot_nt(br, cat(ca_r), hp) - _dot_nt(bi, cat(ca_i), hp)
                k_ref[d, S5_P * g:S5_P * (g + 1), :] = k
                slots = lambda xs: cat([xs[(p - g) % L] for p in range(L)])
                rows = slice(S5_CW * gl, S5_CW * (gl + 1))
                f_ref[0, d, j, rows, 0:sw] = slots(f_r).astype(BF16)
                f_ref[0, d, j, rows, sw:2 * sw] = slots(f_i).astype(BF16)
                e_ref[0, d, 0, j, rows, :] = slots(e_r).astype(BF16)
                e_ref[0, d, 1, j, rows, :] = slots(e_i).astype(BF16)


def _s5_toeplitz(scat_ref, m_ref, k_ref):
    L = S5_L
    kf = k_ref[0].astype(BF16)
    kb = k_ref[1].astype(BF16)
    for s in range(L):
        cols = slice(S5_CW * s, S5_CW * (s + 1))
        res = _dot(kf, scat_ref[0, :, cols]) + _dot(kb, scat_ref[1, :, cols])
        for g in range(S5_G):
            blk = res[S5_P * g:S5_P * (g + 1), :]
            if g:
                blk = pltpu.roll(blk, S5_P * g, 1)
            r0 = S5_P * _s5_slot(g, s)
            m_ref[0, g, r0:r0 + S5_P, :] = blk.astype(BF16)


def _prep(cvecs, w_ada, b_ada, tables, scat):
    pwr, pwi, br, bi, cr, ci, _ = tables
    n = N_MOD * D_MODEL

    def lspec(shape):
        return pl.BlockSpec((1,) + shape, lambda l, q: (l,) + (0,) * len(shape))

    pw_shape = (2, S5_NPAIR, S5_L + 1, 2 * S5_N)
    mat_shape = (2, S5_NPAIR, 2, S5_P, 2 * S5_N)
    return pl.pallas_call(
        _prep_kernel,
        grid=(DEPTH, PREP_STEPS),
        in_specs=[pl.BlockSpec((8, D_MODEL), lambda l, q: (0, 0)),
                  pl.BlockSpec((1, D_MODEL, ADA_TN), lambda l, q: (l, 0, q)),
                  pl.BlockSpec((1, 1, ADA_TN), lambda l, q: (l, 0, q)),
                  lspec(pw_shape), lspec(pw_shape), lspec(mat_shape), lspec(mat_shape), lspec(mat_shape),
                  lspec(mat_shape), _const_spec(scat.shape)],
        out_specs=(pl.BlockSpec((1, 8, ADA_TN), lambda l, q: (l, 0, q)),
                   lspec((S5_G, S5_CW, S5_CW)), lspec((2, S5_NPAIR, 2 * S5_CW, 4 * S5_N)),
                   lspec((2, 2, S5_NPAIR, 2 * S5_CW, 2 * S5_N))),
        out_shape=(jax.ShapeDtypeStruct((DEPTH, 8, n), F32),
                   jax.ShapeDtypeStruct((DEPTH, S5_G, S5_CW, S5_CW), BF16),
                   jax.ShapeDtypeStruct((DEPTH, 2, S5_NPAIR, 2 * S5_CW, 4 * S5_N), BF16),
                   jax.ShapeDtypeStruct((DEPTH, 2, 2, S5_NPAIR, 2 * S5_CW, 2 * S5_N), BF16)),
        scratch_shapes=[pltpu.VMEM((2, S5_G * S5_P, S5_CW), F32)],
        compiler_params=pltpu.CompilerParams(vmem_limit_bytes=VMEM_LIMIT),
        name="param_prep",
    )(cvecs, w_ada, b_ada.reshape(DEPTH, 1, n), pwr, pwi, br, bi, cr, ci, scat)


def _s5_kernel(xlo_ref, xhi_ref, m_ref, f_ref, e_ref, al_ref, h0_ref, ylo_ref, yhi_ref, hfin_ref,
               uy_ref, zh_ref):
    R = S5_STEP_ROWS
    pw = 2 * S5_CW
    sw = 2 * S5_N
    gph = S5_G // 2
    step = pl.program_id(0)

    for half, x_ref in enumerate((xlo_ref, xhi_ref)):
        for t in range(S5_L):
            xt = x_ref[pl.ds(t, R, stride=S5_L), :]
            shift = (S5_P * t) % LANES
            xr = pltpu.roll(xt, shift, 1) if shift else xt
            for gl in range(gph):
                g = gph * half + gl
                src = (S5_P * gl + shift) % LANES
                lo = S5_CW * g + S5_P * _s5_slot(g, t)
                assert lo % LANES == src
                uy_ref[:, lo:lo + S5_P] = xr[:, src:src + S5_P]

    for j in range(S5_NPAIR):
        ub = uy_ref[:, pw * j:pw * (j + 1)].astype(BF16)
        for d in range(2):
            o = _dot(ub, f_ref[d, j])
            zh_ref[d, 0, :, sw * j:sw * (j + 1)] = o[:, :sw]
            zh_ref[d, 1, :, sw * j:sw * (j + 1)] = o[:, sw:]

    def scan_group(seq_rows, nc, init, fin_rows):
        chains = [(d, r0) for d in range(2) for r0 in seq_rows]

        def body(c, carry):
            out = []
            for k, (d, r0) in enumerate(chains):
                hr, hi = carry[2 * k], carry[2 * k + 1]
                r = r0 + (c if d == 0 else nc - 1 - c)
                zr = zh_ref[d, 0, pl.ds(r, 1), :]
                zi = zh_ref[d, 1, pl.ds(r, 1), :]
                zh_ref[d, 0, pl.ds(r, 1), :] = hr
                zh_ref[d, 1, pl.ds(r, 1), :] = hi
                ar = al_ref[d, 0]
                ai = al_ref[d, 1]
                out.append(ar * hr - ai * hi + zr)
                out.append(ar * hi + ai * hr + zi)
            return tuple(out)

        fin = lax.fori_loop(0, nc, body, tuple(init))
        if fin_rows is not None:
            for k, (d, _) in enumerate(chains):
                s = fin_rows[k % len(seq_rows)]
                hfin_ref[d, 0, s:s + 1, :] = fin[2 * k]
                hfin_ref[d, 1, s:s + 1, :] = fin[2 * k + 1]

    @pl.when(step == 0)
    def _():
        zero = jnp.zeros((1, S5_SW), F32)
        for s0 in range(0, BATCH, 4):
            seqs = list(range(s0, s0 + 4))
            scan_group([s * S5_CTX_NC for s in seqs], S5_CTX_NC, [zero] * 16, seqs)

    @pl.when(step == 1)
    def _():
        init = []
        for d in range(2):
            for s in range(DEC_BATCH):
                init.append(h0_ref[d, 0, s:s + 1, :])
                init.append(h0_ref[d, 1, s:s + 1, :])
        scan_group([s * S5_LAT_NC for s in range(DEC_BATCH)], S5_LAT_NC, init, None)

    for j in range(S5_NPAIR):
        acc = None
        for d in range(2):
            for ri in range(2):
                hb = zh_ref[d, ri, :, sw * j:sw * (j + 1)].astype(BF16)
                t = _dot_nt(hb, e_ref[d, ri, j])
                acc = t if acc is None else acc + t
        for gl in range(2):
            g = 2 * j + gl
            ug = uy_ref[:, S5_CW * g:S5_CW * (g + 1)].astype(BF16)
            uy_ref[:, S5_CW * g:S5_CW * (g + 1)] = acc[:, S5_CW * gl:S5_CW * (gl + 1)] + _dot(ug, m_ref[g])

    lane = lax.broadcasted_iota(jnp.int32, (R, LANES), 1)
    in_piece = [jnp.logical_and(lane >= S5_P * k, lane < S5_P * (k + 1)) for k in range(LANES // S5_P)]
    for half, y_ref in enumerate((ylo_ref, yhi_ref)):
        for t in range(S5_L):
            merged = None
            for gl in range(gph):
                g = gph * half + gl
                slot = _s5_slot(g, t)
                col = S5_CW * g + LANES * (S5_P * slot // LANES)
                v = uy_ref[:, col:col + LANES]
                merged = v if merged is None else jnp.where(in_piece[slot % (LANES // S5_P)], v, merged)
            shift = (LANES - S5_P * t % LANES) % LANES
            y_ref[pl.ds(t, R, stride=S5_L), :] = pltpu.roll(merged, shift, 1) if shift else merged


def _s5(proj, m, f2, e2, al, h0, l):
    half = pl.BlockSpec((N_CTX, LANES), lambda s: (s, 0))
    return pl.pallas_call(
        _s5_kernel,
        grid=(2,),
        in_specs=[pl.BlockSpec((N_CTX, LANES), lambda s: (s, 0)),
                  pl.BlockSpec((N_CTX, LANES), lambda s: (s, 1)),
                  _layer_spec(m.shape[1:], l, single=True), _layer_spec(f2.shape[1:], l, single=True),
                  _layer_spec(e2.shape[1:], l, single=True), _layer_spec(al.shape[1:], l),
                  _layer_spec(h0.shape[1:], l)],
        out_specs=(half, half, _const_spec((2, 2, BATCH, S5_SW))),
        out_shape=(jax.ShapeDtypeStruct((N_TOK, LANES), F32),
                   jax.ShapeDtypeStruct((N_TOK, LANES), F32),
                   jax.ShapeDtypeStruct((2, 2, BATCH, S5_SW), F32)),
        scratch_shapes=[pltpu.VMEM((S5_STEP_ROWS, S5_G * S5_CW), F32),
                        pltpu.VMEM((2, 2, S5_STEP_ROWS, S5_SW), F32)],
        compiler_params=pltpu.CompilerParams(vmem_limit_bytes=VMEM_LIMIT),
        name="s5_mixer",
    )(proj, proj, m, f2, e2, al, h0)


HG_NLEV = int(math.log2(HG_L))
HG_W = HG_HEADS * HG_DK
HG_CPS = 8
HG_BLK = HG_CPS * HG_L
HG_NB_CTX = N_CTX // HG_BLK
HG_SPB = HG_BLK // SEQ
HG_CPQ = SEQ // HG_L
HG_NC_LAT = DEC_SEQ // HG_BLK
assert HG_SPB >= 1 and HG_SPB * SEQ == HG_BLK
assert HG_L == HG_DK


def _hgrn_consts():
    L = HG_L
    w = np.zeros((HG_NLEV + 2, L, L), np.float32)
    mask = np.zeros((HG_NLEV + 1, L, L), np.float32)
    for lev in range(HG_NLEV):
        blk = L >> lev
        half = blk // 2
        for t in range(L):
            p, o = divmod(t, blk)
            bd = p * blk + half - 1
            if o >= half:
                w[lev, t, bd + 1:t + 1] = 1.0
            else:
                w[lev, t, t + 1:bd + 1] = 1.0
        jj, ii = np.meshgrid(np.arange(L), np.arange(L), indexing='ij')
        mask[lev] = ((jj // blk == ii // blk) & (jj % blk >= half) & (ii % blk < half)).astype(np.float32)
    mask[HG_NLEV] = np.eye(L, dtype=np.float32)
    for t in range(L):
        w[HG_NLEV, t, :t + 1] = 1.0
        w[HG_NLEV + 1, t, t + 1:] = 1.0
    out = []
    for wd, md in ((w, mask), (w[:, ::-1, ::-1], mask[:, ::-1, ::-1])):
        wflat = wd.reshape((HG_NLEV + 2) * L, L)
        out.append((np.concatenate([wflat] * 3, axis=1), np.tile(md, (1, 1, HG_HEADS))))
    wcat = np.stack([out[0][0], out[1][0]])
    mask4 = np.stack([out[0][1], out[1][1]])
    hm = np.kron(np.eye(HG_HEADS, dtype=np.float32), np.ones((HG_DK, HG_DK), np.float32))
    return wcat, mask4, hm


def _hg_pos(i):
    is_ctx = i < HG_NB_CTX
    c = jnp.where(is_ctx, 0, (i - HG_NB_CTX) % HG_NC_LAT)
    nc = jnp.where(is_ctx, 1, HG_NC_LAT)
    return is_ctx, c, nc


def _hg_bwd_blk(i):
    _, c, nc = _hg_pos(i)
    return i + nc - 1 - 2 * c


def _hg_local(items, lb_ref, wcat_ref, mask_ref, hm):
    L = HG_L
    hmb = hm.astype(BF16)

    def bd4(x):
        return jnp.concatenate([x] * HG_HEADS, axis=0) * hmb

    kks, exs = [], []
    for d, q, z, v in items:
        lbp = lb_ref[d]
        sp = jnp.log1p(jnp.exp(-jnp.abs(z)))
        ls = jnp.minimum(z, 0.0) - sp
        kks.append(lbp[2:3] * jnp.exp(jnp.minimum(-z, 0.0) - sp))
        a = lbp[0:1]
        b = lbp[1:2] + ls
        logf = jnp.maximum(a, b) + jnp.log1p(jnp.exp(-jnp.abs(a - b)))
        p0 = logf.astype(BF16)
        r0 = logf - p0.astype(F32)
        p1 = r0.astype(BF16)
        p2 = (r0 - p1.astype(F32)).astype(BF16)
        exs.append(jnp.exp(_dot(wcat_ref[d], jnp.concatenate([p0, p1, p2], axis=0))))
    scs = [None] * len(items)
    for lev in range(HG_NLEV + 1):
        for n, (d, q, z, v) in enumerate(items):
            if lev < HG_NLEV:
                al = exs[n][L * lev:L * (lev + 1)]
                lhs = (q * al).astype(BF16)
                rhs = (kks[n] * al).astype(BF16)
            else:
                lhs = q.astype(BF16)
                rhs = kks[n].astype(BF16)
            t = _dot_nt(lhs, bd4(rhs)) * mask_ref[d, lev]
            scs[n] = t if scs[n] is None else scs[n] + t
    out = []
    for n, (d, q, z, v) in enumerate(items):
        vb = v.astype(BF16)
        o = _dot(scs[n].astype(BF16), bd4(vb))
        eq = exs[n][L * HG_NLEV:L * (HG_NLEV + 1)]
        qe = (q * eq).astype(BF16)
        ke = (kks[n] * exs[n][L * (HG_NLEV + 1):L * (HG_NLEV + 2)]).astype(BF16)
        g = eq[L - 1:L] if d == 0 else eq[0:1]
        out.append((o, qe, g, _dot_tn(vb, ke) * hm))
    return out


def _hgrn_kernel(qf, zf, vf, qb, zb, vb, lb_ref, w_ref, mask_ref, hm_ref, s0_ref, wup_ref, wdn_ref,
                 of_ref, ob_ref, sfin_ref, wupb_ref, wdnb_ref, st_ref):
    is_ctx, c, nc = _hg_pos(pl.program_id(0))
    wupb_ref[...] = wup_ref[...].astype(BF16)
    wdnb_ref[...] = wdn_ref[...].astype(BF16)

    hm = hm_ref[...]
    rows = [slice(HG_L * k, HG_L * (k + 1)) for k in range(HG_CPS)]
    items = [(d, q_ref[r, :], z_ref[r, :], v_ref[r, :])
             for d, (q_ref, z_ref, v_ref) in enumerate(((qf, zf, vf), (qb, zb, vb))) for r in rows]
    loc = _hg_local(items, lb_ref, w_ref, mask_ref, hm)

    def chain(d, o_ref, st, chunks):
        for k in (chunks if d == 0 else reversed(chunks)):
            o, qe, g, upd = loc[HG_CPS * d + k]
            o_ref[rows[k], :] = o + _dot_nt(qe, st.astype(BF16))
            st = st * g + upd
        return st

    @pl.when(is_ctx)
    def _():
        for sq in range(HG_SPB):
            for d, o_ref in enumerate((of_ref, ob_ref)):
                st = chain(d, o_ref, jnp.zeros((HG_W, HG_W), F32), range(HG_CPQ * sq, HG_CPQ * (sq + 1)))
                s_kv = st.T
                for h in range(HG_HEADS):
                    sfin_ref[sq, d, h] = s_kv[HG_DK * h:HG_DK * (h + 1), HG_DK * h:HG_DK * (h + 1)]

    @pl.when(jnp.logical_not(is_ctx))
    def _():
        @pl.when(c == 0)
        def _():
            st_ref[...] = s0_ref[0]

        for d, o_ref in enumerate((of_ref, ob_ref)):
            st_ref[d] = chain(d, o_ref, st_ref[d], range(HG_CPS))


def _hgrn(proj, lbp, consts, s0, w_up, w_down, l):
    wcat, mask4, hm = consts
    steps = N_TOK // HG_BLK
    up_rows = D_MODEL // steps
    dn_rows = 2 * D_FF // steps

    def fwd(col):
        return pl.BlockSpec((HG_BLK, HG_W), lambda i: (i, col))

    def bwd(col):
        return pl.BlockSpec((HG_BLK, HG_W), lambda i: (_hg_bwd_blk(i), col))

    return pl.pallas_call(
        _hgrn_kernel,
        grid=(N_TOK // HG_BLK,),
        in_specs=[fwd(COL_HQ), fwd(COL_HF_FWD), fwd(COL_HI), bwd(COL_HQ), bwd(COL_HF_BWD), bwd(COL_HI),
                  _layer_spec(lbp.shape[1:], l), _const_spec(wcat.shape), _const_spec(mask4.shape),
                  _const_spec(hm.shape),
                  pl.BlockSpec((None, 1, 2, HG_W, HG_W),
                               lambda i: (l, jnp.maximum(i - HG_NB_CTX, 0) // HG_NC_LAT, 0, 0, 0)),
                  pl.BlockSpec((None, up_rows, 2 * D_FF), lambda i: (l, i, 0)),
                  pl.BlockSpec((None, dn_rows, D_MODEL), lambda i: (l, i // 2, 0))],
        out_specs=(pl.BlockSpec((HG_BLK, HG_W), lambda i: (i, 0)),
                   pl.BlockSpec((HG_BLK, HG_W), lambda i: (_hg_bwd_blk(i), 0)),
                   pl.BlockSpec((HG_SPB, 2, HG_HEADS, HG_DK, HG_DK),
                                lambda i: (jnp.minimum(i, HG_NB_CTX - 1), 0, 0, 0, 0)),
                   pl.BlockSpec((up_rows, 2 * D_FF), lambda i: (i, 0)),
                   pl.BlockSpec((dn_rows, D_MODEL), lambda i: (i // 2, 0))),
        out_shape=(jax.ShapeDtypeStruct((N_TOK, HG_W), F32),
                   jax.ShapeDtypeStruct((N_TOK, HG_W), F32),
                   jax.ShapeDtypeStruct((BATCH, 2, HG_HEADS, HG_DK, HG_DK), F32),
                   jax.ShapeDtypeStruct((D_MODEL, 2 * D_FF), BF16),
                   jax.ShapeDtypeStruct((D_FF, D_MODEL), BF16)),
        scratch_shapes=[pltpu.VMEM((2, HG_W, HG_W), F32)],
        compiler_params=pltpu.CompilerParams(vmem_limit_bytes=VMEM_LIMIT),
        name="hgrn_mixer",
    )(proj, proj, proj, proj, proj, proj, lbp, wcat, mask4, hm, s0, w_up, w_down)


FN_CTX_SPG = 4
FN_LAT_TR = 512


def _dft_consts(t_len):
    n = GROUP_W // 4
    k = np.arange(n)
    ang = 2.0 * np.pi * ((k[:, None] * k[None, :]) % n) / n
    eye = np.eye(4)
    cs = np.concatenate([np.kron(eye, np.cos(ang)), np.kron(eye, np.sin(ang))], axis=1) / math.sqrt(n)
    t = np.arange(t_len)
    angt = 2.0 * np.pi * ((t[:, None] * t[None, :]) % t_len) / t_len
    dft = np.concatenate([np.cos(angt), -np.sin(angt)], axis=1) / math.sqrt(t_len)
    return cs.astype(np.float32), dft.astype(np.float32)


def _fnet_kernel(x_ref, cs_ref, dft_ref, w_ref, o_ref, r_ref, *, t_len, spg, tr):
    @pl.when(pl.program_id(1) == 0)
    def _():
        t = _dot(x_ref[...].astype(BF16), cs_ref[...])
        for s in range(spg):
            rows = slice(t_len * s, t_len * (s + 1))
            r_ref[s, 0:t_len, :] = t[rows, :GROUP_W].astype(BF16)
            r_ref[s, t_len:2 * t_len, :] = t[rows, GROUP_W:].astype(BF16)

    w = w_ref[...].astype(BF16)
    for s in range(spg):
        y = _dot(dft_ref[...], r_ref[s])
        o_ref[tr * s:tr * (s + 1), :] = _dot(y.astype(BF16), w)


def _fnet(proj, cs, dft, fn_w, l, *, n_seq, seq_len, tok0, spg, tr):
    nj = seq_len // tr
    assert spg == 1 or nj == 1
    gb0 = tok0 // (spg * seq_len)
    return pl.pallas_call(
        functools.partial(_fnet_kernel, t_len=seq_len, spg=spg, tr=tr),
        grid=(n_seq // spg, nj),
        in_specs=[
            pl.BlockSpec((spg * seq_len, GROUP_W), lambda g, j: (gb0 + g, COL_XC)),
            _const_spec((GROUP_W, 2 * GROUP_W)),
            pl.BlockSpec((tr, 2 * seq_len), lambda g, j: (j, 0)),
            _layer_spec((GROUP_W, GROUP_W), l),
        ],
        out_specs=pl.BlockSpec((spg * tr, GROUP_W), lambda g, j: (g * nj + j, 0)),
        out_shape=jax.ShapeDtypeStruct((n_seq * seq_len, GROUP_W), F32),
        scratch_shapes=[pltpu.VMEM((spg, 2 * seq_len, GROUP_W), BF16)],
        compiler_params=pltpu.CompilerParams(vmem_limit_bytes=VMEM_LIMIT),
        name="fnet_mixer",
    )(proj, cs, dft, fn_w)


def _outproj_kernel(*refs):
    (xa_ref, hgate_ref, gu_ref, gv_ref, ylo_ref, yhi_ref, of_ref, ob_ref, ycc_ref, ycl_ref, mod_ref, d_ref,
     wglu_ref, gn_ref, bdm_ref, gmn_ref, gmw_ref, gmb_ref, gmh_ref, wout_ref, n2_ref,
     x1_ref, h2_ref, wob_ref) = refs[-24:]
    x_refs = refs[:-24]

    @pl.when(pl.program_id(0) == 0)
    def _():
        wob_ref[...] = wout_ref[...].astype(BF16)

    gn = gn_ref[...]
    ys = jnp.concatenate([ylo_ref[...], yhi_ref[...]], axis=1)
    y5 = _gelu(ys + d_ref[...] * xa_ref[...])
    glu = jax.nn.sigmoid(_dot(y5.astype(BF16), wglu_ref[...].astype(BF16)))
    out_a = _rms(y5 * glu, gn[:, 0:GROUP_W])
    o = of_ref[...] + ob_ref[...]
    o2 = o * o
    o2h = o2.astype(BF16)
    o2l = (o2 - o2h.astype(F32)).astype(BF16)
    ms = _dot(o2h, bdm_ref[...]) + _dot(o2l, bdm_ref[...])
    hg = hgate_ref[...]
    out_b = o * lax.rsqrt(ms + EPS) * gn[:, GROUP_W:2 * GROUP_W] * (hg * jax.nn.sigmoid(hg))
    out_c = _rms(_x_value((ycc_ref, ycl_ref), TBP), gn[:, 2 * GROUP_W:3 * GROUP_W])
    gm_hm = gmh_ref[...]
    out_d = []
    for c in range(TBP // GM_CHUNK):
        rows = slice(GM_CHUNK * c, GM_CHUNK * (c + 1))
        gv = _rms(_gelu(gv_ref[rows, :]), gmn_ref[...]).astype(BF16)
        g4 = jnp.concatenate([gv] * GM_HEADS, axis=0) * gm_hm
        sp = _dot(gmw_ref[...], g4) + gmb_ref[...]
        out_d.append(_rms(_gelu(gu_ref[rows, :]) * sp, gn[:, 3 * GROUP_W:]))
    out_d = jnp.concatenate(out_d, axis=0)
    m = None
    for k, part in enumerate((out_a, out_b, out_c, out_d)):
        t = _dot(part.astype(BF16), wob_ref[GROUP_W * k:GROUP_W * (k + 1), :])
        m = t if m is None else m + t
    mod = mod_ref[...]
    g1 = mod[:, 2 * D_MODEL:3 * D_MODEL]
    sh2 = mod[:, 3 * D_MODEL:4 * D_MODEL]
    sc2 = mod[:, 4 * D_MODEL:5 * D_MODEL]
    x1 = _x_value(x_refs, TBP) + g1 * m
    x1_ref[...] = x1
    h2_ref[...] = (_rms(x1, n2_ref[...]) * (1.0 + sc2) + sh2).astype(BF16)


def _outproj(xs, proj, ylo, yhi, of, ob, yc, mods, s5_d, wglu, gn, bdm, gm_norm_g, gm_w, gm_b, gm_hm, wout,
             norm2_g, l):
    def tok(width, col=0):
        return pl.BlockSpec((TBP, width), lambda i: (i, col))

    return pl.pallas_call(
        _outproj_kernel,
        grid=(N_TOK // TBP,),
        in_specs=_x_specs(len(xs) == 2, tb=TBP) + [
            tok(GROUP_W, COL_XA), tok(GROUP_W, COL_HGATE), tok(GROUP_W, COL_GU), tok(GROUP_W, COL_GV),
            tok(LANES), tok(LANES), tok(GROUP_W), tok(GROUP_W), *_x_specs(True, GROUP_W, TBP),
            _mod_spec(l, TBP),
            _layer_spec((1, GROUP_W), l), _layer_spec((GROUP_W, GROUP_W), l), _layer_spec((1, D_MODEL), l),
            _const_spec((GROUP_W, GROUP_W)),
            _layer_spec((1, GROUP_W), l), _layer_spec((GM_CHUNK, GM_HEADS * GM_CHUNK), l),
            _layer_spec((GM_CHUNK, GROUP_W), l), _const_spec((GM_HEADS * GM_CHUNK, GROUP_W)),
            _layer_spec((D_MODEL, D_MODEL), l, single=True), _layer_spec((1, D_MODEL), l),
        ],
        out_specs=(tok(D_MODEL), tok(D_MODEL)),
        out_shape=(jax.ShapeDtypeStruct((N_TOK, D_MODEL), F32),
                   jax.ShapeDtypeStruct((N_TOK, D_MODEL), BF16)),
        scratch_shapes=[pltpu.VMEM((D_MODEL, D_MODEL), BF16)],
        compiler_params=pltpu.CompilerParams(vmem_limit_bytes=VMEM_LIMIT),
        name="out_proj",
    )(*xs, proj, proj, proj, proj, ylo, yhi, of, ob, *yc, mods, s5_d, wglu, gn, bdm, gm_norm_g, gm_w, gm_b,
      gm_hm, wout, norm2_g)


FF_SEG = GRID_W
FF_NSEG = TB // FF_SEG
FF_SEQ_STRIPS = SEQ // FF_SEG
FF_PERM_ROWS = FF_SEQ_STRIPS * FF_SEG
SUBLANES = 8
FF_NGRP = FF_SEG // SUBLANES
GELU_C0 = 0.7978845608028654
GELU_C1 = GELU_C0 * 0.044715


def _ffn_perm_consts():
    p = np.zeros((FF_PERM_ROWS, FF_PERM_ROWS), np.float32)
    for k in range(FF_SEQ_STRIPS):
        for t in range(FF_SEG):
            p[FF_SEG * k + SUBLANES * (t % FF_NGRP) + t // FF_NGRP, FF_SEG * k + t] = 1.0
    return p


def _ffn_kernel(x1_ref, h2_ref, mod_ref, perm_ref, wup_ref, cw_ref, cb_ref, wdn_ref, fg_ref, *rest, final):
    *o_refs, h2p_ref, z_ref, hid_ref = rest
    i = pl.program_id(0)
    joined = (i < NB_CTX).astype(F32)
    n_tiles = D_FF // FF_TILE
    sub = lax.broadcasted_iota(jnp.int32, (SUBLANES, 2 * FF_TILE), 0)
    is_first = sub == 0
    is_last = sub == SUBLANES - 1
    zero_grp = jnp.zeros((SUBLANES, 2 * FF_TILE), F32)
    for r in range(0, TB, FF_PERM_ROWS):
        h2p_ref[r:r + FF_PERM_ROWS, :] = _dot(perm_ref[...], h2_ref[r:r + FF_PERM_ROWS, :]).astype(BF16)

    def up(j):
        lo = FF_TILE * j
        for c, col in enumerate((lo, D_FF + lo)):
            z_ref[j % 2, :, FF_TILE * c:FF_TILE * (c + 1)] = _dot(h2p_ref[...], wup_ref[:, col:col + FF_TILE])

    def gate(j):
        slot = j % 2
        lo = FF_TILE * j
        w = jnp.concatenate([cw_ref[:, lo:lo + FF_TILE], cw_ref[:, D_FF + lo:D_FF + lo + FF_TILE]], axis=1)
        b = jnp.concatenate([cb_ref[:, lo:lo + FF_TILE], cb_ref[:, D_FF + lo:D_FF + lo + FF_TILE]], axis=1)

        def grp(k, v):
            r = FF_SEG * k + SUBLANES * v
            return z_ref[slot, r:r + SUBLANES, :]

        down = [pltpu.roll(grp(k, FF_NGRP - 1), 1, 0) for k in range(FF_NSEG)]
        up_ = [pltpu.roll(grp(k, 0), SUBLANES - 1, 0) for k in range(FF_NSEG)]
        strips = []
        for k in range(FF_NSEG):
            g = [grp(k, v) for v in range(FF_NGRP)]
            before = down[k - 1] * joined if k % FF_SEQ_STRIPS > 0 else zero_grp
            after = up_[k + 1] * joined if k % FF_SEQ_STRIPS < FF_SEQ_STRIPS - 1 else zero_grp
            zm1 = jnp.concatenate([jnp.where(is_first, before, down[k])] + g[:-1], axis=0)
            zp1 = jnp.concatenate(g[1:] + [jnp.where(is_last, after, up_[k])], axis=0)
            zc = b + zm1 * w[0:1] + jnp.concatenate(g, axis=0) * w[1:2] + zp1 * w[2:3]
            a = zc[:, :FF_TILE]
            t = jnp.tanh(a * (GELU_C0 + GELU_C1 * (a * a)))
            strips.append((a * (0.5 + 0.5 * t) * zc[:, FF_TILE:]).astype(BF16))
        hid_ref[:, lo:lo + FF_TILE] = jnp.concatenate(strips, axis=0)

    up(0)
    for j in range(n_tiles):
        if j + 1 < n_tiles:
            up(j + 1)
        gate(j)
    acc = _dot(hid_ref[...], wdn_ref[...])
    acc = jnp.swapaxes(acc.reshape(FF_NSEG, FF_NGRP, SUBLANES, D_MODEL), 1, 2).reshape(TB, D_MODEL)
    g2 = mod_ref[...][:, 5 * D_MODEL:6 * D_MODEL]
    x2 = x1_ref[...] + g2 * acc
    if final:
        y = _rms(x2, fg_ref[...])
        yp_ref, ys_ref = o_refs

        @pl.when(i < NB_CTX)
        def _():
            yp_ref[...] = y

        @pl.when(i >= NB_CTX)
        def _():
            ys_ref[...] = y
    else:
        o_refs[0][...] = x2


def _ffn(x1, h2, mods, perm, wup_bf, conv_w, conv_b, wdn_bf, final_g, l, *, final):
    if final:
        out_specs = tuple(_x_specs(True))
        out_shape = (jax.ShapeDtypeStruct((N_CTX, D_MODEL), F32), jax.ShapeDtypeStruct((N_LAT, D_MODEL), F32))
    else:
        out_specs = pl.BlockSpec((TB, D_MODEL), lambda i: (i, 0))
        out_shape = jax.ShapeDtypeStruct((N_TOK, D_MODEL), F32)
    return pl.pallas_call(
        functools.partial(_ffn_kernel, final=final),
        grid=(N_TOK // TB,),
        in_specs=[
            pl.BlockSpec((TB, D_MODEL), lambda i: (i, 0)),
            pl.BlockSpec((TB, D_MODEL), lambda i: (i, 0)),
            _mod_spec(l), _const_spec((FF_PERM_ROWS, FF_PERM_ROWS)),
            _const_spec((D_MODEL, 2 * D_FF), single=True), _layer_spec((3, 2 * D_FF), l),
            _layer_spec((1, 2 * D_FF), l), _const_spec((D_FF, D_MODEL), single=True),
            _const_spec((1, D_MODEL)),
        ],
        out_specs=out_specs,
        out_shape=out_shape,
        scratch_shapes=[pltpu.VMEM((TB, D_MODEL), BF16), pltpu.VMEM((2, TB, 2 * FF_TILE), F32),
                        pltpu.VMEM((TB, D_FF), BF16)],
        compiler_params=pltpu.CompilerParams(vmem_limit_bytes=VMEM_LIMIT),
        name="conv_ffn",
    )(x1, h2, mods, perm, wup_bf, conv_w, conv_b, wdn_bf, final_g.reshape(1, D_MODEL))


def kernel(x_prompt, x_sample, state_s5_re, state_s5_im, state_hgrn, c, c_ctx, w_ada, b_ada, norm1_g,
           norm2_g, w_in, s5_lam_re, s5_lam_im, s5_log_dt, s5_b_re, s5_b_im, s5_c_re, s5_c_im, s5_d,
           s5_w_glu, hg_lb_logits, fn_w, gm_norm_g, gm_ws, gm_bs, grp_norm_g, w_out, ffn_w_up,
           ffn_conv_w, ffn_conv_b, ffn_w_down, final_norm_g):
    xs = (x_prompt.reshape(N_CTX, D_MODEL), x_sample.reshape(N_LAT, D_MODEL))
    cvecs = jnp.concatenate([c_ctx[None], c, jnp.zeros((8 - 1 - DEC_BATCH, D_MODEL), F32)], axis=0)

    lb_p = jax.nn.softmax(hg_lb_logits.astype(F32), axis=0)
    lbs = jnp.maximum(jnp.cumsum(lb_p, axis=0) - lb_p[0], 0.0)
    lbps = jnp.stack([jnp.log(lbs), jnp.log1p(-lbs), 1.0 - lbs], axis=2)

    hg_consts_np = _hgrn_consts()
    hg_consts = (jnp.asarray(hg_consts_np[0], BF16), jnp.asarray(hg_consts_np[1], F32),
                 jnp.asarray(hg_consts_np[2], F32))
    cs_np, dft_ctx_np = _dft_consts(SEQ)
    _, dft_lat_np = _dft_consts(DEC_SEQ)
    cs = jnp.asarray(cs_np, F32).astype(BF16)
    dft_ctx = jnp.asarray(dft_ctx_np, F32).astype(BF16)
    dft_lat = jnp.asarray(dft_lat_np, F32).astype(BF16)
    gm_hm = jnp.asarray(np.kron(np.eye(GM_HEADS), np.ones((GM_CHUNK, GROUP_W // GM_HEADS))), BF16)
    bdm = jnp.asarray(np.kron(np.eye(HG_HEADS), np.ones((HG_DK, HG_DK))) / HG_DK, BF16)

    tables = _s5_tables(s5_lam_re, s5_lam_im, s5_log_dt, s5_b_re, s5_b_im, s5_c_re, s5_c_im)
    mods, s5_m, s5_f, s5_e = _prep(cvecs, w_ada, b_ada, tables, jnp.asarray(_s5_shift_consts(), BF16))
    mods = mods.reshape(DEPTH, 8, 1, N_MOD * D_MODEL)
    s5_al = tables[-1]
    s5_h0 = jnp.stack([state_s5_re, state_s5_im]).astype(F32).transpose(2, 3, 0, 1, 4, 5)
    s5_h0 = s5_h0.reshape(DEPTH, 2, 2, DEC_BATCH, S5_SW)
    hg_s0 = jnp.einsum('bldhkv,hg->lbdhvgk', state_hgrn.astype(F32), jnp.eye(HG_HEADS, dtype=F32))
    hg_s0 = hg_s0.reshape(DEPTH, DEC_BATCH, 2, HG_W, HG_W)

    def rows(a):
        return a.reshape(DEPTH, 1, a.shape[-1])

    norm1_r, norm2_r, gn_r, s5_d_r, gmn_r, cb_r = (rows(a) for a in (
        norm1_g, norm2_g, grp_norm_g, s5_d, gm_norm_g, ffn_conv_b))
    gm_w = gm_ws.transpose(0, 2, 1, 3).reshape(DEPTH, GM_CHUNK, GM_HEADS * GM_CHUNK).astype(BF16)
    gm_b = jnp.repeat(gm_bs.transpose(0, 2, 1), GROUP_W // GM_HEADS, axis=2)
    ff_perm = jnp.asarray(_ffn_perm_consts(), BF16)

    new_re, new_im, new_hg = [], [], []
    for l in range(DEPTH):
        proj = _inproj(xs, mods, norm1_r, w_in, l)

        ylo, yhi, hfin = _s5(proj, s5_m, s5_f, s5_e, s5_al, s5_h0, l)
        hfin = hfin.reshape(2, 2, BATCH, S5_G, S5_N).transpose(1, 2, 0, 3, 4)
        new_re.append(hfin[0])
        new_im.append(hfin[1])

        of, ob, sfin, wup_bf, wdn_bf = _hgrn(proj, lbps, hg_consts, hg_s0, ffn_w_up, ffn_w_down, l)
        new_hg.append(sfin)

        yc = (_fnet(proj, cs, dft_ctx, fn_w, l, n_seq=BATCH, seq_len=SEQ, tok0=0, spg=FN_CTX_SPG, tr=SEQ),
              _fnet(proj, cs, dft_lat, fn_w, l, n_seq=DEC_BATCH, seq_len=DEC_SEQ, tok0=N_CTX, spg=1,
                    tr=FN_LAT_TR))

        x1, h2 = _outproj(xs, proj, ylo, yhi, of, ob, yc, mods, s5_d_r, s5_w_glu, gn_r, bdm, gmn_r, gm_w, gm_b,
                          gm_hm, w_out, norm2_r, l)
        res = _ffn(x1, h2, mods, ff_perm, wup_bf, ffn_conv_w, cb_r, wdn_bf, final_norm_g, l,
                   final=(l == DEPTH - 1))
        xs = (res,)

    y_prompt = res[0].reshape(BATCH, SEQ, D_MODEL)
    y_sample = res[1].reshape(DEC_BATCH, DEC_SEQ, D_MODEL)
    return (y_prompt, y_sample, jnp.stack(new_re, axis=1), jnp.stack(new_im, axis=1),
            jnp.stack(new_hg, axis=1))
```

```python
import functools
import math

import numpy as np
import jax
import jax.numpy as jnp
from jax import lax
from jax.experimental import pallas as pl
from jax.experimental.pallas import tpu as pltpu

D_MODEL = 1024
BATCH = 16
SEQ = 256
DEPTH = 2
DEC_BATCH = 2
DEC_SEQ = 2048
GRID_W = 64
GROUP_W = 256
S5_P = 16
S5_G = 16
S5_N = 64
HG_HEADS = 4
HG_DK = 64
GM_HEADS = 4
GM_CHUNK = 128
D_FF = 2816
N_MOD = 6
D_IN = 9 * GROUP_W
EPS = 1e-6
LAM_RE_MAX = -1e-4

N_CTX = BATCH * SEQ
N_LAT = DEC_BATCH * DEC_SEQ
N_TOK = N_CTX + N_LAT
TB = 512
TBP = 512
NB_CTX = N_CTX // TB
S5_L = 16
HG_L = 64
FF_TILE = 256
LANES = 128
VMEM_LIMIT = 56 * 1024 * 1024

F32 = jnp.float32
BF16 = jnp.bfloat16

COL_XA, COL_HQ, COL_HF_FWD, COL_HF_BWD, COL_HI, COL_HGATE, COL_XC, COL_GU, COL_GV = range(9)
HG_IN_W = 4 * GROUP_W
OP_IN_W = 4 * GROUP_W


def _mod_row(i, tb):
    return jnp.where(i < N_CTX // tb, 0, 1 + (i - N_CTX // tb) // (DEC_SEQ // tb))


def _gelu(x):
    return 0.5 * x * (1.0 + jnp.tanh(0.7978845608028654 * (x + 0.044715 * (x * x * x))))


def _rms(x, g):
    return x * lax.rsqrt(jnp.mean(x * x, axis=-1, keepdims=True) + EPS) * g


def _dot(a, b):
    return jnp.dot(a, b, preferred_element_type=F32)


def _dot_nt(a, b, precision=None):
    return lax.dot_general(a, b, (((1,), (1,)), ((), ())), precision=precision, preferred_element_type=F32)


def _dot_tn(a, b):
    return lax.dot_general(a, b, (((0,), (0,)), ((), ())), preferred_element_type=F32)


def _const_spec(shape, single=False):
    kw = {"pipeline_mode": pl.Buffered(1)} if single else {}
    return pl.BlockSpec(shape, lambda *_: (0,) * len(shape), **kw)


def _layer_spec(shape, l, single=False):
    kw = {"pipeline_mode": pl.Buffered(1)} if single else {}
    return pl.BlockSpec((None,) + tuple(shape), lambda *_: (l,) + (0,) * len(shape), **kw)


def _mod_spec(l, tb=TB):
    return pl.BlockSpec((None, None, 1, N_MOD * D_MODEL), lambda i: (l, _mod_row(i, tb), 0, 0))


def _x_specs(split, width=D_MODEL, tb=TB):
    nb_ctx = N_CTX // tb
    if split:
        return [pl.BlockSpec((tb, width), lambda i: (jnp.minimum(i, nb_ctx - 1), 0)),
                pl.BlockSpec((tb, width), lambda i: (jnp.maximum(i - nb_ctx, 0), 0))]
    return [pl.BlockSpec((tb, width), lambda i: (i, 0))]


def _x_value(x_refs, tb=TB):
    if len(x_refs) == 1:
        return x_refs[0][...]
    return jnp.where(pl.program_id(0) < N_CTX // tb, x_refs[0][...], x_refs[1][...])


def _inproj_kernel(*refs):
    *x_refs, mod_ref, g_ref, w_ref, hg_ref, op_ref, xc_ref, xalo_ref, xahi_ref, wb_ref = refs

    @pl.when(pl.program_id(0) == 0)
    def _():
        wb_ref[...] = w_ref[...].astype(BF16)

    mod = mod_ref[...]
    sh = mod[:, 0:D_MODEL]
    sc = mod[:, D_MODEL:2 * D_MODEL]
    h = _rms(_x_value(x_refs, TBP), g_ref[...]) * (1.0 + sc) + sh
    o = _dot(h.astype(BF16), wb_ref[...])
    col = lambda c, n=1: o[:, GROUP_W * c:GROUP_W * (c + n)]
    hg_ref[...] = col(COL_HQ, 4)
    op_ref[...] = jnp.concatenate([col(COL_XA), col(COL_HGATE), col(COL_GU, 2)], axis=1)
    xc_ref[...] = col(COL_XC)
    xalo_ref[...] = o[:, 0:LANES]
    xahi_ref[...] = o[:, LANES:2 * LANES]


def _inproj(xs, mods, norm_g, w_in, l):
    return pl.pallas_call(
        _inproj_kernel,
        grid=(N_TOK // TBP,),
        in_specs=_x_specs(len(xs) == 2, tb=TBP) + [
            _mod_spec(l, TBP),
            _layer_spec((1, D_MODEL), l),
            _layer_spec((D_MODEL, D_IN), l, single=True),
        ],
        out_specs=tuple(pl.BlockSpec((TBP, w), lambda i: (i, 0)) for w in (HG_IN_W, OP_IN_W, GROUP_W, LANES, LANES)),
        out_shape=tuple(jax.ShapeDtypeStruct((N_TOK, w), F32) for w in (HG_IN_W, OP_IN_W, GROUP_W, LANES, LANES)),
        scratch_shapes=[pltpu.VMEM((D_MODEL, D_IN), BF16)],
        compiler_params=pltpu.CompilerParams(vmem_limit_bytes=VMEM_LIMIT),
        name="in_proj",
    )(*xs, mods, norm_g, w_in)


S5_NPAIR = S5_G // 2
S5_CW = S5_L * S5_P
S5_SW = S5_G * S5_N
S5_STEP_ROWS = N_CTX // S5_L
S5_SEG_NC = SEQ // S5_L
S5_LAT_SEGS = DEC_SEQ // SEQ


def _s5_slot(g, t):
    return (t + g) % S5_L


def _s5_tables(lam_re, lam_im, log_dt, b_re, b_im, c_re, c_im):
    lr = jnp.minimum(lam_re.astype(F32), LAM_RE_MAX)
    li = lam_im.astype(F32)
    dt = jnp.exp(log_dt.astype(F32))[..., None]
    mag = jnp.exp(lr * dt)
    ang = li * dt
    ab_re = mag * jnp.cos(ang)
    ab_im = mag * jnp.sin(ang)
    den = lr * lr + li * li
    xr = ab_re - 1.0
    z_re = (xr * lr + ab_im * li) / den
    z_im = (ab_im * lr - xr * li) / den
    bb_re = z_re[..., None] * b_re - z_im[..., None] * b_im
    bb_im = z_re[..., None] * b_im + z_im[..., None] * b_re
    tau = jnp.arange(S5_L + 1, dtype=F32)[:, None, None, None, None]
    pm = jnp.exp(lr * dt * tau)
    pa = li * dt * tau
    pw_re = pm * jnp.cos(pa)
    pw_im = pm * jnp.sin(pa)
    eye2 = jnp.eye(2, dtype=F32)

    def pw_pairs(a):
        a = a.transpose(1, 2, 3, 0, 4).reshape(DEPTH, 2, S5_NPAIR, 2, S5_L + 1, S5_N)
        return a.transpose(0, 1, 2, 4, 3, 5).reshape(DEPTH, 2, S5_NPAIR, S5_L + 1, 2 * S5_N)

    def mat_pairs(a):
        a = a.reshape(DEPTH, 2, S5_NPAIR, 2, S5_P, S5_N)
        return jnp.einsum('ldjaqn,ab->ldjaqbn', a, eye2).reshape(DEPTH, 2, S5_NPAIR, 2, S5_P, 2 * S5_N)

    kc = S5_L * jnp.arange(S5_SEG_NC + 1, dtype=F32)[:, None, None, None, None]
    cm = jnp.exp(lr * dt * kc)
    ca = li * dt * kc
    al = jnp.stack([cm * jnp.cos(ca), cm * jnp.sin(ca)])
    al = al.transpose(2, 3, 0, 1, 4, 5).reshape(DEPTH, 2, 2, S5_SEG_NC + 1, S5_SW)
    return (pw_pairs(pw_re), pw_pairs(pw_im),
            mat_pairs(bb_re.transpose(0, 1, 2, 4, 3)), mat_pairs(bb_im.transpose(0, 1, 2, 4, 3)),
            mat_pairs(c_re.astype(F32)), mat_pairs(c_im.astype(F32)), al)


def _s5_shift_consts():
    n = S5_CW
    r = np.arange(n)[:, None]
    c = np.arange(n)[None, :]
    fwd = [(c == r + S5_P * s) for s in range(S5_L)]
    bwd = [(c == r - S5_P * (S5_L - 1 - s)) for s in range(S5_L)]
    return np.stack([np.concatenate(fwd, axis=1), np.concatenate(bwd, axis=1)]).astype(np.float32)


PREP_STEPS = 4
ADA_TN = N_MOD * D_MODEL // PREP_STEPS


def _prep_kernel(c_ref, wada_ref, bada_ref, pwr_ref, pwi_ref, br_ref, bi_ref, cr_ref, ci_ref, scat_ref,
                 mod_ref, m_ref, f_ref, e_ref, k_ref):
    cv = c_ref[...]
    mod_ref[0] = _dot((cv * jax.nn.sigmoid(cv)).astype(BF16), wada_ref[0].astype(BF16)) + bada_ref[0]
    step = pl.program_id(1)
    for qq in range(PREP_STEPS):
        pl.when(step == qq)(functools.partial(_s5_pair_ops, qq, pwr_ref, pwi_ref, br_ref, bi_ref, cr_ref,
                                              ci_ref, f_ref, e_ref, k_ref))
    pl.when(step == PREP_STEPS - 1)(functools.partial(_s5_toeplitz, scat_ref, m_ref, k_ref))


def _s5_pair_ops(qq, pwr_ref, pwi_ref, br_ref, bi_ref, cr_ref, ci_ref, f_ref, e_ref, k_ref):
    L = S5_L
    hp = lax.Precision.HIGHEST
    sw = 2 * S5_N
    per_step = S5_NPAIR // PREP_STEPS
    for j in range(per_step * qq, per_step * (qq + 1)):
        for d in range(2):
            pwr = pwr_ref[0, d, j]
            pwi = pwi_ref[0, d, j]
            for gl in range(2):
                g = 2 * j + gl
                br, bi = br_ref[0, d, j, gl], bi_ref[0, d, j, gl]
                cr, ci = cr_ref[0, d, j, gl], ci_ref[0, d, j, gl]
                ca_r, ca_i, f_r, f_i, e_r, e_i = [], [], [], [], [], []
                for t in range(L):
                    kk = t if d == 0 else L - 1 - t
                    pr, pi = pwr[kk:kk + 1], pwi[kk:kk + 1]
                    ca_r.append(cr * pr - ci * pi)
                    ca_i.append(cr * pi + ci * pr)
                    kf = L - 1 - t if d == 0 else t
                    pr, pi = pwr[kf:kf + 1], pwi[kf:kf + 1]
                    f_r.append(br * pr - bi * pi)
                    f_i.append(br * pi + bi * pr)
                    ke = t + 1 if d == 0 else L - t
                    pr, pi = pwr[ke:ke + 1], pwi[ke:ke + 1]
                    e_r.append(cr * pr - ci * pi)
                    e_i.append(-(cr * pi + ci * pr))
                cat = lambda xs: jnp.concatenate(xs, axis=0)
                k = _dot_nt(br, cat(ca_r), hp) - _dot_nt(bi, cat(ca_i), hp)
                k_ref[d, S5_P * g:S5_P * (g + 1), :] = k
                slots = lambda xs: cat([xs[(p - g) % L] for p in range(L)])
                rows = slice(S5_CW * gl, S5_CW * (gl + 1))
                f_ref[0, d, j, rows, 0:sw] = slots(f_r).astype(BF16)
                f_ref[0, d, j, rows, sw:2 * sw] = slots(f_i).astype(BF16)
                e_ref[0, d, 0, j, rows, :] = slots(e_r).astype(BF16)
                e_ref[0, d, 1, j, rows, :] = slots(e_i).astype(BF16)


def _s5_toeplitz(scat_ref, m_ref, k_ref):
    L = S5_L
    kf = k_ref[0].astype(BF16)
    kb = k_ref[1].astype(BF16)
    for s in range(L):
        cols = slice(S5_CW * s, S5_CW * (s + 1))
        res = _dot(kf, scat_ref[0, :, cols]) + _dot(kb, scat_ref[1, :, cols])
        for g in range(S5_G):
            blk = res[S5_P * g:S5_P * (g + 1), :]
            if g:
                blk = pltpu.roll(blk, S5_P * g, 1)
            r0 = S5_P * _s5_slot(g, s)
            m_ref[0, g, r0:r0 + S5_P, :] = blk.astype(BF16)


def _prep(cvecs, w_ada, b_ada, tables, scat):
    pwr, pwi, br, bi, cr, ci, _ = tables
    n = N_MOD * D_MODEL

    def lspec(shape):
        return pl.BlockSpec((1,) + shape, lambda l, q: (l,) + (0,) * len(shape))

    pw_shape = (2, S5_NPAIR, S5_L + 1, 2 * S5_N)
    mat_shape = (2, S5_NPAIR, 2, S5_P, 2 * S5_N)
    return pl.pallas_call(
        _prep_kernel,
        grid=(DEPTH, PREP_STEPS),
        in_specs=[pl.BlockSpec((8, D_MODEL), lambda l, q: (0, 0)),
                  pl.BlockSpec((1, D_MODEL, ADA_TN), lambda l, q: (l, 0, q)),
                  pl.BlockSpec((1, 1, ADA_TN), lambda l, q: (l, 0, q)),
                  lspec(pw_shape), lspec(pw_shape), lspec(mat_shape), lspec(mat_shape), lspec(mat_shape),
                  lspec(mat_shape), _const_spec(scat.shape)],
        out_specs=(pl.BlockSpec((1, 8, ADA_TN), lambda l, q: (l, 0, q)),
                   lspec((S5_G, S5_CW, S5_CW)), lspec((2, S5_NPAIR, 2 * S5_CW, 4 * S5_N)),
                   lspec((2, 2, S5_NPAIR, 2 * S5_CW, 2 * S5_N))),
        out_shape=(jax.ShapeDtypeStruct((DEPTH, 8, n), F32),
                   jax.ShapeDtypeStruct((DEPTH, S5_G, S5_CW, S5_CW), BF16),
                   jax.ShapeDtypeStruct((DEPTH, 2, S5_NPAIR, 2 * S5_CW, 4 * S5_N), BF16),
                   jax.ShapeDtypeStruct((DEPTH, 2, 2, S5_NPAIR, 2 * S5_CW, 2 * S5_N), BF16)),
        scratch_shapes=[pltpu.VMEM((2, S5_G * S5_P, S5_CW), F32)],
        compiler_params=pltpu.CompilerParams(vmem_limit_bytes=VMEM_LIMIT),
        name="param_prep",
    )(cvecs, w_ada, b_ada.reshape(DEPTH, 1, n), pwr, pwi, br, bi, cr, ci, scat)


def _s5_kernel(xlo_ref, xhi_ref, m_ref, f_ref, e_ref, al_ref, h0_ref, ylo_ref, yhi_ref, hfin_ref,
               uy_ref, zh_ref, seg_ref):
    R = S5_STEP_ROWS
    pw = 2 * S5_CW
    sw = 2 * S5_N
    gph = S5_G // 2
    step = pl.program_id(0)

    for half, x_ref in enumerate((xlo_ref, xhi_ref)):
        for t in range(S5_L):
            xt = x_ref[pl.ds(t, R, stride=S5_L), :]
            shift = (S5_P * t) % LANES
            xr = pltpu.roll(xt, shift, 1) if shift else xt
            for gl in range(gph):
                g = gph * half + gl
                src = (S5_P * gl + shift) % LANES
                lo = S5_CW * g + S5_P * _s5_slot(g, t)
                assert lo % LANES == src
                uy_ref[:, lo:lo + S5_P] = xr[:, src:src + S5_P]

    for j in range(S5_NPAIR):
        ub = uy_ref[:, pw * j:pw * (j + 1)].astype(BF16)
        for d in range(2):
            o = _dot(ub, f_ref[d, j])
            zh_ref[d, 0, j] = o[:, :sw]
            zh_ref[d, 1, j] = o[:, sw:]

    nseg = R // S5_SEG_NC
    zero = jnp.zeros((nseg, sw), F32)

    def local_scan(tiles):
        keys = [(d, j) for d in range(2) for j in tiles]

        def body(c, carry):
            out = []
            for k, (d, j) in enumerate(keys):
                hr, hi = carry[2 * k], carry[2 * k + 1]
                rows = pl.ds(c if d == 0 else S5_SEG_NC - 1 - c, nseg, stride=S5_SEG_NC)
                zr = zh_ref[d, 0, j, rows, :]
                zi = zh_ref[d, 1, j, rows, :]
                zh_ref[d, 0, j, rows, :] = hr
                zh_ref[d, 1, j, rows, :] = hi
                ar = al_ref[d, 0, 1:2, sw * j:sw * (j + 1)]
                ai = al_ref[d, 1, 1:2, sw * j:sw * (j + 1)]
                out.append(ar * hr - ai * hi + zr)
                out.append(ar * hi + ai * hr + zi)
            return tuple(out)

        fin = lax.fori_loop(0, S5_SEG_NC, body, (zero,) * (2 * len(keys)))
        return {key: (fin[2 * k], fin[2 * k + 1]) for k, key in enumerate(keys)}

    ends = {}
    for j0 in range(0, S5_NPAIR, 4):
        ends.update(local_scan(range(j0, j0 + 4)))

    @pl.when(step == 0)
    def _():
        for (d, j), (er, ei) in ends.items():
            hfin_ref[d, 0, :, sw * j:sw * (j + 1)] = er
            hfin_ref[d, 1, :, sw * j:sw * (j + 1)] = ei

    @pl.when(step == 1)
    def _():
        for (d, j), (er, ei) in ends.items():
            lanes = slice(sw * j, sw * (j + 1))
            ar = al_ref[d, 0, S5_SEG_NC:S5_SEG_NC + 1, lanes]
            ai = al_ref[d, 1, S5_SEG_NC:S5_SEG_NC + 1, lanes]
            for s in range(DEC_BATCH):
                hr = h0_ref[d, 0, s:s + 1, lanes]
                hi = h0_ref[d, 1, s:s + 1, lanes]
                for g in (range(S5_LAT_SEGS) if d == 0 else reversed(range(S5_LAT_SEGS))):
                    ps = S5_LAT_SEGS * s + g
                    seg_ref[0, ps:ps + 1, :] = hr
                    seg_ref[1, ps:ps + 1, :] = hi
                    hr, hi = (ar * hr - ai * hi + er[ps:ps + 1], ar * hi + ai * hr + ei[ps:ps + 1])
            sr = seg_ref[0]
            si = seg_ref[1]
            for c in range(S5_SEG_NC):
                k = c if d == 0 else S5_SEG_NC - 1 - c
                pr = al_ref[d, 0, k:k + 1, lanes]
                pi = al_ref[d, 1, k:k + 1, lanes]
                rows = pl.ds(c, nseg, stride=S5_SEG_NC)
                zh_ref[d, 0, j, rows, :] = zh_ref[d, 0, j, rows, :] + (pr * sr - pi * si)
                zh_ref[d, 1, j, rows, :] = zh_ref[d, 1, j, rows, :] + (pr * si + pi * sr)

    for j in range(S5_NPAIR):
        acc = None
        for d in range(2):
            for ri in range(2):
                t = _dot_nt(zh_ref[d, ri, j].astype(BF16), e_ref[d, ri, j])
                acc = t if acc is None else acc + t
        for gl in range(2):
            g = 2 * j + gl
            ug = uy_ref[:, S5_CW * g:S5_CW * (g + 1)].astype(BF16)
            uy_ref[:, S5_CW * g:S5_CW * (g + 1)] = acc[:, S5_CW * gl:S5_CW * (gl + 1)] + _dot(ug, m_ref[g])

    lane = lax.broadcasted_iota(jnp.int32, (R, LANES), 1)
    in_piece = [jnp.logical_and(lane >= S5_P * k, lane < S5_P * (k + 1)) for k in range(LANES // S5_P)]
    for half, y_ref in enumerate((ylo_ref, yhi_ref)):
        for t in range(S5_L):
            merged = None
            for gl in range(gph):
                g = gph * half + gl
                slot = _s5_slot(g, t)
                col = S5_CW * g + LANES * (S5_P * slot // LANES)
                v = uy_ref[:, col:col + LANES]
                merged = v if merged is None else jnp.where(in_piece[slot % (LANES // S5_P)], v, merged)
            shift = (LANES - S5_P * t % LANES) % LANES
            y_ref[pl.ds(t, R, stride=S5_L), :] = pltpu.roll(merged, shift, 1) if shift else merged


def _s5(xa_lo, xa_hi, m, f2, e2, al, h0, l):
    half = pl.BlockSpec((N_CTX, LANES), lambda s: (s, 0))
    return pl.pallas_call(
        _s5_kernel,
        grid=(2,),
        in_specs=[half, half,
                  _layer_spec(m.shape[1:], l, single=True), _layer_spec(f2.shape[1:], l, single=True),
                  _layer_spec(e2.shape[1:], l, single=True), _layer_spec(al.shape[1:], l),
                  _layer_spec(h0.shape[1:], l)],
        out_specs=(half, half, _const_spec((2, 2, BATCH, S5_SW))),
        out_shape=(jax.ShapeDtypeStruct((N_TOK, LANES), F32),
                   jax.ShapeDtypeStruct((N_TOK, LANES), F32),
                   jax.ShapeDtypeStruct((2, 2, BATCH, S5_SW), F32)),
        scratch_shapes=[pltpu.VMEM((S5_STEP_ROWS, S5_G * S5_CW), F32),
                        pltpu.VMEM((2, 2, S5_NPAIR, S5_STEP_ROWS, 2 * S5_N), F32),
                        pltpu.VMEM((2, S5_STEP_ROWS // S5_SEG_NC, 2 * S5_N), F32)],
        compiler_params=pltpu.CompilerParams(vmem_limit_bytes=VMEM_LIMIT),
        name="s5_mixer",
    )(xa_lo, xa_hi, m, f2, e2, al, h0)


HG_NLEV = int(math.log2(HG_L))
HG_W = HG_HEADS * HG_DK
HG_CPS = 8
HG_BLK = HG_CPS * HG_L
HG_NB_CTX = N_CTX // HG_BLK
HG_SPB = HG_BLK // SEQ
HG_CPQ = SEQ // HG_L
HG_NC_LAT = DEC_SEQ // HG_BLK
assert HG_SPB >= 1 and HG_SPB * SEQ == HG_BLK
assert HG_L == HG_DK


def _hgrn_consts():
    L = HG_L
    w = np.zeros((HG_NLEV + 2, L, L), np.float32)
    mask = np.zeros((HG_NLEV + 1, L, L), np.float32)
    for lev in range(HG_NLEV):
        blk = L >> lev
        half = blk // 2
        for t in range(L):
            p, o = divmod(t, blk)
            bd = p * blk + half - 1
            if o >= half:
                w[lev, t, bd + 1:t + 1] = 1.0
            else:
                w[lev, t, t + 1:bd + 1] = 1.0
        jj, ii = np.meshgrid(np.arange(L), np.arange(L), indexing='ij')
        mask[lev] = ((jj // blk == ii // blk) & (jj % blk >= half) & (ii % blk < half)).astype(np.float32)
    mask[HG_NLEV] = np.eye(L, dtype=np.float32)
    for t in range(L):
        w[HG_NLEV, t, :t + 1] = 1.0
        w[HG_NLEV + 1, t, t + 1:] = 1.0
    out = []
    for wd, md in ((w, mask), (w[:, ::-1, ::-1], mask[:, ::-1, ::-1])):
        wflat = wd.reshape((HG_NLEV + 2) * L, L)
        out.append((np.concatenate([wflat] * 3, axis=1), np.tile(md, (1, 1, HG_HEADS))))
    wcat = np.stack([out[0][0], out[1][0]])
    mask4 = np.stack([out[0][1], out[1][1]])
    hm = np.kron(np.eye(HG_HEADS, dtype=np.float32), np.ones((HG_DK, HG_DK), np.float32))
    return wcat, mask4, hm


def _hg_pos(i):
    is_ctx = i < HG_NB_CTX
    c = jnp.where(is_ctx, 0, (i - HG_NB_CTX) % HG_NC_LAT)
    nc = jnp.where(is_ctx, 1, HG_NC_LAT)
    return is_ctx, c, nc


def _hg_bwd_blk(i):
    _, c, nc = _hg_pos(i)
    return i + nc - 1 - 2 * c


def _hg_local(items, lb_ref, wcat_ref, mask_ref, hm):
    L = HG_L
    hmb = hm.astype(BF16)

    def bd4(x):
        return jnp.concatenate([x] * HG_HEADS, axis=0) * hmb

    kks, exs = [], []
    for d, q, z, v in items:
        lbp = lb_ref[d]
        sp = jnp.log1p(jnp.exp(-jnp.abs(z)))
        ls = jnp.minimum(z, 0.0) - sp
        kks.append(lbp[2:3] * jnp.exp(jnp.minimum(-z, 0.0) - sp))
        a = lbp[0:1]
        b = lbp[1:2] + ls
        logf = jnp.maximum(a, b) + jnp.log1p(jnp.exp(-jnp.abs(a - b)))
        p0 = logf.astype(BF16)
        r0 = logf - p0.astype(F32)
        p1 = r0.astype(BF16)
        p2 = (r0 - p1.astype(F32)).astype(BF16)
        exs.append(jnp.exp(_dot(wcat_ref[d], jnp.concatenate([p0, p1, p2], axis=0))))
    scs = [None] * len(items)
    for lev in range(HG_NLEV + 1):
        for n, (d, q, z, v) in enumerate(items):
            if lev < HG_NLEV:
                al = exs[n][L * lev:L * (lev + 1)]
                lhs = (q * al).astype(BF16)
                rhs = (kks[n] * al).astype(BF16)
            else:
                lhs = q.astype(BF16)
                rhs = kks[n].astype(BF16)
            t = _dot_nt(lhs, bd4(rhs)) * mask_ref[d, lev]
            scs[n] = t if scs[n] is None else scs[n] + t
    out = []
    for n, (d, q, z, v) in enumerate(items):
        vb = v.astype(BF16)
        o = _dot(scs[n].astype(BF16), bd4(vb))
        eq = exs[n][L * HG_NLEV:L * (HG_NLEV + 1)]
        qe = (q * eq).astype(BF16)
        ke = (kks[n] * exs[n][L * (HG_NLEV + 1):L * (HG_NLEV + 2)]).astype(BF16)
        g = eq[L - 1:L] if d == 0 else eq[0:1]
        out.append((o, qe, g, _dot_tn(vb, ke) * hm))
    return out


def _hgrn_kernel(inf_ref, inb_ref, lb_ref, w_ref, mask_ref, hm_ref, s0_ref, wup_ref, wdn_ref,
                 of_ref, ob_ref, sfin_ref, wupb_ref, wdnb_ref, st_ref):
    is_ctx, c, nc = _hg_pos(pl.program_id(0))
    wupb_ref[...] = wup_ref[...].astype(BF16)
    wdnb_ref[...] = wdn_ref[...].astype(BF16)

    hm = hm_ref[...]
    rows = [slice(HG_L * k, HG_L * (k + 1)) for k in range(HG_CPS)]
    lanes = lambda c: slice(GROUP_W * (c - COL_HQ), GROUP_W * (c - COL_HQ + 1))
    items = [(d, in_ref[r, lanes(COL_HQ)], in_ref[r, lanes(zc)], in_ref[r, lanes(COL_HI)])
             for d, (in_ref, zc) in enumerate(((inf_ref, COL_HF_FWD), (inb_ref, COL_HF_BWD))) for r in rows]
    loc = _hg_local(items, lb_ref, w_ref, mask_ref, hm)

    def chain(d, o_ref, st, chunks):
        for k in (chunks if d == 0 else reversed(chunks)):
            o, qe, g, upd = loc[HG_CPS * d + k]
            o_ref[rows[k], :] = o + _dot_nt(qe, st.astype(BF16))
            st = st * g + upd
        return st

    @pl.when(is_ctx)
    def _():
        for sq in range(HG_SPB):
            for d, o_ref in enumerate((of_ref, ob_ref)):
                st = chain(d, o_ref, jnp.zeros((HG_W, HG_W), F32), range(HG_CPQ * sq, HG_CPQ * (sq + 1)))
                s_kv = st.T
                for h in range(HG_HEADS):
                    sfin_ref[sq, d, h] = s_kv[HG_DK * h:HG_DK * (h + 1), HG_DK * h:HG_DK * (h + 1)]

    @pl.when(jnp.logical_not(is_ctx))
    def _():
        @pl.when(c == 0)
        def _():
            st_ref[...] = s0_ref[0]

        for d, o_ref in enumerate((of_ref, ob_ref)):
            st_ref[d] = chain(d, o_ref, st_ref[d], range(HG_CPS))


def _hgrn(hg_in, lbp, consts, s0, w_up, w_down, l):
    wcat, mask4, hm = consts
    steps = N_TOK // HG_BLK
    up_rows = D_MODEL // steps
    dn_rows = 2 * D_FF // steps

    return pl.pallas_call(
        _hgrn_kernel,
        grid=(N_TOK // HG_BLK,),
        in_specs=[pl.BlockSpec((HG_BLK, HG_IN_W), lambda i: (i, 0)),
                  pl.BlockSpec((HG_BLK, HG_IN_W), lambda i: (_hg_bwd_blk(i), 0)),
                  _layer_spec(lbp.shape[1:], l), _const_spec(wcat.shape), _const_spec(mask4.shape),
                  _const_spec(hm.shape),
                  pl.BlockSpec((None, 1, 2, HG_W, HG_W),
                               lambda i: (l, jnp.maximum(i - HG_NB_CTX, 0) // HG_NC_LAT, 0, 0, 0)),
                  pl.BlockSpec((None, up_rows, 2 * D_FF), lambda i: (l, i, 0)),
                  pl.BlockSpec((None, dn_rows, D_MODEL), lambda i: (l, i // 2, 0))],
        out_specs=(pl.BlockSpec((HG_BLK, HG_W), lambda i: (i, 0)),
                   pl.BlockSpec((HG_BLK, HG_W), lambda i: (_hg_bwd_blk(i), 0)),
                   pl.BlockSpec((HG_SPB, 2, HG_HEADS, HG_DK, HG_DK),
                                lambda i: (jnp.minimum(i, HG_NB_CTX - 1), 0, 0, 0, 0)),
                   pl.BlockSpec((up_rows, 2 * D_FF), lambda i: (i, 0)),
                   pl.BlockSpec((dn_rows, D_MODEL), lambda i: (i // 2, 0))),
        out_shape=(jax.ShapeDtypeStruct((N_TOK, HG_W), F32),
                   jax.ShapeDtypeStruct((N_TOK, HG_W), F32),
                   jax.ShapeDtypeStruct((BATCH, 2, HG_HEADS, HG_DK, HG_DK), F32),
                   jax.ShapeDtypeStruct((D_MODEL, 2 * D_FF), BF16),
                   jax.ShapeDtypeStruct((D_FF, D_MODEL), BF16)),
        scratch_shapes=[pltpu.VMEM((2, HG_W, HG_W), F32)],
        compiler_params=pltpu.CompilerParams(vmem_limit_bytes=VMEM_LIMIT),
        name="hgrn_mixer",
    )(hg_in, hg_in, lbp, wcat, mask4, hm, s0, w_up, w_down)


FN_CTX_SPG = 8
FN_LAT_TR = 512


def _dft_consts(t_len):
    n = GROUP_W // 4
    k = np.arange(n)
    ang = 2.0 * np.pi * ((k[:, None] * k[None, :]) % n) / n
    eye = np.eye(4)
    cs = np.concatenate([np.kron(eye, np.cos(ang)), np.kron(eye, np.sin(ang))], axis=1) / math.sqrt(n)
    t = np.arange(t_len)
    angt = 2.0 * np.pi * ((t[:, None] * t[None, :]) % t_len) / t_len
    dft = np.concatenate([np.cos(angt), -np.sin(angt)], axis=1) / math.sqrt(t_len)
    return cs.astype(np.float32), dft.astype(np.float32)


def _fnet_kernel(x_ref, cs_ref, dft_ref, w_ref, o_ref, r_ref, *, t_len, spg, tr):
    @pl.when(pl.program_id(1) == 0)
    def _():
        t = _dot(x_ref[...].astype(BF16), cs_ref[...])
        for s in range(spg):
            rows = slice(t_len * s, t_len * (s + 1))
            r_ref[s, 0:t_len, :] = t[rows, :GROUP_W].astype(BF16)
            r_ref[s, t_len:2 * t_len, :] = t[rows, GROUP_W:].astype(BF16)

    w = w_ref[...].astype(BF16)
    for s in range(spg):
        y = _dot(dft_ref[...], r_ref[s])
        o_ref[tr * s:tr * (s + 1), :] = _dot(y.astype(BF16), w)


def _fnet(xc, cs, dft, fn_w, l, *, n_seq, seq_len, tok0, spg, tr):
    nj = seq_len // tr
    assert spg == 1 or nj == 1
    gb0 = tok0 // (spg * seq_len)
    return pl.pallas_call(
        functools.partial(_fnet_kernel, t_len=seq_len, spg=spg, tr=tr),
        grid=(n_seq // spg, nj),
        in_specs=[
            pl.BlockSpec((spg * seq_len, GROUP_W), lambda g, j: (gb0 + g, 0)),
            _const_spec((GROUP_W, 2 * GROUP_W)),
            pl.BlockSpec((tr, 2 * seq_len), lambda g, j: (j, 0)),
            _layer_spec((GROUP_W, GROUP_W), l),
        ],
        out_specs=pl.BlockSpec((spg * tr, GROUP_W), lambda g, j: (g * nj + j, 0)),
        out_shape=jax.ShapeDtypeStruct((n_seq * seq_len, GROUP_W), F32),
        scratch_shapes=[pltpu.VMEM((spg, 2 * seq_len, GROUP_W), BF16)],
        compiler_params=pltpu.CompilerParams(vmem_limit_bytes=VMEM_LIMIT),
        name="fnet_mixer",
    )(xc, cs, dft, fn_w)


def _outproj_kernel(*refs):
    (op_ref, ylo_ref, yhi_ref, of_ref, ob_ref, ycc_ref, ycl_ref, mod_ref, d_ref,
     wglu_ref, gn_ref, bdm_ref, gmn_ref, gmw_ref, gmb_ref, gmh_ref, wout_ref, n2_ref,
     x1_ref, h2_ref, wob_ref) = refs[-21:]
    x_refs = refs[:-21]
    xa_ref, hgate_ref, gu_ref, gv_ref = (op_ref.at[:, GROUP_W * k:GROUP_W * (k + 1)] for k in range(4))

    @pl.when(pl.program_id(0) == 0)
    def _():
        wob_ref[...] = wout_ref[...].astype(BF16)

    gn = gn_ref[...]
    ys = jnp.concatenate([ylo_ref[...], yhi_ref[...]], axis=1)
    y5 = _gelu(ys + d_ref[...] * xa_ref[...])
    glu = jax.nn.sigmoid(_dot(y5.astype(BF16), wglu_ref[...].astype(BF16)))
    out_a = _rms(y5 * glu, gn[:, 0:GROUP_W])
    o = of_ref[...] + ob_ref[...]
    o2 = o * o
    o2h = o2.astype(BF16)
    o2l = (o2 - o2h.astype(F32)).astype(BF16)
    ms = _dot(o2h, bdm_ref[...]) + _dot(o2l, bdm_ref[...])
    hg = hgate_ref[...]
    out_b = o * lax.rsqrt(ms + EPS) * gn[:, GROUP_W:2 * GROUP_W] * (hg * jax.nn.sigmoid(hg))
    out_c = _rms(_x_value((ycc_ref, ycl_ref), TBP), gn[:, 2 * GROUP_W:3 * GROUP_W])
    gm_hm = gmh_ref[...]
    out_d = []
    for c in range(TBP // GM_CHUNK):
        rows = slice(GM_CHUNK * c, GM_CHUNK * (c + 1))
        gv = _rms(_gelu(gv_ref[rows, :]), gmn_ref[...]).astype(BF16)
        g4 = jnp.concatenate([gv] * GM_HEADS, axis=0) * gm_hm
        sp = _dot(gmw_ref[...], g4) + gmb_ref[...]
        out_d.append(_rms(_gelu(gu_ref[rows, :]) * sp, gn[:, 3 * GROUP_W:]))
    out_d = jnp.concatenate(out_d, axis=0)
    m = None
    for k, part in enumerate((out_a, out_b, out_c, out_d)):
        t = _dot(part.astype(BF16), wob_ref[GROUP_W * k:GROUP_W * (k + 1), :])
        m = t if m is None else m + t
    mod = mod_ref[...]
    g1 = mod[:, 2 * D_MODEL:3 * D_MODEL]
    sh2 = mod[:, 3 * D_MODEL:4 * D_MODEL]
    sc2 = mod[:, 4 * D_MODEL:5 * D_MODEL]
    x1 = _x_value(x_refs, TBP) + g1 * m
    x1_ref[...] = x1
    h2_ref[...] = (_rms(x1, n2_ref[...]) * (1.0 + sc2) + sh2).astype(BF16)


def _outproj(xs, op_in, ylo, yhi, of, ob, yc, mods, s5_d, wglu, gn, bdm, gm_norm_g, gm_w, gm_b, gm_hm, wout,
             norm2_g, l):
    def tok(width, col=0):
        return pl.BlockSpec((TBP, width), lambda i: (i, col))

    return pl.pallas_call(
        _outproj_kernel,
        grid=(N_TOK // TBP,),
        in_specs=_x_specs(len(xs) == 2, tb=TBP) + [
            tok(OP_IN_W), tok(LANES), tok(LANES), tok(GROUP_W), tok(GROUP_W), *_x_specs(True, GROUP_W, TBP),
            _mod_spec(l, TBP),
            _layer_spec((1, GROUP_W), l), _layer_spec((GROUP_W, GROUP_W), l), _layer_spec((1, D_MODEL), l),
            _const_spec((GROUP_W, GROUP_W)),
            _layer_spec((1, GROUP_W), l), _layer_spec((GM_CHUNK, GM_HEADS * GM_CHUNK), l),
            _layer_spec((GM_CHUNK, GROUP_W), l), _const_spec((GM_HEADS * GM_CHUNK, GROUP_W)),
            _layer_spec((D_MODEL, D_MODEL), l, single=True), _layer_spec((1, D_MODEL), l),
        ],
        out_specs=(tok(D_MODEL), tok(D_MODEL)),
        out_shape=(jax.ShapeDtypeStruct((N_TOK, D_MODEL), F32),
                   jax.ShapeDtypeStruct((N_TOK, D_MODEL), BF16)),
        scratch_shapes=[pltpu.VMEM((D_MODEL, D_MODEL), BF16)],
        compiler_params=pltpu.CompilerParams(vmem_limit_bytes=VMEM_LIMIT),
        name="out_proj",
    )(*xs, op_in, ylo, yhi, of, ob, *yc, mods, s5_d, wglu, gn, bdm, gm_norm_g, gm_w, gm_b,
      gm_hm, wout, norm2_g)


FF_SEG = GRID_W
FF_NSEG = TB // FF_SEG
FF_SEQ_STRIPS = SEQ // FF_SEG
FF_PERM_ROWS = FF_SEQ_STRIPS * FF_SEG
SUBLANES = 8
FF_NGRP = FF_SEG // SUBLANES
GELU_C0 = 0.7978845608028654
GELU_C1 = GELU_C0 * 0.044715


def _ffn_perm_consts():
    p = np.zeros((FF_PERM_ROWS, FF_PERM_ROWS), np.float32)
    for k in range(FF_SEQ_STRIPS):
        for t in range(FF_SEG):
            p[FF_SEG * k + SUBLANES * (t % FF_NGRP) + t // FF_NGRP, FF_SEG * k + t] = 1.0
    return p


def _ffn_kernel(x1_ref, h2_ref, mod_ref, perm_ref, wup_ref, cw_ref, cb_ref, wdn_ref, fg_ref, *rest, final):
    *o_refs, h2p_ref, z_ref, hid_ref = rest
    i = pl.program_id(0)
    joined = (i < NB_CTX).astype(F32)
    n_tiles = D_FF // FF_TILE
    sub = lax.broadcasted_iota(jnp.int32, (SUBLANES, 2 * FF_TILE), 0)
    is_first = sub == 0
    is_last = sub == SUBLANES - 1
    zero_grp = jnp.zeros((SUBLANES, 2 * FF_TILE), F32)
    for r in range(0, TB, FF_PERM_ROWS):
        h2p_ref[r:r + FF_PERM_ROWS, :] = _dot(perm_ref[...], h2_ref[r:r + FF_PERM_ROWS, :]).astype(BF16)

    def up(j):
        lo = FF_TILE * j
        for c, col in enumerate((lo, D_FF + lo)):
            z_ref[j % 2, :, FF_TILE * c:FF_TILE * (c + 1)] = _dot(h2p_ref[...], wup_ref[:, col:col + FF_TILE])

    def gate(j):
        slot = j % 2
        lo = FF_TILE * j
        w = jnp.concatenate([cw_ref[:, lo:lo + FF_TILE], cw_ref[:, D_FF + lo:D_FF + lo + FF_TILE]], axis=1)
        b = jnp.concatenate([cb_ref[:, lo:lo + FF_TILE], cb_ref[:, D_FF + lo:D_FF + lo + FF_TILE]], axis=1)

        def grp(k, v):
            r = FF_SEG * k + SUBLANES * v
            return z_ref[slot, r:r + SUBLANES, :]

        down = [pltpu.roll(grp(k, FF_NGRP - 1), 1, 0) for k in range(FF_NSEG)]
        up_ = [pltpu.roll(grp(k, 0), SUBLANES - 1, 0) for k in range(FF_NSEG)]
        strips = []
        for k in range(FF_NSEG):
            g = [grp(k, v) for v in range(FF_NGRP)]
            before = down[k - 1] * joined if k % FF_SEQ_STRIPS > 0 else zero_grp
            after = up_[k + 1] * joined if k % FF_SEQ_STRIPS < FF_SEQ_STRIPS - 1 else zero_grp
            zm1 = jnp.concatenate([jnp.where(is_first, before, down[k])] + g[:-1], axis=0)
            zp1 = jnp.concatenate(g[1:] + [jnp.where(is_last, after, up_[k])], axis=0)
            zc = b + zm1 * w[0:1] + jnp.concatenate(g, axis=0) * w[1:2] + zp1 * w[2:3]
            a = zc[:, :FF_TILE]
            t = jnp.tanh(a * (GELU_C0 + GELU_C1 * (a * a)))
            strips.append((a * (0.5 + 0.5 * t) * zc[:, FF_TILE:]).astype(BF16))
        hid_ref[:, lo:lo + FF_TILE] = jnp.concatenate(strips, axis=0)

    up(0)
    for j in range(n_tiles):
        if j + 1 < n_tiles:
            up(j + 1)
        gate(j)
    acc = _dot(hid_ref[...], wdn_ref[...])
    acc = jnp.swapaxes(acc.reshape(FF_NSEG, FF_NGRP, SUBLANES, D_MODEL), 1, 2).reshape(TB, D_MODEL)
    g2 = mod_ref[...][:, 5 * D_MODEL:6 * D_MODEL]
    x2 = x1_ref[...] + g2 * acc
    if final:
        y = _rms(x2, fg_ref[...])
        yp_ref, ys_ref = o_refs

        @pl.when(i < NB_CTX)
        def _():
            yp_ref[...] = y

        @pl.when(i >= NB_CTX)
        def _():
            ys_ref[...] = y
    else:
        o_refs[0][...] = x2


def _ffn(x1, h2, mods, perm, wup_bf, conv_w, conv_b, wdn_bf, final_g, l, *, final):
    if final:
        out_specs = tuple(_x_specs(True))
        out_shape = (jax.ShapeDtypeStruct((N_CTX, D_MODEL), F32), jax.ShapeDtypeStruct((N_LAT, D_MODEL), F32))
    else:
        out_specs = pl.BlockSpec((TB, D_MODEL), lambda i: (i, 0))
        out_shape = jax.ShapeDtypeStruct((N_TOK, D_MODEL), F32)
    return pl.pallas_call(
        functools.partial(_ffn_kernel, final=final),
        grid=(N_TOK // TB,),
        in_specs=[
            pl.BlockSpec((TB, D_MODEL), lambda i: (i, 0)),
            pl.BlockSpec((TB, D_MODEL), lambda i: (i, 0)),
            _mod_spec(l), _const_spec((FF_PERM_ROWS, FF_PERM_ROWS)),
            _const_spec((D_MODEL, 2 * D_FF), single=True), _layer_spec((3, 2 * D_FF), l),
            _layer_spec((1, 2 * D_FF), l), _const_spec((D_FF, D_MODEL), single=True),
            _const_spec((1, D_MODEL)),
        ],
        out_specs=out_specs,
        out_shape=out_shape,
        scratch_shapes=[pltpu.VMEM((TB, D_MODEL), BF16), pltpu.VMEM((2, TB, 2 * FF_TILE), F32),
                        pltpu.VMEM((TB, D_FF), BF16)],
        compiler_params=pltpu.CompilerParams(vmem_limit_bytes=VMEM_LIMIT),
        name="conv_ffn",
    )(x1, h2, mods, perm, wup_bf, conv_w, conv_b, wdn_bf, final_g.reshape(1, D_MODEL))


def kernel(x_prompt, x_sample, state_s5_re, state_s5_im, state_hgrn, c, c_ctx, w_ada, b_ada, norm1_g,
           norm2_g, w_in, s5_lam_re, s5_lam_im, s5_log_dt, s5_b_re, s5_b_im, s5_c_re, s5_c_im, s5_d,
           s5_w_glu, hg_lb_logits, fn_w, gm_norm_g, gm_ws, gm_bs, grp_norm_g, w_out, ffn_w_up,
           ffn_conv_w, ffn_conv_b, ffn_w_down, final_norm_g):
    xs = (x_prompt.reshape(N_CTX, D_MODEL), x_sample.reshape(N_LAT, D_MODEL))
    cvecs = jnp.concatenate([c_ctx[None], c, jnp.zeros((8 - 1 - DEC_BATCH, D_MODEL), F32)], axis=0)

    lb_p = jax.nn.softmax(hg_lb_logits.astype(F32), axis=0)
    lbs = jnp.maximum(jnp.cumsum(lb_p, axis=0) - lb_p[0], 0.0)
    lbps = jnp.stack([jnp.log(lbs), jnp.log1p(-lbs), 1.0 - lbs], axis=2)

    hg_consts_np = _hgrn_consts()
    hg_consts = (jnp.asarray(hg_consts_np[0], BF16), jnp.asarray(hg_consts_np[1], F32),
                 jnp.asarray(hg_consts_np[2], F32))
    cs_np, dft_ctx_np = _dft_consts(SEQ)
    _, dft_lat_np = _dft_consts(DEC_SEQ)
    cs = jnp.asarray(cs_np, F32).astype(BF16)
    dft_ctx = jnp.asarray(dft_ctx_np, F32).astype(BF16)
    dft_lat = jnp.asarray(dft_lat_np, F32).astype(BF16)
    gm_hm = jnp.asarray(np.kron(np.eye(GM_HEADS), np.ones((GM_CHUNK, GROUP_W // GM_HEADS))), BF16)
    bdm = jnp.asarray(np.kron(np.eye(HG_HEADS), np.ones((HG_DK, HG_DK))) / HG_DK, BF16)

    tables = _s5_tables(s5_lam_re, s5_lam_im, s5_log_dt, s5_b_re, s5_b_im, s5_c_re, s5_c_im)
    mods, s5_m, s5_f, s5_e = _prep(cvecs, w_ada, b_ada, tables, jnp.asarray(_s5_shift_consts(), BF16))
    mods = mods.reshape(DEPTH, 8, 1, N_MOD * D_MODEL)
    s5_al = tables[-1]
    s5_h0 = jnp.stack([state_s5_re, state_s5_im]).astype(F32).transpose(2, 3, 0, 1, 4, 5)
    s5_h0 = s5_h0.reshape(DEPTH, 2, 2, DEC_BATCH, S5_SW)
    hg_s0 = jnp.einsum('bldhkv,hg->lbdhvgk', state_hgrn.astype(F32), jnp.eye(HG_HEADS, dtype=F32))
    hg_s0 = hg_s0.reshape(DEPTH, DEC_BATCH, 2, HG_W, HG_W)

    def rows(a):
        return a.reshape(DEPTH, 1, a.shape[-1])

    norm1_r, norm2_r, gn_r, s5_d_r, gmn_r, cb_r = (rows(a) for a in (
        norm1_g, norm2_g, grp_norm_g, s5_d, gm_norm_g, ffn_conv_b))
    gm_w = gm_ws.transpose(0, 2, 1, 3).reshape(DEPTH, GM_CHUNK, GM_HEADS * GM_CHUNK).astype(BF16)
    gm_b = jnp.repeat(gm_bs.transpose(0, 2, 1), GROUP_W // GM_HEADS, axis=2)
    ff_perm = jnp.asarray(_ffn_perm_consts(), BF16)

    new_re, new_im, new_hg = [], [], []
    for l in range(DEPTH):
        hg_in, op_in, xc, xa_lo, xa_hi = _inproj(xs, mods, norm1_r, w_in, l)

        ylo, yhi, hfin = _s5(xa_lo, xa_hi, s5_m, s5_f, s5_e, s5_al, s5_h0, l)
        hfin = hfin.reshape(2, 2, BATCH, S5_G, S5_N).transpose(1, 2, 0, 3, 4)
        new_re.append(hfin[0])
        new_im.append(hfin[1])

        of, ob, sfin, wup_bf, wdn_bf = _hgrn(hg_in, lbps, hg_consts, hg_s0, ffn_w_up, ffn_w_down, l)
        new_hg.append(sfin)

        yc = (_fnet(xc, cs, dft_ctx, fn_w, l, n_seq=BATCH, seq_len=SEQ, tok0=0, spg=FN_CTX_SPG, tr=SEQ),
              _fnet(xc, cs, dft_lat, fn_w, l, n_seq=DEC_BATCH, seq_len=DEC_SEQ, tok0=N_CTX, spg=1,
                    tr=FN_LAT_TR))

        x1, h2 = _outproj(xs, op_in, ylo, yhi, of, ob, yc, mods, s5_d_r, s5_w_glu, gn_r, bdm, gmn_r, gm_w, gm_b,
                          gm_hm, w_out, norm2_r, l)
        res = _ffn(x1, h2, mods, ff_perm, wup_bf, ffn_conv_w, cb_r, wdn_bf, final_norm_g, l,
                   final=(l == DEPTH - 1))
        xs = (res,)

    y_prompt = res[0].reshape(BATCH, SEQ, D_MODEL)
    y_sample = res[1].reshape(DEC_BATCH, DEC_SEQ, D_MODEL)
    return (y_prompt, y_sample, jnp.stack(new_re, axis=1), jnp.stack(new_im, axis=1),
            jnp.stack(new_hg, axis=1))
```

```python
import functools
import math

import numpy as np
import jax
import jax.numpy as jnp
from jax import lax
from jax.experimental import pallas as pl
from jax.experimental.pallas import tpu as pltpu

D_MODEL = 1024
BATCH = 16
SEQ = 256
DEPTH = 2
DEC_BATCH = 2
DEC_SEQ = 2048
GRID_W = 64
GROUP_W = 256
S5_P = 16
S5_G = 16
S5_N = 64
HG_HEADS = 4
HG_DK = 64
GM_HEADS = 4
GM_CHUNK = 128
D_FF = 2816
N_MOD = 6
D_IN = 9 * GROUP_W
EPS = 1e-6
LAM_RE_MAX = -1e-4

N_CTX = BATCH * SEQ
N_LAT = DEC_BATCH * DEC_SEQ
N_TOK = N_CTX + N_LAT
TB = 512
TBP = 512
NB_CTX = N_CTX // TB
S5_L = 16
HG_L = 64
FF_TILE = 256
LANES = 128
MOD_ROWS = 8
V7X_VMEM_BYTES = 64 * 1024 * 1024
VMEM_LIMIT = V7X_VMEM_BYTES - 8 * 1024 * 1024

F32 = jnp.float32
BF16 = jnp.bfloat16

COL_XA, COL_HQ, COL_HF_FWD, COL_HF_BWD, COL_HI, COL_HGATE, COL_XC, COL_GU, COL_GV = range(9)
HG_IN_W = 4 * GROUP_W
OP_IN_W = 4 * GROUP_W


def _mod_row(i, tb):
    return jnp.where(i < N_CTX // tb, 0, 1 + (i - N_CTX // tb) // (DEC_SEQ // tb))


GELU_C0 = 0.7978845608028654
GELU_C1 = GELU_C0 * 0.044715


def _gelu(x):
    return x * (0.5 + 0.5 * jnp.tanh(x * (GELU_C0 + GELU_C1 * (x * x))))


def _rms(x, g):
    return x * lax.rsqrt(jnp.mean(x * x, axis=-1, keepdims=True) + EPS) * g


def _dot(a, b):
    return jnp.dot(a, b, preferred_element_type=F32)


def _dot_nt(a, b, precision=None):
    return lax.dot_general(a, b, (((1,), (1,)), ((), ())), precision=precision, preferred_element_type=F32)


def _dot_tn(a, b):
    return lax.dot_general(a, b, (((0,), (0,)), ((), ())), preferred_element_type=F32)


def _const_spec(shape, single=False):
    kw = {"pipeline_mode": pl.Buffered(1)} if single else {}
    return pl.BlockSpec(shape, lambda *_: (0,) * len(shape), **kw)


def _layer_spec(shape, l, single=False):
    kw = {"pipeline_mode": pl.Buffered(1)} if single else {}
    return pl.BlockSpec((None,) + tuple(shape), lambda *_: (l,) + (0,) * len(shape), **kw)


def _mod_spec(l, tb=TB):
    return pl.BlockSpec((None, None, 1, N_MOD * D_MODEL), lambda i: (l, _mod_row(i, tb), 0, 0))


def _x_specs(split, width=D_MODEL, tb=TB):
    nb_ctx = N_CTX // tb
    if split:
        return [pl.BlockSpec((tb, width), lambda i: (jnp.minimum(i, nb_ctx - 1), 0)),
                pl.BlockSpec((tb, width), lambda i: (jnp.maximum(i - nb_ctx, 0), 0))]
    return [pl.BlockSpec((tb, width), lambda i: (i, 0))]


def _x_value(x_refs, tb=TB):
    if len(x_refs) == 1:
        return x_refs[0][...]
    return jnp.where(pl.program_id(0) < N_CTX // tb, x_refs[0][...], x_refs[1][...])


def _inproj_kernel(*refs):
    *x_refs, mod_ref, g_ref, w_ref, hg_ref, op_ref, xc_ref, xalo_ref, xahi_ref, wb_ref = refs

    @pl.when(pl.program_id(0) == 0)
    def _():
        wb_ref[...] = w_ref[...].astype(BF16)

    mod = mod_ref[...]
    sh = mod[:, 0:D_MODEL]
    sc = mod[:, D_MODEL:2 * D_MODEL]
    h = _rms(_x_value(x_refs, TBP), g_ref[...]) * (1.0 + sc) + sh
    o = _dot(h.astype(BF16), wb_ref[...])
    col = lambda c, n=1: o[:, GROUP_W * c:GROUP_W * (c + n)]
    hg_ref[...] = col(COL_HQ, 4)
    op_ref[...] = jnp.concatenate([col(COL_XA), col(COL_HGATE), col(COL_GU, 2)], axis=1)
    xc_ref[...] = col(COL_XC)
    xalo_ref[...] = o[:, 0:LANES]
    xahi_ref[...] = o[:, LANES:2 * LANES]


def _inproj(xs, mods, norm_g, w_in, l):
    return pl.pallas_call(
        _inproj_kernel,
        grid=(N_TOK // TBP,),
        in_specs=_x_specs(len(xs) == 2, tb=TBP) + [
            _mod_spec(l, TBP),
            _layer_spec((1, D_MODEL), l),
            _layer_spec((D_MODEL, D_IN), l, single=True),
        ],
        out_specs=tuple(pl.BlockSpec((TBP, w), lambda i: (i, 0)) for w in (HG_IN_W, OP_IN_W, GROUP_W, LANES, LANES)),
        out_shape=tuple(jax.ShapeDtypeStruct((N_TOK, w), F32) for w in (HG_IN_W, OP_IN_W, GROUP_W, LANES, LANES)),
        scratch_shapes=[pltpu.VMEM((D_MODEL, D_IN), BF16)],
        compiler_params=pltpu.CompilerParams(vmem_limit_bytes=VMEM_LIMIT),
        name="in_proj",
    )(*xs, mods, norm_g, w_in)


S5_NPAIR = S5_G // 2
S5_CW = S5_L * S5_P
S5_SW = S5_G * S5_N
S5_STEP_ROWS = N_CTX // S5_L
S5_CTX_NC = SEQ // S5_L
S5_LAT_NC = DEC_SEQ // S5_L


def _s5_slot(g, t):
    return (t + g) % S5_L


def _s5_tables(lam_re, lam_im, log_dt, b_re, b_im, c_re, c_im):
    lr = jnp.minimum(lam_re.astype(F32), LAM_RE_MAX)
    li = lam_im.astype(F32)
    dt = jnp.exp(log_dt.astype(F32))[..., None]
    mag = jnp.exp(lr * dt)
    ang = li * dt
    ab_re = mag * jnp.cos(ang)
    ab_im = mag * jnp.sin(ang)
    den = lr * lr + li * li
    xr = ab_re - 1.0
    z_re = (xr * lr + ab_im * li) / den
    z_im = (ab_im * lr - xr * li) / den
    bb_re = z_re[..., None] * b_re - z_im[..., None] * b_im
    bb_im = z_re[..., None] * b_im + z_im[..., None] * b_re
    tau = jnp.arange(S5_L + 1, dtype=F32)[:, None, None, None, None]
    pm = jnp.exp(lr * dt * tau)
    pa = li * dt * tau
    pw_re = pm * jnp.cos(pa)
    pw_im = pm * jnp.sin(pa)
    eye2 = jnp.eye(2, dtype=F32)

    def pw_pairs(a):
        a = a.transpose(1, 2, 3, 0, 4).reshape(DEPTH, 2, S5_NPAIR, 2, S5_L + 1, S5_N)
        return a.transpose(0, 1, 2, 4, 3, 5).reshape(DEPTH, 2, S5_NPAIR, S5_L + 1, 2 * S5_N)

    def mat_pairs(a):
        a = a.reshape(DEPTH, 2, S5_NPAIR, 2, S5_P, S5_N)
        return jnp.einsum('ldjaqn,ab->ldjaqbn', a, eye2).reshape(DEPTH, 2, S5_NPAIR, 2, S5_P, 2 * S5_N)

    al = jnp.stack([pw_re[S5_L], pw_im[S5_L]], axis=2).reshape(DEPTH, 2, 2, 1, S5_SW)
    return (pw_pairs(pw_re), pw_pairs(pw_im),
            mat_pairs(bb_re.transpose(0, 1, 2, 4, 3)), mat_pairs(bb_im.transpose(0, 1, 2, 4, 3)),
            mat_pairs(c_re.astype(F32)), mat_pairs(c_im.astype(F32)), al)


def _s5_shift_consts():
    n = S5_CW
    r = np.arange(n)[:, None]
    c = np.arange(n)[None, :]
    fwd = [(c == r + S5_P * s) for s in range(S5_L)]
    bwd = [(c == r - S5_P * (S5_L - 1 - s)) for s in range(S5_L)]
    return np.stack([np.concatenate(fwd, axis=1), np.concatenate(bwd, axis=1)]).astype(np.float32)


PREP_STEPS = 4
ADA_TN = N_MOD * D_MODEL // PREP_STEPS


def _prep_kernel(c_ref, wada_ref, bada_ref, pwr_ref, pwi_ref, br_ref, bi_ref, cr_ref, ci_ref, scat_ref,
                 mod_ref, m_ref, f_ref, e_ref, k_ref):
    cv = c_ref[...]
    mod_ref[0] = _dot((cv * jax.nn.sigmoid(cv)).astype(BF16), wada_ref[0].astype(BF16)) + bada_ref[0]
    step = pl.program_id(1)
    for qq in range(PREP_STEPS):
        pl.when(step == qq)(functools.partial(_s5_pair_ops, qq, pwr_ref, pwi_ref, br_ref, bi_ref, cr_ref,
                                              ci_ref, f_ref, e_ref, k_ref))
    pl.when(step == PREP_STEPS - 1)(functools.partial(_s5_toeplitz, scat_ref, m_ref, k_ref))


def _s5_pair_ops(qq, pwr_ref, pwi_ref, br_ref, bi_ref, cr_ref, ci_ref, f_ref, e_ref, k_ref):
    L = S5_L
    hp = lax.Precision.HIGHEST
    sw = 2 * S5_N
    per_step = S5_NPAIR // PREP_STEPS
    for j in range(per_step * qq, per_step * (qq + 1)):
        for d in range(2):
            pwr = pwr_ref[0, d, j]
            pwi = pwi_ref[0, d, j]
            for gl in range(2):
                g = 2 * j + gl
                br, bi = br_ref[0, d, j, gl], bi_ref[0, d, j, gl]
                cr, ci = cr_ref[0, d, j, gl], ci_ref[0, d, j, gl]
                ca_r, ca_i, f_r, f_i, e_r, e_i = [], [], [], [], [], []
                for t in range(L):
                    kk = t if d == 0 else L - 1 - t
                    pr, pi = pwr[kk:kk + 1], pwi[kk:kk + 1]
                    ca_r.append(cr * pr - ci * pi)
                    ca_i.append(cr * pi + ci * pr)
                    kf = L - 1 - t if d == 0 else t
                    pr, pi = pwr[kf:kf + 1], pwi[kf:kf + 1]
                    f_r.append(br * pr - bi * pi)
                    f_i.append(br * pi + bi * pr)
                    ke = t + 1 if d == 0 else L - t
                    pr, pi = pwr[ke:ke + 1], pwi[ke:ke + 1]
                    e_r.append(cr * pr - ci * pi)
                    e_i.append(-(cr * pi + ci * pr))
                cat = lambda xs: jnp.concatenate(xs, axis=0)
                k = _dot_nt(br, cat(ca_r), hp) - _dot_nt(bi, cat(ca_i), hp)
                k_ref[d, S5_P * g:S5_P * (g + 1), :] = k
                slots = lambda xs: cat([xs[(p - g) % L] for p in range(L)])
                rows = slice(S5_CW * gl, S5_CW * (gl + 1))
                f_ref[0, d, j, rows, 0:sw] = slots(f_r).astype(BF16)
                f_ref[0, d, j, rows, sw:2 * sw] = slots(f_i).astype(BF16)
                e_ref[0, d, 0, j, rows, :] = slots(e_r).astype(BF16)
                e_ref[0, d, 1, j, rows, :] = slots(e_i).astype(BF16)


def _s5_toeplitz(scat_ref, m_ref, k_ref):
    L = S5_L
    kf = k_ref[0].astype(BF16)
    kb = k_ref[1].astype(BF16)
    for s in range(L):
        cols = slice(S5_CW * s, S5_CW * (s + 1))
        res = _dot(kf, scat_ref[0, :, cols]) + _dot(kb, scat_ref[1, :, cols])
        for g in range(S5_G):
            blk = res[S5_P * g:S5_P * (g + 1), :]
            if g:
                blk = pltpu.roll(blk, S5_P * g, 1)
            r0 = S5_P * _s5_slot(g, s)
            m_ref[0, g, r0:r0 + S5_P, :] = blk.astype(BF16)


def _prep(cvecs, w_ada, b_ada, tables, scat):
    pwr, pwi, br, bi, cr, ci, _ = tables
    n = N_MOD * D_MODEL

    def lspec(shape):
        return pl.BlockSpec((1,) + shape, lambda l, q: (l,) + (0,) * len(shape))

    pw_shape = (2, S5_NPAIR, S5_L + 1, 2 * S5_N)
    mat_shape = (2, S5_NPAIR, 2, S5_P, 2 * S5_N)
    return pl.pallas_call(
        _prep_kernel,
        grid=(DEPTH, PREP_STEPS),
        in_specs=[pl.BlockSpec((MOD_ROWS, D_MODEL), lambda l, q: (0, 0)),
                  pl.BlockSpec((1, D_MODEL, ADA_TN), lambda l, q: (l, 0, q)),
                  pl.BlockSpec((1, 1, ADA_TN), lambda l, q: (l, 0, q)),
                  lspec(pw_shape), lspec(pw_shape), lspec(mat_shape), lspec(mat_shape), lspec(mat_shape),
                  lspec(mat_shape), _const_spec(scat.shape)],
        out_specs=(pl.BlockSpec((1, MOD_ROWS, ADA_TN), lambda l, q: (l, 0, q)),
                   lspec((S5_G, S5_CW, S5_CW)), lspec((2, S5_NPAIR, 2 * S5_CW, 4 * S5_N)),
                   lspec((2, 2, S5_NPAIR, 2 * S5_CW, 2 * S5_N))),
        out_shape=(jax.ShapeDtypeStruct((DEPTH, MOD_ROWS, n), F32),
                   jax.ShapeDtypeStruct((DEPTH, S5_G, S5_CW, S5_CW), BF16),
                   jax.ShapeDtypeStruct((DEPTH, 2, S5_NPAIR, 2 * S5_CW, 4 * S5_N), BF16),
                   jax.ShapeDtypeStruct((DEPTH, 2, 2, S5_NPAIR, 2 * S5_CW, 2 * S5_N), BF16)),
        scratch_shapes=[pltpu.VMEM((2, S5_G * S5_P, S5_CW), F32)],
        compiler_params=pltpu.CompilerParams(vmem_limit_bytes=VMEM_LIMIT),
        name="param_prep",
    )(cvecs, w_ada, b_ada.reshape(DEPTH, 1, n), pwr, pwi, br, bi, cr, ci, scat)


def _s5_kernel(xlo_ref, xhi_ref, m_ref, f_ref, e_ref, al_ref, h0_ref, ylo_ref, yhi_ref, hfin_ref,
               uy_ref, zh_ref):
    R = S5_STEP_ROWS
    pw = 2 * S5_CW
    sw = 2 * S5_N
    gph = S5_G // 2
    step = pl.program_id(0)

    for half, x_ref in enumerate((xlo_ref, xhi_ref)):
        for t in range(S5_L):
            xt = x_ref[pl.ds(t, R, stride=S5_L), :]
            shift = (S5_P * t) % LANES
            xr = pltpu.roll(xt, shift, 1) if shift else xt
            for gl in range(gph):
                g = gph * half + gl
                src = (S5_P * gl + shift) % LANES
                lo = S5_CW * g + S5_P * _s5_slot(g, t)
                assert lo % LANES == src
                uy_ref[:, lo:lo + S5_P] = xr[:, src:src + S5_P]

    for j in range(S5_NPAIR):
        ub = uy_ref[:, pw * j:pw * (j + 1)].astype(BF16)
        for d in range(2):
            o = _dot(ub, f_ref[d, j])
            zh_ref[d, 0, :, sw * j:sw * (j + 1)] = o[:, :sw]
            zh_ref[d, 1, :, sw * j:sw * (j + 1)] = o[:, sw:]

    def scan_group(seq_rows, nc, init, fin_rows):
        chains = [(d, r0) for d in range(2) for r0 in seq_rows]

        def body(c, carry):
            out = []
            for k, (d, r0) in enumerate(chains):
                hr, hi = carry[2 * k], carry[2 * k + 1]
                r = r0 + (c if d == 0 else nc - 1 - c)
                zr = zh_ref[d, 0, pl.ds(r, 1), :]
                zi = zh_ref[d, 1, pl.ds(r, 1), :]
                zh_ref[d, 0, pl.ds(r, 1), :] = hr
                zh_ref[d, 1, pl.ds(r, 1), :] = hi
                ar = al_ref[d, 0]
                ai = al_ref[d, 1]
                out.append(ar * hr - ai * hi + zr)
                out.append(ar * hi + ai * hr + zi)
            return tuple(out)

        fin = lax.fori_loop(0, nc, body, tuple(init))
        if fin_rows is not None:
            for k, (d, _) in enumerate(chains):
                s = fin_rows[k % len(seq_rows)]
                hfin_ref[d, 0, s:s + 1, :] = fin[2 * k]
                hfin_ref[d, 1, s:s + 1, :] = fin[2 * k + 1]

    @pl.when(step == 0)
    def _():
        zero = jnp.zeros((1, S5_SW), F32)
        for s0 in range(0, BATCH, 4):
            seqs = list(range(s0, s0 + 4))
            scan_group([s * S5_CTX_NC for s in seqs], S5_CTX_NC, [zero] * 16, seqs)

    @pl.when(step == 1)
    def _():
        init = []
        for d in range(2):
            for s in range(DEC_BATCH):
                init.append(h0_ref[d, 0, s:s + 1, :])
                init.append(h0_ref[d, 1, s:s + 1, :])
        scan_group([s * S5_LAT_NC for s in range(DEC_BATCH)], S5_LAT_NC, init, None)

    for j in range(S5_NPAIR):
        acc = None
        for d in range(2):
            for ri in range(2):
                hb = zh_ref[d, ri, :, sw * j:sw * (j + 1)].astype(BF16)
                t = _dot_nt(hb, e_ref[d, ri, j])
                acc = t if acc is None else acc + t
        for gl in range(2):
            g = 2 * j + gl
            ug = uy_ref[:, S5_CW * g:S5_CW * (g + 1)].astype(BF16)
            uy_ref[:, S5_CW * g:S5_CW * (g + 1)] = acc[:, S5_CW * gl:S5_CW * (gl + 1)] + _dot(ug, m_ref[g])

    lane = lax.broadcasted_iota(jnp.int32, (R, LANES), 1)
    in_piece = [jnp.logical_and(lane >= S5_P * k, lane < S5_P * (k + 1)) for k in range(LANES // S5_P)]
    for half, y_ref in enumerate((ylo_ref, yhi_ref)):
        for t in range(S5_L):
            merged = None
            for gl in range(gph):
                g = gph * half + gl
                slot = _s5_slot(g, t)
                col = S5_CW * g + LANES * (S5_P * slot // LANES)
                v = uy_ref[:, col:col + LANES]
                merged = v if merged is None else jnp.where(in_piece[slot % (LANES // S5_P)], v, merged)
            shift = (LANES - S5_P * t % LANES) % LANES
            y_ref[pl.ds(t, R, stride=S5_L), :] = pltpu.roll(merged, shift, 1) if shift else merged


def _s5(xa_lo, xa_hi, m, f2, e2, al, h0, l):
    half = pl.BlockSpec((N_CTX, LANES), lambda s: (s, 0))
    return pl.pallas_call(
        _s5_kernel,
        grid=(2,),
        in_specs=[half, half,
                  _layer_spec(m.shape[1:], l, single=True), _layer_spec(f2.shape[1:], l, single=True),
                  _layer_spec(e2.shape[1:], l, single=True), _layer_spec(al.shape[1:], l),
                  _layer_spec(h0.shape[1:], l)],
        out_specs=(half, half, _const_spec((2, 2, BATCH, S5_SW))),
        out_shape=(jax.ShapeDtypeStruct((N_TOK, LANES), F32),
                   jax.ShapeDtypeStruct((N_TOK, LANES), F32),
                   jax.ShapeDtypeStruct((2, 2, BATCH, S5_SW), F32)),
        scratch_shapes=[pltpu.VMEM((S5_STEP_ROWS, S5_G * S5_CW), F32),
                        pltpu.VMEM((2, 2, S5_STEP_ROWS, S5_SW), F32)],
        compiler_params=pltpu.CompilerParams(vmem_limit_bytes=VMEM_LIMIT),
        name="s5_mixer",
    )(xa_lo, xa_hi, m, f2, e2, al, h0)


HG_NLEV = int(math.log2(HG_L))
HG_W = HG_HEADS * HG_DK
HG_CPS = 8
HG_BLK = HG_CPS * HG_L
HG_NB_CTX = N_CTX // HG_BLK
HG_SPB = HG_BLK // SEQ
HG_CPQ = SEQ // HG_L
HG_NC_LAT = DEC_SEQ // HG_BLK
assert HG_SPB >= 1 and HG_SPB * SEQ == HG_BLK
assert HG_L == HG_DK


def _hgrn_consts():
    L = HG_L
    w = np.zeros((HG_NLEV + 2, L, L), np.float32)
    mask = np.zeros((HG_NLEV + 1, L, L), np.float32)
    for lev in range(HG_NLEV):
        blk = L >> lev
        half = blk // 2
        for t in range(L):
            p, o = divmod(t, blk)
            bd = p * blk + half - 1
            if o >= half:
                w[lev, t, bd + 1:t + 1] = 1.0
            else:
                w[lev, t, t + 1:bd + 1] = 1.0
        jj, ii = np.meshgrid(np.arange(L), np.arange(L), indexing='ij')
        mask[lev] = ((jj // blk == ii // blk) & (jj % blk >= half) & (ii % blk < half)).astype(np.float32)
    mask[HG_NLEV] = np.eye(L, dtype=np.float32)
    for t in range(L):
        w[HG_NLEV, t, :t + 1] = 1.0
        w[HG_NLEV + 1, t, t + 1:] = 1.0
    out = []
    for wd, md in ((w, mask), (w[:, ::-1, ::-1], mask[:, ::-1, ::-1])):
        wflat = wd.reshape((HG_NLEV + 2) * L, L)
        out.append((np.concatenate([wflat] * 3, axis=1), np.tile(md, (1, 1, HG_HEADS))))
    wcat = np.stack([out[0][0], out[1][0]])
    mask4 = np.stack([out[0][1], out[1][1]])
    hm = np.kron(np.eye(HG_HEADS, dtype=np.float32), np.ones((HG_DK, HG_DK), np.float32))
    return wcat, mask4, hm


def _hg_pos(i):
    is_ctx = i < HG_NB_CTX
    c = jnp.where(is_ctx, 0, (i - HG_NB_CTX) % HG_NC_LAT)
    nc = jnp.where(is_ctx, 1, HG_NC_LAT)
    return is_ctx, c, nc


def _hg_bwd_blk(i):
    _, c, nc = _hg_pos(i)
    return i + nc - 1 - 2 * c


def _hg_local(items, lb_ref, wcat_ref, mask_ref, hm):
    L = HG_L
    hmb = hm.astype(BF16)

    def bd4(x):
        return jnp.concatenate([x] * HG_HEADS, axis=0) * hmb

    kks, exs = [], []
    for d, q, z, v in items:
        lbp = lb_ref[d]
        sp = jnp.log1p(jnp.exp(-jnp.abs(z)))
        ls = jnp.minimum(z, 0.0) - sp
        kks.append(lbp[2:3] * jnp.exp(jnp.minimum(-z, 0.0) - sp))
        a = lbp[0:1]
        b = lbp[1:2] + ls
        logf = jnp.maximum(a, b) + jnp.log1p(jnp.exp(-jnp.abs(a - b)))
        p0 = logf.astype(BF16)
        r0 = logf - p0.astype(F32)
        p1 = r0.astype(BF16)
        p2 = (r0 - p1.astype(F32)).astype(BF16)
        exs.append(jnp.exp(_dot(wcat_ref[d], jnp.concatenate([p0, p1, p2], axis=0))))
    scs = [None] * len(items)
    for lev in range(HG_NLEV + 1):
        for n, (d, q, z, v) in enumerate(items):
            if lev < HG_NLEV:
                al = exs[n][L * lev:L * (lev + 1)]
                lhs = (q * al).astype(BF16)
                rhs = (kks[n] * al).astype(BF16)
            else:
                lhs = q.astype(BF16)
                rhs = kks[n].astype(BF16)
            t = _dot_nt(lhs, bd4(rhs)) * mask_ref[d, lev]
            scs[n] = t if scs[n] is None else scs[n] + t
    out = []
    for n, (d, q, z, v) in enumerate(items):
        vb = v.astype(BF16)
        o = _dot(scs[n].astype(BF16), bd4(vb))
        eq = exs[n][L * HG_NLEV:L * (HG_NLEV + 1)]
        qe = (q * eq).astype(BF16)
        ke = (kks[n] * exs[n][L * (HG_NLEV + 1):L * (HG_NLEV + 2)]).astype(BF16)
        g = eq[L - 1:L] if d == 0 else eq[0:1]
        out.append((o, qe, g, _dot_tn(vb, ke) * hm))
    return out


def _hgrn_kernel(inf_ref, inb_ref, lb_ref, w_ref, mask_ref, hm_ref, s0_ref, wup_ref, wdn_ref,
                 of_ref, ob_ref, sfin_ref, wupb_ref, wdnb_ref, st_ref):
    is_ctx, c, nc = _hg_pos(pl.program_id(0))
    wupb_ref[...] = wup_ref[...].astype(BF16)
    wdnb_ref[...] = wdn_ref[...].astype(BF16)

    hm = hm_ref[...]
    rows = [slice(HG_L * k, HG_L * (k + 1)) for k in range(HG_CPS)]
    lanes = lambda c: slice(GROUP_W * (c - COL_HQ), GROUP_W * (c - COL_HQ + 1))
    items = [(d, in_ref[r, lanes(COL_HQ)], in_ref[r, lanes(zc)], in_ref[r, lanes(COL_HI)])
             for d, (in_ref, zc) in enumerate(((inf_ref, COL_HF_FWD), (inb_ref, COL_HF_BWD))) for r in rows]
    loc = _hg_local(items, lb_ref, w_ref, mask_ref, hm)

    def chain(d, o_ref, st, chunks):
        for k in (chunks if d == 0 else reversed(chunks)):
            o, qe, g, upd = loc[HG_CPS * d + k]
            o_ref[rows[k], :] = o + _dot_nt(qe, st.astype(BF16))
            st = st * g + upd
        return st

    @pl.when(is_ctx)
    def _():
        for sq in range(HG_SPB):
            for d, o_ref in enumerate((of_ref, ob_ref)):
                st = chain(d, o_ref, jnp.zeros((HG_W, HG_W), F32), range(HG_CPQ * sq, HG_CPQ * (sq + 1)))
                s_kv = st.T
                for h in range(HG_HEADS):
                    sfin_ref[sq, d, h] = s_kv[HG_DK * h:HG_DK * (h + 1), HG_DK * h:HG_DK * (h + 1)]

    @pl.when(jnp.logical_not(is_ctx))
    def _():
        @pl.when(c == 0)
        def _():
            st_ref[...] = s0_ref[0]

        for d, o_ref in enumerate((of_ref, ob_ref)):
            st_ref[d] = chain(d, o_ref, st_ref[d], range(HG_CPS))


def _hgrn(hg_in, lbp, consts, s0, w_up, w_down, l):
    wcat, mask4, hm = consts
    steps = N_TOK // HG_BLK
    up_rows = D_MODEL // steps
    dn_rows = 2 * D_FF // steps

    return pl.pallas_call(
        _hgrn_kernel,
        grid=(N_TOK // HG_BLK,),
        in_specs=[pl.BlockSpec((HG_BLK, HG_IN_W), lambda i: (i, 0)),
                  pl.BlockSpec((HG_BLK, HG_IN_W), lambda i: (_hg_bwd_blk(i), 0)),
                  _layer_spec(lbp.shape[1:], l), _const_spec(wcat.shape), _const_spec(mask4.shape),
                  _const_spec(hm.shape),
                  pl.BlockSpec((None, 1, 2, HG_W, HG_W),
                               lambda i: (l, jnp.maximum(i - HG_NB_CTX, 0) // HG_NC_LAT, 0, 0, 0)),
                  pl.BlockSpec((None, up_rows, 2 * D_FF), lambda i: (l, i, 0)),
                  pl.BlockSpec((None, dn_rows, D_MODEL), lambda i: (l, i // 2, 0))],
        out_specs=(pl.BlockSpec((HG_BLK, HG_W), lambda i: (i, 0)),
                   pl.BlockSpec((HG_BLK, HG_W), lambda i: (_hg_bwd_blk(i), 0)),
                   pl.BlockSpec((HG_SPB, 2, HG_HEADS, HG_DK, HG_DK),
                                lambda i: (jnp.minimum(i, HG_NB_CTX - 1), 0, 0, 0, 0)),
                   pl.BlockSpec((up_rows, 2 * D_FF), lambda i: (i, 0)),
                   pl.BlockSpec((dn_rows, D_MODEL), lambda i: (i // 2, 0))),
        out_shape=(jax.ShapeDtypeStruct((N_TOK, HG_W), F32),
                   jax.ShapeDtypeStruct((N_TOK, HG_W), F32),
                   jax.ShapeDtypeStruct((BATCH, 2, HG_HEADS, HG_DK, HG_DK), F32),
                   jax.ShapeDtypeStruct((D_MODEL, 2 * D_FF), BF16),
                   jax.ShapeDtypeStruct((D_FF, D_MODEL), BF16)),
        scratch_shapes=[pltpu.VMEM((2, HG_W, HG_W), F32)],
        compiler_params=pltpu.CompilerParams(vmem_limit_bytes=VMEM_LIMIT),
        name="hgrn_mixer",
    )(hg_in, hg_in, lbp, wcat, mask4, hm, s0, w_up, w_down)


FN_CTX_SPG = 4
FN_LAT_TR = 512


def _dft_consts(t_len):
    n = GROUP_W // 4
    k = np.arange(n)
    ang = 2.0 * np.pi * ((k[:, None] * k[None, :]) % n) / n
    eye = np.eye(4)
    cs = np.concatenate([np.kron(eye, np.cos(ang)), np.kron(eye, np.sin(ang))], axis=1) / math.sqrt(n)
    t = np.arange(t_len)
    angt = 2.0 * np.pi * ((t[:, None] * t[None, :]) % t_len) / t_len
    dft = np.concatenate([np.cos(angt), -np.sin(angt)], axis=1) / math.sqrt(t_len)
    return cs.astype(np.float32), dft.astype(np.float32)


def _fnet_kernel(x_ref, cs_ref, dft_ref, w_ref, o_ref, r_ref, *, t_len, spg, tr):
    @pl.when(pl.program_id(1) == 0)
    def _():
        t = _dot(x_ref[...].astype(BF16), cs_ref[...])
        for s in range(spg):
            rows = slice(t_len * s, t_len * (s + 1))
            r_ref[s, 0:t_len, :] = t[rows, :GROUP_W].astype(BF16)
            r_ref[s, t_len:2 * t_len, :] = t[rows, GROUP_W:].astype(BF16)

    w = w_ref[...].astype(BF16)
    for s in range(spg):
        y = _dot(dft_ref[...], r_ref[s])
        o_ref[s] = _dot(y.astype(BF16), w)


def _fnet(xc, cs, dft, fn_w, l, *, n_seq, seq_len, tok0, spg, tr):
    nj = seq_len // tr
    gb0 = tok0 // (spg * seq_len)
    out = pl.pallas_call(
        functools.partial(_fnet_kernel, t_len=seq_len, spg=spg, tr=tr),
        grid=(n_seq // spg, nj),
        in_specs=[
            pl.BlockSpec((spg * seq_len, GROUP_W), lambda g, j: (gb0 + g, 0)),
            _const_spec((GROUP_W, 2 * GROUP_W)),
            pl.BlockSpec((tr, 2 * seq_len), lambda g, j: (j, 0)),
            _layer_spec((GROUP_W, GROUP_W), l),
        ],
        out_specs=pl.BlockSpec((None, spg, tr, GROUP_W), lambda g, j: (g, 0, j, 0)),
        out_shape=jax.ShapeDtypeStruct((n_seq // spg, spg, seq_len, GROUP_W), F32),
        scratch_shapes=[pltpu.VMEM((spg, 2 * seq_len, GROUP_W), BF16)],
        compiler_params=pltpu.CompilerParams(vmem_limit_bytes=VMEM_LIMIT),
        name="fnet_mixer",
    )(xc, cs, dft, fn_w)
    return out.reshape(n_seq * seq_len, GROUP_W)


def _outproj_kernel(*refs):
    (op_ref, ylo_ref, yhi_ref, of_ref, ob_ref, ycc_ref, ycl_ref, mod_ref, d_ref,
     wglu_ref, gn_ref, bdm_ref, gmn_ref, gmw_ref, gmb_ref, gmh_ref, wout_ref, n2_ref,
     x1_ref, h2_ref, wob_ref) = refs[-21:]
    x_refs = refs[:-21]
    xa_ref, hgate_ref, gu_ref, gv_ref = (op_ref.at[:, GROUP_W * k:GROUP_W * (k + 1)] for k in range(4))

    @pl.when(pl.program_id(0) == 0)
    def _():
        wob_ref[...] = wout_ref[...].astype(BF16)

    gn = gn_ref[...]
    ys = jnp.concatenate([ylo_ref[...], yhi_ref[...]], axis=1)
    y5 = _gelu(ys + d_ref[...] * xa_ref[...])
    glu = jax.nn.sigmoid(_dot(y5.astype(BF16), wglu_ref[...].astype(BF16)))
    out_a = _rms(y5 * glu, gn[:, 0:GROUP_W])
    o = of_ref[...] + ob_ref[...]
    o2 = o * o
    o2h = o2.astype(BF16)
    o2l = (o2 - o2h.astype(F32)).astype(BF16)
    ms = _dot(o2h, bdm_ref[...]) + _dot(o2l, bdm_ref[...])
    hg = hgate_ref[...]
    out_b = o * lax.rsqrt(ms + EPS) * gn[:, GROUP_W:2 * GROUP_W] * (hg * jax.nn.sigmoid(hg))
    out_c = _rms(_x_value((ycc_ref, ycl_ref), TBP), gn[:, 2 * GROUP_W:3 * GROUP_W])
    gm_hm = gmh_ref[...]
    out_d = []
    for c in range(TBP // GM_CHUNK):
        rows = slice(GM_CHUNK * c, GM_CHUNK * (c + 1))
        gv = _rms(_gelu(gv_ref[rows, :]), gmn_ref[...]).astype(BF16)
        g4 = jnp.concatenate([gv] * GM_HEADS, axis=0) * gm_hm
        sp = _dot(gmw_ref[...], g4) + gmb_ref[...]
        out_d.append(_rms(_gelu(gu_ref[rows, :]) * sp, gn[:, 3 * GROUP_W:]))
    out_d = jnp.concatenate(out_d, axis=0)
    m = None
    for k, part in enumerate((out_a, out_b, out_c, out_d)):
        t = _dot(part.astype(BF16), wob_ref[GROUP_W * k:GROUP_W * (k + 1), :])
        m = t if m is None else m + t
    mod = mod_ref[...]
    g1 = mod[:, 2 * D_MODEL:3 * D_MODEL]
    sh2 = mod[:, 3 * D_MODEL:4 * D_MODEL]
    sc2 = mod[:, 4 * D_MODEL:5 * D_MODEL]
    x1 = _x_value(x_refs, TBP) + g1 * m
    x1_ref[...] = x1
    h2_ref[...] = (_rms(x1, n2_ref[...]) * (1.0 + sc2) + sh2).astype(BF16)


def _outproj(xs, op_in, ylo, yhi, of, ob, yc, mods, s5_d, wglu, gn, bdm, gm_norm_g, gm_w, gm_b, gm_hm, wout,
             norm2_g, l):
    def tok(width, col=0):
        return pl.BlockSpec((TBP, width), lambda i: (i, col))

    return pl.pallas_call(
        _outproj_kernel,
        grid=(N_TOK // TBP,),
        in_specs=_x_specs(len(xs) == 2, tb=TBP) + [
            tok(OP_IN_W), tok(LANES), tok(LANES), tok(GROUP_W), tok(GROUP_W), *_x_specs(True, GROUP_W, TBP),
            _mod_spec(l, TBP),
            _layer_spec((1, GROUP_W), l), _layer_spec((GROUP_W, GROUP_W), l), _layer_spec((1, D_MODEL), l),
            _const_spec((GROUP_W, GROUP_W)),
            _layer_spec((1, GROUP_W), l), _layer_spec((GM_CHUNK, GM_HEADS * GM_CHUNK), l),
            _layer_spec((GM_CHUNK, GROUP_W), l), _const_spec((GM_HEADS * GM_CHUNK, GROUP_W)),
            _layer_spec((D_MODEL, D_MODEL), l, single=True), _layer_spec((1, D_MODEL), l),
        ],
        out_specs=(tok(D_MODEL), tok(D_MODEL)),
        out_shape=(jax.ShapeDtypeStruct((N_TOK, D_MODEL), F32),
                   jax.ShapeDtypeStruct((N_TOK, D_MODEL), BF16)),
        scratch_shapes=[pltpu.VMEM((D_MODEL, D_MODEL), BF16)],
        compiler_params=pltpu.CompilerParams(vmem_limit_bytes=VMEM_LIMIT),
        name="out_proj",
    )(*xs, op_in, ylo, yhi, of, ob, *yc, mods, s5_d, wglu, gn, bdm, gm_norm_g, gm_w, gm_b,
      gm_hm, wout, norm2_g)


FF_SEG = GRID_W
FF_NSEG = TB // FF_SEG
FF_SEQ_STRIPS = SEQ // FF_SEG
FF_PERM_ROWS = FF_SEQ_STRIPS * FF_SEG
SUBLANES = 8
FF_NGRP = FF_SEG // SUBLANES


def _ffn_perm_consts():
    p = np.zeros((FF_PERM_ROWS, FF_PERM_ROWS), np.float32)
    for k in range(FF_SEQ_STRIPS):
        for t in range(FF_SEG):
            p[FF_SEG * k + SUBLANES * (t % FF_NGRP) + t // FF_NGRP, FF_SEG * k + t] = 1.0
    return p


def _ffn_kernel(x1_ref, h2_ref, mod_ref, perm_ref, wup_ref, cw_ref, cb_ref, wdn_ref, fg_ref, *rest, final):
    *o_refs, h2p_ref, z_ref, hid_ref = rest
    i = pl.program_id(0)
    joined = (i < NB_CTX).astype(F32)
    n_tiles = D_FF // FF_TILE
    sub = lax.broadcasted_iota(jnp.int32, (SUBLANES, 2 * FF_TILE), 0)
    is_first = sub == 0
    is_last = sub == SUBLANES - 1
    zero_grp = jnp.zeros((SUBLANES, 2 * FF_TILE), F32)
    for r in range(0, TB, FF_PERM_ROWS):
        h2p_ref[r:r + FF_PERM_ROWS, :] = _dot(perm_ref[...], h2_ref[r:r + FF_PERM_ROWS, :]).astype(BF16)

    def up(j):
        lo = FF_TILE * j
        for c, col in enumerate((lo, D_FF + lo)):
            z_ref[j % 2, :, FF_TILE * c:FF_TILE * (c + 1)] = _dot(h2p_ref[...], wup_ref[:, col:col + FF_TILE])

    def gate(j):
        slot = j % 2
        lo = FF_TILE * j
        w = jnp.concatenate([cw_ref[:, lo:lo + FF_TILE], cw_ref[:, D_FF + lo:D_FF + lo + FF_TILE]], axis=1)
        b = jnp.concatenate([cb_ref[:, lo:lo + FF_TILE], cb_ref[:, D_FF + lo:D_FF + lo + FF_TILE]], axis=1)

        def grp(k, v):
            r = FF_SEG * k + SUBLANES * v
            return z_ref[slot, r:r + SUBLANES, :]

        down = [pltpu.roll(grp(k, FF_NGRP - 1), 1, 0) for k in range(FF_NSEG)]
        up_ = [pltpu.roll(grp(k, 0), SUBLANES - 1, 0) for k in range(FF_NSEG)]
        strips = []
        for k in range(FF_NSEG):
            g = [grp(k, v) for v in range(FF_NGRP)]
            before = down[k - 1] * joined if k % FF_SEQ_STRIPS > 0 else zero_grp
            after = up_[k + 1] * joined if k % FF_SEQ_STRIPS < FF_SEQ_STRIPS - 1 else zero_grp
            zm1 = jnp.concatenate([jnp.where(is_first, before, down[k])] + g[:-1], axis=0)
            zp1 = jnp.concatenate(g[1:] + [jnp.where(is_last, after, up_[k])], axis=0)
            zc = b + zm1 * w[0:1] + jnp.concatenate(g, axis=0) * w[1:2] + zp1 * w[2:3]
            strips.append((_gelu(zc[:, :FF_TILE]) * zc[:, FF_TILE:]).astype(BF16))
        hid_ref[:, lo:lo + FF_TILE] = jnp.concatenate(strips, axis=0)

    up(0)
    for j in range(n_tiles):
        if j + 1 < n_tiles:
            up(j + 1)
        gate(j)
    acc = _dot(hid_ref[...], wdn_ref[...])
    acc = jnp.swapaxes(acc.reshape(FF_NSEG, FF_NGRP, SUBLANES, D_MODEL), 1, 2).reshape(TB, D_MODEL)
    g2 = mod_ref[...][:, 5 * D_MODEL:6 * D_MODEL]
    x2 = x1_ref[...] + g2 * acc
    if final:
        y = _rms(x2, fg_ref[...])
        yp_ref, ys_ref = o_refs

        @pl.when(i < NB_CTX)
        def _():
            yp_ref[...] = y

        @pl.when(i >= NB_CTX)
        def _():
            ys_ref[...] = y
    else:
        o_refs[0][...] = x2


def _ffn(x1, h2, mods, perm, wup_bf, conv_w, conv_b, wdn_bf, final_g, l, *, final):
    if final:
        out_specs = tuple(_x_specs(True))
        out_shape = (jax.ShapeDtypeStruct((N_CTX, D_MODEL), F32), jax.ShapeDtypeStruct((N_LAT, D_MODEL), F32))
    else:
        out_specs = pl.BlockSpec((TB, D_MODEL), lambda i: (i, 0))
        out_shape = jax.ShapeDtypeStruct((N_TOK, D_MODEL), F32)
    return pl.pallas_call(
        functools.partial(_ffn_kernel, final=final),
        grid=(N_TOK // TB,),
        in_specs=[
            pl.BlockSpec((TB, D_MODEL), lambda i: (i, 0)),
            pl.BlockSpec((TB, D_MODEL), lambda i: (i, 0)),
            _mod_spec(l), _const_spec((FF_PERM_ROWS, FF_PERM_ROWS)),
            _const_spec((D_MODEL, 2 * D_FF), single=True), _layer_spec((3, 2 * D_FF), l),
            _layer_spec((1, 2 * D_FF), l), _const_spec((D_FF, D_MODEL), single=True),
            _const_spec((1, D_MODEL)),
        ],
        out_specs=out_specs,
        out_shape=out_shape,
        scratch_shapes=[pltpu.VMEM((TB, D_MODEL), BF16), pltpu.VMEM((2, TB, 2 * FF_TILE), F32),
                        pltpu.VMEM((TB, D_FF), BF16)],
        compiler_params=pltpu.CompilerParams(vmem_limit_bytes=VMEM_LIMIT),
        name="conv_ffn",
    )(x1, h2, mods, perm, wup_bf, conv_w, conv_b, wdn_bf, final_g.reshape(1, D_MODEL))


def kernel(x_prompt, x_sample, state_s5_re, state_s5_im, state_hgrn, c, c_ctx, w_ada, b_ada, norm1_g,
           norm2_g, w_in, s5_lam_re, s5_lam_im, s5_log_dt, s5_b_re, s5_b_im, s5_c_re, s5_c_im, s5_d,
           s5_w_glu, hg_lb_logits, fn_w, gm_norm_g, gm_ws, gm_bs, grp_norm_g, w_out, ffn_w_up,
           ffn_conv_w, ffn_conv_b, ffn_w_down, final_norm_g):
    xs = (x_prompt.reshape(N_CTX, D_MODEL), x_sample.reshape(N_LAT, D_MODEL))
    cvecs = jnp.concatenate([c_ctx[None], c, jnp.zeros((MOD_ROWS - 1 - DEC_BATCH, D_MODEL), F32)], axis=0)

    lb_p = jax.nn.softmax(hg_lb_logits.astype(F32), axis=0)
    lbs = jnp.maximum(jnp.cumsum(lb_p, axis=0) - lb_p[0], 0.0)
    lbps = jnp.stack([jnp.log(lbs), jnp.log1p(-lbs), 1.0 - lbs], axis=2)

    hg_consts_np = _hgrn_consts()
    hg_consts = (jnp.asarray(hg_consts_np[0], BF16), jnp.asarray(hg_consts_np[1], F32),
                 jnp.asarray(hg_consts_np[2], F32))
    cs_np, dft_ctx_np = _dft_consts(SEQ)
    _, dft_lat_np = _dft_consts(DEC_SEQ)
    cs = jnp.asarray(cs_np, F32).astype(BF16)
    dft_ctx = jnp.asarray(dft_ctx_np, F32).astype(BF16)
    dft_lat = jnp.asarray(dft_lat_np, F32).astype(BF16)
    gm_hm = jnp.asarray(np.kron(np.eye(GM_HEADS), np.ones((GM_CHUNK, GROUP_W // GM_HEADS))), BF16)
    bdm = jnp.asarray(np.kron(np.eye(HG_HEADS), np.ones((HG_DK, HG_DK))) / HG_DK, BF16)

    tables = _s5_tables(s5_lam_re, s5_lam_im, s5_log_dt, s5_b_re, s5_b_im, s5_c_re, s5_c_im)
    mods, s5_m, s5_f, s5_e = _prep(cvecs, w_ada, b_ada, tables, jnp.asarray(_s5_shift_consts(), BF16))
    mods = mods.reshape(DEPTH, MOD_ROWS, 1, N_MOD * D_MODEL)
    s5_al = tables[-1]
    s5_h0 = jnp.stack([state_s5_re, state_s5_im]).astype(F32).transpose(2, 3, 0, 1, 4, 5)
    s5_h0 = s5_h0.reshape(DEPTH, 2, 2, DEC_BATCH, S5_SW)
    hg_s0 = jnp.einsum('bldhkv,hg->lbdhvgk', state_hgrn.astype(F32), jnp.eye(HG_HEADS, dtype=F32))
    hg_s0 = hg_s0.reshape(DEPTH, DEC_BATCH, 2, HG_W, HG_W)

    def rows(a):
        return a.reshape(DEPTH, 1, a.shape[-1])

    norm1_r, norm2_r, gn_r, s5_d_r, gmn_r, cb_r = (rows(a) for a in (
        norm1_g, norm2_g, grp_norm_g, s5_d, gm_norm_g, ffn_conv_b))
    gm_w = gm_ws.transpose(0, 2, 1, 3).reshape(DEPTH, GM_CHUNK, GM_HEADS * GM_CHUNK).astype(BF16)
    gm_b = jnp.repeat(gm_bs.transpose(0, 2, 1), GROUP_W // GM_HEADS, axis=2)
    ff_perm = jnp.asarray(_ffn_perm_consts(), BF16)

    new_re, new_im, new_hg = [], [], []
    for l in range(DEPTH):
        hg_in, op_in, xc, xa_lo, xa_hi = _inproj(xs, mods, norm1_r, w_in, l)

        ylo, yhi, hfin = _s5(xa_lo, xa_hi, s5_m, s5_f, s5_e, s5_al, s5_h0, l)
        hfin = hfin.reshape(2, 2, BATCH, S5_G, S5_N).transpose(1, 2, 0, 3, 4)
        new_re.append(hfin[0])
        new_im.append(hfin[1])

        of, ob, sfin, wup_bf, wdn_bf = _hgrn(hg_in, lbps, hg_consts, hg_s0, ffn_w_up, ffn_w_down, l)
        new_hg.append(sfin)

        yc = (_fnet(xc, cs, dft_ctx, fn_w, l, n_seq=BATCH, seq_len=SEQ, tok0=0, spg=FN_CTX_SPG, tr=SEQ),
              _fnet(xc, cs, dft_lat, fn_w, l, n_seq=DEC_BATCH, seq_len=DEC_SEQ, tok0=N_CTX, spg=DEC_BATCH,
                    tr=FN_LAT_TR))

        x1, h2 = _outproj(xs, op_in, ylo, yhi, of, ob, yc, mods, s5_d_r, s5_w_glu, gn_r, bdm, gmn_r, gm_w, gm_b,
                          gm_hm, w_out, norm2_r, l)
        res = _ffn(x1, h2, mods, ff_perm, wup_bf, ffn_conv_w, cb_r, wdn_bf, final_norm_g, l,
                   final=(l == DEPTH - 1))
        xs = (res,)

    y_prompt = res[0].reshape(BATCH, SEQ, D_MODEL)
    y_sample = res[1].reshape(DEC_BATCH, DEC_SEQ, D_MODEL)
    return (y_prompt, y_sample, jnp.stack(new_re, axis=1), jnp.stack(new_im, axis=1),
            jnp.stack(new_hg, axis=1))
```

```python
import functools
import math

import numpy as np
import jax
import jax.numpy as jnp
from jax import lax
from jax.experimental import pallas as pl
from jax.experimental.pallas import tpu as pltpu

D_MODEL = 1024
BATCH = 16
SEQ = 256
DEPTH = 2
DEC_BATCH = 2
DEC_SEQ = 2048
GRID_W = 64
GROUP_W = 256
S5_P = 16
S5_G = 16
S5_N = 64
HG_HEADS = 4
HG_DK = 64
GM_HEADS = 4
GM_CHUNK = 128
D_FF = 2816
N_MOD = 6
D_IN = 9 * GROUP_W
EPS = 1e-6
LAM_RE_MAX = -1e-4

N_CTX = BATCH * SEQ
N_LAT = DEC_BATCH * DEC_SEQ
N_TOK = N_CTX + N_LAT
TB = 512
TBP = 512
NB_CTX = N_CTX // TB
S5_L = 16
HG_L = 64
FF_TILE = 256
LANES = 128
MOD_ROWS = 8
V7X_VMEM_BYTES = 64 * 1024 * 1024
VMEM_LIMIT = V7X_VMEM_BYTES - 8 * 1024 * 1024

F32 = jnp.float32
BF16 = jnp.bfloat16

COL_XA, COL_HQ, COL_HF_FWD, COL_HF_BWD, COL_HI, COL_HGATE, COL_XC, COL_GU, COL_GV = range(9)
HG_IN_W = 4 * GROUP_W
OP_IN_W = 4 * GROUP_W


def _mod_row(i, tb):
    return jnp.where(i < N_CTX // tb, 0, 1 + (i - N_CTX // tb) // (DEC_SEQ // tb))


GELU_C0 = 0.7978845608028654
GELU_C1 = GELU_C0 * 0.044715


def _gelu(x):
    return x * (0.5 + 0.5 * jnp.tanh(x * (GELU_C0 + GELU_C1 * (x * x))))


def _rms(x, g):
    return x * lax.rsqrt(jnp.mean(x * x, axis=-1, keepdims=True) + EPS) * g


def _dot(a, b):
    return jnp.dot(a, b, preferred_element_type=F32)


def _dot_nt(a, b, precision=None):
    return lax.dot_general(a, b, (((1,), (1,)), ((), ())), precision=precision, preferred_element_type=F32)


def _dot_tn(a, b):
    return lax.dot_general(a, b, (((0,), (0,)), ((), ())), preferred_element_type=F32)


def _const_spec(shape, single=False):
    kw = {"pipeline_mode": pl.Buffered(1)} if single else {}
    return pl.BlockSpec(shape, lambda *_: (0,) * len(shape), **kw)


def _layer_spec(shape, l, single=False):
    kw = {"pipeline_mode": pl.Buffered(1)} if single else {}
    return pl.BlockSpec((None,) + tuple(shape), lambda *_: (l,) + (0,) * len(shape), **kw)


def _mod_spec(l, tb=TB):
    return pl.BlockSpec((None, None, 1, N_MOD * D_MODEL), lambda i: (l, _mod_row(i, tb), 0, 0))


def _x_specs(split, width=D_MODEL, tb=TB):
    nb_ctx = N_CTX // tb
    if split:
        return [pl.BlockSpec((tb, width), lambda i: (jnp.minimum(i, nb_ctx - 1), 0)),
                pl.BlockSpec((tb, width), lambda i: (jnp.maximum(i - nb_ctx, 0), 0))]
    return [pl.BlockSpec((tb, width), lambda i: (i, 0))]


def _x_value(x_refs, tb=TB):
    if len(x_refs) == 1:
        return x_refs[0][...]
    return jnp.where(pl.program_id(0) < N_CTX // tb, x_refs[0][...], x_refs[1][...])


def _inproj_kernel(*refs):
    *x_refs, mod_ref, g_ref, w_ref, hg_ref, op_ref, xc_ref, xalo_ref, xahi_ref, wb_ref = refs

    @pl.when(pl.program_id(0) == 0)
    def _():
        wb_ref[...] = w_ref[...].astype(BF16)

    mod = mod_ref[...]
    sh = mod[:, 0:D_MODEL]
    sc = mod[:, D_MODEL:2 * D_MODEL]
    h = _rms(_x_value(x_refs, TBP), g_ref[...]) * (1.0 + sc) + sh
    o = _dot(h.astype(BF16), wb_ref[...])
    col = lambda c, n=1: o[:, GROUP_W * c:GROUP_W * (c + n)]
    hg_ref[...] = col(COL_HQ, 4)
    op_ref[...] = jnp.concatenate([col(COL_XA), col(COL_HGATE), col(COL_GU, 2)], axis=1)
    xc_ref[...] = col(COL_XC)
    xalo_ref[...] = o[:, 0:LANES]
    xahi_ref[...] = o[:, LANES:2 * LANES]


def _inproj(xs, mods, norm_g, w_in, l):
    return pl.pallas_call(
        _inproj_kernel,
        grid=(N_TOK // TBP,),
        in_specs=_x_specs(len(xs) == 2, tb=TBP) + [
            _mod_spec(l, TBP),
            _layer_spec((1, D_MODEL), l),
            _layer_spec((D_MODEL, D_IN), l, single=True),
        ],
        out_specs=tuple(pl.BlockSpec((TBP, w), lambda i: (i, 0)) for w in (HG_IN_W, OP_IN_W, GROUP_W, LANES, LANES)),
        out_shape=tuple(jax.ShapeDtypeStruct((N_TOK, w), F32) for w in (HG_IN_W, OP_IN_W, GROUP_W, LANES, LANES)),
        scratch_shapes=[pltpu.VMEM((D_MODEL, D_IN), BF16)],
        compiler_params=pltpu.CompilerParams(vmem_limit_bytes=VMEM_LIMIT),
        name="in_proj",
    )(*xs, mods, norm_g, w_in)


S5_NPAIR = S5_G // 2
S5_CW = S5_L * S5_P
S5_SW = S5_G * S5_N
S5_STEP_ROWS = N_CTX // S5_L
S5_CTX_NC = SEQ // S5_L
S5_LAT_NC = DEC_SEQ // S5_L


def _s5_slot(g, t):
    return (t + g) % S5_L


def _s5_tables(lam_re, lam_im, log_dt, b_re, b_im, c_re, c_im):
    lr = jnp.minimum(lam_re.astype(F32), LAM_RE_MAX)
    li = lam_im.astype(F32)
    dt = jnp.exp(log_dt.astype(F32))[..., None]
    mag = jnp.exp(lr * dt)
    ang = li * dt
    ab_re = mag * jnp.cos(ang)
    ab_im = mag * jnp.sin(ang)
    den = lr * lr + li * li
    xr = ab_re - 1.0
    z_re = (xr * lr + ab_im * li) / den
    z_im = (ab_im * lr - xr * li) / den
    bb_re = z_re[..., None] * b_re - z_im[..., None] * b_im
    bb_im = z_re[..., None] * b_im + z_im[..., None] * b_re
    tau = jnp.arange(S5_L + 1, dtype=F32)[:, None, None, None, None]
    pm = jnp.exp(lr * dt * tau)
    pa = li * dt * tau
    pw_re = pm * jnp.cos(pa)
    pw_im = pm * jnp.sin(pa)
    eye2 = jnp.eye(2, dtype=F32)

    def pw_pairs(a):
        a = a.transpose(1, 2, 3, 0, 4).reshape(DEPTH, 2, S5_NPAIR, 2, S5_L + 1, S5_N)
        return a.transpose(0, 1, 2, 4, 3, 5).reshape(DEPTH, 2, S5_NPAIR, S5_L + 1, 2 * S5_N)

    def mat_pairs(a):
        a = a.reshape(DEPTH, 2, S5_NPAIR, 2, S5_P, S5_N)
        return jnp.einsum('ldjaqn,ab->ldjaqbn', a, eye2).reshape(DEPTH, 2, S5_NPAIR, 2, S5_P, 2 * S5_N)

    al = jnp.stack([pw_re[S5_L], pw_im[S5_L]], axis=2).reshape(DEPTH, 2, 2, 1, S5_SW)
    return (pw_pairs(pw_re), pw_pairs(pw_im),
            mat_pairs(bb_re.transpose(0, 1, 2, 4, 3)), mat_pairs(bb_im.transpose(0, 1, 2, 4, 3)),
            mat_pairs(c_re.astype(F32)), mat_pairs(c_im.astype(F32)), al)


def _s5_shift_consts():
    n = S5_CW
    r = np.arange(n)[:, None]
    c = np.arange(n)[None, :]
    fwd = [(c == r + S5_P * s) for s in range(S5_L)]
    bwd = [(c == r - S5_P * (S5_L - 1 - s)) for s in range(S5_L)]
    return np.stack([np.concatenate(fwd, axis=1), np.concatenate(bwd, axis=1)]).astype(np.float32)


PREP_STEPS = 4
ADA_TN = N_MOD * D_MODEL // PREP_STEPS


def _prep_kernel(c_ref, wada_ref, bada_ref, pwr_ref, pwi_ref, br_ref, bi_ref, cr_ref, ci_ref, scat_ref,
                 mod_ref, m_ref, f_ref, e_ref, k_ref):
    cv = c_ref[...]
    mod_ref[0] = _dot((cv * jax.nn.sigmoid(cv)).astype(BF16), wada_ref[0].astype(BF16)) + bada_ref[0]
    step = pl.program_id(1)
    for qq in range(PREP_STEPS):
        pl.when(step == qq)(functools.partial(_s5_pair_ops, qq, pwr_ref, pwi_ref, br_ref, bi_ref, cr_ref,
                                              ci_ref, f_ref, e_ref, k_ref))
    pl.when(step == PREP_STEPS - 1)(functools.partial(_s5_toeplitz, scat_ref, m_ref, k_ref))


def _s5_pair_ops(qq, pwr_ref, pwi_ref, br_ref, bi_ref, cr_ref, ci_ref, f_ref, e_ref, k_ref):
    L = S5_L
    hp = lax.Precision.HIGHEST
    sw = 2 * S5_N
    per_step = S5_NPAIR // PREP_STEPS
    for j in range(per_step * qq, per_step * (qq + 1)):
        for d in range(2):
            pwr = pwr_ref[0, d, j]
            pwi = pwi_ref[0, d, j]
            for gl in range(2):
                g = 2 * j + gl
                br, bi = br_ref[0, d, j, gl], bi_ref[0, d, j, gl]
                cr, ci = cr_ref[0, d, j, gl], ci_ref[0, d, j, gl]
                ca_r, ca_i, f_r, f_i, e_r, e_i = [], [], [], [], [], []
                for t in range(L):
                    kk = t if d == 0 else L - 1 - t
                    pr, pi = pwr[kk:kk + 1], pwi[kk:kk + 1]
                    ca_r.append(cr * pr - ci * pi)
                    ca_i.append(cr * pi + ci * pr)
                    kf = L - 1 - t if d == 0 else t
                    pr, pi = pwr[kf:kf + 1], pwi[kf:kf + 1]
                    f_r.append(br * pr - bi * pi)
                    f_i.append(br * pi + bi * pr)
                    ke = t + 1 if d == 0 else L - t
                    pr, pi = pwr[ke:ke + 1], pwi[ke:ke + 1]
                    e_r.append(cr * pr - ci * pi)
                    e_i.append(-(cr * pi + ci * pr))
                cat = lambda xs: jnp.concatenate(xs, axis=0)
                k = _dot_nt(br, cat(ca_r), hp) - _dot_nt(bi, cat(ca_i), hp)
                k_ref[d, S5_P * g:S5_P * (g + 1), :] = k
                slots = lambda xs: cat([xs[(p - g) % L] for p in range(L)])
                rows = slice(S5_CW * gl, S5_CW * (gl + 1))
                f_ref[0, d, j, rows, 0:sw] = slots(f_r).astype(BF16)
                f_ref[0, d, j, rows, sw:2 * sw] = slots(f_i).astype(BF16)
                e_ref[0, d, 0, j, rows, :] = slots(e_r).astype(BF16)
                e_ref[0, d, 1, j, rows, :] = slots(e_i).astype(BF16)


def _s5_toeplitz(scat_ref, m_ref, k_ref):
    L = S5_L
    kf = k_ref[0].astype(BF16)
    kb = k_ref[1].astype(BF16)
    for s in range(L):
        cols = slice(S5_CW * s, S5_CW * (s + 1))
        res = _dot(kf, scat_ref[0, :, cols]) + _dot(kb, scat_ref[1, :, cols])
        for g in range(S5_G):
            blk = res[S5_P * g:S5_P * (g + 1), :]
            if g:
                blk = pltpu.roll(blk, S5_P * g, 1)
            r0 = S5_P * _s5_slot(g, s)
            m_ref[0, g, r0:r0 + S5_P, :] = blk.astype(BF16)


def _prep(cvecs, w_ada, b_ada, tables, scat):
    pwr, pwi, br, bi, cr, ci, _ = tables
    n = N_MOD * D_MODEL

    def lspec(shape):
        return pl.BlockSpec((1,) + shape, lambda l, q: (l,) + (0,) * len(shape))

    pw_shape = (2, S5_NPAIR, S5_L + 1, 2 * S5_N)
    mat_shape = (2, S5_NPAIR, 2, S5_P, 2 * S5_N)
    return pl.pallas_call(
        _prep_kernel,
        grid=(DEPTH, PREP_STEPS),
        in_specs=[pl.BlockSpec((MOD_ROWS, D_MODEL), lambda l, q: (0, 0)),
                  pl.BlockSpec((1, D_MODEL, ADA_TN), lambda l, q: (l, 0, q)),
                  pl.BlockSpec((1, 1, ADA_TN), lambda l, q: (l, 0, q)),
                  lspec(pw_shape), lspec(pw_shape), lspec(mat_shape), lspec(mat_shape), lspec(mat_shape),
                  lspec(mat_shape), _const_spec(scat.shape)],
        out_specs=(pl.BlockSpec((1, MOD_ROWS, ADA_TN), lambda l, q: (l, 0, q)),
                   lspec((S5_G, S5_CW, S5_CW)), lspec((2, S5_NPAIR, 2 * S5_CW, 4 * S5_N)),
                   lspec((2, 2, S5_NPAIR, 2 * S5_CW, 2 * S5_N))),
        out_shape=(jax.ShapeDtypeStruct((DEPTH, MOD_ROWS, n), F32),
                   jax.ShapeDtypeStruct((DEPTH, S5_G, S5_CW, S5_CW), BF16),
                   jax.ShapeDtypeStruct((DEPTH, 2, S5_NPAIR, 2 * S5_CW, 4 * S5_N), BF16),
                   jax.ShapeDtypeStruct((DEPTH, 2, 2, S5_NPAIR, 2 * S5_CW, 2 * S5_N), BF16)),
        scratch_shapes=[pltpu.VMEM((2, S5_G * S5_P, S5_CW), F32)],
        compiler_params=pltpu.CompilerParams(vmem_limit_bytes=VMEM_LIMIT),
        name="param_prep",
    )(cvecs, w_ada, b_ada.reshape(DEPTH, 1, n), pwr, pwi, br, bi, cr, ci, scat)


def _s5_kernel(xlo_ref, xhi_ref, m_ref, f_ref, e_ref, al_ref, h0_ref, ylo_ref, yhi_ref, hfin_ref,
               uy_ref, zh_ref):
    R = S5_STEP_ROWS
    pw = 2 * S5_CW
    sw = 2 * S5_N
    gph = S5_G // 2
    step = pl.program_id(0)

    for half, x_ref in enumerate((xlo_ref, xhi_ref)):
        for t in range(S5_L):
            xt = x_ref[pl.ds(t, R, stride=S5_L), :]
            shift = (S5_P * t) % LANES
            xr = pltpu.roll(xt, shift, 1) if shift else xt
            for gl in range(gph):
                g = gph * half + gl
                src = (S5_P * gl + shift) % LANES
                lo = S5_CW * g + S5_P * _s5_slot(g, t)
                assert lo % LANES == src
                uy_ref[:, lo:lo + S5_P] = xr[:, src:src + S5_P]

    for j in range(S5_NPAIR):
        ub = uy_ref[:, pw * j:pw * (j + 1)].astype(BF16)
        for d in range(2):
            o = _dot(ub, f_ref[d, j])
            zh_ref[d, 0, :, sw * j:sw * (j + 1)] = o[:, :sw]
            zh_ref[d, 1, :, sw * j:sw * (j + 1)] = o[:, sw:]

    def scan_group(seq_rows, nc, init, fin_rows):
        chains = [(d, r0) for d in range(2) for r0 in seq_rows]

        def body(c, carry):
            out = []
            for k, (d, r0) in enumerate(chains):
                hr, hi = carry[2 * k], carry[2 * k + 1]
                r = r0 + (c if d == 0 else nc - 1 - c)
                zr = zh_ref[d, 0, pl.ds(r, 1), :]
                zi = zh_ref[d, 1, pl.ds(r, 1), :]
                zh_ref[d, 0, pl.ds(r, 1), :] = hr
                zh_ref[d, 1, pl.ds(r, 1), :] = hi
                ar = al_ref[d, 0]
                ai = al_ref[d, 1]
                out.append(ar * hr - ai * hi + zr)
                out.append(ar * hi + ai * hr + zi)
            return tuple(out)

        fin = lax.fori_loop(0, nc, body, tuple(init))
        if fin_rows is not None:
            for k, (d, _) in enumerate(chains):
                s = fin_rows[k % len(seq_rows)]
                hfin_ref[d, 0, s:s + 1, :] = fin[2 * k]
                hfin_ref[d, 1, s:s + 1, :] = fin[2 * k + 1]

    @pl.when(step == 0)
    def _():
        zero = jnp.zeros((1, S5_SW), F32)
        for s0 in range(0, BATCH, 4):
            seqs = list(range(s0, s0 + 4))
            scan_group([s * S5_CTX_NC for s in seqs], S5_CTX_NC, [zero] * 16, seqs)

    @pl.when(step == 1)
    def _():
        init = []
        for d in range(2):
            for s in range(DEC_BATCH):
                init.append(h0_ref[d, 0, s:s + 1, :])
                init.append(h0_ref[d, 1, s:s + 1, :])
        scan_group([s * S5_LAT_NC for s in range(DEC_BATCH)], S5_LAT_NC, init, None)

    for j in range(S5_NPAIR):
        acc = None
        for d in range(2):
            for ri in range(2):
                hb = zh_ref[d, ri, :, sw * j:sw * (j + 1)].astype(BF16)
                t = _dot_nt(hb, e_ref[d, ri, j])
                acc = t if acc is None else acc + t
        for gl in range(2):
            g = 2 * j + gl
            ug = uy_ref[:, S5_CW * g:S5_CW * (g + 1)].astype(BF16)
            uy_ref[:, S5_CW * g:S5_CW * (g + 1)] = acc[:, S5_CW * gl:S5_CW * (gl + 1)] + _dot(ug, m_ref[g])

    lane = lax.broadcasted_iota(jnp.int32, (R, LANES), 1)
    in_piece = [jnp.logical_and(lane >= S5_P * k, lane < S5_P * (k + 1)) for k in range(LANES // S5_P)]
    for half, y_ref in enumerate((ylo_ref, yhi_ref)):
        for t in range(S5_L):
            merged = None
            for gl in range(gph):
                g = gph * half + gl
                slot = _s5_slot(g, t)
                col = S5_CW * g + LANES * (S5_P * slot // LANES)
                v = uy_ref[:, col:col + LANES]
                merged = v if merged is None else jnp.where(in_piece[slot % (LANES // S5_P)], v, merged)
            shift = (LANES - S5_P * t % LANES) % LANES
            y_ref[pl.ds(t, R, stride=S5_L), :] = pltpu.roll(merged, shift, 1) if shift else merged


def _s5(xa_lo, xa_hi, m, f2, e2, al, h0, l):
    half = pl.BlockSpec((N_CTX, LANES), lambda s: (s, 0))
    return pl.pallas_call(
        _s5_kernel,
        grid=(2,),
        in_specs=[half, half,
                  _layer_spec(m.shape[1:], l, single=True), _layer_spec(f2.shape[1:], l, single=True),
                  _layer_spec(e2.shape[1:], l, single=True), _layer_spec(al.shape[1:], l),
                  _layer_spec(h0.shape[1:], l)],
        out_specs=(half, half, _const_spec((2, 2, BATCH, S5_SW))),
        out_shape=(jax.ShapeDtypeStruct((N_TOK, LANES), F32),
                   jax.ShapeDtypeStruct((N_TOK, LANES), F32),
                   jax.ShapeDtypeStruct((2, 2, BATCH, S5_SW), F32)),
        scratch_shapes=[pltpu.VMEM((S5_STEP_ROWS, S5_G * S5_CW), F32),
                        pltpu.VMEM((2, 2, S5_STEP_ROWS, S5_SW), F32)],
        compiler_params=pltpu.CompilerParams(vmem_limit_bytes=VMEM_LIMIT),
        name="s5_mixer",
    )(xa_lo, xa_hi, m, f2, e2, al, h0)


HG_NLEV = int(math.log2(HG_L))
HG_W = HG_HEADS * HG_DK
HG_CPS = 8
HG_BLK = HG_CPS * HG_L
HG_NB_CTX = N_CTX // HG_BLK
HG_SPB = HG_BLK // SEQ
HG_CPQ = SEQ // HG_L
HG_NC_LAT = DEC_SEQ // HG_BLK
assert HG_SPB >= 1 and HG_SPB * SEQ == HG_BLK
assert HG_L == HG_DK


def _hgrn_consts():
    L = HG_L
    w = np.zeros((HG_NLEV + 2, L, L), np.float32)
    mask = np.zeros((HG_NLEV + 1, L, L), np.float32)
    for lev in range(HG_NLEV):
        blk = L >> lev
        half = blk // 2
        for t in range(L):
            p, o = divmod(t, blk)
            bd = p * blk + half - 1
            if o >= half:
                w[lev, t, bd + 1:t + 1] = 1.0
            else:
                w[lev, t, t + 1:bd + 1] = 1.0
        jj, ii = np.meshgrid(np.arange(L), np.arange(L), indexing='ij')
        mask[lev] = ((jj // blk == ii // blk) & (jj % blk >= half) & (ii % blk < half)).astype(np.float32)
    mask[HG_NLEV] = np.eye(L, dtype=np.float32)
    for t in range(L):
        w[HG_NLEV, t, :t + 1] = 1.0
        w[HG_NLEV + 1, t, t + 1:] = 1.0
    out = []
    for wd, md in ((w, mask), (w[:, ::-1, ::-1], mask[:, ::-1, ::-1])):
        wflat = wd.reshape((HG_NLEV + 2) * L, L)
        out.append((np.concatenate([wflat] * 3, axis=1), np.tile(md, (1, 1, HG_HEADS))))
    wcat = np.stack([out[0][0], out[1][0]])
    mask4 = np.stack([out[0][1], out[1][1]])
    hm = np.kron(np.eye(HG_HEADS, dtype=np.float32), np.ones((HG_DK, HG_DK), np.float32))
    return wcat, mask4, hm


def _hg_pos(i):
    is_ctx = i < HG_NB_CTX
    c = jnp.where(is_ctx, 0, (i - HG_NB_CTX) % HG_NC_LAT)
    nc = jnp.where(is_ctx, 1, HG_NC_LAT)
    return is_ctx, c, nc


def _hg_bwd_blk(i):
    _, c, nc = _hg_pos(i)
    return i + nc - 1 - 2 * c


def _hg_local(items, lb_ref, wcat_ref, mask_ref, hm):
    L = HG_L
    hmb = hm.astype(BF16)

    def bd4(x):
        return jnp.concatenate([x] * HG_HEADS, axis=0) * hmb

    kks, exs = [], []
    for d, q, z, v in items:
        lbp = lb_ref[d]
        sp = jnp.log1p(jnp.exp(-jnp.abs(z)))
        ls = jnp.minimum(z, 0.0) - sp
        kks.append(lbp[2:3] * jnp.exp(jnp.minimum(-z, 0.0) - sp))
        a = lbp[0:1]
        b = lbp[1:2] + ls
        logf = jnp.maximum(a, b) + jnp.log1p(jnp.exp(-jnp.abs(a - b)))
        p0 = logf.astype(BF16)
        r0 = logf - p0.astype(F32)
        p1 = r0.astype(BF16)
        p2 = (r0 - p1.astype(F32)).astype(BF16)
        exs.append(jnp.exp(_dot(wcat_ref[d], jnp.concatenate([p0, p1, p2], axis=0))))
    scs = [None] * len(items)
    for lev in range(HG_NLEV + 1):
        for n, (d, q, z, v) in enumerate(items):
            if lev < HG_NLEV:
                al = exs[n][L * lev:L * (lev + 1)]
                lhs = (q * al).astype(BF16)
                rhs = (kks[n] * al).astype(BF16)
            else:
                lhs = q.astype(BF16)
                rhs = kks[n].astype(BF16)
            t = _dot_nt(lhs, bd4(rhs)) * mask_ref[d, lev]
            scs[n] = t if scs[n] is None else scs[n] + t
    out = []
    for n, (d, q, z, v) in enumerate(items):
        vb = v.astype(BF16)
        o = _dot(scs[n].astype(BF16), bd4(vb))
        eq = exs[n][L * HG_NLEV:L * (HG_NLEV + 1)]
        qe = (q * eq).astype(BF16)
        ke = (kks[n] * exs[n][L * (HG_NLEV + 1):L * (HG_NLEV + 2)]).astype(BF16)
        g = eq[L - 1:L] if d == 0 else eq[0:1]
        out.append((o, qe, g, _dot_tn(vb, ke) * hm))
    return out


def _hgrn_kernel(inf_ref, inb_ref, lb_ref, w_ref, mask_ref, hm_ref, s0_ref, wup_ref, wdn_ref,
                 of_ref, ob_ref, sfin_ref, wupb_ref, wdnb_ref, st_ref):
    is_ctx, c, nc = _hg_pos(pl.program_id(0))
    wupb_ref[...] = wup_ref[...].astype(BF16)
    wdnb_ref[...] = wdn_ref[...].astype(BF16)

    hm = hm_ref[...]
    rows = [slice(HG_L * k, HG_L * (k + 1)) for k in range(HG_CPS)]
    lanes = lambda c: slice(GROUP_W * (c - COL_HQ), GROUP_W * (c - COL_HQ + 1))
    items = [(d, in_ref[r, lanes(COL_HQ)], in_ref[r, lanes(zc)], in_ref[r, lanes(COL_HI)])
             for d, (in_ref, zc) in enumerate(((inf_ref, COL_HF_FWD), (inb_ref, COL_HF_BWD))) for r in rows]
    loc = _hg_local(items, lb_ref, w_ref, mask_ref, hm)

    def chain(d, o_ref, st, chunks):
        for k in (chunks if d == 0 else reversed(chunks)):
            o, qe, g, upd = loc[HG_CPS * d + k]
            o_ref[rows[k], :] = o + _dot_nt(qe, st.astype(BF16))
            st = st * g + upd
        return st

    @pl.when(is_ctx)
    def _():
        for sq in range(HG_SPB):
            for d, o_ref in enumerate((of_ref, ob_ref)):
                st = chain(d, o_ref, jnp.zeros((HG_W, HG_W), F32), range(HG_CPQ * sq, HG_CPQ * (sq + 1)))
                s_kv = st.T
                for h in range(HG_HEADS):
                    sfin_ref[sq, d, h] = s_kv[HG_DK * h:HG_DK * (h + 1), HG_DK * h:HG_DK * (h + 1)]

    @pl.when(jnp.logical_not(is_ctx))
    def _():
        @pl.when(c == 0)
        def _():
            st_ref[...] = s0_ref[0]

        for d, o_ref in enumerate((of_ref, ob_ref)):
            st_ref[d] = chain(d, o_ref, st_ref[d], range(HG_CPS))


def _hgrn(hg_in, lbp, consts, s0, w_up, w_down, l):
    wcat, mask4, hm = consts
    steps = N_TOK // HG_BLK
    up_rows = D_MODEL // steps
    dn_rows = 2 * D_FF // steps

    return pl.pallas_call(
        _hgrn_kernel,
        grid=(N_TOK // HG_BLK,),
        in_specs=[pl.BlockSpec((HG_BLK, HG_IN_W), lambda i: (i, 0)),
                  pl.BlockSpec((HG_BLK, HG_IN_W), lambda i: (_hg_bwd_blk(i), 0)),
                  _layer_spec(lbp.shape[1:], l), _const_spec(wcat.shape), _const_spec(mask4.shape),
                  _const_spec(hm.shape),
                  pl.BlockSpec((None, 1, 2, HG_W, HG_W),
                               lambda i: (l, jnp.maximum(i - HG_NB_CTX, 0) // HG_NC_LAT, 0, 0, 0)),
                  pl.BlockSpec((None, up_rows, 2 * D_FF), lambda i: (l, i, 0)),
                  pl.BlockSpec((None, dn_rows, D_MODEL), lambda i: (l, i // 2, 0))],
        out_specs=(pl.BlockSpec((HG_BLK, HG_W), lambda i: (i, 0)),
                   pl.BlockSpec((HG_BLK, HG_W), lambda i: (_hg_bwd_blk(i), 0)),
                   pl.BlockSpec((HG_SPB, 2, HG_HEADS, HG_DK, HG_DK),
                                lambda i: (jnp.minimum(i, HG_NB_CTX - 1), 0, 0, 0, 0)),
                   pl.BlockSpec((up_rows, 2 * D_FF), lambda i: (i, 0)),
                   pl.BlockSpec((dn_rows, D_MODEL), lambda i: (i // 2, 0))),
        out_shape=(jax.ShapeDtypeStruct((N_TOK, HG_W), F32),
                   jax.ShapeDtypeStruct((N_TOK, HG_W), F32),
                   jax.ShapeDtypeStruct((BATCH, 2, HG_HEADS, HG_DK, HG_DK), F32),
                   jax.ShapeDtypeStruct((D_MODEL, 2 * D_FF), BF16),
                   jax.ShapeDtypeStruct((D_FF, D_MODEL), BF16)),
        scratch_shapes=[pltpu.VMEM((2, HG_W, HG_W), F32)],
        compiler_params=pltpu.CompilerParams(vmem_limit_bytes=VMEM_LIMIT),
        name="hgrn_mixer",
    )(hg_in, hg_in, lbp, wcat, mask4, hm, s0, w_up, w_down)


FN_CTX_SPG = 4
FN_LAT_TR = 512


def _dft_consts(t_len):
    n = GROUP_W // 4
    k = np.arange(n)
    ang = 2.0 * np.pi * ((k[:, None] * k[None, :]) % n) / n
    eye = np.eye(4)
    cs = np.concatenate([np.kron(eye, np.cos(ang)), np.kron(eye, np.sin(ang))], axis=1) / math.sqrt(n)
    t = np.arange(t_len)
    angt = 2.0 * np.pi * ((t[:, None] * t[None, :]) % t_len) / t_len
    dft = np.concatenate([np.cos(angt), -np.sin(angt)], axis=1) / math.sqrt(t_len)
    return cs.astype(np.float32), dft.astype(np.float32)


def _fnet_kernel(x_ref, cs_ref, dft_ref, w_ref, o_ref, r_ref, *, t_len, spg, tr):
    @pl.when(pl.program_id(1) == 0)
    def _():
        t = _dot(x_ref[...].astype(BF16), cs_ref[...])
        for s in range(spg):
            rows = slice(t_len * s, t_len * (s + 1))
            r_ref[s, 0:t_len, :] = t[rows, :GROUP_W].astype(BF16)
            r_ref[s, t_len:2 * t_len, :] = t[rows, GROUP_W:].astype(BF16)

    w = w_ref[...].astype(BF16)
    for s in range(spg):
        y = _dot(dft_ref[...], r_ref[s])
        o_ref[tr * s:tr * (s + 1), :] = _dot(y.astype(BF16), w)


def _fnet(xc, cs, dft, fn_w, l, *, n_seq, seq_len, tok0, spg, tr):
    nj = seq_len // tr
    assert spg == 1 or nj == 1
    gb0 = tok0 // (spg * seq_len)
    return pl.pallas_call(
        functools.partial(_fnet_kernel, t_len=seq_len, spg=spg, tr=tr),
        grid=(n_seq // spg, nj),
        in_specs=[
            pl.BlockSpec((spg * seq_len, GROUP_W), lambda g, j: (gb0 + g, 0)),
            _const_spec((GROUP_W, 2 * GROUP_W)),
            pl.BlockSpec((tr, 2 * seq_len), lambda g, j: (j, 0)),
            _layer_spec((GROUP_W, GROUP_W), l),
        ],
        out_specs=pl.BlockSpec((spg * tr, GROUP_W), lambda g, j: (g * nj + j, 0)),
        out_shape=jax.ShapeDtypeStruct((n_seq * seq_len, GROUP_W), F32),
        scratch_shapes=[pltpu.VMEM((spg, 2 * seq_len, GROUP_W), BF16)],
        compiler_params=pltpu.CompilerParams(vmem_limit_bytes=VMEM_LIMIT),
        name="fnet_mixer",
    )(xc, cs, dft, fn_w)


def _outproj_kernel(*refs):
    (op_ref, ylo_ref, yhi_ref, of_ref, ob_ref, ycc_ref, ycl_ref, mod_ref, d_ref,
     wglu_ref, gn_ref, bdm_ref, gmn_ref, gmw_ref, gmb_ref, gmh_ref, wout_ref, n2_ref,
     x1_ref, h2_ref, wob_ref) = refs[-21:]
    x_refs = refs[:-21]
    xa_ref, hgate_ref, gu_ref, gv_ref = (op_ref.at[:, GROUP_W * k:GROUP_W * (k + 1)] for k in range(4))

    @pl.when(pl.program_id(0) == 0)
    def _():
        wob_ref[...] = wout_ref[...].astype(BF16)

    gn = gn_ref[...]
    ys = jnp.concatenate([ylo_ref[...], yhi_ref[...]], axis=1)
    y5 = _gelu(ys + d_ref[...] * xa_ref[...])
    glu = jax.nn.sigmoid(_dot(y5.astype(BF16), wglu_ref[...].astype(BF16)))
    out_a = _rms(y5 * glu, gn[:, 0:GROUP_W])
    o = of_ref[...] + ob_ref[...]
    o2 = o * o
    o2h = o2.astype(BF16)
    o2l = (o2 - o2h.astype(F32)).astype(BF16)
    ms = _dot(o2h, bdm_ref[...]) + _dot(o2l, bdm_ref[...])
    hg = hgate_ref[...]
    out_b = o * lax.rsqrt(ms + EPS) * gn[:, GROUP_W:2 * GROUP_W] * (hg * jax.nn.sigmoid(hg))
    out_c = _rms(_x_value((ycc_ref, ycl_ref), TBP), gn[:, 2 * GROUP_W:3 * GROUP_W])
    gm_hm = gmh_ref[...]
    out_d = []
    for c in range(TBP // GM_CHUNK):
        rows = slice(GM_CHUNK * c, GM_CHUNK * (c + 1))
        gv = _rms(_gelu(gv_ref[rows, :]), gmn_ref[...]).astype(BF16)
        g4 = jnp.concatenate([gv] * GM_HEADS, axis=0) * gm_hm
        sp = _dot(gmw_ref[...], g4) + gmb_ref[...]
        out_d.append(_rms(_gelu(gu_ref[rows, :]) * sp, gn[:, 3 * GROUP_W:]))
    out_d = jnp.concatenate(out_d, axis=0)
    m = None
    for k, part in enumerate((out_a, out_b, out_c, out_d)):
        t = _dot(part.astype(BF16), wob_ref[GROUP_W * k:GROUP_W * (k + 1), :])
        m = t if m is None else m + t
    mod = mod_ref[...]
    g1 = mod[:, 2 * D_MODEL:3 * D_MODEL]
    sh2 = mod[:, 3 * D_MODEL:4 * D_MODEL]
    sc2 = mod[:, 4 * D_MODEL:5 * D_MODEL]
    x1 = _x_value(x_refs, TBP) + g1 * m
    x1_ref[...] = x1
    h2_ref[...] = (_rms(x1, n2_ref[...]) * (1.0 + sc2) + sh2).astype(BF16)


def _outproj(xs, op_in, ylo, yhi, of, ob, yc, mods, s5_d, wglu, gn, bdm, gm_norm_g, gm_w, gm_b, gm_hm, wout,
             norm2_g, l):
    def tok(width, col=0):
        return pl.BlockSpec((TBP, width), lambda i: (i, col))

    return pl.pallas_call(
        _outproj_kernel,
        grid=(N_TOK // TBP,),
        in_specs=_x_specs(len(xs) == 2, tb=TBP) + [
            tok(OP_IN_W), tok(LANES), tok(LANES), tok(GROUP_W), tok(GROUP_W), *_x_specs(True, GROUP_W, TBP),
            _mod_spec(l, TBP),
            _layer_spec((1, GROUP_W), l), _layer_spec((GROUP_W, GROUP_W), l), _layer_spec((1, D_MODEL), l),
            _const_spec((GROUP_W, GROUP_W)),
            _layer_spec((1, GROUP_W), l), _layer_spec((GM_CHUNK, GM_HEADS * GM_CHUNK), l),
            _layer_spec((GM_CHUNK, GROUP_W), l), _const_spec((GM_HEADS * GM_CHUNK, GROUP_W)),
            _layer_spec((D_MODEL, D_MODEL), l, single=True), _layer_spec((1, D_MODEL), l),
        ],
        out_specs=(tok(D_MODEL), tok(D_MODEL)),
        out_shape=(jax.ShapeDtypeStruct((N_TOK, D_MODEL), F32),
                   jax.ShapeDtypeStruct((N_TOK, D_MODEL), BF16)),
        scratch_shapes=[pltpu.VMEM((D_MODEL, D_MODEL), BF16)],
        compiler_params=pltpu.CompilerParams(vmem_limit_bytes=VMEM_LIMIT),
        name="out_proj",
    )(*xs, op_in, ylo, yhi, of, ob, *yc, mods, s5_d, wglu, gn, bdm, gm_norm_g, gm_w, gm_b,
      gm_hm, wout, norm2_g)


FF_SEG = GRID_W
FF_NSEG = TB // FF_SEG
FF_SEQ_STRIPS = SEQ // FF_SEG
FF_PERM_ROWS = FF_SEQ_STRIPS * FF_SEG
SUBLANES = 8
FF_NGRP = FF_SEG // SUBLANES


def _ffn_perm_consts():
    p = np.zeros((FF_PERM_ROWS, FF_PERM_ROWS), np.float32)
    for k in range(FF_SEQ_STRIPS):
        for t in range(FF_SEG):
            p[FF_SEG * k + SUBLANES * (t % FF_NGRP) + t // FF_NGRP, FF_SEG * k + t] = 1.0
    return p


def _ffn_kernel(x1_ref, h2_ref, mod_ref, perm_ref, wup_ref, cw_ref, cb_ref, wdn_ref, fg_ref, *rest, final):
    *o_refs, h2p_ref, z_ref, hid_ref = rest
    i = pl.program_id(0)
    joined = (i < NB_CTX).astype(F32)
    n_tiles = D_FF // FF_TILE
    sub = lax.broadcasted_iota(jnp.int32, (SUBLANES, 2 * FF_TILE), 0)
    is_first = sub == 0
    is_last = sub == SUBLANES - 1
    zero_grp = jnp.zeros((SUBLANES, 2 * FF_TILE), F32)
    for r in range(0, TB, FF_PERM_ROWS):
        h2p_ref[r:r + FF_PERM_ROWS, :] = _dot(perm_ref[...], h2_ref[r:r + FF_PERM_ROWS, :]).astype(BF16)

    def up(j):
        lo = FF_TILE * j
        for c, col in enumerate((lo, D_FF + lo)):
            z_ref[j % 2, :, FF_TILE * c:FF_TILE * (c + 1)] = _dot(h2p_ref[...], wup_ref[:, col:col + FF_TILE])

    def gate(j):
        slot = j % 2
        lo = FF_TILE * j
        w = jnp.concatenate([cw_ref[:, lo:lo + FF_TILE], cw_ref[:, D_FF + lo:D_FF + lo + FF_TILE]], axis=1)
        b = jnp.concatenate([cb_ref[:, lo:lo + FF_TILE], cb_ref[:, D_FF + lo:D_FF + lo + FF_TILE]], axis=1)

        def grp(k, v):
            r = FF_SEG * k + SUBLANES * v
            return z_ref[slot, r:r + SUBLANES, :]

        down = [pltpu.roll(grp(k, FF_NGRP - 1), 1, 0) for k in range(FF_NSEG)]
        up_ = [pltpu.roll(grp(k, 0), SUBLANES - 1, 0) for k in range(FF_NSEG)]
        strips = []
        for k in range(FF_NSEG):
            g = [grp(k, v) for v in range(FF_NGRP)]
            before = down[k - 1] * joined if k % FF_SEQ_STRIPS > 0 else zero_grp
            after = up_[k + 1] * joined if k % FF_SEQ_STRIPS < FF_SEQ_STRIPS - 1 else zero_grp
            zm1 = jnp.concatenate([jnp.where(is_first, before, down[k])] + g[:-1], axis=0)
            zp1 = jnp.concatenate(g[1:] + [jnp.where(is_last, after, up_[k])], axis=0)
            zc = b + zm1 * w[0:1] + jnp.concatenate(g, axis=0) * w[1:2] + zp1 * w[2:3]
            strips.append((_gelu(zc[:, :FF_TILE]) * zc[:, FF_TILE:]).astype(BF16))
        hid_ref[:, lo:lo + FF_TILE] = jnp.concatenate(strips, axis=0)

    up(0)
    for j in range(n_tiles):
        if j + 1 < n_tiles:
            up(j + 1)
        gate(j)
    acc = _dot(hid_ref[...], wdn_ref[...])
    acc = jnp.swapaxes(acc.reshape(FF_NSEG, FF_NGRP, SUBLANES, D_MODEL), 1, 2).reshape(TB, D_MODEL)
    g2 = mod_ref[...][:, 5 * D_MODEL:6 * D_MODEL]
    x2 = x1_ref[...] + g2 * acc
    if final:
        y = _rms(x2, fg_ref[...])
        yp_ref, ys_ref = o_refs

        @pl.when(i < NB_CTX)
        def _():
            yp_ref[...] = y

        @pl.when(i >= NB_CTX)
        def _():
            ys_ref[...] = y
    else:
        o_refs[0][...] = x2


def _ffn(x1, h2, mods, perm, wup_bf, conv_w, conv_b, wdn_bf, final_g, l, *, final):
    if final:
        out_specs = tuple(_x_specs(True))
        out_shape = (jax.ShapeDtypeStruct((N_CTX, D_MODEL), F32), jax.ShapeDtypeStruct((N_LAT, D_MODEL), F32))
    else:
        out_specs = pl.BlockSpec((TB, D_MODEL), lambda i: (i, 0))
        out_shape = jax.ShapeDtypeStruct((N_TOK, D_MODEL), F32)
    return pl.pallas_call(
        functools.partial(_ffn_kernel, final=final),
        grid=(N_TOK // TB,),
        in_specs=[
            pl.BlockSpec((TB, D_MODEL), lambda i: (i, 0)),
            pl.BlockSpec((TB, D_MODEL), lambda i: (i, 0)),
            _mod_spec(l), _const_spec((FF_PERM_ROWS, FF_PERM_ROWS)),
            _const_spec((D_MODEL, 2 * D_FF), single=True), _layer_spec((3, 2 * D_FF), l),
            _layer_spec((1, 2 * D_FF), l), _const_spec((D_FF, D_MODEL), single=True),
            _const_spec((1, D_MODEL)),
        ],
        out_specs=out_specs,
        out_shape=out_shape,
        scratch_shapes=[pltpu.VMEM((TB, D_MODEL), BF16), pltpu.VMEM((2, TB, 2 * FF_TILE), F32),
                        pltpu.VMEM((TB, D_FF), BF16)],
        compiler_params=pltpu.CompilerParams(vmem_limit_bytes=VMEM_LIMIT),
        name="conv_ffn",
    )(x1, h2, mods, perm, wup_bf, conv_w, conv_b, wdn_bf, final_g.reshape(1, D_MODEL))


def kernel(x_prompt, x_sample, state_s5_re, state_s5_im, state_hgrn, c, c_ctx, w_ada, b_ada, norm1_g,
           norm2_g, w_in, s5_lam_re, s5_lam_im, s5_log_dt, s5_b_re, s5_b_im, s5_c_re, s5_c_im, s5_d,
           s5_w_glu, hg_lb_logits, fn_w, gm_norm_g, gm_ws, gm_bs, grp_norm_g, w_out, ffn_w_up,
           ffn_conv_w, ffn_conv_b, ffn_w_down, final_norm_g):
    xs = (x_prompt.reshape(N_CTX, D_MODEL), x_sample.reshape(N_LAT, D_MODEL))
    cvecs = jnp.concatenate([c_ctx[None], c, jnp.zeros((MOD_ROWS - 1 - DEC_BATCH, D_MODEL), F32)], axis=0)

    lb_p = jax.nn.softmax(hg_lb_logits.astype(F32), axis=0)
    lbs = jnp.maximum(jnp.cumsum(lb_p, axis=0) - lb_p[0], 0.0)
    lbps = jnp.stack([jnp.log(lbs), jnp.log1p(-lbs), 1.0 - lbs], axis=2)

    hg_consts_np = _hgrn_consts()
    hg_consts = (jnp.asarray(hg_consts_np[0], BF16), jnp.asarray(hg_consts_np[1], F32),
                 jnp.asarray(hg_consts_np[2], F32))
    cs_np, dft_ctx_np = _dft_consts(SEQ)
    _, dft_lat_np = _dft_consts(DEC_SEQ)
    cs = jnp.asarray(cs_np, F32).astype(BF16)
    dft_ctx = jnp.asarray(dft_ctx_np, F32).astype(BF16)
    dft_lat = jnp.asarray(dft_lat_np, F32).astype(BF16)
    gm_hm = jnp.asarray(np.kron(np.eye(GM_HEADS), np.ones((GM_CHUNK, GROUP_W // GM_HEADS))), BF16)
    bdm = jnp.asarray(np.kron(np.eye(HG_HEADS), np.ones((HG_DK, HG_DK))) / HG_DK, BF16)

    tables = _s5_tables(s5_lam_re, s5_lam_im, s5_log_dt, s5_b_re, s5_b_im, s5_c_re, s5_c_im)
    mods, s5_m, s5_f, s5_e = _prep(cvecs, w_ada, b_ada, tables, jnp.asarray(_s5_shift_consts(), BF16))
    mods = mods.reshape(DEPTH, MOD_ROWS, 1, N_MOD * D_MODEL)
    s5_al = tables[-1]
    s5_h0 = jnp.stack([state_s5_re, state_s5_im]).astype(F32).transpose(2, 3, 0, 1, 4, 5)
    s5_h0 = s5_h0.reshape(DEPTH, 2, 2, DEC_BATCH, S5_SW)
    hg_s0 = jnp.einsum('bldhkv,hg->lbdhvgk', state_hgrn.astype(F32), jnp.eye(HG_HEADS, dtype=F32))
    hg_s0 = hg_s0.reshape(DEPTH, DEC_BATCH, 2, HG_W, HG_W)

    def rows(a):
        return a.reshape(DEPTH, 1, a.shape[-1])

    norm1_r, norm2_r, gn_r, s5_d_r, gmn_r, cb_r = (rows(a) for a in (
        norm1_g, norm2_g, grp_norm_g, s5_d, gm_norm_g, ffn_conv_b))
    gm_w = gm_ws.transpose(0, 2, 1, 3).reshape(DEPTH, GM_CHUNK, GM_HEADS * GM_CHUNK).astype(BF16)
    gm_b = jnp.repeat(gm_bs.transpose(0, 2, 1), GROUP_W // GM_HEADS, axis=2)
    ff_perm = jnp.asarray(_ffn_perm_consts(), BF16)

    new_re, new_im, new_hg = [], [], []
    for l in range(DEPTH):
        hg_in, op_in, xc, xa_lo, xa_hi = _inproj(xs, mods, norm1_r, w_in, l)

        ylo, yhi, hfin = _s5(xa_lo, xa_hi, s5_m, s5_f, s5_e, s5_al, s5_h0, l)
        hfin = hfin.reshape(2, 2, BATCH, S5_G, S5_N).transpose(1, 2, 0, 3, 4)
        new_re.append(hfin[0])
        new_im.append(hfin[1])

        of, ob, sfin, wup_bf, wdn_bf = _hgrn(hg_in, lbps, hg_consts, hg_s0, ffn_w_up, ffn_w_down, l)
        new_hg.append(sfin)

        yc = (_fnet(xc, cs, dft_ctx, fn_w, l, n_seq=BATCH, seq_len=SEQ, tok0=0, spg=FN_CTX_SPG, tr=SEQ),
              _fnet(xc, cs, dft_lat, fn_w, l, n_seq=DEC_BATCH, seq_len=DEC_SEQ, tok0=N_CTX, spg=1,
                    tr=FN_LAT_TR))

        x1, h2 = _outproj(xs, op_in, ylo, yhi, of, ob, yc, mods, s5_d_r, s5_w_glu, gn_r, bdm, gmn_r, gm_w, gm_b,
                          gm_hm, w_out, norm2_r, l)
        res = _ffn(x1, h2, mods, ff_perm, wup_bf, ffn_conv_w, cb_r, wdn_bf, final_norm_g, l,
                   final=(l == DEPTH - 1))
        xs = (res,)

    y_prompt = res[0].reshape(BATCH, SEQ, D_MODEL)
    y_sample = res[1].reshape(DEC_BATCH, DEC_SEQ, D_MODEL)
    return (y_prompt, y_sample, jnp.stack(new_re, axis=1), jnp.stack(new_im, axis=1),
            jnp.stack(new_hg, axis=1))
```

```python
import functools
import math

import numpy as np
import jax
import jax.numpy as jnp
from jax import lax
from jax.experimental import pallas as pl
from jax.experimental.pallas import tpu as pltpu

D_MODEL = 1024
BATCH = 16
SEQ = 256
DEPTH = 2
DEC_BATCH = 2
DEC_SEQ = 2048
GRID_W = 64
GROUP_W = 256
S5_P = 16
S5_G = 16
S5_N = 64
HG_HEADS = 4
HG_DK = 64
GM_HEADS = 4
GM_CHUNK = 128
D_FF = 2816
N_MOD = 6
D_IN = 9 * GROUP_W
EPS = 1e-6
LAM_RE_MAX = -1e-4

N_CTX = BATCH * SEQ
N_LAT = DEC_BATCH * DEC_SEQ
N_TOK = N_CTX + N_LAT
TB = 512
TBP = 512
NB_CTX = N_CTX // TB
S5_L = 16
HG_L = 64
FF_TILE = 256
LANES = 128
MOD_ROWS = 8
V7X_VMEM_BYTES = 64 * 1024 * 1024
VMEM_LIMIT = V7X_VMEM_BYTES - 8 * 1024 * 1024

F32 = jnp.float32
BF16 = jnp.bfloat16

COL_XA, COL_HQ, COL_HF_FWD, COL_HF_BWD, COL_HI, COL_HGATE, COL_XC, COL_GU, COL_GV = range(9)
HG_Z_W = 2 * GROUP_W
HG_QV_W = 2 * GROUP_W
OP_IN_W = 4 * GROUP_W


def _mod_row(i, tb):
    return jnp.where(i < N_CTX // tb, 0, 1 + (i - N_CTX // tb) // (DEC_SEQ // tb))


GELU_C0 = 0.7978845608028654
GELU_C1 = GELU_C0 * 0.044715


def _gelu(x):
    return x * (0.5 + 0.5 * jnp.tanh(x * (GELU_C0 + GELU_C1 * (x * x))))


def _rms(x, g):
    return x * lax.rsqrt(jnp.mean(x * x, axis=-1, keepdims=True) + EPS) * g


def _dot(a, b):
    return jnp.dot(a, b, preferred_element_type=F32)


def _dot_nt(a, b, precision=None):
    return lax.dot_general(a, b, (((1,), (1,)), ((), ())), precision=precision, preferred_element_type=F32)


def _dot_tn(a, b):
    return lax.dot_general(a, b, (((0,), (0,)), ((), ())), preferred_element_type=F32)


def _const_spec(shape, single=False):
    kw = {"pipeline_mode": pl.Buffered(1)} if single else {}
    return pl.BlockSpec(shape, lambda *_: (0,) * len(shape), **kw)


def _layer_spec(shape, l, single=False):
    kw = {"pipeline_mode": pl.Buffered(1)} if single else {}
    return pl.BlockSpec((None,) + tuple(shape), lambda *_: (l,) + (0,) * len(shape), **kw)


def _mod_spec(l, tb=TB):
    return pl.BlockSpec((None, None, 1, N_MOD * D_MODEL), lambda i: (l, _mod_row(i, tb), 0, 0))


def _x_specs(split, width=D_MODEL, tb=TB):
    nb_ctx = N_CTX // tb
    if split:
        return [pl.BlockSpec((tb, width), lambda i: (jnp.minimum(i, nb_ctx - 1), 0)),
                pl.BlockSpec((tb, width), lambda i: (jnp.maximum(i - nb_ctx, 0), 0))]
    return [pl.BlockSpec((tb, width), lambda i: (i, 0))]


def _x_value(x_refs, tb=TB):
    if len(x_refs) == 1:
        return x_refs[0][...]
    return jnp.where(pl.program_id(0) < N_CTX // tb, x_refs[0][...], x_refs[1][...])


INPROJ_SUB = 256
INPROJ_OUT = ((HG_Z_W, F32), (HG_QV_W, BF16), (OP_IN_W, BF16), (GROUP_W, BF16), (LANES, F32), (LANES, F32))


def _inproj_kernel(*refs):
    *x_refs, mod_ref, g_ref, w_ref, hz_ref, hqv_ref, op_ref, xc_ref, xalo_ref, xahi_ref, wb_ref = refs

    @pl.when(pl.program_id(0) == 0)
    def _():
        wb_ref[...] = w_ref[...].astype(BF16)

    mod = mod_ref[...]
    sh = mod[:, 0:D_MODEL]
    sc = mod[:, D_MODEL:2 * D_MODEL]
    is_ctx = pl.program_id(0) < N_CTX // TBP
    for r in range(0, TBP, INPROJ_SUB):
        rows = slice(r, r + INPROJ_SUB)
        x = x_refs[0][rows, :] if len(x_refs) == 1 else jnp.where(is_ctx, x_refs[0][rows, :], x_refs[1][rows, :])
        h = _rms(x, g_ref[...]) * (1.0 + sc) + sh
        o = _dot(h.astype(BF16), wb_ref[...])
        col = lambda c, n=1: o[:, GROUP_W * c:GROUP_W * (c + n)]
        hz_ref[rows, :] = col(COL_HF_FWD, 2)
        hqv_ref[rows, :] = jnp.concatenate([col(COL_HQ), col(COL_HI)], axis=1).astype(BF16)
        op_ref[rows, :] = jnp.concatenate([col(COL_XA), col(COL_HGATE), col(COL_GU, 2)], axis=1).astype(BF16)
        xc_ref[rows, :] = col(COL_XC).astype(BF16)
        xalo_ref[rows, :] = o[:, 0:LANES]
        xahi_ref[rows, :] = o[:, LANES:2 * LANES]


def _inproj(xs, mods, norm_g, w_in, l):
    return pl.pallas_call(
        _inproj_kernel,
        grid=(N_TOK // TBP,),
        in_specs=_x_specs(len(xs) == 2, tb=TBP) + [
            _mod_spec(l, TBP),
            _layer_spec((1, D_MODEL), l),
            _layer_spec((D_MODEL, D_IN), l, single=True),
        ],
        out_specs=tuple(pl.BlockSpec((TBP, w), lambda i: (i, 0)) for w, _ in INPROJ_OUT),
        out_shape=tuple(jax.ShapeDtypeStruct((N_TOK, w), dt) for w, dt in INPROJ_OUT),
        scratch_shapes=[pltpu.VMEM((D_MODEL, D_IN), BF16)],
        compiler_params=pltpu.CompilerParams(vmem_limit_bytes=VMEM_LIMIT),
        name="in_proj",
    )(*xs, mods, norm_g, w_in)


S5_NPAIR = S5_G // 2
S5_CW = S5_L * S5_P
S5_SW = S5_G * S5_N
S5_STEP_ROWS = N_CTX // S5_L
S5_CTX_NC = SEQ // S5_L
S5_LAT_NC = DEC_SEQ // S5_L


def _s5_slot(g, t):
    return (t + g) % S5_L


def _s5_tables(lam_re, lam_im, log_dt, b_re, b_im, c_re, c_im):
    lr = jnp.minimum(lam_re.astype(F32), LAM_RE_MAX)
    li = lam_im.astype(F32)
    dt = jnp.exp(log_dt.astype(F32))[..., None]
    mag = jnp.exp(lr * dt)
    ang = li * dt
    ab_re = mag * jnp.cos(ang)
    ab_im = mag * jnp.sin(ang)
    den = lr * lr + li * li
    xr = ab_re - 1.0
    z_re = (xr * lr + ab_im * li) / den
    z_im = (ab_im * lr - xr * li) / den
    bb_re = z_re[..., None] * b_re - z_im[..., None] * b_im
    bb_im = z_re[..., None] * b_im + z_im[..., None] * b_re
    tau = jnp.arange(S5_L + 1, dtype=F32)[:, None, None, None, None]
    pm = jnp.exp(lr * dt * tau)
    pa = li * dt * tau
    pw_re = pm * jnp.cos(pa)
    pw_im = pm * jnp.sin(pa)
    eye2 = jnp.eye(2, dtype=F32)

    def pw_pairs(a):
        a = a.transpose(1, 2, 3, 0, 4).reshape(DEPTH, 2, S5_NPAIR, 2, S5_L + 1, S5_N)
        return a.transpose(0, 1, 2, 4, 3, 5).reshape(DEPTH, 2, S5_NPAIR, S5_L + 1, 2 * S5_N)

    def mat_pairs(a):
        a = a.reshape(DEPTH, 2, S5_NPAIR, 2, S5_P, S5_N)
        return jnp.einsum('ldjaqn,ab->ldjaqbn', a, eye2).reshape(DEPTH, 2, S5_NPAIR, 2, S5_P, 2 * S5_N)

    al = jnp.stack([pw_re[S5_L], pw_im[S5_L]], axis=2).reshape(DEPTH, 2, 2, 1, S5_SW)
    return (pw_pairs(pw_re), pw_pairs(pw_im),
            mat_pairs(bb_re.transpose(0, 1, 2, 4, 3)), mat_pairs(bb_im.transpose(0, 1, 2, 4, 3)),
            mat_pairs(c_re.astype(F32)), mat_pairs(c_im.astype(F32)), al)


def _s5_shift_consts():
    n = S5_CW
    r = np.arange(n)[:, None]
    c = np.arange(n)[None, :]
    fwd = [(c == r + S5_P * s) for s in range(S5_L)]
    bwd = [(c == r - S5_P * (S5_L - 1 - s)) for s in range(S5_L)]
    return np.stack([np.concatenate(fwd, axis=1), np.concatenate(bwd, axis=1)]).astype(np.float32)


PREP_STEPS = 4
ADA_TN = N_MOD * D_MODEL // PREP_STEPS


def _prep_kernel(c_ref, wada_ref, bada_ref, pwr_ref, pwi_ref, br_ref, bi_ref, cr_ref, ci_ref, scat_ref,
                 mod_ref, m_ref, f_ref, e_ref, k_ref):
    cv = c_ref[...]
    mod_ref[0] = _dot((cv * jax.nn.sigmoid(cv)).astype(BF16), wada_ref[0].astype(BF16)) + bada_ref[0]
    step = pl.program_id(1)
    for qq in range(PREP_STEPS):
        pl.when(step == qq)(functools.partial(_s5_pair_ops, qq, pwr_ref, pwi_ref, br_ref, bi_ref, cr_ref,
                                              ci_ref, f_ref, e_ref, k_ref))
    pl.when(step == PREP_STEPS - 1)(functools.partial(_s5_toeplitz, scat_ref, m_ref, k_ref))


def _s5_pair_ops(qq, pwr_ref, pwi_ref, br_ref, bi_ref, cr_ref, ci_ref, f_ref, e_ref, k_ref):
    L = S5_L
    hp = lax.Precision.HIGHEST
    sw = 2 * S5_N
    per_step = S5_NPAIR // PREP_STEPS
    for j in range(per_step * qq, per_step * (qq + 1)):
        for d in range(2):
            pwr = pwr_ref[0, d, j]
            pwi = pwi_ref[0, d, j]
            for gl in range(2):
                g = 2 * j + gl
                br, bi = br_ref[0, d, j, gl], bi_ref[0, d, j, gl]
                cr, ci = cr_ref[0, d, j, gl], ci_ref[0, d, j, gl]
                ca_r, ca_i, f_r, f_i, e_r, e_i = [], [], [], [], [], []
                for t in range(L):
                    kk = t if d == 0 else L - 1 - t
                    pr, pi = pwr[kk:kk + 1], pwi[kk:kk + 1]
                    ca_r.append(cr * pr - ci * pi)
                    ca_i.append(cr * pi + ci * pr)
                    kf = L - 1 - t if d == 0 else t
                    pr, pi = pwr[kf:kf + 1], pwi[kf:kf + 1]
                    f_r.append(br * pr - bi * pi)
                    f_i.append(br * pi + bi * pr)
                    ke = t + 1 if d == 0 else L - t
                    pr, pi = pwr[ke:ke + 1], pwi[ke:ke + 1]
                    e_r.append(cr * pr - ci * pi)
                    e_i.append(-(cr * pi + ci * pr))
                cat = lambda xs: jnp.concatenate(xs, axis=0)
                k = _dot_nt(br, cat(ca_r), hp) - _dot_nt(bi, cat(ca_i), hp)
                k_ref[d, S5_P * g:S5_P * (g + 1), :] = k
                slots = lambda xs: cat([xs[(p - g) % L] for p in range(L)])
                rows = slice(S5_CW * gl, S5_CW * (gl + 1))
                f_ref[0, d, j, rows, 0:sw] = slots(f_r).astype(BF16)
                f_ref[0, d, j, rows, sw:2 * sw] = slots(f_i).astype(BF16)
                e_ref[0, d, 0, j, rows, :] = slots(e_r).astype(BF16)
                e_ref[0, d, 1, j, rows, :] = slots(e_i).astype(BF16)


def _s5_toeplitz(scat_ref, m_ref, k_ref):
    L = S5_L
    kf = k_ref[0].astype(BF16)
    kb = k_ref[1].astype(BF16)
    for s in range(L):
        cols = slice(S5_CW * s, S5_CW * (s + 1))
        res = _dot(kf, scat_ref[0, :, cols]) + _dot(kb, scat_ref[1, :, cols])
        for g in range(S5_G):
            blk = res[S5_P * g:S5_P * (g + 1), :]
            if g:
                blk = pltpu.roll(blk, S5_P * g, 1)
            r0 = S5_P * _s5_slot(g, s)
            m_ref[0, g, r0:r0 + S5_P, :] = blk.astype(BF16)


def _prep(cvecs, w_ada, b_ada, tables, scat):
    pwr, pwi, br, bi, cr, ci, _ = tables
    n = N_MOD * D_MODEL

    def lspec(shape):
        return pl.BlockSpec((1,) + shape, lambda l, q: (l,) + (0,) * len(shape))

    pw_shape = (2, S5_NPAIR, S5_L + 1, 2 * S5_N)
    mat_shape = (2, S5_NPAIR, 2, S5_P, 2 * S5_N)
    return pl.pallas_call(
        _prep_kernel,
        grid=(DEPTH, PREP_STEPS),
        in_specs=[pl.BlockSpec((MOD_ROWS, D_MODEL), lambda l, q: (0, 0)),
                  pl.BlockSpec((1, D_MODEL, ADA_TN), lambda l, q: (l, 0, q)),
                  pl.BlockSpec((1, 1, ADA_TN), lambda l, q: (l, 0, q)),
                  lspec(pw_shape), lspec(pw_shape), lspec(mat_shape), lspec(mat_shape), lspec(mat_shape),
                  lspec(mat_shape), _const_spec(scat.shape)],
        out_specs=(pl.BlockSpec((1, MOD_ROWS, ADA_TN), lambda l, q: (l, 0, q)),
                   lspec((S5_G, S5_CW, S5_CW)), lspec((2, S5_NPAIR, 2 * S5_CW, 4 * S5_N)),
                   lspec((2, 2, S5_NPAIR, 2 * S5_CW, 2 * S5_N))),
        out_shape=(jax.ShapeDtypeStruct((DEPTH, MOD_ROWS, n), F32),
                   jax.ShapeDtypeStruct((DEPTH, S5_G, S5_CW, S5_CW), BF16),
                   jax.ShapeDtypeStruct((DEPTH, 2, S5_NPAIR, 2 * S5_CW, 4 * S5_N), BF16),
                   jax.ShapeDtypeStruct((DEPTH, 2, 2, S5_NPAIR, 2 * S5_CW, 2 * S5_N), BF16)),
        scratch_shapes=[pltpu.VMEM((2, S5_G * S5_P, S5_CW), F32)],
        compiler_params=pltpu.CompilerParams(vmem_limit_bytes=VMEM_LIMIT),
        name="param_prep",
    )(cvecs, w_ada, b_ada.reshape(DEPTH, 1, n), pwr, pwi, br, bi, cr, ci, scat)


def _s5_kernel(xlo_ref, xhi_ref, m_ref, f_ref, e_ref, al_ref, h0_ref, ylo_ref, yhi_ref, hfin_ref,
               uy_ref, zh_ref):
    R = S5_STEP_ROWS
    pw = 2 * S5_CW
    sw = 2 * S5_N
    gph = S5_G // 2
    step = pl.program_id(0)

    for half, x_ref in enumerate((xlo_ref, xhi_ref)):
        for t in range(S5_L):
            xt = x_ref[pl.ds(t, R, stride=S5_L), :]
            shift = (S5_P * t) % LANES
            xr = pltpu.roll(xt, shift, 1) if shift else xt
            for gl in range(gph):
                g = gph * half + gl
                src = (S5_P * gl + shift) % LANES
                lo = S5_CW * g + S5_P * _s5_slot(g, t)
                assert lo % LANES == src
                uy_ref[:, lo:lo + S5_P] = xr[:, src:src + S5_P]

    for j in range(S5_NPAIR):
        ub = uy_ref[:, pw * j:pw * (j + 1)].astype(BF16)
        for d in range(2):
            o = _dot(ub, f_ref[d, j])
            zh_ref[d, 0, :, sw * j:sw * (j + 1)] = o[:, :sw]
            zh_ref[d, 1, :, sw * j:sw * (j + 1)] = o[:, sw:]

    def scan_group(seq_rows, nc, init, fin_rows):
        chains = [(d, r0) for d in range(2) for r0 in seq_rows]

        def body(c, carry):
            out = []
            for k, (d, r0) in enumerate(chains):
                hr, hi = carry[2 * k], carry[2 * k + 1]
                r = r0 + (c if d == 0 else nc - 1 - c)
                zr = zh_ref[d, 0, pl.ds(r, 1), :]
                zi = zh_ref[d, 1, pl.ds(r, 1), :]
                zh_ref[d, 0, pl.ds(r, 1), :] = hr
                zh_ref[d, 1, pl.ds(r, 1), :] = hi
                ar = al_ref[d, 0]
                ai = al_ref[d, 1]
                out.append(ar * hr - ai * hi + zr)
                out.append(ar * hi + ai * hr + zi)
            return tuple(out)

        fin = lax.fori_loop(0, nc, body, tuple(init))
        if fin_rows is not None:
            for k, (d, _) in enumerate(chains):
                s = fin_rows[k % len(seq_rows)]
                hfin_ref[d, 0, s:s + 1, :] = fin[2 * k]
                hfin_ref[d, 1, s:s + 1, :] = fin[2 * k + 1]

    @pl.when(step == 0)
    def _():
        zero = jnp.zeros((1, S5_SW), F32)
        for s0 in range(0, BATCH, 4):
            seqs = list(range(s0, s0 + 4))
            scan_group([s * S5_CTX_NC for s in seqs], S5_CTX_NC, [zero] * 16, seqs)

    @pl.when(step == 1)
    def _():
        init = []
        for d in range(2):
            for s in range(DEC_BATCH):
                init.append(h0_ref[d, 0, s:s + 1, :])
                init.append(h0_ref[d, 1, s:s + 1, :])
        scan_group([s * S5_LAT_NC for s in range(DEC_BATCH)], S5_LAT_NC, init, None)

    for j in range(S5_NPAIR):
        acc = None
        for d in range(2):
            for ri in range(2):
                hb = zh_ref[d, ri, :, sw * j:sw * (j + 1)].astype(BF16)
                t = _dot_nt(hb, e_ref[d, ri, j])
                acc = t if acc is None else acc + t
        for gl in range(2):
            g = 2 * j + gl
            ug = uy_ref[:, S5_CW * g:S5_CW * (g + 1)].astype(BF16)
            uy_ref[:, S5_CW * g:S5_CW * (g + 1)] = acc[:, S5_CW * gl:S5_CW * (gl + 1)] + _dot(ug, m_ref[g])

    lane = lax.broadcasted_iota(jnp.int32, (R, LANES), 1)
    in_piece = [jnp.logical_and(lane >= S5_P * k, lane < S5_P * (k + 1)) for k in range(LANES // S5_P)]
    for half, y_ref in enumerate((ylo_ref, yhi_ref)):
        for t in range(S5_L):
            merged = None
            for gl in range(gph):
                g = gph * half + gl
                slot = _s5_slot(g, t)
                col = S5_CW * g + LANES * (S5_P * slot // LANES)
                v = uy_ref[:, col:col + LANES]
                merged = v if merged is None else jnp.where(in_piece[slot % (LANES // S5_P)], v, merged)
            shift = (LANES - S5_P * t % LANES) % LANES
            y_ref[pl.ds(t, R, stride=S5_L), :] = pltpu.roll(merged, shift, 1) if shift else merged


def _s5(xa_lo, xa_hi, m, f2, e2, al, h0, l):
    half = pl.BlockSpec((N_CTX, LANES), lambda s: (s, 0))
    return pl.pallas_call(
        _s5_kernel,
        grid=(2,),
        in_specs=[half, half,
                  _layer_spec(m.shape[1:], l, single=True), _layer_spec(f2.shape[1:], l, single=True),
                  _layer_spec(e2.shape[1:], l, single=True), _layer_spec(al.shape[1:], l),
                  _layer_spec(h0.shape[1:], l)],
        out_specs=(half, half, _const_spec((2, 2, BATCH, S5_SW))),
        out_shape=(jax.ShapeDtypeStruct((N_TOK, LANES), F32),
                   jax.ShapeDtypeStruct((N_TOK, LANES), F32),
                   jax.ShapeDtypeStruct((2, 2, BATCH, S5_SW), F32)),
        scratch_shapes=[pltpu.VMEM((S5_STEP_ROWS, S5_G * S5_CW), F32),
                        pltpu.VMEM((2, 2, S5_STEP_ROWS, S5_SW), F32)],
        compiler_params=pltpu.CompilerParams(vmem_limit_bytes=VMEM_LIMIT),
        name="s5_mixer",
    )(xa_lo, xa_hi, m, f2, e2, al, h0)


HG_NLEV = int(math.log2(HG_L))
HG_W = HG_HEADS * HG_DK
HG_CPS = 8
HG_BLK = HG_CPS * HG_L
HG_NB_CTX = N_CTX // HG_BLK
HG_SPB = HG_BLK // SEQ
HG_CPQ = SEQ // HG_L
HG_NC_LAT = DEC_SEQ // HG_BLK
assert HG_SPB >= 1 and HG_SPB * SEQ == HG_BLK
assert HG_L == HG_DK


def _hgrn_consts():
    L = HG_L
    w = np.zeros((HG_NLEV + 2, L, L), np.float32)
    mask = np.zeros((HG_NLEV + 1, L, L), np.float32)
    for lev in range(HG_NLEV):
        blk = L >> lev
        half = blk // 2
        for t in range(L):
            p, o = divmod(t, blk)
            bd = p * blk + half - 1
            if o >= half:
                w[lev, t, bd + 1:t + 1] = 1.0
            else:
                w[lev, t, t + 1:bd + 1] = 1.0
        jj, ii = np.meshgrid(np.arange(L), np.arange(L), indexing='ij')
        mask[lev] = ((jj // blk == ii // blk) & (jj % blk >= half) & (ii % blk < half)).astype(np.float32)
    mask[HG_NLEV] = np.eye(L, dtype=np.float32)
    for t in range(L):
        w[HG_NLEV, t, :t + 1] = 1.0
        w[HG_NLEV + 1, t, t + 1:] = 1.0
    out = []
    for wd, md in ((w, mask), (w[:, ::-1, ::-1], mask[:, ::-1, ::-1])):
        wflat = wd.reshape((HG_NLEV + 2) * L, L)
        out.append((np.concatenate([wflat] * 3, axis=1), np.tile(md, (1, 1, HG_HEADS))))
    wcat = np.stack([out[0][0], out[1][0]])
    mask4 = np.stack([out[0][1], out[1][1]])
    hm = np.kron(np.eye(HG_HEADS, dtype=np.float32), np.ones((HG_DK, HG_DK), np.float32))
    return wcat, mask4, hm


def _hg_pos(i):
    is_ctx = i < HG_NB_CTX
    c = jnp.where(is_ctx, 0, (i - HG_NB_CTX) % HG_NC_LAT)
    nc = jnp.where(is_ctx, 1, HG_NC_LAT)
    return is_ctx, c, nc


def _hg_bwd_blk(i):
    _, c, nc = _hg_pos(i)
    return i + nc - 1 - 2 * c


def _hg_local(items, lb_ref, wcat_ref, mask_ref, hm):
    L = HG_L
    hmb = hm.astype(BF16)

    def bd4(x):
        return jnp.concatenate([x] * HG_HEADS, axis=0) * hmb

    kks, exs = [], []
    for d, q, z, v in items:
        lbp = lb_ref[d]
        sp = jnp.log1p(jnp.exp(-jnp.abs(z)))
        ls = jnp.minimum(z, 0.0) - sp
        kks.append(lbp[2:3] * jnp.exp(jnp.minimum(-z, 0.0) - sp))
        a = lbp[0:1]
        b = lbp[1:2] + ls
        logf = jnp.maximum(a, b) + jnp.log1p(jnp.exp(-jnp.abs(a - b)))
        p0 = logf.astype(BF16)
        r0 = logf - p0.astype(F32)
        p1 = r0.astype(BF16)
        p2 = (r0 - p1.astype(F32)).astype(BF16)
        exs.append(jnp.exp(_dot(wcat_ref[d], jnp.concatenate([p0, p1, p2], axis=0))))
    scs = [None] * len(items)
    for lev in range(HG_NLEV + 1):
        for n, (d, q, z, v) in enumerate(items):
            if lev < HG_NLEV:
                al = exs[n][L * lev:L * (lev + 1)]
                lhs = (q * al).astype(BF16)
                rhs = (kks[n] * al).astype(BF16)
            else:
                lhs = q.astype(BF16)
                rhs = kks[n].astype(BF16)
            t = _dot_nt(lhs, bd4(rhs)) * mask_ref[d, lev]
            scs[n] = t if scs[n] is None else scs[n] + t
    out = []
    for n, (d, q, z, v) in enumerate(items):
        vb = v.astype(BF16)
        o = _dot(scs[n].astype(BF16), bd4(vb))
        eq = exs[n][L * HG_NLEV:L * (HG_NLEV + 1)]
        qe = (q * eq).astype(BF16)
        ke = (kks[n] * exs[n][L * (HG_NLEV + 1):L * (HG_NLEV + 2)]).astype(BF16)
        g = eq[L - 1:L] if d == 0 else eq[0:1]
        out.append((o, qe, g, _dot_tn(vb, ke) * hm))
    return out


def _hgrn_kernel(zf_ref, qvf_ref, zb_ref, qvb_ref, lb_ref, w_ref, mask_ref, hm_ref, s0_ref, wup_ref, wdn_ref,
                 of_ref, ob_ref, sfin_ref, wupb_ref, wdnb_ref, st_ref):
    is_ctx, c, nc = _hg_pos(pl.program_id(0))
    wupb_ref[...] = wup_ref[...].astype(BF16)
    wdnb_ref[...] = wdn_ref[...].astype(BF16)

    hm = hm_ref[...]
    rows = [slice(HG_L * k, HG_L * (k + 1)) for k in range(HG_CPS)]
    items = [(d, qv_ref[r, 0:GROUP_W], z_ref[r, GROUP_W * d:GROUP_W * (d + 1)], qv_ref[r, GROUP_W:2 * GROUP_W])
             for d, (z_ref, qv_ref) in enumerate(((zf_ref, qvf_ref), (zb_ref, qvb_ref))) for r in rows]
    loc = _hg_local(items, lb_ref, w_ref, mask_ref, hm)

    def chain(d, o_ref, st, chunks):
        for k in (chunks if d == 0 else reversed(chunks)):
            o, qe, g, upd = loc[HG_CPS * d + k]
            o_ref[rows[k], :] = o + _dot_nt(qe, st.astype(BF16))
            st = st * g + upd
        return st

    @pl.when(is_ctx)
    def _():
        for sq in range(HG_SPB):
            for d, o_ref in enumerate((of_ref, ob_ref)):
                st = chain(d, o_ref, jnp.zeros((HG_W, HG_W), F32), range(HG_CPQ * sq, HG_CPQ * (sq + 1)))
                s_kv = st.T
                for h in range(HG_HEADS):
                    sfin_ref[sq, d, h] = s_kv[HG_DK * h:HG_DK * (h + 1), HG_DK * h:HG_DK * (h + 1)]

    @pl.when(jnp.logical_not(is_ctx))
    def _():
        @pl.when(c == 0)
        def _():
            st_ref[...] = s0_ref[0]

        for d, o_ref in enumerate((of_ref, ob_ref)):
            st_ref[d] = chain(d, o_ref, st_ref[d], range(HG_CPS))


def _hgrn(hz, hqv, lbp, consts, s0, w_up, w_down, l):
    wcat, mask4, hm = consts
    steps = N_TOK // HG_BLK
    up_rows = D_MODEL // steps
    dn_rows = 2 * D_FF // steps

    return pl.pallas_call(
        _hgrn_kernel,
        grid=(N_TOK // HG_BLK,),
        in_specs=[pl.BlockSpec((HG_BLK, HG_Z_W), lambda i: (i, 0)),
                  pl.BlockSpec((HG_BLK, HG_QV_W), lambda i: (i, 0)),
                  pl.BlockSpec((HG_BLK, HG_Z_W), lambda i: (_hg_bwd_blk(i), 0)),
                  pl.BlockSpec((HG_BLK, HG_QV_W), lambda i: (_hg_bwd_blk(i), 0)),
                  _layer_spec(lbp.shape[1:], l), _const_spec(wcat.shape), _const_spec(mask4.shape),
                  _const_spec(hm.shape),
                  pl.BlockSpec((None, 1, 2, HG_W, HG_W),
                               lambda i: (l, jnp.maximum(i - HG_NB_CTX, 0) // HG_NC_LAT, 0, 0, 0)),
                  pl.BlockSpec((None, up_rows, 2 * D_FF), lambda i: (l, i, 0)),
                  pl.BlockSpec((None, dn_rows, D_MODEL), lambda i: (l, i // 2, 0))],
        out_specs=(pl.BlockSpec((HG_BLK, HG_W), lambda i: (i, 0)),
                   pl.BlockSpec((HG_BLK, HG_W), lambda i: (_hg_bwd_blk(i), 0)),
                   pl.BlockSpec((HG_SPB, 2, HG_HEADS, HG_DK, HG_DK),
                                lambda i: (jnp.minimum(i, HG_NB_CTX - 1), 0, 0, 0, 0)),
                   pl.BlockSpec((up_rows, 2 * D_FF), lambda i: (i, 0)),
                   pl.BlockSpec((dn_rows, D_MODEL), lambda i: (i // 2, 0))),
        out_shape=(jax.ShapeDtypeStruct((N_TOK, HG_W), F32),
                   jax.ShapeDtypeStruct((N_TOK, HG_W), F32),
                   jax.ShapeDtypeStruct((BATCH, 2, HG_HEADS, HG_DK, HG_DK), F32),
                   jax.ShapeDtypeStruct((D_MODEL, 2 * D_FF), BF16),
                   jax.ShapeDtypeStruct((D_FF, D_MODEL), BF16)),
        scratch_shapes=[pltpu.VMEM((2, HG_W, HG_W), F32)],
        compiler_params=pltpu.CompilerParams(vmem_limit_bytes=VMEM_LIMIT),
        name="hgrn_mixer",
    )(hz, hqv, hz, hqv, lbp, wcat, mask4, hm, s0, w_up, w_down)


FN_CTX_SPG = 4
FN_LAT_TR = 512


def _dft_consts(t_len):
    n = GROUP_W // 4
    k = np.arange(n)
    ang = 2.0 * np.pi * ((k[:, None] * k[None, :]) % n) / n
    eye = np.eye(4)
    cs = np.concatenate([np.kron(eye, np.cos(ang)), np.kron(eye, np.sin(ang))], axis=1) / math.sqrt(n)
    t = np.arange(t_len)
    angt = 2.0 * np.pi * ((t[:, None] * t[None, :]) % t_len) / t_len
    dft = np.concatenate([np.cos(angt), -np.sin(angt)], axis=1) / math.sqrt(t_len)
    return cs.astype(np.float32), dft.astype(np.float32)


def _fnet_kernel(x_ref, cs_ref, dft_ref, w_ref, o_ref, r_ref, *, t_len, spg, tr):
    @pl.when(pl.program_id(1) == 0)
    def _():
        t = _dot(x_ref[...].astype(BF16), cs_ref[...])
        for s in range(spg):
            rows = slice(t_len * s, t_len * (s + 1))
            r_ref[s, 0:t_len, :] = t[rows, :GROUP_W].astype(BF16)
            r_ref[s, t_len:2 * t_len, :] = t[rows, GROUP_W:].astype(BF16)

    w = w_ref[...].astype(BF16)
    for s in range(spg):
        y = _dot(dft_ref[...], r_ref[s])
        o_ref[tr * s:tr * (s + 1), :] = _dot(y.astype(BF16), w)


def _fnet(xc, cs, dft, fn_w, l, *, n_seq, seq_len, tok0, spg, tr):
    nj = seq_len // tr
    assert spg == 1 or nj == 1
    gb0 = tok0 // (spg * seq_len)
    return pl.pallas_call(
        functools.partial(_fnet_kernel, t_len=seq_len, spg=spg, tr=tr),
        grid=(n_seq // spg, nj),
        in_specs=[
            pl.BlockSpec((spg * seq_len, GROUP_W), lambda g, j: (gb0 + g, 0)),
            _const_spec((GROUP_W, 2 * GROUP_W)),
            pl.BlockSpec((tr, 2 * seq_len), lambda g, j: (j, 0)),
            _layer_spec((GROUP_W, GROUP_W), l),
        ],
        out_specs=pl.BlockSpec((spg * tr, GROUP_W), lambda g, j: (g * nj + j, 0)),
        out_shape=jax.ShapeDtypeStruct((n_seq * seq_len, GROUP_W), F32),
        scratch_shapes=[pltpu.VMEM((spg, 2 * seq_len, GROUP_W), BF16)],
        compiler_params=pltpu.CompilerParams(vmem_limit_bytes=VMEM_LIMIT),
        name="fnet_mixer",
    )(xc, cs, dft, fn_w)


OUTPROJ_SUB = 256


def _outproj_kernel(*refs):
    (op_ref, ylo_ref, yhi_ref, of_ref, ob_ref, ycc_ref, ycl_ref, mod_ref, d_ref,
     wglu_ref, gn_ref, bdm_ref, gmn_ref, gmw_ref, gmb_ref, gmh_ref, wout_ref, n2_ref,
     x1_ref, h2_ref, wob_ref) = refs[-21:]
    x_refs = refs[:-21]
    xa_ref, hgate_ref, gu_ref, gv_ref = (op_ref.at[:, GROUP_W * k:GROUP_W * (k + 1)] for k in range(4))

    @pl.when(pl.program_id(0) == 0)
    def _():
        wob_ref[...] = wout_ref[...].astype(BF16)

    gn = gn_ref[...]
    gm_hm = gmh_ref[...]
    mod = mod_ref[...]
    g1 = mod[:, 2 * D_MODEL:3 * D_MODEL]
    sh2 = mod[:, 3 * D_MODEL:4 * D_MODEL]
    sc2 = mod[:, 4 * D_MODEL:5 * D_MODEL]
    is_ctx = pl.program_id(0) < N_CTX // TBP
    pick = lambda refs, rows: (refs[0][rows, :] if len(refs) == 1
                               else jnp.where(is_ctx, refs[0][rows, :], refs[1][rows, :]))
    for r in range(0, TBP, OUTPROJ_SUB):
        rows = slice(r, r + OUTPROJ_SUB)
        ys = jnp.concatenate([ylo_ref[rows, :], yhi_ref[rows, :]], axis=1)
        y5 = _gelu(ys + d_ref[...] * xa_ref[rows, :].astype(F32))
        glu = jax.nn.sigmoid(_dot(y5.astype(BF16), wglu_ref[...].astype(BF16)))
        out_a = _rms(y5 * glu, gn[:, 0:GROUP_W])
        o = of_ref[rows, :] + ob_ref[rows, :]
        o2 = o * o
        o2h = o2.astype(BF16)
        o2l = (o2 - o2h.astype(F32)).astype(BF16)
        ms = _dot(o2h, bdm_ref[...]) + _dot(o2l, bdm_ref[...])
        hg = hgate_ref[rows, :].astype(F32)
        out_b = o * lax.rsqrt(ms + EPS) * gn[:, GROUP_W:2 * GROUP_W] * (hg * jax.nn.sigmoid(hg))
        out_c = _rms(pick((ycc_ref, ycl_ref), rows), gn[:, 2 * GROUP_W:3 * GROUP_W])
        out_d = []
        for c in range(r // GM_CHUNK, (r + OUTPROJ_SUB) // GM_CHUNK):
            crow = slice(GM_CHUNK * c, GM_CHUNK * (c + 1))
            gv = _rms(_gelu(gv_ref[crow, :].astype(F32)), gmn_ref[...]).astype(BF16)
            g4 = jnp.concatenate([gv] * GM_HEADS, axis=0) * gm_hm
            sp = _dot(gmw_ref[...], g4) + gmb_ref[...]
            out_d.append(_rms(_gelu(gu_ref[crow, :].astype(F32)) * sp, gn[:, 3 * GROUP_W:]))
        out_d = jnp.concatenate(out_d, axis=0)
        m = None
        for k, part in enumerate((out_a, out_b, out_c, out_d)):
            t = _dot(part.astype(BF16), wob_ref[GROUP_W * k:GROUP_W * (k + 1), :])
            m = t if m is None else m + t
        x1 = pick(x_refs, rows) + g1 * m
        x1_ref[rows, :] = x1
        h2_ref[rows, :] = (_rms(x1, n2_ref[...]) * (1.0 + sc2) + sh2).astype(BF16)


def _outproj(xs, op_in, ylo, yhi, of, ob, yc, mods, s5_d, wglu, gn, bdm, gm_norm_g, gm_w, gm_b, gm_hm, wout,
             norm2_g, l):
    def tok(width, col=0):
        return pl.BlockSpec((TBP, width), lambda i: (i, col))

    return pl.pallas_call(
        _outproj_kernel,
        grid=(N_TOK // TBP,),
        in_specs=_x_specs(len(xs) == 2, tb=TBP) + [
            tok(OP_IN_W), tok(LANES), tok(LANES), tok(GROUP_W), tok(GROUP_W), *_x_specs(True, GROUP_W, TBP),
            _mod_spec(l, TBP),
            _layer_spec((1, GROUP_W), l), _layer_spec((GROUP_W, GROUP_W), l), _layer_spec((1, D_MODEL), l),
            _const_spec((GROUP_W, GROUP_W)),
            _layer_spec((1, GROUP_W), l), _layer_spec((GM_CHUNK, GM_HEADS * GM_CHUNK), l),
            _layer_spec((GM_CHUNK, GROUP_W), l), _const_spec((GM_HEADS * GM_CHUNK, GROUP_W)),
            _layer_spec((D_MODEL, D_MODEL), l, single=True), _layer_spec((1, D_MODEL), l),
        ],
        out_specs=(tok(D_MODEL), tok(D_MODEL)),
        out_shape=(jax.ShapeDtypeStruct((N_TOK, D_MODEL), F32),
                   jax.ShapeDtypeStruct((N_TOK, D_MODEL), BF16)),
        scratch_shapes=[pltpu.VMEM((D_MODEL, D_MODEL), BF16)],
        compiler_params=pltpu.CompilerParams(vmem_limit_bytes=VMEM_LIMIT),
        name="out_proj",
    )(*xs, op_in, ylo, yhi, of, ob, *yc, mods, s5_d, wglu, gn, bdm, gm_norm_g, gm_w, gm_b,
      gm_hm, wout, norm2_g)


FF_SEG = GRID_W
FF_NSEG = TB // FF_SEG
FF_SEQ_STRIPS = SEQ // FF_SEG
FF_PERM_ROWS = FF_SEQ_STRIPS * FF_SEG
SUBLANES = 8
FF_NGRP = FF_SEG // SUBLANES


def _ffn_perm_consts():
    p = np.zeros((FF_PERM_ROWS, FF_PERM_ROWS), np.float32)
    for k in range(FF_SEQ_STRIPS):
        for t in range(FF_SEG):
            p[FF_SEG * k + SUBLANES * (t % FF_NGRP) + t // FF_NGRP, FF_SEG * k + t] = 1.0
    return p


def _ffn_kernel(x1_ref, h2_ref, mod_ref, perm_ref, wup_ref, cw_ref, cb_ref, wdn_ref, fg_ref, *rest, final):
    *o_refs, h2p_ref, z_ref, hid_ref = rest
    i = pl.program_id(0)
    joined = (i < NB_CTX).astype(F32)
    n_tiles = D_FF // FF_TILE
    sub = lax.broadcasted_iota(jnp.int32, (SUBLANES, 2 * FF_TILE), 0)
    is_first = sub == 0
    is_last = sub == SUBLANES - 1
    zero_grp = jnp.zeros((SUBLANES, 2 * FF_TILE), F32)
    for r in range(0, TB, FF_PERM_ROWS):
        h2p_ref[r:r + FF_PERM_ROWS, :] = _dot(perm_ref[...], h2_ref[r:r + FF_PERM_ROWS, :]).astype(BF16)

    def up(j):
        lo = FF_TILE * j
        for c, col in enumerate((lo, D_FF + lo)):
            z_ref[j % 2, :, FF_TILE * c:FF_TILE * (c + 1)] = _dot(h2p_ref[...], wup_ref[:, col:col + FF_TILE])

    def gate(j):
        slot = j % 2
        lo = FF_TILE * j
        w = jnp.concatenate([cw_ref[:, lo:lo + FF_TILE], cw_ref[:, D_FF + lo:D_FF + lo + FF_TILE]], axis=1)
        b = jnp.concatenate([cb_ref[:, lo:lo + FF_TILE], cb_ref[:, D_FF + lo:D_FF + lo + FF_TILE]], axis=1)

        def grp(k, v):
            r = FF_SEG * k + SUBLANES * v
            return z_ref[slot, r:r + SUBLANES, :]

        down = [pltpu.roll(grp(k, FF_NGRP - 1), 1, 0) for k in range(FF_NSEG)]
        up_ = [pltpu.roll(grp(k, 0), SUBLANES - 1, 0) for k in range(FF_NSEG)]
        strips = []
        for k in range(FF_NSEG):
            g = [grp(k, v) for v in range(FF_NGRP)]
            before = down[k - 1] * joined if k % FF_SEQ_STRIPS > 0 else zero_grp
            after = up_[k + 1] * joined if k % FF_SEQ_STRIPS < FF_SEQ_STRIPS - 1 else zero_grp
            zm1 = jnp.concatenate([jnp.where(is_first, before, down[k])] + g[:-1], axis=0)
            zp1 = jnp.concatenate(g[1:] + [jnp.where(is_last, after, up_[k])], axis=0)
            zc = b + zm1 * w[0:1] + jnp.concatenate(g, axis=0) * w[1:2] + zp1 * w[2:3]
            strips.append((_gelu(zc[:, :FF_TILE]) * zc[:, FF_TILE:]).astype(BF16))
        hid_ref[:, lo:lo + FF_TILE] = jnp.concatenate(strips, axis=0)

    up(0)
    for j in range(n_tiles):
        if j + 1 < n_tiles:
            up(j + 1)
        gate(j)
    acc = _dot(hid_ref[...], wdn_ref[...])
    acc = jnp.swapaxes(acc.reshape(FF_NSEG, FF_NGRP, SUBLANES, D_MODEL), 1, 2).reshape(TB, D_MODEL)
    g2 = mod_ref[...][:, 5 * D_MODEL:6 * D_MODEL]
    x2 = x1_ref[...] + g2 * acc
    if final:
        y = _rms(x2, fg_ref[...])
        yp_ref, ys_ref = o_refs

        @pl.when(i < NB_CTX)
        def _():
            yp_ref[...] = y

        @pl.when(i >= NB_CTX)
        def _():
            ys_ref[...] = y
    else:
        o_refs[0][...] = x2


def _ffn(x1, h2, mods, perm, wup_bf, conv_w, conv_b, wdn_bf, final_g, l, *, final):
    if final:
        out_specs = tuple(_x_specs(True))
        out_shape = (jax.ShapeDtypeStruct((N_CTX, D_MODEL), F32), jax.ShapeDtypeStruct((N_LAT, D_MODEL), F32))
    else:
        out_specs = pl.BlockSpec((TB, D_MODEL), lambda i: (i, 0))
        out_shape = jax.ShapeDtypeStruct((N_TOK, D_MODEL), F32)
    return pl.pallas_call(
        functools.partial(_ffn_kernel, final=final),
        grid=(N_TOK // TB,),
        in_specs=[
            pl.BlockSpec((TB, D_MODEL), lambda i: (i, 0)),
            pl.BlockSpec((TB, D_MODEL), lambda i: (i, 0)),
            _mod_spec(l), _const_spec((FF_PERM_ROWS, FF_PERM_ROWS)),
            _const_spec((D_MODEL, 2 * D_FF), single=True), _layer_spec((3, 2 * D_FF), l),
            _layer_spec((1, 2 * D_FF), l), _const_spec((D_FF, D_MODEL), single=True),
            _const_spec((1, D_MODEL)),
        ],
        out_specs=out_specs,
        out_shape=out_shape,
        scratch_shapes=[pltpu.VMEM((TB, D_MODEL), BF16), pltpu.VMEM((2, TB, 2 * FF_TILE), F32),
                        pltpu.VMEM((TB, D_FF), BF16)],
        compiler_params=pltpu.CompilerParams(vmem_limit_bytes=VMEM_LIMIT),
        name="conv_ffn",
    )(x1, h2, mods, perm, wup_bf, conv_w, conv_b, wdn_bf, final_g.reshape(1, D_MODEL))


def kernel(x_prompt, x_sample, state_s5_re, state_s5_im, state_hgrn, c, c_ctx, w_ada, b_ada, norm1_g,
           norm2_g, w_in, s5_lam_re, s5_lam_im, s5_log_dt, s5_b_re, s5_b_im, s5_c_re, s5_c_im, s5_d,
           s5_w_glu, hg_lb_logits, fn_w, gm_norm_g, gm_ws, gm_bs, grp_norm_g, w_out, ffn_w_up,
           ffn_conv_w, ffn_conv_b, ffn_w_down, final_norm_g):
    xs = (x_prompt.reshape(N_CTX, D_MODEL), x_sample.reshape(N_LAT, D_MODEL))
    cvecs = jnp.concatenate([c_ctx[None], c, jnp.zeros((MOD_ROWS - 1 - DEC_BATCH, D_MODEL), F32)], axis=0)

    lb_p = jax.nn.softmax(hg_lb_logits.astype(F32), axis=0)
    lbs = jnp.maximum(jnp.cumsum(lb_p, axis=0) - lb_p[0], 0.0)
    lbps = jnp.stack([jnp.log(lbs), jnp.log1p(-lbs), 1.0 - lbs], axis=2)

    hg_consts_np = _hgrn_consts()
    hg_consts = (jnp.asarray(hg_consts_np[0], BF16), jnp.asarray(hg_consts_np[1], F32),
                 jnp.asarray(hg_consts_np[2], F32))
    cs_np, dft_ctx_np = _dft_consts(SEQ)
    _, dft_lat_np = _dft_consts(DEC_SEQ)
    cs = jnp.asarray(cs_np, F32).astype(BF16)
    dft_ctx = jnp.asarray(dft_ctx_np, F32).astype(BF16)
    dft_lat = jnp.asarray(dft_lat_np, F32).astype(BF16)
    gm_hm = jnp.asarray(np.kron(np.eye(GM_HEADS), np.ones((GM_CHUNK, GROUP_W // GM_HEADS))), BF16)
    bdm = jnp.asarray(np.kron(np.eye(HG_HEADS), np.ones((HG_DK, HG_DK))) / HG_DK, BF16)

    tables = _s5_tables(s5_lam_re, s5_lam_im, s5_log_dt, s5_b_re, s5_b_im, s5_c_re, s5_c_im)
    mods, s5_m, s5_f, s5_e = _prep(cvecs, w_ada, b_ada, tables, jnp.asarray(_s5_shift_consts(), BF16))
    mods = mods.reshape(DEPTH, MOD_ROWS, 1, N_MOD * D_MODEL)
    s5_al = tables[-1]
    s5_h0 = jnp.stack([state_s5_re, state_s5_im]).astype(F32).transpose(2, 3, 0, 1, 4, 5)
    s5_h0 = s5_h0.reshape(DEPTH, 2, 2, DEC_BATCH, S5_SW)
    hg_s0 = jnp.einsum('bldhkv,hg->lbdhvgk', state_hgrn.astype(F32), jnp.eye(HG_HEADS, dtype=F32))
    hg_s0 = hg_s0.reshape(DEPTH, DEC_BATCH, 2, HG_W, HG_W)

    def rows(a):
        return a.reshape(DEPTH, 1, a.shape[-1])

    norm1_r, norm2_r, gn_r, s5_d_r, gmn_r, cb_r = (rows(a) for a in (
        norm1_g, norm2_g, grp_norm_g, s5_d, gm_norm_g, ffn_conv_b))
    gm_w = gm_ws.transpose(0, 2, 1, 3).reshape(DEPTH, GM_CHUNK, GM_HEADS * GM_CHUNK).astype(BF16)
    gm_b = jnp.repeat(gm_bs.transpose(0, 2, 1), GROUP_W // GM_HEADS, axis=2)
    ff_perm = jnp.asarray(_ffn_perm_consts(), BF16)

    new_re, new_im, new_hg = [], [], []
    for l in range(DEPTH):
        hz, hqv, op_in, xc, xa_lo, xa_hi = _inproj(xs, mods, norm1_r, w_in, l)

        ylo, yhi, hfin = _s5(xa_lo, xa_hi, s5_m, s5_f, s5_e, s5_al, s5_h0, l)
        hfin = hfin.reshape(2, 2, BATCH, S5_G, S5_N).transpose(1, 2, 0, 3, 4)
        new_re.append(hfin[0])
        new_im.append(hfin[1])

        of, ob, sfin, wup_bf, wdn_bf = _hgrn(hz, hqv, lbps, hg_consts, hg_s0, ffn_w_up, ffn_w_down, l)
        new_hg.append(sfin)

        yc = (_fnet(xc, cs, dft_ctx, fn_w, l, n_seq=BATCH, seq_len=SEQ, tok0=0, spg=FN_CTX_SPG, tr=SEQ),
              _fnet(xc, cs, dft_lat, fn_w, l, n_seq=DEC_BATCH, seq_len=DEC_SEQ, tok0=N_CTX, spg=1,
                    tr=FN_LAT_TR))

        x1, h2 = _outproj(xs, op_in, ylo, yhi, of, ob, yc, mods, s5_d_r, s5_w_glu, gn_r, bdm, gmn_r, gm_w, gm_b,
                          gm_hm, w_out, norm2_r, l)
        res = _ffn(x1, h2, mods, ff_perm, wup_bf, ffn_conv_w, cb_r, wdn_bf, final_norm_g, l,
                   final=(l == DEPTH - 1))
        xs = (res,)

    y_prompt = res[0].reshape(BATCH, SEQ, D_MODEL)
    y_sample = res[1].reshape(DEC_BATCH, DEC_SEQ, D_MODEL)
    return (y_prompt, y_sample, jnp.stack(new_re, axis=1), jnp.stack(new_im, axis=1),
            jnp.stack(new_hg, axis=1))
```

```python
import functools
import math

import numpy as np
import jax
import jax.numpy as jnp
from jax import lax
from jax.experimental import pallas as pl
from jax.experimental.pallas import tpu as pltpu

D_MODEL = 1024
BATCH = 16
SEQ = 256
DEPTH = 2
DEC_BATCH = 2
DEC_SEQ = 2048
GRID_W = 64
GROUP_W = 256
S5_P = 16
S5_G = 16
S5_N = 64
HG_HEADS = 4
HG_DK = 64
GM_HEADS = 4
GM_CHUNK = 128
D_FF = 2816
N_MOD = 6
D_IN = 9 * GROUP_W
EPS = 1e-6
LAM_RE_MAX = -1e-4

N_CTX = BATCH * SEQ
N_LAT = DEC_BATCH * DEC_SEQ
N_TOK = N_CTX + N_LAT
TB = 512
TBP = 512
NB_CTX = N_CTX // TB
S5_L = 16
HG_L = 64
FF_TILE = 256
LANES = 128
MOD_ROWS = 8
V7X_VMEM_BYTES = 64 * 1024 * 1024
VMEM_LIMIT = V7X_VMEM_BYTES - 8 * 1024 * 1024

F32 = jnp.float32
BF16 = jnp.bfloat16

COL_XA, COL_HQ, COL_HF_FWD, COL_HF_BWD, COL_HI, COL_HGATE, COL_XC, COL_GU, COL_GV = range(9)
HG_Z_W = 2 * GROUP_W
HG_QV_W = 2 * GROUP_W
OP_IN_W = 4 * GROUP_W


def _mod_row(i, tb):
    return jnp.where(i < N_CTX // tb, 0, 1 + (i - N_CTX // tb) // (DEC_SEQ // tb))


GELU_C0 = 0.7978845608028654
GELU_C1 = GELU_C0 * 0.044715


def _gelu(x):
    return x * (0.5 + 0.5 * jnp.tanh(x * (GELU_C0 + GELU_C1 * (x * x))))


def _rms(x, g):
    return x * lax.rsqrt(jnp.mean(x * x, axis=-1, keepdims=True) + EPS) * g


def _dot(a, b):
    return jnp.dot(a, b, preferred_element_type=F32)


def _dot_nt(a, b, precision=None):
    return lax.dot_general(a, b, (((1,), (1,)), ((), ())), precision=precision, preferred_element_type=F32)


def _dot_tn(a, b):
    return lax.dot_general(a, b, (((0,), (0,)), ((), ())), preferred_element_type=F32)


def _const_spec(shape, single=False):
    kw = {"pipeline_mode": pl.Buffered(1)} if single else {}
    return pl.BlockSpec(shape, lambda *_: (0,) * len(shape), **kw)


def _layer_spec(shape, l, single=False):
    kw = {"pipeline_mode": pl.Buffered(1)} if single else {}
    return pl.BlockSpec((None,) + tuple(shape), lambda *_: (l,) + (0,) * len(shape), **kw)


def _mod_spec(l, tb=TB):
    return pl.BlockSpec((None, None, 1, N_MOD * D_MODEL), lambda i: (l, _mod_row(i, tb), 0, 0))


def _x_specs(split, width=D_MODEL, tb=TB):
    nb_ctx = N_CTX // tb
    if split:
        return [pl.BlockSpec((tb, width), lambda i: (jnp.minimum(i, nb_ctx - 1), 0)),
                pl.BlockSpec((tb, width), lambda i: (jnp.maximum(i - nb_ctx, 0), 0))]
    return [pl.BlockSpec((tb, width), lambda i: (i, 0))]


def _pick(refs, rows, is_ctx):
    if len(refs) == 1:
        return refs[0][rows, :]
    return jnp.where(is_ctx, refs[0][rows, :], refs[1][rows, :])


INPROJ_SUB = 256
INPROJ_OUT = ((HG_Z_W, F32), (HG_QV_W, BF16), (OP_IN_W, BF16), (GROUP_W, BF16), (LANES, F32), (LANES, F32))


def _inproj_kernel(*refs):
    *x_refs, mod_ref, g_ref, w_ref, hz_ref, hqv_ref, op_ref, xc_ref, xalo_ref, xahi_ref, wb_ref = refs

    @pl.when(pl.program_id(0) == 0)
    def _():
        wb_ref[...] = w_ref[...].astype(BF16)

    mod = mod_ref[...]
    sh = mod[:, 0:D_MODEL]
    sc = mod[:, D_MODEL:2 * D_MODEL]
    is_ctx = pl.program_id(0) < N_CTX // TBP
    for r in range(0, TBP, INPROJ_SUB):
        rows = slice(r, r + INPROJ_SUB)
        h = _rms(_pick(x_refs, rows, is_ctx), g_ref[...]) * (1.0 + sc) + sh
        o = _dot(h.astype(BF16), wb_ref[...])
        col = lambda c, n=1: o[:, GROUP_W * c:GROUP_W * (c + n)]
        hz_ref[rows, :] = col(COL_HF_FWD, 2)
        hqv_ref[rows, :] = jnp.concatenate([col(COL_HQ), col(COL_HI)], axis=1).astype(BF16)
        op_ref[rows, :] = jnp.concatenate([col(COL_XA), col(COL_HGATE), col(COL_GU, 2)], axis=1).astype(BF16)
        xc_ref[rows, :] = col(COL_XC).astype(BF16)
        xalo_ref[rows, :] = o[:, 0:LANES]
        xahi_ref[rows, :] = o[:, LANES:2 * LANES]


def _inproj(xs, mods, norm_g, w_in, l):
    return pl.pallas_call(
        _inproj_kernel,
        grid=(N_TOK // TBP,),
        in_specs=_x_specs(len(xs) == 2, tb=TBP) + [
            _mod_spec(l, TBP),
            _layer_spec((1, D_MODEL), l),
            _layer_spec((D_MODEL, D_IN), l, single=True),
        ],
        out_specs=tuple(pl.BlockSpec((TBP, w), lambda i: (i, 0)) for w, _ in INPROJ_OUT),
        out_shape=tuple(jax.ShapeDtypeStruct((N_TOK, w), dt) for w, dt in INPROJ_OUT),
        scratch_shapes=[pltpu.VMEM((D_MODEL, D_IN), BF16)],
        compiler_params=pltpu.CompilerParams(vmem_limit_bytes=VMEM_LIMIT),
        name="in_proj",
    )(*xs, mods, norm_g, w_in)


S5_NPAIR = S5_G // 2
S5_CW = S5_L * S5_P
S5_SW = S5_G * S5_N
S5_STEP_ROWS = N_CTX // S5_L
S5_CTX_NC = SEQ // S5_L
S5_LAT_NC = DEC_SEQ // S5_L


def _s5_slot(g, t):
    return (t + g) % S5_L


def _s5_tables(lam_re, lam_im, log_dt, b_re, b_im, c_re, c_im):
    lr = jnp.minimum(lam_re.astype(F32), LAM_RE_MAX)
    li = lam_im.astype(F32)
    dt = jnp.exp(log_dt.astype(F32))[..., None]
    mag = jnp.exp(lr * dt)
    ang = li * dt
    ab_re = mag * jnp.cos(ang)
    ab_im = mag * jnp.sin(ang)
    den = lr * lr + li * li
    xr = ab_re - 1.0
    z_re = (xr * lr + ab_im * li) / den
    z_im = (ab_im * lr - xr * li) / den
    bb_re = z_re[..., None] * b_re - z_im[..., None] * b_im
    bb_im = z_re[..., None] * b_im + z_im[..., None] * b_re
    tau = jnp.arange(S5_L + 1, dtype=F32)[:, None, None, None, None]
    pm = jnp.exp(lr * dt * tau)
    pa = li * dt * tau
    pw_re = pm * jnp.cos(pa)
    pw_im = pm * jnp.sin(pa)
    eye2 = jnp.eye(2, dtype=F32)

    def pw_pairs(a):
        a = a.transpose(1, 2, 3, 0, 4).reshape(DEPTH, 2, S5_NPAIR, 2, S5_L + 1, S5_N)
        return a.transpose(0, 1, 2, 4, 3, 5).reshape(DEPTH, 2, S5_NPAIR, S5_L + 1, 2 * S5_N)

    def mat_pairs(a):
        a = a.reshape(DEPTH, 2, S5_NPAIR, 2, S5_P, S5_N)
        return jnp.einsum('ldjaqn,ab->ldjaqbn', a, eye2).reshape(DEPTH, 2, S5_NPAIR, 2, S5_P, 2 * S5_N)

    al = jnp.stack([pw_re[S5_L], pw_im[S5_L]], axis=2).reshape(DEPTH, 2, 2, 1, S5_SW)
    return (pw_pairs(pw_re), pw_pairs(pw_im),
            mat_pairs(bb_re.transpose(0, 1, 2, 4, 3)), mat_pairs(bb_im.transpose(0, 1, 2, 4, 3)),
            mat_pairs(c_re.astype(F32)), mat_pairs(c_im.astype(F32)), al)


def _s5_shift_consts():
    n = S5_CW
    r = np.arange(n)[:, None]
    c = np.arange(n)[None, :]
    fwd = [(c == r + S5_P * s) for s in range(S5_L)]
    bwd = [(c == r - S5_P * (S5_L - 1 - s)) for s in range(S5_L)]
    return np.stack([np.concatenate(fwd, axis=1), np.concatenate(bwd, axis=1)]).astype(np.float32)


PREP_STEPS = 4
ADA_TN = N_MOD * D_MODEL // PREP_STEPS


def _prep_kernel(c_ref, wada_ref, bada_ref, pwr_ref, pwi_ref, br_ref, bi_ref, cr_ref, ci_ref, scat_ref,
                 mod_ref, m_ref, f_ref, e_ref, k_ref):
    cv = c_ref[...]
    mod_ref[0] = _dot((cv * jax.nn.sigmoid(cv)).astype(BF16), wada_ref[0].astype(BF16)) + bada_ref[0]
    step = pl.program_id(1)
    for qq in range(PREP_STEPS):
        pl.when(step == qq)(functools.partial(_s5_pair_ops, qq, pwr_ref, pwi_ref, br_ref, bi_ref, cr_ref,
                                              ci_ref, f_ref, e_ref, k_ref))
    pl.when(step == PREP_STEPS - 1)(functools.partial(_s5_toeplitz, scat_ref, m_ref, k_ref))


def _s5_pair_ops(qq, pwr_ref, pwi_ref, br_ref, bi_ref, cr_ref, ci_ref, f_ref, e_ref, k_ref):
    L = S5_L
    hp = lax.Precision.HIGHEST
    sw = 2 * S5_N
    per_step = S5_NPAIR // PREP_STEPS
    for j in range(per_step * qq, per_step * (qq + 1)):
        for d in range(2):
            pwr = pwr_ref[0, d, j]
            pwi = pwi_ref[0, d, j]
            for gl in range(2):
                g = 2 * j + gl
                br, bi = br_ref[0, d, j, gl], bi_ref[0, d, j, gl]
                cr, ci = cr_ref[0, d, j, gl], ci_ref[0, d, j, gl]
                ca_r, ca_i, f_r, f_i, e_r, e_i = [], [], [], [], [], []
                for t in range(L):
                    kk = t if d == 0 else L - 1 - t
                    pr, pi = pwr[kk:kk + 1], pwi[kk:kk + 1]
                    ca_r.append(cr * pr - ci * pi)
                    ca_i.append(cr * pi + ci * pr)
                    kf = L - 1 - t if d == 0 else t
                    pr, pi = pwr[kf:kf + 1], pwi[kf:kf + 1]
                    f_r.append(br * pr - bi * pi)
                    f_i.append(br * pi + bi * pr)
                    ke = t + 1 if d == 0 else L - t
                    pr, pi = pwr[ke:ke + 1], pwi[ke:ke + 1]
                    e_r.append(cr * pr - ci * pi)
                    e_i.append(-(cr * pi + ci * pr))
                cat = lambda xs: jnp.concatenate(xs, axis=0)
                k = _dot_nt(br, cat(ca_r), hp) - _dot_nt(bi, cat(ca_i), hp)
                k_ref[d, S5_P * g:S5_P * (g + 1), :] = k
                slots = lambda xs: cat([xs[(p - g) % L] for p in range(L)])
                rows = slice(S5_CW * gl, S5_CW * (gl + 1))
                f_ref[0, d, j, rows, 0:sw] = slots(f_r).astype(BF16)
                f_ref[0, d, j, rows, sw:2 * sw] = slots(f_i).astype(BF16)
                e_ref[0, d, 0, j, rows, :] = slots(e_r).astype(BF16)
                e_ref[0, d, 1, j, rows, :] = slots(e_i).astype(BF16)


def _s5_toeplitz(scat_ref, m_ref, k_ref):
    L = S5_L
    kf = k_ref[0].astype(BF16)
    kb = k_ref[1].astype(BF16)
    for s in range(L):
        cols = slice(S5_CW * s, S5_CW * (s + 1))
        res = _dot(kf, scat_ref[0, :, cols]) + _dot(kb, scat_ref[1, :, cols])
        for g in range(S5_G):
            blk = res[S5_P * g:S5_P * (g + 1), :]
            if g:
                blk = pltpu.roll(blk, S5_P * g, 1)
            r0 = S5_P * _s5_slot(g, s)
            m_ref[0, g, r0:r0 + S5_P, :] = blk.astype(BF16)


def _prep(cvecs, w_ada, b_ada, tables, scat):
    pwr, pwi, br, bi, cr, ci, _ = tables
    n = N_MOD * D_MODEL

    def lspec(shape):
        return pl.BlockSpec((1,) + shape, lambda l, q: (l,) + (0,) * len(shape))

    pw_shape = (2, S5_NPAIR, S5_L + 1, 2 * S5_N)
    mat_shape = (2, S5_NPAIR, 2, S5_P, 2 * S5_N)
    return pl.pallas_call(
        _prep_kernel,
        grid=(DEPTH, PREP_STEPS),
        in_specs=[pl.BlockSpec((MOD_ROWS, D_MODEL), lambda l, q: (0, 0)),
                  pl.BlockSpec((1, D_MODEL, ADA_TN), lambda l, q: (l, 0, q)),
                  pl.BlockSpec((1, 1, ADA_TN), lambda l, q: (l, 0, q)),
                  lspec(pw_shape), lspec(pw_shape), lspec(mat_shape), lspec(mat_shape), lspec(mat_shape),
                  lspec(mat_shape), _const_spec(scat.shape)],
        out_specs=(pl.BlockSpec((1, MOD_ROWS, ADA_TN), lambda l, q: (l, 0, q)),
                   lspec((S5_G, S5_CW, S5_CW)), lspec((2, S5_NPAIR, 2 * S5_CW, 4 * S5_N)),
                   lspec((2, 2, S5_NPAIR, 2 * S5_CW, 2 * S5_N))),
        out_shape=(jax.ShapeDtypeStruct((DEPTH, MOD_ROWS, n), F32),
                   jax.ShapeDtypeStruct((DEPTH, S5_G, S5_CW, S5_CW), BF16),
                   jax.ShapeDtypeStruct((DEPTH, 2, S5_NPAIR, 2 * S5_CW, 4 * S5_N), BF16),
                   jax.ShapeDtypeStruct((DEPTH, 2, 2, S5_NPAIR, 2 * S5_CW, 2 * S5_N), BF16)),
        scratch_shapes=[pltpu.VMEM((2, S5_G * S5_P, S5_CW), F32)],
        compiler_params=pltpu.CompilerParams(vmem_limit_bytes=VMEM_LIMIT),
        name="param_prep",
    )(cvecs, w_ada, b_ada.reshape(DEPTH, 1, n), pwr, pwi, br, bi, cr, ci, scat)


def _s5_kernel(xlo_ref, xhi_ref, m_ref, f_ref, e_ref, al_ref, h0_ref, ylo_ref, yhi_ref, hfin_ref,
               uy_ref, zh_ref):
    R = S5_STEP_ROWS
    pw = 2 * S5_CW
    sw = 2 * S5_N
    gph = S5_G // 2
    step = pl.program_id(0)

    for half, x_ref in enumerate((xlo_ref, xhi_ref)):
        for t in range(S5_L):
            xt = x_ref[pl.ds(t, R, stride=S5_L), :]
            shift = (S5_P * t) % LANES
            xr = pltpu.roll(xt, shift, 1) if shift else xt
            for gl in range(gph):
                g = gph * half + gl
                src = (S5_P * gl + shift) % LANES
                lo = S5_CW * g + S5_P * _s5_slot(g, t)
                assert lo % LANES == src
                uy_ref[:, lo:lo + S5_P] = xr[:, src:src + S5_P]

    for j in range(S5_NPAIR):
        ub = uy_ref[:, pw * j:pw * (j + 1)].astype(BF16)
        for d in range(2):
            o = _dot(ub, f_ref[d, j])
            zh_ref[d, 0, :, sw * j:sw * (j + 1)] = o[:, :sw]
            zh_ref[d, 1, :, sw * j:sw * (j + 1)] = o[:, sw:]

    def scan_group(seq_rows, nc, init, fin_rows):
        chains = [(d, r0) for d in range(2) for r0 in seq_rows]

        def body(c, carry):
            out = []
            for k, (d, r0) in enumerate(chains):
                hr, hi = carry[2 * k], carry[2 * k + 1]
                r = r0 + (c if d == 0 else nc - 1 - c)
                zr = zh_ref[d, 0, pl.ds(r, 1), :]
                zi = zh_ref[d, 1, pl.ds(r, 1), :]
                zh_ref[d, 0, pl.ds(r, 1), :] = hr
                zh_ref[d, 1, pl.ds(r, 1), :] = hi
                ar = al_ref[d, 0]
                ai = al_ref[d, 1]
                out.append(ar * hr - ai * hi + zr)
                out.append(ar * hi + ai * hr + zi)
            return tuple(out)

        fin = lax.fori_loop(0, nc, body, tuple(init))
        if fin_rows is not None:
            for k, (d, _) in enumerate(chains):
                s = fin_rows[k % len(seq_rows)]
                hfin_ref[d, 0, s:s + 1, :] = fin[2 * k]
                hfin_ref[d, 1, s:s + 1, :] = fin[2 * k + 1]

    @pl.when(step == 0)
    def _():
        zero = jnp.zeros((1, S5_SW), F32)
        for s0 in range(0, BATCH, 4):
            seqs = list(range(s0, s0 + 4))
            scan_group([s * S5_CTX_NC for s in seqs], S5_CTX_NC, [zero] * 16, seqs)

    @pl.when(step == 1)
    def _():
        init = []
        for d in range(2):
            for s in range(DEC_BATCH):
                init.append(h0_ref[d, 0, s:s + 1, :])
                init.append(h0_ref[d, 1, s:s + 1, :])
        scan_group([s * S5_LAT_NC for s in range(DEC_BATCH)], S5_LAT_NC, init, None)

    for j in range(S5_NPAIR):
        acc = None
        for d in range(2):
            for ri in range(2):
                hb = zh_ref[d, ri, :, sw * j:sw * (j + 1)].astype(BF16)
                t = _dot_nt(hb, e_ref[d, ri, j])
                acc = t if acc is None else acc + t
        for gl in range(2):
            g = 2 * j + gl
            ug = uy_ref[:, S5_CW * g:S5_CW * (g + 1)].astype(BF16)
            uy_ref[:, S5_CW * g:S5_CW * (g + 1)] = acc[:, S5_CW * gl:S5_CW * (gl + 1)] + _dot(ug, m_ref[g])

    lane = lax.broadcasted_iota(jnp.int32, (R, LANES), 1)
    in_piece = [jnp.logical_and(lane >= S5_P * k, lane < S5_P * (k + 1)) for k in range(LANES // S5_P)]
    for half, y_ref in enumerate((ylo_ref, yhi_ref)):
        for t in range(S5_L):
            merged = None
            for gl in range(gph):
                g = gph * half + gl
                slot = _s5_slot(g, t)
                col = S5_CW * g + LANES * (S5_P * slot // LANES)
                v = uy_ref[:, col:col + LANES]
                merged = v if merged is None else jnp.where(in_piece[slot % (LANES // S5_P)], v, merged)
            shift = (LANES - S5_P * t % LANES) % LANES
            y_ref[pl.ds(t, R, stride=S5_L), :] = pltpu.roll(merged, shift, 1) if shift else merged


def _s5(xa_lo, xa_hi, m, f2, e2, al, h0, l):
    half = pl.BlockSpec((N_CTX, LANES), lambda s: (s, 0))
    return pl.pallas_call(
        _s5_kernel,
        grid=(2,),
        in_specs=[half, half,
                  _layer_spec(m.shape[1:], l, single=True), _layer_spec(f2.shape[1:], l, single=True),
                  _layer_spec(e2.shape[1:], l, single=True), _layer_spec(al.shape[1:], l),
                  _layer_spec(h0.shape[1:], l)],
        out_specs=(half, half, _const_spec((2, 2, BATCH, S5_SW))),
        out_shape=(jax.ShapeDtypeStruct((N_TOK, LANES), F32),
                   jax.ShapeDtypeStruct((N_TOK, LANES), F32),
                   jax.ShapeDtypeStruct((2, 2, BATCH, S5_SW), F32)),
        scratch_shapes=[pltpu.VMEM((S5_STEP_ROWS, S5_G * S5_CW), F32),
                        pltpu.VMEM((2, 2, S5_STEP_ROWS, S5_SW), F32)],
        compiler_params=pltpu.CompilerParams(vmem_limit_bytes=VMEM_LIMIT),
        name="s5_mixer",
    )(xa_lo, xa_hi, m, f2, e2, al, h0)


HG_NLEV = int(math.log2(HG_L))
HG_W = HG_HEADS * HG_DK
HG_CPS = 8
HG_BLK = HG_CPS * HG_L
HG_NB_CTX = N_CTX // HG_BLK
HG_SPB = HG_BLK // SEQ
HG_CPQ = SEQ // HG_L
HG_NC_LAT = DEC_SEQ // HG_BLK
assert HG_SPB >= 1 and HG_SPB * SEQ == HG_BLK
assert HG_L == HG_DK


def _hgrn_consts():
    L = HG_L
    w = np.zeros((HG_NLEV + 2, L, L), np.float32)
    mask = np.zeros((HG_NLEV + 1, L, L), np.float32)
    for lev in range(HG_NLEV):
        blk = L >> lev
        half = blk // 2
        for t in range(L):
            p, o = divmod(t, blk)
            bd = p * blk + half - 1
            if o >= half:
                w[lev, t, bd + 1:t + 1] = 1.0
            else:
                w[lev, t, t + 1:bd + 1] = 1.0
        jj, ii = np.meshgrid(np.arange(L), np.arange(L), indexing='ij')
        mask[lev] = ((jj // blk == ii // blk) & (jj % blk >= half) & (ii % blk < half)).astype(np.float32)
    mask[HG_NLEV] = np.eye(L, dtype=np.float32)
    for t in range(L):
        w[HG_NLEV, t, :t + 1] = 1.0
        w[HG_NLEV + 1, t, t + 1:] = 1.0
    out = []
    for wd, md in ((w, mask), (w[:, ::-1, ::-1], mask[:, ::-1, ::-1])):
        wflat = wd.reshape((HG_NLEV + 2) * L, L)
        out.append((np.concatenate([wflat] * 3, axis=1), np.tile(md, (1, 1, HG_HEADS))))
    wcat = np.stack([out[0][0], out[1][0]])
    mask4 = np.stack([out[0][1], out[1][1]])
    hm = np.kron(np.eye(HG_HEADS, dtype=np.float32), np.ones((HG_DK, HG_DK), np.float32))
    return wcat, mask4, hm


def _hg_pos(i):
    is_ctx = i < HG_NB_CTX
    c = jnp.where(is_ctx, 0, (i - HG_NB_CTX) % HG_NC_LAT)
    nc = jnp.where(is_ctx, 1, HG_NC_LAT)
    return is_ctx, c, nc


def _hg_bwd_blk(i):
    _, c, nc = _hg_pos(i)
    return i + nc - 1 - 2 * c


def _hg_local(items, lb_ref, wcat_ref, mask_ref, hm):
    L = HG_L
    hmb = hm.astype(BF16)

    def bd4(x):
        return jnp.concatenate([x] * HG_HEADS, axis=0) * hmb

    kks, exs = [], []
    for d, q, z, v in items:
        lbp = lb_ref[d]
        sp = jnp.log1p(jnp.exp(-jnp.abs(z)))
        ls = jnp.minimum(z, 0.0) - sp
        kks.append(lbp[2:3] * jnp.exp(jnp.minimum(-z, 0.0) - sp))
        a = lbp[0:1]
        b = lbp[1:2] + ls
        logf = jnp.maximum(a, b) + jnp.log1p(jnp.exp(-jnp.abs(a - b)))
        p0 = logf.astype(BF16)
        r0 = logf - p0.astype(F32)
        p1 = r0.astype(BF16)
        p2 = (r0 - p1.astype(F32)).astype(BF16)
        exs.append(jnp.exp(_dot(wcat_ref[d], jnp.concatenate([p0, p1, p2], axis=0))))
    scs = [None] * len(items)
    for lev in range(HG_NLEV + 1):
        for n, (d, q, z, v) in enumerate(items):
            if lev < HG_NLEV:
                al = exs[n][L * lev:L * (lev + 1)]
                lhs = (q * al).astype(BF16)
                rhs = (kks[n] * al).astype(BF16)
            else:
                lhs = q.astype(BF16)
                rhs = kks[n].astype(BF16)
            t = _dot_nt(lhs, bd4(rhs)) * mask_ref[d, lev]
            scs[n] = t if scs[n] is None else scs[n] + t
    out = []
    for n, (d, q, z, v) in enumerate(items):
        vb = v.astype(BF16)
        o = _dot(scs[n].astype(BF16), bd4(vb))
        eq = exs[n][L * HG_NLEV:L * (HG_NLEV + 1)]
        qe = (q * eq).astype(BF16)
        ke = (kks[n] * exs[n][L * (HG_NLEV + 1):L * (HG_NLEV + 2)]).astype(BF16)
        g = eq[L - 1:L] if d == 0 else eq[0:1]
        out.append((o, qe, g, _dot_tn(vb, ke) * hm))
    return out


def _hgrn_kernel(zf_ref, qvf_ref, zb_ref, qvb_ref, lb_ref, w_ref, mask_ref, hm_ref, s0_ref, wup_ref, wdn_ref,
                 of_ref, ob_ref, sfin_ref, wupb_ref, wdnb_ref, st_ref):
    is_ctx, c, nc = _hg_pos(pl.program_id(0))
    wupb_ref[...] = wup_ref[...].astype(BF16)
    wdnb_ref[...] = wdn_ref[...].astype(BF16)

    hm = hm_ref[...]
    rows = [slice(HG_L * k, HG_L * (k + 1)) for k in range(HG_CPS)]
    items = [(d, qv_ref[r, 0:GROUP_W], z_ref[r, GROUP_W * d:GROUP_W * (d + 1)], qv_ref[r, GROUP_W:2 * GROUP_W])
             for d, (z_ref, qv_ref) in enumerate(((zf_ref, qvf_ref), (zb_ref, qvb_ref))) for r in rows]
    loc = _hg_local(items, lb_ref, w_ref, mask_ref, hm)

    def chain(d, o_ref, st, chunks):
        for k in (chunks if d == 0 else reversed(chunks)):
            o, qe, g, upd = loc[HG_CPS * d + k]
            o_ref[rows[k], :] = o + _dot_nt(qe, st.astype(BF16))
            st = st * g + upd
        return st

    @pl.when(is_ctx)
    def _():
        for sq in range(HG_SPB):
            for d, o_ref in enumerate((of_ref, ob_ref)):
                st = chain(d, o_ref, jnp.zeros((HG_W, HG_W), F32), range(HG_CPQ * sq, HG_CPQ * (sq + 1)))
                s_kv = st.T
                for h in range(HG_HEADS):
                    sfin_ref[sq, d, h] = s_kv[HG_DK * h:HG_DK * (h + 1), HG_DK * h:HG_DK * (h + 1)]

    @pl.when(jnp.logical_not(is_ctx))
    def _():
        @pl.when(c == 0)
        def _():
            st_ref[...] = s0_ref[0]

        for d, o_ref in enumerate((of_ref, ob_ref)):
            st_ref[d] = chain(d, o_ref, st_ref[d], range(HG_CPS))


def _hgrn(hz, hqv, lbp, consts, s0, w_up, w_down, l):
    wcat, mask4, hm = consts
    steps = N_TOK // HG_BLK
    up_rows = D_MODEL // steps
    dn_rows = 2 * D_FF // steps

    return pl.pallas_call(
        _hgrn_kernel,
        grid=(N_TOK // HG_BLK,),
        in_specs=[pl.BlockSpec((HG_BLK, HG_Z_W), lambda i: (i, 0)),
                  pl.BlockSpec((HG_BLK, HG_QV_W), lambda i: (i, 0)),
                  pl.BlockSpec((HG_BLK, HG_Z_W), lambda i: (_hg_bwd_blk(i), 0)),
                  pl.BlockSpec((HG_BLK, HG_QV_W), lambda i: (_hg_bwd_blk(i), 0)),
                  _layer_spec(lbp.shape[1:], l), _const_spec(wcat.shape), _const_spec(mask4.shape),
                  _const_spec(hm.shape),
                  pl.BlockSpec((None, 1, 2, HG_W, HG_W),
                               lambda i: (l, jnp.maximum(i - HG_NB_CTX, 0) // HG_NC_LAT, 0, 0, 0)),
                  pl.BlockSpec((None, up_rows, 2 * D_FF), lambda i: (l, i, 0)),
                  pl.BlockSpec((None, dn_rows, D_MODEL), lambda i: (l, i // 2, 0))],
        out_specs=(pl.BlockSpec((HG_BLK, HG_W), lambda i: (i, 0)),
                   pl.BlockSpec((HG_BLK, HG_W), lambda i: (_hg_bwd_blk(i), 0)),
                   pl.BlockSpec((HG_SPB, 2, HG_HEADS, HG_DK, HG_DK),
                                lambda i: (jnp.minimum(i, HG_NB_CTX - 1), 0, 0, 0, 0)),
                   pl.BlockSpec((up_rows, 2 * D_FF), lambda i: (i, 0)),
                   pl.BlockSpec((dn_rows, D_MODEL), lambda i: (i // 2, 0))),
        out_shape=(jax.ShapeDtypeStruct((N_TOK, HG_W), F32),
                   jax.ShapeDtypeStruct((N_TOK, HG_W), F32),
                   jax.ShapeDtypeStruct((BATCH, 2, HG_HEADS, HG_DK, HG_DK), F32),
                   jax.ShapeDtypeStruct((D_MODEL, 2 * D_FF), BF16),
                   jax.ShapeDtypeStruct((D_FF, D_MODEL), BF16)),
        scratch_shapes=[pltpu.VMEM((2, HG_W, HG_W), F32)],
        compiler_params=pltpu.CompilerParams(vmem_limit_bytes=VMEM_LIMIT),
        name="hgrn_mixer",
    )(hz, hqv, hz, hqv, lbp, wcat, mask4, hm, s0, w_up, w_down)


FN_CTX_SPG = 4
FN_LAT_TR = 512


def _dft_consts(t_len):
    n = GROUP_W // 4
    k = np.arange(n)
    ang = 2.0 * np.pi * ((k[:, None] * k[None, :]) % n) / n
    eye = np.eye(4)
    cs = np.concatenate([np.kron(eye, np.cos(ang)), np.kron(eye, np.sin(ang))], axis=1) / math.sqrt(n)
    t = np.arange(t_len)
    angt = 2.0 * np.pi * ((t[:, None] * t[None, :]) % t_len) / t_len
    dft = np.concatenate([np.cos(angt), -np.sin(angt)], axis=1) / math.sqrt(t_len)
    return cs.astype(np.float32), dft.astype(np.float32)


def _fnet_kernel(x_ref, cs_ref, dft_ref, w_ref, o_ref, r_ref, *, t_len, spg, tr):
    @pl.when(pl.program_id(1) == 0)
    def _():
        t = _dot(x_ref[...].astype(BF16), cs_ref[...])
        for s in range(spg):
            rows = slice(t_len * s, t_len * (s + 1))
            r_ref[s, 0:t_len, :] = t[rows, :GROUP_W].astype(BF16)
            r_ref[s, t_len:2 * t_len, :] = t[rows, GROUP_W:].astype(BF16)

    w = w_ref[...].astype(BF16)
    for s in range(spg):
        y = _dot(dft_ref[...], r_ref[s])
        o_ref[tr * s:tr * (s + 1), :] = _dot(y.astype(BF16), w)


def _fnet(xc, cs, dft, fn_w, l, *, n_seq, seq_len, tok0, spg, tr):
    nj = seq_len // tr
    assert spg == 1 or nj == 1
    gb0 = tok0 // (spg * seq_len)
    return pl.pallas_call(
        functools.partial(_fnet_kernel, t_len=seq_len, spg=spg, tr=tr),
        grid=(n_seq // spg, nj),
        in_specs=[
            pl.BlockSpec((spg * seq_len, GROUP_W), lambda g, j: (gb0 + g, 0)),
            _const_spec((GROUP_W, 2 * GROUP_W)),
            pl.BlockSpec((tr, 2 * seq_len), lambda g, j: (j, 0)),
            _layer_spec((GROUP_W, GROUP_W), l),
        ],
        out_specs=pl.BlockSpec((spg * tr, GROUP_W), lambda g, j: (g * nj + j, 0)),
        out_shape=jax.ShapeDtypeStruct((n_seq * seq_len, GROUP_W), F32),
        scratch_shapes=[pltpu.VMEM((spg, 2 * seq_len, GROUP_W), BF16)],
        compiler_params=pltpu.CompilerParams(vmem_limit_bytes=VMEM_LIMIT),
        name="fnet_mixer",
    )(xc, cs, dft, fn_w)


OUTPROJ_SUB = 256


def _outproj_kernel(*refs):
    (op_ref, ylo_ref, yhi_ref, of_ref, ob_ref, ycc_ref, ycl_ref, mod_ref, d_ref,
     wglu_ref, gn_ref, bdm_ref, gmn_ref, gmw_ref, gmb_ref, gmh_ref, wout_ref, n2_ref,
     x1_ref, h2_ref, wob_ref) = refs[-21:]
    x_refs = refs[:-21]
    xa_ref, hgate_ref, gu_ref, gv_ref = (op_ref.at[:, GROUP_W * k:GROUP_W * (k + 1)] for k in range(4))

    @pl.when(pl.program_id(0) == 0)
    def _():
        wob_ref[...] = wout_ref[...].astype(BF16)

    gn = gn_ref[...]
    gm_hm = gmh_ref[...]
    mod = mod_ref[...]
    g1 = mod[:, 2 * D_MODEL:3 * D_MODEL]
    sh2 = mod[:, 3 * D_MODEL:4 * D_MODEL]
    sc2 = mod[:, 4 * D_MODEL:5 * D_MODEL]
    is_ctx = pl.program_id(0) < N_CTX // TBP
    for r in range(0, TBP, OUTPROJ_SUB):
        rows = slice(r, r + OUTPROJ_SUB)
        ys = jnp.concatenate([ylo_ref[rows, :], yhi_ref[rows, :]], axis=1)
        y5 = _gelu(ys + d_ref[...] * xa_ref[rows, :].astype(F32))
        glu = jax.nn.sigmoid(_dot(y5.astype(BF16), wglu_ref[...].astype(BF16)))
        out_a = _rms(y5 * glu, gn[:, 0:GROUP_W])
        o = of_ref[rows, :] + ob_ref[rows, :]
        o2 = o * o
        o2h = o2.astype(BF16)
        o2l = (o2 - o2h.astype(F32)).astype(BF16)
        ms = _dot(o2h, bdm_ref[...]) + _dot(o2l, bdm_ref[...])
        hg = hgate_ref[rows, :].astype(F32)
        out_b = o * lax.rsqrt(ms + EPS) * gn[:, GROUP_W:2 * GROUP_W] * (hg * jax.nn.sigmoid(hg))
        out_c = _rms(_pick((ycc_ref, ycl_ref), rows, is_ctx), gn[:, 2 * GROUP_W:3 * GROUP_W])
        out_d = []
        for c in range(r // GM_CHUNK, (r + OUTPROJ_SUB) // GM_CHUNK):
            crow = slice(GM_CHUNK * c, GM_CHUNK * (c + 1))
            gv = _rms(_gelu(gv_ref[crow, :].astype(F32)), gmn_ref[...]).astype(BF16)
            g4 = jnp.concatenate([gv] * GM_HEADS, axis=0) * gm_hm
            sp = _dot(gmw_ref[...], g4) + gmb_ref[...]
            out_d.append(_rms(_gelu(gu_ref[crow, :].astype(F32)) * sp, gn[:, 3 * GROUP_W:]))
        out_d = jnp.concatenate(out_d, axis=0)
        m = None
        for k, part in enumerate((out_a, out_b, out_c, out_d)):
            t = _dot(part.astype(BF16), wob_ref[GROUP_W * k:GROUP_W * (k + 1), :])
            m = t if m is None else m + t
        x1 = _pick(x_refs, rows, is_ctx) + g1 * m
        x1_ref[rows, :] = x1
        h2_ref[rows, :] = (_rms(x1, n2_ref[...]) * (1.0 + sc2) + sh2).astype(BF16)


def _outproj(xs, op_in, ylo, yhi, of, ob, yc, mods, s5_d, wglu, gn, bdm, gm_norm_g, gm_w, gm_b, gm_hm, wout,
             norm2_g, l):
    def tok(width, col=0):
        return pl.BlockSpec((TBP, width), lambda i: (i, col))

    return pl.pallas_call(
        _outproj_kernel,
        grid=(N_TOK // TBP,),
        in_specs=_x_specs(len(xs) == 2, tb=TBP) + [
            tok(OP_IN_W), tok(LANES), tok(LANES), tok(GROUP_W), tok(GROUP_W), *_x_specs(True, GROUP_W, TBP),
            _mod_spec(l, TBP),
            _layer_spec((1, GROUP_W), l), _layer_spec((GROUP_W, GROUP_W), l), _layer_spec((1, D_MODEL), l),
            _const_spec((GROUP_W, GROUP_W)),
            _layer_spec((1, GROUP_W), l), _layer_spec((GM_CHUNK, GM_HEADS * GM_CHUNK), l),
            _layer_spec((GM_CHUNK, GROUP_W), l), _const_spec((GM_HEADS * GM_CHUNK, GROUP_W)),
            _layer_spec((D_MODEL, D_MODEL), l, single=True), _layer_spec((1, D_MODEL), l),
        ],
        out_specs=(tok(D_MODEL), tok(D_MODEL)),
        out_shape=(jax.ShapeDtypeStruct((N_TOK, D_MODEL), F32),
                   jax.ShapeDtypeStruct((N_TOK, D_MODEL), BF16)),
        scratch_shapes=[pltpu.VMEM((D_MODEL, D_MODEL), BF16)],
        compiler_params=pltpu.CompilerParams(vmem_limit_bytes=VMEM_LIMIT),
        name="out_proj",
    )(*xs, op_in, ylo, yhi, of, ob, *yc, mods, s5_d, wglu, gn, bdm, gm_norm_g, gm_w, gm_b,
      gm_hm, wout, norm2_g)


FF_SEG = GRID_W
FF_NSEG = TB // FF_SEG
FF_SEQ_STRIPS = SEQ // FF_SEG
FF_PERM_ROWS = FF_SEQ_STRIPS * FF_SEG
SUBLANES = 8
FF_NGRP = FF_SEG // SUBLANES


def _ffn_perm_consts():
    p = np.zeros((FF_PERM_ROWS, FF_PERM_ROWS), np.float32)
    for k in range(FF_SEQ_STRIPS):
        for t in range(FF_SEG):
            p[FF_SEG * k + SUBLANES * (t % FF_NGRP) + t // FF_NGRP, FF_SEG * k + t] = 1.0
    return p


def _ffn_kernel(x1_ref, h2_ref, mod_ref, perm_ref, wup_ref, cw_ref, cb_ref, wdn_ref, fg_ref, *rest, final):
    *o_refs, h2p_ref, z_ref, hid_ref = rest
    i = pl.program_id(0)
    joined = (i < NB_CTX).astype(F32)
    n_tiles = D_FF // FF_TILE
    sub = lax.broadcasted_iota(jnp.int32, (SUBLANES, 2 * FF_TILE), 0)
    is_first = sub == 0
    is_last = sub == SUBLANES - 1
    zero_grp = jnp.zeros((SUBLANES, 2 * FF_TILE), F32)
    for r in range(0, TB, FF_PERM_ROWS):
        h2p_ref[r:r + FF_PERM_ROWS, :] = _dot(perm_ref[...], h2_ref[r:r + FF_PERM_ROWS, :]).astype(BF16)

    def up(j):
        lo = FF_TILE * j
        for c, col in enumerate((lo, D_FF + lo)):
            z_ref[j % 2, :, FF_TILE * c:FF_TILE * (c + 1)] = _dot(h2p_ref[...], wup_ref[:, col:col + FF_TILE])

    def gate(j):
        slot = j % 2
        lo = FF_TILE * j
        w = jnp.concatenate([cw_ref[:, lo:lo + FF_TILE], cw_ref[:, D_FF + lo:D_FF + lo + FF_TILE]], axis=1)
        b = jnp.concatenate([cb_ref[:, lo:lo + FF_TILE], cb_ref[:, D_FF + lo:D_FF + lo + FF_TILE]], axis=1)

        def grp(k, v):
            r = FF_SEG * k + SUBLANES * v
            return z_ref[slot, r:r + SUBLANES, :]

        down = [pltpu.roll(grp(k, FF_NGRP - 1), 1, 0) for k in range(FF_NSEG)]
        up_ = [pltpu.roll(grp(k, 0), SUBLANES - 1, 0) for k in range(FF_NSEG)]
        strips = []
        for k in range(FF_NSEG):
            g = [grp(k, v) for v in range(FF_NGRP)]
            before = down[k - 1] * joined if k % FF_SEQ_STRIPS > 0 else zero_grp
            after = up_[k + 1] * joined if k % FF_SEQ_STRIPS < FF_SEQ_STRIPS - 1 else zero_grp
            zm1 = jnp.concatenate([jnp.where(is_first, before, down[k])] + g[:-1], axis=0)
            zp1 = jnp.concatenate(g[1:] + [jnp.where(is_last, after, up_[k])], axis=0)
            zc = b + zm1 * w[0:1] + jnp.concatenate(g, axis=0) * w[1:2] + zp1 * w[2:3]
            strips.append((_gelu(zc[:, :FF_TILE]) * zc[:, FF_TILE:]).astype(BF16))
        hid_ref[:, lo:lo + FF_TILE] = jnp.concatenate(strips, axis=0)

    up(0)
    for j in range(n_tiles):
        if j + 1 < n_tiles:
            up(j + 1)
        gate(j)
    acc = _dot(hid_ref[...], wdn_ref[...])
    acc = jnp.swapaxes(acc.reshape(FF_NSEG, FF_NGRP, SUBLANES, D_MODEL), 1, 2).reshape(TB, D_MODEL)
    g2 = mod_ref[...][:, 5 * D_MODEL:6 * D_MODEL]
    x2 = x1_ref[...] + g2 * acc
    if final:
        y = _rms(x2, fg_ref[...])
        yp_ref, ys_ref = o_refs

        @pl.when(i < NB_CTX)
        def _():
            yp_ref[...] = y

        @pl.when(i >= NB_CTX)
        def _():
            ys_ref[...] = y
    else:
        o_refs[0][...] = x2


def _ffn(x1, h2, mods, perm, wup_bf, conv_w, conv_b, wdn_bf, final_g, l, *, final):
    if final:
        out_specs = tuple(_x_specs(True))
        out_shape = (jax.ShapeDtypeStruct((N_CTX, D_MODEL), F32), jax.ShapeDtypeStruct((N_LAT, D_MODEL), F32))
    else:
        out_specs = pl.BlockSpec((TB, D_MODEL), lambda i: (i, 0))
        out_shape = jax.ShapeDtypeStruct((N_TOK, D_MODEL), F32)
    return pl.pallas_call(
        functools.partial(_ffn_kernel, final=final),
        grid=(N_TOK // TB,),
        in_specs=[
            pl.BlockSpec((TB, D_MODEL), lambda i: (i, 0)),
            pl.BlockSpec((TB, D_MODEL), lambda i: (i, 0)),
            _mod_spec(l), _const_spec((FF_PERM_ROWS, FF_PERM_ROWS)),
            _const_spec((D_MODEL, 2 * D_FF), single=True), _layer_spec((3, 2 * D_FF), l),
            _layer_spec((1, 2 * D_FF), l), _const_spec((D_FF, D_MODEL), single=True),
            _const_spec((1, D_MODEL)),
        ],
        out_specs=out_specs,
        out_shape=out_shape,
        scratch_shapes=[pltpu.VMEM((TB, D_MODEL), BF16), pltpu.VMEM((2, TB, 2 * FF_TILE), F32),
                        pltpu.VMEM((TB, D_FF), BF16)],
        compiler_params=pltpu.CompilerParams(vmem_limit_bytes=VMEM_LIMIT),
        name="conv_ffn",
    )(x1, h2, mods, perm, wup_bf, conv_w, conv_b, wdn_bf, final_g.reshape(1, D_MODEL))


def kernel(x_prompt, x_sample, state_s5_re, state_s5_im, state_hgrn, c, c_ctx, w_ada, b_ada, norm1_g,
           norm2_g, w_in, s5_lam_re, s5_lam_im, s5_log_dt, s5_b_re, s5_b_im, s5_c_re, s5_c_im, s5_d,
           s5_w_glu, hg_lb_logits, fn_w, gm_norm_g, gm_ws, gm_bs, grp_norm_g, w_out, ffn_w_up,
           ffn_conv_w, ffn_conv_b, ffn_w_down, final_norm_g):
    xs = (x_prompt.reshape(N_CTX, D_MODEL), x_sample.reshape(N_LAT, D_MODEL))
    cvecs = jnp.concatenate([c_ctx[None], c, jnp.zeros((MOD_ROWS - 1 - DEC_BATCH, D_MODEL), F32)], axis=0)

    lb_p = jax.nn.softmax(hg_lb_logits.astype(F32), axis=0)
    lbs = jnp.maximum(jnp.cumsum(lb_p, axis=0) - lb_p[0], 0.0)
    lbps = jnp.stack([jnp.log(lbs), jnp.log1p(-lbs), 1.0 - lbs], axis=2)

    hg_consts_np = _hgrn_consts()
    hg_consts = (jnp.asarray(hg_consts_np[0], BF16), jnp.asarray(hg_consts_np[1], F32),
                 jnp.asarray(hg_consts_np[2], F32))
    cs_np, dft_ctx_np = _dft_consts(SEQ)
    _, dft_lat_np = _dft_consts(DEC_SEQ)
    cs = jnp.asarray(cs_np, F32).astype(BF16)
    dft_ctx = jnp.asarray(dft_ctx_np, F32).astype(BF16)
    dft_lat = jnp.asarray(dft_lat_np, F32).astype(BF16)
    gm_hm = jnp.asarray(np.kron(np.eye(GM_HEADS), np.ones((GM_CHUNK, GROUP_W // GM_HEADS))), BF16)
    bdm = jnp.asarray(np.kron(np.eye(HG_HEADS), np.ones((HG_DK, HG_DK))) / HG_DK, BF16)

    tables = _s5_tables(s5_lam_re, s5_lam_im, s5_log_dt, s5_b_re, s5_b_im, s5_c_re, s5_c_im)
    mods, s5_m, s5_f, s5_e = _prep(cvecs, w_ada, b_ada, tables, jnp.asarray(_s5_shift_consts(), BF16))
    mods = mods.reshape(DEPTH, MOD_ROWS, 1, N_MOD * D_MODEL)
    s5_al = tables[-1]
    s5_h0 = jnp.stack([state_s5_re, state_s5_im]).astype(F32).transpose(2, 3, 0, 1, 4, 5)
    s5_h0 = s5_h0.reshape(DEPTH, 2, 2, DEC_BATCH, S5_SW)
    hg_s0 = jnp.einsum('bldhkv,hg->lbdhvgk', state_hgrn.astype(F32), jnp.eye(HG_HEADS, dtype=F32))
    hg_s0 = hg_s0.reshape(DEPTH, DEC_BATCH, 2, HG_W, HG_W)

    def rows(a):
        return a.reshape(DEPTH, 1, a.shape[-1])

    norm1_r, norm2_r, gn_r, s5_d_r, gmn_r, cb_r = (rows(a) for a in (
        norm1_g, norm2_g, grp_norm_g, s5_d, gm_norm_g, ffn_conv_b))
    gm_w = gm_ws.transpose(0, 2, 1, 3).reshape(DEPTH, GM_CHUNK, GM_HEADS * GM_CHUNK).astype(BF16)
    gm_b = jnp.repeat(gm_bs.transpose(0, 2, 1), GROUP_W // GM_HEADS, axis=2)
    ff_perm = jnp.asarray(_ffn_perm_consts(), BF16)

    new_re, new_im, new_hg = [], [], []
    for l in range(DEPTH):
        hz, hqv, op_in, xc, xa_lo, xa_hi = _inproj(xs, mods, norm1_r, w_in, l)

        ylo, yhi, hfin = _s5(xa_lo, xa_hi, s5_m, s5_f, s5_e, s5_al, s5_h0, l)
        hfin = hfin.reshape(2, 2, BATCH, S5_G, S5_N).transpose(1, 2, 0, 3, 4)
        new_re.append(hfin[0])
        new_im.append(hfin[1])

        of, ob, sfin, wup_bf, wdn_bf = _hgrn(hz, hqv, lbps, hg_consts, hg_s0, ffn_w_up, ffn_w_down, l)
        new_hg.append(sfin)

        yc = (_fnet(xc, cs, dft_ctx, fn_w, l, n_seq=BATCH, seq_len=SEQ, tok0=0, spg=FN_CTX_SPG, tr=SEQ),
              _fnet(xc, cs, dft_lat, fn_w, l, n_seq=DEC_BATCH, seq_len=DEC_SEQ, tok0=N_CTX, spg=1,
                    tr=FN_LAT_TR))

        x1, h2 = _outproj(xs, op_in, ylo, yhi, of, ob, yc, mods, s5_d_r, s5_w_glu, gn_r, bdm, gmn_r, gm_w, gm_b,
                          gm_hm, w_out, norm2_r, l)
        res = _ffn(x1, h2, mods, ff_perm, wup_bf, ffn_conv_w, cb_r, wdn_bf, final_norm_g, l,
                   final=(l == DEPTH - 1))
        xs = (res,)

    y_prompt = res[0].reshape(BATCH, SEQ, D_MODEL)
    y_sample = res[1].reshape(DEC_BATCH, DEC_SEQ, D_MODEL)
    return (y_prompt, y_sample, jnp.stack(new_re, axis=1), jnp.stack(new_im, axis=1),
            jnp.stack(new_hg, axis=1))
```

```python
import functools
import math

import numpy as np
import jax
import jax.numpy as jnp
from jax import lax
from jax.experimental import pallas as pl
from jax.experimental.pallas import tpu as pltpu

D_MODEL = 1024
BATCH = 16
SEQ = 256
DEPTH = 2
DEC_BATCH = 2
DEC_SEQ = 2048
GRID_W = 64
GROUP_W = 256
S5_P = 16
S5_G = 16
S5_N = 64
HG_HEADS = 4
HG_DK = 64
GM_HEADS = 4
GM_CHUNK = 128
D_FF = 2816
N_MOD = 6
D_IN = 9 * GROUP_W
EPS = 1e-6
LAM_RE_MAX = -1e-4

N_CTX = BATCH * SEQ
N_LAT = DEC_BATCH * DEC_SEQ
N_TOK = N_CTX + N_LAT
TB = 512
TBP = 512
NB_CTX = N_CTX // TB
S5_L = 16
HG_L = 64
FF_TILE = 256
LANES = 128
MOD_ROWS = 8
V7X_VMEM_BYTES = 64 * 1024 * 1024
VMEM_LIMIT = V7X_VMEM_BYTES - 8 * 1024 * 1024

F32 = jnp.float32
BF16 = jnp.bfloat16

COL_XA, COL_HQ, COL_HF_FWD, COL_HF_BWD, COL_HI, COL_HGATE, COL_XC, COL_GU, COL_GV = range(9)
HG_Z_W = 2 * GROUP_W
HG_QV_W = 2 * GROUP_W
OP_IN_W = 4 * GROUP_W


def _mod_row(i, tb):
    return jnp.where(i < N_CTX // tb, 0, 1 + (i - N_CTX // tb) // (DEC_SEQ // tb))


GELU_C0 = 0.7978845608028654
GELU_C1 = GELU_C0 * 0.044715


def _gelu(x):
    return x * (0.5 + 0.5 * jnp.tanh(x * (GELU_C0 + GELU_C1 * (x * x))))


def _rms(x, g):
    return x * lax.rsqrt(jnp.mean(x * x, axis=-1, keepdims=True) + EPS) * g


def _dot(a, b):
    return jnp.dot(a, b, preferred_element_type=F32)


def _dot_nt(a, b, precision=None):
    return lax.dot_general(a, b, (((1,), (1,)), ((), ())), precision=precision, preferred_element_type=F32)


def _dot_tn(a, b):
    return lax.dot_general(a, b, (((0,), (0,)), ((), ())), preferred_element_type=F32)


def _const_spec(shape, single=False):
    kw = {"pipeline_mode": pl.Buffered(1)} if single else {}
    return pl.BlockSpec(shape, lambda *_: (0,) * len(shape), **kw)


def _layer_spec(shape, l, single=False):
    kw = {"pipeline_mode": pl.Buffered(1)} if single else {}
    return pl.BlockSpec((None,) + tuple(shape), lambda *_: (l,) + (0,) * len(shape), **kw)


def _mod_spec(l, tb=TB):
    return pl.BlockSpec((None, None, 1, N_MOD * D_MODEL), lambda i: (l, _mod_row(i, tb), 0, 0))


def _x_specs(split, width=D_MODEL, tb=TB):
    nb_ctx = N_CTX // tb
    if split:
        return [pl.BlockSpec((tb, width), lambda i: (jnp.minimum(i, nb_ctx - 1), 0)),
                pl.BlockSpec((tb, width), lambda i: (jnp.maximum(i - nb_ctx, 0), 0))]
    return [pl.BlockSpec((tb, width), lambda i: (i, 0))]


def _pick(refs, rows, is_ctx):
    if len(refs) == 1:
        return refs[0][rows, :]
    return jnp.where(is_ctx, refs[0][rows, :], refs[1][rows, :])


INPROJ_SUB = 256
INPROJ_OUT = ((HG_Z_W, F32), (HG_QV_W, BF16), (OP_IN_W, BF16), (GROUP_W, BF16), (LANES, F32), (LANES, F32))


def _inproj_kernel(*refs):
    *x_refs, mod_ref, g_ref, w_ref, hz_ref, hqv_ref, op_ref, xc_ref, xalo_ref, xahi_ref, wb_ref = refs

    @pl.when(pl.program_id(0) == 0)
    def _():
        wb_ref[...] = w_ref[...].astype(BF16)

    mod = mod_ref[...]
    sh = mod[:, 0:D_MODEL]
    sc = mod[:, D_MODEL:2 * D_MODEL]
    is_ctx = pl.program_id(0) < N_CTX // TBP
    for r in range(0, TBP, INPROJ_SUB):
        rows = slice(r, r + INPROJ_SUB)
        h = _rms(_pick(x_refs, rows, is_ctx), g_ref[...]) * (1.0 + sc) + sh
        o = _dot(h.astype(BF16), wb_ref[...])
        col = lambda c, n=1: o[:, GROUP_W * c:GROUP_W * (c + n)]
        hz_ref[rows, :] = col(COL_HF_FWD, 2)
        hqv_ref[rows, :] = jnp.concatenate([col(COL_HQ), col(COL_HI)], axis=1).astype(BF16)
        op_ref[rows, :] = jnp.concatenate([col(COL_XA), col(COL_HGATE), col(COL_GU, 2)], axis=1).astype(BF16)
        xc_ref[rows, :] = col(COL_XC).astype(BF16)
        xalo_ref[rows, :] = o[:, 0:LANES]
        xahi_ref[rows, :] = o[:, LANES:2 * LANES]


def _inproj(xs, mods, norm_g, w_in, l):
    return pl.pallas_call(
        _inproj_kernel,
        grid=(N_TOK // TBP,),
        in_specs=_x_specs(len(xs) == 2, tb=TBP) + [
            _mod_spec(l, TBP),
            _layer_spec((1, D_MODEL), l),
            _layer_spec((D_MODEL, D_IN), l, single=True),
        ],
        out_specs=tuple(pl.BlockSpec((TBP, w), lambda i: (i, 0)) for w, _ in INPROJ_OUT),
        out_shape=tuple(jax.ShapeDtypeStruct((N_TOK, w), dt) for w, dt in INPROJ_OUT),
        scratch_shapes=[pltpu.VMEM((D_MODEL, D_IN), BF16)],
        compiler_params=pltpu.CompilerParams(vmem_limit_bytes=VMEM_LIMIT),
        name="in_proj",
    )(*xs, mods, norm_g, w_in)


S5_NPAIR = S5_G // 2
S5_CW = S5_L * S5_P
S5_SW = S5_G * S5_N
S5_STEP_ROWS = N_CTX // S5_L
S5_CTX_NC = SEQ // S5_L
S5_LAT_NC = DEC_SEQ // S5_L


def _s5_slot(g, t):
    return (t + g) % S5_L


def _s5_tables(lam_re, lam_im, log_dt, b_re, b_im, c_re, c_im):
    lr = jnp.minimum(lam_re.astype(F32), LAM_RE_MAX)
    li = lam_im.astype(F32)
    dt = jnp.exp(log_dt.astype(F32))[..., None]
    mag = jnp.exp(lr * dt)
    ang = li * dt
    ab_re = mag * jnp.cos(ang)
    ab_im = mag * jnp.sin(ang)
    den = lr * lr + li * li
    xr = ab_re - 1.0
    z_re = (xr * lr + ab_im * li) / den
    z_im = (ab_im * lr - xr * li) / den
    bb_re = z_re[..., None] * b_re - z_im[..., None] * b_im
    bb_im = z_re[..., None] * b_im + z_im[..., None] * b_re
    tau = jnp.arange(S5_L + 1, dtype=F32)[:, None, None, None, None]
    pm = jnp.exp(lr * dt * tau)
    pa = li * dt * tau
    pw_re = pm * jnp.cos(pa)
    pw_im = pm * jnp.sin(pa)
    eye2 = jnp.eye(2, dtype=F32)

    def pw_pairs(a):
        a = a.transpose(1, 2, 3, 0, 4).reshape(DEPTH, 2, S5_NPAIR, 2, S5_L + 1, S5_N)
        return a.transpose(0, 1, 2, 4, 3, 5).reshape(DEPTH, 2, S5_NPAIR, S5_L + 1, 2 * S5_N)

    def mat_pairs(a):
        a = a.reshape(DEPTH, 2, S5_NPAIR, 2, S5_P, S5_N)
        return jnp.einsum('ldjaqn,ab->ldjaqbn', a, eye2).reshape(DEPTH, 2, S5_NPAIR, 2, S5_P, 2 * S5_N)

    al = jnp.stack([pw_re[S5_L], pw_im[S5_L]], axis=2).reshape(DEPTH, 2, 2, 1, S5_SW)
    return (pw_pairs(pw_re), pw_pairs(pw_im),
            mat_pairs(bb_re.transpose(0, 1, 2, 4, 3)), mat_pairs(bb_im.transpose(0, 1, 2, 4, 3)),
            mat_pairs(c_re.astype(F32)), mat_pairs(c_im.astype(F32)), al)


def _s5_shift_consts():
    n = S5_CW
    r = np.arange(n)[:, None]
    c = np.arange(n)[None, :]
    fwd = [(c == r + S5_P * s) for s in range(S5_L)]
    bwd = [(c == r - S5_P * (S5_L - 1 - s)) for s in range(S5_L)]
    return np.stack([np.concatenate(fwd, axis=1), np.concatenate(bwd, axis=1)]).astype(np.float32)


PREP_STEPS = 4
ADA_TN = N_MOD * D_MODEL // PREP_STEPS


def _prep_kernel(c_ref, wada_ref, bada_ref, pwr_ref, pwi_ref, br_ref, bi_ref, cr_ref, ci_ref, scat_ref,
                 mod_ref, m_ref, f_ref, e_ref, k_ref):
    cv = c_ref[...]
    mod_ref[0] = _dot((cv * jax.nn.sigmoid(cv)).astype(BF16), wada_ref[0].astype(BF16)) + bada_ref[0]
    step = pl.program_id(1)
    for qq in range(PREP_STEPS):
        pl.when(step == qq)(functools.partial(_s5_pair_ops, qq, pwr_ref, pwi_ref, br_ref, bi_ref, cr_ref,
                                              ci_ref, f_ref, e_ref, k_ref))
    pl.when(step == PREP_STEPS - 1)(functools.partial(_s5_toeplitz, scat_ref, m_ref, k_ref))


def _s5_pair_ops(qq, pwr_ref, pwi_ref, br_ref, bi_ref, cr_ref, ci_ref, f_ref, e_ref, k_ref):
    L = S5_L
    hp = lax.Precision.HIGHEST
    sw = 2 * S5_N
    per_step = S5_NPAIR // PREP_STEPS
    for j in range(per_step * qq, per_step * (qq + 1)):
        for d in range(2):
            pwr = pwr_ref[0, d, j]
            pwi = pwi_ref[0, d, j]
            for gl in range(2):
                g = 2 * j + gl
                br, bi = br_ref[0, d, j, gl], bi_ref[0, d, j, gl]
                cr, ci = cr_ref[0, d, j, gl], ci_ref[0, d, j, gl]
                ca_r, ca_i, f_r, f_i, e_r, e_i = [], [], [], [], [], []
                for t in range(L):
                    kk = t if d == 0 else L - 1 - t
                    pr, pi = pwr[kk:kk + 1], pwi[kk:kk + 1]
                    ca_r.append(cr * pr - ci * pi)
                    ca_i.append(cr * pi + ci * pr)
                    kf = L - 1 - t if d == 0 else t
                    pr, pi = pwr[kf:kf + 1], pwi[kf:kf + 1]
                    f_r.append(br * pr - bi * pi)
                    f_i.append(br * pi + bi * pr)
                    ke = t + 1 if d == 0 else L - t
                    pr, pi = pwr[ke:ke + 1], pwi[ke:ke + 1]
                    e_r.append(cr * pr - ci * pi)
                    e_i.append(-(cr * pi + ci * pr))
                cat = lambda xs: jnp.concatenate(xs, axis=0)
                k = _dot_nt(br, cat(ca_r), hp) - _dot_nt(bi, cat(ca_i), hp)
                k_ref[d, S5_P * g:S5_P * (g + 1), :] = k
                slots = lambda xs: cat([xs[(p - g) % L] for p in range(L)])
                rows = slice(S5_CW * gl, S5_CW * (gl + 1))
                f_ref[0, d, j, rows, 0:sw] = slots(f_r).astype(BF16)
                f_ref[0, d, j, rows, sw:2 * sw] = slots(f_i).astype(BF16)
                e_ref[0, d, 0, j, rows, :] = slots(e_r).astype(BF16)
                e_ref[0, d, 1, j, rows, :] = slots(e_i).astype(BF16)


def _s5_toeplitz(scat_ref, m_ref, k_ref):
    L = S5_L
    kf = k_ref[0].astype(BF16)
    kb = k_ref[1].astype(BF16)
    for s in range(L):
        cols = slice(S5_CW * s, S5_CW * (s + 1))
        res = _dot(kf, scat_ref[0, :, cols]) + _dot(kb, scat_ref[1, :, cols])
        for g in range(S5_G):
            blk = res[S5_P * g:S5_P * (g + 1), :]
            if g:
                blk = pltpu.roll(blk, S5_P * g, 1)
            r0 = S5_P * _s5_slot(g, s)
            m_ref[0, g, r0:r0 + S5_P, :] = blk.astype(BF16)


def _prep(cvecs, w_ada, b_ada, tables, scat):
    pwr, pwi, br, bi, cr, ci, _ = tables
    n = N_MOD * D_MODEL

    def lspec(shape):
        return pl.BlockSpec((1,) + shape, lambda l, q: (l,) + (0,) * len(shape))

    pw_shape = (2, S5_NPAIR, S5_L + 1, 2 * S5_N)
    mat_shape = (2, S5_NPAIR, 2, S5_P, 2 * S5_N)
    return pl.pallas_call(
        _prep_kernel,
        grid=(DEPTH, PREP_STEPS),
        in_specs=[pl.BlockSpec((MOD_ROWS, D_MODEL), lambda l, q: (0, 0)),
                  pl.BlockSpec((1, D_MODEL, ADA_TN), lambda l, q: (l, 0, q)),
                  pl.BlockSpec((1, 1, ADA_TN), lambda l, q: (l, 0, q)),
                  lspec(pw_shape), lspec(pw_shape), lspec(mat_shape), lspec(mat_shape), lspec(mat_shape),
                  lspec(mat_shape), _const_spec(scat.shape)],
        out_specs=(pl.BlockSpec((1, MOD_ROWS, ADA_TN), lambda l, q: (l, 0, q)),
                   lspec((S5_G, S5_CW, S5_CW)), lspec((2, S5_NPAIR, 2 * S5_CW, 4 * S5_N)),
                   lspec((2, 2, S5_NPAIR, 2 * S5_CW, 2 * S5_N))),
        out_shape=(jax.ShapeDtypeStruct((DEPTH, MOD_ROWS, n), F32),
                   jax.ShapeDtypeStruct((DEPTH, S5_G, S5_CW, S5_CW), BF16),
                   jax.ShapeDtypeStruct((DEPTH, 2, S5_NPAIR, 2 * S5_CW, 4 * S5_N), BF16),
                   jax.ShapeDtypeStruct((DEPTH, 2, 2, S5_NPAIR, 2 * S5_CW, 2 * S5_N), BF16)),
        scratch_shapes=[pltpu.VMEM((2, S5_G * S5_P, S5_CW), F32)],
        compiler_params=pltpu.CompilerParams(vmem_limit_bytes=VMEM_LIMIT),
        name="param_prep",
    )(cvecs, w_ada, b_ada.reshape(DEPTH, 1, n), pwr, pwi, br, bi, cr, ci, scat)


def _s5_kernel(xlo_ref, xhi_ref, m_ref, f_ref, e_ref, al_ref, h0_ref, ylo_ref, yhi_ref, hfin_ref,
               uy_ref, zh_ref):
    R = S5_STEP_ROWS
    pw = 2 * S5_CW
    sw = 2 * S5_N
    gph = S5_G // 2
    step = pl.program_id(0)

    for half, x_ref in enumerate((xlo_ref, xhi_ref)):
        for t in range(S5_L):
            xt = x_ref[pl.ds(t, R, stride=S5_L), :]
            shift = (S5_P * t) % LANES
            xr = pltpu.roll(xt, shift, 1) if shift else xt
            for gl in range(gph):
                g = gph * half + gl
                src = (S5_P * gl + shift) % LANES
                lo = S5_CW * g + S5_P * _s5_slot(g, t)
                assert lo % LANES == src
                uy_ref[:, lo:lo + S5_P] = xr[:, src:src + S5_P]

    for j in range(S5_NPAIR):
        ub = uy_ref[:, pw * j:pw * (j + 1)].astype(BF16)
        for d in range(2):
            o = _dot(ub, f_ref[d, j])
            zh_ref[d, 0, :, sw * j:sw * (j + 1)] = o[:, :sw]
            zh_ref[d, 1, :, sw * j:sw * (j + 1)] = o[:, sw:]

    def scan_group(seq_rows, nc, init, fin_rows):
        chains = [(d, r0) for d in range(2) for r0 in seq_rows]

        def body(c, carry):
            out = []
            for k, (d, r0) in enumerate(chains):
                hr, hi = carry[2 * k], carry[2 * k + 1]
                r = r0 + (c if d == 0 else nc - 1 - c)
                zr = zh_ref[d, 0, pl.ds(r, 1), :]
                zi = zh_ref[d, 1, pl.ds(r, 1), :]
                zh_ref[d, 0, pl.ds(r, 1), :] = hr
                zh_ref[d, 1, pl.ds(r, 1), :] = hi
                ar = al_ref[d, 0]
                ai = al_ref[d, 1]
                out.append(ar * hr - ai * hi + zr)
                out.append(ar * hi + ai * hr + zi)
            return tuple(out)

        fin = lax.fori_loop(0, nc, body, tuple(init))
        if fin_rows is not None:
            for k, (d, _) in enumerate(chains):
                s = fin_rows[k % len(seq_rows)]
                hfin_ref[d, 0, s:s + 1, :] = fin[2 * k]
                hfin_ref[d, 1, s:s + 1, :] = fin[2 * k + 1]

    @pl.when(step == 0)
    def _():
        zero = jnp.zeros((1, S5_SW), F32)
        for s0 in range(0, BATCH, 4):
            seqs = list(range(s0, s0 + 4))
            scan_group([s * S5_CTX_NC for s in seqs], S5_CTX_NC, [zero] * 16, seqs)

    @pl.when(step == 1)
    def _():
        init = []
        for d in range(2):
            for s in range(DEC_BATCH):
                init.append(h0_ref[d, 0, s:s + 1, :])
                init.append(h0_ref[d, 1, s:s + 1, :])
        scan_group([s * S5_LAT_NC for s in range(DEC_BATCH)], S5_LAT_NC, init, None)

    for j in range(S5_NPAIR):
        acc = None
        for d in range(2):
            for ri in range(2):
                hb = zh_ref[d, ri, :, sw * j:sw * (j + 1)].astype(BF16)
                t = _dot_nt(hb, e_ref[d, ri, j])
                acc = t if acc is None else acc + t
        for gl in range(2):
            g = 2 * j + gl
            ug = uy_ref[:, S5_CW * g:S5_CW * (g + 1)].astype(BF16)
            uy_ref[:, S5_CW * g:S5_CW * (g + 1)] = acc[:, S5_CW * gl:S5_CW * (gl + 1)] + _dot(ug, m_ref[g])

    lane = lax.broadcasted_iota(jnp.int32, (R, LANES), 1)
    in_piece = [jnp.logical_and(lane >= S5_P * k, lane < S5_P * (k + 1)) for k in range(LANES // S5_P)]
    for half, y_ref in enumerate((ylo_ref, yhi_ref)):
        for t in range(S5_L):
            merged = None
            for gl in range(gph):
                g = gph * half + gl
                slot = _s5_slot(g, t)
                col = S5_CW * g + LANES * (S5_P * slot // LANES)
                v = uy_ref[:, col:col + LANES]
                merged = v if merged is None else jnp.where(in_piece[slot % (LANES // S5_P)], v, merged)
            shift = (LANES - S5_P * t % LANES) % LANES
            y_ref[pl.ds(t, R, stride=S5_L), :] = pltpu.roll(merged, shift, 1) if shift else merged


def _s5(xa_lo, xa_hi, m, f2, e2, al, h0, l):
    half = pl.BlockSpec((N_CTX, LANES), lambda s: (s, 0))
    return pl.pallas_call(
        _s5_kernel,
        grid=(2,),
        in_specs=[half, half,
                  _layer_spec(m.shape[1:], l, single=True), _layer_spec(f2.shape[1:], l, single=True),
                  _layer_spec(e2.shape[1:], l, single=True), _layer_spec(al.shape[1:], l),
                  _layer_spec(h0.shape[1:], l)],
        out_specs=(half, half, _const_spec((2, 2, BATCH, S5_SW))),
        out_shape=(jax.ShapeDtypeStruct((N_TOK, LANES), F32),
                   jax.ShapeDtypeStruct((N_TOK, LANES), F32),
                   jax.ShapeDtypeStruct((2, 2, BATCH, S5_SW), F32)),
        scratch_shapes=[pltpu.VMEM((S5_STEP_ROWS, S5_G * S5_CW), F32),
                        pltpu.VMEM((2, 2, S5_STEP_ROWS, S5_SW), F32)],
        compiler_params=pltpu.CompilerParams(vmem_limit_bytes=VMEM_LIMIT),
        name="s5_mixer",
    )(xa_lo, xa_hi, m, f2, e2, al, h0)


HG_NLEV = int(math.log2(HG_L))
HG_W = HG_HEADS * HG_DK
HG_CPS = 8
HG_BLK = HG_CPS * HG_L
HG_NB_CTX = N_CTX // HG_BLK
HG_SPB = HG_BLK // SEQ
HG_CPQ = SEQ // HG_L
HG_NC_LAT = DEC_SEQ // HG_BLK
assert HG_SPB >= 1 and HG_SPB * SEQ == HG_BLK
assert HG_L == HG_DK


def _hgrn_consts():
    L = HG_L
    w = np.zeros((HG_NLEV + 2, L, L), np.float32)
    mask = np.zeros((HG_NLEV + 1, L, L), np.float32)
    for lev in range(HG_NLEV):
        blk = L >> lev
        half = blk // 2
        for t in range(L):
            p, o = divmod(t, blk)
            bd = p * blk + half - 1
            if o >= half:
                w[lev, t, bd + 1:t + 1] = 1.0
            else:
                w[lev, t, t + 1:bd + 1] = 1.0
        jj, ii = np.meshgrid(np.arange(L), np.arange(L), indexing='ij')
        mask[lev] = ((jj // blk == ii // blk) & (jj % blk >= half) & (ii % blk < half)).astype(np.float32)
    mask[HG_NLEV] = np.eye(L, dtype=np.float32)
    for t in range(L):
        w[HG_NLEV, t, :t + 1] = 1.0
        w[HG_NLEV + 1, t, t + 1:] = 1.0
    out = []
    for wd, md in ((w, mask), (w[:, ::-1, ::-1], mask[:, ::-1, ::-1])):
        wflat = wd.reshape((HG_NLEV + 2) * L, L)
        out.append((np.concatenate([wflat] * 3, axis=1), np.tile(md, (1, 1, HG_HEADS))))
    wcat = np.stack([out[0][0], out[1][0]])
    mask4 = np.stack([out[0][1], out[1][1]])
    hm = np.kron(np.eye(HG_HEADS, dtype=np.float32), np.ones((HG_DK, HG_DK), np.float32))
    return wcat, mask4, hm


def _hg_pos(i):
    is_ctx = i < HG_NB_CTX
    c = jnp.where(is_ctx, 0, (i - HG_NB_CTX) % HG_NC_LAT)
    nc = jnp.where(is_ctx, 1, HG_NC_LAT)
    return is_ctx, c, nc


def _hg_bwd_blk(i):
    _, c, nc = _hg_pos(i)
    return i + nc - 1 - 2 * c


def _hg_local(items, lb_ref, wcat_ref, mask_ref, hm):
    L = HG_L
    hmb = hm.astype(BF16)

    def bd4(x):
        return jnp.concatenate([x] * HG_HEADS, axis=0) * hmb

    kks, exs = [], []
    for d, q, z, v in items:
        lbp = lb_ref[d]
        sp = jnp.log1p(jnp.exp(-jnp.abs(z)))
        ls = jnp.minimum(z, 0.0) - sp
        kks.append(lbp[2:3] * jnp.exp(jnp.minimum(-z, 0.0) - sp))
        a = lbp[0:1]
        b = lbp[1:2] + ls
        logf = jnp.maximum(a, b) + jnp.log1p(jnp.exp(-jnp.abs(a - b)))
        p0 = logf.astype(BF16)
        r0 = logf - p0.astype(F32)
        p1 = r0.astype(BF16)
        p2 = (r0 - p1.astype(F32)).astype(BF16)
        exs.append(jnp.exp(_dot(wcat_ref[d], jnp.concatenate([p0, p1, p2], axis=0))))
    scs = [None] * len(items)
    for lev in range(HG_NLEV + 1):
        for n, (d, q, z, v) in enumerate(items):
            if lev < HG_NLEV:
                al = exs[n][L * lev:L * (lev + 1)]
                lhs = (q * al).astype(BF16)
                rhs = (kks[n] * al).astype(BF16)
            else:
                lhs = q.astype(BF16)
                rhs = kks[n].astype(BF16)
            t = _dot_nt(lhs, bd4(rhs)) * mask_ref[d, lev]
            scs[n] = t if scs[n] is None else scs[n] + t
    out = []
    for n, (d, q, z, v) in enumerate(items):
        vb = v.astype(BF16)
        o = _dot(scs[n].astype(BF16), bd4(vb))
        eq = exs[n][L * HG_NLEV:L * (HG_NLEV + 1)]
        qe = (q * eq).astype(BF16)
        ke = (kks[n] * exs[n][L * (HG_NLEV + 1):L * (HG_NLEV + 2)]).astype(BF16)
        g = eq[L - 1:L] if d == 0 else eq[0:1]
        out.append((o, qe, g, _dot_tn(vb, ke) * hm))
    return out


def _hgrn_kernel(zf_ref, qvf_ref, zb_ref, qvb_ref, lb_ref, w_ref, mask_ref, hm_ref, s0_ref, wup_ref, wdn_ref,
                 of_ref, ob_ref, sfin_ref, wupb_ref, wdnb_ref, st_ref):
    is_ctx, c, nc = _hg_pos(pl.program_id(0))
    wupb_ref[...] = wup_ref[...].astype(BF16)
    wdnb_ref[...] = wdn_ref[...].astype(BF16)

    hm = hm_ref[...]
    rows = [slice(HG_L * k, HG_L * (k + 1)) for k in range(HG_CPS)]
    items = [(d, qv_ref[r, 0:GROUP_W], z_ref[r, GROUP_W * d:GROUP_W * (d + 1)], qv_ref[r, GROUP_W:2 * GROUP_W])
             for d, (z_ref, qv_ref) in enumerate(((zf_ref, qvf_ref), (zb_ref, qvb_ref))) for r in rows]
    loc = _hg_local(items, lb_ref, w_ref, mask_ref, hm)

    def chain(d, o_ref, st, chunks):
        for k in (chunks if d == 0 else reversed(chunks)):
            o, qe, g, upd = loc[HG_CPS * d + k]
            o_ref[rows[k], :] = (o + _dot_nt(qe, st.astype(BF16))).astype(BF16)
            st = st * g + upd
        return st

    @pl.when(is_ctx)
    def _():
        for sq in range(HG_SPB):
            for d, o_ref in enumerate((of_ref, ob_ref)):
                st = chain(d, o_ref, jnp.zeros((HG_W, HG_W), F32), range(HG_CPQ * sq, HG_CPQ * (sq + 1)))
                s_kv = st.T
                for h in range(HG_HEADS):
                    sfin_ref[sq, d, h] = s_kv[HG_DK * h:HG_DK * (h + 1), HG_DK * h:HG_DK * (h + 1)]

    @pl.when(jnp.logical_not(is_ctx))
    def _():
        @pl.when(c == 0)
        def _():
            st_ref[...] = s0_ref[0]

        for d, o_ref in enumerate((of_ref, ob_ref)):
            st_ref[d] = chain(d, o_ref, st_ref[d], range(HG_CPS))


def _hgrn(hz, hqv, lbp, consts, s0, w_up, w_down, l):
    wcat, mask4, hm = consts
    steps = N_TOK // HG_BLK
    up_rows = D_MODEL // steps
    dn_rows = 2 * D_FF // steps

    return pl.pallas_call(
        _hgrn_kernel,
        grid=(N_TOK // HG_BLK,),
        in_specs=[pl.BlockSpec((HG_BLK, HG_Z_W), lambda i: (i, 0)),
                  pl.BlockSpec((HG_BLK, HG_QV_W), lambda i: (i, 0)),
                  pl.BlockSpec((HG_BLK, HG_Z_W), lambda i: (_hg_bwd_blk(i), 0)),
                  pl.BlockSpec((HG_BLK, HG_QV_W), lambda i: (_hg_bwd_blk(i), 0)),
                  _layer_spec(lbp.shape[1:], l), _const_spec(wcat.shape), _const_spec(mask4.shape),
                  _const_spec(hm.shape),
                  pl.BlockSpec((None, 1, 2, HG_W, HG_W),
                               lambda i: (l, jnp.maximum(i - HG_NB_CTX, 0) // HG_NC_LAT, 0, 0, 0)),
                  pl.BlockSpec((None, up_rows, 2 * D_FF), lambda i: (l, i, 0)),
                  pl.BlockSpec((None, dn_rows, D_MODEL), lambda i: (l, i // 2, 0))],
        out_specs=(pl.BlockSpec((HG_BLK, HG_W), lambda i: (i, 0)),
                   pl.BlockSpec((HG_BLK, HG_W), lambda i: (_hg_bwd_blk(i), 0)),
                   pl.BlockSpec((HG_SPB, 2, HG_HEADS, HG_DK, HG_DK),
                                lambda i: (jnp.minimum(i, HG_NB_CTX - 1), 0, 0, 0, 0)),
                   pl.BlockSpec((up_rows, 2 * D_FF), lambda i: (i, 0)),
                   pl.BlockSpec((dn_rows, D_MODEL), lambda i: (i // 2, 0))),
        out_shape=(jax.ShapeDtypeStruct((N_TOK, HG_W), BF16),
                   jax.ShapeDtypeStruct((N_TOK, HG_W), BF16),
                   jax.ShapeDtypeStruct((BATCH, 2, HG_HEADS, HG_DK, HG_DK), F32),
                   jax.ShapeDtypeStruct((D_MODEL, 2 * D_FF), BF16),
                   jax.ShapeDtypeStruct((D_FF, D_MODEL), BF16)),
        scratch_shapes=[pltpu.VMEM((2, HG_W, HG_W), F32)],
        compiler_params=pltpu.CompilerParams(vmem_limit_bytes=VMEM_LIMIT),
        name="hgrn_mixer",
    )(hz, hqv, hz, hqv, lbp, wcat, mask4, hm, s0, w_up, w_down)


FN_CTX_SPG = 4
FN_LAT_TR = 512


def _dft_consts(t_len):
    n = GROUP_W // 4
    k = np.arange(n)
    ang = 2.0 * np.pi * ((k[:, None] * k[None, :]) % n) / n
    eye = np.eye(4)
    cs = np.concatenate([np.kron(eye, np.cos(ang)), np.kron(eye, np.sin(ang))], axis=1) / math.sqrt(n)
    t = np.arange(t_len)
    angt = 2.0 * np.pi * ((t[:, None] * t[None, :]) % t_len) / t_len
    dft = np.concatenate([np.cos(angt), -np.sin(angt)], axis=1) / math.sqrt(t_len)
    return cs.astype(np.float32), dft.astype(np.float32)


def _fnet_kernel(x_ref, cs_ref, dft_ref, w_ref, o_ref, r_ref, *, t_len, spg, tr):
    @pl.when(pl.program_id(1) == 0)
    def _():
        t = _dot(x_ref[...].astype(BF16), cs_ref[...])
        for s in range(spg):
            rows = slice(t_len * s, t_len * (s + 1))
            r_ref[s, 0:t_len, :] = t[rows, :GROUP_W].astype(BF16)
            r_ref[s, t_len:2 * t_len, :] = t[rows, GROUP_W:].astype(BF16)

    w = w_ref[...].astype(BF16)
    for s in range(spg):
        y = _dot(dft_ref[...], r_ref[s])
        o_ref[tr * s:tr * (s + 1), :] = _dot(y.astype(BF16), w).astype(BF16)


def _fnet(xc, cs, dft, fn_w, l, *, n_seq, seq_len, tok0, spg, tr):
    nj = seq_len // tr
    assert spg == 1 or nj == 1
    gb0 = tok0 // (spg * seq_len)
    return pl.pallas_call(
        functools.partial(_fnet_kernel, t_len=seq_len, spg=spg, tr=tr),
        grid=(n_seq // spg, nj),
        in_specs=[
            pl.BlockSpec((spg * seq_len, GROUP_W), lambda g, j: (gb0 + g, 0)),
            _const_spec((GROUP_W, 2 * GROUP_W)),
            pl.BlockSpec((tr, 2 * seq_len), lambda g, j: (j, 0)),
            _layer_spec((GROUP_W, GROUP_W), l),
        ],
        out_specs=pl.BlockSpec((spg * tr, GROUP_W), lambda g, j: (g * nj + j, 0)),
        out_shape=jax.ShapeDtypeStruct((n_seq * seq_len, GROUP_W), BF16),
        scratch_shapes=[pltpu.VMEM((spg, 2 * seq_len, GROUP_W), BF16)],
        compiler_params=pltpu.CompilerParams(vmem_limit_bytes=VMEM_LIMIT),
        name="fnet_mixer",
    )(xc, cs, dft, fn_w)


OUTPROJ_SUB = 256


def _outproj_kernel(*refs):
    (op_ref, ylo_ref, yhi_ref, of_ref, ob_ref, ycc_ref, ycl_ref, mod_ref, d_ref,
     wglu_ref, gn_ref, bdm_ref, gmn_ref, gmw_ref, gmb_ref, gmh_ref, wout_ref, n2_ref,
     x1_ref, h2_ref, wob_ref) = refs[-21:]
    x_refs = refs[:-21]
    xa_ref, hgate_ref, gu_ref, gv_ref = (op_ref.at[:, GROUP_W * k:GROUP_W * (k + 1)] for k in range(4))

    @pl.when(pl.program_id(0) == 0)
    def _():
        wob_ref[...] = wout_ref[...].astype(BF16)

    gn = gn_ref[...]
    gm_hm = gmh_ref[...]
    mod = mod_ref[...]
    g1 = mod[:, 2 * D_MODEL:3 * D_MODEL]
    sh2 = mod[:, 3 * D_MODEL:4 * D_MODEL]
    sc2 = mod[:, 4 * D_MODEL:5 * D_MODEL]
    is_ctx = pl.program_id(0) < N_CTX // TBP
    for r in range(0, TBP, OUTPROJ_SUB):
        rows = slice(r, r + OUTPROJ_SUB)
        ys = jnp.concatenate([ylo_ref[rows, :], yhi_ref[rows, :]], axis=1)
        y5 = _gelu(ys + d_ref[...] * xa_ref[rows, :].astype(F32))
        glu = jax.nn.sigmoid(_dot(y5.astype(BF16), wglu_ref[...].astype(BF16)))
        out_a = _rms(y5 * glu, gn[:, 0:GROUP_W])
        o = of_ref[rows, :].astype(F32) + ob_ref[rows, :].astype(F32)
        o2 = o * o
        o2h = o2.astype(BF16)
        o2l = (o2 - o2h.astype(F32)).astype(BF16)
        ms = _dot(o2h, bdm_ref[...]) + _dot(o2l, bdm_ref[...])
        hg = hgate_ref[rows, :].astype(F32)
        out_b = o * lax.rsqrt(ms + EPS) * gn[:, GROUP_W:2 * GROUP_W] * (hg * jax.nn.sigmoid(hg))
        out_c = _rms(_pick((ycc_ref, ycl_ref), rows, is_ctx).astype(F32), gn[:, 2 * GROUP_W:3 * GROUP_W])
        out_d = []
        for c in range(r // GM_CHUNK, (r + OUTPROJ_SUB) // GM_CHUNK):
            crow = slice(GM_CHUNK * c, GM_CHUNK * (c + 1))
            gv = _rms(_gelu(gv_ref[crow, :].astype(F32)), gmn_ref[...]).astype(BF16)
            g4 = jnp.concatenate([gv] * GM_HEADS, axis=0) * gm_hm
            sp = _dot(gmw_ref[...], g4) + gmb_ref[...]
            out_d.append(_rms(_gelu(gu_ref[crow, :].astype(F32)) * sp, gn[:, 3 * GROUP_W:]))
        out_d = jnp.concatenate(out_d, axis=0)
        m = None
        for k, part in enumerate((out_a, out_b, out_c, out_d)):
            t = _dot(part.astype(BF16), wob_ref[GROUP_W * k:GROUP_W * (k + 1), :])
            m = t if m is None else m + t
        x1 = _pick(x_refs, rows, is_ctx) + g1 * m
        x1_ref[rows, :] = x1
        h2_ref[rows, :] = (_rms(x1, n2_ref[...]) * (1.0 + sc2) + sh2).astype(BF16)


def _outproj(xs, op_in, ylo, yhi, of, ob, yc, mods, s5_d, wglu, gn, bdm, gm_norm_g, gm_w, gm_b, gm_hm, wout,
             norm2_g, l):
    def tok(width, col=0):
        return pl.BlockSpec((TBP, width), lambda i: (i, col))

    return pl.pallas_call(
        _outproj_kernel,
        grid=(N_TOK // TBP,),
        in_specs=_x_specs(len(xs) == 2, tb=TBP) + [
            tok(OP_IN_W), tok(LANES), tok(LANES), tok(GROUP_W), tok(GROUP_W), *_x_specs(True, GROUP_W, TBP),
            _mod_spec(l, TBP),
            _layer_spec((1, GROUP_W), l), _layer_spec((GROUP_W, GROUP_W), l), _layer_spec((1, D_MODEL), l),
            _const_spec((GROUP_W, GROUP_W)),
            _layer_spec((1, GROUP_W), l), _layer_spec((GM_CHUNK, GM_HEADS * GM_CHUNK), l),
            _layer_spec((GM_CHUNK, GROUP_W), l), _const_spec((GM_HEADS * GM_CHUNK, GROUP_W)),
            _layer_spec((D_MODEL, D_MODEL), l, single=True), _layer_spec((1, D_MODEL), l),
        ],
        out_specs=(tok(D_MODEL), tok(D_MODEL)),
        out_shape=(jax.ShapeDtypeStruct((N_TOK, D_MODEL), F32),
                   jax.ShapeDtypeStruct((N_TOK, D_MODEL), BF16)),
        scratch_shapes=[pltpu.VMEM((D_MODEL, D_MODEL), BF16)],
        compiler_params=pltpu.CompilerParams(vmem_limit_bytes=VMEM_LIMIT),
        name="out_proj",
    )(*xs, op_in, ylo, yhi, of, ob, *yc, mods, s5_d, wglu, gn, bdm, gm_norm_g, gm_w, gm_b,
      gm_hm, wout, norm2_g)


FF_SEG = GRID_W
FF_NSEG = TB // FF_SEG
FF_SEQ_STRIPS = SEQ // FF_SEG
FF_PERM_ROWS = FF_SEQ_STRIPS * FF_SEG
SUBLANES = 8
FF_NGRP = FF_SEG // SUBLANES


def _ffn_perm_consts():
    p = np.zeros((FF_PERM_ROWS, FF_PERM_ROWS), np.float32)
    for k in range(FF_SEQ_STRIPS):
        for t in range(FF_SEG):
            p[FF_SEG * k + SUBLANES * (t % FF_NGRP) + t // FF_NGRP, FF_SEG * k + t] = 1.0
    return p


def _ffn_kernel(x1_ref, h2_ref, mod_ref, perm_ref, wup_ref, cw_ref, cb_ref, wdn_ref, fg_ref, *rest, final):
    *o_refs, h2p_ref, z_ref, hid_ref = rest
    i = pl.program_id(0)
    joined = (i < NB_CTX).astype(F32)
    n_tiles = D_FF // FF_TILE
    sub = lax.broadcasted_iota(jnp.int32, (SUBLANES, 2 * FF_TILE), 0)
    is_first = sub == 0
    is_last = sub == SUBLANES - 1
    zero_grp = jnp.zeros((SUBLANES, 2 * FF_TILE), F32)
    for r in range(0, TB, FF_PERM_ROWS):
        h2p_ref[r:r + FF_PERM_ROWS, :] = _dot(perm_ref[...], h2_ref[r:r + FF_PERM_ROWS, :]).astype(BF16)

    def up(j):
        lo = FF_TILE * j
        for c, col in enumerate((lo, D_FF + lo)):
            z_ref[j % 2, :, FF_TILE * c:FF_TILE * (c + 1)] = _dot(h2p_ref[...], wup_ref[:, col:col + FF_TILE])

    def gate(j):
        slot = j % 2
        lo = FF_TILE * j
        w = jnp.concatenate([cw_ref[:, lo:lo + FF_TILE], cw_ref[:, D_FF + lo:D_FF + lo + FF_TILE]], axis=1)
        b = jnp.concatenate([cb_ref[:, lo:lo + FF_TILE], cb_ref[:, D_FF + lo:D_FF + lo + FF_TILE]], axis=1)

        def grp(k, v):
            r = FF_SEG * k + SUBLANES * v
            return z_ref[slot, r:r + SUBLANES, :]

        down = [pltpu.roll(grp(k, FF_NGRP - 1), 1, 0) for k in range(FF_NSEG)]
        up_ = [pltpu.roll(grp(k, 0), SUBLANES - 1, 0) for k in range(FF_NSEG)]
        strips = []
        for k in range(FF_NSEG):
            g = [grp(k, v) for v in range(FF_NGRP)]
            before = down[k - 1] * joined if k % FF_SEQ_STRIPS > 0 else zero_grp
            after = up_[k + 1] * joined if k % FF_SEQ_STRIPS < FF_SEQ_STRIPS - 1 else zero_grp
            zm1 = jnp.concatenate([jnp.where(is_first, before, down[k])] + g[:-1], axis=0)
            zp1 = jnp.concatenate(g[1:] + [jnp.where(is_last, after, up_[k])], axis=0)
            zc = b + zm1 * w[0:1] + jnp.concatenate(g, axis=0) * w[1:2] + zp1 * w[2:3]
            strips.append((_gelu(zc[:, :FF_TILE]) * zc[:, FF_TILE:]).astype(BF16))
        hid_ref[:, lo:lo + FF_TILE] = jnp.concatenate(strips, axis=0)

    up(0)
    for j in range(n_tiles):
        if j + 1 < n_tiles:
            up(j + 1)
        gate(j)
    acc = _dot(hid_ref[...], wdn_ref[...])
    acc = jnp.swapaxes(acc.reshape(FF_NSEG, FF_NGRP, SUBLANES, D_MODEL), 1, 2).reshape(TB, D_MODEL)
    g2 = mod_ref[...][:, 5 * D_MODEL:6 * D_MODEL]
    x2 = x1_ref[...] + g2 * acc
    if final:
        y = _rms(x2, fg_ref[...])
        yp_ref, ys_ref = o_refs

        @pl.when(i < NB_CTX)
        def _():
            yp_ref[...] = y

        @pl.when(i >= NB_CTX)
        def _():
            ys_ref[...] = y
    else:
        o_refs[0][...] = x2


def _ffn(x1, h2, mods, perm, wup_bf, conv_w, conv_b, wdn_bf, final_g, l, *, final):
    if final:
        out_specs = tuple(_x_specs(True))
        out_shape = (jax.ShapeDtypeStruct((N_CTX, D_MODEL), F32), jax.ShapeDtypeStruct((N_LAT, D_MODEL), F32))
    else:
        out_specs = pl.BlockSpec((TB, D_MODEL), lambda i: (i, 0))
        out_shape = jax.ShapeDtypeStruct((N_TOK, D_MODEL), F32)
    return pl.pallas_call(
        functools.partial(_ffn_kernel, final=final),
        grid=(N_TOK // TB,),
        in_specs=[
            pl.BlockSpec((TB, D_MODEL), lambda i: (i, 0)),
            pl.BlockSpec((TB, D_MODEL), lambda i: (i, 0)),
            _mod_spec(l), _const_spec((FF_PERM_ROWS, FF_PERM_ROWS)),
            _const_spec((D_MODEL, 2 * D_FF), single=True), _layer_spec((3, 2 * D_FF), l),
            _layer_spec((1, 2 * D_FF), l), _const_spec((D_FF, D_MODEL), single=True),
            _const_spec((1, D_MODEL)),
        ],
        out_specs=out_specs,
        out_shape=out_shape,
        scratch_shapes=[pltpu.VMEM((TB, D_MODEL), BF16), pltpu.VMEM((2, TB, 2 * FF_TILE), F32),
                        pltpu.VMEM((TB, D_FF), BF16)],
        compiler_params=pltpu.CompilerParams(vmem_limit_bytes=VMEM_LIMIT),
        name="conv_ffn",
    )(x1, h2, mods, perm, wup_bf, conv_w, conv_b, wdn_bf, final_g.reshape(1, D_MODEL))


def kernel(x_prompt, x_sample, state_s5_re, state_s5_im, state_hgrn, c, c_ctx, w_ada, b_ada, norm1_g,
           norm2_g, w_in, s5_lam_re, s5_lam_im, s5_log_dt, s5_b_re, s5_b_im, s5_c_re, s5_c_im, s5_d,
           s5_w_glu, hg_lb_logits, fn_w, gm_norm_g, gm_ws, gm_bs, grp_norm_g, w_out, ffn_w_up,
           ffn_conv_w, ffn_conv_b, ffn_w_down, final_norm_g):
    xs = (x_prompt.reshape(N_CTX, D_MODEL), x_sample.reshape(N_LAT, D_MODEL))
    cvecs = jnp.concatenate([c_ctx[None], c, jnp.zeros((MOD_ROWS - 1 - DEC_BATCH, D_MODEL), F32)], axis=0)

    lb_p = jax.nn.softmax(hg_lb_logits.astype(F32), axis=0)
    lbs = jnp.maximum(jnp.cumsum(lb_p, axis=0) - lb_p[0], 0.0)
    lbps = jnp.stack([jnp.log(lbs), jnp.log1p(-lbs), 1.0 - lbs], axis=2)

    hg_consts_np = _hgrn_consts()
    hg_consts = (jnp.asarray(hg_consts_np[0], BF16), jnp.asarray(hg_consts_np[1], F32),
                 jnp.asarray(hg_consts_np[2], F32))
    cs_np, dft_ctx_np = _dft_consts(SEQ)
    _, dft_lat_np = _dft_consts(DEC_SEQ)
    cs = jnp.asarray(cs_np, F32).astype(BF16)
    dft_ctx = jnp.asarray(dft_ctx_np, F32).astype(BF16)
    dft_lat = jnp.asarray(dft_lat_np, F32).astype(BF16)
    gm_hm = jnp.asarray(np.kron(np.eye(GM_HEADS), np.ones((GM_CHUNK, GROUP_W // GM_HEADS))), BF16)
    bdm = jnp.asarray(np.kron(np.eye(HG_HEADS), np.ones((HG_DK, HG_DK))) / HG_DK, BF16)

    tables = _s5_tables(s5_lam_re, s5_lam_im, s5_log_dt, s5_b_re, s5_b_im, s5_c_re, s5_c_im)
    mods, s5_m, s5_f, s5_e = _prep(cvecs, w_ada, b_ada, tables, jnp.asarray(_s5_shift_consts(), BF16))
    mods = mods.reshape(DEPTH, MOD_ROWS, 1, N_MOD * D_MODEL)
    s5_al = tables[-1]
    s5_h0 = jnp.stack([state_s5_re, state_s5_im]).astype(F32).transpose(2, 3, 0, 1, 4, 5)
    s5_h0 = s5_h0.reshape(DEPTH, 2, 2, DEC_BATCH, S5_SW)
    hg_s0 = jnp.einsum('bldhkv,hg->lbdhvgk', state_hgrn.astype(F32), jnp.eye(HG_HEADS, dtype=F32))
    hg_s0 = hg_s0.reshape(DEPTH, DEC_BATCH, 2, HG_W, HG_W)

    def rows(a):
        return a.reshape(DEPTH, 1, a.shape[-1])

    norm1_r, norm2_r, gn_r, s5_d_r, gmn_r, cb_r = (rows(a) for a in (
        norm1_g, norm2_g, grp_norm_g, s5_d, gm_norm_g, ffn_conv_b))
    gm_w = gm_ws.transpose(0, 2, 1, 3).reshape(DEPTH, GM_CHUNK, GM_HEADS * GM_CHUNK).astype(BF16)
    gm_b = jnp.repeat(gm_bs.transpose(0, 2, 1), GROUP_W // GM_HEADS, axis=2)
    ff_perm = jnp.asarray(_ffn_perm_consts(), BF16)

    new_re, new_im, new_hg = [], [], []
    for l in range(DEPTH):
        hz, hqv, op_in, xc, xa_lo, xa_hi = _inproj(xs, mods, norm1_r, w_in, l)

        ylo, yhi, hfin = _s5(xa_lo, xa_hi, s5_m, s5_f, s5_e, s5_al, s5_h0, l)
        hfin = hfin.reshape(2, 2, BATCH, S5_G, S5_N).transpose(1, 2, 0, 3, 4)
        new_re.append(hfin[0])
        new_im.append(hfin[1])

        of, ob, sfin, wup_bf, wdn_bf = _hgrn(hz, hqv, lbps, hg_consts, hg_s0, ffn_w_up, ffn_w_down, l)
        new_hg.append(sfin)

        yc = (_fnet(xc, cs, dft_ctx, fn_w, l, n_seq=BATCH, seq_len=SEQ, tok0=0, spg=FN_CTX_SPG, tr=SEQ),
              _fnet(xc, cs, dft_lat, fn_w, l, n_seq=DEC_BATCH, seq_len=DEC_SEQ, tok0=N_CTX, spg=1,
                    tr=FN_LAT_TR))

        x1, h2 = _outproj(xs, op_in, ylo, yhi, of, ob, yc, mods, s5_d_r, s5_w_glu, gn_r, bdm, gmn_r, gm_w, gm_b,
                          gm_hm, w_out, norm2_r, l)
        res = _ffn(x1, h2, mods, ff_perm, wup_bf, ffn_conv_w, cb_r, wdn_bf, final_norm_g, l,
                   final=(l == DEPTH - 1))
        xs = (res,)

    y_prompt = res[0].reshape(BATCH, SEQ, D_MODEL)
    y_sample = res[1].reshape(DEC_BATCH, DEC_SEQ, D_MODEL)
    return (y_prompt, y_sample, jnp.stack(new_re, axis=1), jnp.stack(new_im, axis=1),
            jnp.stack(new_hg, axis=1))
```

```python
import functools
import math

import numpy as np
import jax
import jax.numpy as jnp
from jax import lax
from jax.experimental import pallas as pl
from jax.experimental.pallas import tpu as pltpu

D_MODEL = 1024
BATCH = 16
SEQ = 256
DEPTH = 2
DEC_BATCH = 2
DEC_SEQ = 2048
GRID_W = 64
GROUP_W = 256
S5_P = 16
S5_G = 16
S5_N = 64
HG_HEADS = 4
HG_DK = 64
GM_HEADS = 4
GM_CHUNK = 128
D_FF = 2816
N_MOD = 6
D_IN = 9 * GROUP_W
EPS = 1e-6
LAM_RE_MAX = -1e-4

N_CTX = BATCH * SEQ
N_LAT = DEC_BATCH * DEC_SEQ
N_TOK = N_CTX + N_LAT
TB = 512
TBP = 512
NB_CTX = N_CTX // TB
S5_L = 16
HG_L = 64
FF_TILE = 256
LANES = 128
MOD_ROWS = 8
V7X_VMEM_BYTES = 64 * 1024 * 1024
VMEM_LIMIT = V7X_VMEM_BYTES - 8 * 1024 * 1024

F32 = jnp.float32
BF16 = jnp.bfloat16

COL_XA, COL_HQ, COL_HF_FWD, COL_HF_BWD, COL_HI, COL_HGATE, COL_XC, COL_GU, COL_GV = range(9)
HG_Z_W = 2 * GROUP_W
HG_QV_W = 2 * GROUP_W
OP_IN_W = 4 * GROUP_W


def _mod_row(i, tb):
    return jnp.where(i < N_CTX // tb, 0, 1 + (i - N_CTX // tb) // (DEC_SEQ // tb))


GELU_C0 = 0.7978845608028654
GELU_C1 = GELU_C0 * 0.044715


def _gelu(x):
    return x * (0.5 + 0.5 * jnp.tanh(x * (GELU_C0 + GELU_C1 * (x * x))))


def _rms(x, g):
    return x * lax.rsqrt(jnp.mean(x * x, axis=-1, keepdims=True) + EPS) * g


def _dot(a, b):
    return jnp.dot(a, b, preferred_element_type=F32)


def _dot_nt(a, b, precision=None):
    return lax.dot_general(a, b, (((1,), (1,)), ((), ())), precision=precision, preferred_element_type=F32)


def _dot_tn(a, b):
    return lax.dot_general(a, b, (((0,), (0,)), ((), ())), preferred_element_type=F32)


def _const_spec(shape, single=False):
    kw = {"pipeline_mode": pl.Buffered(1)} if single else {}
    return pl.BlockSpec(shape, lambda *_: (0,) * len(shape), **kw)


def _layer_spec(shape, l, single=False):
    kw = {"pipeline_mode": pl.Buffered(1)} if single else {}
    return pl.BlockSpec((None,) + tuple(shape), lambda *_: (l,) + (0,) * len(shape), **kw)


def _mod_spec(l, tb=TB):
    return pl.BlockSpec((None, None, 1, N_MOD * D_MODEL), lambda i: (l, _mod_row(i, tb), 0, 0))


def _x_specs(split, width=D_MODEL, tb=TB):
    nb_ctx = N_CTX // tb
    if split:
        return [pl.BlockSpec((tb, width), lambda i: (jnp.minimum(i, nb_ctx - 1), 0)),
                pl.BlockSpec((tb, width), lambda i: (jnp.maximum(i - nb_ctx, 0), 0))]
    return [pl.BlockSpec((tb, width), lambda i: (i, 0))]


def _pick(refs, rows, is_ctx):
    if len(refs) == 1:
        return refs[0][rows, :]
    return jnp.where(is_ctx, refs[0][rows, :], refs[1][rows, :])


INPROJ_SUB = 256
INPROJ_OUT = ((HG_Z_W, F32), (HG_QV_W, BF16), (OP_IN_W, BF16), (GROUP_W, BF16), (LANES, F32), (LANES, F32))


def _inproj_kernel(*refs):
    *x_refs, mod_ref, g_ref, w_ref, hz_ref, hqv_ref, op_ref, xc_ref, xalo_ref, xahi_ref, wb_ref = refs

    @pl.when(pl.program_id(0) == 0)
    def _():
        wb_ref[...] = w_ref[...].astype(BF16)

    mod = mod_ref[...]
    sh = mod[:, 0:D_MODEL]
    sc = mod[:, D_MODEL:2 * D_MODEL]
    is_ctx = pl.program_id(0) < N_CTX // TBP
    for r in range(0, TBP, INPROJ_SUB):
        rows = slice(r, r + INPROJ_SUB)
        h = _rms(_pick(x_refs, rows, is_ctx), g_ref[...]) * (1.0 + sc) + sh
        o = _dot(h.astype(BF16), wb_ref[...])
        col = lambda c, n=1: o[:, GROUP_W * c:GROUP_W * (c + n)]
        hz_ref[rows, :] = col(COL_HF_FWD, 2)
        hqv_ref[rows, :] = jnp.concatenate([col(COL_HQ), col(COL_HI)], axis=1).astype(BF16)
        op_ref[rows, :] = jnp.concatenate([col(COL_XA), col(COL_HGATE), col(COL_GU, 2)], axis=1).astype(BF16)
        xc_ref[rows, :] = col(COL_XC).astype(BF16)
        xalo_ref[rows, :] = o[:, 0:LANES]
        xahi_ref[rows, :] = o[:, LANES:2 * LANES]


def _inproj(xs, mods, norm_g, w_in, l):
    return pl.pallas_call(
        _inproj_kernel,
        grid=(N_TOK // TBP,),
        in_specs=_x_specs(len(xs) == 2, tb=TBP) + [
            _mod_spec(l, TBP),
            _layer_spec((1, D_MODEL), l),
            _layer_spec((D_MODEL, D_IN), l, single=True),
        ],
        out_specs=tuple(pl.BlockSpec((TBP, w), lambda i: (i, 0)) for w, _ in INPROJ_OUT),
        out_shape=tuple(jax.ShapeDtypeStruct((N_TOK, w), dt) for w, dt in INPROJ_OUT),
        scratch_shapes=[pltpu.VMEM((D_MODEL, D_IN), BF16)],
        compiler_params=pltpu.CompilerParams(vmem_limit_bytes=VMEM_LIMIT),
        name="in_proj",
    )(*xs, mods, norm_g, w_in)


S5_NPAIR = S5_G // 2
S5_CW = S5_L * S5_P
S5_SW = S5_G * S5_N
S5_STEP_ROWS = N_CTX // S5_L
S5_CTX_NC = SEQ // S5_L
S5_LAT_NC = DEC_SEQ // S5_L


def _s5_slot(g, t):
    return (t + g) % S5_L


def _s5_tables(lam_re, lam_im, log_dt, b_re, b_im, c_re, c_im):
    lr = jnp.minimum(lam_re.astype(F32), LAM_RE_MAX)
    li = lam_im.astype(F32)
    dt = jnp.exp(log_dt.astype(F32))[..., None]
    mag = jnp.exp(lr * dt)
    ang = li * dt
    ab_re = mag * jnp.cos(ang)
    ab_im = mag * jnp.sin(ang)
    den = lr * lr + li * li
    xr = ab_re - 1.0
    z_re = (xr * lr + ab_im * li) / den
    z_im = (ab_im * lr - xr * li) / den
    bb_re = z_re[..., None] * b_re - z_im[..., None] * b_im
    bb_im = z_re[..., None] * b_im + z_im[..., None] * b_re
    tau = jnp.arange(S5_L + 1, dtype=F32)[:, None, None, None, None]
    pm = jnp.exp(lr * dt * tau)
    pa = li * dt * tau
    pw_re = pm * jnp.cos(pa)
    pw_im = pm * jnp.sin(pa)
    eye2 = jnp.eye(2, dtype=F32)

    def pw_pairs(a):
        a = a.transpose(1, 2, 3, 0, 4).reshape(DEPTH, 2, S5_NPAIR, 2, S5_L + 1, S5_N)
        return a.transpose(0, 1, 2, 4, 3, 5).reshape(DEPTH, 2, S5_NPAIR, S5_L + 1, 2 * S5_N)

    def mat_pairs(a):
        a = a.reshape(DEPTH, 2, S5_NPAIR, 2, S5_P, S5_N)
        return jnp.einsum('ldjaqn,ab->ldjaqbn', a, eye2).reshape(DEPTH, 2, S5_NPAIR, 2, S5_P, 2 * S5_N)

    al = jnp.stack([pw_re[S5_L], pw_im[S5_L]], axis=2).reshape(DEPTH, 2, 2, 1, S5_SW)
    return (pw_pairs(pw_re), pw_pairs(pw_im),
            mat_pairs(bb_re.transpose(0, 1, 2, 4, 3)), mat_pairs(bb_im.transpose(0, 1, 2, 4, 3)),
            mat_pairs(c_re.astype(F32)), mat_pairs(c_im.astype(F32)), al)


def _s5_shift_consts():
    n = S5_CW
    r = np.arange(n)[:, None]
    c = np.arange(n)[None, :]
    fwd = [(c == r + S5_P * s) for s in range(S5_L)]
    bwd = [(c == r - S5_P * (S5_L - 1 - s)) for s in range(S5_L)]
    return np.stack([np.concatenate(fwd, axis=1), np.concatenate(bwd, axis=1)]).astype(np.float32)


PREP_STEPS = 4
ADA_TN = N_MOD * D_MODEL // PREP_STEPS


def _ada_copy(wada_hbm, buf_ref, sem_ref, s):
    col = pl.multiple_of((s % PREP_STEPS) * ADA_TN, LANES)
    return pltpu.make_async_copy(wada_hbm.at[s // PREP_STEPS, :, pl.ds(col, ADA_TN)], buf_ref.at[s % 2],
                                 sem_ref.at[s % 2])


def _prep_kernel(c_ref, wada_hbm, bada_ref, pwr_ref, pwi_ref, br_ref, bi_ref, cr_ref, ci_ref, scat_ref,
                 mod_ref, m_ref, f_ref, e_ref, k_ref, wbuf_ref, wsem_ref):
    step = pl.program_id(1)
    s = pl.program_id(0) * PREP_STEPS + step
    n_steps = DEPTH * PREP_STEPS

    @pl.when(s == 0)
    def _():
        _ada_copy(wada_hbm, wbuf_ref, wsem_ref, s).start()

    @pl.when(s + 1 < n_steps)
    def _():
        _ada_copy(wada_hbm, wbuf_ref, wsem_ref, s + 1).start()

    for qq in range(PREP_STEPS):
        pl.when(step == qq)(functools.partial(_s5_pair_ops, qq, pwr_ref, pwi_ref, br_ref, bi_ref, cr_ref,
                                              ci_ref, f_ref, e_ref, k_ref))
    pl.when(step == PREP_STEPS - 1)(functools.partial(_s5_toeplitz, scat_ref, m_ref, k_ref))
    _ada_copy(wada_hbm, wbuf_ref, wsem_ref, s).wait()
    cv = c_ref[...]
    mod_ref[0] = _dot((cv * jax.nn.sigmoid(cv)).astype(BF16), wbuf_ref[s % 2].astype(BF16)) + bada_ref[0]


def _s5_pair_ops(qq, pwr_ref, pwi_ref, br_ref, bi_ref, cr_ref, ci_ref, f_ref, e_ref, k_ref):
    L = S5_L
    hp = lax.Precision.HIGHEST
    sw = 2 * S5_N
    per_step = S5_NPAIR // PREP_STEPS
    for j in range(per_step * qq, per_step * (qq + 1)):
        for d in range(2):
            pwr = pwr_ref[0, d, j]
            pwi = pwi_ref[0, d, j]
            for gl in range(2):
                g = 2 * j + gl
                br, bi = br_ref[0, d, j, gl], bi_ref[0, d, j, gl]
                cr, ci = cr_ref[0, d, j, gl], ci_ref[0, d, j, gl]
                ca_r, ca_i, f_r, f_i, e_r, e_i = [], [], [], [], [], []
                for t in range(L):
                    kk = t if d == 0 else L - 1 - t
                    pr, pi = pwr[kk:kk + 1], pwi[kk:kk + 1]
                    ca_r.append(cr * pr - ci * pi)
                    ca_i.append(cr * pi + ci * pr)
                    kf = L - 1 - t if d == 0 else t
                    pr, pi = pwr[kf:kf + 1], pwi[kf:kf + 1]
                    f_r.append(br * pr - bi * pi)
                    f_i.append(br * pi + bi * pr)
                    ke = t + 1 if d == 0 else L - t
                    pr, pi = pwr[ke:ke + 1], pwi[ke:ke + 1]
                    e_r.append(cr * pr - ci * pi)
                    e_i.append(-(cr * pi + ci * pr))
                cat = lambda xs: jnp.concatenate(xs, axis=0)
                k = _dot_nt(br, cat(ca_r), hp) - _dot_nt(bi, cat(ca_i), hp)
                k_ref[d, S5_P * g:S5_P * (g + 1), :] = k
                slots = lambda xs: cat([xs[(p - g) % L] for p in range(L)])
                rows = slice(S5_CW * gl, S5_CW * (gl + 1))
                f_ref[0, d, j, rows, 0:sw] = slots(f_r).astype(BF16)
                f_ref[0, d, j, rows, sw:2 * sw] = slots(f_i).astype(BF16)
                e_ref[0, d, 0, j, rows, :] = slots(e_r).astype(BF16)
                e_ref[0, d, 1, j, rows, :] = slots(e_i).astype(BF16)


def _s5_toeplitz(scat_ref, m_ref, k_ref):
    L = S5_L
    kf = k_ref[0].astype(BF16)
    kb = k_ref[1].astype(BF16)
    for s in range(L):
        cols = slice(S5_CW * s, S5_CW * (s + 1))
        res = _dot(kf, scat_ref[0, :, cols]) + _dot(kb, scat_ref[1, :, cols])
        for g in range(S5_G):
            blk = res[S5_P * g:S5_P * (g + 1), :]
            if g:
                blk = pltpu.roll(blk, S5_P * g, 1)
            r0 = S5_P * _s5_slot(g, s)
            m_ref[0, g, r0:r0 + S5_P, :] = blk.astype(BF16)


def _prep(cvecs, w_ada, b_ada, tables, scat):
    pwr, pwi, br, bi, cr, ci, _ = tables
    n = N_MOD * D_MODEL

    def lspec(shape):
        return pl.BlockSpec((1,) + shape, lambda l, q: (l,) + (0,) * len(shape))

    pw_shape = (2, S5_NPAIR, S5_L + 1, 2 * S5_N)
    mat_shape = (2, S5_NPAIR, 2, S5_P, 2 * S5_N)
    return pl.pallas_call(
        _prep_kernel,
        grid=(DEPTH, PREP_STEPS),
        in_specs=[pl.BlockSpec((MOD_ROWS, D_MODEL), lambda l, q: (0, 0)),
                  pl.BlockSpec(memory_space=pl.ANY),
                  pl.BlockSpec((1, 1, ADA_TN), lambda l, q: (l, 0, q)),
                  lspec(pw_shape), lspec(pw_shape), lspec(mat_shape), lspec(mat_shape), lspec(mat_shape),
                  lspec(mat_shape), _const_spec(scat.shape)],
        out_specs=(pl.BlockSpec((1, MOD_ROWS, ADA_TN), lambda l, q: (l, 0, q)),
                   lspec((S5_G, S5_CW, S5_CW)), lspec((2, S5_NPAIR, 2 * S5_CW, 4 * S5_N)),
                   lspec((2, 2, S5_NPAIR, 2 * S5_CW, 2 * S5_N))),
        out_shape=(jax.ShapeDtypeStruct((DEPTH, MOD_ROWS, n), F32),
                   jax.ShapeDtypeStruct((DEPTH, S5_G, S5_CW, S5_CW), BF16),
                   jax.ShapeDtypeStruct((DEPTH, 2, S5_NPAIR, 2 * S5_CW, 4 * S5_N), BF16),
                   jax.ShapeDtypeStruct((DEPTH, 2, 2, S5_NPAIR, 2 * S5_CW, 2 * S5_N), BF16)),
        scratch_shapes=[pltpu.VMEM((2, S5_G * S5_P, S5_CW), F32), pltpu.VMEM((2, D_MODEL, ADA_TN), F32),
                        pltpu.SemaphoreType.DMA((2,))],
        compiler_params=pltpu.CompilerParams(vmem_limit_bytes=VMEM_LIMIT,
                                             dimension_semantics=("arbitrary", "arbitrary")),
        name="param_prep",
    )(cvecs, w_ada, b_ada.reshape(DEPTH, 1, n), pwr, pwi, br, bi, cr, ci, scat)


def _s5_kernel(xlo_ref, xhi_ref, m_ref, f_ref, e_ref, al_ref, h0_ref, ylo_ref, yhi_ref, hfin_ref,
               uy_ref, zh_ref):
    R = S5_STEP_ROWS
    pw = 2 * S5_CW
    sw = 2 * S5_N
    gph = S5_G // 2
    step = pl.program_id(0)

    for half, x_ref in enumerate((xlo_ref, xhi_ref)):
        for t in range(S5_L):
            xt = x_ref[pl.ds(t, R, stride=S5_L), :]
            shift = (S5_P * t) % LANES
            xr = pltpu.roll(xt, shift, 1) if shift else xt
            for gl in range(gph):
                g = gph * half + gl
                src = (S5_P * gl + shift) % LANES
                lo = S5_CW * g + S5_P * _s5_slot(g, t)
                assert lo % LANES == src
                uy_ref[:, lo:lo + S5_P] = xr[:, src:src + S5_P]

    for j in range(S5_NPAIR):
        ub = uy_ref[:, pw * j:pw * (j + 1)].astype(BF16)
        for d in range(2):
            o = _dot(ub, f_ref[d, j])
            zh_ref[d, 0, :, sw * j:sw * (j + 1)] = o[:, :sw]
            zh_ref[d, 1, :, sw * j:sw * (j + 1)] = o[:, sw:]

    def scan_group(seq_rows, nc, init, fin_rows):
        chains = [(d, r0) for d in range(2) for r0 in seq_rows]

        def body(c, carry):
            out = []
            for k, (d, r0) in enumerate(chains):
                hr, hi = carry[2 * k], carry[2 * k + 1]
                r = r0 + (c if d == 0 else nc - 1 - c)
                zr = zh_ref[d, 0, pl.ds(r, 1), :]
                zi = zh_ref[d, 1, pl.ds(r, 1), :]
                zh_ref[d, 0, pl.ds(r, 1), :] = hr
                zh_ref[d, 1, pl.ds(r, 1), :] = hi
                ar = al_ref[d, 0]
                ai = al_ref[d, 1]
                out.append(ar * hr - ai * hi + zr)
                out.append(ar * hi + ai * hr + zi)
            return tuple(out)

        fin = lax.fori_loop(0, nc, body, tuple(init))
        if fin_rows is not None:
            for k, (d, _) in enumerate(chains):
                s = fin_rows[k % len(seq_rows)]
                hfin_ref[d, 0, s:s + 1, :] = fin[2 * k]
                hfin_ref[d, 1, s:s + 1, :] = fin[2 * k + 1]

    @pl.when(step == 0)
    def _():
        zero = jnp.zeros((1, S5_SW), F32)
        for s0 in range(0, BATCH, 4):
            seqs = list(range(s0, s0 + 4))
            scan_group([s * S5_CTX_NC for s in seqs], S5_CTX_NC, [zero] * 16, seqs)

    @pl.when(step == 1)
    def _():
        init = []
        for d in range(2):
            for s in range(DEC_BATCH):
                init.append(h0_ref[d, 0, s:s + 1, :])
                init.append(h0_ref[d, 1, s:s + 1, :])
        scan_group([s * S5_LAT_NC for s in range(DEC_BATCH)], S5_LAT_NC, init, None)

    for j in range(S5_NPAIR):
        acc = None
        for d in range(2):
            for ri in range(2):
                hb = zh_ref[d, ri, :, sw * j:sw * (j + 1)].astype(BF16)
                t = _dot_nt(hb, e_ref[d, ri, j])
                acc = t if acc is None else acc + t
        for gl in range(2):
            g = 2 * j + gl
            ug = uy_ref[:, S5_CW * g:S5_CW * (g + 1)].astype(BF16)
            uy_ref[:, S5_CW * g:S5_CW * (g + 1)] = acc[:, S5_CW * gl:S5_CW * (gl + 1)] + _dot(ug, m_ref[g])

    lane = lax.broadcasted_iota(jnp.int32, (R, LANES), 1)
    in_piece = [jnp.logical_and(lane >= S5_P * k, lane < S5_P * (k + 1)) for k in range(LANES // S5_P)]
    for half, y_ref in enumerate((ylo_ref, yhi_ref)):
        for t in range(S5_L):
            merged = None
            for gl in range(gph):
                g = gph * half + gl
                slot = _s5_slot(g, t)
                col = S5_CW * g + LANES * (S5_P * slot // LANES)
                v = uy_ref[:, col:col + LANES]
                merged = v if merged is None else jnp.where(in_piece[slot % (LANES // S5_P)], v, merged)
            shift = (LANES - S5_P * t % LANES) % LANES
            y_ref[pl.ds(t, R, stride=S5_L), :] = pltpu.roll(merged, shift, 1) if shift else merged


def _s5(xa_lo, xa_hi, m, f2, e2, al, h0, l):
    half = pl.BlockSpec((N_CTX, LANES), lambda s: (s, 0))
    return pl.pallas_call(
        _s5_kernel,
        grid=(2,),
        in_specs=[half, half,
                  _layer_spec(m.shape[1:], l, single=True), _layer_spec(f2.shape[1:], l, single=True),
                  _layer_spec(e2.shape[1:], l, single=True), _layer_spec(al.shape[1:], l),
                  _layer_spec(h0.shape[1:], l)],
        out_specs=(half, half, _const_spec((2, 2, BATCH, S5_SW))),
        out_shape=(jax.ShapeDtypeStruct((N_TOK, LANES), F32),
                   jax.ShapeDtypeStruct((N_TOK, LANES), F32),
                   jax.ShapeDtypeStruct((2, 2, BATCH, S5_SW), F32)),
        scratch_shapes=[pltpu.VMEM((S5_STEP_ROWS, S5_G * S5_CW), F32),
                        pltpu.VMEM((2, 2, S5_STEP_ROWS, S5_SW), F32)],
        compiler_params=pltpu.CompilerParams(vmem_limit_bytes=VMEM_LIMIT),
        name="s5_mixer",
    )(xa_lo, xa_hi, m, f2, e2, al, h0)


HG_NLEV = int(math.log2(HG_L))
HG_W = HG_HEADS * HG_DK
HG_CPS = 8
HG_BLK = HG_CPS * HG_L
HG_NB_CTX = N_CTX // HG_BLK
HG_SPB = HG_BLK // SEQ
HG_CPQ = SEQ // HG_L
HG_NC_LAT = DEC_SEQ // HG_BLK
assert HG_SPB >= 1 and HG_SPB * SEQ == HG_BLK
assert HG_L == HG_DK


def _hgrn_consts():
    L = HG_L
    w = np.zeros((HG_NLEV + 2, L, L), np.float32)
    mask = np.zeros((HG_NLEV + 1, L, L), np.float32)
    for lev in range(HG_NLEV):
        blk = L >> lev
        half = blk // 2
        for t in range(L):
            p, o = divmod(t, blk)
            bd = p * blk + half - 1
            if o >= half:
                w[lev, t, bd + 1:t + 1] = 1.0
            else:
                w[lev, t, t + 1:bd + 1] = 1.0
        jj, ii = np.meshgrid(np.arange(L), np.arange(L), indexing='ij')
        mask[lev] = ((jj // blk == ii // blk) & (jj % blk >= half) & (ii % blk < half)).astype(np.float32)
    mask[HG_NLEV] = np.eye(L, dtype=np.float32)
    for t in range(L):
        w[HG_NLEV, t, :t + 1] = 1.0
        w[HG_NLEV + 1, t, t + 1:] = 1.0
    out = []
    for wd, md in ((w, mask), (w[:, ::-1, ::-1], mask[:, ::-1, ::-1])):
        wflat = wd.reshape((HG_NLEV + 2) * L, L)
        out.append((np.concatenate([wflat] * 3, axis=1), np.tile(md, (1, 1, HG_HEADS))))
    wcat = np.stack([out[0][0], out[1][0]])
    mask4 = np.stack([out[0][1], out[1][1]])
    hm = np.kron(np.eye(HG_HEADS, dtype=np.float32), np.ones((HG_DK, HG_DK), np.float32))
    return wcat, mask4, hm


def _hg_pos(i):
    is_ctx = i < HG_NB_CTX
    c = jnp.where(is_ctx, 0, (i - HG_NB_CTX) % HG_NC_LAT)
    nc = jnp.where(is_ctx, 1, HG_NC_LAT)
    return is_ctx, c, nc


def _hg_bwd_blk(i):
    _, c, nc = _hg_pos(i)
    return i + nc - 1 - 2 * c


def _hg_local(items, lb_ref, wcat_ref, mask_ref, hm):
    L = HG_L
    hmb = hm.astype(BF16)

    def bd4(x):
        return jnp.concatenate([x] * HG_HEADS, axis=0) * hmb

    kks, exs = [], []
    for d, q, z, v in items:
        lbp = lb_ref[d]
        sp = jnp.log1p(jnp.exp(-jnp.abs(z)))
        ls = jnp.minimum(z, 0.0) - sp
        kks.append(lbp[2:3] * jnp.exp(jnp.minimum(-z, 0.0) - sp))
        a = lbp[0:1]
        b = lbp[1:2] + ls
        logf = jnp.maximum(a, b) + jnp.log1p(jnp.exp(-jnp.abs(a - b)))
        p0 = logf.astype(BF16)
        r0 = logf - p0.astype(F32)
        p1 = r0.astype(BF16)
        p2 = (r0 - p1.astype(F32)).astype(BF16)
        exs.append(jnp.exp(_dot(wcat_ref[d], jnp.concatenate([p0, p1, p2], axis=0))))
    scs = [None] * len(items)
    for lev in range(HG_NLEV + 1):
        for n, (d, q, z, v) in enumerate(items):
            if lev < HG_NLEV:
                al = exs[n][L * lev:L * (lev + 1)]
                lhs = (q * al).astype(BF16)
                rhs = (kks[n] * al).astype(BF16)
            else:
                lhs = q.astype(BF16)
                rhs = kks[n].astype(BF16)
            t = _dot_nt(lhs, bd4(rhs)) * mask_ref[d, lev]
            scs[n] = t if scs[n] is None else scs[n] + t
    out = []
    for n, (d, q, z, v) in enumerate(items):
        vb = v.astype(BF16)
        o = _dot(scs[n].astype(BF16), bd4(vb))
        eq = exs[n][L * HG_NLEV:L * (HG_NLEV + 1)]
        qe = (q * eq).astype(BF16)
        ke = (kks[n] * exs[n][L * (HG_NLEV + 1):L * (HG_NLEV + 2)]).astype(BF16)
        g = eq[L - 1:L] if d == 0 else eq[0:1]
        out.append((o, qe, g, _dot_tn(vb, ke) * hm))
    return out


def _hgrn_kernel(zf_ref, qvf_ref, zb_ref, qvb_ref, lb_ref, w_ref, mask_ref, hm_ref, s0_ref, wup_ref, wdn_ref,
                 of_ref, ob_ref, sfin_ref, wupb_ref, wdnb_ref, st_ref):
    is_ctx, c, nc = _hg_pos(pl.program_id(0))
    wupb_ref[...] = wup_ref[...].astype(BF16)
    wdnb_ref[...] = wdn_ref[...].astype(BF16)

    hm = hm_ref[...]
    rows = [slice(HG_L * k, HG_L * (k + 1)) for k in range(HG_CPS)]
    items = [(d, qv_ref[r, 0:GROUP_W], z_ref[r, :], qv_ref[r, GROUP_W:2 * GROUP_W])
             for d, (z_ref, qv_ref) in enumerate(((zf_ref, qvf_ref), (zb_ref, qvb_ref))) for r in rows]
    loc = _hg_local(items, lb_ref, w_ref, mask_ref, hm)

    def chain(d, o_ref, st, chunks):
        for k in (chunks if d == 0 else reversed(chunks)):
            o, qe, g, upd = loc[HG_CPS * d + k]
            o_ref[rows[k], :] = (o + _dot_nt(qe, st.astype(BF16))).astype(BF16)
            st = st * g + upd
        return st

    @pl.when(is_ctx)
    def _():
        for sq in range(HG_SPB):
            for d, o_ref in enumerate((of_ref, ob_ref)):
                st = chain(d, o_ref, jnp.zeros((HG_W, HG_W), F32), range(HG_CPQ * sq, HG_CPQ * (sq + 1)))
                s_kv = st.T
                for h in range(HG_HEADS):
                    sfin_ref[sq, d, h] = s_kv[HG_DK * h:HG_DK * (h + 1), HG_DK * h:HG_DK * (h + 1)]

    @pl.when(jnp.logical_not(is_ctx))
    def _():
        @pl.when(c == 0)
        def _():
            st_ref[...] = s0_ref[0]

        for d, o_ref in enumerate((of_ref, ob_ref)):
            st_ref[d] = chain(d, o_ref, st_ref[d], range(HG_CPS))


def _hgrn(hz, hqv, lbp, consts, s0, w_up, w_down, l):
    wcat, mask4, hm = consts
    steps = N_TOK // HG_BLK
    up_rows = D_MODEL // steps
    dn_rows = 2 * D_FF // steps

    return pl.pallas_call(
        _hgrn_kernel,
        grid=(N_TOK // HG_BLK,),
        in_specs=[pl.BlockSpec((HG_BLK, GROUP_W), lambda i: (i, 0)),
                  pl.BlockSpec((HG_BLK, HG_QV_W), lambda i: (i, 0)),
                  pl.BlockSpec((HG_BLK, GROUP_W), lambda i: (_hg_bwd_blk(i), 1)),
                  pl.BlockSpec((HG_BLK, HG_QV_W), lambda i: (_hg_bwd_blk(i), 0)),
                  _layer_spec(lbp.shape[1:], l), _const_spec(wcat.shape), _const_spec(mask4.shape),
                  _const_spec(hm.shape),
                  pl.BlockSpec((None, 1, 2, HG_W, HG_W),
                               lambda i: (l, jnp.maximum(i - HG_NB_CTX, 0) // HG_NC_LAT, 0, 0, 0)),
                  pl.BlockSpec((None, up_rows, 2 * D_FF), lambda i: (l, i, 0)),
                  pl.BlockSpec((None, dn_rows, D_MODEL), lambda i: (l, i // 2, 0))],
        out_specs=(pl.BlockSpec((HG_BLK, HG_W), lambda i: (i, 0)),
                   pl.BlockSpec((HG_BLK, HG_W), lambda i: (_hg_bwd_blk(i), 0)),
                   pl.BlockSpec((HG_SPB, 2, HG_HEADS, HG_DK, HG_DK),
                                lambda i: (jnp.minimum(i, HG_NB_CTX - 1), 0, 0, 0, 0)),
                   pl.BlockSpec((up_rows, 2 * D_FF), lambda i: (i, 0)),
                   pl.BlockSpec((dn_rows, D_MODEL), lambda i: (i // 2, 0))),
        out_shape=(jax.ShapeDtypeStruct((N_TOK, HG_W), BF16),
                   jax.ShapeDtypeStruct((N_TOK, HG_W), BF16),
                   jax.ShapeDtypeStruct((BATCH, 2, HG_HEADS, HG_DK, HG_DK), F32),
                   jax.ShapeDtypeStruct((D_MODEL, 2 * D_FF), BF16),
                   jax.ShapeDtypeStruct((D_FF, D_MODEL), BF16)),
        scratch_shapes=[pltpu.VMEM((2, HG_W, HG_W), F32)],
        compiler_params=pltpu.CompilerParams(vmem_limit_bytes=VMEM_LIMIT),
        name="hgrn_mixer",
    )(hz, hqv, hz, hqv, lbp, wcat, mask4, hm, s0, w_up, w_down)


FN_CTX_SPG = 4
FN_LAT_TR = 512


def _dft_consts(t_len):
    n = GROUP_W // 4
    k = np.arange(n)
    ang = 2.0 * np.pi * ((k[:, None] * k[None, :]) % n) / n
    eye = np.eye(4)
    cs = np.concatenate([np.kron(eye, np.cos(ang)), np.kron(eye, np.sin(ang))], axis=1) / math.sqrt(n)
    t = np.arange(t_len)
    angt = 2.0 * np.pi * ((t[:, None] * t[None, :]) % t_len) / t_len
    dft = np.concatenate([np.cos(angt), -np.sin(angt)], axis=1) / math.sqrt(t_len)
    return cs.astype(np.float32), dft.astype(np.float32)


def _fnet_kernel(x_ref, cs_ref, dft_ref, w_ref, o_ref, r_ref, *, t_len, spg, tr):
    g = pl.program_id(1)

    @pl.when(pl.program_id(0) == 0)
    def _():
        t = _dot(x_ref[...].astype(BF16), cs_ref[...])
        for s in range(spg):
            rows = slice(t_len * s, t_len * (s + 1))
            r_ref[spg * g + s, 0:t_len, :] = t[rows, :GROUP_W].astype(BF16)
            r_ref[spg * g + s, t_len:2 * t_len, :] = t[rows, GROUP_W:].astype(BF16)

    w = w_ref[...].astype(BF16)
    for s in range(spg):
        y = _dot(dft_ref[...], r_ref[spg * g + s])
        o_ref[tr * s:tr * (s + 1), :] = _dot(y.astype(BF16), w).astype(BF16)


def _fnet(xc, cs, dft, fn_w, l, *, n_seq, seq_len, tok0, spg, tr):
    nj = seq_len // tr
    assert spg == 1 or nj == 1
    gb0 = tok0 // (spg * seq_len)
    return pl.pallas_call(
        functools.partial(_fnet_kernel, t_len=seq_len, spg=spg, tr=tr),
        grid=(nj, n_seq // spg),
        in_specs=[
            pl.BlockSpec((spg * seq_len, GROUP_W), lambda j, g: (gb0 + g, 0)),
            _const_spec((GROUP_W, 2 * GROUP_W)),
            pl.BlockSpec((tr, 2 * seq_len), lambda j, g: (j, 0)),
            _layer_spec((GROUP_W, GROUP_W), l),
        ],
        out_specs=pl.BlockSpec((spg * tr, GROUP_W), lambda j, g: (g * nj + j, 0)),
        out_shape=jax.ShapeDtypeStruct((n_seq * seq_len, GROUP_W), BF16),
        scratch_shapes=[pltpu.VMEM((n_seq, 2 * seq_len, GROUP_W), BF16)],
        compiler_params=pltpu.CompilerParams(vmem_limit_bytes=VMEM_LIMIT),
        name="fnet_mixer",
    )(xc, cs, dft, fn_w)


OUTPROJ_SUB = 256


def _outproj_kernel(*refs):
    (op_ref, ylo_ref, yhi_ref, of_ref, ob_ref, ycc_ref, ycl_ref, mod_ref, d_ref,
     wglu_ref, gn_ref, bdm_ref, gmn_ref, gmw_ref, gmb_ref, gmh_ref, wout_ref, n2_ref,
     x1_ref, h2_ref, wob_ref) = refs[-21:]
    x_refs = refs[:-21]
    xa_ref, hgate_ref, gu_ref, gv_ref = (op_ref.at[:, GROUP_W * k:GROUP_W * (k + 1)] for k in range(4))

    @pl.when(pl.program_id(0) == 0)
    def _():
        wob_ref[...] = wout_ref[...].astype(BF16)

    gn = gn_ref[...]
    gm_hm = gmh_ref[...]
    mod = mod_ref[...]
    g1 = mod[:, 2 * D_MODEL:3 * D_MODEL]
    sh2 = mod[:, 3 * D_MODEL:4 * D_MODEL]
    sc2 = mod[:, 4 * D_MODEL:5 * D_MODEL]
    is_ctx = pl.program_id(0) < N_CTX // TBP
    for r in range(0, TBP, OUTPROJ_SUB):
        rows = slice(r, r + OUTPROJ_SUB)
        ys = jnp.concatenate([ylo_ref[rows, :], yhi_ref[rows, :]], axis=1)
        y5 = _gelu(ys + d_ref[...] * xa_ref[rows, :].astype(F32))
        glu = jax.nn.sigmoid(_dot(y5.astype(BF16), wglu_ref[...].astype(BF16)))
        out_a = _rms(y5 * glu, gn[:, 0:GROUP_W])
        o = of_ref[rows, :].astype(F32) + ob_ref[rows, :].astype(F32)
        o2 = o * o
        o2h = o2.astype(BF16)
        o2l = (o2 - o2h.astype(F32)).astype(BF16)
        ms = _dot(o2h, bdm_ref[...]) + _dot(o2l, bdm_ref[...])
        hg = hgate_ref[rows, :].astype(F32)
        out_b = o * lax.rsqrt(ms + EPS) * gn[:, GROUP_W:2 * GROUP_W] * (hg * jax.nn.sigmoid(hg))
        out_c = _rms(_pick((ycc_ref, ycl_ref), rows, is_ctx).astype(F32), gn[:, 2 * GROUP_W:3 * GROUP_W])
        out_d = []
        for c in range(r // GM_CHUNK, (r + OUTPROJ_SUB) // GM_CHUNK):
            crow = slice(GM_CHUNK * c, GM_CHUNK * (c + 1))
            gv = _rms(_gelu(gv_ref[crow, :].astype(F32)), gmn_ref[...]).astype(BF16)
            g4 = jnp.concatenate([gv] * GM_HEADS, axis=0) * gm_hm
            sp = _dot(gmw_ref[...], g4) + gmb_ref[...]
            out_d.append(_rms(_gelu(gu_ref[crow, :].astype(F32)) * sp, gn[:, 3 * GROUP_W:]))
        out_d = jnp.concatenate(out_d, axis=0)
        m = None
        for k, part in enumerate((out_a, out_b, out_c, out_d)):
            t = _dot(part.astype(BF16), wob_ref[GROUP_W * k:GROUP_W * (k + 1), :])
            m = t if m is None else m + t
        x1 = _pick(x_refs, rows, is_ctx) + g1 * m
        x1_ref[rows, :] = x1
        h2_ref[rows, :] = (_rms(x1, n2_ref[...]) * (1.0 + sc2) + sh2).astype(BF16)


def _outproj(xs, op_in, ylo, yhi, of, ob, yc, mods, s5_d, wglu, gn, bdm, gm_norm_g, gm_w, gm_b, gm_hm, wout,
             norm2_g, l):
    def tok(width, col=0):
        return pl.BlockSpec((TBP, width), lambda i: (i, col))

    return pl.pallas_call(
        _outproj_kernel,
        grid=(N_TOK // TBP,),
        in_specs=_x_specs(len(xs) == 2, tb=TBP) + [
            tok(OP_IN_W), tok(LANES), tok(LANES), tok(GROUP_W), tok(GROUP_W), *_x_specs(True, GROUP_W, TBP),
            _mod_spec(l, TBP),
            _layer_spec((1, GROUP_W), l), _layer_spec((GROUP_W, GROUP_W), l), _layer_spec((1, D_MODEL), l),
            _const_spec((GROUP_W, GROUP_W)),
            _layer_spec((1, GROUP_W), l), _layer_spec((GM_CHUNK, GM_HEADS * GM_CHUNK), l),
            _layer_spec((GM_CHUNK, GROUP_W), l), _const_spec((GM_HEADS * GM_CHUNK, GROUP_W)),
            _layer_spec((D_MODEL, D_MODEL), l, single=True), _layer_spec((1, D_MODEL), l),
        ],
        out_specs=(tok(D_MODEL), tok(D_MODEL)),
        out_shape=(jax.ShapeDtypeStruct((N_TOK, D_MODEL), F32),
                   jax.ShapeDtypeStruct((N_TOK, D_MODEL), BF16)),
        scratch_shapes=[pltpu.VMEM((D_MODEL, D_MODEL), BF16)],
        compiler_params=pltpu.CompilerParams(vmem_limit_bytes=VMEM_LIMIT),
        name="out_proj",
    )(*xs, op_in, ylo, yhi, of, ob, *yc, mods, s5_d, wglu, gn, bdm, gm_norm_g, gm_w, gm_b,
      gm_hm, wout, norm2_g)


FF_SEG = GRID_W
FF_NSEG = TB // FF_SEG
FF_SEQ_STRIPS = SEQ // FF_SEG
FF_PERM_ROWS = FF_SEQ_STRIPS * FF_SEG
SUBLANES = 8
FF_NGRP = FF_SEG // SUBLANES


def _ffn_perm_consts():
    p = np.zeros((FF_PERM_ROWS, FF_PERM_ROWS), np.float32)
    for k in range(FF_SEQ_STRIPS):
        for t in range(FF_SEG):
            p[FF_SEG * k + SUBLANES * (t % FF_NGRP) + t // FF_NGRP, FF_SEG * k + t] = 1.0
    return p


def _ffn_kernel(x1_ref, h2_ref, mod_ref, perm_ref, wup_ref, cw_ref, cb_ref, wdn_ref, fg_ref, *rest, final):
    *o_refs, h2p_ref, z_ref, hid_ref = rest
    i = pl.program_id(0)
    joined = (i < NB_CTX).astype(F32)
    n_tiles = D_FF // FF_TILE
    sub = lax.broadcasted_iota(jnp.int32, (SUBLANES, 2 * FF_TILE), 0)
    is_first = sub == 0
    is_last = sub == SUBLANES - 1
    zero_grp = jnp.zeros((SUBLANES, 2 * FF_TILE), F32)
    for r in range(0, TB, FF_PERM_ROWS):
        h2p_ref[r:r + FF_PERM_ROWS, :] = _dot(perm_ref[...], h2_ref[r:r + FF_PERM_ROWS, :]).astype(BF16)

    def up(j):
        lo = FF_TILE * j
        for c, col in enumerate((lo, D_FF + lo)):
            z_ref[j % 2, :, FF_TILE * c:FF_TILE * (c + 1)] = _dot(h2p_ref[...], wup_ref[:, col:col + FF_TILE])

    def gate(j):
        slot = j % 2
        lo = FF_TILE * j
        w = jnp.concatenate([cw_ref[:, lo:lo + FF_TILE], cw_ref[:, D_FF + lo:D_FF + lo + FF_TILE]], axis=1)
        b = jnp.concatenate([cb_ref[:, lo:lo + FF_TILE], cb_ref[:, D_FF + lo:D_FF + lo + FF_TILE]], axis=1)

        def grp(k, v):
            r = FF_SEG * k + SUBLANES * v
            return z_ref[slot, r:r + SUBLANES, :]

        down = [pltpu.roll(grp(k, FF_NGRP - 1), 1, 0) for k in range(FF_NSEG)]
        up_ = [pltpu.roll(grp(k, 0), SUBLANES - 1, 0) for k in range(FF_NSEG)]
        strips = []
        for k in range(FF_NSEG):
            g = [grp(k, v) for v in range(FF_NGRP)]
            before = down[k - 1] * joined if k % FF_SEQ_STRIPS > 0 else zero_grp
            after = up_[k + 1] * joined if k % FF_SEQ_STRIPS < FF_SEQ_STRIPS - 1 else zero_grp
            zm1 = jnp.concatenate([jnp.where(is_first, before, down[k])] + g[:-1], axis=0)
            zp1 = jnp.concatenate(g[1:] + [jnp.where(is_last, after, up_[k])], axis=0)
            zc = b + zm1 * w[0:1] + jnp.concatenate(g, axis=0) * w[1:2] + zp1 * w[2:3]
            strips.append((_gelu(zc[:, :FF_TILE]) * zc[:, FF_TILE:]).astype(BF16))
        hid_ref[:, lo:lo + FF_TILE] = jnp.concatenate(strips, axis=0)

    up(0)
    for j in range(n_tiles):
        if j + 1 < n_tiles:
            up(j + 1)
        gate(j)
    acc = _dot(hid_ref[...], wdn_ref[...])
    acc = jnp.swapaxes(acc.reshape(FF_NSEG, FF_NGRP, SUBLANES, D_MODEL), 1, 2).reshape(TB, D_MODEL)
    g2 = mod_ref[...][:, 5 * D_MODEL:6 * D_MODEL]
    x2 = x1_ref[...] + g2 * acc
    if final:
        y = _rms(x2, fg_ref[...])
        yp_ref, ys_ref = o_refs

        @pl.when(i < NB_CTX)
        def _():
            yp_ref[...] = y

        @pl.when(i >= NB_CTX)
        def _():
            ys_ref[...] = y
    else:
        o_refs[0][...] = x2


def _ffn(x1, h2, mods, perm, wup_bf, conv_w, conv_b, wdn_bf, final_g, l, *, final):
    if final:
        out_specs = tuple(_x_specs(True))
        out_shape = (jax.ShapeDtypeStruct((N_CTX, D_MODEL), F32), jax.ShapeDtypeStruct((N_LAT, D_MODEL), F32))
    else:
        out_specs = pl.BlockSpec((TB, D_MODEL), lambda i: (i, 0))
        out_shape = jax.ShapeDtypeStruct((N_TOK, D_MODEL), F32)
    return pl.pallas_call(
        functools.partial(_ffn_kernel, final=final),
        grid=(N_TOK // TB,),
        in_specs=[
            pl.BlockSpec((TB, D_MODEL), lambda i: (i, 0)),
            pl.BlockSpec((TB, D_MODEL), lambda i: (i, 0)),
            _mod_spec(l), _const_spec((FF_PERM_ROWS, FF_PERM_ROWS)),
            _const_spec((D_MODEL, 2 * D_FF), single=True), _layer_spec((3, 2 * D_FF), l),
            _layer_spec((1, 2 * D_FF), l), _const_spec((D_FF, D_MODEL), single=True),
            _const_spec((1, D_MODEL)),
        ],
        out_specs=out_specs,
        out_shape=out_shape,
        scratch_shapes=[pltpu.VMEM((TB, D_MODEL), BF16), pltpu.VMEM((2, TB, 2 * FF_TILE), F32),
                        pltpu.VMEM((TB, D_FF), BF16)],
        compiler_params=pltpu.CompilerParams(vmem_limit_bytes=VMEM_LIMIT),
        name="conv_ffn",
    )(x1, h2, mods, perm, wup_bf, conv_w, conv_b, wdn_bf, final_g.reshape(1, D_MODEL))


def kernel(x_prompt, x_sample, state_s5_re, state_s5_im, state_hgrn, c, c_ctx, w_ada, b_ada, norm1_g,
           norm2_g, w_in, s5_lam_re, s5_lam_im, s5_log_dt, s5_b_re, s5_b_im, s5_c_re, s5_c_im, s5_d,
           s5_w_glu, hg_lb_logits, fn_w, gm_norm_g, gm_ws, gm_bs, grp_norm_g, w_out, ffn_w_up,
           ffn_conv_w, ffn_conv_b, ffn_w_down, final_norm_g):
    xs = (x_prompt.reshape(N_CTX, D_MODEL), x_sample.reshape(N_LAT, D_MODEL))
    cvecs = jnp.concatenate([c_ctx[None], c, jnp.zeros((MOD_ROWS - 1 - DEC_BATCH, D_MODEL), F32)], axis=0)

    lb_p = jax.nn.softmax(hg_lb_logits.astype(F32), axis=0)
    lbs = jnp.maximum(jnp.cumsum(lb_p, axis=0) - lb_p[0], 0.0)
    lbps = jnp.stack([jnp.log(lbs), jnp.log1p(-lbs), 1.0 - lbs], axis=2)

    hg_consts_np = _hgrn_consts()
    hg_consts = (jnp.asarray(hg_consts_np[0], BF16), jnp.asarray(hg_consts_np[1], F32),
                 jnp.asarray(hg_consts_np[2], F32))
    cs_np, dft_ctx_np = _dft_consts(SEQ)
    _, dft_lat_np = _dft_consts(DEC_SEQ)
    cs = jnp.asarray(cs_np, F32).astype(BF16)
    dft_ctx = jnp.asarray(dft_ctx_np, F32).astype(BF16)
    dft_lat = jnp.asarray(dft_lat_np, F32).astype(BF16)
    gm_hm = jnp.asarray(np.kron(np.eye(GM_HEADS), np.ones((GM_CHUNK, GROUP_W // GM_HEADS))), BF16)
    bdm = jnp.asarray(np.kron(np.eye(HG_HEADS), np.ones((HG_DK, HG_DK))) / HG_DK, BF16)

    tables = _s5_tables(s5_lam_re, s5_lam_im, s5_log_dt, s5_b_re, s5_b_im, s5_c_re, s5_c_im)
    mods, s5_m, s5_f, s5_e = _prep(cvecs, w_ada, b_ada, tables, jnp.asarray(_s5_shift_consts(), BF16))
    mods = mods.reshape(DEPTH, MOD_ROWS, 1, N_MOD * D_MODEL)
    s5_al = tables[-1]
    s5_h0 = jnp.stack([state_s5_re, state_s5_im]).astype(F32).transpose(2, 3, 0, 1, 4, 5)
    s5_h0 = s5_h0.reshape(DEPTH, 2, 2, DEC_BATCH, S5_SW)
    hg_s0 = jnp.einsum('bldhkv,hg->lbdhvgk', state_hgrn.astype(F32), jnp.eye(HG_HEADS, dtype=F32))
    hg_s0 = hg_s0.reshape(DEPTH, DEC_BATCH, 2, HG_W, HG_W)

    def rows(a):
        return a.reshape(DEPTH, 1, a.shape[-1])

    norm1_r, norm2_r, gn_r, s5_d_r, gmn_r, cb_r = (rows(a) for a in (
        norm1_g, norm2_g, grp_norm_g, s5_d, gm_norm_g, ffn_conv_b))
    gm_w = gm_ws.transpose(0, 2, 1, 3).reshape(DEPTH, GM_CHUNK, GM_HEADS * GM_CHUNK).astype(BF16)
    gm_b = jnp.repeat(gm_bs.transpose(0, 2, 1), GROUP_W // GM_HEADS, axis=2)
    ff_perm = jnp.asarray(_ffn_perm_consts(), BF16)

    new_re, new_im, new_hg = [], [], []
    for l in range(DEPTH):
        hz, hqv, op_in, xc, xa_lo, xa_hi = _inproj(xs, mods, norm1_r, w_in, l)

        ylo, yhi, hfin = _s5(xa_lo, xa_hi, s5_m, s5_f, s5_e, s5_al, s5_h0, l)
        hfin = hfin.reshape(2, 2, BATCH, S5_G, S5_N).transpose(1, 2, 0, 3, 4)
        new_re.append(hfin[0])
        new_im.append(hfin[1])

        of, ob, sfin, wup_bf, wdn_bf = _hgrn(hz, hqv, lbps, hg_consts, hg_s0, ffn_w_up, ffn_w_down, l)
        new_hg.append(sfin)

        yc = (_fnet(xc, cs, dft_ctx, fn_w, l, n_seq=BATCH, seq_len=SEQ, tok0=0, spg=FN_CTX_SPG, tr=SEQ),
              _fnet(xc, cs, dft_lat, fn_w, l, n_seq=DEC_BATCH, seq_len=DEC_SEQ, tok0=N_CTX, spg=1,
                    tr=FN_LAT_TR))

        x1, h2 = _outproj(xs, op_in, ylo, yhi, of, ob, yc, mods, s5_d_r, s5_w_glu, gn_r, bdm, gmn_r, gm_w, gm_b,
                          gm_hm, w_out, norm2_r, l)
        res = _ffn(x1, h2, mods, ff_perm, wup_bf, ffn_conv_w, cb_r, wdn_bf, final_norm_g, l,
                   final=(l == DEPTH - 1))
        xs = (res,)

    y_prompt = res[0].reshape(BATCH, SEQ, D_MODEL)
    y_sample = res[1].reshape(DEC_BATCH, DEC_SEQ, D_MODEL)
    return (y_prompt, y_sample, jnp.stack(new_re, axis=1), jnp.stack(new_im, axis=1),
            jnp.stack(new_hg, axis=1))
```

```python
import functools
import math

import numpy as np
import jax
import jax.numpy as jnp
from jax import lax
from jax.experimental import pallas as pl
from jax.experimental.pallas import tpu as pltpu

D_MODEL = 1024
BATCH = 16
SEQ = 256
DEPTH = 2
DEC_BATCH = 2
DEC_SEQ = 2048
GRID_W = 64
GROUP_W = 256
S5_P = 16
S5_G = 16
S5_N = 64
HG_HEADS = 4
HG_DK = 64
GM_HEADS = 4
GM_CHUNK = 128
D_FF = 2816
N_MOD = 6
D_IN = 9 * GROUP_W
EPS = 1e-6
LAM_RE_MAX = -1e-4

N_CTX = BATCH * SEQ
N_LAT = DEC_BATCH * DEC_SEQ
N_TOK = N_CTX + N_LAT
TB = 512
TBP = 512
NB_CTX = N_CTX // TB
S5_L = 16
HG_L = 64
FF_TILE = 256
LANES = 128
MOD_ROWS = 8
V7X_VMEM_BYTES = 64 * 1024 * 1024
VMEM_LIMIT = V7X_VMEM_BYTES - 8 * 1024 * 1024

F32 = jnp.float32
BF16 = jnp.bfloat16

COL_XA, COL_HQ, COL_HF_FWD, COL_HF_BWD, COL_HI, COL_HGATE, COL_XC, COL_GU, COL_GV = range(9)
HG_Z_W = 2 * GROUP_W
HG_QV_W = 2 * GROUP_W
OP_IN_W = 4 * GROUP_W


def _mod_row(i, tb):
    return jnp.where(i < N_CTX // tb, 0, 1 + (i - N_CTX // tb) // (DEC_SEQ // tb))


GELU_C0 = 0.7978845608028654
GELU_C1 = GELU_C0 * 0.044715


def _gelu(x):
    return x * (0.5 + 0.5 * jnp.tanh(x * (GELU_C0 + GELU_C1 * (x * x))))


def _rms(x, g):
    return x * lax.rsqrt(jnp.mean(x * x, axis=-1, keepdims=True) + EPS) * g


def _dot(a, b):
    return jnp.dot(a, b, preferred_element_type=F32)


def _dot_nt(a, b, precision=None):
    return lax.dot_general(a, b, (((1,), (1,)), ((), ())), precision=precision, preferred_element_type=F32)


def _dot_tn(a, b):
    return lax.dot_general(a, b, (((0,), (0,)), ((), ())), preferred_element_type=F32)


def _const_spec(shape, single=False):
    kw = {"pipeline_mode": pl.Buffered(1)} if single else {}
    return pl.BlockSpec(shape, lambda *_: (0,) * len(shape), **kw)


def _layer_spec(shape, l, single=False):
    kw = {"pipeline_mode": pl.Buffered(1)} if single else {}
    return pl.BlockSpec((None,) + tuple(shape), lambda *_: (l,) + (0,) * len(shape), **kw)


def _mod_spec(l, tb=TB):
    return pl.BlockSpec((None, None, 1, N_MOD * D_MODEL), lambda i: (l, _mod_row(i, tb), 0, 0))


def _x_specs(split, width=D_MODEL, tb=TB):
    nb_ctx = N_CTX // tb
    if split:
        return [pl.BlockSpec((tb, width), lambda i: (jnp.minimum(i, nb_ctx - 1), 0)),
                pl.BlockSpec((tb, width), lambda i: (jnp.maximum(i - nb_ctx, 0), 0))]
    return [pl.BlockSpec((tb, width), lambda i: (i, 0))]


def _pick(refs, rows, is_ctx):
    if len(refs) == 1:
        return refs[0][rows, :]
    return jnp.where(is_ctx, refs[0][rows, :], refs[1][rows, :])


INPROJ_SUB = 256
INPROJ_OUT = ((HG_Z_W, F32), (HG_QV_W, BF16), (OP_IN_W, BF16), (GROUP_W, BF16), (LANES, F32), (LANES, F32))


def _inproj_kernel(*refs):
    *x_refs, mod_ref, g_ref, w_ref, hz_ref, hqv_ref, op_ref, xc_ref, xalo_ref, xahi_ref, wb_ref = refs

    @pl.when(pl.program_id(0) == 0)
    def _():
        wb_ref[...] = w_ref[...].astype(BF16)

    mod = mod_ref[...]
    sh = mod[:, 0:D_MODEL]
    sc = mod[:, D_MODEL:2 * D_MODEL]
    is_ctx = pl.program_id(0) < N_CTX // TBP
    for r in range(0, TBP, INPROJ_SUB):
        rows = slice(r, r + INPROJ_SUB)
        h = _rms(_pick(x_refs, rows, is_ctx), g_ref[...]) * (1.0 + sc) + sh
        o = _dot(h.astype(BF16), wb_ref[...])
        col = lambda c, n=1: o[:, GROUP_W * c:GROUP_W * (c + n)]
        hz_ref[rows, :] = col(COL_HF_FWD, 2)
        hqv_ref[rows, :] = jnp.concatenate([col(COL_HQ), col(COL_HI)], axis=1).astype(BF16)
        op_ref[rows, :] = jnp.concatenate([col(COL_XA), col(COL_HGATE), col(COL_GU, 2)], axis=1).astype(BF16)
        xc_ref[rows, :] = col(COL_XC).astype(BF16)
        xalo_ref[rows, :] = o[:, 0:LANES]
        xahi_ref[rows, :] = o[:, LANES:2 * LANES]


def _inproj(xs, mods, norm_g, w_in, l):
    return pl.pallas_call(
        _inproj_kernel,
        grid=(N_TOK // TBP,),
        in_specs=_x_specs(len(xs) == 2, tb=TBP) + [
            _mod_spec(l, TBP),
            _layer_spec((1, D_MODEL), l),
            _layer_spec((D_MODEL, D_IN), l, single=True),
        ],
        out_specs=tuple(pl.BlockSpec((TBP, w), lambda i: (i, 0)) for w, _ in INPROJ_OUT),
        out_shape=tuple(jax.ShapeDtypeStruct((N_TOK, w), dt) for w, dt in INPROJ_OUT),
        scratch_shapes=[pltpu.VMEM((D_MODEL, D_IN), BF16)],
        compiler_params=pltpu.CompilerParams(vmem_limit_bytes=VMEM_LIMIT),
        name="in_proj",
    )(*xs, mods, norm_g, w_in)


S5_NPAIR = S5_G // 2
S5_CW = S5_L * S5_P
S5_SW = S5_G * S5_N
S5_STEP_ROWS = N_CTX // S5_L
S5_CTX_NC = SEQ // S5_L
S5_LAT_NC = DEC_SEQ // S5_L


def _s5_slot(g, t):
    return (t + g) % S5_L


def _s5_tables(lam_re, lam_im, log_dt, b_re, b_im, c_re, c_im):
    lr = jnp.minimum(lam_re.astype(F32), LAM_RE_MAX)
    li = lam_im.astype(F32)
    dt = jnp.exp(log_dt.astype(F32))[..., None]
    mag = jnp.exp(lr * dt)
    ang = li * dt
    ab_re = mag * jnp.cos(ang)
    ab_im = mag * jnp.sin(ang)
    den = lr * lr + li * li
    xr = ab_re - 1.0
    z_re = (xr * lr + ab_im * li) / den
    z_im = (ab_im * lr - xr * li) / den
    bb_re = z_re[..., None] * b_re - z_im[..., None] * b_im
    bb_im = z_re[..., None] * b_im + z_im[..., None] * b_re
    tau = jnp.arange(S5_L + 1, dtype=F32)[:, None, None, None, None]
    pm = jnp.exp(lr * dt * tau)
    pa = li * dt * tau
    pw_re = pm * jnp.cos(pa)
    pw_im = pm * jnp.sin(pa)
    eye2 = jnp.eye(2, dtype=F32)

    def pw_pairs(a):
        a = a.transpose(1, 2, 3, 0, 4).reshape(DEPTH, 2, S5_NPAIR, 2, S5_L + 1, S5_N)
        return a.transpose(0, 1, 2, 4, 3, 5).reshape(DEPTH, 2, S5_NPAIR, S5_L + 1, 2 * S5_N)

    def mat_pairs(a):
        a = a.reshape(DEPTH, 2, S5_NPAIR, 2, S5_P, S5_N)
        return jnp.einsum('ldjaqn,ab->ldjaqbn', a, eye2).reshape(DEPTH, 2, S5_NPAIR, 2, S5_P, 2 * S5_N)

    al = jnp.stack([pw_re[S5_L], pw_im[S5_L]], axis=2).reshape(DEPTH, 2, 2, 1, S5_SW)
    return (pw_pairs(pw_re), pw_pairs(pw_im),
            mat_pairs(bb_re.transpose(0, 1, 2, 4, 3)), mat_pairs(bb_im.transpose(0, 1, 2, 4, 3)),
            mat_pairs(c_re.astype(F32)), mat_pairs(c_im.astype(F32)), al)


def _s5_shift_consts():
    n = S5_CW
    r = np.arange(n)[:, None]
    c = np.arange(n)[None, :]
    fwd = [(c == r + S5_P * s) for s in range(S5_L)]
    bwd = [(c == r - S5_P * (S5_L - 1 - s)) for s in range(S5_L)]
    return np.stack([np.concatenate(fwd, axis=1), np.concatenate(bwd, axis=1)]).astype(np.float32)


PREP_STEPS = 4
ADA_TN = N_MOD * D_MODEL // PREP_STEPS


def _ada_copy(wada_hbm, buf_ref, sem_ref, s):
    col = pl.multiple_of((s % PREP_STEPS) * ADA_TN, LANES)
    return pltpu.make_async_copy(wada_hbm.at[s // PREP_STEPS, :, pl.ds(col, ADA_TN)], buf_ref.at[s % 2],
                                 sem_ref.at[s % 2])


def _prep_kernel(c_ref, wada_hbm, bada_ref, pwr_ref, pwi_ref, br_ref, bi_ref, cr_ref, ci_ref, scat_ref,
                 mod_ref, m_ref, f_ref, e_ref, k_ref, wbuf_ref, wsem_ref):
    step = pl.program_id(1)
    s = pl.program_id(0) * PREP_STEPS + step
    n_steps = DEPTH * PREP_STEPS

    @pl.when(s == 0)
    def _():
        _ada_copy(wada_hbm, wbuf_ref, wsem_ref, s).start()

    @pl.when(s + 1 < n_steps)
    def _():
        _ada_copy(wada_hbm, wbuf_ref, wsem_ref, s + 1).start()

    for qq in range(PREP_STEPS):
        pl.when(step == qq)(functools.partial(_s5_pair_ops, qq, pwr_ref, pwi_ref, br_ref, bi_ref, cr_ref,
                                              ci_ref, f_ref, e_ref, k_ref))
    pl.when(step == PREP_STEPS - 1)(functools.partial(_s5_toeplitz, scat_ref, m_ref, k_ref))
    _ada_copy(wada_hbm, wbuf_ref, wsem_ref, s).wait()
    cv = c_ref[...]
    mod_ref[0] = _dot((cv * jax.nn.sigmoid(cv)).astype(BF16), wbuf_ref[s % 2].astype(BF16)) + bada_ref[0]


def _s5_pair_ops(qq, pwr_ref, pwi_ref, br_ref, bi_ref, cr_ref, ci_ref, f_ref, e_ref, k_ref):
    L = S5_L
    hp = lax.Precision.HIGHEST
    sw = 2 * S5_N
    per_step = S5_NPAIR // PREP_STEPS
    for j in range(per_step * qq, per_step * (qq + 1)):
        for d in range(2):
            pwr = pwr_ref[0, d, j]
            pwi = pwi_ref[0, d, j]
            for gl in range(2):
                g = 2 * j + gl
                br, bi = br_ref[0, d, j, gl], bi_ref[0, d, j, gl]
                cr, ci = cr_ref[0, d, j, gl], ci_ref[0, d, j, gl]
                ca_r, ca_i, f_r, f_i, e_r, e_i = [], [], [], [], [], []
                for t in range(L):
                    kk = t if d == 0 else L - 1 - t
                    pr, pi = pwr[kk:kk + 1], pwi[kk:kk + 1]
                    ca_r.append(cr * pr - ci * pi)
                    ca_i.append(cr * pi + ci * pr)
                    kf = L - 1 - t if d == 0 else t
                    pr, pi = pwr[kf:kf + 1], pwi[kf:kf + 1]
                    f_r.append(br * pr - bi * pi)
                    f_i.append(br * pi + bi * pr)
                    ke = t + 1 if d == 0 else L - t
                    pr, pi = pwr[ke:ke + 1], pwi[ke:ke + 1]
                    e_r.append(cr * pr - ci * pi)
                    e_i.append(-(cr * pi + ci * pr))
                cat = lambda xs: jnp.concatenate(xs, axis=0)
                k = _dot_nt(br, cat(ca_r), hp) - _dot_nt(bi, cat(ca_i), hp)
                k_ref[d, S5_P * g:S5_P * (g + 1), :] = k
                slots = lambda xs: cat([xs[(p - g) % L] for p in range(L)])
                rows = slice(S5_CW * gl, S5_CW * (gl + 1))
                f_ref[0, d, j, rows, 0:sw] = slots(f_r).astype(BF16)
                f_ref[0, d, j, rows, sw:2 * sw] = slots(f_i).astype(BF16)
                e_ref[0, d, 0, j, rows, :] = slots(e_r).astype(BF16)
                e_ref[0, d, 1, j, rows, :] = slots(e_i).astype(BF16)


def _s5_toeplitz(scat_ref, m_ref, k_ref):
    L = S5_L
    kf = k_ref[0].astype(BF16)
    kb = k_ref[1].astype(BF16)
    for s in range(L):
        cols = slice(S5_CW * s, S5_CW * (s + 1))
        res = _dot(kf, scat_ref[0, :, cols]) + _dot(kb, scat_ref[1, :, cols])
        for g in range(S5_G):
            blk = res[S5_P * g:S5_P * (g + 1), :]
            if g:
                blk = pltpu.roll(blk, S5_P * g, 1)
            r0 = S5_P * _s5_slot(g, s)
            m_ref[0, g, r0:r0 + S5_P, :] = blk.astype(BF16)


def _prep(cvecs, w_ada, b_ada, tables, scat):
    pwr, pwi, br, bi, cr, ci, _ = tables
    n = N_MOD * D_MODEL

    def lspec(shape):
        return pl.BlockSpec((1,) + shape, lambda l, q: (l,) + (0,) * len(shape))

    pw_shape = (2, S5_NPAIR, S5_L + 1, 2 * S5_N)
    mat_shape = (2, S5_NPAIR, 2, S5_P, 2 * S5_N)
    return pl.pallas_call(
        _prep_kernel,
        grid=(DEPTH, PREP_STEPS),
        in_specs=[pl.BlockSpec((MOD_ROWS, D_MODEL), lambda l, q: (0, 0)),
                  pl.BlockSpec(memory_space=pl.ANY),
                  pl.BlockSpec((1, 1, ADA_TN), lambda l, q: (l, 0, q)),
                  lspec(pw_shape), lspec(pw_shape), lspec(mat_shape), lspec(mat_shape), lspec(mat_shape),
                  lspec(mat_shape), _const_spec(scat.shape)],
        out_specs=(pl.BlockSpec((1, MOD_ROWS, ADA_TN), lambda l, q: (l, 0, q)),
                   lspec((S5_G, S5_CW, S5_CW)), lspec((2, S5_NPAIR, 2 * S5_CW, 4 * S5_N)),
                   lspec((2, 2, S5_NPAIR, 2 * S5_CW, 2 * S5_N))),
        out_shape=(jax.ShapeDtypeStruct((DEPTH, MOD_ROWS, n), F32),
                   jax.ShapeDtypeStruct((DEPTH, S5_G, S5_CW, S5_CW), BF16),
                   jax.ShapeDtypeStruct((DEPTH, 2, S5_NPAIR, 2 * S5_CW, 4 * S5_N), BF16),
                   jax.ShapeDtypeStruct((DEPTH, 2, 2, S5_NPAIR, 2 * S5_CW, 2 * S5_N), BF16)),
        scratch_shapes=[pltpu.VMEM((2, S5_G * S5_P, S5_CW), F32), pltpu.VMEM((2, D_MODEL, ADA_TN), F32),
                        pltpu.SemaphoreType.DMA((2,))],
        compiler_params=pltpu.CompilerParams(vmem_limit_bytes=VMEM_LIMIT,
                                             dimension_semantics=("arbitrary", "arbitrary")),
        name="param_prep",
    )(cvecs, w_ada, b_ada.reshape(DEPTH, 1, n), pwr, pwi, br, bi, cr, ci, scat)


def _s5_kernel(xlo_ref, xhi_ref, m_ref, f_ref, e_ref, al_ref, h0_ref, ylo_ref, yhi_ref, hfin_ref,
               uy_ref, zh_ref):
    R = S5_STEP_ROWS
    pw = 2 * S5_CW
    sw = 2 * S5_N
    gph = S5_G // 2
    step = pl.program_id(0)

    for half, x_ref in enumerate((xlo_ref, xhi_ref)):
        for t in range(S5_L):
            xt = x_ref[pl.ds(t, R, stride=S5_L), :]
            shift = (S5_P * t) % LANES
            xr = pltpu.roll(xt, shift, 1) if shift else xt
            for gl in range(gph):
                g = gph * half + gl
                src = (S5_P * gl + shift) % LANES
                lo = S5_CW * g + S5_P * _s5_slot(g, t)
                assert lo % LANES == src
                uy_ref[:, lo:lo + S5_P] = xr[:, src:src + S5_P]

    for j in range(S5_NPAIR):
        ub = uy_ref[:, pw * j:pw * (j + 1)].astype(BF16)
        for d in range(2):
            o = _dot(ub, f_ref[d, j])
            zh_ref[d, 0, :, sw * j:sw * (j + 1)] = o[:, :sw]
            zh_ref[d, 1, :, sw * j:sw * (j + 1)] = o[:, sw:]

    def scan_group(seq_rows, nc, init, fin_rows):
        chains = [(d, r0) for d in range(2) for r0 in seq_rows]

        def body(c, carry):
            out = []
            for k, (d, r0) in enumerate(chains):
                hr, hi = carry[2 * k], carry[2 * k + 1]
                r = r0 + (c if d == 0 else nc - 1 - c)
                zr = zh_ref[d, 0, pl.ds(r, 1), :]
                zi = zh_ref[d, 1, pl.ds(r, 1), :]
                zh_ref[d, 0, pl.ds(r, 1), :] = hr
                zh_ref[d, 1, pl.ds(r, 1), :] = hi
                ar = al_ref[d, 0]
                ai = al_ref[d, 1]
                out.append(ar * hr - ai * hi + zr)
                out.append(ar * hi + ai * hr + zi)
            return tuple(out)

        fin = lax.fori_loop(0, nc, body, tuple(init))
        if fin_rows is not None:
            for k, (d, _) in enumerate(chains):
                s = fin_rows[k % len(seq_rows)]
                hfin_ref[d, 0, s:s + 1, :] = fin[2 * k]
                hfin_ref[d, 1, s:s + 1, :] = fin[2 * k + 1]

    @pl.when(step == 0)
    def _():
        zero = jnp.zeros((1, S5_SW), F32)
        for s0 in range(0, BATCH, 4):
            seqs = list(range(s0, s0 + 4))
            scan_group([s * S5_CTX_NC for s in seqs], S5_CTX_NC, [zero] * 16, seqs)

    @pl.when(step == 1)
    def _():
        init = []
        for d in range(2):
            for s in range(DEC_BATCH):
                init.append(h0_ref[d, 0, s:s + 1, :])
                init.append(h0_ref[d, 1, s:s + 1, :])
        scan_group([s * S5_LAT_NC for s in range(DEC_BATCH)], S5_LAT_NC, init, None)

    for j in range(S5_NPAIR):
        acc = None
        for d in range(2):
            for ri in range(2):
                hb = zh_ref[d, ri, :, sw * j:sw * (j + 1)].astype(BF16)
                t = _dot_nt(hb, e_ref[d, ri, j])
                acc = t if acc is None else acc + t
        for gl in range(2):
            g = 2 * j + gl
            ug = uy_ref[:, S5_CW * g:S5_CW * (g + 1)].astype(BF16)
            uy_ref[:, S5_CW * g:S5_CW * (g + 1)] = acc[:, S5_CW * gl:S5_CW * (gl + 1)] + _dot(ug, m_ref[g])

    lane = lax.broadcasted_iota(jnp.int32, (R, LANES), 1)
    in_piece = [jnp.logical_and(lane >= S5_P * k, lane < S5_P * (k + 1)) for k in range(LANES // S5_P)]
    for half, y_ref in enumerate((ylo_ref, yhi_ref)):
        for t in range(S5_L):
            merged = None
            for gl in range(gph):
                g = gph * half + gl
                slot = _s5_slot(g, t)
                col = S5_CW * g + LANES * (S5_P * slot // LANES)
                v = uy_ref[:, col:col + LANES]
                merged = v if merged is None else jnp.where(in_piece[slot % (LANES // S5_P)], v, merged)
            shift = (LANES - S5_P * t % LANES) % LANES
            y_ref[pl.ds(t, R, stride=S5_L), :] = pltpu.roll(merged, shift, 1) if shift else merged


def _s5(xa_lo, xa_hi, m, f2, e2, al, h0, l):
    half = pl.BlockSpec((N_CTX, LANES), lambda s: (s, 0))
    return pl.pallas_call(
        _s5_kernel,
        grid=(2,),
        in_specs=[half, half,
                  _layer_spec(m.shape[1:], l, single=True), _layer_spec(f2.shape[1:], l, single=True),
                  _layer_spec(e2.shape[1:], l, single=True), _layer_spec(al.shape[1:], l),
                  _layer_spec(h0.shape[1:], l)],
        out_specs=(half, half, _const_spec((2, 2, BATCH, S5_SW))),
        out_shape=(jax.ShapeDtypeStruct((N_TOK, LANES), F32),
                   jax.ShapeDtypeStruct((N_TOK, LANES), F32),
                   jax.ShapeDtypeStruct((2, 2, BATCH, S5_SW), F32)),
        scratch_shapes=[pltpu.VMEM((S5_STEP_ROWS, S5_G * S5_CW), F32),
                        pltpu.VMEM((2, 2, S5_STEP_ROWS, S5_SW), F32)],
        compiler_params=pltpu.CompilerParams(vmem_limit_bytes=VMEM_LIMIT),
        name="s5_mixer",
    )(xa_lo, xa_hi, m, f2, e2, al, h0)


HG_NLEV = int(math.log2(HG_L))
HG_W = HG_HEADS * HG_DK
HG_CPS = 8
HG_BLK = HG_CPS * HG_L
HG_NB_CTX = N_CTX // HG_BLK
HG_SPB = HG_BLK // SEQ
HG_CPQ = SEQ // HG_L
HG_NC_LAT = DEC_SEQ // HG_BLK
assert HG_SPB >= 1 and HG_SPB * SEQ == HG_BLK
assert HG_L == HG_DK


def _hgrn_consts():
    L = HG_L
    w = np.zeros((HG_NLEV + 2, L, L), np.float32)
    mask = np.zeros((HG_NLEV + 1, L, L), np.float32)
    for lev in range(HG_NLEV):
        blk = L >> lev
        half = blk // 2
        for t in range(L):
            p, o = divmod(t, blk)
            bd = p * blk + half - 1
            if o >= half:
                w[lev, t, bd + 1:t + 1] = 1.0
            else:
                w[lev, t, t + 1:bd + 1] = 1.0
        jj, ii = np.meshgrid(np.arange(L), np.arange(L), indexing='ij')
        mask[lev] = ((jj // blk == ii // blk) & (jj % blk >= half) & (ii % blk < half)).astype(np.float32)
    mask[HG_NLEV] = np.eye(L, dtype=np.float32)
    for t in range(L):
        w[HG_NLEV, t, :t + 1] = 1.0
        w[HG_NLEV + 1, t, t + 1:] = 1.0
    out = []
    for wd, md in ((w, mask), (w[:, ::-1, ::-1], mask[:, ::-1, ::-1])):
        wflat = wd.reshape((HG_NLEV + 2) * L, L)
        out.append((np.concatenate([wflat] * 3, axis=1), np.tile(md, (1, 1, HG_HEADS))))
    wcat = np.stack([out[0][0], out[1][0]])
    mask4 = np.stack([out[0][1], out[1][1]])
    hm = np.kron(np.eye(HG_HEADS, dtype=np.float32), np.ones((HG_DK, HG_DK), np.float32))
    return wcat, mask4, hm


def _hg_pos(i):
    is_ctx = i < HG_NB_CTX
    c = jnp.where(is_ctx, 0, (i - HG_NB_CTX) % HG_NC_LAT)
    nc = jnp.where(is_ctx, 1, HG_NC_LAT)
    return is_ctx, c, nc


def _hg_bwd_blk(i):
    _, c, nc = _hg_pos(i)
    return i + nc - 1 - 2 * c


def _hg_local(items, lb_ref, wcat_ref, mask_ref, hm):
    L = HG_L
    hmb = hm.astype(BF16)

    def bd4(x):
        return jnp.concatenate([x] * HG_HEADS, axis=0) * hmb

    kks, exs = [], []
    for d, q, z, v in items:
        lbp = lb_ref[d]
        sp = jnp.log1p(jnp.exp(-jnp.abs(z)))
        ls = jnp.minimum(z, 0.0) - sp
        kks.append(lbp[2:3] * jnp.exp(jnp.minimum(-z, 0.0) - sp))
        a = lbp[0:1]
        b = lbp[1:2] + ls
        logf = jnp.maximum(a, b) + jnp.log1p(jnp.exp(-jnp.abs(a - b)))
        p0 = logf.astype(BF16)
        r0 = logf - p0.astype(F32)
        p1 = r0.astype(BF16)
        p2 = (r0 - p1.astype(F32)).astype(BF16)
        exs.append(jnp.exp(_dot(wcat_ref[d], jnp.concatenate([p0, p1, p2], axis=0))))
    scs = [None] * len(items)
    for lev in range(HG_NLEV + 1):
        for n, (d, q, z, v) in enumerate(items):
            if lev < HG_NLEV:
                al = exs[n][L * lev:L * (lev + 1)]
                lhs = (q * al).astype(BF16)
                rhs = (kks[n] * al).astype(BF16)
            else:
                lhs = q.astype(BF16)
                rhs = kks[n].astype(BF16)
            t = _dot_nt(lhs, bd4(rhs)) * mask_ref[d, lev]
            scs[n] = t if scs[n] is None else scs[n] + t
    out = []
    for n, (d, q, z, v) in enumerate(items):
        vb = v.astype(BF16)
        o = _dot(scs[n].astype(BF16), bd4(vb))
        eq = exs[n][L * HG_NLEV:L * (HG_NLEV + 1)]
        qe = (q * eq).astype(BF16)
        ke = (kks[n] * exs[n][L * (HG_NLEV + 1):L * (HG_NLEV + 2)]).astype(BF16)
        g = eq[L - 1:L] if d == 0 else eq[0:1]
        out.append((o, qe, g, _dot_tn(vb, ke) * hm))
    return out


def _hgrn_kernel(zf_ref, qvf_ref, zb_ref, qvb_ref, lb_ref, w_ref, mask_ref, hm_ref, s0_ref, wup_ref, wdn_ref,
                 of_ref, ob_ref, sfin_ref, wupb_ref, wdnb_ref, st_ref):
    is_ctx, c, nc = _hg_pos(pl.program_id(0))
    wupb_ref[...] = wup_ref[...].astype(BF16)
    wdnb_ref[...] = wdn_ref[...].astype(BF16)

    hm = hm_ref[...]
    rows = [slice(HG_L * k, HG_L * (k + 1)) for k in range(HG_CPS)]
    items = [(d, qv_ref[r, 0:GROUP_W], z_ref[r, :], qv_ref[r, GROUP_W:2 * GROUP_W])
             for d, (z_ref, qv_ref) in enumerate(((zf_ref, qvf_ref), (zb_ref, qvb_ref))) for r in rows]
    loc = _hg_local(items, lb_ref, w_ref, mask_ref, hm)

    def chain(d, o_ref, st, chunks):
        for k in (chunks if d == 0 else reversed(chunks)):
            o, qe, g, upd = loc[HG_CPS * d + k]
            o_ref[rows[k], :] = (o + _dot_nt(qe, st.astype(BF16))).astype(BF16)
            st = st * g + upd
        return st

    @pl.when(is_ctx)
    def _():
        for sq in range(HG_SPB):
            for d, o_ref in enumerate((of_ref, ob_ref)):
                st = chain(d, o_ref, jnp.zeros((HG_W, HG_W), F32), range(HG_CPQ * sq, HG_CPQ * (sq + 1)))
                s_kv = st.T
                for h in range(HG_HEADS):
                    sfin_ref[sq, d, h] = s_kv[HG_DK * h:HG_DK * (h + 1), HG_DK * h:HG_DK * (h + 1)]

    @pl.when(jnp.logical_not(is_ctx))
    def _():
        @pl.when(c == 0)
        def _():
            st_ref[...] = s0_ref[0]

        for d, o_ref in enumerate((of_ref, ob_ref)):
            st_ref[d] = chain(d, o_ref, st_ref[d], range(HG_CPS))


def _hgrn(hz, hqv, lbp, consts, s0, w_up, w_down, l):
    wcat, mask4, hm = consts
    steps = N_TOK // HG_BLK
    up_rows = D_MODEL // steps
    dn_rows = 2 * D_FF // steps

    return pl.pallas_call(
        _hgrn_kernel,
        grid=(N_TOK // HG_BLK,),
        in_specs=[pl.BlockSpec((HG_BLK, GROUP_W), lambda i: (i, 0)),
                  pl.BlockSpec((HG_BLK, HG_QV_W), lambda i: (i, 0)),
                  pl.BlockSpec((HG_BLK, GROUP_W), lambda i: (_hg_bwd_blk(i), 1)),
                  pl.BlockSpec((HG_BLK, HG_QV_W), lambda i: (_hg_bwd_blk(i), 0)),
                  _layer_spec(lbp.shape[1:], l), _const_spec(wcat.shape), _const_spec(mask4.shape),
                  _const_spec(hm.shape),
                  pl.BlockSpec((None, 1, 2, HG_W, HG_W),
                               lambda i: (l, jnp.maximum(i - HG_NB_CTX, 0) // HG_NC_LAT, 0, 0, 0)),
                  pl.BlockSpec((None, up_rows, 2 * D_FF), lambda i: (l, i, 0)),
                  pl.BlockSpec((None, dn_rows, D_MODEL), lambda i: (l, i // 2, 0))],
        out_specs=(pl.BlockSpec((HG_BLK, HG_W), lambda i: (i, 0)),
                   pl.BlockSpec((HG_BLK, HG_W), lambda i: (_hg_bwd_blk(i), 0)),
                   pl.BlockSpec((HG_SPB, 2, HG_HEADS, HG_DK, HG_DK),
                                lambda i: (jnp.minimum(i, HG_NB_CTX - 1), 0, 0, 0, 0)),
                   pl.BlockSpec((up_rows, 2 * D_FF), lambda i: (i, 0)),
                   pl.BlockSpec((dn_rows, D_MODEL), lambda i: (i // 2, 0))),
        out_shape=(jax.ShapeDtypeStruct((N_TOK, HG_W), BF16),
                   jax.ShapeDtypeStruct((N_TOK, HG_W), BF16),
                   jax.ShapeDtypeStruct((BATCH, 2, HG_HEADS, HG_DK, HG_DK), F32),
                   jax.ShapeDtypeStruct((D_MODEL, 2 * D_FF), BF16),
                   jax.ShapeDtypeStruct((D_FF, D_MODEL), BF16)),
        scratch_shapes=[pltpu.VMEM((2, HG_W, HG_W), F32)],
        compiler_params=pltpu.CompilerParams(vmem_limit_bytes=VMEM_LIMIT),
        name="hgrn_mixer",
    )(hz, hqv, hz, hqv, lbp, wcat, mask4, hm, s0, w_up, w_down)


FN_CTX_SPG = 4
FN_LAT_TR = 512


def _dft_consts(t_len):
    n = GROUP_W // 4
    k = np.arange(n)
    ang = 2.0 * np.pi * ((k[:, None] * k[None, :]) % n) / n
    eye = np.eye(4)
    cs = np.concatenate([np.kron(eye, np.cos(ang)), np.kron(eye, np.sin(ang))], axis=1) / math.sqrt(n)
    t = np.arange(t_len)
    angt = 2.0 * np.pi * ((t[:, None] * t[None, :]) % t_len) / t_len
    dft = np.concatenate([np.cos(angt), -np.sin(angt)], axis=1) / math.sqrt(t_len)
    return cs.astype(np.float32), dft.astype(np.float32)


def _fnet_kernel(x_ref, cs_ref, dft_ref, w_ref, o_ref, r_ref, *, t_len, spg, tr):
    g = pl.program_id(1)

    @pl.when(pl.program_id(0) == 0)
    def _():
        t = _dot(x_ref[...].astype(BF16), cs_ref[...])
        for s in range(spg):
            rows = slice(t_len * s, t_len * (s + 1))
            r_ref[spg * g + s, 0:t_len, :] = t[rows, :GROUP_W].astype(BF16)
            r_ref[spg * g + s, t_len:2 * t_len, :] = t[rows, GROUP_W:].astype(BF16)

    w = w_ref[...].astype(BF16)
    for s in range(spg):
        y = _dot(dft_ref[...], r_ref[spg * g + s])
        o_ref[tr * s:tr * (s + 1), :] = _dot(y.astype(BF16), w).astype(BF16)


def _fnet(xc, cs, dft, fn_w, l, *, n_seq, seq_len, tok0, spg, tr):
    nj = seq_len // tr
    assert spg == 1 or nj == 1
    gb0 = tok0 // (spg * seq_len)
    return pl.pallas_call(
        functools.partial(_fnet_kernel, t_len=seq_len, spg=spg, tr=tr),
        grid=(nj, n_seq // spg),
        in_specs=[
            pl.BlockSpec((spg * seq_len, GROUP_W), lambda j, g: (gb0 + g, 0)),
            _const_spec((GROUP_W, 2 * GROUP_W)),
            pl.BlockSpec((tr, 2 * seq_len), lambda j, g: (j, 0)),
            _layer_spec((GROUP_W, GROUP_W), l),
        ],
        out_specs=pl.BlockSpec((spg * tr, GROUP_W), lambda j, g: (g * nj + j, 0)),
        out_shape=jax.ShapeDtypeStruct((n_seq * seq_len, GROUP_W), BF16),
        scratch_shapes=[pltpu.VMEM((n_seq, 2 * seq_len, GROUP_W), BF16)],
        compiler_params=pltpu.CompilerParams(vmem_limit_bytes=VMEM_LIMIT),
        name="fnet_mixer",
    )(xc, cs, dft, fn_w)


OUTPROJ_SUB = 256


def _outproj_kernel(*refs):
    (op_ref, ylo_ref, yhi_ref, of_ref, ob_ref, ycc_ref, ycl_ref, mod_ref, d_ref,
     wglu_ref, gn_ref, bdm_ref, gmn_ref, gmw_ref, gmb_ref, gmh_ref, wout_ref, n2_ref,
     x1_ref, h2_ref, wob_ref) = refs[-21:]
    x_refs = refs[:-21]
    xa_ref, hgate_ref, gu_ref, gv_ref = (op_ref.at[:, GROUP_W * k:GROUP_W * (k + 1)] for k in range(4))

    @pl.when(pl.program_id(0) == 0)
    def _():
        wob_ref[...] = wout_ref[...].astype(BF16)

    gn = gn_ref[...]
    gm_hm = gmh_ref[...]
    mod = mod_ref[...]
    g1 = mod[:, 2 * D_MODEL:3 * D_MODEL]
    sh2 = mod[:, 3 * D_MODEL:4 * D_MODEL]
    sc2 = mod[:, 4 * D_MODEL:5 * D_MODEL]
    is_ctx = pl.program_id(0) < N_CTX // TBP
    for r in range(0, TBP, OUTPROJ_SUB):
        rows = slice(r, r + OUTPROJ_SUB)
        ys = jnp.concatenate([ylo_ref[rows, :], yhi_ref[rows, :]], axis=1)
        y5 = _gelu(ys + d_ref[...] * xa_ref[rows, :].astype(F32))
        glu = jax.nn.sigmoid(_dot(y5.astype(BF16), wglu_ref[...].astype(BF16)))
        out_a = _rms(y5 * glu, gn[:, 0:GROUP_W])
        o = of_ref[rows, :].astype(F32) + ob_ref[rows, :].astype(F32)
        o2 = o * o
        o2h = o2.astype(BF16)
        o2l = (o2 - o2h.astype(F32)).astype(BF16)
        ms = _dot(o2h, bdm_ref[...]) + _dot(o2l, bdm_ref[...])
        hg = hgate_ref[rows, :].astype(F32)
        out_b = o * lax.rsqrt(ms + EPS) * gn[:, GROUP_W:2 * GROUP_W] * (hg * jax.nn.sigmoid(hg))
        out_c = _rms(_pick((ycc_ref, ycl_ref), rows, is_ctx).astype(F32), gn[:, 2 * GROUP_W:3 * GROUP_W])
        out_d = []
        for c in range(r // GM_CHUNK, (r + OUTPROJ_SUB) // GM_CHUNK):
            crow = slice(GM_CHUNK * c, GM_CHUNK * (c + 1))
            gv = _rms(_gelu(gv_ref[crow, :].astype(F32)), gmn_ref[...]).astype(BF16)
            g4 = jnp.concatenate([gv] * GM_HEADS, axis=0) * gm_hm
            sp = _dot(gmw_ref[...], g4) + gmb_ref[...]
            out_d.append(_rms(_gelu(gu_ref[crow, :].astype(F32)) * sp, gn[:, 3 * GROUP_W:]))
        out_d = jnp.concatenate(out_d, axis=0)
        m = None
        for k, part in enumerate((out_a, out_b, out_c, out_d)):
            t = _dot(part.astype(BF16), wob_ref[GROUP_W * k:GROUP_W * (k + 1), :])
            m = t if m is None else m + t
        x1 = _pick(x_refs, rows, is_ctx) + g1 * m
        x1_ref[rows, :] = x1
        h2_ref[rows, :] = (_rms(x1, n2_ref[...]) * (1.0 + sc2) + sh2).astype(BF16)


def _outproj(xs, op_in, ylo, yhi, of, ob, yc, mods, s5_d, wglu, gn, bdm, gm_norm_g, gm_w, gm_b, gm_hm, wout,
             norm2_g, l):
    def tok(width, col=0):
        return pl.BlockSpec((TBP, width), lambda i: (i, col))

    return pl.pallas_call(
        _outproj_kernel,
        grid=(N_TOK // TBP,),
        in_specs=_x_specs(len(xs) == 2, tb=TBP) + [
            tok(OP_IN_W), tok(LANES), tok(LANES), tok(GROUP_W), tok(GROUP_W), *_x_specs(True, GROUP_W, TBP),
            _mod_spec(l, TBP),
            _layer_spec((1, GROUP_W), l), _layer_spec((GROUP_W, GROUP_W), l), _layer_spec((1, D_MODEL), l),
            _const_spec((GROUP_W, GROUP_W)),
            _layer_spec((1, GROUP_W), l), _layer_spec((GM_CHUNK, GM_HEADS * GM_CHUNK), l),
            _layer_spec((GM_CHUNK, GROUP_W), l), _const_spec((GM_HEADS * GM_CHUNK, GROUP_W)),
            _layer_spec((D_MODEL, D_MODEL), l, single=True), _layer_spec((1, D_MODEL), l),
        ],
        out_specs=(tok(D_MODEL), tok(D_MODEL)),
        out_shape=(jax.ShapeDtypeStruct((N_TOK, D_MODEL), F32),
                   jax.ShapeDtypeStruct((N_TOK, D_MODEL), BF16)),
        scratch_shapes=[pltpu.VMEM((D_MODEL, D_MODEL), BF16)],
        compiler_params=pltpu.CompilerParams(vmem_limit_bytes=VMEM_LIMIT),
        name="out_proj",
    )(*xs, op_in, ylo, yhi, of, ob, *yc, mods, s5_d, wglu, gn, bdm, gm_norm_g, gm_w, gm_b,
      gm_hm, wout, norm2_g)


FF_SEG = GRID_W
FF_NSEG = TB // FF_SEG
FF_SEQ_STRIPS = SEQ // FF_SEG
FF_PERM_ROWS = FF_SEQ_STRIPS * FF_SEG
SUBLANES = 8
FF_NGRP = FF_SEG // SUBLANES


def _ffn_perm_consts():
    p = np.zeros((FF_PERM_ROWS, FF_PERM_ROWS), np.float32)
    for k in range(FF_SEQ_STRIPS):
        for t in range(FF_SEG):
            p[FF_SEG * k + SUBLANES * (t % FF_NGRP) + t // FF_NGRP, FF_SEG * k + t] = 1.0
    return p


def _ffn_kernel(x1_ref, h2_ref, mod_ref, perm_ref, wup_ref, cw_ref, cb_ref, wdn_hbm, fg_ref, *rest, final):
    *o_refs, h2p_ref, z_ref, hid_ref, wdn_ref, wdn_sem = rest
    i = pl.program_id(0)
    wdn_copy = pltpu.make_async_copy(wdn_hbm, wdn_ref, wdn_sem.at[0])
    pl.when(i == 0)(wdn_copy.start)
    joined = (i < NB_CTX).astype(F32)
    n_tiles = D_FF // FF_TILE
    sub = lax.broadcasted_iota(jnp.int32, (SUBLANES, 2 * FF_TILE), 0)
    is_first = sub == 0
    is_last = sub == SUBLANES - 1
    zero_grp = jnp.zeros((SUBLANES, 2 * FF_TILE), F32)
    for r in range(0, TB, FF_PERM_ROWS):
        h2p_ref[r:r + FF_PERM_ROWS, :] = _dot(perm_ref[...], h2_ref[r:r + FF_PERM_ROWS, :]).astype(BF16)

    def up(j):
        lo = FF_TILE * j
        for c, col in enumerate((lo, D_FF + lo)):
            z_ref[j % 2, :, FF_TILE * c:FF_TILE * (c + 1)] = _dot(h2p_ref[...], wup_ref[:, col:col + FF_TILE])

    def gate(j):
        slot = j % 2
        lo = FF_TILE * j
        w = jnp.concatenate([cw_ref[:, lo:lo + FF_TILE], cw_ref[:, D_FF + lo:D_FF + lo + FF_TILE]], axis=1)
        b = jnp.concatenate([cb_ref[:, lo:lo + FF_TILE], cb_ref[:, D_FF + lo:D_FF + lo + FF_TILE]], axis=1)

        def grp(k, v):
            r = FF_SEG * k + SUBLANES * v
            return z_ref[slot, r:r + SUBLANES, :]

        down = [pltpu.roll(grp(k, FF_NGRP - 1), 1, 0) for k in range(FF_NSEG)]
        up_ = [pltpu.roll(grp(k, 0), SUBLANES - 1, 0) for k in range(FF_NSEG)]
        strips = []
        for k in range(FF_NSEG):
            g = [grp(k, v) for v in range(FF_NGRP)]
            before = down[k - 1] * joined if k % FF_SEQ_STRIPS > 0 else zero_grp
            after = up_[k + 1] * joined if k % FF_SEQ_STRIPS < FF_SEQ_STRIPS - 1 else zero_grp
            zm1 = jnp.concatenate([jnp.where(is_first, before, down[k])] + g[:-1], axis=0)
            zp1 = jnp.concatenate(g[1:] + [jnp.where(is_last, after, up_[k])], axis=0)
            zc = b + zm1 * w[0:1] + jnp.concatenate(g, axis=0) * w[1:2] + zp1 * w[2:3]
            strips.append((_gelu(zc[:, :FF_TILE]) * zc[:, FF_TILE:]).astype(BF16))
        hid_ref[:, lo:lo + FF_TILE] = jnp.concatenate(strips, axis=0)

    up(0)
    for j in range(n_tiles):
        if j + 1 < n_tiles:
            up(j + 1)
        gate(j)
    pl.when(i == 0)(wdn_copy.wait)
    acc = _dot(hid_ref[...], wdn_ref[...])
    acc = jnp.swapaxes(acc.reshape(FF_NSEG, FF_NGRP, SUBLANES, D_MODEL), 1, 2).reshape(TB, D_MODEL)
    g2 = mod_ref[...][:, 5 * D_MODEL:6 * D_MODEL]
    x2 = x1_ref[...] + g2 * acc
    if final:
        y = _rms(x2, fg_ref[...])
        yp_ref, ys_ref = o_refs

        @pl.when(i < NB_CTX)
        def _():
            yp_ref[...] = y

        @pl.when(i >= NB_CTX)
        def _():
            ys_ref[...] = y
    else:
        o_refs[0][...] = x2


def _ffn(x1, h2, mods, perm, wup_bf, conv_w, conv_b, wdn_bf, final_g, l, *, final):
    if final:
        out_specs = tuple(_x_specs(True))
        out_shape = (jax.ShapeDtypeStruct((N_CTX, D_MODEL), F32), jax.ShapeDtypeStruct((N_LAT, D_MODEL), F32))
    else:
        out_specs = pl.BlockSpec((TB, D_MODEL), lambda i: (i, 0))
        out_shape = jax.ShapeDtypeStruct((N_TOK, D_MODEL), F32)
    return pl.pallas_call(
        functools.partial(_ffn_kernel, final=final),
        grid=(N_TOK // TB,),
        in_specs=[
            pl.BlockSpec((TB, D_MODEL), lambda i: (i, 0)),
            pl.BlockSpec((TB, D_MODEL), lambda i: (i, 0)),
            _mod_spec(l), _const_spec((FF_PERM_ROWS, FF_PERM_ROWS)),
            _const_spec((D_MODEL, 2 * D_FF), single=True), _layer_spec((3, 2 * D_FF), l),
            _layer_spec((1, 2 * D_FF), l), pl.BlockSpec(memory_space=pl.ANY),
            _const_spec((1, D_MODEL)),
        ],
        out_specs=out_specs,
        out_shape=out_shape,
        scratch_shapes=[pltpu.VMEM((TB, D_MODEL), BF16), pltpu.VMEM((2, TB, 2 * FF_TILE), F32),
                        pltpu.VMEM((TB, D_FF), BF16), pltpu.VMEM((D_FF, D_MODEL), BF16),
                        pltpu.SemaphoreType.DMA((1,))],
        compiler_params=pltpu.CompilerParams(vmem_limit_bytes=VMEM_LIMIT, dimension_semantics=("arbitrary",)),
        name="conv_ffn",
    )(x1, h2, mods, perm, wup_bf, conv_w, conv_b, wdn_bf, final_g.reshape(1, D_MODEL))


def kernel(x_prompt, x_sample, state_s5_re, state_s5_im, state_hgrn, c, c_ctx, w_ada, b_ada, norm1_g,
           norm2_g, w_in, s5_lam_re, s5_lam_im, s5_log_dt, s5_b_re, s5_b_im, s5_c_re, s5_c_im, s5_d,
           s5_w_glu, hg_lb_logits, fn_w, gm_norm_g, gm_ws, gm_bs, grp_norm_g, w_out, ffn_w_up,
           ffn_conv_w, ffn_conv_b, ffn_w_down, final_norm_g):
    xs = (x_prompt.reshape(N_CTX, D_MODEL), x_sample.reshape(N_LAT, D_MODEL))
    cvecs = jnp.concatenate([c_ctx[None], c, jnp.zeros((MOD_ROWS - 1 - DEC_BATCH, D_MODEL), F32)], axis=0)

    lb_p = jax.nn.softmax(hg_lb_logits.astype(F32), axis=0)
    lbs = jnp.maximum(jnp.cumsum(lb_p, axis=0) - lb_p[0], 0.0)
    lbps = jnp.stack([jnp.log(lbs), jnp.log1p(-lbs), 1.0 - lbs], axis=2)

    hg_consts_np = _hgrn_consts()
    hg_consts = (jnp.asarray(hg_consts_np[0], BF16), jnp.asarray(hg_consts_np[1], F32),
                 jnp.asarray(hg_consts_np[2], F32))
    cs_np, dft_ctx_np = _dft_consts(SEQ)
    _, dft_lat_np = _dft_consts(DEC_SEQ)
    cs = jnp.asarray(cs_np, F32).astype(BF16)
    dft_ctx = jnp.asarray(dft_ctx_np, F32).astype(BF16)
    dft_lat = jnp.asarray(dft_lat_np, F32).astype(BF16)
    gm_hm = jnp.asarray(np.kron(np.eye(GM_HEADS), np.ones((GM_CHUNK, GROUP_W // GM_HEADS))), BF16)
    bdm = jnp.asarray(np.kron(np.eye(HG_HEADS), np.ones((HG_DK, HG_DK))) / HG_DK, BF16)

    tables = _s5_tables(s5_lam_re, s5_lam_im, s5_log_dt, s5_b_re, s5_b_im, s5_c_re, s5_c_im)
    mods, s5_m, s5_f, s5_e = _prep(cvecs, w_ada, b_ada, tables, jnp.asarray(_s5_shift_consts(), BF16))
    mods = mods.reshape(DEPTH, MOD_ROWS, 1, N_MOD * D_MODEL)
    s5_al = tables[-1]
    s5_h0 = jnp.stack([state_s5_re, state_s5_im]).astype(F32).transpose(2, 3, 0, 1, 4, 5)
    s5_h0 = s5_h0.reshape(DEPTH, 2, 2, DEC_BATCH, S5_SW)
    hg_s0 = jnp.einsum('bldhkv,hg->lbdhvgk', state_hgrn.astype(F32), jnp.eye(HG_HEADS, dtype=F32))
    hg_s0 = hg_s0.reshape(DEPTH, DEC_BATCH, 2, HG_W, HG_W)

    def rows(a):
        return a.reshape(DEPTH, 1, a.shape[-1])

    norm1_r, norm2_r, gn_r, s5_d_r, gmn_r, cb_r = (rows(a) for a in (
        norm1_g, norm2_g, grp_norm_g, s5_d, gm_norm_g, ffn_conv_b))
    gm_w = gm_ws.transpose(0, 2, 1, 3).reshape(DEPTH, GM_CHUNK, GM_HEADS * GM_CHUNK).astype(BF16)
    gm_b = jnp.repeat(gm_bs.transpose(0, 2, 1), GROUP_W // GM_HEADS, axis=2)
    ff_perm = jnp.asarray(_ffn_perm_consts(), BF16)

    new_re, new_im, new_hg = [], [], []
    for l in range(DEPTH):
        hz, hqv, op_in, xc, xa_lo, xa_hi = _inproj(xs, mods, norm1_r, w_in, l)

        ylo, yhi, hfin = _s5(xa_lo, xa_hi, s5_m, s5_f, s5_e, s5_al, s5_h0, l)
        hfin = hfin.reshape(2, 2, BATCH, S5_G, S5_N).transpose(1, 2, 0, 3, 4)
        new_re.append(hfin[0])
        new_im.append(hfin[1])

        of, ob, sfin, wup_bf, wdn_bf = _hgrn(hz, hqv, lbps, hg_consts, hg_s0, ffn_w_up, ffn_w_down, l)
        new_hg.append(sfin)

        yc = (_fnet(xc, cs, dft_ctx, fn_w, l, n_seq=BATCH, seq_len=SEQ, tok0=0, spg=FN_CTX_SPG, tr=SEQ),
              _fnet(xc, cs, dft_lat, fn_w, l, n_seq=DEC_BATCH, seq_len=DEC_SEQ, tok0=N_CTX, spg=1,
                    tr=FN_LAT_TR))

        x1, h2 = _outproj(xs, op_in, ylo, yhi, of, ob, yc, mods, s5_d_r, s5_w_glu, gn_r, bdm, gmn_r, gm_w, gm_b,
                          gm_hm, w_out, norm2_r, l)
        res = _ffn(x1, h2, mods, ff_perm, wup_bf, ffn_conv_w, cb_r, wdn_bf, final_norm_g, l,
                   final=(l == DEPTH - 1))
        xs = (res,)

    y_prompt = res[0].reshape(BATCH, SEQ, D_MODEL)
    y_sample = res[1].reshape(DEC_BATCH, DEC_SEQ, D_MODEL)
    return (y_prompt, y_sample, jnp.stack(new_re, axis=1), jnp.stack(new_im, axis=1),
            jnp.stack(new_hg, axis=1))
```
